```python
import jax, jax.numpy as jnp
from jax import lax
import numpy as np

D_MODEL = 2048
BATCH = 8
SEQ = 2048
DEPTH = 4

HEAD_DIM = 64
N_FOX_HEADS = D_MODEL // 2 // HEAD_DIM
N_SWA_HEADS = D_MODEL // 2 // HEAD_DIM
N_SWA_KV_HEADS = max(1, N_SWA_HEADS // 8)
FOX_WIDTH = N_FOX_HEADS * HEAD_DIM
SWA_WIDTH = N_SWA_HEADS * HEAD_DIM
SWA_KV_WIDTH = N_SWA_KV_HEADS * HEAD_DIM
MIX_WIDTH = FOX_WIDTH + SWA_WIDTH
IN_SPLIT_SIZES = (FOX_WIDTH, FOX_WIDTH, FOX_WIDTH, N_FOX_HEADS, SWA_WIDTH, SWA_KV_WIDTH, SWA_KV_WIDTH)
IN_PROJ_WIDTH = sum(IN_SPLIT_SIZES)
D_FF = ((8 * D_MODEL // 3 + 255) // 256) * 256
N_META = 16
BLOCK = 128
WINDOW = 128
PAD = BLOCK - N_META
EPS = 1e-6
NEG_INF = -1e30

kernel_name = "hymba_fox_swa_sink_alibi_macaron"


def rms_norm(x, g):
    xf = x.astype(jnp.float32)
    y = xf * lax.rsqrt(jnp.mean(xf * xf, axis=-1, keepdims=True) + EPS)
    return (y * g.astype(jnp.float32)).astype(x.dtype)


def swiglu(x, w_gate, w_up, w_down):
    return (jax.nn.silu(x @ w_gate) * (x @ w_up)) @ w_down


def alibi_slopes(n_heads):
    return jnp.asarray(2.0 ** (-8.0 * np.arange(1, n_heads + 1) / n_heads), dtype=jnp.float32)


def forgetting_attention(q, k, v, log_f):
    L = q.shape[1]
    scale = HEAD_DIM ** -0.5
    c = jnp.cumsum(log_f, axis=1).transpose(0, 2, 1)
    pos = jnp.arange(L)
    outs = []
    for i in range(L // BLOCK):
        q0, q1 = i * BLOCK, (i + 1) * BLOCK
        s = jnp.einsum('bqhd,bkhd->bhqk', q[:, q0:q1], k[:, :q1]).astype(jnp.float32) * scale
        s = s + c[:, :, q0:q1, None] - c[:, :, None, :q1]
        qp = pos[q0:q1][:, None]
        kp = pos[:q1][None, :]
        allowed = (kp <= qp) & (kp >= PAD)
        p = jax.nn.softmax(jnp.where(allowed, s, NEG_INF), axis=-1)
        outs.append(jnp.einsum('bhqk,bkhd->bqhd', p.astype(v.dtype), v[:, :q1]))
    return jnp.concatenate(outs, axis=1)


def sliding_window_sink_attention(q, k, v, sinks):
    B, L, Hq, Dh = q.shape
    Hkv = k.shape[2]
    G = Hq // Hkv
    NB = L // BLOCK
    scale = HEAD_DIM ** -0.5
    qb = q.reshape(B, NB, BLOCK, Hkv, G, Dh)
    kb = k.reshape(B, NB, BLOCK, Hkv, Dh)
    vb = v.reshape(B, NB, BLOCK, Hkv, Dh)
    shift = ((0, 0), (1, 0), (0, 0), (0, 0), (0, 0))
    k_band = jnp.concatenate([jnp.pad(kb, shift)[:, :-1], kb], axis=2)
    v_band = jnp.concatenate([jnp.pad(vb, shift)[:, :-1], vb], axis=2)
    s = jnp.einsum('bnqhgd,bnkhd->bnhgqk', qb, k_band).astype(jnp.float32) * scale
    blk = jnp.arange(NB)[:, None] * BLOCK
    qpos = blk + jnp.arange(BLOCK)[None, :]
    kpos = blk - BLOCK + jnp.arange(2 * BLOCK)[None, :]
    dist = qpos[:, :, None] - kpos[:, None, :]
    allowed = (dist >= 0) & (dist < WINDOW) & (kpos[:, None, :] >= PAD)
    slopes = alibi_slopes(Hq).reshape(Hkv, G)[None, None, :, :, None, None]
    s = s - slopes * dist.astype(jnp.float32)[None, :, None, None]
    s = jnp.where(allowed[None, :, None, None], s, NEG_INF)
    sink = sinks.astype(jnp.float32).reshape(Hkv, G)[None, None, :, :, None, None]
    m = jnp.maximum(jnp.max(s, axis=-1, keepdims=True), sink)
    p = jnp.exp(s - m)
    p = p / (jnp.sum(p, axis=-1, keepdims=True) + jnp.exp(sink - m))
    o = jnp.einsum('bnhgqk,bnkhd->bnqhgd', p.astype(v.dtype), v_band)
    return o.reshape(B, L, Hq, Dh)


def _fwd_setup_inputs(seed: int = 0) -> dict:
    key = jax.random.key(seed)
    ks = jax.random.split(key, 24)
    f32 = jnp.float32

    def nrm(k, shape, scale):
        return jax.random.normal(k, shape, f32) * scale

    def gain(k, shape):
        return 1.0 + 0.02 * jax.random.normal(k, shape, f32)

    return {
        "x": nrm(ks[0], (BATCH, SEQ, D_MODEL), 1.0),
        "meta_tokens": nrm(ks[1], (N_META, D_MODEL), 1.0),
        "ffn1_norm": gain(ks[2], (DEPTH, D_MODEL)),
        "ffn1_w_gate": nrm(ks[3], (DEPTH, D_MODEL, D_FF), D_MODEL ** -0.5),
        "ffn1_w_up": nrm(ks[4], (DEPTH, D_MODEL, D_FF), D_MODEL ** -0.5),
        "ffn1_w_down": nrm(ks[5], (DEPTH, D_FF, D_MODEL), D_FF ** -0.5),
        "mix_norm": gain(ks[6], (DEPTH, D_MODEL)),
        "w_in": nrm(ks[7], (DEPTH, D_MODEL, IN_PROJ_WIDTH), D_MODEL ** -0.5),
        "b_forget": 2.0 + 0.3 * jax.random.normal(ks[8], (DEPTH, N_FOX_HEADS), f32),
        "fox_q_norm": gain(ks[9], (DEPTH, HEAD_DIM)),
        "fox_k_norm": gain(ks[10], (DEPTH, HEAD_DIM)),
        "swa_q_norm": gain(ks[11], (DEPTH, HEAD_DIM)),
        "swa_k_norm": gain(ks[12], (DEPTH, HEAD_DIM)),
        "swa_sinks": nrm(ks[13], (DEPTH, N_SWA_HEADS), 1.0),
        "fox_out_norm": gain(ks[14], (DEPTH, FOX_WIDTH)),
        "swa_out_norm": gain(ks[15], (DEPTH, SWA_WIDTH)),
        "w_out": nrm(ks[16], (DEPTH, MIX_WIDTH, D_MODEL), MIX_WIDTH ** -0.5),
        "ffn2_norm": gain(ks[17], (DEPTH, D_MODEL)),
        "ffn2_w_gate": nrm(ks[18], (DEPTH, D_MODEL, D_FF), D_MODEL ** -0.5),
        "ffn2_w_up": nrm(ks[19], (DEPTH, D_MODEL, D_FF), D_MODEL ** -0.5),
        "ffn2_w_down": nrm(ks[20], (DEPTH, D_FF, D_MODEL), D_FF ** -0.5),
    }


def _fwd_reference(x, meta_tokens, ffn1_norm, ffn1_w_gate, ffn1_w_up, ffn1_w_down, mix_norm, w_in,
              b_forget, fox_q_norm, fox_k_norm, swa_q_norm, swa_k_norm, swa_sinks,
              fox_out_norm, swa_out_norm, w_out, ffn2_norm, ffn2_w_gate, ffn2_w_up, ffn2_w_down):
    B, S, D = x.shape
    h = jnp.concatenate([
        jnp.zeros((B, PAD, D), x.dtype),
        jnp.broadcast_to(meta_tokens.astype(x.dtype)[None], (B, N_META, D)),
        x,
    ], axis=1)
    L = h.shape[1]
    split_idx = [int(v) for v in np.cumsum(IN_SPLIT_SIZES)[:-1]]

    for l in range(DEPTH):
        h = h + 0.5 * swiglu(rms_norm(h, ffn1_norm[l]), ffn1_w_gate[l], ffn1_w_up[l], ffn1_w_down[l])

        u = rms_norm(h, mix_norm[l]) @ w_in[l]
        fq, fk, fv, fz, sq, sk, sv = jnp.split(u, split_idx, axis=-1)
        fq = rms_norm(fq.reshape(B, L, N_FOX_HEADS, HEAD_DIM), fox_q_norm[l])
        fk = rms_norm(fk.reshape(B, L, N_FOX_HEADS, HEAD_DIM), fox_k_norm[l])
        fv = fv.reshape(B, L, N_FOX_HEADS, HEAD_DIM)
        log_f = jax.nn.log_sigmoid(fz.astype(jnp.float32) + b_forget[l].astype(jnp.float32))
        sq = rms_norm(sq.reshape(B, L, N_SWA_HEADS, HEAD_DIM), swa_q_norm[l])
        sk = rms_norm(sk.reshape(B, L, N_SWA_KV_HEADS, HEAD_DIM), swa_k_norm[l])
        sv = sv.reshape(B, L, N_SWA_KV_HEADS, HEAD_DIM)

        o_fox = forgetting_attention(fq, fk, fv, log_f).reshape(B, L, FOX_WIDTH)
        o_swa = sliding_window_sink_attention(sq, sk, sv, swa_sinks[l]).reshape(B, L, SWA_WIDTH)
        o = jnp.concatenate([rms_norm(o_fox, fox_out_norm[l]), rms_norm(o_swa, swa_out_norm[l])], axis=-1)
        h = h + o @ w_out[l]

        h = h + 0.5 * swiglu(rms_norm(h, ffn2_norm[l]), ffn2_w_gate[l], ffn2_w_up[l], ffn2_w_down[l])

    return h[:, BLOCK:]


import jax as _jax
import jax.numpy as _jnp

TWIN_FORMAT = 'train_step'
FWD_PARAMS = ['x', 'meta_tokens', 'ffn1_norm', 'ffn1_w_gate', 'ffn1_w_up', 'ffn1_w_down', 'mix_norm', 'w_in', 'b_forget', 'fox_q_norm', 'fox_k_norm', 'swa_q_norm', 'swa_k_norm', 'swa_sinks', 'fox_out_norm', 'swa_out_norm', 'w_out', 'ffn2_norm', 'ffn2_w_gate', 'ffn2_w_up', 'ffn2_w_down']
TWIN_WEIGHTS = ['meta_tokens', 'ffn1_norm', 'ffn1_w_gate', 'ffn1_w_up', 'ffn1_w_down', 'mix_norm', 'w_in', 'b_forget', 'fox_q_norm', 'fox_k_norm', 'swa_q_norm', 'swa_k_norm', 'swa_sinks', 'fox_out_norm', 'swa_out_norm', 'w_out', 'ffn2_norm', 'ffn2_w_gate', 'ffn2_w_up', 'ffn2_w_down']
TWIN_DIFF_INPUT = 'x'
TWIN_INPUTS = ['x', 'meta_tokens', 'ffn1_norm', 'ffn1_w_gate', 'ffn1_w_up', 'ffn1_w_down', 'mix_norm', 'w_in', 'b_forget', 'fox_q_norm', 'fox_k_norm', 'swa_q_norm', 'swa_k_norm', 'swa_sinks', 'fox_out_norm', 'swa_out_norm', 'w_out', 'ffn2_norm', 'ffn2_w_gate', 'ffn2_w_up', 'ffn2_w_down', 'loss_target', 'm_meta_tokens', 'm_ffn1_norm', 'm_ffn1_w_gate', 'm_ffn1_w_up', 'm_ffn1_w_down', 'm_mix_norm', 'm_w_in', 'm_b_forget', 'm_fox_q_norm', 'm_fox_k_norm', 'm_swa_q_norm', 'm_swa_k_norm', 'm_swa_sinks', 'm_fox_out_norm', 'm_swa_out_norm', 'm_w_out', 'm_ffn2_norm', 'm_ffn2_w_gate', 'm_ffn2_w_up', 'm_ffn2_w_down', 'v_meta_tokens', 'v_ffn1_norm', 'v_ffn1_w_gate', 'v_ffn1_w_up', 'v_ffn1_w_down', 'v_mix_norm', 'v_w_in', 'v_b_forget', 'v_fox_q_norm', 'v_fox_k_norm', 'v_swa_q_norm', 'v_swa_k_norm', 'v_swa_sinks', 'v_fox_out_norm', 'v_swa_out_norm', 'v_w_out', 'v_ffn2_norm', 'v_ffn2_w_gate', 'v_ffn2_w_up', 'v_ffn2_w_down']
TWIN_OUTPUTS = ['loss', 'grad_x', 'grad_meta_tokens', 'grad_ffn1_norm', 'grad_ffn1_w_gate', 'grad_ffn1_w_up', 'grad_ffn1_w_down', 'grad_mix_norm', 'grad_w_in', 'grad_b_forget', 'grad_fox_q_norm', 'grad_fox_k_norm', 'grad_swa_q_norm', 'grad_swa_k_norm', 'grad_swa_sinks', 'grad_fox_out_norm', 'grad_swa_out_norm', 'grad_w_out', 'grad_ffn2_norm', 'grad_ffn2_w_gate', 'grad_ffn2_w_up', 'grad_ffn2_w_down', 'delta_meta_tokens', 'delta_ffn1_norm', 'delta_ffn1_w_gate', 'delta_ffn1_w_up', 'delta_ffn1_w_down', 'delta_mix_norm', 'delta_w_in', 'delta_b_forget', 'delta_fox_q_norm', 'delta_fox_k_norm', 'delta_swa_q_norm', 'delta_swa_k_norm', 'delta_swa_sinks', 'delta_fox_out_norm', 'delta_swa_out_norm', 'delta_w_out', 'delta_ffn2_norm', 'delta_ffn2_w_gate', 'delta_ffn2_w_up', 'delta_ffn2_w_down', 'new_m_meta_tokens', 'new_m_ffn1_norm', 'new_m_ffn1_w_gate', 'new_m_ffn1_w_up', 'new_m_ffn1_w_down', 'new_m_mix_norm', 'new_m_w_in', 'new_m_b_forget', 'new_m_fox_q_norm', 'new_m_fox_k_norm', 'new_m_swa_q_norm', 'new_m_swa_k_norm', 'new_m_swa_sinks', 'new_m_fox_out_norm', 'new_m_swa_out_norm', 'new_m_w_out', 'new_m_ffn2_norm', 'new_m_ffn2_w_gate', 'new_m_ffn2_w_up', 'new_m_ffn2_w_down', 'new_v_meta_tokens', 'new_v_ffn1_norm', 'new_v_ffn1_w_gate', 'new_v_ffn1_w_up', 'new_v_ffn1_w_down', 'new_v_mix_norm', 'new_v_w_in', 'new_v_b_forget', 'new_v_fox_q_norm', 'new_v_fox_k_norm', 'new_v_swa_q_norm', 'new_v_swa_k_norm', 'new_v_swa_sinks', 'new_v_fox_out_norm', 'new_v_swa_out_norm', 'new_v_w_out', 'new_v_ffn2_norm', 'new_v_ffn2_w_gate', 'new_v_ffn2_w_up', 'new_v_ffn2_w_down']
TWIN_LEAF_KINDS = {'loss': 'loss', 'grad_x': 'grad_x', 'grad_meta_tokens': 'grad_w', 'grad_ffn1_norm': 'grad_w', 'grad_ffn1_w_gate': 'grad_w', 'grad_ffn1_w_up': 'grad_w', 'grad_ffn1_w_down': 'grad_w', 'grad_mix_norm': 'grad_w', 'grad_w_in': 'grad_w', 'grad_b_forget': 'grad_w', 'grad_fox_q_norm': 'grad_w', 'grad_fox_k_norm': 'grad_w', 'grad_swa_q_norm': 'grad_w', 'grad_swa_k_norm': 'grad_w', 'grad_swa_sinks': 'grad_w', 'grad_fox_out_norm': 'grad_w', 'grad_swa_out_norm': 'grad_w', 'grad_w_out': 'grad_w', 'grad_ffn2_norm': 'grad_w', 'grad_ffn2_w_gate': 'grad_w', 'grad_ffn2_w_up': 'grad_w', 'grad_ffn2_w_down': 'grad_w', 'delta_meta_tokens': 'delta_w', 'delta_ffn1_norm': 'delta_w', 'delta_ffn1_w_gate': 'delta_w', 'delta_ffn1_w_up': 'delta_w', 'delta_ffn1_w_down': 'delta_w', 'delta_mix_norm': 'delta_w', 'delta_w_in': 'delta_w', 'delta_b_forget': 'delta_w', 'delta_fox_q_norm': 'delta_w', 'delta_fox_k_norm': 'delta_w', 'delta_swa_q_norm': 'delta_w', 'delta_swa_k_norm': 'delta_w', 'delta_swa_sinks': 'delta_w', 'delta_fox_out_norm': 'delta_w', 'delta_swa_out_norm': 'delta_w', 'delta_w_out': 'delta_w', 'delta_ffn2_norm': 'delta_w', 'delta_ffn2_w_gate': 'delta_w', 'delta_ffn2_w_up': 'delta_w', 'delta_ffn2_w_down': 'delta_w', 'new_m_meta_tokens': 'new_m', 'new_m_ffn1_norm': 'new_m', 'new_m_ffn1_w_gate': 'new_m', 'new_m_ffn1_w_up': 'new_m', 'new_m_ffn1_w_down': 'new_m', 'new_m_mix_norm': 'new_m', 'new_m_w_in': 'new_m', 'new_m_b_forget': 'new_m', 'new_m_fox_q_norm': 'new_m', 'new_m_fox_k_norm': 'new_m', 'new_m_swa_q_norm': 'new_m', 'new_m_swa_k_norm': 'new_m', 'new_m_swa_sinks': 'new_m', 'new_m_fox_out_norm': 'new_m', 'new_m_swa_out_norm': 'new_m', 'new_m_w_out': 'new_m', 'new_m_ffn2_norm': 'new_m', 'new_m_ffn2_w_gate': 'new_m', 'new_m_ffn2_w_up': 'new_m', 'new_m_ffn2_w_down': 'new_m', 'new_v_meta_tokens': 'new_v', 'new_v_ffn1_norm': 'new_v', 'new_v_ffn1_w_gate': 'new_v', 'new_v_ffn1_w_up': 'new_v', 'new_v_ffn1_w_down': 'new_v', 'new_v_mix_norm': 'new_v', 'new_v_w_in': 'new_v', 'new_v_b_forget': 'new_v', 'new_v_fox_q_norm': 'new_v', 'new_v_fox_k_norm': 'new_v', 'new_v_swa_q_norm': 'new_v', 'new_v_swa_k_norm': 'new_v', 'new_v_swa_sinks': 'new_v', 'new_v_fox_out_norm': 'new_v', 'new_v_swa_out_norm': 'new_v', 'new_v_w_out': 'new_v', 'new_v_ffn2_norm': 'new_v', 'new_v_ffn2_w_gate': 'new_v', 'new_v_ffn2_w_up': 'new_v', 'new_v_ffn2_w_down': 'new_v'}


def _forward(args):
    return _fwd_reference(*[args[k] for k in FWD_PARAMS])


def _output_shape():
    out = _jax.eval_shape(lambda: _forward(_fwd_setup_inputs(0)))
    return out.shape, out.dtype

N_MICROBATCH = 1
ADAM_LR = 0.001
ADAM_B1 = 0.9
ADAM_B2 = 0.999
ADAM_EPS = 1e-08
ADAM_WD = 0.01
ADAM_STEP = 10
PER_EXAMPLE_BATCH_AXIS = {'x': 0, 'loss_target': 0}
SHARED_INPUTS = []
_WEIGHT_DTYPES = {'meta_tokens': _jnp.float32, 'ffn1_norm': _jnp.float32, 'ffn1_w_gate': _jnp.float32, 'ffn1_w_up': _jnp.float32, 'ffn1_w_down': _jnp.float32, 'mix_norm': _jnp.float32, 'w_in': _jnp.float32, 'b_forget': _jnp.float32, 'fox_q_norm': _jnp.float32, 'fox_k_norm': _jnp.float32, 'swa_q_norm': _jnp.float32, 'swa_k_norm': _jnp.float32, 'swa_sinks': _jnp.float32, 'fox_out_norm': _jnp.float32, 'swa_out_norm': _jnp.float32, 'w_out': _jnp.float32, 'ffn2_norm': _jnp.float32, 'ffn2_w_gate': _jnp.float32, 'ffn2_w_up': _jnp.float32, 'ffn2_w_down': _jnp.float32}
MOMENT_SCALE = {'meta_tokens': 1.439741e-01, 'ffn1_norm': 1.386769e+00, 'ffn1_w_gate': 1.216983e-01, 'ffn1_w_up': 1.206853e-01, 'ffn1_w_down': 1.996366e-01, 'mix_norm': 8.241384e-01, 'w_in': 5.682822e-01, 'b_forget': 2.131487e+00, 'fox_q_norm': 8.985922e-01, 'fox_k_norm': 8.969168e-01, 'swa_q_norm': 9.256431e-01, 'swa_k_norm': 9.207436e-01, 'swa_sinks': 1.058921e+00, 'fox_out_norm': 7.844163e+00, 'swa_out_norm': 9.448701e+00, 'w_out': 7.981200e-01, 'ffn2_norm': 1.472746e+00, 'ffn2_w_gate': 6.440365e-02, 'ffn2_w_up': 6.904773e-02, 'ffn2_w_down': 1.138927e-01}


def _to_microbatches(a, axis):
    t = _jnp.moveaxis(a, axis, 0)
    t = t.reshape((N_MICROBATCH, t.shape[0] // N_MICROBATCH) + t.shape[1:])
    return _jnp.moveaxis(t, 1, axis + 1)


def setup_inputs(seed: int = 0) -> dict:
    inp = _fwd_setup_inputs(seed)
    key = _jax.random.fold_in(_jax.random.key(seed), 7919)
    shape, _ = _output_shape()
    out = dict(inp)
    out["loss_target"] = _jax.random.normal(_jax.random.fold_in(key, 0), shape, _jnp.float32)
    for i, name in enumerate(TWIN_WEIGHTS):
        w = inp[name].astype(_jnp.float32)
        if MOMENT_SCALE is None:
            s = _jnp.sqrt(_jnp.mean(_jnp.square(w)) + 1e-30)
        else:
            s = MOMENT_SCALE[name]
        km, kv = _jax.random.split(_jax.random.fold_in(key, i + 1))
        out[name] = w
        out["m_" + name] = s * _jax.random.normal(km, w.shape, _jnp.float32)
        out["v_" + name] = (s * s) * _jax.random.uniform(kv, w.shape, _jnp.float32, 0.5, 1.5)
    if N_MICROBATCH > 1:
        for name, axis in PER_EXAMPLE_BATCH_AXIS.items():
            out[name] = _to_microbatches(out[name], axis)
    return {'x': out['x'], 'meta_tokens': out['meta_tokens'], 'ffn1_norm': out['ffn1_norm'], 'ffn1_w_gate': out['ffn1_w_gate'], 'ffn1_w_up': out['ffn1_w_up'], 'ffn1_w_down': out['ffn1_w_down'], 'mix_norm': out['mix_norm'], 'w_in': out['w_in'], 'b_forget': out['b_forget'], 'fox_q_norm': out['fox_q_norm'], 'fox_k_norm': out['fox_k_norm'], 'swa_q_norm': out['swa_q_norm'], 'swa_k_norm': out['swa_k_norm'], 'swa_sinks': out['swa_sinks'], 'fox_out_norm': out['fox_out_norm'], 'swa_out_norm': out['swa_out_norm'], 'w_out': out['w_out'], 'ffn2_norm': out['ffn2_norm'], 'ffn2_w_gate': out['ffn2_w_gate'], 'ffn2_w_up': out['ffn2_w_up'], 'ffn2_w_down': out['ffn2_w_down'], 'loss_target': out['loss_target'], 'm_meta_tokens': out['m_meta_tokens'], 'm_ffn1_norm': out['m_ffn1_norm'], 'm_ffn1_w_gate': out['m_ffn1_w_gate'], 'm_ffn1_w_up': out['m_ffn1_w_up'], 'm_ffn1_w_down': out['m_ffn1_w_down'], 'm_mix_norm': out['m_mix_norm'], 'm_w_in': out['m_w_in'], 'm_b_forget': out['m_b_forget'], 'm_fox_q_norm': out['m_fox_q_norm'], 'm_fox_k_norm': out['m_fox_k_norm'], 'm_swa_q_norm': out['m_swa_q_norm'], 'm_swa_k_norm': out['m_swa_k_norm'], 'm_swa_sinks': out['m_swa_sinks'], 'm_fox_out_norm': out['m_fox_out_norm'], 'm_swa_out_norm': out['m_swa_out_norm'], 'm_w_out': out['m_w_out'], 'm_ffn2_norm': out['m_ffn2_norm'], 'm_ffn2_w_gate': out['m_ffn2_w_gate'], 'm_ffn2_w_up': out['m_ffn2_w_up'], 'm_ffn2_w_down': out['m_ffn2_w_down'], 'v_meta_tokens': out['v_meta_tokens'], 'v_ffn1_norm': out['v_ffn1_norm'], 'v_ffn1_w_gate': out['v_ffn1_w_gate'], 'v_ffn1_w_up': out['v_ffn1_w_up'], 'v_ffn1_w_down': out['v_ffn1_w_down'], 'v_mix_norm': out['v_mix_norm'], 'v_w_in': out['v_w_in'], 'v_b_forget': out['v_b_forget'], 'v_fox_q_norm': out['v_fox_q_norm'], 'v_fox_k_norm': out['v_fox_k_norm'], 'v_swa_q_norm': out['v_swa_q_norm'], 'v_swa_k_norm': out['v_swa_k_norm'], 'v_swa_sinks': out['v_swa_sinks'], 'v_fox_out_norm': out['v_fox_out_norm'], 'v_swa_out_norm': out['v_swa_out_norm'], 'v_w_out': out['v_w_out'], 'v_ffn2_norm': out['v_ffn2_norm'], 'v_ffn2_w_gate': out['v_ffn2_w_gate'], 'v_ffn2_w_up': out['v_ffn2_w_up'], 'v_ffn2_w_down': out['v_ffn2_w_down']}


def _loss(weights, diff, rest, loss_target):
    with _jax.named_scope("forward"):
        args = {**rest, TWIN_DIFF_INPUT: diff, **{k: w.astype(_WEIGHT_DTYPES[k]) for k, w in weights.items()}}
        y = _forward(args)
    with _jax.named_scope("loss_head"):
        err = _jnp.square(y.astype(_jnp.float32) - loss_target)
        return 0.5 * _jnp.sum(_jnp.mean(err, axis=-1)) if err.ndim else 0.5 * err


def _adamw(w, g, m, v):
    m = ADAM_B1 * m + (1.0 - ADAM_B1) * g
    v = ADAM_B2 * v + (1.0 - ADAM_B2) * _jnp.square(g)
    m_hat = m / (1.0 - ADAM_B1 ** ADAM_STEP)
    v_hat = v / (1.0 - ADAM_B2 ** ADAM_STEP)
    delta = -ADAM_LR * (m_hat / (_jnp.sqrt(v_hat) + ADAM_EPS) + ADAM_WD * w)
    return delta, m, v


def reference(x, meta_tokens, ffn1_norm, ffn1_w_gate, ffn1_w_up, ffn1_w_down, mix_norm, w_in, b_forget, fox_q_norm, fox_k_norm, swa_q_norm, swa_k_norm, swa_sinks, fox_out_norm, swa_out_norm, w_out, ffn2_norm, ffn2_w_gate, ffn2_w_up, ffn2_w_down, loss_target, m_meta_tokens, m_ffn1_norm, m_ffn1_w_gate, m_ffn1_w_up, m_ffn1_w_down, m_mix_norm, m_w_in, m_b_forget, m_fox_q_norm, m_fox_k_norm, m_swa_q_norm, m_swa_k_norm, m_swa_sinks, m_fox_out_norm, m_swa_out_norm, m_w_out, m_ffn2_norm, m_ffn2_w_gate, m_ffn2_w_up, m_ffn2_w_down, v_meta_tokens, v_ffn1_norm, v_ffn1_w_gate, v_ffn1_w_up, v_ffn1_w_down, v_mix_norm, v_w_in, v_b_forget, v_fox_q_norm, v_fox_k_norm, v_swa_q_norm, v_swa_k_norm, v_swa_sinks, v_fox_out_norm, v_swa_out_norm, v_w_out, v_ffn2_norm, v_ffn2_w_gate, v_ffn2_w_up, v_ffn2_w_down):
    given = dict(x=x, meta_tokens=meta_tokens, ffn1_norm=ffn1_norm, ffn1_w_gate=ffn1_w_gate, ffn1_w_up=ffn1_w_up, ffn1_w_down=ffn1_w_down, mix_norm=mix_norm, w_in=w_in, b_forget=b_forget, fox_q_norm=fox_q_norm, fox_k_norm=fox_k_norm, swa_q_norm=swa_q_norm, swa_k_norm=swa_k_norm, swa_sinks=swa_sinks, fox_out_norm=fox_out_norm, swa_out_norm=swa_out_norm, w_out=w_out, ffn2_norm=ffn2_norm, ffn2_w_gate=ffn2_w_gate, ffn2_w_up=ffn2_w_up, ffn2_w_down=ffn2_w_down, loss_target=loss_target, m_meta_tokens=m_meta_tokens, m_ffn1_norm=m_ffn1_norm, m_ffn1_w_gate=m_ffn1_w_gate, m_ffn1_w_up=m_ffn1_w_up, m_ffn1_w_down=m_ffn1_w_down, m_mix_norm=m_mix_norm, m_w_in=m_w_in, m_b_forget=m_b_forget, m_fox_q_norm=m_fox_q_norm, m_fox_k_norm=m_fox_k_norm, m_swa_q_norm=m_swa_q_norm, m_swa_k_norm=m_swa_k_norm, m_swa_sinks=m_swa_sinks, m_fox_out_norm=m_fox_out_norm, m_swa_out_norm=m_swa_out_norm, m_w_out=m_w_out, m_ffn2_norm=m_ffn2_norm, m_ffn2_w_gate=m_ffn2_w_gate, m_ffn2_w_up=m_ffn2_w_up, m_ffn2_w_down=m_ffn2_w_down, v_meta_tokens=v_meta_tokens, v_ffn1_norm=v_ffn1_norm, v_ffn1_w_gate=v_ffn1_w_gate, v_ffn1_w_up=v_ffn1_w_up, v_ffn1_w_down=v_ffn1_w_down, v_mix_norm=v_mix_norm, v_w_in=v_w_in, v_b_forget=v_b_forget, v_fox_q_norm=v_fox_q_norm, v_fox_k_norm=v_fox_k_norm, v_swa_q_norm=v_swa_q_norm, v_swa_k_norm=v_swa_k_norm, v_swa_sinks=v_swa_sinks, v_fox_out_norm=v_fox_out_norm, v_swa_out_norm=v_swa_out_norm, v_w_out=v_w_out, v_ffn2_norm=v_ffn2_norm, v_ffn2_w_gate=v_ffn2_w_gate, v_ffn2_w_up=v_ffn2_w_up, v_ffn2_w_down=v_ffn2_w_down)
    weights = {n: given[n] for n in TWIN_WEIGHTS}
    shared = {n: given[n] for n in SHARED_INPUTS}
    per_example = {n: given[n] for n in ['x']}
    grad_fn = _jax.value_and_grad(_loss, argnums=(0, 1))

    def one_microbatch(ex, loss_target):
        ex = dict(ex)
        diff = ex.pop(TWIN_DIFF_INPUT)
        return grad_fn(weights, diff, {**shared, **ex}, loss_target)

    if N_MICROBATCH == 1:
        loss, (grad_w, grad_x) = one_microbatch(per_example, given["loss_target"])
    else:
        def body(carry, xs):
            loss_sum, grad_sum = carry
            l_k, (gw_k, gx_k) = one_microbatch(xs[0], xs[1])
            with _jax.named_scope("update"):
                return (loss_sum + l_k, _jax.tree.map(_jnp.add, grad_sum, gw_k)), gx_k

        init = (_jnp.zeros((), _jnp.float32), _jax.tree.map(_jnp.zeros_like, weights))
        (loss, grad_w), grad_x = _jax.lax.scan(body, init, (per_example, given["loss_target"]))
    with _jax.named_scope("update"):
        delta_w, new_m, new_v = {}, {}, {}
        for n in TWIN_WEIGHTS:
            delta_w[n], new_m[n], new_v[n] = _adamw(weights[n], grad_w[n], given["m_" + n], given["v_" + n])
    return (loss, grad_x, *[grad_w[n] for n in TWIN_WEIGHTS], *[delta_w[n] for n in TWIN_WEIGHTS],
            *[new_m[n] for n in TWIN_WEIGHTS], *[new_v[n] for n in TWIN_WEIGHTS])
```

```python
import numpy as np
import jax
import jax.numpy as jnp
from jax import lax
from jax.experimental import pallas as pl
from jax.experimental.pallas import tpu as pltpu

F32 = jnp.float32
BF16 = jnp.bfloat16

HEAD_DIM = 64
N_META = 16
BLOCK = 128
WINDOW = 128
PAD = BLOCK - N_META
EPS = 1e-6
NEG_INF = -1e30
SWA_GROUP = 8
NCH = 4
LANES = 128
QBLOCK = 512

ADAM_LR = 0.001
ADAM_B1 = 0.9
ADAM_B2 = 0.999
ADAM_EPS = 1e-08
ADAM_WD = 0.01
ADAM_STEP = 10

V7X_VMEM_BYTES = 64 * 1024 * 1024
VMEM_LIMIT = V7X_VMEM_BYTES - 8 * 1024 * 1024
MESH = pl.DeviceIdType.MESH
HIGHEST = lax.Precision.HIGHEST

_TM = (1088, 1024, 704, 512, 384, 256, 128)
_TN = (1408, 1024, 768, 512, 384, 256, 128)
_TK = (2176, 1408, 1024, 512, 384, 256, 128)
_TR = (544, 512, 384, 272, 256, 128)


def _pick(n, cands):
    for c in cands:
        if n % c == 0:
            return c
    return n


def _cparams(*sem):
    return pltpu.CompilerParams(dimension_semantics=sem if sem else None, vmem_limit_bytes=VMEM_LIMIT)


def _matmul(a, b, *, name, nt=False, b_chunked=False, out_chunked=False, out_dtype=F32,
            residual=None, scale=1.0):
    M, K = a.shape
    if not nt:
        N = b.shape[-1] * (NCH if b_chunked else 1)
        assert b.shape[-2] == K
        k_unit = K
    else:
        N = b.shape[-2]
        k_unit = b.shape[-1]
        assert k_unit * (NCH if b_chunked else 1) == K
    n_unit = N // NCH if (out_chunked or (b_chunked and not nt)) else N
    tm, tn, tk = _pick(M, _TM), _pick(n_unit, _TN), _pick(k_unit, _TK)
    osz = jnp.dtype(out_dtype).itemsize

    def est(tm_):
        return (2 * tm_ * tk * 2 + 2 * tk * tn * 2 + tm_ * tn * 4 + 2 * tm_ * tn * osz
                + (2 * tm_ * tn * 4 if residual is not None else 0))

    while est(tm) > VMEM_LIMIT * 3 // 4 and tm % 32 == 0:
        tm //= 2
    npc, kpc = n_unit // tn, k_unit // tk
    nk = K // tk
    grid = (M // tm, N // tn, nk)

    a_spec = pl.BlockSpec((tm, tk), lambda i, j, k: (i, k))
    if not nt:
        if b_chunked:
            b_spec = pl.BlockSpec((None, tk, tn), lambda i, j, k: (j // npc, k, j % npc))
        else:
            b_spec = pl.BlockSpec((tk, tn), lambda i, j, k: (k, j))
        dims = (((1,), (0,)), ((), ()))
    else:
        if b_chunked:
            b_spec = pl.BlockSpec((None, tn, tk), lambda i, j, k: (k // kpc, j, k % kpc))
        else:
            b_spec = pl.BlockSpec((tn, tk), lambda i, j, k: (j, k))
        dims = (((1,), (1,)), ((), ()))
    if out_chunked:
        o_spec = pl.BlockSpec((None, tm, tn), lambda i, j, k: (j // npc, i, j % npc))
        out_shape = jax.ShapeDtypeStruct((NCH, M, n_unit), out_dtype)
    else:
        o_spec = pl.BlockSpec((tm, tn), lambda i, j, k: (i, j))
        out_shape = jax.ShapeDtypeStruct((M, N), out_dtype)
    in_specs = [a_spec, b_spec]
    args = [a, b]
    if residual is not None:
        assert not out_chunked
        in_specs.append(pl.BlockSpec((tm, tn), lambda i, j, k: (i, j)))
        args.append(residual)

    def body(*refs):
        if residual is not None:
            a_ref, b_ref, r_ref, o_ref, acc_ref = refs
        else:
            a_ref, b_ref, o_ref, acc_ref = refs
        k = pl.program_id(2)

        @pl.when(k == 0)
        def _():
            acc_ref[...] = jnp.zeros_like(acc_ref)

        acc_ref[...] += lax.dot_general(a_ref[...], b_ref[...], dims, preferred_element_type=F32)

        @pl.when(k == nk - 1)
        def _():
            r = acc_ref[...]
            if scale != 1.0:
                r = r * scale
            if residual is not None:
                r = r + r_ref[...]
            o_ref[...] = r.astype(o_ref.dtype)

    return pl.pallas_call(
        body, name=name, grid=grid, in_specs=in_specs, out_specs=o_spec, out_shape=out_shape,
        scratch_shapes=[pltpu.VMEM((tm, tn), F32)],
        compiler_params=_cparams("parallel", "parallel", "arbitrary"),
    )(*args)


def _transpose(x, *, name):
    M, N = x.shape
    tc = _pick(N, (512, 384, 256, 128))

    def body(x_ref, o_ref):
        o_ref[...] = x_ref[...].astype(F32).T.astype(o_ref.dtype)

    return pl.pallas_call(
        body, name=name, grid=(N // tc,),
        in_specs=[pl.BlockSpec((M, tc), lambda j: (0, j))],
        out_specs=pl.BlockSpec((tc, M), lambda j: (j, 0)),
        out_shape=jax.ShapeDtypeStruct((N, M), x.dtype),
        compiler_params=_cparams("parallel"),
    )(x)


def _rms_fwd(h, g, *, name):
    L, D = h.shape
    tr = _pick(L, _TR)

    def body(h_ref, g_ref, o_ref):
        x = h_ref[...]
        r = lax.rsqrt(jnp.mean(x * x, axis=-1, keepdims=True) + EPS)
        o_ref[...] = (x * r * g_ref[...]).astype(o_ref.dtype)

    return pl.pallas_call(
        body, name=name, grid=(L // tr,),
        in_specs=[pl.BlockSpec((tr, D), lambda i: (i, 0)), pl.BlockSpec((1, D), lambda i: (0, 0))],
        out_specs=pl.BlockSpec((tr, D), lambda i: (i, 0)),
        out_shape=jax.ShapeDtypeStruct((L, D), BF16),
        compiler_params=_cparams("parallel"),
    )(h, g)


def _rms_bwd(dy, h, g, dh, *, name):
    L, D = h.shape
    tr = _pick(L, _TR)

    def body(dy_ref, h_ref, g_ref, dh_ref, o_ref, dg_ref):
        i = pl.program_id(0)
        x = h_ref[...]
        dyv = dy_ref[...]
        r = lax.rsqrt(jnp.mean(x * x, axis=-1, keepdims=True) + EPS)
        xh = x * r
        dxh = dyv * g_ref[...]
        dx = r * (dxh - xh * jnp.mean(dxh * xh, axis=-1, keepdims=True))
        o_ref[...] = dh_ref[...] + dx
        part = jnp.sum(dyv * xh, axis=0, keepdims=True)

        @pl.when(i == 0)
        def _():
            dg_ref[...] = part

        @pl.when(i > 0)
        def _():
            dg_ref[...] += part

    row = pl.BlockSpec((tr, D), lambda i: (i, 0))
    vec = pl.BlockSpec((1, D), lambda i: (0, 0))
    return pl.pallas_call(
        body, name=name, grid=(L // tr,),
        in_specs=[row, row, vec, row], out_specs=[row, vec],
        out_shape=[jax.ShapeDtypeStruct((L, D), F32), jax.ShapeDtypeStruct((1, D), F32)],
        compiler_params=_cparams("arbitrary"),
    )(dy, h, g, dh)


def _swiglu_fwd(gate, up, *, name):
    L, F = gate.shape
    tr, tc = _pick(L, _TR), _pick(F, _TN)

    def body(g_ref, u_ref, o_ref):
        g = g_ref[...].astype(F32)
        o_ref[...] = (g * jax.nn.sigmoid(g) * u_ref[...].astype(F32)).astype(o_ref.dtype)

    blk = pl.BlockSpec((tr, tc), lambda i, j: (i, j))
    return pl.pallas_call(
        body, name=name, grid=(L // tr, F // tc), in_specs=[blk, blk], out_specs=blk,
        out_shape=jax.ShapeDtypeStruct((L, F), BF16),
        compiler_params=_cparams("parallel", "parallel"),
    )(gate, up)


def _swiglu_bwd(dact, gate, up, *, name):
    L, F = gate.shape
    tr, tc = _pick(L, _TR), _pick(F, _TN)

    def body(d_ref, g_ref, u_ref, dg_ref, du_ref):
        d = d_ref[...].astype(F32)
        g = g_ref[...].astype(F32)
        u = u_ref[...].astype(F32)
        sg = jax.nn.sigmoid(g)
        du_ref[...] = (d * g * sg).astype(du_ref.dtype)
        dg_ref[...] = (d * u * sg * (1.0 + g * (1.0 - sg))).astype(dg_ref.dtype)

    blk = pl.BlockSpec((tr, tc), lambda i, j: (i, j))
    return pl.pallas_call(
        body, name=name, grid=(L // tr, F // tc), in_specs=[blk, blk, blk], out_specs=[blk, blk],
        out_shape=[jax.ShapeDtypeStruct((L, F), BF16)] * 2,
        compiler_params=_cparams("parallel", "parallel"),
    )(dact, gate, up)


def _cast_bf16(x, *, name):
    L, D = x.shape
    tr = _pick(L, _TR)

    def body(x_ref, o_ref):
        o_ref[...] = x_ref[...].astype(o_ref.dtype)

    blk = pl.BlockSpec((tr, D), lambda i: (i, 0))
    return pl.pallas_call(
        body, name=name, grid=(L // tr,), in_specs=[blk], out_specs=blk,
        out_shape=jax.ShapeDtypeStruct((L, D), BF16), compiler_params=_cparams("parallel"),
    )(x)


def _loss_grad(h, target, *, name):
    L, D = h.shape
    S = target.shape[0]
    nb = L // BLOCK

    def body(h_ref, t_ref, loss_ref, dh_ref):
        i = pl.program_id(0)

        @pl.when(i == 0)
        def _():
            loss_ref[...] = jnp.zeros_like(loss_ref)
            dh_ref[...] = jnp.zeros_like(dh_ref)

        @pl.when(i > 0)
        def _():
            err = h_ref[...] - t_ref[...]
            dh_ref[...] = err * (1.0 / D)
            loss_ref[...] += jnp.full(loss_ref.shape, (0.5 / D) * jnp.sum(err * err), F32)

    return pl.pallas_call(
        body, name=name, grid=(nb,),
        in_specs=[pl.BlockSpec((BLOCK, D), lambda i: (i, 0)),
                  pl.BlockSpec((BLOCK, D), lambda i: (jnp.maximum(i - 1, 0), 0))],
        out_specs=[pl.BlockSpec((1, LANES), lambda i: (0, 0)), pl.BlockSpec((BLOCK, D), lambda i: (i, 0))],
        out_shape=[jax.ShapeDtypeStruct((1, LANES), F32), jax.ShapeDtypeStruct((L, D), F32)],
        compiler_params=_cparams("arbitrary"),
    )(h, target)


def _ffn_fwd(h, g, wg, wu, wd, tag):
    hn = _rms_fwd(h, g, name=f"{tag}_rms")
    gate = _matmul(hn, wg, name=f"{tag}_gate", b_chunked=True, out_dtype=BF16)
    up = _matmul(hn, wu, name=f"{tag}_up", b_chunked=True, out_dtype=BF16)
    act = _swiglu_fwd(gate, up, name=f"{tag}_act")
    h_out = _matmul(act, wd, name=f"{tag}_down", residual=h, scale=0.5)
    return h_out, (h, hn, gate, up, act)


def _ffn_bwd(dh, saved, g, wg, wu, wd, tag):
    h, hn, gate, up, act = saved
    dout = _cast_bf16(dh, name=f"{tag}_dout")
    dact = _matmul(dout, wd, name=f"{tag}_dact", nt=True, out_dtype=BF16, scale=0.5)
    actT = _transpose(act, name=f"{tag}_actT")
    dwd = _matmul(actT, dout, name=f"{tag}_dwd", out_dtype=BF16, scale=0.5)
    dgate, dup = _swiglu_bwd(dact, gate, up, name=f"{tag}_dswiglu")
    hnT = _transpose(hn, name=f"{tag}_hnT")
    dwg = _matmul(hnT, dgate, name=f"{tag}_dwg", out_chunked=True, out_dtype=BF16)
    dwu = _matmul(hnT, dup, name=f"{tag}_dwu", out_chunked=True, out_dtype=BF16)
    dhn = _matmul(dgate, wg, name=f"{tag}_dhn_g", nt=True, b_chunked=True)
    dhn = _matmul(dup, wu, name=f"{tag}_dhn_u", nt=True, b_chunked=True, residual=dhn)
    dh_in, dg = _rms_bwd(dhn, h, g, dh, name=f"{tag}_drms")
    return dh_in, dg, dwg, dwu, dwd


class _MixDims:
    def __init__(self, d_model):
        self.wf = d_model // 2
        self.ws = d_model // 2
        self.pf = self.wf // LANES
        self.ps = self.ws // LANES
        self.hf = self.wf // HEAD_DIM
        self.hq = self.ws // HEAD_DIM
        self.nkv = max(1, self.hq // SWA_GROUP)
        self.g = self.hq // self.nkv
        self.bq_f, self.bk_f, self.bv_f = 0, self.pf, 2 * self.pf
        self.bq_s = 3 * self.pf
        self.bk_s = self.bq_s + self.ps
        self.bv_s = self.bk_s + self.nkv
        self.bz = self.bv_s + self.nkv
        self.nu = (self.bz + 1) * LANES
        self.nup = -(-self.nu // 512) * 512
        self.in_width = 3 * self.wf + self.hf + self.ws + 2 * self.nkv * HEAD_DIM
        assert self.hf <= 2 * (LANES // 8)

    def gate_lane(self, h):
        return 8 * (h // 2) + h % 2

    def column_map(self):
        wf, ws, hd = self.wf, self.ws, HEAD_DIM
        src = np.full((self.nup,), -1, np.int64)
        src[0:3 * wf] = np.arange(3 * wf)
        o_sq = 3 * wf + self.hf
        src[self.bq_s * LANES:self.bq_s * LANES + ws] = o_sq + np.arange(ws)
        o_sk = o_sq + ws
        o_sv = o_sk + self.nkv * hd
        for kv in range(self.nkv):
            for rep in range(2):
                c0 = (self.bk_s + kv) * LANES + rep * hd
                src[c0:c0 + hd] = o_sk + kv * hd + np.arange(hd)
                c0 = (self.bv_s + kv) * LANES + rep * hd
                src[c0:c0 + hd] = o_sv + kv * hd + np.arange(hd)
        for h in range(self.hf):
            src[self.bz * LANES + self.gate_lane(h)] = 3 * wf + h
        return src

    def grad_column_map(self):
        src = self.column_map()
        dst = np.zeros((self.in_width,), np.int64)
        for col in range(self.nup - 1, -1, -1):
            if src[col] >= 0:
                dst[src[col]] = col
        return dst


def _block_diag_mean():
    m = np.zeros((LANES, LANES), np.float32)
    m[:HEAD_DIM, :HEAD_DIM] = 1.0 / HEAD_DIM
    m[HEAD_DIM:, HEAD_DIM:] = 1.0 / HEAD_DIM
    return jnp.asarray(m)


def _fold_halves():
    m = np.eye(LANES, dtype=np.float32)
    m[np.arange(LANES), (np.arange(LANES) + HEAD_DIM) % LANES] = 1.0
    return jnp.asarray(m)


def _gate_expand(md):
    e = np.zeros((LANES, md.wf), np.float32)
    for h in range(md.hf):
        e[md.gate_lane(h), h * HEAD_DIM:(h + 1) * HEAD_DIM] = 1.0
    return jnp.asarray(e)


def _qblocks(L, qb):
    blocks = [(0, BLOCK)]
    r = BLOCK
    while r < L:
        blocks.append((r, qb))
        r += qb
    assert r == L
    return blocks


def _f32dot(a, b):
    return jnp.dot(a, b, precision=HIGHEST, preferred_element_type=F32)


_DIMS_NT = (((1,), (1,)), ((), ()))


def _dot_nt(a, b):
    return lax.dot_general(a, b, _DIMS_NT, preferred_element_type=F32)


def _dot_tn(a, b):
    return jnp.dot(a.T.astype(BF16), b, preferred_element_type=F32)


def _log_sigmoid(z):
    return jnp.minimum(z, 0.0) - jnp.log(1.0 + jnp.exp(-jnp.abs(z)))


def _gate_fwd(u, b, md, *, name):
    L = u.shape[0]
    nb = L // BLOCK
    expand = _gate_expand(md)

    def body(z_ref, b_ref, e_ref, cexp_ref, ct_ref, c_s):
        ri = lax.broadcasted_iota(jnp.int32, (BLOCK, BLOCK), 0)
        ci = lax.broadcasted_iota(jnp.int32, (BLOCK, BLOCK), 1)
        tri = (ri >= ci).astype(F32)
        carry = jnp.zeros((1, LANES), F32)
        for bi in range(nb):
            rows = pl.ds(bi * BLOCK, BLOCK)
            logf = _log_sigmoid(z_ref[rows, :] + b_ref[...])
            blk = _f32dot(tri, logf) + carry
            c_s[rows, :] = blk
            carry = blk[BLOCK - 1:BLOCK, :]
        c = c_s[...]
        ct_ref[...] = c.T
        cexp_ref[...] = _f32dot(c, e_ref[...])

    return pl.pallas_call(
        body, name=name, grid=(1,),
        in_specs=[pl.BlockSpec((L, LANES), lambda i: (0, md.bz)), pl.BlockSpec((1, LANES), lambda i: (0, 0)),
                  pl.BlockSpec((LANES, md.wf), lambda i: (0, 0))],
        out_specs=[pl.BlockSpec((L, md.wf), lambda i: (0, 0)), pl.BlockSpec((LANES, L), lambda i: (0, 0))],
        out_shape=[jax.ShapeDtypeStruct((L, md.wf), F32), jax.ShapeDtypeStruct((LANES, L), F32)],
        scratch_shapes=[pltpu.VMEM((L, LANES), F32)],
        compiler_params=_cparams("arbitrary"),
    )(u, b, expand)


def _gate_bwd(u, b, dck_t, md, *, name):
    L = u.shape[0]
    nb = L // BLOCK

    def body(z_ref, b_ref, dck_ref, dz_ref, db_ref, dc_s):
        ri = lax.broadcasted_iota(jnp.int32, (BLOCK, BLOCK), 0)
        ci = lax.broadcasted_iota(jnp.int32, (BLOCK, BLOCK), 1)
        triu = (ri <= ci).astype(F32)
        dc_s[...] = -dck_ref[...].T
        carry = jnp.zeros((1, LANES), F32)
        db = jnp.zeros((1, LANES), F32)
        for bi in range(nb - 1, -1, -1):
            rows = pl.ds(bi * BLOCK, BLOCK)
            blk = _f32dot(triu, dc_s[rows, :]) + carry
            carry = blk[0:1, :]
            z = z_ref[rows, :] + b_ref[...]
            dz = blk * jax.nn.sigmoid(-z)
            if bi == 0:
                dz = jnp.where(lax.broadcasted_iota(jnp.int32, (BLOCK, LANES), 0) >= PAD, dz, 0.0)
            dz_ref[rows, :] = dz
            db = db + jnp.sum(dz, axis=0, keepdims=True)
        db_ref[...] = db

    return pl.pallas_call(
        body, name=name, grid=(1,),
        in_specs=[pl.BlockSpec((L, LANES), lambda i: (0, md.bz)), pl.BlockSpec((1, LANES), lambda i: (0, 0)),
                  pl.BlockSpec((LANES, L), lambda i: (0, 0))],
        out_specs=[pl.BlockSpec((L, LANES), lambda i: (0, 0)), pl.BlockSpec((1, LANES), lambda i: (0, 0))],
        out_shape=[jax.ShapeDtypeStruct((L, LANES), F32), jax.ShapeDtypeStruct((1, LANES), F32)],
        scratch_shapes=[pltpu.VMEM((L, LANES), F32)],
        compiler_params=_cparams("arbitrary"),
    )(u, b, dck_t)


def _head_norm(x, g, bd):
    r = lax.rsqrt(_f32dot(x * x, bd) + EPS)
    xh = x * r
    return xh * g, xh, r


def _head_norm_bwd(dy, xh, r, g, bd):
    dxh = dy * g
    dx = r * (dxh - xh * _f32dot(dxh * xh, bd))
    return dx, jnp.sum(dy * xh, axis=0, keepdims=True)


def _lane_half():
    return lax.broadcasted_iota(jnp.int32, (1, LANES), 1) < HEAD_DIM


def _store_head_pair(x, half, a_s, b_s):
    a_s[...] = jnp.where(half, x, 0.0).astype(BF16)
    b_s[...] = jnp.where(half, 0.0, x).astype(BF16)


def _fox_scores(qm, kn_s, cexp_ref, ct_ref, r0, nr, klen, hh):
    s = _dot_nt(qm, kn_s[0:klen, :]) * (HEAD_DIM ** -0.5)
    s = s + cexp_ref[r0:r0 + nr, HEAD_DIM * hh:HEAD_DIM * hh + 1] - ct_ref[hh:hh + 1, 0:klen]
    qp = r0 + lax.broadcasted_iota(jnp.int32, (nr, klen), 0)
    kp = lax.broadcasted_iota(jnp.int32, (nr, klen), 1)
    return jnp.where((kp <= qp) & (kp >= PAD), s, NEG_INF)


def _fox_fwd(u, cexp, ct3, qg, kg, md, *, name):
    L = u.shape[0]
    blocks = _qblocks(L, QBLOCK)
    bd = _block_diag_mean()

    def body(q_ref, k_ref, v_ref, cexp_ref, ct_ref, qg_ref, kg_ref, bd_ref, o_ref, lse_ref, qa_s, qb_s, kn_s, v_s):
        half = _lane_half()
        bdv = bd_ref[...]
        _store_head_pair(_head_norm(q_ref[...], qg_ref[...], bdv)[0], half, qa_s, qb_s)
        kn_s[...] = _head_norm(k_ref[...], kg_ref[...], bdv)[0].astype(BF16)
        v_s[...] = v_ref[...].astype(BF16)
        for r0, nr in blocks:
            klen = r0 + nr
            o_blk = lse_blk = None
            for hh, q_s in enumerate((qa_s, qb_s)):
                s = _fox_scores(q_s[r0:r0 + nr, :], kn_s, cexp_ref, ct_ref, r0, nr, klen, hh)
                m = jnp.max(s, axis=-1, keepdims=True)
                p = jnp.exp(s - m)
                l = jnp.sum(p, axis=-1, keepdims=True)
                oh = jnp.dot(p.astype(BF16), v_s[0:klen, :], preferred_element_type=F32) * (1.0 / l)
                lh = jnp.broadcast_to(jnp.where(m > 0.5 * NEG_INF, m + jnp.log(l), 0.0), (nr, LANES))
                o_blk = oh if hh == 0 else jnp.where(half, o_blk, oh)
                lse_blk = lh if hh == 0 else jnp.where(half, lse_blk, lh)
            o_ref[r0:r0 + nr, :] = o_blk
            lse_ref[r0:r0 + nr, :] = lse_blk

    col = lambda base: pl.BlockSpec((L, LANES), lambda j: (0, base + j))
    vec = pl.BlockSpec((1, LANES), lambda j: (0, 0))
    return pl.pallas_call(
        body, name=name, grid=(md.pf,),
        in_specs=[col(md.bq_f), col(md.bk_f), col(md.bv_f), col(0),
                  pl.BlockSpec((None, 8, L), lambda j: (j, 0, 0)), vec, vec,
                  pl.BlockSpec((LANES, LANES), lambda j: (0, 0))],
        out_specs=[col(0), col(0)],
        out_shape=[jax.ShapeDtypeStruct((L, md.wf), F32)] * 2,
        scratch_shapes=[pltpu.VMEM((L, LANES), BF16)] * 4,
        compiler_params=_cparams("parallel"),
    )(u, u, u, cexp, ct3, qg, kg, bd)


def _softmax_bwd(p, dp):
    pdp = p * dp
    return pdp - p * jnp.sum(pdp, axis=-1, keepdims=True)


def _fox_bwd(u, cexp, ct3, qg, kg, do, lse, md, *, name):
    L = u.shape[0]
    blocks = _qblocks(L, QBLOCK // 2)
    bd = _block_diag_mean()
    fold = _fold_halves()
    npairs = md.pf

    def body(q_ref, k_ref, v_ref, cexp_ref, ct_ref, qg_ref, kg_ref, bd_ref, fold_ref, do_ref, lse_ref,
             dq_ref, dk_ref, dv_ref, dck_ref, dqg_ref, dkg_ref,
             qa_s, qb_s, kn_s, v_s, doa_s, dob_s, dqn_s, dkn_s, dvv_s):
        j = pl.program_id(0)
        half = _lane_half()
        bdv = bd_ref[...]
        qy, qh, rq = _head_norm(q_ref[...], qg_ref[...], bdv)
        ky, kh, rk = _head_norm(k_ref[...], kg_ref[...], bdv)
        _store_head_pair(qy, half, qa_s, qb_s)
        _store_head_pair(do_ref[...], half, doa_s, dob_s)
        kn_s[...] = ky.astype(BF16)
        v_s[...] = v_ref[...].astype(BF16)
        dkn_s[...] = jnp.zeros_like(dkn_s)
        dvv_s[...] = jnp.zeros_like(dvv_s)
        dck_ref[...] = jnp.zeros_like(dck_ref)
        for r0, nr in blocks:
            klen = r0 + nr
            dq_blk = None
            for hh, (q_s, do_s) in enumerate(((qa_s, doa_s), (qb_s, dob_s))):
                c0 = HEAD_DIM * hh
                qm = q_s[r0:r0 + nr, :]
                dom = do_s[r0:r0 + nr, :]
                s = _fox_scores(qm, kn_s, cexp_ref, ct_ref, r0, nr, klen, hh)
                p = jnp.exp(s - lse_ref[r0:r0 + nr, c0:c0 + 1])
                ds = _softmax_bwd(p, _dot_nt(dom, v_s[0:klen, :]))
                dck_ref[hh:hh + 1, 0:klen] += jnp.sum(ds, axis=0, keepdims=True)
                ds = ds * (HEAD_DIM ** -0.5)
                dq_h = jnp.dot(ds.astype(BF16), kn_s[0:klen, :], preferred_element_type=F32)
                dkn_s[0:klen, :] += _dot_tn(ds, qm)
                dvv_s[0:klen, :] += _dot_tn(p, dom)
                dq_blk = dq_h if hh == 0 else jnp.where(half, dq_blk, dq_h)
            dqn_s[r0:r0 + nr, :] = dq_blk
        dq, dqg = _head_norm_bwd(dqn_s[...], qh, rq, qg_ref[...], bdv)
        dk, dkg = _head_norm_bwd(dkn_s[...], kh, rk, kg_ref[...], bdv)
        dq_ref[...] = dq
        dk_ref[...] = dk
        dv_ref[...] = dvv_s[...]

        @pl.when(j == 0)
        def _():
            dqg_ref[...] = jnp.zeros_like(dqg_ref)
            dkg_ref[...] = jnp.zeros_like(dkg_ref)

        dqg_ref[...] += dqg
        dkg_ref[...] += dkg

        @pl.when(j == npairs - 1)
        def _():
            dqg_ref[...] = _f32dot(jnp.broadcast_to(dqg_ref[...], (8, LANES)), fold_ref[...])[0:1, :]
            dkg_ref[...] = _f32dot(jnp.broadcast_to(dkg_ref[...], (8, LANES)), fold_ref[...])[0:1, :]

    col = lambda base: pl.BlockSpec((L, LANES), lambda j: (0, base + j))
    vec = pl.BlockSpec((1, LANES), lambda j: (0, 0))
    sq = pl.BlockSpec((LANES, LANES), lambda j: (0, 0))
    ct_spec = pl.BlockSpec((None, 8, L), lambda j: (j, 0, 0))
    big = jax.ShapeDtypeStruct((L, md.wf), F32)
    small = jax.ShapeDtypeStruct((1, LANES), F32)
    return pl.pallas_call(
        body, name=name, grid=(md.pf,),
        in_specs=[col(md.bq_f), col(md.bk_f), col(md.bv_f), col(0), ct_spec, vec, vec, sq, sq, col(0), col(0)],
        out_specs=[col(0), col(0), col(0), ct_spec, vec, vec],
        out_shape=[big, big, big, jax.ShapeDtypeStruct((md.pf, 8, L), F32), small, small],
        scratch_shapes=[pltpu.VMEM((L, LANES), BF16)] * 6 + [pltpu.VMEM((L, LANES), F32)] * 3,
        compiler_params=_cparams("arbitrary"),
    )(u, u, u, cexp, ct3, qg, kg, bd, fold, do, lse)


def _swa_scores(qm, kn_s, slope, k0, r0, nr, klen):
    s = _dot_nt(qm, kn_s[k0:k0 + klen, :]) * (HEAD_DIM ** -0.5)
    qp = r0 + lax.broadcasted_iota(jnp.int32, (nr, klen), 0)
    kp = k0 + lax.broadcasted_iota(jnp.int32, (nr, klen), 1)
    dist = qp - kp
    s = s - slope * dist.astype(F32)
    return jnp.where((dist >= 0) & (dist < WINDOW) & (kp >= PAD), s, NEG_INF)


def _swa_fwd(u, sinkp, slopep, qg, kg, md, *, name):
    L = u.shape[0]
    blocks = _qblocks(L, QBLOCK)
    bd = _block_diag_mean()

    def body(q_ref, k_ref, v_ref, sink_ref, slope_ref, qg_ref, kg_ref, bd_ref, o_ref, lse_ref, qa_s, qb_s, kn_s, v_s):
        half = _lane_half()
        bdv = bd_ref[...]
        _store_head_pair(_head_norm(q_ref[...], qg_ref[...], bdv)[0], half, qa_s, qb_s)
        kn_s[...] = _head_norm(k_ref[...], kg_ref[...], bdv)[0].astype(BF16)
        v_s[...] = v_ref[...].astype(BF16)
        for r0, nr in blocks:
            k0 = max(r0 - BLOCK, 0)
            klen = r0 + nr - k0
            o_blk = lse_blk = None
            for hh, q_s in enumerate((qa_s, qb_s)):
                c0 = HEAD_DIM * hh
                s = _swa_scores(q_s[r0:r0 + nr, :], kn_s, slope_ref[0:1, c0:c0 + 1], k0, r0, nr, klen)
                sink = sink_ref[0:1, c0:c0 + 1]
                m = jnp.maximum(jnp.max(s, axis=-1, keepdims=True), sink)
                p = jnp.exp(s - m)
                den = jnp.sum(p, axis=-1, keepdims=True) + jnp.exp(sink - m)
                oh = jnp.dot(p.astype(BF16), v_s[k0:k0 + klen, :], preferred_element_type=F32) * (1.0 / den)
                lh = jnp.broadcast_to(m + jnp.log(den), (nr, LANES))
                o_blk = oh if hh == 0 else jnp.where(half, o_blk, oh)
                lse_blk = lh if hh == 0 else jnp.where(half, lse_blk, lh)
            o_ref[r0:r0 + nr, :] = o_blk
            lse_ref[r0:r0 + nr, :] = lse_blk

    g2 = md.g // 2
    qcol = pl.BlockSpec((L, LANES), lambda j: (0, md.bq_s + j))
    kcol = pl.BlockSpec((L, LANES), lambda j: (0, md.bk_s + j // g2))
    vcol = pl.BlockSpec((L, LANES), lambda j: (0, md.bv_s + j // g2))
    ocol = pl.BlockSpec((L, LANES), lambda j: (0, j))
    pvec = pl.BlockSpec((1, LANES), lambda j: (0, j))
    vec = pl.BlockSpec((1, LANES), lambda j: (0, 0))
    return pl.pallas_call(
        body, name=name, grid=(md.ps,),
        in_specs=[qcol, kcol, vcol, pvec, pvec, vec, vec, pl.BlockSpec((LANES, LANES), lambda j: (0, 0))],
        out_specs=[ocol, ocol],
        out_shape=[jax.ShapeDtypeStruct((L, md.ws), F32)] * 2,
        scratch_shapes=[pltpu.VMEM((L, LANES), BF16)] * 4,
        compiler_params=_cparams("parallel"),
    )(u, u, u, sinkp, slopep, qg, kg, bd)


def _swa_bwd(u, sinkp, slopep, qg, kg, do, lse, md, *, name):
    L = u.shape[0]
    blocks = _qblocks(L, QBLOCK // 2)
    bd = _block_diag_mean()
    fold = _fold_halves()
    g2 = md.g // 2
    npairs = md.ps

    def body(q_ref, k_ref, v_ref, sink_ref, slope_ref, qg_ref, kg_ref, bd_ref, fold_ref, do_ref, lse_ref,
             dq_ref, dk_ref, dv_ref, dsink_ref, dqg_ref, dkg_ref,
             qa_s, qb_s, kn_s, v_s, doa_s, dob_s, dqn_s):
        j = pl.program_id(0)
        half = _lane_half()
        bdv = bd_ref[...]
        qy, qh, rq = _head_norm(q_ref[...], qg_ref[...], bdv)
        ky, kh, rk = _head_norm(k_ref[...], kg_ref[...], bdv)
        _store_head_pair(qy, half, qa_s, qb_s)
        _store_head_pair(do_ref[...], half, doa_s, dob_s)
        kn_s[...] = ky.astype(BF16)
        v_s[...] = v_ref[...].astype(BF16)

        @pl.when(j % g2 == 0)
        def _():
            dk_ref[...] = jnp.zeros_like(dk_ref)
            dv_ref[...] = jnp.zeros_like(dv_ref)

        @pl.when(j == 0)
        def _():
            dqg_ref[...] = jnp.zeros_like(dqg_ref)
            dkg_ref[...] = jnp.zeros_like(dkg_ref)

        dsink = [jnp.zeros((1, 1), F32), jnp.zeros((1, 1), F32)]
        for r0, nr in blocks:
            k0 = max(r0 - BLOCK, 0)
            klen = r0 + nr - k0
            dq_blk = None
            for hh, (q_s, do_s) in enumerate(((qa_s, doa_s), (qb_s, dob_s))):
                c0 = HEAD_DIM * hh
                qm = q_s[r0:r0 + nr, :]
                dom = do_s[r0:r0 + nr, :]
                s = _swa_scores(qm, kn_s, slope_ref[0:1, c0:c0 + 1], k0, r0, nr, klen)
                lse_h = lse_ref[r0:r0 + nr, c0:c0 + 1]
                p = jnp.exp(s - lse_h)
                pdp = p * _dot_nt(dom, v_s[k0:k0 + klen, :])
                delta = jnp.sum(pdp, axis=-1, keepdims=True)
                p_sink = jnp.exp(sink_ref[0:1, c0:c0 + 1] - lse_h)
                dsink[hh] = dsink[hh] - jnp.sum(p_sink * delta, axis=0, keepdims=True)
                ds = (pdp - p * delta) * (HEAD_DIM ** -0.5)
                dq_h = jnp.dot(ds.astype(BF16), kn_s[k0:k0 + klen, :], preferred_element_type=F32)
                dk_ref[k0:k0 + klen, :] += _dot_tn(ds, qm)
                dv_ref[k0:k0 + klen, :] += _dot_tn(p, dom)
                dq_blk = dq_h if hh == 0 else jnp.where(half, dq_blk, dq_h)
            dqn_s[r0:r0 + nr, :] = dq_blk
        dq, dqg = _head_norm_bwd(dqn_s[...], qh, rq, qg_ref[...], bdv)
        dq_ref[...] = dq
        dqg_ref[...] += dqg
        dsink_ref[...] = jnp.where(half, jnp.broadcast_to(dsink[0], (1, LANES)), jnp.broadcast_to(dsink[1], (1, LANES)))

        @pl.when(j % g2 == g2 - 1)
        def _():
            dkn = _f32dot(dk_ref[...], fold_ref[...])
            dk, dkg = _head_norm_bwd(dkn, kh, rk, kg_ref[...], bdv)
            dk_ref[...] = jnp.where(half, dk, 0.0)
            dv_ref[...] = jnp.where(half, _f32dot(dv_ref[...], fold_ref[...]), 0.0)
            dkg_ref[...] += dkg

        @pl.when(j == npairs - 1)
        def _():
            dqg_ref[...] = _f32dot(jnp.broadcast_to(dqg_ref[...], (8, LANES)), fold_ref[...])[0:1, :]

    qcol = pl.BlockSpec((L, LANES), lambda j: (0, md.bq_s + j))
    kcol = pl.BlockSpec((L, LANES), lambda j: (0, md.bk_s + j // g2))
    vcol = pl.BlockSpec((L, LANES), lambda j: (0, md.bv_s + j // g2))
    ocol = pl.BlockSpec((L, LANES), lambda j: (0, j))
    kvout = pl.BlockSpec((L, LANES), lambda j: (0, j // g2))
    pvec = pl.BlockSpec((1, LANES), lambda j: (0, j))
    vec = pl.BlockSpec((1, LANES), lambda j: (0, 0))
    sq = pl.BlockSpec((LANES, LANES), lambda j: (0, 0))
    kvshape = jax.ShapeDtypeStruct((L, LANES * md.nkv), F32)
    small = jax.ShapeDtypeStruct((1, LANES), F32)
    return pl.pallas_call(
        body, name=name, grid=(md.ps,),
        in_specs=[qcol, kcol, vcol, pvec, pvec, vec, vec, sq, sq, ocol, ocol],
        out_specs=[ocol, kvout, kvout, pvec, vec, vec],
        out_shape=[jax.ShapeDtypeStruct((L, md.ws), F32), kvshape, kvshape,
                   jax.ShapeDtypeStruct((1, md.ws), F32), small, small],
        scratch_shapes=[pltpu.VMEM((L, LANES), BF16)] * 6 + [pltpu.VMEM((L, LANES), F32)],
        compiler_params=_cparams("arbitrary"),
    )(u, u, u, sinkp, slopep, qg, kg, bd, fold, do, lse)


def _outnorm_fwd(of, os_, gf, gs, *, name):
    L, wf = of.shape
    ws = os_.shape[1]
    tr = _pick(L, _TR)

    def body(of_ref, os_ref, gf_ref, gs_ref, o_ref):
        for src, g_ref, c0, w in ((of_ref, gf_ref, 0, wf), (os_ref, gs_ref, wf, ws)):
            x = src[...]
            r = lax.rsqrt(jnp.mean(x * x, axis=-1, keepdims=True) + EPS)
            o_ref[:, c0:c0 + w] = (x * r * g_ref[...]).astype(o_ref.dtype)

    return pl.pallas_call(
        body, name=name, grid=(L // tr,),
        in_specs=[pl.BlockSpec((tr, wf), lambda i: (i, 0)), pl.BlockSpec((tr, ws), lambda i: (i, 0)),
                  pl.BlockSpec((1, wf), lambda i: (0, 0)), pl.BlockSpec((1, ws), lambda i: (0, 0))],
        out_specs=pl.BlockSpec((tr, wf + ws), lambda i: (i, 0)),
        out_shape=jax.ShapeDtypeStruct((L, wf + ws), BF16),
        compiler_params=_cparams("parallel"),
    )(of, os_, gf, gs)


def _outnorm_bwd(don, of, os_, gf, gs, *, name):
    L, wf = of.shape
    ws = os_.shape[1]
    tr = _pick(L, _TR)

    def body(d_ref, of_ref, os_ref, gf_ref, gs_ref, dof_ref, dos_ref, dgf_ref, dgs_ref):
        i = pl.program_id(0)
        for src, g_ref, c0, w, dx_ref, dg_ref in ((of_ref, gf_ref, 0, wf, dof_ref, dgf_ref),
                                                  (os_ref, gs_ref, wf, ws, dos_ref, dgs_ref)):
            x = src[...]
            dy = d_ref[:, c0:c0 + w]
            r = lax.rsqrt(jnp.mean(x * x, axis=-1, keepdims=True) + EPS)
            xh = x * r
            dxh = dy * g_ref[...]
            dx_ref[...] = r * (dxh - xh * jnp.mean(dxh * xh, axis=-1, keepdims=True))
            part = jnp.sum(dy * xh, axis=0, keepdims=True)

            @pl.when(i == 0)
            def _():
                dg_ref[...] = part

            @pl.when(i > 0)
            def _():
                dg_ref[...] += part

    rf = pl.BlockSpec((tr, wf), lambda i: (i, 0))
    rs = pl.BlockSpec((tr, ws), lambda i: (i, 0))
    vf = pl.BlockSpec((1, wf), lambda i: (0, 0))
    vs = pl.BlockSpec((1, ws), lambda i: (0, 0))
    return pl.pallas_call(
        body, name=name, grid=(L // tr,),
        in_specs=[pl.BlockSpec((tr, wf + ws), lambda i: (i, 0)), rf, rs, vf, vs],
        out_specs=[rf, rs, vf, vs],
        out_shape=[jax.ShapeDtypeStruct((L, wf), F32), jax.ShapeDtypeStruct((L, ws), F32),
                   jax.ShapeDtypeStruct((1, wf), F32), jax.ShapeDtypeStruct((1, ws), F32)],
        compiler_params=_cparams("arbitrary"),
    )(don, of, os_, gf, gs)


def _win_to_mine(w, md):
    d = w.shape[0]
    wf, ws, hd = md.wf, md.ws, HEAD_DIM
    o_z = 3 * wf
    o_sq = o_z + md.hf
    o_sk = o_sq + ws
    o_sv = o_sk + md.nkv * hd
    parts = [w[:, :3 * wf], w[:, o_sq:o_sq + ws]]
    for base in (o_sk, o_sv):
        for kv in range(md.nkv):
            blk = w[:, base + kv * hd:base + (kv + 1) * hd]
            parts += [blk, blk]
    z = w[:, o_z:o_z + md.hf].reshape(d, md.hf // 2, 2)
    z = jnp.pad(z, ((0, 0), (0, 0), (0, 6))).reshape(d, 4 * md.hf)
    parts.append(jnp.pad(z, ((0, 0), (0, LANES - 4 * md.hf + md.nup - md.nu))))
    return jnp.concatenate(parts, axis=1)


def _win_grad_to_ref(dw, md):
    d = dw.shape[0]
    wf, ws, hd = md.wf, md.ws, HEAD_DIM
    z = dw[:, md.bz * LANES:md.bz * LANES + 4 * md.hf].reshape(d, md.hf // 2, 8)[:, :, :2].reshape(d, md.hf)
    parts = [dw[:, :3 * wf], z, dw[:, md.bq_s * LANES:md.bq_s * LANES + ws]]
    for base in (md.bk_s, md.bv_s):
        for kv in range(md.nkv):
            c0 = (base + kv) * LANES
            parts.append(dw[:, c0:c0 + hd])
    return jnp.concatenate(parts, axis=1)


def _mix_small(p, md):
    tile2 = lambda v: jnp.tile(v.reshape(1, HEAD_DIM), (1, 2))
    b = p["b_forget"].reshape(md.hf // 2, 2)
    b = jnp.pad(b, ((0, 0), (0, 6))).reshape(1, 4 * md.hf)
    slopes = np.asarray(2.0 ** (-8.0 * np.arange(1, md.hq + 1) / md.hq), np.float32)
    return dict(
        g_mix=p["mix_norm"].reshape(1, -1),
        b_gate=jnp.pad(b, ((0, 0), (0, LANES - 4 * md.hf))),
        fqg=tile2(p["fox_q_norm"]), fkg=tile2(p["fox_k_norm"]),
        sqg=tile2(p["swa_q_norm"]), skg=tile2(p["swa_k_norm"]),
        sinkp=jnp.repeat(p["swa_sinks"], HEAD_DIM).reshape(1, md.ws),
        slopep=jnp.asarray(np.repeat(slopes, HEAD_DIM).reshape(1, md.ws)),
        gfo=p["fox_out_norm"].reshape(1, md.wf), gso=p["swa_out_norm"].reshape(1, md.ws),
    )


def _mix_fwd(h, sp, w_in, w_out, md):
    L = h.shape[0]
    hn = _rms_fwd(h, sp["g_mix"], name="mix_rms")
    u = _matmul(hn, w_in, name="mix_u")
    cexp, ct = _gate_fwd(u, sp["b_gate"], md, name="gate_fwd")
    ct3 = ct[:8 * md.pf].reshape(md.pf, 8, L)
    of, lsef = _fox_fwd(u, cexp, ct3, sp["fqg"], sp["fkg"], md, name="fox_fwd")
    os_, lses = _swa_fwd(u, sp["sinkp"], sp["slopep"], sp["sqg"], sp["skg"], md, name="swa_fwd")
    on = _outnorm_fwd(of, os_, sp["gfo"], sp["gso"], name="outnorm_fwd")
    h_out = _matmul(on, w_out, name="mix_out", residual=h)
    return h_out, (h, hn, u, cexp, ct3, of, lsef, os_, lses, on)


def _mix_bwd(dh, saved, sp, w_in, w_out, md):
    h, hn, u, cexp, ct3, of, lsef, os_, lses, on = saved
    L = h.shape[0]
    dhb = _cast_bf16(dh, name="mix_dhb")
    don = _matmul(dhb, w_out, name="mix_don", nt=True)
    dw_out = _matmul(_transpose(on, name="mix_onT"), dhb, name="mix_dwout", out_dtype=BF16)
    dof, dos, dgfo, dgso = _outnorm_bwd(don, of, os_, sp["gfo"], sp["gso"], name="outnorm_bwd")
    duq, duk, duv, dck, dfqg, dfkg = _fox_bwd(u, cexp, ct3, sp["fqg"], sp["fkg"], dof, lsef, md, name="fox_bwd")
    dsq, dsk, dsv, dsinkp, dsqg, dskg = _swa_bwd(u, sp["sinkp"], sp["slopep"], sp["sqg"], sp["skg"], dos, lses, md, name="swa_bwd")
    dck_t = jnp.pad(dck.reshape(8 * md.pf, L), ((0, LANES - 8 * md.pf), (0, 0)))
    dz, db = _gate_bwd(u, sp["b_gate"], dck_t, md, name="gate_bwd")
    du = jnp.concatenate([duq, duk, duv, dsq, dsk, dsv, dz, jnp.zeros((L, md.nup - md.nu), F32)], axis=1).astype(BF16)
    dhn = _matmul(du, w_in, name="mix_dhn", nt=True)
    dw_in = _matmul(_transpose(hn, name="mix_hnT"), du, name="mix_dwin", out_dtype=BF16)
    dh_in, dg_mix = _rms_bwd(dhn, h, sp["g_mix"], dh, name="mix_drms")
    small = dict(
        mix_norm=dg_mix.reshape(-1),
        b_forget=db[0, :4 * md.hf].reshape(md.hf // 2, 8)[:, :2].reshape(md.hf),
        fox_q_norm=dfqg[0, :HEAD_DIM], fox_k_norm=dfkg[0, :HEAD_DIM],
        swa_q_norm=dsqg[0, :HEAD_DIM], swa_k_norm=dskg[0, :HEAD_DIM],
        swa_sinks=dsinkp[0, ::HEAD_DIM],
        fox_out_norm=dgfo.reshape(-1), swa_out_norm=dgso.reshape(-1),
    )
    return dh_in, dw_in, dw_out, small


_ANY = pl.BlockSpec(memory_space=pl.ANY)
_HALF_ROWS = (512, 352, 256, 192, 128, 64, 32, 16)


def _mesh_pos():
    return lax.axis_index("x"), lax.axis_index("y"), lax.axis_index("c")


def _other_chips(x, y):
    return [(1 - x, y), (x, 1 - y), (1 - x, 1 - y)]


def _rows_half(ref, which):
    rh = ref.shape[-2] // 2
    if len(ref.shape) == 2:
        return ref.at[pl.ds(which * rh, rh), :]
    return ref.at[:, pl.ds(which * rh, rh), :]


def _remote(src, dst, send_sems, recv_sems, idx, dev):
    return pltpu.make_async_remote_copy(src_ref=src, dst_ref=dst, send_sem=send_sems.at[idx], recv_sem=recv_sems.at[idx],
                                        device_id=dev, device_id_type=MESH)


def _allgather_chips(arrs, *, name):
    n = len(arrs)

    def body(*refs):
        srcs, outs = refs[:n], refs[n:2 * n]
        send_sems, recv_sems, local_sems = refs[2 * n:]
        x, y, c = _mesh_pos()
        k = 2 * x + y
        sibling = (x, y, 1 - c)
        chips = _other_chips(x, y)
        local, sent = [], []
        for i in range(n):
            own = pltpu.make_async_copy(srcs[i], outs[i].at[k], local_sems.at[i])
            own.start()
            local.append(own)
            for j, (cx, cy) in enumerate(chips):
                cp = _remote(_rows_half(srcs[i], c), _rows_half(outs[i].at[k], c), send_sems, recv_sems, 6 * i + j, (cx, cy, c))
                cp.start()
                sent.append(cp)
        for i in range(n):
            for j, (cx, cy) in enumerate(chips):
                blk = _rows_half(outs[i].at[2 * cx + cy], c)
                _remote(blk, blk, send_sems, recv_sems, 6 * i + j, (cx, cy, c)).wait_recv()
                fwd = _remote(blk, blk, send_sems, recv_sems, 6 * i + 3 + j, sibling)
                fwd.start()
                sent.append(fwd)
        for i in range(n):
            for j, (cx, cy) in enumerate(chips):
                blk = _rows_half(outs[i].at[2 * cx + cy], 1 - c)
                _remote(blk, blk, send_sems, recv_sems, 6 * i + 3 + j, sibling).wait_recv()
        for cp in sent:
            cp.wait_send()
        for cp in local:
            cp.wait()

    return pl.pallas_call(
        body, name=name, in_specs=[_ANY] * n, out_specs=[_ANY] * n,
        out_shape=[jax.ShapeDtypeStruct((NCH,) + a.shape, a.dtype) for a in arrs],
        scratch_shapes=[pltpu.SemaphoreType.DMA((6 * n,)), pltpu.SemaphoreType.DMA((6 * n,)), pltpu.SemaphoreType.DMA((n,))],
    )(*arrs)


def _exchange_sibling_halves(arrs, *, name):
    n = len(arrs)

    def body(*refs):
        srcs, lands = refs[:n], refs[n:2 * n]
        send_sems, recv_sems = refs[2 * n:]
        x, y, c = _mesh_pos()
        sibling = (x, y, 1 - c)
        cps = [_remote(_rows_half(srcs[i], 1 - c), lands[i], send_sems, recv_sems, i, sibling) for i in range(n)]
        for cp in cps:
            cp.start()
        for cp in cps:
            cp.wait_recv()
        for cp in cps:
            cp.wait_send()

    return pl.pallas_call(
        body, name=name, in_specs=[_ANY] * n, out_specs=[_ANY] * n,
        out_shape=[jax.ShapeDtypeStruct((NCH, a.shape[1] // 2, a.shape[2]), a.dtype) for a in arrs],
        scratch_shapes=[pltpu.SemaphoreType.DMA((n,)), pltpu.SemaphoreType.DMA((n,))],
    )(*arrs)


def _exchange_chip_chunks(arrs, *, name):
    n = len(arrs)

    def body(*refs):
        srcs, lands = refs[:n], refs[n:2 * n]
        send_sems, recv_sems = refs[2 * n:]
        x, y, c = _mesh_pos()
        cps = []
        for i in range(n):
            for j, (cx, cy) in enumerate(_other_chips(x, y)):
                cps.append(_remote(srcs[i].at[2 * cx + cy], lands[i].at[j], send_sems, recv_sems, 3 * i + j, (cx, cy, c)))
        for cp in cps:
            cp.start()
        for cp in cps:
            cp.wait_recv()
        for cp in cps:
            cp.wait_send()

    return pl.pallas_call(
        body, name=name, in_specs=[_ANY] * n, out_specs=[_ANY] * n,
        out_shape=[jax.ShapeDtypeStruct((NCH - 1,) + a.shape[1:], a.dtype) for a in arrs],
        scratch_shapes=[pltpu.SemaphoreType.DMA((3 * n,)), pltpu.SemaphoreType.DMA((3 * n,))],
    )(*arrs)


def _share_with_sibling(arrs, *, name):
    n = len(arrs)

    def body(*refs):
        srcs, outs = refs[:n], refs[n:2 * n]
        send_sems, recv_sems, local_sems = refs[2 * n:]
        x, y, c = _mesh_pos()
        sibling = (x, y, 1 - c)
        local = [pltpu.make_async_copy(srcs[i], _rows_half(outs[i], c), local_sems.at[i]) for i in range(n)]
        cps = [_remote(srcs[i], _rows_half(outs[i], c), send_sems, recv_sems, i, sibling) for i in range(n)]
        for cp in local + cps:
            cp.start()
        for i in range(n):
            theirs = _rows_half(outs[i], 1 - c)
            _remote(theirs, theirs, send_sems, recv_sems, i, sibling).wait_recv()
        for cp in cps:
            cp.wait_send()
        for cp in local:
            cp.wait()

    return pl.pallas_call(
        body, name=name, in_specs=[_ANY] * n, out_specs=[_ANY] * n,
        out_shape=[jax.ShapeDtypeStruct((2 * a.shape[0], a.shape[1]), a.dtype) for a in arrs],
        scratch_shapes=[pltpu.SemaphoreType.DMA((n,)), pltpu.SemaphoreType.DMA((n,)), pltpu.SemaphoreType.DMA((n,))],
    )(*arrs)


def _allgather_devices(v, *, name):
    m = v.shape[0]

    def body(v_ref, out_ref, send_sems, recv_sems, local_sem):
        x, y, c = _mesh_pos()
        mine = 4 * x + 2 * y + c
        own = pltpu.make_async_copy(v_ref, out_ref.at[mine], local_sem)
        own.start()
        cps = []
        for r in range(1, 8):
            px, py, pc = (x + (r >> 2)) % 2, (y + ((r >> 1) & 1)) % 2, (c + (r & 1)) % 2
            cps.append((_remote(v_ref, out_ref.at[mine], send_sems, recv_sems, r - 1, (px, py, pc)), 4 * px + 2 * py + pc))
        for cp, _ in cps:
            cp.start()
        for r, (cp, theirs) in enumerate(cps):
            blk = out_ref.at[theirs]
            _remote(blk, blk, send_sems, recv_sems, r, (x, y, c)).wait_recv()
        for cp, _ in cps:
            cp.wait_send()
        own.wait()

    return pl.pallas_call(
        body, name=name, in_specs=[_ANY], out_specs=_ANY,
        out_shape=jax.ShapeDtypeStruct((8, m, LANES), v.dtype),
        scratch_shapes=[pltpu.SemaphoreType.DMA((7,)), pltpu.SemaphoreType.DMA((7,)), pltpu.SemaphoreType.DMA],
    )(v)


def _add_own_half(g, land, c_idx, *, name):
    nch, rh, cols = land.shape
    tr = _pick(rh, _HALF_ROWS)
    nt = rh // tr

    def body(c_ref, g_ref, l_ref, o_ref):
        o_ref[...] = (g_ref[...].astype(F32) + l_ref[...].astype(F32)).astype(o_ref.dtype)

    return pl.pallas_call(
        body, name=name,
        grid_spec=pltpu.PrefetchScalarGridSpec(
            num_scalar_prefetch=1, grid=(nch, nt),
            in_specs=[pl.BlockSpec((None, tr, cols), lambda k, i, c: (k, c[0] * nt + i, 0)),
                      pl.BlockSpec((None, tr, cols), lambda k, i, c: (k, i, 0))],
            out_specs=pl.BlockSpec((None, tr, cols), lambda k, i, c: (k, i, 0))),
        out_shape=jax.ShapeDtypeStruct(land.shape, BF16),
        compiler_params=_cparams("parallel", "parallel"),
    )(c_idx, g, land)


def _add_chunks(s, land, k_idx, *, name):
    _, rh, cols = s.shape
    tr = _pick(rh, _HALF_ROWS)

    def body(k_ref, s_ref, l_ref, o_ref):
        t = s_ref[...].astype(F32)
        for j in range(NCH - 1):
            t = t + l_ref[j].astype(F32)
        o_ref[...] = t

    return pl.pallas_call(
        body, name=name,
        grid_spec=pltpu.PrefetchScalarGridSpec(
            num_scalar_prefetch=1, grid=(rh // tr,),
            in_specs=[pl.BlockSpec((None, tr, cols), lambda i, k: (k[0], i, 0)),
                      pl.BlockSpec((NCH - 1, tr, cols), lambda i, k: (0, i, 0))],
            out_specs=pl.BlockSpec((tr, cols), lambda i, k: (i, 0))),
        out_shape=jax.ShapeDtypeStruct((rh, cols), F32),
        compiler_params=_cparams("parallel"),
    )(k_idx, s, land)


def _sum_devices(v, *, name):
    _, m, _ = v.shape

    def body(v_ref, o_ref):
        t = v_ref[0]
        for d in range(1, 8):
            t = t + v_ref[d]
        o_ref[...] = t

    return pl.pallas_call(
        body, name=name, grid=(1,),
        in_specs=[pl.BlockSpec((8, m, LANES), lambda i: (0, 0, 0))],
        out_specs=pl.BlockSpec((m, LANES), lambda i: (0, 0)),
        out_shape=jax.ShapeDtypeStruct((m, LANES), F32),
        compiler_params=_cparams("arbitrary"),
    )(v)


def _adamw_math(w, g, m, v):
    m = ADAM_B1 * m + (1.0 - ADAM_B1) * g
    v = ADAM_B2 * v + (1.0 - ADAM_B2) * (g * g)
    m_hat = m / (1.0 - ADAM_B1 ** ADAM_STEP)
    v_hat = v / (1.0 - ADAM_B2 ** ADAM_STEP)
    delta = -ADAM_LR * (m_hat / (jnp.sqrt(v_hat) + ADAM_EPS) + ADAM_WD * w)
    return delta, m, v


def _adamw_layer(w, m, v, g, layer, prev, *, name):
    depth, rows, cols = w.shape
    tr = _pick(rows, (256, 128, 64, 32, 16, 8))
    lay = pl.BlockSpec((None, tr, cols), lambda i, l: (l[0], i, 0))
    n_prev = 0 if prev is None else 4

    def body(l_ref, w_ref, m_ref, v_ref, g_ref, *rest):
        go_ref, d_ref, mo_ref, vo_ref = rest[n_prev:]
        g = g_ref[...]
        delta, m_new, v_new = _adamw_math(w_ref[...], g, m_ref[...], v_ref[...])
        go_ref[...] = g
        d_ref[...] = delta
        mo_ref[...] = m_new
        vo_ref[...] = v_new

    stack = jax.ShapeDtypeStruct(w.shape, F32)
    return pl.pallas_call(
        body, name=name,
        grid_spec=pltpu.PrefetchScalarGridSpec(
            num_scalar_prefetch=1, grid=(rows // tr,),
            in_specs=[lay, lay, lay, pl.BlockSpec((tr, cols), lambda i, l: (i, 0))] + [_ANY] * n_prev,
            out_specs=[lay] * 4),
        out_shape=[stack] * 4,
        input_output_aliases={} if prev is None else {5 + q: q for q in range(4)},
        compiler_params=_cparams("parallel"),
    )(layer, w, m, v, g, *(() if prev is None else prev))


def _adamw_flat(w, g, m, v, *, name):
    def body(w_ref, g_ref, m_ref, v_ref, d_ref, mo_ref, vo_ref):
        d_ref[...], mo_ref[...], vo_ref[...] = _adamw_math(w_ref[...], g_ref[...], m_ref[...], v_ref[...])

    blk = pl.BlockSpec(w.shape, lambda i: (0, 0))
    return pl.pallas_call(
        body, name=name, grid=(1,), in_specs=[blk] * 4, out_specs=[blk] * 3,
        out_shape=[jax.ShapeDtypeStruct(w.shape, F32)] * 3, compiler_params=_cparams("arbitrary"),
    )(w, g, m, v)


def _reduce_scatter(parts, c_idx, k_idx, tag):
    lands = _exchange_sibling_halves(parts, name=f"{tag}_rs_sibling")
    sums = [_add_own_half(p, l, c_idx, name=f"{tag}_rs_add2") for p, l in zip(parts, lands)]
    lands = _exchange_chip_chunks(sums, name=f"{tag}_rs_chips")
    tots = [_add_chunks(s, l, k_idx, name=f"{tag}_rs_add4") for s, l in zip(sums, lands)]
    return _share_with_sibling(tots, name=f"{tag}_rs_share")


_WEIGHTS = ('meta_tokens', 'ffn1_norm', 'ffn1_w_gate', 'ffn1_w_up', 'ffn1_w_down', 'mix_norm', 'w_in', 'b_forget',
            'fox_q_norm', 'fox_k_norm', 'swa_q_norm', 'swa_k_norm', 'swa_sinks', 'fox_out_norm', 'swa_out_norm', 'w_out',
            'ffn2_norm', 'ffn2_w_gate', 'ffn2_w_up', 'ffn2_w_down')
_BIG = ('ffn1_w_gate', 'ffn1_w_up', 'ffn1_w_down', 'w_in', 'w_out', 'ffn2_w_gate', 'ffn2_w_up', 'ffn2_w_down')
_SMALL = tuple(n for n in _WEIGHTS if n not in _BIG and n != 'meta_tokens')
_MIX_SMALL = ('mix_norm', 'b_forget', 'fox_q_norm', 'fox_k_norm', 'swa_q_norm', 'swa_k_norm', 'swa_sinks',
              'fox_out_norm', 'swa_out_norm')


def _pack_rows(vectors):
    flat = jnp.concatenate([v.reshape(-1) for v in vectors])
    n = flat.shape[0]
    m = -(-n // (8 * LANES)) * 8
    return jnp.pad(flat, (0, m * LANES - n)).reshape(m, LANES)


def _unpack_rows(packed, shapes):
    flat = packed.reshape(-1)
    out, o = [], 0
    for s in shapes:
        n = int(np.prod(s))
        out.append(flat[o:o + n].reshape(s))
        o += n
    return out


def kernel(x, meta_tokens, ffn1_norm, ffn1_w_gate, ffn1_w_up, ffn1_w_down, mix_norm, w_in, b_forget, fox_q_norm, fox_k_norm, swa_q_norm, swa_k_norm, swa_sinks, fox_out_norm, swa_out_norm, w_out, ffn2_norm, ffn2_w_gate, ffn2_w_up, ffn2_w_down, loss_target, m_meta_tokens, m_ffn1_norm, m_ffn1_w_gate, m_ffn1_w_up, m_ffn1_w_down, m_mix_norm, m_w_in, m_b_forget, m_fox_q_norm, m_fox_k_norm, m_swa_q_norm, m_swa_k_norm, m_swa_sinks, m_fox_out_norm, m_swa_out_norm, m_w_out, m_ffn2_norm, m_ffn2_w_gate, m_ffn2_w_up, m_ffn2_w_down, v_meta_tokens, v_ffn1_norm, v_ffn1_w_gate, v_ffn1_w_up, v_ffn1_w_down, v_mix_norm, v_w_in, v_b_forget, v_fox_q_norm, v_fox_k_norm, v_swa_q_norm, v_swa_k_norm, v_swa_sinks, v_fox_out_norm, v_swa_out_norm, v_w_out, v_ffn2_norm, v_ffn2_w_gate, v_ffn2_w_up, v_ffn2_w_down):
    W = dict(meta_tokens=meta_tokens, ffn1_norm=ffn1_norm, ffn1_w_gate=ffn1_w_gate, ffn1_w_up=ffn1_w_up, ffn1_w_down=ffn1_w_down, mix_norm=mix_norm, w_in=w_in, b_forget=b_forget, fox_q_norm=fox_q_norm, fox_k_norm=fox_k_norm, swa_q_norm=swa_q_norm, swa_k_norm=swa_k_norm, swa_sinks=swa_sinks, fox_out_norm=fox_out_norm, swa_out_norm=swa_out_norm, w_out=w_out, ffn2_norm=ffn2_norm, ffn2_w_gate=ffn2_w_gate, ffn2_w_up=ffn2_w_up, ffn2_w_down=ffn2_w_down)
    Mo = dict(meta_tokens=m_meta_tokens, ffn1_norm=m_ffn1_norm, ffn1_w_gate=m_ffn1_w_gate, ffn1_w_up=m_ffn1_w_up, ffn1_w_down=m_ffn1_w_down, mix_norm=m_mix_norm, w_in=m_w_in, b_forget=m_b_forget, fox_q_norm=m_fox_q_norm, fox_k_norm=m_fox_k_norm, swa_q_norm=m_swa_q_norm, swa_k_norm=m_swa_k_norm, swa_sinks=m_swa_sinks, fox_out_norm=m_fox_out_norm, swa_out_norm=m_swa_out_norm, w_out=m_w_out, ffn2_norm=m_ffn2_norm, ffn2_w_gate=m_ffn2_w_gate, ffn2_w_up=m_ffn2_w_up, ffn2_w_down=m_ffn2_w_down)
    Vo = dict(meta_tokens=v_meta_tokens, ffn1_norm=v_ffn1_norm, ffn1_w_gate=v_ffn1_w_gate, ffn1_w_up=v_ffn1_w_up, ffn1_w_down=v_ffn1_w_down, mix_norm=v_mix_norm, w_in=v_w_in, b_forget=v_b_forget, fox_q_norm=v_fox_q_norm, fox_k_norm=v_fox_k_norm, swa_q_norm=v_swa_q_norm, swa_k_norm=v_swa_k_norm, swa_sinks=v_swa_sinks, fox_out_norm=v_fox_out_norm, swa_out_norm=v_swa_out_norm, w_out=v_w_out, ffn2_norm=v_ffn2_norm, ffn2_w_gate=v_ffn2_w_gate, ffn2_w_up=v_ffn2_w_up, ffn2_w_down=v_ffn2_w_down)

    _, S, D = x.shape
    L = S + BLOCK
    depth = ffn1_norm.shape[0]
    md = _MixDims(D)
    mx, my, mc = _mesh_pos()
    k_idx = (2 * mx + my).astype(jnp.int32).reshape(1)
    c_idx = mc.astype(jnp.int32).reshape(1)
    dcols = D // NCH

    meta_all = _allgather_devices(meta_tokens.reshape(-1, LANES), name="meta_allgather")
    meta_full = jnp.transpose(meta_all[0::2].reshape(NCH, N_META, dcols), (1, 0, 2)).reshape(N_META, D)

    wts = []
    for l in range(depth):
        bf = lambda name: W[name][l].astype(BF16)
        g1, u1, d1 = _allgather_chips([bf('ffn1_w_gate'), bf('ffn1_w_up'), bf('ffn1_w_down')], name="ffn_allgather")
        wi, wo = _allgather_chips([bf('w_in'), bf('w_out')], name="mix_allgather")
        g2, u2, d2 = _allgather_chips([bf('ffn2_w_gate'), bf('ffn2_w_up'), bf('ffn2_w_down')], name="ffn_allgather")
        wi = _win_to_mine(jnp.transpose(wi, (1, 0, 2)).reshape(D, NCH * wi.shape[2]), md)
        wts.append(dict(g1=g1, u1=u1, d1=d1.reshape(-1, D), wi=wi, wo=wo.reshape(-1, D), g2=g2, u2=u2, d2=d2.reshape(-1, D)))

    h = jnp.concatenate([jnp.zeros((PAD, D), F32), meta_full, x[0]], axis=0)
    saved = []
    for l in range(depth):
        wl = wts[l]
        sp = _mix_small({n: W[n][l] for n in _MIX_SMALL}, md)
        h, s1 = _ffn_fwd(h, ffn1_norm[l].reshape(1, D), wl['g1'], wl['u1'], wl['d1'], "ffn")
        h, s2 = _mix_fwd(h, sp, wl['wi'], wl['wo'], md)
        h, s3 = _ffn_fwd(h, ffn2_norm[l].reshape(1, D), wl['g2'], wl['u2'], wl['d2'], "ffn")
        saved.append((s1, s2, s3, sp))

    loss_part, dh = _loss_grad(h, loss_target[0], name="loss_grad")

    small_grads = {n: [None] * depth for n in _SMALL}
    stacks = {n: None for n in _BIG}

    def update(name, l, grad):
        stacks[name] = _adamw_layer(W[name], Mo[name], Vo[name], grad, jnp.full((1,), l, jnp.int32), stacks[name],
                                    name="adamw_layer")

    for l in range(depth - 1, -1, -1):
        wl = wts[l]
        s1, s2, s3, sp = saved[l]
        dh, dg, dwg, dwu, dwd = _ffn_bwd(dh, s3, ffn2_norm[l].reshape(1, D), wl['g2'], wl['u2'], wl['d2'], "ffn")
        small_grads['ffn2_norm'][l] = dg.reshape(-1)
        rg, ru, rd = _reduce_scatter([dwg, dwu, dwd.reshape(NCH, -1, D)], c_idx, k_idx, "ffn")
        update('ffn2_w_gate', l, rg)
        update('ffn2_w_up', l, ru)
        update('ffn2_w_down', l, rd)

        dh, dwi, dwo, sm = _mix_bwd(dh, s2, sp, wl['wi'], wl['wo'], md)
        for n in _MIX_SMALL:
            small_grads[n][l] = sm[n]
        dwi = _win_grad_to_ref(dwi, md)
        dwi = jnp.transpose(dwi.reshape(D, NCH, -1), (1, 0, 2))
        ri, ro = _reduce_scatter([dwi, dwo.reshape(NCH, -1, D)], c_idx, k_idx, "mix")
        update('w_in', l, ri)
        update('w_out', l, ro)

        dh, dg, dwg, dwu, dwd = _ffn_bwd(dh, s1, ffn1_norm[l].reshape(1, D), wl['g1'], wl['u1'], wl['d1'], "ffn")
        small_grads['ffn1_norm'][l] = dg.reshape(-1)
        rg, ru, rd = _reduce_scatter([dwg, dwu, dwd.reshape(NCH, -1, D)], c_idx, k_idx, "ffn")
        update('ffn1_w_gate', l, rg)
        update('ffn1_w_up', l, ru)
        update('ffn1_w_down', l, rd)

    grad_x = dh[BLOCK:][None]

    small_shapes = [W[n].shape for n in _SMALL]
    parts = [jnp.stack(small_grads[n]) for n in _SMALL] + [dh[PAD:BLOCK], loss_part[0, :1]]
    packed = _pack_rows(parts)
    total = _sum_devices(_allgather_devices(packed, name="small_allgather"), name="small_sum")
    *g_small, g_meta, loss = _unpack_rows(total, small_shapes + [(N_META, D), (1,)])
    g_meta = lax.dynamic_slice(g_meta, (0, k_idx[0] * dcols), (N_META, dcols))

    sw = _pack_rows([W[n] for n in _SMALL])
    sd, smm, svv = _adamw_flat(sw, _pack_rows(g_small), _pack_rows([Mo[n] for n in _SMALL]),
                               _pack_rows([Vo[n] for n in _SMALL]), name="adamw_small")
    d_small, m_small, v_small = (_unpack_rows(t, small_shapes) for t in (sd, smm, svv))
    d_meta, m_meta, v_meta = _adamw_flat(meta_tokens, g_meta, m_meta_tokens, v_meta_tokens, name="adamw_meta")

    grads, deltas, new_m, new_v = {}, {}, {}, {}
    for n in _BIG:
        grads[n], deltas[n], new_m[n], new_v[n] = stacks[n]
    for i, n in enumerate(_SMALL):
        grads[n], deltas[n], new_m[n], new_v[n] = g_small[i], d_small[i], m_small[i], v_small[i]
    grads['meta_tokens'], deltas['meta_tokens'], new_m['meta_tokens'], new_v['meta_tokens'] = g_meta, d_meta, m_meta, v_meta
    return (loss.reshape(()), grad_x, *[grads[n] for n in _WEIGHTS], *[deltas[n] for n in _WEIGHTS],
            *[new_m[n] for n in _WEIGHTS], *[new_v[n] for n in _WEIGHTS])
```

```python
import numpy as np
import jax
import jax.numpy as jnp
from jax import lax
from jax.experimental import pallas as pl
from jax.experimental.pallas import tpu as pltpu

F32 = jnp.float32
BF16 = jnp.bfloat16

HEAD_DIM = 64
N_META = 16
BLOCK = 128
WINDOW = 128
PAD = BLOCK - N_META
EPS = 1e-6
NEG_INF = -1e30
SWA_GROUP = 8
NCH = 4
LANES = 128
QBLOCK = 512

ADAM_LR = 0.001
ADAM_B1 = 0.9
ADAM_B2 = 0.999
ADAM_EPS = 1e-08
ADAM_WD = 0.01
ADAM_STEP = 10

V7X_VMEM_BYTES = 64 * 1024 * 1024
VMEM_LIMIT = V7X_VMEM_BYTES - 8 * 1024 * 1024
MESH = pl.DeviceIdType.MESH
HIGHEST = lax.Precision.HIGHEST

_TM = (1088, 1024, 704, 512, 384, 256, 128)
_TN = (1408, 1024, 768, 512, 384, 256, 128)
_TK = (2176, 1408, 1024, 512, 384, 256, 128)
_TR = (544, 512, 384, 272, 256, 128)


def _pick(n, cands):
    for c in cands:
        if n % c == 0:
            return c
    return n


def _cparams(*sem):
    return pltpu.CompilerParams(dimension_semantics=sem if sem else None, vmem_limit_bytes=VMEM_LIMIT)


def _matmul(a, b, *, name, nt=False, b_chunked=False, out_chunked=False, out_dtype=F32,
            residual=None, scale=1.0):
    M, K = a.shape
    if not nt:
        N = b.shape[-1] * (NCH if b_chunked else 1)
        assert b.shape[-2] == K
        k_unit = K
    else:
        N = b.shape[-2]
        k_unit = b.shape[-1]
        assert k_unit * (NCH if b_chunked else 1) == K
    n_unit = N // NCH if (out_chunked or (b_chunked and not nt)) else N
    tm, tn, tk = _pick(M, _TM), _pick(n_unit, _TN), _pick(k_unit, _TK)
    osz = jnp.dtype(out_dtype).itemsize

    def est(tm_):
        return (2 * tm_ * tk * 2 + 2 * tk * tn * 2 + tm_ * tn * 4 + 2 * tm_ * tn * osz
                + (2 * tm_ * tn * 4 if residual is not None else 0))

    while est(tm) > VMEM_LIMIT * 3 // 4 and tm % 32 == 0:
        tm //= 2
    npc, kpc = n_unit // tn, k_unit // tk
    nk = K // tk
    grid = (M // tm, N // tn, nk)

    a_spec = pl.BlockSpec((tm, tk), lambda i, j, k: (i, k))
    if not nt:
        if b_chunked:
            b_spec = pl.BlockSpec((None, tk, tn), lambda i, j, k: (j // npc, k, j % npc))
        else:
            b_spec = pl.BlockSpec((tk, tn), lambda i, j, k: (k, j))
        dims = (((1,), (0,)), ((), ()))
    else:
        if b_chunked:
            b_spec = pl.BlockSpec((None, tn, tk), lambda i, j, k: (k // kpc, j, k % kpc))
        else:
            b_spec = pl.BlockSpec((tn, tk), lambda i, j, k: (j, k))
        dims = (((1,), (1,)), ((), ()))
    if out_chunked:
        o_spec = pl.BlockSpec((None, tm, tn), lambda i, j, k: (j // npc, i, j % npc))
        out_shape = jax.ShapeDtypeStruct((NCH, M, n_unit), out_dtype)
    else:
        o_spec = pl.BlockSpec((tm, tn), lambda i, j, k: (i, j))
        out_shape = jax.ShapeDtypeStruct((M, N), out_dtype)
    in_specs = [a_spec, b_spec]
    args = [a, b]
    if residual is not None:
        assert not out_chunked
        in_specs.append(pl.BlockSpec((tm, tn), lambda i, j, k: (i, j)))
        args.append(residual)

    def body(*refs):
        if residual is not None:
            a_ref, b_ref, r_ref, o_ref, acc_ref = refs
        else:
            a_ref, b_ref, o_ref, acc_ref = refs
        k = pl.program_id(2)

        @pl.when(k == 0)
        def _():
            acc_ref[...] = jnp.zeros_like(acc_ref)

        acc_ref[...] += lax.dot_general(a_ref[...], b_ref[...], dims, preferred_element_type=F32)

        @pl.when(k == nk - 1)
        def _():
            r = acc_ref[...]
            if scale != 1.0:
                r = r * scale
            if residual is not None:
                r = r + r_ref[...]
            o_ref[...] = r.astype(o_ref.dtype)

    return pl.pallas_call(
        body, name=name, grid=grid, in_specs=in_specs, out_specs=o_spec, out_shape=out_shape,
        scratch_shapes=[pltpu.VMEM((tm, tn), F32)],
        compiler_params=_cparams("parallel", "parallel", "arbitrary"),
    )(*args)


def _transpose(x, *, name):
    M, N = x.shape
    tc = _pick(N, (512, 384, 256, 128))

    def body(x_ref, o_ref):
        o_ref[...] = x_ref[...].astype(F32).T.astype(o_ref.dtype)

    return pl.pallas_call(
        body, name=name, grid=(N // tc,),
        in_specs=[pl.BlockSpec((M, tc), lambda j: (0, j))],
        out_specs=pl.BlockSpec((tc, M), lambda j: (j, 0)),
        out_shape=jax.ShapeDtypeStruct((N, M), x.dtype),
        compiler_params=_cparams("parallel"),
    )(x)


def _rms_fwd(h, g, *, name):
    L, D = h.shape
    tr = _pick(L, _TR)

    def body(h_ref, g_ref, o_ref):
        x = h_ref[...]
        r = lax.rsqrt(jnp.mean(x * x, axis=-1, keepdims=True) + EPS)
        o_ref[...] = (x * r * g_ref[...]).astype(o_ref.dtype)

    return pl.pallas_call(
        body, name=name, grid=(L // tr,),
        in_specs=[pl.BlockSpec((tr, D), lambda i: (i, 0)), pl.BlockSpec((1, D), lambda i: (0, 0))],
        out_specs=pl.BlockSpec((tr, D), lambda i: (i, 0)),
        out_shape=jax.ShapeDtypeStruct((L, D), BF16),
        compiler_params=_cparams("parallel"),
    )(h, g)


def _rms_bwd(dy, h, g, dh, *, name):
    L, D = h.shape
    tr = _pick(L, _TR)

    def body(dy_ref, h_ref, g_ref, dh_ref, o_ref, dg_ref):
        i = pl.program_id(0)
        x = h_ref[...]
        dyv = dy_ref[...]
        r = lax.rsqrt(jnp.mean(x * x, axis=-1, keepdims=True) + EPS)
        xh = x * r
        dxh = dyv * g_ref[...]
        dx = r * (dxh - xh * jnp.mean(dxh * xh, axis=-1, keepdims=True))
        o_ref[...] = dh_ref[...] + dx
        part = jnp.sum(dyv * xh, axis=0, keepdims=True)

        @pl.when(i == 0)
        def _():
            dg_ref[...] = part

        @pl.when(i > 0)
        def _():
            dg_ref[...] += part

    row = pl.BlockSpec((tr, D), lambda i: (i, 0))
    vec = pl.BlockSpec((1, D), lambda i: (0, 0))
    return pl.pallas_call(
        body, name=name, grid=(L // tr,),
        in_specs=[row, row, vec, row], out_specs=[row, vec],
        out_shape=[jax.ShapeDtypeStruct((L, D), F32), jax.ShapeDtypeStruct((1, D), F32)],
        compiler_params=_cparams("arbitrary"),
    )(dy, h, g, dh)


def _swiglu_fwd(gate, up, *, name):
    L, F = gate.shape
    tr, tc = _pick(L, _TR), _pick(F, _TN)

    def body(g_ref, u_ref, o_ref):
        g = g_ref[...].astype(F32)
        o_ref[...] = (g * jax.nn.sigmoid(g) * u_ref[...].astype(F32)).astype(o_ref.dtype)

    blk = pl.BlockSpec((tr, tc), lambda i, j: (i, j))
    return pl.pallas_call(
        body, name=name, grid=(L // tr, F // tc), in_specs=[blk, blk], out_specs=blk,
        out_shape=jax.ShapeDtypeStruct((L, F), BF16),
        compiler_params=_cparams("parallel", "parallel"),
    )(gate, up)


def _swiglu_bwd(dact, gate, up, *, name):
    L, F = gate.shape
    tr, tc = _pick(L, _TR), _pick(F, _TN)

    def body(d_ref, g_ref, u_ref, dg_ref, du_ref):
        d = d_ref[...].astype(F32)
        g = g_ref[...].astype(F32)
        u = u_ref[...].astype(F32)
        sg = jax.nn.sigmoid(g)
        du_ref[...] = (d * g * sg).astype(du_ref.dtype)
        dg_ref[...] = (d * u * sg * (1.0 + g * (1.0 - sg))).astype(dg_ref.dtype)

    blk = pl.BlockSpec((tr, tc), lambda i, j: (i, j))
    return pl.pallas_call(
        body, name=name, grid=(L // tr, F // tc), in_specs=[blk, blk, blk], out_specs=[blk, blk],
        out_shape=[jax.ShapeDtypeStruct((L, F), BF16)] * 2,
        compiler_params=_cparams("parallel", "parallel"),
    )(dact, gate, up)


def _cast_bf16(x, *, name):
    L, D = x.shape
    tr = _pick(L, _TR)

    def body(x_ref, o_ref):
        o_ref[...] = x_ref[...].astype(o_ref.dtype)

    blk = pl.BlockSpec((tr, D), lambda i: (i, 0))
    return pl.pallas_call(
        body, name=name, grid=(L // tr,), in_specs=[blk], out_specs=blk,
        out_shape=jax.ShapeDtypeStruct((L, D), BF16), compiler_params=_cparams("parallel"),
    )(x)


def _loss_grad(h, target, *, name):
    L, D = h.shape
    S = target.shape[0]
    nb = L // BLOCK

    def body(h_ref, t_ref, loss_ref, dh_ref):
        i = pl.program_id(0)

        @pl.when(i == 0)
        def _():
            loss_ref[...] = jnp.zeros_like(loss_ref)
            dh_ref[...] = jnp.zeros_like(dh_ref)

        @pl.when(i > 0)
        def _():
            err = h_ref[...] - t_ref[...]
            dh_ref[...] = err * (1.0 / D)
            loss_ref[...] += jnp.full(loss_ref.shape, (0.5 / D) * jnp.sum(err * err), F32)

    return pl.pallas_call(
        body, name=name, grid=(nb,),
        in_specs=[pl.BlockSpec((BLOCK, D), lambda i: (i, 0)),
                  pl.BlockSpec((BLOCK, D), lambda i: (jnp.maximum(i - 1, 0), 0))],
        out_specs=[pl.BlockSpec((1, LANES), lambda i: (0, 0)), pl.BlockSpec((BLOCK, D), lambda i: (i, 0))],
        out_shape=[jax.ShapeDtypeStruct((1, LANES), F32), jax.ShapeDtypeStruct((L, D), F32)],
        compiler_params=_cparams("arbitrary"),
    )(h, target)


def _ffn_fwd(h, g, wg, wu, wd, tag):
    hn = _rms_fwd(h, g, name=f"{tag}_rms")
    gate = _matmul(hn, wg, name=f"{tag}_gate", b_chunked=True, out_dtype=BF16)
    up = _matmul(hn, wu, name=f"{tag}_up", b_chunked=True, out_dtype=BF16)
    act = _swiglu_fwd(gate, up, name=f"{tag}_act")
    h_out = _matmul(act, wd, name=f"{tag}_down", residual=h, scale=0.5)
    return h_out, (h, hn, gate, up, act)


def _ffn_bwd(dh, saved, g, wg, wu, wd, tag):
    h, hn, gate, up, act = saved
    dout = _cast_bf16(dh, name=f"{tag}_dout")
    dact = _matmul(dout, wd, name=f"{tag}_dact", nt=True, out_dtype=BF16, scale=0.5)
    actT = _transpose(act, name=f"{tag}_actT")
    dwd = _matmul(actT, dout, name=f"{tag}_dwd", out_dtype=BF16, scale=0.5)
    dgate, dup = _swiglu_bwd(dact, gate, up, name=f"{tag}_dswiglu")
    hnT = _transpose(hn, name=f"{tag}_hnT")
    dwg = _matmul(hnT, dgate, name=f"{tag}_dwg", out_chunked=True, out_dtype=BF16)
    dwu = _matmul(hnT, dup, name=f"{tag}_dwu", out_chunked=True, out_dtype=BF16)
    dhn = _matmul(dgate, wg, name=f"{tag}_dhn_g", nt=True, b_chunked=True)
    dhn = _matmul(dup, wu, name=f"{tag}_dhn_u", nt=True, b_chunked=True, residual=dhn)
    dh_in, dg = _rms_bwd(dhn, h, g, dh, name=f"{tag}_drms")
    return dh_in, dg, dwg, dwu, dwd


class _MixDims:
    def __init__(self, d_model):
        self.wf = d_model // 2
        self.ws = d_model // 2
        self.pf = self.wf // LANES
        self.ps = self.ws // LANES
        self.hf = self.wf // HEAD_DIM
        self.hq = self.ws // HEAD_DIM
        self.nkv = max(1, self.hq // SWA_GROUP)
        self.g = self.hq // self.nkv
        self.bq_f, self.bk_f, self.bv_f = 0, self.pf, 2 * self.pf
        self.bq_s = 3 * self.pf
        self.bk_s = self.bq_s + self.ps
        self.bv_s = self.bk_s + self.nkv
        self.bz = self.bv_s + self.nkv
        self.nu = (self.bz + 1) * LANES
        self.nup = -(-self.nu // 512) * 512
        self.in_width = 3 * self.wf + self.hf + self.ws + 2 * self.nkv * HEAD_DIM
        assert self.hf <= 2 * (LANES // 8)

    def gate_lane(self, h):
        return 8 * (h // 2) + h % 2

    def column_map(self):
        wf, ws, hd = self.wf, self.ws, HEAD_DIM
        src = np.full((self.nup,), -1, np.int64)
        src[0:3 * wf] = np.arange(3 * wf)
        o_sq = 3 * wf + self.hf
        src[self.bq_s * LANES:self.bq_s * LANES + ws] = o_sq + np.arange(ws)
        o_sk = o_sq + ws
        o_sv = o_sk + self.nkv * hd
        for kv in range(self.nkv):
            for rep in range(2):
                c0 = (self.bk_s + kv) * LANES + rep * hd
                src[c0:c0 + hd] = o_sk + kv * hd + np.arange(hd)
                c0 = (self.bv_s + kv) * LANES + rep * hd
                src[c0:c0 + hd] = o_sv + kv * hd + np.arange(hd)
        for h in range(self.hf):
            src[self.bz * LANES + self.gate_lane(h)] = 3 * wf + h
        return src

    def grad_column_map(self):
        src = self.column_map()
        dst = np.zeros((self.in_width,), np.int64)
        for col in range(self.nup - 1, -1, -1):
            if src[col] >= 0:
                dst[src[col]] = col
        return dst


def _block_diag_mean():
    m = np.zeros((LANES, LANES), np.float32)
    m[:HEAD_DIM, :HEAD_DIM] = 1.0 / HEAD_DIM
    m[HEAD_DIM:, HEAD_DIM:] = 1.0 / HEAD_DIM
    return jnp.asarray(m)


def _fold_halves():
    m = np.eye(LANES, dtype=np.float32)
    m[np.arange(LANES), (np.arange(LANES) + HEAD_DIM) % LANES] = 1.0
    return jnp.asarray(m)


def _gate_expand(md):
    e = np.zeros((LANES, md.wf), np.float32)
    for h in range(md.hf):
        e[md.gate_lane(h), h * HEAD_DIM:(h + 1) * HEAD_DIM] = 1.0
    return jnp.asarray(e)


def _qblocks(L, qb):
    blocks = [(0, BLOCK)]
    r = BLOCK
    while r < L:
        blocks.append((r, qb))
        r += qb
    assert r == L
    return blocks


def _f32dot(a, b):
    return jnp.dot(a, b, precision=HIGHEST, preferred_element_type=F32)


_DIMS_NT = (((1,), (1,)), ((), ()))


def _dot_nt(a, b):
    return lax.dot_general(a, b, _DIMS_NT, preferred_element_type=F32)


def _dot_tn(a, b):
    return jnp.dot(a.T.astype(BF16), b, preferred_element_type=F32)


def _log_sigmoid(z):
    return jnp.minimum(z, 0.0) - jnp.log(1.0 + jnp.exp(-jnp.abs(z)))


def _gate_fwd(u, b, md, *, name):
    L = u.shape[0]
    nb = L // BLOCK
    expand = _gate_expand(md)

    def body(z_ref, b_ref, e_ref, cexp_ref, ct_ref, c_s):
        ri = lax.broadcasted_iota(jnp.int32, (BLOCK, BLOCK), 0)
        ci = lax.broadcasted_iota(jnp.int32, (BLOCK, BLOCK), 1)
        tri = (ri >= ci).astype(F32)
        carry = jnp.zeros((1, LANES), F32)
        for bi in range(nb):
            rows = pl.ds(bi * BLOCK, BLOCK)
            logf = _log_sigmoid(z_ref[rows, :] + b_ref[...])
            blk = _f32dot(tri, logf) + carry
            c_s[rows, :] = blk
            carry = blk[BLOCK - 1:BLOCK, :]
        c = c_s[...]
        ct_ref[...] = c.T
        cexp_ref[...] = _f32dot(c, e_ref[...])

    return pl.pallas_call(
        body, name=name, grid=(1,),
        in_specs=[pl.BlockSpec((L, LANES), lambda i: (0, md.bz)), pl.BlockSpec((1, LANES), lambda i: (0, 0)),
                  pl.BlockSpec((LANES, md.wf), lambda i: (0, 0))],
        out_specs=[pl.BlockSpec((L, md.wf), lambda i: (0, 0)), pl.BlockSpec((LANES, L), lambda i: (0, 0))],
        out_shape=[jax.ShapeDtypeStruct((L, md.wf), F32), jax.ShapeDtypeStruct((LANES, L), F32)],
        scratch_shapes=[pltpu.VMEM((L, LANES), F32)],
        compiler_params=_cparams("arbitrary"),
    )(u, b, expand)


def _gate_bwd(u, b, dck_t, md, *, name):
    L = u.shape[0]
    nb = L // BLOCK

    def body(z_ref, b_ref, dck_ref, dz_ref, db_ref, dc_s):
        ri = lax.broadcasted_iota(jnp.int32, (BLOCK, BLOCK), 0)
        ci = lax.broadcasted_iota(jnp.int32, (BLOCK, BLOCK), 1)
        triu = (ri <= ci).astype(F32)
        dc_s[...] = -dck_ref[...].T
        carry = jnp.zeros((1, LANES), F32)
        db = jnp.zeros((1, LANES), F32)
        for bi in range(nb - 1, -1, -1):
            rows = pl.ds(bi * BLOCK, BLOCK)
            blk = _f32dot(triu, dc_s[rows, :]) + carry
            carry = blk[0:1, :]
            z = z_ref[rows, :] + b_ref[...]
            dz = blk * jax.nn.sigmoid(-z)
            if bi == 0:
                dz = jnp.where(lax.broadcasted_iota(jnp.int32, (BLOCK, LANES), 0) >= PAD, dz, 0.0)
            dz_ref[rows, :] = dz
            db = db + jnp.sum(dz, axis=0, keepdims=True)
        db_ref[...] = db

    return pl.pallas_call(
        body, name=name, grid=(1,),
        in_specs=[pl.BlockSpec((L, LANES), lambda i: (0, md.bz)), pl.BlockSpec((1, LANES), lambda i: (0, 0)),
                  pl.BlockSpec((LANES, L), lambda i: (0, 0))],
        out_specs=[pl.BlockSpec((L, LANES), lambda i: (0, 0)), pl.BlockSpec((1, LANES), lambda i: (0, 0))],
        out_shape=[jax.ShapeDtypeStruct((L, LANES), F32), jax.ShapeDtypeStruct((1, LANES), F32)],
        scratch_shapes=[pltpu.VMEM((L, LANES), F32)],
        compiler_params=_cparams("arbitrary"),
    )(u, b, dck_t)


def _head_norm(x, g, bd):
    r = lax.rsqrt(_f32dot(x * x, bd) + EPS)
    xh = x * r
    return xh * g, xh, r


def _head_norm_bwd(dy, xh, r, g, bd):
    dxh = dy * g
    dx = r * (dxh - xh * _f32dot(dxh * xh, bd))
    return dx, jnp.sum(dy * xh, axis=0, keepdims=True)


def _lane_half():
    return lax.broadcasted_iota(jnp.int32, (1, LANES), 1) < HEAD_DIM


def _store_head_pair(x, half, a_s, b_s):
    a_s[...] = jnp.where(half, x, 0.0).astype(BF16)
    b_s[...] = jnp.where(half, 0.0, x).astype(BF16)


def _fox_scores(qm, kn_s, cexp_ref, ct_ref, r0, nr, klen, hh):
    s = _dot_nt(qm, kn_s[0:klen, :]) * (HEAD_DIM ** -0.5)
    s = s + cexp_ref[r0:r0 + nr, HEAD_DIM * hh:HEAD_DIM * hh + 1] - ct_ref[hh:hh + 1, 0:klen]
    qp = r0 + lax.broadcasted_iota(jnp.int32, (nr, klen), 0)
    kp = lax.broadcasted_iota(jnp.int32, (nr, klen), 1)
    return jnp.where((kp <= qp) & (kp >= PAD), s, NEG_INF)


def _fox_fwd(u, cexp, ct3, qg, kg, md, *, name):
    L = u.shape[0]
    blocks = _qblocks(L, QBLOCK)
    bd = _block_diag_mean()

    def body(q_ref, k_ref, v_ref, cexp_ref, ct_ref, qg_ref, kg_ref, bd_ref, o_ref, lse_ref, qa_s, qb_s, kn_s, v_s):
        half = _lane_half()
        bdv = bd_ref[...]
        _store_head_pair(_head_norm(q_ref[...], qg_ref[...], bdv)[0], half, qa_s, qb_s)
        kn_s[...] = _head_norm(k_ref[...], kg_ref[...], bdv)[0].astype(BF16)
        v_s[...] = v_ref[...].astype(BF16)
        for r0, nr in blocks:
            klen = r0 + nr
            o_blk = lse_blk = None
            for hh, q_s in enumerate((qa_s, qb_s)):
                s = _fox_scores(q_s[r0:r0 + nr, :], kn_s, cexp_ref, ct_ref, r0, nr, klen, hh)
                m = jnp.max(s, axis=-1, keepdims=True)
                p = jnp.exp(s - m)
                l = jnp.sum(p, axis=-1, keepdims=True)
                oh = jnp.dot(p.astype(BF16), v_s[0:klen, :], preferred_element_type=F32) * (1.0 / l)
                lh = jnp.broadcast_to(jnp.where(m > 0.5 * NEG_INF, m + jnp.log(l), 0.0), (nr, LANES))
                o_blk = oh if hh == 0 else jnp.where(half, o_blk, oh)
                lse_blk = lh if hh == 0 else jnp.where(half, lse_blk, lh)
            o_ref[r0:r0 + nr, :] = o_blk
            lse_ref[r0:r0 + nr, :] = lse_blk

    col = lambda base: pl.BlockSpec((L, LANES), lambda j: (0, base + j))
    vec = pl.BlockSpec((1, LANES), lambda j: (0, 0))
    return pl.pallas_call(
        body, name=name, grid=(md.pf,),
        in_specs=[col(md.bq_f), col(md.bk_f), col(md.bv_f), col(0),
                  pl.BlockSpec((None, 8, L), lambda j: (j, 0, 0)), vec, vec,
                  pl.BlockSpec((LANES, LANES), lambda j: (0, 0))],
        out_specs=[col(0), col(0)],
        out_shape=[jax.ShapeDtypeStruct((L, md.wf), F32)] * 2,
        scratch_shapes=[pltpu.VMEM((L, LANES), BF16)] * 4,
        compiler_params=_cparams("parallel"),
    )(u, u, u, cexp, ct3, qg, kg, bd)


def _softmax_bwd(p, dp):
    pdp = p * dp
    return pdp - p * jnp.sum(pdp, axis=-1, keepdims=True)


def _fox_bwd(u, cexp, ct3, qg, kg, do, lse, md, *, name):
    L = u.shape[0]
    blocks = _qblocks(L, QBLOCK // 2)
    bd = _block_diag_mean()
    fold = _fold_halves()
    npairs = md.pf

    def body(q_ref, k_ref, v_ref, cexp_ref, ct_ref, qg_ref, kg_ref, bd_ref, fold_ref, do_ref, lse_ref,
             dq_ref, dk_ref, dv_ref, dck_ref, dqg_ref, dkg_ref,
             qa_s, qb_s, kn_s, v_s, doa_s, dob_s, dqn_s, dkn_s, dvv_s):
        j = pl.program_id(0)
        half = _lane_half()
        bdv = bd_ref[...]
        qy, qh, rq = _head_norm(q_ref[...], qg_ref[...], bdv)
        ky, kh, rk = _head_norm(k_ref[...], kg_ref[...], bdv)
        _store_head_pair(qy, half, qa_s, qb_s)
        _store_head_pair(do_ref[...], half, doa_s, dob_s)
        kn_s[...] = ky.astype(BF16)
        v_s[...] = v_ref[...].astype(BF16)
        dkn_s[...] = jnp.zeros_like(dkn_s)
        dvv_s[...] = jnp.zeros_like(dvv_s)
        dck_ref[...] = jnp.zeros_like(dck_ref)
        for r0, nr in blocks:
            klen = r0 + nr
            dq_blk = None
            for hh, (q_s, do_s) in enumerate(((qa_s, doa_s), (qb_s, dob_s))):
                c0 = HEAD_DIM * hh
                qm = q_s[r0:r0 + nr, :]
                dom = do_s[r0:r0 + nr, :]
                s = _fox_scores(qm, kn_s, cexp_ref, ct_ref, r0, nr, klen, hh)
                p = jnp.exp(s - lse_ref[r0:r0 + nr, c0:c0 + 1])
                ds = _softmax_bwd(p, _dot_nt(dom, v_s[0:klen, :]))
                dck_ref[hh:hh + 1, 0:klen] += jnp.sum(ds, axis=0, keepdims=True)
                ds = ds * (HEAD_DIM ** -0.5)
                dq_h = jnp.dot(ds.astype(BF16), kn_s[0:klen, :], preferred_element_type=F32)
                dkn_s[0:klen, :] += _dot_tn(ds, qm)
                dvv_s[0:klen, :] += _dot_tn(p, dom)
                dq_blk = dq_h if hh == 0 else jnp.where(half, dq_blk, dq_h)
            dqn_s[r0:r0 + nr, :] = dq_blk
        dq, dqg = _head_norm_bwd(dqn_s[...], qh, rq, qg_ref[...], bdv)
        dk, dkg = _head_norm_bwd(dkn_s[...], kh, rk, kg_ref[...], bdv)
        dq_ref[...] = dq
        dk_ref[...] = dk
        dv_ref[...] = dvv_s[...]

        @pl.when(j == 0)
        def _():
            dqg_ref[...] = jnp.zeros_like(dqg_ref)
            dkg_ref[...] = jnp.zeros_like(dkg_ref)

        dqg_ref[...] += dqg
        dkg_ref[...] += dkg

        @pl.when(j == npairs - 1)
        def _():
            dqg_ref[...] = _f32dot(jnp.broadcast_to(dqg_ref[...], (8, LANES)), fold_ref[...])[0:1, :]
            dkg_ref[...] = _f32dot(jnp.broadcast_to(dkg_ref[...], (8, LANES)), fold_ref[...])[0:1, :]

    col = lambda base: pl.BlockSpec((L, LANES), lambda j: (0, base + j))
    vec = pl.BlockSpec((1, LANES), lambda j: (0, 0))
    sq = pl.BlockSpec((LANES, LANES), lambda j: (0, 0))
    ct_spec = pl.BlockSpec((None, 8, L), lambda j: (j, 0, 0))
    big = jax.ShapeDtypeStruct((L, md.wf), F32)
    small = jax.ShapeDtypeStruct((1, LANES), F32)
    return pl.pallas_call(
        body, name=name, grid=(md.pf,),
        in_specs=[col(md.bq_f), col(md.bk_f), col(md.bv_f), col(0), ct_spec, vec, vec, sq, sq, col(0), col(0)],
        out_specs=[col(0), col(0), col(0), ct_spec, vec, vec],
        out_shape=[big, big, big, jax.ShapeDtypeStruct((md.pf, 8, L), F32), small, small],
        scratch_shapes=[pltpu.VMEM((L, LANES), BF16)] * 6 + [pltpu.VMEM((L, LANES), F32)] * 3,
        compiler_params=_cparams("arbitrary"),
    )(u, u, u, cexp, ct3, qg, kg, bd, fold, do, lse)


def _swa_scores(qm, kn_s, slope, k0, r0, nr, klen):
    s = _dot_nt(qm, kn_s[k0:k0 + klen, :]) * (HEAD_DIM ** -0.5)
    qp = r0 + lax.broadcasted_iota(jnp.int32, (nr, klen), 0)
    kp = k0 + lax.broadcasted_iota(jnp.int32, (nr, klen), 1)
    dist = qp - kp
    s = s - slope * dist.astype(F32)
    return jnp.where((dist >= 0) & (dist < WINDOW) & (kp >= PAD), s, NEG_INF)


def _swa_fwd(u, sinkp, slopep, qg, kg, md, *, name):
    L = u.shape[0]
    blocks = _qblocks(L, QBLOCK)
    bd = _block_diag_mean()

    def body(q_ref, k_ref, v_ref, sink_ref, slope_ref, qg_ref, kg_ref, bd_ref, o_ref, lse_ref, qa_s, qb_s, kn_s, v_s):
        half = _lane_half()
        bdv = bd_ref[...]
        _store_head_pair(_head_norm(q_ref[...], qg_ref[...], bdv)[0], half, qa_s, qb_s)
        kn_s[...] = _head_norm(k_ref[...], kg_ref[...], bdv)[0].astype(BF16)
        v_s[...] = v_ref[...].astype(BF16)
        for r0, nr in blocks:
            k0 = max(r0 - BLOCK, 0)
            klen = r0 + nr - k0
            o_blk = lse_blk = None
            for hh, q_s in enumerate((qa_s, qb_s)):
                c0 = HEAD_DIM * hh
                s = _swa_scores(q_s[r0:r0 + nr, :], kn_s, slope_ref[0:1, c0:c0 + 1], k0, r0, nr, klen)
                sink = sink_ref[0:1, c0:c0 + 1]
                m = jnp.maximum(jnp.max(s, axis=-1, keepdims=True), sink)
                p = jnp.exp(s - m)
                den = jnp.sum(p, axis=-1, keepdims=True) + jnp.exp(sink - m)
                oh = jnp.dot(p.astype(BF16), v_s[k0:k0 + klen, :], preferred_element_type=F32) * (1.0 / den)
                lh = jnp.broadcast_to(m + jnp.log(den), (nr, LANES))
                o_blk = oh if hh == 0 else jnp.where(half, o_blk, oh)
                lse_blk = lh if hh == 0 else jnp.where(half, lse_blk, lh)
            o_ref[r0:r0 + nr, :] = o_blk
            lse_ref[r0:r0 + nr, :] = lse_blk

    g2 = md.g // 2
    qcol = pl.BlockSpec((L, LANES), lambda j: (0, md.bq_s + j))
    kcol = pl.BlockSpec((L, LANES), lambda j: (0, md.bk_s + j // g2))
    vcol = pl.BlockSpec((L, LANES), lambda j: (0, md.bv_s + j // g2))
    ocol = pl.BlockSpec((L, LANES), lambda j: (0, j))
    pvec = pl.BlockSpec((1, LANES), lambda j: (0, j))
    vec = pl.BlockSpec((1, LANES), lambda j: (0, 0))
    return pl.pallas_call(
        body, name=name, grid=(md.ps,),
        in_specs=[qcol, kcol, vcol, pvec, pvec, vec, vec, pl.BlockSpec((LANES, LANES), lambda j: (0, 0))],
        out_specs=[ocol, ocol],
        out_shape=[jax.ShapeDtypeStruct((L, md.ws), F32)] * 2,
        scratch_shapes=[pltpu.VMEM((L, LANES), BF16)] * 4,
        compiler_params=_cparams("parallel"),
    )(u, u, u, sinkp, slopep, qg, kg, bd)


def _swa_bwd(u, sinkp, slopep, qg, kg, do, lse, md, *, name):
    L = u.shape[0]
    blocks = _qblocks(L, QBLOCK // 2)
    bd = _block_diag_mean()
    fold = _fold_halves()
    g2 = md.g // 2
    npairs = md.ps

    def body(q_ref, k_ref, v_ref, sink_ref, slope_ref, qg_ref, kg_ref, bd_ref, fold_ref, do_ref, lse_ref,
             dq_ref, dk_ref, dv_ref, dsink_ref, dqg_ref, dkg_ref,
             qa_s, qb_s, kn_s, v_s, doa_s, dob_s, dqn_s):
        j = pl.program_id(0)
        half = _lane_half()
        bdv = bd_ref[...]
        qy, qh, rq = _head_norm(q_ref[...], qg_ref[...], bdv)
        ky, kh, rk = _head_norm(k_ref[...], kg_ref[...], bdv)
        _store_head_pair(qy, half, qa_s, qb_s)
        _store_head_pair(do_ref[...], half, doa_s, dob_s)
        kn_s[...] = ky.astype(BF16)
        v_s[...] = v_ref[...].astype(BF16)

        @pl.when(j % g2 == 0)
        def _():
            dk_ref[...] = jnp.zeros_like(dk_ref)
            dv_ref[...] = jnp.zeros_like(dv_ref)

        @pl.when(j == 0)
        def _():
            dqg_ref[...] = jnp.zeros_like(dqg_ref)
            dkg_ref[...] = jnp.zeros_like(dkg_ref)

        dsink = [jnp.zeros((1, 1), F32), jnp.zeros((1, 1), F32)]
        for r0, nr in blocks:
            k0 = max(r0 - BLOCK, 0)
            klen = r0 + nr - k0
            dq_blk = None
            for hh, (q_s, do_s) in enumerate(((qa_s, doa_s), (qb_s, dob_s))):
                c0 = HEAD_DIM * hh
                qm = q_s[r0:r0 + nr, :]
                dom = do_s[r0:r0 + nr, :]
                s = _swa_scores(qm, kn_s, slope_ref[0:1, c0:c0 + 1], k0, r0, nr, klen)
                lse_h = lse_ref[r0:r0 + nr, c0:c0 + 1]
                p = jnp.exp(s - lse_h)
                pdp = p * _dot_nt(dom, v_s[k0:k0 + klen, :])
                delta = jnp.sum(pdp, axis=-1, keepdims=True)
                p_sink = jnp.exp(sink_ref[0:1, c0:c0 + 1] - lse_h)
                dsink[hh] = dsink[hh] - jnp.sum(p_sink * delta, axis=0, keepdims=True)
                ds = (pdp - p * delta) * (HEAD_DIM ** -0.5)
                dq_h = jnp.dot(ds.astype(BF16), kn_s[k0:k0 + klen, :], preferred_element_type=F32)
                dk_ref[k0:k0 + klen, :] += _dot_tn(ds, qm)
                dv_ref[k0:k0 + klen, :] += _dot_tn(p, dom)
                dq_blk = dq_h if hh == 0 else jnp.where(half, dq_blk, dq_h)
            dqn_s[r0:r0 + nr, :] = dq_blk
        dq, dqg = _head_norm_bwd(dqn_s[...], qh, rq, qg_ref[...], bdv)
        dq_ref[...] = dq
        dqg_ref[...] += dqg
        dsink_ref[...] = jnp.where(half, jnp.broadcast_to(dsink[0], (1, LANES)), jnp.broadcast_to(dsink[1], (1, LANES)))

        @pl.when(j % g2 == g2 - 1)
        def _():
            dkn = _f32dot(dk_ref[...], fold_ref[...])
            dk, dkg = _head_norm_bwd(dkn, kh, rk, kg_ref[...], bdv)
            dk_ref[...] = jnp.where(half, dk, 0.0)
            dv_ref[...] = jnp.where(half, _f32dot(dv_ref[...], fold_ref[...]), 0.0)
            dkg_ref[...] += dkg

        @pl.when(j == npairs - 1)
        def _():
            dqg_ref[...] = _f32dot(jnp.broadcast_to(dqg_ref[...], (8, LANES)), fold_ref[...])[0:1, :]

    qcol = pl.BlockSpec((L, LANES), lambda j: (0, md.bq_s + j))
    kcol = pl.BlockSpec((L, LANES), lambda j: (0, md.bk_s + j // g2))
    vcol = pl.BlockSpec((L, LANES), lambda j: (0, md.bv_s + j // g2))
    ocol = pl.BlockSpec((L, LANES), lambda j: (0, j))
    kvout = pl.BlockSpec((L, LANES), lambda j: (0, j // g2))
    pvec = pl.BlockSpec((1, LANES), lambda j: (0, j))
    vec = pl.BlockSpec((1, LANES), lambda j: (0, 0))
    sq = pl.BlockSpec((LANES, LANES), lambda j: (0, 0))
    kvshape = jax.ShapeDtypeStruct((L, LANES * md.nkv), F32)
    small = jax.ShapeDtypeStruct((1, LANES), F32)
    return pl.pallas_call(
        body, name=name, grid=(md.ps,),
        in_specs=[qcol, kcol, vcol, pvec, pvec, vec, vec, sq, sq, ocol, ocol],
        out_specs=[ocol, kvout, kvout, pvec, vec, vec],
        out_shape=[jax.ShapeDtypeStruct((L, md.ws), F32), kvshape, kvshape,
                   jax.ShapeDtypeStruct((1, md.ws), F32), small, small],
        scratch_shapes=[pltpu.VMEM((L, LANES), BF16)] * 6 + [pltpu.VMEM((L, LANES), F32)],
        compiler_params=_cparams("arbitrary"),
    )(u, u, u, sinkp, slopep, qg, kg, bd, fold, do, lse)


def _outnorm_fwd(of, os_, gf, gs, *, name):
    L, wf = of.shape
    ws = os_.shape[1]
    tr = _pick(L, _TR)

    def body(of_ref, os_ref, gf_ref, gs_ref, o_ref):
        for src, g_ref, c0, w in ((of_ref, gf_ref, 0, wf), (os_ref, gs_ref, wf, ws)):
            x = src[...]
            r = lax.rsqrt(jnp.mean(x * x, axis=-1, keepdims=True) + EPS)
            o_ref[:, c0:c0 + w] = (x * r * g_ref[...]).astype(o_ref.dtype)

    return pl.pallas_call(
        body, name=name, grid=(L // tr,),
        in_specs=[pl.BlockSpec((tr, wf), lambda i: (i, 0)), pl.BlockSpec((tr, ws), lambda i: (i, 0)),
                  pl.BlockSpec((1, wf), lambda i: (0, 0)), pl.BlockSpec((1, ws), lambda i: (0, 0))],
        out_specs=pl.BlockSpec((tr, wf + ws), lambda i: (i, 0)),
        out_shape=jax.ShapeDtypeStruct((L, wf + ws), BF16),
        compiler_params=_cparams("parallel"),
    )(of, os_, gf, gs)


def _outnorm_bwd(don, of, os_, gf, gs, *, name):
    L, wf = of.shape
    ws = os_.shape[1]
    tr = _pick(L, _TR)

    def body(d_ref, of_ref, os_ref, gf_ref, gs_ref, dof_ref, dos_ref, dgf_ref, dgs_ref):
        i = pl.program_id(0)
        for src, g_ref, c0, w, dx_ref, dg_ref in ((of_ref, gf_ref, 0, wf, dof_ref, dgf_ref),
                                                  (os_ref, gs_ref, wf, ws, dos_ref, dgs_ref)):
            x = src[...]
            dy = d_ref[:, c0:c0 + w]
            r = lax.rsqrt(jnp.mean(x * x, axis=-1, keepdims=True) + EPS)
            xh = x * r
            dxh = dy * g_ref[...]
            dx_ref[...] = r * (dxh - xh * jnp.mean(dxh * xh, axis=-1, keepdims=True))
            part = jnp.sum(dy * xh, axis=0, keepdims=True)

            @pl.when(i == 0)
            def _():
                dg_ref[...] = part

            @pl.when(i > 0)
            def _():
                dg_ref[...] += part

    rf = pl.BlockSpec((tr, wf), lambda i: (i, 0))
    rs = pl.BlockSpec((tr, ws), lambda i: (i, 0))
    vf = pl.BlockSpec((1, wf), lambda i: (0, 0))
    vs = pl.BlockSpec((1, ws), lambda i: (0, 0))
    return pl.pallas_call(
        body, name=name, grid=(L // tr,),
        in_specs=[pl.BlockSpec((tr, wf + ws), lambda i: (i, 0)), rf, rs, vf, vs],
        out_specs=[rf, rs, vf, vs],
        out_shape=[jax.ShapeDtypeStruct((L, wf), F32), jax.ShapeDtypeStruct((L, ws), F32),
                   jax.ShapeDtypeStruct((1, wf), F32), jax.ShapeDtypeStruct((1, ws), F32)],
        compiler_params=_cparams("arbitrary"),
    )(don, of, os_, gf, gs)


def _win_to_mine(w, md):
    d = w.shape[0]
    wf, ws, hd = md.wf, md.ws, HEAD_DIM
    o_z = 3 * wf
    o_sq = o_z + md.hf
    o_sk = o_sq + ws
    o_sv = o_sk + md.nkv * hd
    parts = [w[:, :3 * wf], w[:, o_sq:o_sq + ws]]
    for base in (o_sk, o_sv):
        for kv in range(md.nkv):
            blk = w[:, base + kv * hd:base + (kv + 1) * hd]
            parts += [blk, blk]
    z = w[:, o_z:o_z + md.hf].reshape(d, md.hf // 2, 2)
    z = jnp.pad(z, ((0, 0), (0, 0), (0, 6))).reshape(d, 4 * md.hf)
    parts.append(jnp.pad(z, ((0, 0), (0, LANES - 4 * md.hf + md.nup - md.nu))))
    return jnp.concatenate(parts, axis=1)


def _win_grad_to_ref(dw, md):
    d = dw.shape[0]
    wf, ws, hd = md.wf, md.ws, HEAD_DIM
    z = dw[:, md.bz * LANES:md.bz * LANES + 4 * md.hf].reshape(d, md.hf // 2, 8)[:, :, :2].reshape(d, md.hf)
    parts = [dw[:, :3 * wf], z, dw[:, md.bq_s * LANES:md.bq_s * LANES + ws]]
    for base in (md.bk_s, md.bv_s):
        for kv in range(md.nkv):
            c0 = (base + kv) * LANES
            parts.append(dw[:, c0:c0 + hd])
    return jnp.concatenate(parts, axis=1)


def _mix_small(p, md):
    tile2 = lambda v: jnp.tile(v.reshape(1, HEAD_DIM), (1, 2))
    b = p["b_forget"].reshape(md.hf // 2, 2)
    b = jnp.pad(b, ((0, 0), (0, 6))).reshape(1, 4 * md.hf)
    slopes = np.asarray(2.0 ** (-8.0 * np.arange(1, md.hq + 1) / md.hq), np.float32)
    return dict(
        g_mix=p["mix_norm"].reshape(1, -1),
        b_gate=jnp.pad(b, ((0, 0), (0, LANES - 4 * md.hf))),
        fqg=tile2(p["fox_q_norm"]), fkg=tile2(p["fox_k_norm"]),
        sqg=tile2(p["swa_q_norm"]), skg=tile2(p["swa_k_norm"]),
        sinkp=jnp.repeat(p["swa_sinks"], HEAD_DIM).reshape(1, md.ws),
        slopep=jnp.asarray(np.repeat(slopes, HEAD_DIM).reshape(1, md.ws)),
        gfo=p["fox_out_norm"].reshape(1, md.wf), gso=p["swa_out_norm"].reshape(1, md.ws),
    )


def _mix_fwd(h, sp, w_in, w_out, md):
    L = h.shape[0]
    hn = _rms_fwd(h, sp["g_mix"], name="mix_rms")
    u = _matmul(hn, w_in, name="mix_u")
    cexp, ct = _gate_fwd(u, sp["b_gate"], md, name="gate_fwd")
    ct3 = ct[:8 * md.pf].reshape(md.pf, 8, L)
    of, lsef = _fox_fwd(u, cexp, ct3, sp["fqg"], sp["fkg"], md, name="fox_fwd")
    os_, lses = _swa_fwd(u, sp["sinkp"], sp["slopep"], sp["sqg"], sp["skg"], md, name="swa_fwd")
    on = _outnorm_fwd(of, os_, sp["gfo"], sp["gso"], name="outnorm_fwd")
    h_out = _matmul(on, w_out, name="mix_out", residual=h)
    return h_out, (h, hn, u, cexp, ct3, of, lsef, os_, lses, on)


def _mix_bwd(dh, saved, sp, w_in, w_out, md):
    h, hn, u, cexp, ct3, of, lsef, os_, lses, on = saved
    L = h.shape[0]
    dhb = _cast_bf16(dh, name="mix_dhb")
    don = _matmul(dhb, w_out, name="mix_don", nt=True)
    dw_out = _matmul(_transpose(on, name="mix_onT"), dhb, name="mix_dwout", out_dtype=BF16)
    dof, dos, dgfo, dgso = _outnorm_bwd(don, of, os_, sp["gfo"], sp["gso"], name="outnorm_bwd")
    duq, duk, duv, dck, dfqg, dfkg = _fox_bwd(u, cexp, ct3, sp["fqg"], sp["fkg"], dof, lsef, md, name="fox_bwd")
    dsq, dsk, dsv, dsinkp, dsqg, dskg = _swa_bwd(u, sp["sinkp"], sp["slopep"], sp["sqg"], sp["skg"], dos, lses, md, name="swa_bwd")
    dck_t = jnp.pad(dck.reshape(8 * md.pf, L), ((0, LANES - 8 * md.pf), (0, 0)))
    dz, db = _gate_bwd(u, sp["b_gate"], dck_t, md, name="gate_bwd")
    du = jnp.concatenate([duq, duk, duv, dsq, dsk, dsv, dz, jnp.zeros((L, md.nup - md.nu), F32)], axis=1).astype(BF16)
    dhn = _matmul(du, w_in, name="mix_dhn", nt=True)
    dw_in = _matmul(_transpose(hn, name="mix_hnT"), du, name="mix_dwin", out_dtype=BF16)
    dh_in, dg_mix = _rms_bwd(dhn, h, sp["g_mix"], dh, name="mix_drms")
    small = dict(
        mix_norm=dg_mix.reshape(-1),
        b_forget=db[0, :4 * md.hf].reshape(md.hf // 2, 8)[:, :2].reshape(md.hf),
        fox_q_norm=dfqg[0, :HEAD_DIM], fox_k_norm=dfkg[0, :HEAD_DIM],
        swa_q_norm=dsqg[0, :HEAD_DIM], swa_k_norm=dskg[0, :HEAD_DIM],
        swa_sinks=dsinkp[0, ::HEAD_DIM],
        fox_out_norm=dgfo.reshape(-1), swa_out_norm=dgso.reshape(-1),
    )
    return dh_in, dw_in, dw_out, small


_ANY = pl.BlockSpec(memory_space=pl.ANY)
_HALF_ROWS = (512, 352, 256, 192, 128, 64, 32, 16)


def _mesh_pos():
    return lax.axis_index("x"), lax.axis_index("y"), lax.axis_index("c")


def _other_chips(x, y):
    return [(1 - x, y), (x, 1 - y), (1 - x, 1 - y)]


def _rows_half(ref, which):
    rh = ref.shape[-2] // 2
    if len(ref.shape) == 2:
        return ref.at[pl.ds(which * rh, rh), :]
    return ref.at[:, pl.ds(which * rh, rh), :]


def _remote(src, dst, send_sems, recv_sems, idx, dev):
    return pltpu.make_async_remote_copy(src_ref=src, dst_ref=dst, send_sem=send_sems.at[idx], recv_sem=recv_sems.at[idx],
                                        device_id=dev, device_id_type=MESH)


def _cast_into_chunk(w, l_idx, k_idx, *, name):
    _, rows, cols = w.shape
    tr = _pick(rows, (512, 352, 256, 128, 64, 32, 16))

    def body(l_ref, k_ref, w_ref, o_ref):
        o_ref[...] = w_ref[...].astype(o_ref.dtype)

    return pl.pallas_call(
        body, name=name,
        grid_spec=pltpu.PrefetchScalarGridSpec(
            num_scalar_prefetch=2, grid=(rows // tr,),
            in_specs=[pl.BlockSpec((None, tr, cols), lambda i, l, k: (l[0], i, 0))],
            out_specs=pl.BlockSpec((None, tr, cols), lambda i, l, k: (k[0], i, 0))),
        out_shape=jax.ShapeDtypeStruct((NCH, rows, cols), BF16),
        compiler_params=_cparams("parallel"),
    )(l_idx, k_idx, w)


def _allgather_chips(arrs, *, name):
    n = len(arrs)

    def body(*refs):
        outs = refs[n:2 * n]
        send_sems, recv_sems = refs[2 * n:]
        x, y, c = _mesh_pos()
        k = 2 * x + y
        sibling = (x, y, 1 - c)
        chips = _other_chips(x, y)
        sent = []
        for i in range(n):
            mine = _rows_half(outs[i].at[k], c)
            for j, (cx, cy) in enumerate(chips):
                cp = _remote(mine, mine, send_sems, recv_sems, 6 * i + j, (cx, cy, c))
                cp.start()
                sent.append(cp)
        for i in range(n):
            for j, (cx, cy) in enumerate(chips):
                blk = _rows_half(outs[i].at[2 * cx + cy], c)
                _remote(blk, blk, send_sems, recv_sems, 6 * i + j, (cx, cy, c)).wait_recv()
                fwd = _remote(blk, blk, send_sems, recv_sems, 6 * i + 3 + j, sibling)
                fwd.start()
                sent.append(fwd)
        for i in range(n):
            for j, (cx, cy) in enumerate(chips):
                blk = _rows_half(outs[i].at[2 * cx + cy], 1 - c)
                _remote(blk, blk, send_sems, recv_sems, 6 * i + 3 + j, sibling).wait_recv()
        for cp in sent:
            cp.wait_send()

    return pl.pallas_call(
        body, name=name, in_specs=[_ANY] * n, out_specs=[_ANY] * n,
        out_shape=[jax.ShapeDtypeStruct(a.shape, a.dtype) for a in arrs],
        input_output_aliases={i: i for i in range(n)},
        scratch_shapes=[pltpu.SemaphoreType.DMA((6 * n,)), pltpu.SemaphoreType.DMA((6 * n,))],
    )(*arrs)


def _exchange_sibling_halves(arrs, *, name):
    n = len(arrs)

    def body(*refs):
        srcs, lands = refs[:n], refs[n:2 * n]
        send_sems, recv_sems = refs[2 * n:]
        x, y, c = _mesh_pos()
        sibling = (x, y, 1 - c)
        cps = [_remote(_rows_half(srcs[i], 1 - c), lands[i], send_sems, recv_sems, i, sibling) for i in range(n)]
        for cp in cps:
            cp.start()
        for cp in cps:
            cp.wait_recv()
        for cp in cps:
            cp.wait_send()

    return pl.pallas_call(
        body, name=name, in_specs=[_ANY] * n, out_specs=[_ANY] * n,
        out_shape=[jax.ShapeDtypeStruct((NCH, a.shape[1] // 2, a.shape[2]), a.dtype) for a in arrs],
        scratch_shapes=[pltpu.SemaphoreType.DMA((n,)), pltpu.SemaphoreType.DMA((n,))],
    )(*arrs)


def _exchange_chip_chunks(arrs, *, name):
    n = len(arrs)

    def body(*refs):
        srcs, lands = refs[:n], refs[n:2 * n]
        send_sems, recv_sems = refs[2 * n:]
        x, y, c = _mesh_pos()
        cps = []
        for i in range(n):
            for j, (cx, cy) in enumerate(_other_chips(x, y)):
                cps.append(_remote(srcs[i].at[2 * cx + cy], lands[i].at[j], send_sems, recv_sems, 3 * i + j, (cx, cy, c)))
        for cp in cps:
            cp.start()
        for cp in cps:
            cp.wait_recv()
        for cp in cps:
            cp.wait_send()

    return pl.pallas_call(
        body, name=name, in_specs=[_ANY] * n, out_specs=[_ANY] * n,
        out_shape=[jax.ShapeDtypeStruct((NCH - 1,) + a.shape[1:], a.dtype) for a in arrs],
        scratch_shapes=[pltpu.SemaphoreType.DMA((3 * n,)), pltpu.SemaphoreType.DMA((3 * n,))],
    )(*arrs)


def _share_with_sibling(arrs, *, name):
    n = len(arrs)

    def body(*refs):
        outs = refs[n:2 * n]
        send_sems, recv_sems = refs[2 * n:]
        x, y, c = _mesh_pos()
        sibling = (x, y, 1 - c)
        cps = []
        for i in range(n):
            mine = _rows_half(outs[i], c)
            cps.append(_remote(mine, mine, send_sems, recv_sems, i, sibling))
        for cp in cps:
            cp.start()
        for i in range(n):
            theirs = _rows_half(outs[i], 1 - c)
            _remote(theirs, theirs, send_sems, recv_sems, i, sibling).wait_recv()
        for cp in cps:
            cp.wait_send()

    return pl.pallas_call(
        body, name=name, in_specs=[_ANY] * n, out_specs=[_ANY] * n,
        out_shape=[jax.ShapeDtypeStruct(a.shape, a.dtype) for a in arrs],
        input_output_aliases={i: i for i in range(n)},
        scratch_shapes=[pltpu.SemaphoreType.DMA((n,)), pltpu.SemaphoreType.DMA((n,))],
    )(*arrs)


def _allgather_devices(v, *, name):
    m = v.shape[0]

    def body(v_ref, out_ref, send_sems, recv_sems, local_sem):
        x, y, c = _mesh_pos()
        mine = 4 * x + 2 * y + c
        own = pltpu.make_async_copy(v_ref, out_ref.at[mine], local_sem)
        own.start()
        cps = []
        for r in range(1, 8):
            px, py, pc = (x + (r >> 2)) % 2, (y + ((r >> 1) & 1)) % 2, (c + (r & 1)) % 2
            cps.append((_remote(v_ref, out_ref.at[mine], send_sems, recv_sems, r - 1, (px, py, pc)), 4 * px + 2 * py + pc))
        for cp, _ in cps:
            cp.start()
        for r, (cp, theirs) in enumerate(cps):
            blk = out_ref.at[theirs]
            _remote(blk, blk, send_sems, recv_sems, r, (x, y, c)).wait_recv()
        for cp, _ in cps:
            cp.wait_send()
        own.wait()

    return pl.pallas_call(
        body, name=name, in_specs=[_ANY], out_specs=_ANY,
        out_shape=jax.ShapeDtypeStruct((8, m, LANES), v.dtype),
        scratch_shapes=[pltpu.SemaphoreType.DMA((7,)), pltpu.SemaphoreType.DMA((7,)), pltpu.SemaphoreType.DMA],
    )(v)


def _add_own_half(g, land, c_idx, *, name):
    nch, rh, cols = land.shape
    tr = _pick(rh, _HALF_ROWS)
    nt = rh // tr

    def body(c_ref, g_ref, l_ref, o_ref):
        o_ref[...] = (g_ref[...].astype(F32) + l_ref[...].astype(F32)).astype(o_ref.dtype)

    return pl.pallas_call(
        body, name=name,
        grid_spec=pltpu.PrefetchScalarGridSpec(
            num_scalar_prefetch=1, grid=(nch, nt),
            in_specs=[pl.BlockSpec((None, tr, cols), lambda k, i, c: (k, c[0] * nt + i, 0)),
                      pl.BlockSpec((None, tr, cols), lambda k, i, c: (k, i, 0))],
            out_specs=pl.BlockSpec((None, tr, cols), lambda k, i, c: (k, i, 0))),
        out_shape=jax.ShapeDtypeStruct(land.shape, BF16),
        compiler_params=_cparams("parallel", "parallel"),
    )(c_idx, g, land)


def _add_chunks(s, land, k_idx, c_idx, *, name):
    _, rh, cols = s.shape
    tr = _pick(rh, _HALF_ROWS)
    nt = rh // tr

    def body(k_ref, c_ref, s_ref, l_ref, o_ref):
        t = s_ref[...].astype(F32)
        for j in range(NCH - 1):
            t = t + l_ref[j].astype(F32)
        o_ref[...] = t

    return pl.pallas_call(
        body, name=name,
        grid_spec=pltpu.PrefetchScalarGridSpec(
            num_scalar_prefetch=2, grid=(nt,),
            in_specs=[pl.BlockSpec((None, tr, cols), lambda i, k, c: (k[0], i, 0)),
                      pl.BlockSpec((NCH - 1, tr, cols), lambda i, k, c: (0, i, 0))],
            out_specs=pl.BlockSpec((tr, cols), lambda i, k, c: (c[0] * nt + i, 0))),
        out_shape=jax.ShapeDtypeStruct((2 * rh, cols), F32),
        compiler_params=_cparams("parallel"),
    )(k_idx, c_idx, s, land)


def _sum_devices(v, *, name):
    _, m, _ = v.shape

    def body(v_ref, o_ref):
        t = v_ref[0]
        for d in range(1, 8):
            t = t + v_ref[d]
        o_ref[...] = t

    return pl.pallas_call(
        body, name=name, grid=(1,),
        in_specs=[pl.BlockSpec((8, m, LANES), lambda i: (0, 0, 0))],
        out_specs=pl.BlockSpec((m, LANES), lambda i: (0, 0)),
        out_shape=jax.ShapeDtypeStruct((m, LANES), F32),
        compiler_params=_cparams("arbitrary"),
    )(v)


def _adamw_math(w, g, m, v):
    m = ADAM_B1 * m + (1.0 - ADAM_B1) * g
    v = ADAM_B2 * v + (1.0 - ADAM_B2) * (g * g)
    m_hat = m / (1.0 - ADAM_B1 ** ADAM_STEP)
    v_hat = v / (1.0 - ADAM_B2 ** ADAM_STEP)
    delta = -ADAM_LR * (m_hat / (jnp.sqrt(v_hat) + ADAM_EPS) + ADAM_WD * w)
    return delta, m, v


def _adamw_layer(w, m, v, g, layer, prev, *, name):
    depth, rows, cols = w.shape
    tr = _pick(rows, (256, 128, 64, 32, 16, 8))
    lay = pl.BlockSpec((None, tr, cols), lambda i, l: (l[0], i, 0))
    n_prev = 0 if prev is None else 4

    def body(l_ref, w_ref, m_ref, v_ref, g_ref, *rest):
        go_ref, d_ref, mo_ref, vo_ref = rest[n_prev:]
        g = g_ref[...]
        delta, m_new, v_new = _adamw_math(w_ref[...], g, m_ref[...], v_ref[...])
        go_ref[...] = g
        d_ref[...] = delta
        mo_ref[...] = m_new
        vo_ref[...] = v_new

    stack = jax.ShapeDtypeStruct(w.shape, F32)
    return pl.pallas_call(
        body, name=name,
        grid_spec=pltpu.PrefetchScalarGridSpec(
            num_scalar_prefetch=1, grid=(rows // tr,),
            in_specs=[lay, lay, lay, pl.BlockSpec((tr, cols), lambda i, l: (i, 0))] + [_ANY] * n_prev,
            out_specs=[lay] * 4),
        out_shape=[stack] * 4,
        input_output_aliases={} if prev is None else {5 + q: q for q in range(4)},
        compiler_params=_cparams("parallel"),
    )(layer, w, m, v, g, *(() if prev is None else prev))


def _adamw_flat(w, g, m, v, *, name):
    def body(w_ref, g_ref, m_ref, v_ref, d_ref, mo_ref, vo_ref):
        d_ref[...], mo_ref[...], vo_ref[...] = _adamw_math(w_ref[...], g_ref[...], m_ref[...], v_ref[...])

    blk = pl.BlockSpec(w.shape, lambda i: (0, 0))
    return pl.pallas_call(
        body, name=name, grid=(1,), in_specs=[blk] * 4, out_specs=[blk] * 3,
        out_shape=[jax.ShapeDtypeStruct(w.shape, F32)] * 3, compiler_params=_cparams("arbitrary"),
    )(w, g, m, v)


def _reduce_scatter(parts, c_idx, k_idx, tag):
    lands = _exchange_sibling_halves(parts, name=f"{tag}_rs_sibling")
    sums = [_add_own_half(p, l, c_idx, name=f"{tag}_rs_add2") for p, l in zip(parts, lands)]
    lands = _exchange_chip_chunks(sums, name=f"{tag}_rs_chips")
    tots = [_add_chunks(s, l, k_idx, c_idx, name=f"{tag}_rs_add4") for s, l in zip(sums, lands)]
    return _share_with_sibling(tots, name=f"{tag}_rs_share")


_WEIGHTS = ('meta_tokens', 'ffn1_norm', 'ffn1_w_gate', 'ffn1_w_up', 'ffn1_w_down', 'mix_norm', 'w_in', 'b_forget',
            'fox_q_norm', 'fox_k_norm', 'swa_q_norm', 'swa_k_norm', 'swa_sinks', 'fox_out_norm', 'swa_out_norm', 'w_out',
            'ffn2_norm', 'ffn2_w_gate', 'ffn2_w_up', 'ffn2_w_down')
_BIG = ('ffn1_w_gate', 'ffn1_w_up', 'ffn1_w_down', 'w_in', 'w_out', 'ffn2_w_gate', 'ffn2_w_up', 'ffn2_w_down')
_SMALL = tuple(n for n in _WEIGHTS if n not in _BIG and n != 'meta_tokens')
_MIX_SMALL = ('mix_norm', 'b_forget', 'fox_q_norm', 'fox_k_norm', 'swa_q_norm', 'swa_k_norm', 'swa_sinks',
              'fox_out_norm', 'swa_out_norm')


def _pack_rows(vectors):
    flat = jnp.concatenate([v.reshape(-1) for v in vectors])
    n = flat.shape[0]
    m = -(-n // (8 * LANES)) * 8
    return jnp.pad(flat, (0, m * LANES - n)).reshape(m, LANES)


def _unpack_rows(packed, shapes):
    flat = packed.reshape(-1)
    out, o = [], 0
    for s in shapes:
        n = int(np.prod(s))
        out.append(flat[o:o + n].reshape(s))
        o += n
    return out


def kernel(x, meta_tokens, ffn1_norm, ffn1_w_gate, ffn1_w_up, ffn1_w_down, mix_norm, w_in, b_forget, fox_q_norm, fox_k_norm, swa_q_norm, swa_k_norm, swa_sinks, fox_out_norm, swa_out_norm, w_out, ffn2_norm, ffn2_w_gate, ffn2_w_up, ffn2_w_down, loss_target, m_meta_tokens, m_ffn1_norm, m_ffn1_w_gate, m_ffn1_w_up, m_ffn1_w_down, m_mix_norm, m_w_in, m_b_forget, m_fox_q_norm, m_fox_k_norm, m_swa_q_norm, m_swa_k_norm, m_swa_sinks, m_fox_out_norm, m_swa_out_norm, m_w_out, m_ffn2_norm, m_ffn2_w_gate, m_ffn2_w_up, m_ffn2_w_down, v_meta_tokens, v_ffn1_norm, v_ffn1_w_gate, v_ffn1_w_up, v_ffn1_w_down, v_mix_norm, v_w_in, v_b_forget, v_fox_q_norm, v_fox_k_norm, v_swa_q_norm, v_swa_k_norm, v_swa_sinks, v_fox_out_norm, v_swa_out_norm, v_w_out, v_ffn2_norm, v_ffn2_w_gate, v_ffn2_w_up, v_ffn2_w_down):
    W = dict(meta_tokens=meta_tokens, ffn1_norm=ffn1_norm, ffn1_w_gate=ffn1_w_gate, ffn1_w_up=ffn1_w_up, ffn1_w_down=ffn1_w_down, mix_norm=mix_norm, w_in=w_in, b_forget=b_forget, fox_q_norm=fox_q_norm, fox_k_norm=fox_k_norm, swa_q_norm=swa_q_norm, swa_k_norm=swa_k_norm, swa_sinks=swa_sinks, fox_out_norm=fox_out_norm, swa_out_norm=swa_out_norm, w_out=w_out, ffn2_norm=ffn2_norm, ffn2_w_gate=ffn2_w_gate, ffn2_w_up=ffn2_w_up, ffn2_w_down=ffn2_w_down)
    Mo = dict(meta_tokens=m_meta_tokens, ffn1_norm=m_ffn1_norm, ffn1_w_gate=m_ffn1_w_gate, ffn1_w_up=m_ffn1_w_up, ffn1_w_down=m_ffn1_w_down, mix_norm=m_mix_norm, w_in=m_w_in, b_forget=m_b_forget, fox_q_norm=m_fox_q_norm, fox_k_norm=m_fox_k_norm, swa_q_norm=m_swa_q_norm, swa_k_norm=m_swa_k_norm, swa_sinks=m_swa_sinks, fox_out_norm=m_fox_out_norm, swa_out_norm=m_swa_out_norm, w_out=m_w_out, ffn2_norm=m_ffn2_norm, ffn2_w_gate=m_ffn2_w_gate, ffn2_w_up=m_ffn2_w_up, ffn2_w_down=m_ffn2_w_down)
    Vo = dict(meta_tokens=v_meta_tokens, ffn1_norm=v_ffn1_norm, ffn1_w_gate=v_ffn1_w_gate, ffn1_w_up=v_ffn1_w_up, ffn1_w_down=v_ffn1_w_down, mix_norm=v_mix_norm, w_in=v_w_in, b_forget=v_b_forget, fox_q_norm=v_fox_q_norm, fox_k_norm=v_fox_k_norm, swa_q_norm=v_swa_q_norm, swa_k_norm=v_swa_k_norm, swa_sinks=v_swa_sinks, fox_out_norm=v_fox_out_norm, swa_out_norm=v_swa_out_norm, w_out=v_w_out, ffn2_norm=v_ffn2_norm, ffn2_w_gate=v_ffn2_w_gate, ffn2_w_up=v_ffn2_w_up, ffn2_w_down=v_ffn2_w_down)

    _, S, D = x.shape
    L = S + BLOCK
    depth = ffn1_norm.shape[0]
    md = _MixDims(D)
    mx, my, mc = _mesh_pos()
    k_idx = (2 * mx + my).astype(jnp.int32).reshape(1)
    c_idx = mc.astype(jnp.int32).reshape(1)
    dcols = D // NCH

    meta_all = _allgather_devices(meta_tokens.reshape(-1, LANES), name="meta_allgather")
    meta_full = jnp.transpose(meta_all[0::2].reshape(NCH, N_META, dcols), (1, 0, 2)).reshape(N_META, D)

    wts = []
    for l in range(depth):
        l_idx = jnp.full((1,), l, jnp.int32)
        bf = lambda name: _cast_into_chunk(W[name], l_idx, k_idx, name="cast_chunk")
        g1, u1, d1 = _allgather_chips([bf('ffn1_w_gate'), bf('ffn1_w_up'), bf('ffn1_w_down')], name="ffn_allgather")
        wi, wo = _allgather_chips([bf('w_in'), bf('w_out')], name="mix_allgather")
        g2, u2, d2 = _allgather_chips([bf('ffn2_w_gate'), bf('ffn2_w_up'), bf('ffn2_w_down')], name="ffn_allgather")
        wi = _win_to_mine(jnp.transpose(wi, (1, 0, 2)).reshape(D, NCH * wi.shape[2]), md)
        wts.append(dict(g1=g1, u1=u1, d1=d1.reshape(-1, D), wi=wi, wo=wo.reshape(-1, D), g2=g2, u2=u2, d2=d2.reshape(-1, D)))

    h = jnp.concatenate([jnp.zeros((PAD, D), F32), meta_full, x[0]], axis=0)
    saved = []
    for l in range(depth):
        wl = wts[l]
        sp = _mix_small({n: W[n][l] for n in _MIX_SMALL}, md)
        h, s1 = _ffn_fwd(h, ffn1_norm[l].reshape(1, D), wl['g1'], wl['u1'], wl['d1'], "ffn")
        h, s2 = _mix_fwd(h, sp, wl['wi'], wl['wo'], md)
        h, s3 = _ffn_fwd(h, ffn2_norm[l].reshape(1, D), wl['g2'], wl['u2'], wl['d2'], "ffn")
        saved.append((s1, s2, s3, sp))

    loss_part, dh = _loss_grad(h, loss_target[0], name="loss_grad")

    small_grads = {n: [None] * depth for n in _SMALL}
    stacks = {n: None for n in _BIG}

    def update(name, l, grad):
        stacks[name] = _adamw_layer(W[name], Mo[name], Vo[name], grad, jnp.full((1,), l, jnp.int32), stacks[name],
                                    name="adamw_layer")

    for l in range(depth - 1, -1, -1):
        wl = wts[l]
        s1, s2, s3, sp = saved[l]
        dh, dg, dwg, dwu, dwd = _ffn_bwd(dh, s3, ffn2_norm[l].reshape(1, D), wl['g2'], wl['u2'], wl['d2'], "ffn")
        small_grads['ffn2_norm'][l] = dg.reshape(-1)
        rg, ru, rd = _reduce_scatter([dwg, dwu, dwd.reshape(NCH, -1, D)], c_idx, k_idx,"ffn")
        update('ffn2_w_gate', l, rg)
        update('ffn2_w_up', l, ru)
        update('ffn2_w_down', l, rd)

        dh, dwi, dwo, sm = _mix_bwd(dh, s2, sp, wl['wi'], wl['wo'], md)
        for n in _MIX_SMALL:
            small_grads[n][l] = sm[n]
        dwi = _win_grad_to_ref(dwi, md)
        dwi = jnp.transpose(dwi.reshape(D, NCH, -1), (1, 0, 2))
        ri, ro = _reduce_scatter([dwi, dwo.reshape(NCH, -1, D)], c_idx, k_idx,"mix")
        update('w_in', l, ri)
        update('w_out', l, ro)

        dh, dg, dwg, dwu, dwd = _ffn_bwd(dh, s1, ffn1_norm[l].reshape(1, D), wl['g1'], wl['u1'], wl['d1'], "ffn")
        small_grads['ffn1_norm'][l] = dg.reshape(-1)
        rg, ru, rd = _reduce_scatter([dwg, dwu, dwd.reshape(NCH, -1, D)], c_idx, k_idx,"ffn")
        update('ffn1_w_gate', l, rg)
        update('ffn1_w_up', l, ru)
        update('ffn1_w_down', l, rd)

    grad_x = dh[BLOCK:][None]

    small_shapes = [W[n].shape for n in _SMALL]
    parts = [jnp.stack(small_grads[n]) for n in _SMALL] + [dh[PAD:BLOCK], loss_part[0, :1]]
    packed = _pack_rows(parts)
    total = _sum_devices(_allgather_devices(packed, name="small_allgather"), name="small_sum")
    *g_small, g_meta, loss = _unpack_rows(total, small_shapes + [(N_META, D), (1,)])
    g_meta = lax.dynamic_slice(g_meta, (0, k_idx[0] * dcols), (N_META, dcols))

    sw = _pack_rows([W[n] for n in _SMALL])
    sd, smm, svv = _adamw_flat(sw, _pack_rows(g_small), _pack_rows([Mo[n] for n in _SMALL]),
                               _pack_rows([Vo[n] for n in _SMALL]), name="adamw_small")
    d_small, m_small, v_small = (_unpack_rows(t, small_shapes) for t in (sd, smm, svv))
    d_meta, m_meta, v_meta = _adamw_flat(meta_tokens, g_meta, m_meta_tokens, v_meta_tokens, name="adamw_meta")

    grads, deltas, new_m, new_v = {}, {}, {}, {}
    for n in _BIG:
        grads[n], deltas[n], new_m[n], new_v[n] = stacks[n]
    for i, n in enumerate(_SMALL):
        grads[n], deltas[n], new_m[n], new_v[n] = g_small[i], d_small[i], m_small[i], v_small[i]
    grads['meta_tokens'], deltas['meta_tokens'], new_m['meta_tokens'], new_v['meta_tokens'] = g_meta, d_meta, m_meta, v_meta
    return (loss.reshape(()), grad_x, *[grads[n] for n in _WEIGHTS], *[deltas[n] for n in _WEIGHTS],
            *[new_m[n] for n in _WEIGHTS], *[new_v[n] for n in _WEIGHTS])
```

```python
import numpy as np
import jax
import jax.numpy as jnp
from jax import lax
from jax.experimental import pallas as pl
from jax.experimental.pallas import tpu as pltpu

F32 = jnp.float32
BF16 = jnp.bfloat16

HEAD_DIM = 64
N_META = 16
BLOCK = 128
WINDOW = 128
PAD = BLOCK - N_META
EPS = 1e-6
NEG_INF = -1e30
SWA_GROUP = 8
NCH = 4
LANES = 128
QBLOCK = 512

ADAM_LR = 0.001
ADAM_B1 = 0.9
ADAM_B2 = 0.999
ADAM_EPS = 1e-08
ADAM_WD = 0.01
ADAM_STEP = 10

V7X_VMEM_BYTES = 64 * 1024 * 1024
VMEM_LIMIT = V7X_VMEM_BYTES - 8 * 1024 * 1024
MESH = pl.DeviceIdType.MESH
HIGHEST = lax.Precision.HIGHEST

_TM = (1088, 1024, 704, 512, 384, 256, 128)
_TN = (1408, 1024, 768, 512, 384, 256, 128)
_TK = (2176, 1408, 1024, 512, 384, 256, 128)
_TR = (544, 512, 384, 272, 256, 128)


def _pick(n, cands):
    for c in cands:
        if n % c == 0:
            return c
    return n


def _cparams(*sem):
    return pltpu.CompilerParams(dimension_semantics=sem if sem else None, vmem_limit_bytes=VMEM_LIMIT)


def _matmul(a, b, *, name, nt=False, b_chunked=False, out_chunked=False, out_dtype=F32,
            residual=None, scale=1.0, comm=None):
    M, K = a.shape
    if not nt:
        N = b.shape[-1] * (NCH if b_chunked else 1)
        assert b.shape[-2] == K
        k_unit = K
    else:
        N = b.shape[-2]
        k_unit = b.shape[-1]
        assert k_unit * (NCH if b_chunked else 1) == K
    n_unit = N // NCH if (out_chunked or (b_chunked and not nt)) else N
    tm, tn, tk = _pick(M, _TM), _pick(n_unit, _TN), _pick(k_unit, _TK)
    osz = jnp.dtype(out_dtype).itemsize

    def est(tm_):
        return (2 * tm_ * tk * 2 + 2 * tk * tn * 2 + tm_ * tn * 4 + 2 * tm_ * tn * osz
                + (2 * tm_ * tn * 4 if residual is not None else 0))

    while est(tm) > VMEM_LIMIT * 3 // 4 and tm % 32 == 0:
        tm //= 2
    npc, kpc = n_unit // tn, k_unit // tk
    nk = K // tk
    grid = (M // tm, N // tn, nk)

    a_spec = pl.BlockSpec((tm, tk), lambda i, j, k: (i, k))
    if not nt:
        if b_chunked:
            b_spec = pl.BlockSpec((None, tk, tn), lambda i, j, k: (j // npc, k, j % npc))
        else:
            b_spec = pl.BlockSpec((tk, tn), lambda i, j, k: (k, j))
        dims = (((1,), (0,)), ((), ()))
    else:
        if b_chunked:
            b_spec = pl.BlockSpec((None, tn, tk), lambda i, j, k: (k // kpc, j, k % kpc))
        else:
            b_spec = pl.BlockSpec((tn, tk), lambda i, j, k: (j, k))
        dims = (((1,), (1,)), ((), ()))
    if out_chunked:
        o_spec = pl.BlockSpec((None, tm, tn), lambda i, j, k: (j // npc, i, j % npc))
        out_shape = jax.ShapeDtypeStruct((NCH, M, n_unit), out_dtype)
    else:
        o_spec = pl.BlockSpec((tm, tn), lambda i, j, k: (i, j))
        out_shape = jax.ShapeDtypeStruct((M, N), out_dtype)
    in_specs = [a_spec, b_spec]
    args = [a, b]
    if residual is not None:
        assert not out_chunked
        in_specs.append(pl.BlockSpec((tm, tn), lambda i, j, k: (i, j)))
        args.append(residual)

    n_main = len(args)
    n_cin = 0 if comm is None else len(comm.operands)
    n_cout = 0 if comm is None else len(comm.out_shapes)

    def body(*refs):
        r_ref = refs[2] if residual is not None else None
        a_ref, b_ref = refs[0], refs[1]
        o_ref = refs[n_main + n_cin]
        acc_ref = refs[n_main + n_cin + 1 + n_cout]
        i, j, k = pl.program_id(0), pl.program_id(1), pl.program_id(2)
        if comm is not None:
            cin = refs[n_main:n_main + n_cin]
            cout = refs[n_main + n_cin + 1:n_main + n_cin + 1 + n_cout]
            send_sems, recv_sems = refs[-2:]

            @pl.when((i == 0) & (j == 0) & (k == 0))
            def _():
                for cp in comm.copies(cin, cout, send_sems, recv_sems):
                    cp.start()

        @pl.when(k == 0)
        def _():
            acc_ref[...] = jnp.zeros_like(acc_ref)

        acc_ref[...] += lax.dot_general(a_ref[...], b_ref[...], dims, preferred_element_type=F32)

        @pl.when(k == nk - 1)
        def _():
            r = acc_ref[...]
            if scale != 1.0:
                r = r * scale
            if residual is not None:
                r = r + r_ref[...]
            o_ref[...] = r.astype(o_ref.dtype)

        if comm is not None:
            @pl.when((i == grid[0] - 1) & (j == grid[1] - 1) & (k == nk - 1))
            def _():
                cps = comm.copies(cin, cout, send_sems, recv_sems)
                for cp in cps:
                    cp.wait_recv()
                for cp in cps:
                    cp.wait_send()

    scratch = [pltpu.VMEM((tm, tn), F32)]
    if comm is None:
        return pl.pallas_call(
            body, name=name, grid=grid, in_specs=in_specs, out_specs=o_spec, out_shape=out_shape,
            scratch_shapes=scratch, compiler_params=_cparams("parallel", "parallel", "arbitrary"),
        )(*args)
    scratch += [pltpu.SemaphoreType.DMA((comm.n_sems,)), pltpu.SemaphoreType.DMA((comm.n_sems,))]
    return pl.pallas_call(
        body, name=name, grid=grid, in_specs=in_specs + [_ANY] * n_cin,
        out_specs=[o_spec] + [_ANY] * n_cout, out_shape=[out_shape] + list(comm.out_shapes),
        input_output_aliases={n_main + s: 1 + d for s, d in comm.aliases.items()},
        scratch_shapes=scratch, compiler_params=_cparams("arbitrary", "arbitrary", "arbitrary"),
    )(*args, *comm.operands)


def _transpose(x, *, name):
    M, N = x.shape
    tc = _pick(N, (512, 384, 256, 128))

    def body(x_ref, o_ref):
        o_ref[...] = x_ref[...].astype(F32).T.astype(o_ref.dtype)

    return pl.pallas_call(
        body, name=name, grid=(N // tc,),
        in_specs=[pl.BlockSpec((M, tc), lambda j: (0, j))],
        out_specs=pl.BlockSpec((tc, M), lambda j: (j, 0)),
        out_shape=jax.ShapeDtypeStruct((N, M), x.dtype),
        compiler_params=_cparams("parallel"),
    )(x)


def _rms_fwd(h, g, *, name):
    L, D = h.shape
    tr = _pick(L, _TR)

    def body(h_ref, g_ref, o_ref):
        x = h_ref[...]
        r = lax.rsqrt(jnp.mean(x * x, axis=-1, keepdims=True) + EPS)
        o_ref[...] = (x * r * g_ref[...]).astype(o_ref.dtype)

    return pl.pallas_call(
        body, name=name, grid=(L // tr,),
        in_specs=[pl.BlockSpec((tr, D), lambda i: (i, 0)), pl.BlockSpec((1, D), lambda i: (0, 0))],
        out_specs=pl.BlockSpec((tr, D), lambda i: (i, 0)),
        out_shape=jax.ShapeDtypeStruct((L, D), BF16),
        compiler_params=_cparams("parallel"),
    )(h, g)


def _rms_bwd(dy, h, g, dh, *, name):
    L, D = h.shape
    tr = _pick(L, _TR)

    def body(dy_ref, h_ref, g_ref, dh_ref, o_ref, dg_ref):
        i = pl.program_id(0)
        x = h_ref[...]
        dyv = dy_ref[...]
        r = lax.rsqrt(jnp.mean(x * x, axis=-1, keepdims=True) + EPS)
        xh = x * r
        dxh = dyv * g_ref[...]
        dx = r * (dxh - xh * jnp.mean(dxh * xh, axis=-1, keepdims=True))
        o_ref[...] = dh_ref[...] + dx
        part = jnp.sum(dyv * xh, axis=0, keepdims=True)

        @pl.when(i == 0)
        def _():
            dg_ref[...] = part

        @pl.when(i > 0)
        def _():
            dg_ref[...] += part

    row = pl.BlockSpec((tr, D), lambda i: (i, 0))
    vec = pl.BlockSpec((1, D), lambda i: (0, 0))
    return pl.pallas_call(
        body, name=name, grid=(L // tr,),
        in_specs=[row, row, vec, row], out_specs=[row, vec],
        out_shape=[jax.ShapeDtypeStruct((L, D), F32), jax.ShapeDtypeStruct((1, D), F32)],
        compiler_params=_cparams("arbitrary"),
    )(dy, h, g, dh)


def _swiglu_fwd(gate, up, *, name):
    L, F = gate.shape
    tr, tc = _pick(L, _TR), _pick(F, _TN)

    def body(g_ref, u_ref, o_ref):
        g = g_ref[...].astype(F32)
        o_ref[...] = (g * jax.nn.sigmoid(g) * u_ref[...].astype(F32)).astype(o_ref.dtype)

    blk = pl.BlockSpec((tr, tc), lambda i, j: (i, j))
    return pl.pallas_call(
        body, name=name, grid=(L // tr, F // tc), in_specs=[blk, blk], out_specs=blk,
        out_shape=jax.ShapeDtypeStruct((L, F), BF16),
        compiler_params=_cparams("parallel", "parallel"),
    )(gate, up)


def _swiglu_bwd(dact, gate, up, *, name):
    L, F = gate.shape
    tr, tc = _pick(L, _TR), _pick(F, _TN)

    def body(d_ref, g_ref, u_ref, dg_ref, du_ref):
        d = d_ref[...].astype(F32)
        g = g_ref[...].astype(F32)
        u = u_ref[...].astype(F32)
        sg = jax.nn.sigmoid(g)
        du_ref[...] = (d * g * sg).astype(du_ref.dtype)
        dg_ref[...] = (d * u * sg * (1.0 + g * (1.0 - sg))).astype(dg_ref.dtype)

    blk = pl.BlockSpec((tr, tc), lambda i, j: (i, j))
    return pl.pallas_call(
        body, name=name, grid=(L // tr, F // tc), in_specs=[blk, blk, blk], out_specs=[blk, blk],
        out_shape=[jax.ShapeDtypeStruct((L, F), BF16)] * 2,
        compiler_params=_cparams("parallel", "parallel"),
    )(dact, gate, up)


def _cast_bf16(x, *, name):
    L, D = x.shape
    tr = _pick(L, _TR)

    def body(x_ref, o_ref):
        o_ref[...] = x_ref[...].astype(o_ref.dtype)

    blk = pl.BlockSpec((tr, D), lambda i: (i, 0))
    return pl.pallas_call(
        body, name=name, grid=(L // tr,), in_specs=[blk], out_specs=blk,
        out_shape=jax.ShapeDtypeStruct((L, D), BF16), compiler_params=_cparams("parallel"),
    )(x)


def _loss_grad(h, target, *, name):
    L, D = h.shape
    S = target.shape[0]
    nb = L // BLOCK

    def body(h_ref, t_ref, loss_ref, dh_ref):
        i = pl.program_id(0)

        @pl.when(i == 0)
        def _():
            loss_ref[...] = jnp.zeros_like(loss_ref)
            dh_ref[...] = jnp.zeros_like(dh_ref)

        @pl.when(i > 0)
        def _():
            err = h_ref[...] - t_ref[...]
            dh_ref[...] = err * (1.0 / D)
            loss_ref[...] += jnp.full(loss_ref.shape, (0.5 / D) * jnp.sum(err * err), F32)

    return pl.pallas_call(
        body, name=name, grid=(nb,),
        in_specs=[pl.BlockSpec((BLOCK, D), lambda i: (i, 0)),
                  pl.BlockSpec((BLOCK, D), lambda i: (jnp.maximum(i - 1, 0), 0))],
        out_specs=[pl.BlockSpec((1, LANES), lambda i: (0, 0)), pl.BlockSpec((BLOCK, D), lambda i: (i, 0))],
        out_shape=[jax.ShapeDtypeStruct((1, LANES), F32), jax.ShapeDtypeStruct((L, D), F32)],
        compiler_params=_cparams("arbitrary"),
    )(h, target)


def _ffn_fwd(h, g, wg, wu, wd, tag):
    hn = _rms_fwd(h, g, name=f"{tag}_rms")
    gate = _matmul(hn, wg, name=f"{tag}_gate", b_chunked=True, out_dtype=BF16)
    up = _matmul(hn, wu, name=f"{tag}_up", b_chunked=True, out_dtype=BF16)
    act = _swiglu_fwd(gate, up, name=f"{tag}_act")
    h_out = _matmul(act, wd, name=f"{tag}_down", residual=h, scale=0.5)
    return h_out, (h, hn, gate, up, act)


def _ffn_bwd(dh, saved, g, wg, wu, wd, tag, queue=None):
    h, hn, gate, up, act = saved
    dout = _cast_bf16(dh, name=f"{tag}_dout")
    dact = _mm(queue, dout, wd, name=f"{tag}_dact", nt=True, out_dtype=BF16, scale=0.5)
    actT = _transpose(act, name=f"{tag}_actT")
    dwd = _mm(queue, actT, dout, name=f"{tag}_dwd", out_dtype=BF16, scale=0.5)
    dgate, dup = _swiglu_bwd(dact, gate, up, name=f"{tag}_dswiglu")
    hnT = _transpose(hn, name=f"{tag}_hnT")
    dwg = _mm(queue, hnT, dgate, name=f"{tag}_dwg", out_chunked=True, out_dtype=BF16)
    dwu = _mm(queue, hnT, dup, name=f"{tag}_dwu", out_chunked=True, out_dtype=BF16)
    dhn = _mm(queue, dgate, wg, name=f"{tag}_dhn_g", nt=True, b_chunked=True)
    dhn = _mm(queue, dup, wu, name=f"{tag}_dhn_u", nt=True, b_chunked=True, residual=dhn)
    dh_in, dg = _rms_bwd(dhn, h, g, dh, name=f"{tag}_drms")
    return dh_in, dg, dwg, dwu, dwd


class _MixDims:
    def __init__(self, d_model):
        self.wf = d_model // 2
        self.ws = d_model // 2
        self.pf = self.wf // LANES
        self.ps = self.ws // LANES
        self.hf = self.wf // HEAD_DIM
        self.hq = self.ws // HEAD_DIM
        self.nkv = max(1, self.hq // SWA_GROUP)
        self.g = self.hq // self.nkv
        self.bq_f, self.bk_f, self.bv_f = 0, self.pf, 2 * self.pf
        self.bq_s = 3 * self.pf
        self.bk_s = self.bq_s + self.ps
        self.bv_s = self.bk_s + self.nkv
        self.bz = self.bv_s + self.nkv
        self.nu = (self.bz + 1) * LANES
        self.nup = -(-self.nu // 512) * 512
        self.in_width = 3 * self.wf + self.hf + self.ws + 2 * self.nkv * HEAD_DIM
        assert self.hf <= 2 * (LANES // 8)

    def gate_lane(self, h):
        return 8 * (h // 2) + h % 2

    def column_map(self):
        wf, ws, hd = self.wf, self.ws, HEAD_DIM
        src = np.full((self.nup,), -1, np.int64)
        src[0:3 * wf] = np.arange(3 * wf)
        o_sq = 3 * wf + self.hf
        src[self.bq_s * LANES:self.bq_s * LANES + ws] = o_sq + np.arange(ws)
        o_sk = o_sq + ws
        o_sv = o_sk + self.nkv * hd
        for kv in range(self.nkv):
            for rep in range(2):
                c0 = (self.bk_s + kv) * LANES + rep * hd
                src[c0:c0 + hd] = o_sk + kv * hd + np.arange(hd)
                c0 = (self.bv_s + kv) * LANES + rep * hd
                src[c0:c0 + hd] = o_sv + kv * hd + np.arange(hd)
        for h in range(self.hf):
            src[self.bz * LANES + self.gate_lane(h)] = 3 * wf + h
        return src

    def grad_column_map(self):
        src = self.column_map()
        dst = np.zeros((self.in_width,), np.int64)
        for col in range(self.nup - 1, -1, -1):
            if src[col] >= 0:
                dst[src[col]] = col
        return dst


def _block_diag_mean():
    m = np.zeros((LANES, LANES), np.float32)
    m[:HEAD_DIM, :HEAD_DIM] = 1.0 / HEAD_DIM
    m[HEAD_DIM:, HEAD_DIM:] = 1.0 / HEAD_DIM
    return jnp.asarray(m)


def _fold_halves():
    m = np.eye(LANES, dtype=np.float32)
    m[np.arange(LANES), (np.arange(LANES) + HEAD_DIM) % LANES] = 1.0
    return jnp.asarray(m)


def _gate_expand(md):
    e = np.zeros((LANES, md.wf), np.float32)
    for h in range(md.hf):
        e[md.gate_lane(h), h * HEAD_DIM:(h + 1) * HEAD_DIM] = 1.0
    return jnp.asarray(e)


def _qblocks(L, qb):
    blocks = [(0, BLOCK)]
    r = BLOCK
    while r < L:
        blocks.append((r, qb))
        r += qb
    assert r == L
    return blocks


def _f32dot(a, b):
    return jnp.dot(a, b, precision=HIGHEST, preferred_element_type=F32)


_DIMS_NT = (((1,), (1,)), ((), ()))


def _dot_nt(a, b):
    return lax.dot_general(a, b, _DIMS_NT, preferred_element_type=F32)


def _dot_tn(a, b):
    return jnp.dot(a.T.astype(BF16), b, preferred_element_type=F32)


def _log_sigmoid(z):
    return jnp.minimum(z, 0.0) - jnp.log(1.0 + jnp.exp(-jnp.abs(z)))


def _gate_fwd(u, b, md, *, name):
    L = u.shape[0]
    nb = L // BLOCK
    expand = _gate_expand(md)

    def body(z_ref, b_ref, e_ref, cexp_ref, ct_ref, c_s):
        ri = lax.broadcasted_iota(jnp.int32, (BLOCK, BLOCK), 0)
        ci = lax.broadcasted_iota(jnp.int32, (BLOCK, BLOCK), 1)
        tri = (ri >= ci).astype(F32)
        carry = jnp.zeros((1, LANES), F32)
        for bi in range(nb):
            rows = pl.ds(bi * BLOCK, BLOCK)
            logf = _log_sigmoid(z_ref[rows, :] + b_ref[...])
            blk = _f32dot(tri, logf) + carry
            c_s[rows, :] = blk
            carry = blk[BLOCK - 1:BLOCK, :]
        c = c_s[...]
        ct_ref[...] = c.T
        cexp_ref[...] = _f32dot(c, e_ref[...])

    return pl.pallas_call(
        body, name=name, grid=(1,),
        in_specs=[pl.BlockSpec((L, LANES), lambda i: (0, md.bz)), pl.BlockSpec((1, LANES), lambda i: (0, 0)),
                  pl.BlockSpec((LANES, md.wf), lambda i: (0, 0))],
        out_specs=[pl.BlockSpec((L, md.wf), lambda i: (0, 0)), pl.BlockSpec((LANES, L), lambda i: (0, 0))],
        out_shape=[jax.ShapeDtypeStruct((L, md.wf), F32), jax.ShapeDtypeStruct((LANES, L), F32)],
        scratch_shapes=[pltpu.VMEM((L, LANES), F32)],
        compiler_params=_cparams("arbitrary"),
    )(u, b, expand)


def _gate_bwd(u, b, dck_t, md, *, name):
    L = u.shape[0]
    nb = L // BLOCK

    def body(z_ref, b_ref, dck_ref, dz_ref, db_ref, dc_s):
        ri = lax.broadcasted_iota(jnp.int32, (BLOCK, BLOCK), 0)
        ci = lax.broadcasted_iota(jnp.int32, (BLOCK, BLOCK), 1)
        triu = (ri <= ci).astype(F32)
        dc_s[...] = -dck_ref[...].T
        carry = jnp.zeros((1, LANES), F32)
        db = jnp.zeros((1, LANES), F32)
        for bi in range(nb - 1, -1, -1):
            rows = pl.ds(bi * BLOCK, BLOCK)
            blk = _f32dot(triu, dc_s[rows, :]) + carry
            carry = blk[0:1, :]
            z = z_ref[rows, :] + b_ref[...]
            dz = blk * jax.nn.sigmoid(-z)
            if bi == 0:
                dz = jnp.where(lax.broadcasted_iota(jnp.int32, (BLOCK, LANES), 0) >= PAD, dz, 0.0)
            dz_ref[rows, :] = dz
            db = db + jnp.sum(dz, axis=0, keepdims=True)
        db_ref[...] = db

    return pl.pallas_call(
        body, name=name, grid=(1,),
        in_specs=[pl.BlockSpec((L, LANES), lambda i: (0, md.bz)), pl.BlockSpec((1, LANES), lambda i: (0, 0)),
                  pl.BlockSpec((LANES, L), lambda i: (0, 0))],
        out_specs=[pl.BlockSpec((L, LANES), lambda i: (0, 0)), pl.BlockSpec((1, LANES), lambda i: (0, 0))],
        out_shape=[jax.ShapeDtypeStruct((L, LANES), F32), jax.ShapeDtypeStruct((1, LANES), F32)],
        scratch_shapes=[pltpu.VMEM((L, LANES), F32)],
        compiler_params=_cparams("arbitrary"),
    )(u, b, dck_t)


def _head_norm(x, g, bd):
    r = lax.rsqrt(_f32dot(x * x, bd) + EPS)
    xh = x * r
    return xh * g, xh, r


def _head_norm_bwd(dy, xh, r, g, bd):
    dxh = dy * g
    dx = r * (dxh - xh * _f32dot(dxh * xh, bd))
    return dx, jnp.sum(dy * xh, axis=0, keepdims=True)


def _lane_half():
    return lax.broadcasted_iota(jnp.int32, (1, LANES), 1) < HEAD_DIM


def _store_head_pair(x, half, a_s, b_s):
    a_s[...] = jnp.where(half, x, 0.0).astype(BF16)
    b_s[...] = jnp.where(half, 0.0, x).astype(BF16)


def _fox_scores(qm, kn_s, cexp_ref, ct_ref, r0, nr, klen, hh):
    s = _dot_nt(qm, kn_s[0:klen, :]) * (HEAD_DIM ** -0.5)
    s = s + cexp_ref[r0:r0 + nr, HEAD_DIM * hh:HEAD_DIM * hh + 1] - ct_ref[hh:hh + 1, 0:klen]
    qp = r0 + lax.broadcasted_iota(jnp.int32, (nr, klen), 0)
    kp = lax.broadcasted_iota(jnp.int32, (nr, klen), 1)
    return jnp.where((kp <= qp) & (kp >= PAD), s, NEG_INF)


def _fox_fwd(u, cexp, ct3, qg, kg, md, *, name):
    L = u.shape[0]
    blocks = _qblocks(L, QBLOCK)
    bd = _block_diag_mean()

    def body(q_ref, k_ref, v_ref, cexp_ref, ct_ref, qg_ref, kg_ref, bd_ref, o_ref, lse_ref, qa_s, qb_s, kn_s, v_s):
        half = _lane_half()
        bdv = bd_ref[...]
        _store_head_pair(_head_norm(q_ref[...], qg_ref[...], bdv)[0], half, qa_s, qb_s)
        kn_s[...] = _head_norm(k_ref[...], kg_ref[...], bdv)[0].astype(BF16)
        v_s[...] = v_ref[...].astype(BF16)
        for r0, nr in blocks:
            klen = r0 + nr
            o_blk = lse_blk = None
            for hh, q_s in enumerate((qa_s, qb_s)):
                s = _fox_scores(q_s[r0:r0 + nr, :], kn_s, cexp_ref, ct_ref, r0, nr, klen, hh)
                m = jnp.max(s, axis=-1, keepdims=True)
                p = jnp.exp(s - m)
                l = jnp.sum(p, axis=-1, keepdims=True)
                oh = jnp.dot(p.astype(BF16), v_s[0:klen, :], preferred_element_type=F32) * (1.0 / l)
                lh = jnp.broadcast_to(jnp.where(m > 0.5 * NEG_INF, m + jnp.log(l), 0.0), (nr, LANES))
                o_blk = oh if hh == 0 else jnp.where(half, o_blk, oh)
                lse_blk = lh if hh == 0 else jnp.where(half, lse_blk, lh)
            o_ref[r0:r0 + nr, :] = o_blk
            lse_ref[r0:r0 + nr, :] = lse_blk

    col = lambda base: pl.BlockSpec((L, LANES), lambda j: (0, base + j))
    vec = pl.BlockSpec((1, LANES), lambda j: (0, 0))
    return pl.pallas_call(
        body, name=name, grid=(md.pf,),
        in_specs=[col(md.bq_f), col(md.bk_f), col(md.bv_f), col(0),
                  pl.BlockSpec((None, 8, L), lambda j: (j, 0, 0)), vec, vec,
                  pl.BlockSpec((LANES, LANES), lambda j: (0, 0))],
        out_specs=[col(0), col(0)],
        out_shape=[jax.ShapeDtypeStruct((L, md.wf), F32)] * 2,
        scratch_shapes=[pltpu.VMEM((L, LANES), BF16)] * 4,
        compiler_params=_cparams("parallel"),
    )(u, u, u, cexp, ct3, qg, kg, bd)


def _softmax_bwd(p, dp):
    pdp = p * dp
    return pdp - p * jnp.sum(pdp, axis=-1, keepdims=True)


def _fox_bwd(u, cexp, ct3, qg, kg, do, lse, md, *, name):
    L = u.shape[0]
    blocks = _qblocks(L, QBLOCK // 2)
    bd = _block_diag_mean()
    fold = _fold_halves()
    npairs = md.pf

    def body(q_ref, k_ref, v_ref, cexp_ref, ct_ref, qg_ref, kg_ref, bd_ref, fold_ref, do_ref, lse_ref,
             dq_ref, dk_ref, dv_ref, dck_ref, dqg_ref, dkg_ref,
             qa_s, qb_s, kn_s, v_s, doa_s, dob_s, dqn_s, dkn_s, dvv_s):
        j = pl.program_id(0)
        half = _lane_half()
        bdv = bd_ref[...]
        qy, qh, rq = _head_norm(q_ref[...], qg_ref[...], bdv)
        ky, kh, rk = _head_norm(k_ref[...], kg_ref[...], bdv)
        _store_head_pair(qy, half, qa_s, qb_s)
        _store_head_pair(do_ref[...], half, doa_s, dob_s)
        kn_s[...] = ky.astype(BF16)
        v_s[...] = v_ref[...].astype(BF16)
        dkn_s[...] = jnp.zeros_like(dkn_s)
        dvv_s[...] = jnp.zeros_like(dvv_s)
        dck_ref[...] = jnp.zeros_like(dck_ref)
        for r0, nr in blocks:
            klen = r0 + nr
            dq_blk = None
            for hh, (q_s, do_s) in enumerate(((qa_s, doa_s), (qb_s, dob_s))):
                c0 = HEAD_DIM * hh
                qm = q_s[r0:r0 + nr, :]
                dom = do_s[r0:r0 + nr, :]
                s = _fox_scores(qm, kn_s, cexp_ref, ct_ref, r0, nr, klen, hh)
                p = jnp.exp(s - lse_ref[r0:r0 + nr, c0:c0 + 1])
                ds = _softmax_bwd(p, _dot_nt(dom, v_s[0:klen, :]))
                dck_ref[hh:hh + 1, 0:klen] += jnp.sum(ds, axis=0, keepdims=True)
                ds = ds * (HEAD_DIM ** -0.5)
                dq_h = jnp.dot(ds.astype(BF16), kn_s[0:klen, :], preferred_element_type=F32)
                dkn_s[0:klen, :] += _dot_tn(ds, qm)
                dvv_s[0:klen, :] += _dot_tn(p, dom)
                dq_blk = dq_h if hh == 0 else jnp.where(half, dq_blk, dq_h)
            dqn_s[r0:r0 + nr, :] = dq_blk
        dq, dqg = _head_norm_bwd(dqn_s[...], qh, rq, qg_ref[...], bdv)
        dk, dkg = _head_norm_bwd(dkn_s[...], kh, rk, kg_ref[...], bdv)
        dq_ref[...] = dq
        dk_ref[...] = dk
        dv_ref[...] = dvv_s[...]

        @pl.when(j == 0)
        def _():
            dqg_ref[...] = jnp.zeros_like(dqg_ref)
            dkg_ref[...] = jnp.zeros_like(dkg_ref)

        dqg_ref[...] += dqg
        dkg_ref[...] += dkg

        @pl.when(j == npairs - 1)
        def _():
            dqg_ref[...] = _f32dot(jnp.broadcast_to(dqg_ref[...], (8, LANES)), fold_ref[...])[0:1, :]
            dkg_ref[...] = _f32dot(jnp.broadcast_to(dkg_ref[...], (8, LANES)), fold_ref[...])[0:1, :]

    col = lambda base: pl.BlockSpec((L, LANES), lambda j: (0, base + j))
    vec = pl.BlockSpec((1, LANES), lambda j: (0, 0))
    sq = pl.BlockSpec((LANES, LANES), lambda j: (0, 0))
    ct_spec = pl.BlockSpec((None, 8, L), lambda j: (j, 0, 0))
    big = jax.ShapeDtypeStruct((L, md.wf), F32)
    small = jax.ShapeDtypeStruct((1, LANES), F32)
    return pl.pallas_call(
        body, name=name, grid=(md.pf,),
        in_specs=[col(md.bq_f), col(md.bk_f), col(md.bv_f), col(0), ct_spec, vec, vec, sq, sq, col(0), col(0)],
        out_specs=[col(0), col(0), col(0), ct_spec, vec, vec],
        out_shape=[big, big, big, jax.ShapeDtypeStruct((md.pf, 8, L), F32), small, small],
        scratch_shapes=[pltpu.VMEM((L, LANES), BF16)] * 6 + [pltpu.VMEM((L, LANES), F32)] * 3,
        compiler_params=_cparams("arbitrary"),
    )(u, u, u, cexp, ct3, qg, kg, bd, fold, do, lse)


def _swa_scores(qm, kn_s, slope, k0, r0, nr, klen):
    s = _dot_nt(qm, kn_s[k0:k0 + klen, :]) * (HEAD_DIM ** -0.5)
    qp = r0 + lax.broadcasted_iota(jnp.int32, (nr, klen), 0)
    kp = k0 + lax.broadcasted_iota(jnp.int32, (nr, klen), 1)
    dist = qp - kp
    s = s - slope * dist.astype(F32)
    return jnp.where((dist >= 0) & (dist < WINDOW) & (kp >= PAD), s, NEG_INF)


def _swa_fwd(u, sinkp, slopep, qg, kg, md, *, name):
    L = u.shape[0]
    blocks = _qblocks(L, QBLOCK)
    bd = _block_diag_mean()

    def body(q_ref, k_ref, v_ref, sink_ref, slope_ref, qg_ref, kg_ref, bd_ref, o_ref, lse_ref, qa_s, qb_s, kn_s, v_s):
        half = _lane_half()
        bdv = bd_ref[...]
        _store_head_pair(_head_norm(q_ref[...], qg_ref[...], bdv)[0], half, qa_s, qb_s)
        kn_s[...] = _head_norm(k_ref[...], kg_ref[...], bdv)[0].astype(BF16)
        v_s[...] = v_ref[...].astype(BF16)
        for r0, nr in blocks:
            k0 = max(r0 - BLOCK, 0)
            klen = r0 + nr - k0
            o_blk = lse_blk = None
            for hh, q_s in enumerate((qa_s, qb_s)):
                c0 = HEAD_DIM * hh
                s = _swa_scores(q_s[r0:r0 + nr, :], kn_s, slope_ref[0:1, c0:c0 + 1], k0, r0, nr, klen)
                sink = sink_ref[0:1, c0:c0 + 1]
                m = jnp.maximum(jnp.max(s, axis=-1, keepdims=True), sink)
                p = jnp.exp(s - m)
                den = jnp.sum(p, axis=-1, keepdims=True) + jnp.exp(sink - m)
                oh = jnp.dot(p.astype(BF16), v_s[k0:k0 + klen, :], preferred_element_type=F32) * (1.0 / den)
                lh = jnp.broadcast_to(m + jnp.log(den), (nr, LANES))
                o_blk = oh if hh == 0 else jnp.where(half, o_blk, oh)
                lse_blk = lh if hh == 0 else jnp.where(half, lse_blk, lh)
            o_ref[r0:r0 + nr, :] = o_blk
            lse_ref[r0:r0 + nr, :] = lse_blk

    g2 = md.g // 2
    qcol = pl.BlockSpec((L, LANES), lambda j: (0, md.bq_s + j))
    kcol = pl.BlockSpec((L, LANES), lambda j: (0, md.bk_s + j // g2))
    vcol = pl.BlockSpec((L, LANES), lambda j: (0, md.bv_s + j // g2))
    ocol = pl.BlockSpec((L, LANES), lambda j: (0, j))
    pvec = pl.BlockSpec((1, LANES), lambda j: (0, j))
    vec = pl.BlockSpec((1, LANES), lambda j: (0, 0))
    return pl.pallas_call(
        body, name=name, grid=(md.ps,),
        in_specs=[qcol, kcol, vcol, pvec, pvec, vec, vec, pl.BlockSpec((LANES, LANES), lambda j: (0, 0))],
        out_specs=[ocol, ocol],
        out_shape=[jax.ShapeDtypeStruct((L, md.ws), F32)] * 2,
        scratch_shapes=[pltpu.VMEM((L, LANES), BF16)] * 4,
        compiler_params=_cparams("parallel"),
    )(u, u, u, sinkp, slopep, qg, kg, bd)


def _swa_bwd(u, sinkp, slopep, qg, kg, do, lse, md, *, name):
    L = u.shape[0]
    blocks = _qblocks(L, QBLOCK // 2)
    bd = _block_diag_mean()
    fold = _fold_halves()
    g2 = md.g // 2
    npairs = md.ps

    def body(q_ref, k_ref, v_ref, sink_ref, slope_ref, qg_ref, kg_ref, bd_ref, fold_ref, do_ref, lse_ref,
             dq_ref, dk_ref, dv_ref, dsink_ref, dqg_ref, dkg_ref,
             qa_s, qb_s, kn_s, v_s, doa_s, dob_s, dqn_s):
        j = pl.program_id(0)
        half = _lane_half()
        bdv = bd_ref[...]
        qy, qh, rq = _head_norm(q_ref[...], qg_ref[...], bdv)
        ky, kh, rk = _head_norm(k_ref[...], kg_ref[...], bdv)
        _store_head_pair(qy, half, qa_s, qb_s)
        _store_head_pair(do_ref[...], half, doa_s, dob_s)
        kn_s[...] = ky.astype(BF16)
        v_s[...] = v_ref[...].astype(BF16)

        @pl.when(j % g2 == 0)
        def _():
            dk_ref[...] = jnp.zeros_like(dk_ref)
            dv_ref[...] = jnp.zeros_like(dv_ref)

        @pl.when(j == 0)
        def _():
            dqg_ref[...] = jnp.zeros_like(dqg_ref)
            dkg_ref[...] = jnp.zeros_like(dkg_ref)

        dsink = [jnp.zeros((1, 1), F32), jnp.zeros((1, 1), F32)]
        for r0, nr in blocks:
            k0 = max(r0 - BLOCK, 0)
            klen = r0 + nr - k0
            dq_blk = None
            for hh, (q_s, do_s) in enumerate(((qa_s, doa_s), (qb_s, dob_s))):
                c0 = HEAD_DIM * hh
                qm = q_s[r0:r0 + nr, :]
                dom = do_s[r0:r0 + nr, :]
                s = _swa_scores(qm, kn_s, slope_ref[0:1, c0:c0 + 1], k0, r0, nr, klen)
                lse_h = lse_ref[r0:r0 + nr, c0:c0 + 1]
                p = jnp.exp(s - lse_h)
                pdp = p * _dot_nt(dom, v_s[k0:k0 + klen, :])
                delta = jnp.sum(pdp, axis=-1, keepdims=True)
                p_sink = jnp.exp(sink_ref[0:1, c0:c0 + 1] - lse_h)
                dsink[hh] = dsink[hh] - jnp.sum(p_sink * delta, axis=0, keepdims=True)
                ds = (pdp - p * delta) * (HEAD_DIM ** -0.5)
                dq_h = jnp.dot(ds.astype(BF16), kn_s[k0:k0 + klen, :], preferred_element_type=F32)
                dk_ref[k0:k0 + klen, :] += _dot_tn(ds, qm)
                dv_ref[k0:k0 + klen, :] += _dot_tn(p, dom)
                dq_blk = dq_h if hh == 0 else jnp.where(half, dq_blk, dq_h)
            dqn_s[r0:r0 + nr, :] = dq_blk
        dq, dqg = _head_norm_bwd(dqn_s[...], qh, rq, qg_ref[...], bdv)
        dq_ref[...] = dq
        dqg_ref[...] += dqg
        dsink_ref[...] = jnp.where(half, jnp.broadcast_to(dsink[0], (1, LANES)), jnp.broadcast_to(dsink[1], (1, LANES)))

        @pl.when(j % g2 == g2 - 1)
        def _():
            dkn = _f32dot(dk_ref[...], fold_ref[...])
            dk, dkg = _head_norm_bwd(dkn, kh, rk, kg_ref[...], bdv)
            dk_ref[...] = jnp.where(half, dk, 0.0)
            dv_ref[...] = jnp.where(half, _f32dot(dv_ref[...], fold_ref[...]), 0.0)
            dkg_ref[...] += dkg

        @pl.when(j == npairs - 1)
        def _():
            dqg_ref[...] = _f32dot(jnp.broadcast_to(dqg_ref[...], (8, LANES)), fold_ref[...])[0:1, :]

    qcol = pl.BlockSpec((L, LANES), lambda j: (0, md.bq_s + j))
    kcol = pl.BlockSpec((L, LANES), lambda j: (0, md.bk_s + j // g2))
    vcol = pl.BlockSpec((L, LANES), lambda j: (0, md.bv_s + j // g2))
    ocol = pl.BlockSpec((L, LANES), lambda j: (0, j))
    kvout = pl.BlockSpec((L, LANES), lambda j: (0, j // g2))
    pvec = pl.BlockSpec((1, LANES), lambda j: (0, j))
    vec = pl.BlockSpec((1, LANES), lambda j: (0, 0))
    sq = pl.BlockSpec((LANES, LANES), lambda j: (0, 0))
    kvshape = jax.ShapeDtypeStruct((L, LANES * md.nkv), F32)
    small = jax.ShapeDtypeStruct((1, LANES), F32)
    return pl.pallas_call(
        body, name=name, grid=(md.ps,),
        in_specs=[qcol, kcol, vcol, pvec, pvec, vec, vec, sq, sq, ocol, ocol],
        out_specs=[ocol, kvout, kvout, pvec, vec, vec],
        out_shape=[jax.ShapeDtypeStruct((L, md.ws), F32), kvshape, kvshape,
                   jax.ShapeDtypeStruct((1, md.ws), F32), small, small],
        scratch_shapes=[pltpu.VMEM((L, LANES), BF16)] * 6 + [pltpu.VMEM((L, LANES), F32)],
        compiler_params=_cparams("arbitrary"),
    )(u, u, u, sinkp, slopep, qg, kg, bd, fold, do, lse)


def _outnorm_fwd(of, os_, gf, gs, *, name):
    L, wf = of.shape
    ws = os_.shape[1]
    tr = _pick(L, _TR)

    def body(of_ref, os_ref, gf_ref, gs_ref, o_ref):
        for src, g_ref, c0, w in ((of_ref, gf_ref, 0, wf), (os_ref, gs_ref, wf, ws)):
            x = src[...]
            r = lax.rsqrt(jnp.mean(x * x, axis=-1, keepdims=True) + EPS)
            o_ref[:, c0:c0 + w] = (x * r * g_ref[...]).astype(o_ref.dtype)

    return pl.pallas_call(
        body, name=name, grid=(L // tr,),
        in_specs=[pl.BlockSpec((tr, wf), lambda i: (i, 0)), pl.BlockSpec((tr, ws), lambda i: (i, 0)),
                  pl.BlockSpec((1, wf), lambda i: (0, 0)), pl.BlockSpec((1, ws), lambda i: (0, 0))],
        out_specs=pl.BlockSpec((tr, wf + ws), lambda i: (i, 0)),
        out_shape=jax.ShapeDtypeStruct((L, wf + ws), BF16),
        compiler_params=_cparams("parallel"),
    )(of, os_, gf, gs)


def _outnorm_bwd(don, of, os_, gf, gs, *, name):
    L, wf = of.shape
    ws = os_.shape[1]
    tr = _pick(L, _TR)

    def body(d_ref, of_ref, os_ref, gf_ref, gs_ref, dof_ref, dos_ref, dgf_ref, dgs_ref):
        i = pl.program_id(0)
        for src, g_ref, c0, w, dx_ref, dg_ref in ((of_ref, gf_ref, 0, wf, dof_ref, dgf_ref),
                                                  (os_ref, gs_ref, wf, ws, dos_ref, dgs_ref)):
            x = src[...]
            dy = d_ref[:, c0:c0 + w]
            r = lax.rsqrt(jnp.mean(x * x, axis=-1, keepdims=True) + EPS)
            xh = x * r
            dxh = dy * g_ref[...]
            dx_ref[...] = r * (dxh - xh * jnp.mean(dxh * xh, axis=-1, keepdims=True))
            part = jnp.sum(dy * xh, axis=0, keepdims=True)

            @pl.when(i == 0)
            def _():
                dg_ref[...] = part

            @pl.when(i > 0)
            def _():
                dg_ref[...] += part

    rf = pl.BlockSpec((tr, wf), lambda i: (i, 0))
    rs = pl.BlockSpec((tr, ws), lambda i: (i, 0))
    vf = pl.BlockSpec((1, wf), lambda i: (0, 0))
    vs = pl.BlockSpec((1, ws), lambda i: (0, 0))
    return pl.pallas_call(
        body, name=name, grid=(L // tr,),
        in_specs=[pl.BlockSpec((tr, wf + ws), lambda i: (i, 0)), rf, rs, vf, vs],
        out_specs=[rf, rs, vf, vs],
        out_shape=[jax.ShapeDtypeStruct((L, wf), F32), jax.ShapeDtypeStruct((L, ws), F32),
                   jax.ShapeDtypeStruct((1, wf), F32), jax.ShapeDtypeStruct((1, ws), F32)],
        compiler_params=_cparams("arbitrary"),
    )(don, of, os_, gf, gs)


def _win_to_mine(w, md):
    d = w.shape[0]
    wf, ws, hd = md.wf, md.ws, HEAD_DIM
    o_z = 3 * wf
    o_sq = o_z + md.hf
    o_sk = o_sq + ws
    o_sv = o_sk + md.nkv * hd
    parts = [w[:, :3 * wf], w[:, o_sq:o_sq + ws]]
    for base in (o_sk, o_sv):
        for kv in range(md.nkv):
            blk = w[:, base + kv * hd:base + (kv + 1) * hd]
            parts += [blk, blk]
    z = w[:, o_z:o_z + md.hf].reshape(d, md.hf // 2, 2)
    z = jnp.pad(z, ((0, 0), (0, 0), (0, 6))).reshape(d, 4 * md.hf)
    parts.append(jnp.pad(z, ((0, 0), (0, LANES - 4 * md.hf + md.nup - md.nu))))
    return jnp.concatenate(parts, axis=1)


def _win_grad_to_ref(dw, md):
    d = dw.shape[0]
    wf, ws, hd = md.wf, md.ws, HEAD_DIM
    z = dw[:, md.bz * LANES:md.bz * LANES + 4 * md.hf].reshape(d, md.hf // 2, 8)[:, :, :2].reshape(d, md.hf)
    parts = [dw[:, :3 * wf], z, dw[:, md.bq_s * LANES:md.bq_s * LANES + ws]]
    for base in (md.bk_s, md.bv_s):
        for kv in range(md.nkv):
            c0 = (base + kv) * LANES
            parts.append(dw[:, c0:c0 + hd])
    return jnp.concatenate(parts, axis=1)


def _mix_small(p, md):
    tile2 = lambda v: jnp.tile(v.reshape(1, HEAD_DIM), (1, 2))
    b = p["b_forget"].reshape(md.hf // 2, 2)
    b = jnp.pad(b, ((0, 0), (0, 6))).reshape(1, 4 * md.hf)
    slopes = np.asarray(2.0 ** (-8.0 * np.arange(1, md.hq + 1) / md.hq), np.float32)
    return dict(
        g_mix=p["mix_norm"].reshape(1, -1),
        b_gate=jnp.pad(b, ((0, 0), (0, LANES - 4 * md.hf))),
        fqg=tile2(p["fox_q_norm"]), fkg=tile2(p["fox_k_norm"]),
        sqg=tile2(p["swa_q_norm"]), skg=tile2(p["swa_k_norm"]),
        sinkp=jnp.repeat(p["swa_sinks"], HEAD_DIM).reshape(1, md.ws),
        slopep=jnp.asarray(np.repeat(slopes, HEAD_DIM).reshape(1, md.ws)),
        gfo=p["fox_out_norm"].reshape(1, md.wf), gso=p["swa_out_norm"].reshape(1, md.ws),
    )


def _mix_fwd(h, sp, w_in, w_out, md):
    L = h.shape[0]
    hn = _rms_fwd(h, sp["g_mix"], name="mix_rms")
    u = _matmul(hn, w_in, name="mix_u")
    cexp, ct = _gate_fwd(u, sp["b_gate"], md, name="gate_fwd")
    ct3 = ct[:8 * md.pf].reshape(md.pf, 8, L)
    of, lsef = _fox_fwd(u, cexp, ct3, sp["fqg"], sp["fkg"], md, name="fox_fwd")
    os_, lses = _swa_fwd(u, sp["sinkp"], sp["slopep"], sp["sqg"], sp["skg"], md, name="swa_fwd")
    on = _outnorm_fwd(of, os_, sp["gfo"], sp["gso"], name="outnorm_fwd")
    h_out = _matmul(on, w_out, name="mix_out", residual=h)
    return h_out, (h, hn, u, cexp, ct3, of, lsef, os_, lses, on)


def _mix_bwd(dh, saved, sp, w_in, w_out, md, queue=None):
    h, hn, u, cexp, ct3, of, lsef, os_, lses, on = saved
    L = h.shape[0]
    dhb = _cast_bf16(dh, name="mix_dhb")
    don = _mm(queue, dhb, w_out, name="mix_don", nt=True)
    dw_out = _mm(queue, _transpose(on, name="mix_onT"), dhb, name="mix_dwout", out_dtype=BF16)
    dof, dos, dgfo, dgso = _outnorm_bwd(don, of, os_, sp["gfo"], sp["gso"], name="outnorm_bwd")
    duq, duk, duv, dck, dfqg, dfkg = _fox_bwd(u, cexp, ct3, sp["fqg"], sp["fkg"], dof, lsef, md, name="fox_bwd")
    dsq, dsk, dsv, dsinkp, dsqg, dskg = _swa_bwd(u, sp["sinkp"], sp["slopep"], sp["sqg"], sp["skg"], dos, lses, md, name="swa_bwd")
    dck_t = jnp.pad(dck.reshape(8 * md.pf, L), ((0, LANES - 8 * md.pf), (0, 0)))
    dz, db = _gate_bwd(u, sp["b_gate"], dck_t, md, name="gate_bwd")
    du = jnp.concatenate([duq, duk, duv, dsq, dsk, dsv, dz, jnp.zeros((L, md.nup - md.nu), F32)], axis=1).astype(BF16)
    dhn = _mm(queue, du, w_in, name="mix_dhn", nt=True)
    dw_in = _mm(queue, _transpose(hn, name="mix_hnT"), du, name="mix_dwin", out_dtype=BF16)
    dh_in, dg_mix = _rms_bwd(dhn, h, sp["g_mix"], dh, name="mix_drms")
    small = dict(
        mix_norm=dg_mix.reshape(-1),
        b_forget=db[0, :4 * md.hf].reshape(md.hf // 2, 8)[:, :2].reshape(md.hf),
        fox_q_norm=dfqg[0, :HEAD_DIM], fox_k_norm=dfkg[0, :HEAD_DIM],
        swa_q_norm=dsqg[0, :HEAD_DIM], swa_k_norm=dskg[0, :HEAD_DIM],
        swa_sinks=dsinkp[0, ::HEAD_DIM],
        fox_out_norm=dgfo.reshape(-1), swa_out_norm=dgso.reshape(-1),
    )
    return dh_in, dw_in, dw_out, small


_ANY = pl.BlockSpec(memory_space=pl.ANY)
_HALF_ROWS = (512, 352, 256, 192, 128, 64, 32, 16)


def _mesh_pos():
    return lax.axis_index("x"), lax.axis_index("y"), lax.axis_index("c")


def _other_chips(x, y):
    return [(1 - x, y), (x, 1 - y), (1 - x, 1 - y)]


def _rows_half(ref, which):
    rh = ref.shape[-2] // 2
    if len(ref.shape) == 2:
        return ref.at[pl.ds(which * rh, rh), :]
    return ref.at[:, pl.ds(which * rh, rh), :]


def _remote(src, dst, send_sems, recv_sems, idx, dev):
    return pltpu.make_async_remote_copy(src_ref=src, dst_ref=dst, send_sem=send_sems.at[idx], recv_sem=recv_sems.at[idx],
                                        device_id=dev, device_id_type=MESH)


def _cast_into_chunk(w, l_idx, k_idx, *, name):
    _, rows, cols = w.shape
    tr = _pick(rows, (512, 352, 256, 128, 64, 32, 16))

    def body(l_ref, k_ref, w_ref, o_ref):
        o_ref[...] = w_ref[...].astype(o_ref.dtype)

    return pl.pallas_call(
        body, name=name,
        grid_spec=pltpu.PrefetchScalarGridSpec(
            num_scalar_prefetch=2, grid=(rows // tr,),
            in_specs=[pl.BlockSpec((None, tr, cols), lambda i, l, k: (l[0], i, 0))],
            out_specs=pl.BlockSpec((None, tr, cols), lambda i, l, k: (k[0], i, 0))),
        out_shape=jax.ShapeDtypeStruct((NCH, rows, cols), BF16),
        compiler_params=_cparams("parallel"),
    )(l_idx, k_idx, w)


def _allgather_chips(arrs, *, name):
    n = len(arrs)

    def body(*refs):
        outs = refs[n:2 * n]
        send_sems, recv_sems = refs[2 * n:]
        x, y, c = _mesh_pos()
        k = 2 * x + y
        sibling = (x, y, 1 - c)
        chips = _other_chips(x, y)
        sent = []
        for i in range(n):
            mine = _rows_half(outs[i].at[k], c)
            for j, (cx, cy) in enumerate(chips):
                cp = _remote(mine, mine, send_sems, recv_sems, 6 * i + j, (cx, cy, c))
                cp.start()
                sent.append(cp)
        for i in range(n):
            for j, (cx, cy) in enumerate(chips):
                blk = _rows_half(outs[i].at[2 * cx + cy], c)
                _remote(blk, blk, send_sems, recv_sems, 6 * i + j, (cx, cy, c)).wait_recv()
                fwd = _remote(blk, blk, send_sems, recv_sems, 6 * i + 3 + j, sibling)
                fwd.start()
                sent.append(fwd)
        for i in range(n):
            for j, (cx, cy) in enumerate(chips):
                blk = _rows_half(outs[i].at[2 * cx + cy], 1 - c)
                _remote(blk, blk, send_sems, recv_sems, 6 * i + 3 + j, sibling).wait_recv()
        for cp in sent:
            cp.wait_send()

    return pl.pallas_call(
        body, name=name, in_specs=[_ANY] * n, out_specs=[_ANY] * n,
        out_shape=[jax.ShapeDtypeStruct(a.shape, a.dtype) for a in arrs],
        input_output_aliases={i: i for i in range(n)},
        scratch_shapes=[pltpu.SemaphoreType.DMA((6 * n,)), pltpu.SemaphoreType.DMA((6 * n,))],
    )(*arrs)


def _exchange_sibling_halves(arrs, *, name):
    n = len(arrs)

    def body(*refs):
        srcs, lands = refs[:n], refs[n:2 * n]
        send_sems, recv_sems = refs[2 * n:]
        x, y, c = _mesh_pos()
        sibling = (x, y, 1 - c)
        cps = [_remote(_rows_half(srcs[i], 1 - c), lands[i], send_sems, recv_sems, i, sibling) for i in range(n)]
        for cp in cps:
            cp.start()
        for cp in cps:
            cp.wait_recv()
        for cp in cps:
            cp.wait_send()

    return pl.pallas_call(
        body, name=name, in_specs=[_ANY] * n, out_specs=[_ANY] * n,
        out_shape=[jax.ShapeDtypeStruct((NCH, a.shape[1] // 2, a.shape[2]), a.dtype) for a in arrs],
        scratch_shapes=[pltpu.SemaphoreType.DMA((n,)), pltpu.SemaphoreType.DMA((n,))],
    )(*arrs)


class _ChipExchange:
    n_sems = NCH - 1

    def __init__(self, s, r0, r1, land=None):
        self.r0, self.r1 = r0, r1
        self.operands = [s] if land is None else [s, land]
        self.out_shapes = [jax.ShapeDtypeStruct((NCH - 1,) + s.shape[1:], s.dtype)]
        self.aliases = {} if land is None else {1: 0}

    def copies(self, cin, cout, send_sems, recv_sems):
        x, y, c = _mesh_pos()
        rows = pl.ds(self.r0, self.r1 - self.r0)
        return [_remote(cin[0].at[2 * cx + cy, rows, :], cout[0].at[j, rows, :], send_sems, recv_sems, j, (cx, cy, c))
                for j, (cx, cy) in enumerate(_other_chips(x, y))]


def _run_exchange(job, *, name):
    n_in, n_out = len(job.operands), len(job.out_shapes)

    def body(*refs):
        cps = job.copies(refs[:n_in], refs[n_in:n_in + n_out], refs[-2], refs[-1])
        for cp in cps:
            cp.start()
        for cp in cps:
            cp.wait_recv()
        for cp in cps:
            cp.wait_send()

    return pl.pallas_call(
        body, name=name, in_specs=[_ANY] * n_in, out_specs=[_ANY] * n_out, out_shape=list(job.out_shapes),
        input_output_aliases=dict(job.aliases),
        scratch_shapes=[pltpu.SemaphoreType.DMA((job.n_sems,)), pltpu.SemaphoreType.DMA((job.n_sems,))],
    )(*job.operands)


class _ExchangeQueue:
    def __init__(self, parts=2):
        self.parts = parts
        self.todo = []
        self.groups = {}

    def add(self, group, sums, done):
        self.groups[group] = dict(sums=sums, lands=[None] * len(sums), left=len(sums) * self.parts, done=done)
        self.todo += [(group, ai, p) for ai in range(len(sums)) for p in range(self.parts)]

    def take(self):
        if not self.todo:
            return None
        group, ai, p = self.cur = self.todo.pop(0)
        g = self.groups[group]
        step = g['sums'][ai].shape[1] // self.parts
        return _ChipExchange(g['sums'][ai], p * step, (p + 1) * step, g['lands'][ai])

    def give(self, outs):
        group, ai, _ = self.cur
        g = self.groups[group]
        g['lands'][ai] = outs[0]
        g['left'] -= 1

    def finish_ready(self):
        for group in [k for k, g in self.groups.items() if g['left'] == 0]:
            g = self.groups.pop(group)
            g['done'](g['sums'], g['lands'])

    def drain(self, name):
        while (job := self.take()) is not None:
            self.give(_run_exchange(job, name=name))
        self.finish_ready()


def _mm(queue, a, b, **kw):
    job = queue.take() if queue is not None else None
    if job is None:
        return _matmul(a, b, **kw)
    out, *landed = _matmul(a, b, comm=job, **kw)
    queue.give(landed)
    return out


def _share_with_sibling(arrs, *, name):
    n = len(arrs)

    def body(*refs):
        outs = refs[n:2 * n]
        send_sems, recv_sems = refs[2 * n:]
        x, y, c = _mesh_pos()
        sibling = (x, y, 1 - c)
        cps = []
        for i in range(n):
            mine = _rows_half(outs[i], c)
            cps.append(_remote(mine, mine, send_sems, recv_sems, i, sibling))
        for cp in cps:
            cp.start()
        for i in range(n):
            theirs = _rows_half(outs[i], 1 - c)
            _remote(theirs, theirs, send_sems, recv_sems, i, sibling).wait_recv()
        for cp in cps:
            cp.wait_send()

    return pl.pallas_call(
        body, name=name, in_specs=[_ANY] * n, out_specs=[_ANY] * n,
        out_shape=[jax.ShapeDtypeStruct(a.shape, a.dtype) for a in arrs],
        input_output_aliases={i: i for i in range(n)},
        scratch_shapes=[pltpu.SemaphoreType.DMA((n,)), pltpu.SemaphoreType.DMA((n,))],
    )(*arrs)


def _allgather_devices(v, *, name):
    m = v.shape[0]

    def body(v_ref, out_ref, send_sems, recv_sems, local_sem):
        x, y, c = _mesh_pos()
        mine = 4 * x + 2 * y + c
        own = pltpu.make_async_copy(v_ref, out_ref.at[mine], local_sem)
        own.start()
        cps = []
        for r in range(1, 8):
            px, py, pc = (x + (r >> 2)) % 2, (y + ((r >> 1) & 1)) % 2, (c + (r & 1)) % 2
            cps.append((_remote(v_ref, out_ref.at[mine], send_sems, recv_sems, r - 1, (px, py, pc)), 4 * px + 2 * py + pc))
        for cp, _ in cps:
            cp.start()
        for r, (cp, theirs) in enumerate(cps):
            blk = out_ref.at[theirs]
            _remote(blk, blk, send_sems, recv_sems, r, (x, y, c)).wait_recv()
        for cp, _ in cps:
            cp.wait_send()
        own.wait()

    return pl.pallas_call(
        body, name=name, in_specs=[_ANY], out_specs=_ANY,
        out_shape=jax.ShapeDtypeStruct((8, m, LANES), v.dtype),
        scratch_shapes=[pltpu.SemaphoreType.DMA((7,)), pltpu.SemaphoreType.DMA((7,)), pltpu.SemaphoreType.DMA],
    )(v)


def _add_own_half(g, land, c_idx, *, name):
    nch, rh, cols = land.shape
    tr = _pick(rh, _HALF_ROWS)
    nt = rh // tr

    def body(c_ref, g_ref, l_ref, o_ref):
        o_ref[...] = (g_ref[...].astype(F32) + l_ref[...].astype(F32)).astype(o_ref.dtype)

    return pl.pallas_call(
        body, name=name,
        grid_spec=pltpu.PrefetchScalarGridSpec(
            num_scalar_prefetch=1, grid=(nch, nt),
            in_specs=[pl.BlockSpec((None, tr, cols), lambda k, i, c: (k, c[0] * nt + i, 0)),
                      pl.BlockSpec((None, tr, cols), lambda k, i, c: (k, i, 0))],
            out_specs=pl.BlockSpec((None, tr, cols), lambda k, i, c: (k, i, 0))),
        out_shape=jax.ShapeDtypeStruct(land.shape, BF16),
        compiler_params=_cparams("parallel", "parallel"),
    )(c_idx, g, land)


def _add_chunks(s, land, k_idx, c_idx, *, name):
    _, rh, cols = s.shape
    tr = _pick(rh, _HALF_ROWS)
    nt = rh // tr

    def body(k_ref, c_ref, s_ref, l_ref, o_ref):
        t = s_ref[...].astype(F32)
        for j in range(NCH - 1):
            t = t + l_ref[j].astype(F32)
        o_ref[...] = t

    return pl.pallas_call(
        body, name=name,
        grid_spec=pltpu.PrefetchScalarGridSpec(
            num_scalar_prefetch=2, grid=(nt,),
            in_specs=[pl.BlockSpec((None, tr, cols), lambda i, k, c: (k[0], i, 0)),
                      pl.BlockSpec((NCH - 1, tr, cols), lambda i, k, c: (0, i, 0))],
            out_specs=pl.BlockSpec((tr, cols), lambda i, k, c: (c[0] * nt + i, 0))),
        out_shape=jax.ShapeDtypeStruct((2 * rh, cols), F32),
        compiler_params=_cparams("parallel"),
    )(k_idx, c_idx, s, land)


def _sum_devices(v, *, name):
    _, m, _ = v.shape

    def body(v_ref, o_ref):
        t = v_ref[0]
        for d in range(1, 8):
            t = t + v_ref[d]
        o_ref[...] = t

    return pl.pallas_call(
        body, name=name, grid=(1,),
        in_specs=[pl.BlockSpec((8, m, LANES), lambda i: (0, 0, 0))],
        out_specs=pl.BlockSpec((m, LANES), lambda i: (0, 0)),
        out_shape=jax.ShapeDtypeStruct((m, LANES), F32),
        compiler_params=_cparams("arbitrary"),
    )(v)


def _adamw_math(w, g, m, v):
    m = ADAM_B1 * m + (1.0 - ADAM_B1) * g
    v = ADAM_B2 * v + (1.0 - ADAM_B2) * (g * g)
    m_hat = m / (1.0 - ADAM_B1 ** ADAM_STEP)
    v_hat = v / (1.0 - ADAM_B2 ** ADAM_STEP)
    delta = -ADAM_LR * (m_hat / (jnp.sqrt(v_hat) + ADAM_EPS) + ADAM_WD * w)
    return delta, m, v


def _adamw_layer(w, m, v, g, layer, prev, *, name):
    depth, rows, cols = w.shape
    tr = _pick(rows, (256, 128, 64, 32, 16, 8))
    lay = pl.BlockSpec((None, tr, cols), lambda i, l: (l[0], i, 0))
    n_prev = 0 if prev is None else 4

    def body(l_ref, w_ref, m_ref, v_ref, g_ref, *rest):
        go_ref, d_ref, mo_ref, vo_ref = rest[n_prev:]
        g = g_ref[...]
        delta, m_new, v_new = _adamw_math(w_ref[...], g, m_ref[...], v_ref[...])
        go_ref[...] = g
        d_ref[...] = delta
        mo_ref[...] = m_new
        vo_ref[...] = v_new

    stack = jax.ShapeDtypeStruct(w.shape, F32)
    return pl.pallas_call(
        body, name=name,
        grid_spec=pltpu.PrefetchScalarGridSpec(
            num_scalar_prefetch=1, grid=(rows // tr,),
            in_specs=[lay, lay, lay, pl.BlockSpec((tr, cols), lambda i, l: (i, 0))] + [_ANY] * n_prev,
            out_specs=[lay] * 4),
        out_shape=[stack] * 4,
        input_output_aliases={} if prev is None else {5 + q: q for q in range(4)},
        compiler_params=_cparams("parallel"),
    )(layer, w, m, v, g, *(() if prev is None else prev))


def _adamw_flat(w, g, m, v, *, name):
    def body(w_ref, g_ref, m_ref, v_ref, d_ref, mo_ref, vo_ref):
        d_ref[...], mo_ref[...], vo_ref[...] = _adamw_math(w_ref[...], g_ref[...], m_ref[...], v_ref[...])

    blk = pl.BlockSpec(w.shape, lambda i: (0, 0))
    return pl.pallas_call(
        body, name=name, grid=(1,), in_specs=[blk] * 4, out_specs=[blk] * 3,
        out_shape=[jax.ShapeDtypeStruct(w.shape, F32)] * 3, compiler_params=_cparams("arbitrary"),
    )(w, g, m, v)


def _reduce_scatter_begin(parts, c_idx, tag):
    lands = _exchange_sibling_halves(parts, name=f"{tag}_rs_sibling")
    return [_add_own_half(p, l, c_idx, name=f"{tag}_rs_add2") for p, l in zip(parts, lands)]


def _reduce_scatter_end(sums, lands, c_idx, k_idx, tag):
    tots = [_add_chunks(s, l, k_idx, c_idx, name=f"{tag}_rs_add4") for s, l in zip(sums, lands)]
    return _share_with_sibling(tots, name=f"{tag}_rs_share")


_WEIGHTS = ('meta_tokens', 'ffn1_norm', 'ffn1_w_gate', 'ffn1_w_up', 'ffn1_w_down', 'mix_norm', 'w_in', 'b_forget',
            'fox_q_norm', 'fox_k_norm', 'swa_q_norm', 'swa_k_norm', 'swa_sinks', 'fox_out_norm', 'swa_out_norm', 'w_out',
            'ffn2_norm', 'ffn2_w_gate', 'ffn2_w_up', 'ffn2_w_down')
_BIG = ('ffn1_w_gate', 'ffn1_w_up', 'ffn1_w_down', 'w_in', 'w_out', 'ffn2_w_gate', 'ffn2_w_up', 'ffn2_w_down')
_SMALL = tuple(n for n in _WEIGHTS if n not in _BIG and n != 'meta_tokens')
_MIX_SMALL = ('mix_norm', 'b_forget', 'fox_q_norm', 'fox_k_norm', 'swa_q_norm', 'swa_k_norm', 'swa_sinks',
              'fox_out_norm', 'swa_out_norm')


def _pack_rows(vectors):
    flat = jnp.concatenate([v.reshape(-1) for v in vectors])
    n = flat.shape[0]
    m = -(-n // (8 * LANES)) * 8
    return jnp.pad(flat, (0, m * LANES - n)).reshape(m, LANES)


def _unpack_rows(packed, shapes):
    flat = packed.reshape(-1)
    out, o = [], 0
    for s in shapes:
        n = int(np.prod(s))
        out.append(flat[o:o + n].reshape(s))
        o += n
    return out


def kernel(x, meta_tokens, ffn1_norm, ffn1_w_gate, ffn1_w_up, ffn1_w_down, mix_norm, w_in, b_forget, fox_q_norm, fox_k_norm, swa_q_norm, swa_k_norm, swa_sinks, fox_out_norm, swa_out_norm, w_out, ffn2_norm, ffn2_w_gate, ffn2_w_up, ffn2_w_down, loss_target, m_meta_tokens, m_ffn1_norm, m_ffn1_w_gate, m_ffn1_w_up, m_ffn1_w_down, m_mix_norm, m_w_in, m_b_forget, m_fox_q_norm, m_fox_k_norm, m_swa_q_norm, m_swa_k_norm, m_swa_sinks, m_fox_out_norm, m_swa_out_norm, m_w_out, m_ffn2_norm, m_ffn2_w_gate, m_ffn2_w_up, m_ffn2_w_down, v_meta_tokens, v_ffn1_norm, v_ffn1_w_gate, v_ffn1_w_up, v_ffn1_w_down, v_mix_norm, v_w_in, v_b_forget, v_fox_q_norm, v_fox_k_norm, v_swa_q_norm, v_swa_k_norm, v_swa_sinks, v_fox_out_norm, v_swa_out_norm, v_w_out, v_ffn2_norm, v_ffn2_w_gate, v_ffn2_w_up, v_ffn2_w_down):
    W = dict(meta_tokens=meta_tokens, ffn1_norm=ffn1_norm, ffn1_w_gate=ffn1_w_gate, ffn1_w_up=ffn1_w_up, ffn1_w_down=ffn1_w_down, mix_norm=mix_norm, w_in=w_in, b_forget=b_forget, fox_q_norm=fox_q_norm, fox_k_norm=fox_k_norm, swa_q_norm=swa_q_norm, swa_k_norm=swa_k_norm, swa_sinks=swa_sinks, fox_out_norm=fox_out_norm, swa_out_norm=swa_out_norm, w_out=w_out, ffn2_norm=ffn2_norm, ffn2_w_gate=ffn2_w_gate, ffn2_w_up=ffn2_w_up, ffn2_w_down=ffn2_w_down)
    Mo = dict(meta_tokens=m_meta_tokens, ffn1_norm=m_ffn1_norm, ffn1_w_gate=m_ffn1_w_gate, ffn1_w_up=m_ffn1_w_up, ffn1_w_down=m_ffn1_w_down, mix_norm=m_mix_norm, w_in=m_w_in, b_forget=m_b_forget, fox_q_norm=m_fox_q_norm, fox_k_norm=m_fox_k_norm, swa_q_norm=m_swa_q_norm, swa_k_norm=m_swa_k_norm, swa_sinks=m_swa_sinks, fox_out_norm=m_fox_out_norm, swa_out_norm=m_swa_out_norm, w_out=m_w_out, ffn2_norm=m_ffn2_norm, ffn2_w_gate=m_ffn2_w_gate, ffn2_w_up=m_ffn2_w_up, ffn2_w_down=m_ffn2_w_down)
    Vo = dict(meta_tokens=v_meta_tokens, ffn1_norm=v_ffn1_norm, ffn1_w_gate=v_ffn1_w_gate, ffn1_w_up=v_ffn1_w_up, ffn1_w_down=v_ffn1_w_down, mix_norm=v_mix_norm, w_in=v_w_in, b_forget=v_b_forget, fox_q_norm=v_fox_q_norm, fox_k_norm=v_fox_k_norm, swa_q_norm=v_swa_q_norm, swa_k_norm=v_swa_k_norm, swa_sinks=v_swa_sinks, fox_out_norm=v_fox_out_norm, swa_out_norm=v_swa_out_norm, w_out=v_w_out, ffn2_norm=v_ffn2_norm, ffn2_w_gate=v_ffn2_w_gate, ffn2_w_up=v_ffn2_w_up, ffn2_w_down=v_ffn2_w_down)

    _, S, D = x.shape
    L = S + BLOCK
    depth = ffn1_norm.shape[0]
    md = _MixDims(D)
    mx, my, mc = _mesh_pos()
    k_idx = (2 * mx + my).astype(jnp.int32).reshape(1)
    c_idx = mc.astype(jnp.int32).reshape(1)
    dcols = D // NCH

    meta_all = _allgather_devices(meta_tokens.reshape(-1, LANES), name="meta_allgather")
    meta_full = jnp.transpose(meta_all[0::2].reshape(NCH, N_META, dcols), (1, 0, 2)).reshape(N_META, D)

    wts = []
    for l in range(depth):
        l_idx = jnp.full((1,), l, jnp.int32)
        bf = lambda name: _cast_into_chunk(W[name], l_idx, k_idx, name="cast_chunk")
        g1, u1, d1 = _allgather_chips([bf('ffn1_w_gate'), bf('ffn1_w_up'), bf('ffn1_w_down')], name="ffn_allgather")
        wi, wo = _allgather_chips([bf('w_in'), bf('w_out')], name="mix_allgather")
        g2, u2, d2 = _allgather_chips([bf('ffn2_w_gate'), bf('ffn2_w_up'), bf('ffn2_w_down')], name="ffn_allgather")
        wi = _win_to_mine(jnp.transpose(wi, (1, 0, 2)).reshape(D, NCH * wi.shape[2]), md)
        wts.append(dict(g1=g1, u1=u1, d1=d1.reshape(-1, D), wi=wi, wo=wo.reshape(-1, D), g2=g2, u2=u2, d2=d2.reshape(-1, D)))

    h = jnp.concatenate([jnp.zeros((PAD, D), F32), meta_full, x[0]], axis=0)
    saved = []
    for l in range(depth):
        wl = wts[l]
        sp = _mix_small({n: W[n][l] for n in _MIX_SMALL}, md)
        h, s1 = _ffn_fwd(h, ffn1_norm[l].reshape(1, D), wl['g1'], wl['u1'], wl['d1'], "ffn")
        h, s2 = _mix_fwd(h, sp, wl['wi'], wl['wo'], md)
        h, s3 = _ffn_fwd(h, ffn2_norm[l].reshape(1, D), wl['g2'], wl['u2'], wl['d2'], "ffn")
        saved.append((s1, s2, s3, sp))

    loss_part, dh = _loss_grad(h, loss_target[0], name="loss_grad")

    small_grads = {n: [None] * depth for n in _SMALL}
    stacks = {n: None for n in _BIG}

    def update(name, l, grad):
        stacks[name] = _adamw_layer(W[name], Mo[name], Vo[name], grad, jnp.full((1,), l, jnp.int32), stacks[name],
                                    name="adamw_layer")

    queue = _ExchangeQueue()

    def reduce_later(names, l, parts, tag):
        def done(sums, lands):
            for n, grad in zip(names, _reduce_scatter_end(sums, lands, c_idx, k_idx, tag)):
                update(n, l, grad)
        queue.add((tag, names[0], l), _reduce_scatter_begin(parts, c_idx, tag), done)

    for l in range(depth - 1, -1, -1):
        wl = wts[l]
        s1, s2, s3, sp = saved[l]
        dh, dg, dwg, dwu, dwd = _ffn_bwd(dh, s3, ffn2_norm[l].reshape(1, D), wl['g2'], wl['u2'], wl['d2'], "ffn", queue)
        queue.finish_ready()
        small_grads['ffn2_norm'][l] = dg.reshape(-1)
        reduce_later(('ffn2_w_gate', 'ffn2_w_up', 'ffn2_w_down'), l, [dwg, dwu, dwd.reshape(NCH, -1, D)], "ffn")

        dh, dwi, dwo, sm = _mix_bwd(dh, s2, sp, wl['wi'], wl['wo'], md, queue)
        queue.finish_ready()
        for n in _MIX_SMALL:
            small_grads[n][l] = sm[n]
        dwi = _win_grad_to_ref(dwi, md)
        dwi = jnp.transpose(dwi.reshape(D, NCH, -1), (1, 0, 2))
        reduce_later(('w_in', 'w_out'), l, [dwi, dwo.reshape(NCH, -1, D)], "mix")

        dh, dg, dwg, dwu, dwd = _ffn_bwd(dh, s1, ffn1_norm[l].reshape(1, D), wl['g1'], wl['u1'], wl['d1'], "ffn", queue)
        queue.finish_ready()
        small_grads['ffn1_norm'][l] = dg.reshape(-1)
        reduce_later(('ffn1_w_gate', 'ffn1_w_up', 'ffn1_w_down'), l, [dwg, dwu, dwd.reshape(NCH, -1, D)], "ffn")
    queue.drain("rs_chips")

    grad_x = dh[BLOCK:][None]

    small_shapes = [W[n].shape for n in _SMALL]
    parts = [jnp.stack(small_grads[n]) for n in _SMALL] + [dh[PAD:BLOCK], loss_part[0, :1]]
    packed = _pack_rows(parts)
    total = _sum_devices(_allgather_devices(packed, name="small_allgather"), name="small_sum")
    *g_small, g_meta, loss = _unpack_rows(total, small_shapes + [(N_META, D), (1,)])
    g_meta = lax.dynamic_slice(g_meta, (0, k_idx[0] * dcols), (N_META, dcols))

    sw = _pack_rows([W[n] for n in _SMALL])
    sd, smm, svv = _adamw_flat(sw, _pack_rows(g_small), _pack_rows([Mo[n] for n in _SMALL]),
                               _pack_rows([Vo[n] for n in _SMALL]), name="adamw_small")
    d_small, m_small, v_small = (_unpack_rows(t, small_shapes) for t in (sd, smm, svv))
    d_meta, m_meta, v_meta = _adamw_flat(meta_tokens, g_meta, m_meta_tokens, v_meta_tokens, name="adamw_meta")

    grads, deltas, new_m, new_v = {}, {}, {}, {}
    for n in _BIG:
        grads[n], deltas[n], new_m[n], new_v[n] = stacks[n]
    for i, n in enumerate(_SMALL):
        grads[n], deltas[n], new_m[n], new_v[n] = g_small[i], d_small[i], m_small[i], v_small[i]
    grads['meta_tokens'], deltas['meta_tokens'], new_m['meta_tokens'], new_v['meta_tokens'] = g_meta, d_meta, m_meta, v_meta
    return (loss.reshape(()), grad_x, *[grads[n] for n in _WEIGHTS], *[deltas[n] for n in _WEIGHTS],
            *[new_m[n] for n in _WEIGHTS], *[new_v[n] for n in _WEIGHTS])
```

```python
import numpy as np
import jax
import jax.numpy as jnp
from jax import lax
from jax.experimental import pallas as pl
from jax.experimental.pallas import tpu as pltpu

F32 = jnp.float32
BF16 = jnp.bfloat16

HEAD_DIM = 64
N_META = 16
BLOCK = 128
WINDOW = 128
PAD = BLOCK - N_META
EPS = 1e-6
NEG_INF = -1e30
SWA_GROUP = 8
NCH = 4
LANES = 128
QBLOCK = 512

ADAM_LR = 0.001
ADAM_B1 = 0.9
ADAM_B2 = 0.999
ADAM_EPS = 1e-08
ADAM_WD = 0.01
ADAM_STEP = 10

V7X_VMEM_BYTES = 64 * 1024 * 1024
VMEM_LIMIT = V7X_VMEM_BYTES - 8 * 1024 * 1024
MESH = pl.DeviceIdType.MESH
HIGHEST = lax.Precision.HIGHEST

_TM = (1088, 1024, 704, 512, 384, 256, 128)
_TN = (1408, 1024, 768, 512, 384, 256, 128)
_TK = (2176, 1408, 1024, 512, 384, 256, 128)
_TR = (544, 512, 384, 272, 256, 128)


def _pick(n, cands):
    for c in cands:
        if n % c == 0:
            return c
    return n


def _cparams(*sem):
    return pltpu.CompilerParams(dimension_semantics=sem if sem else None, vmem_limit_bytes=VMEM_LIMIT)


def _matmul(a, b, *, name, nt=False, b_chunked=False, out_chunked=False, out_dtype=F32,
            residual=None, scale=1.0, comm=None):
    M, K = a.shape
    if not nt:
        N = b.shape[-1] * (NCH if b_chunked else 1)
        assert b.shape[-2] == K
        k_unit = K
    else:
        N = b.shape[-2]
        k_unit = b.shape[-1]
        assert k_unit * (NCH if b_chunked else 1) == K
    n_unit = N // NCH if (out_chunked or (b_chunked and not nt)) else N
    tm, tn, tk = _pick(M, _TM), _pick(n_unit, _TN), _pick(k_unit, _TK)
    osz = jnp.dtype(out_dtype).itemsize

    def est(tm_):
        return (2 * tm_ * tk * 2 + 2 * tk * tn * 2 + tm_ * tn * 4 + 2 * tm_ * tn * osz
                + (2 * tm_ * tn * 4 if residual is not None else 0))

    while est(tm) > VMEM_LIMIT * 3 // 4 and tm % 32 == 0:
        tm //= 2
    npc, kpc = n_unit // tn, k_unit // tk
    nk = K // tk
    grid = (M // tm, N // tn, nk)

    a_spec = pl.BlockSpec((tm, tk), lambda i, j, k: (i, k))
    if not nt:
        if b_chunked:
            b_spec = pl.BlockSpec((None, tk, tn), lambda i, j, k: (j // npc, k, j % npc))
        else:
            b_spec = pl.BlockSpec((tk, tn), lambda i, j, k: (k, j))
        dims = (((1,), (0,)), ((), ()))
    else:
        if b_chunked:
            b_spec = pl.BlockSpec((None, tn, tk), lambda i, j, k: (k // kpc, j, k % kpc))
        else:
            b_spec = pl.BlockSpec((tn, tk), lambda i, j, k: (j, k))
        dims = (((1,), (1,)), ((), ()))
    if out_chunked:
        o_spec = pl.BlockSpec((None, tm, tn), lambda i, j, k: (j // npc, i, j % npc))
        out_shape = jax.ShapeDtypeStruct((NCH, M, n_unit), out_dtype)
    else:
        o_spec = pl.BlockSpec((tm, tn), lambda i, j, k: (i, j))
        out_shape = jax.ShapeDtypeStruct((M, N), out_dtype)
    in_specs = [a_spec, b_spec]
    args = [a, b]
    if residual is not None:
        assert not out_chunked
        in_specs.append(pl.BlockSpec((tm, tn), lambda i, j, k: (i, j)))
        args.append(residual)

    n_main = len(args)
    n_cin = 0 if comm is None else len(comm.operands)
    n_cout = 0 if comm is None else len(comm.out_shapes)

    def body(*refs):
        r_ref = refs[2] if residual is not None else None
        a_ref, b_ref = refs[0], refs[1]
        o_ref = refs[n_main + n_cin]
        acc_ref = refs[n_main + n_cin + 1 + n_cout]
        i, j, k = pl.program_id(0), pl.program_id(1), pl.program_id(2)
        if comm is not None:
            cin = refs[n_main:n_main + n_cin]
            cout = refs[n_main + n_cin + 1:n_main + n_cin + 1 + n_cout]
            send_sems, recv_sems = refs[-2:]

            @pl.when((i == 0) & (j == 0) & (k == 0))
            def _():
                for cp, _ in comm.copies(cin, cout, send_sems, recv_sems):
                    cp.start()

        @pl.when(k == 0)
        def _():
            acc_ref[...] = jnp.zeros_like(acc_ref)

        acc_ref[...] += lax.dot_general(a_ref[...], b_ref[...], dims, preferred_element_type=F32)

        @pl.when(k == nk - 1)
        def _():
            r = acc_ref[...]
            if scale != 1.0:
                r = r * scale
            if residual is not None:
                r = r + r_ref[...]
            o_ref[...] = r.astype(o_ref.dtype)

        if comm is not None:
            @pl.when((i == grid[0] - 1) & (j == grid[1] - 1) & (k == nk - 1))
            def _():
                cps = comm.copies(cin, cout, send_sems, recv_sems)
                for _, arrival in cps:
                    arrival.wait_recv()
                for cp, _ in cps:
                    cp.wait_send()

    scratch = [pltpu.VMEM((tm, tn), F32)]
    if comm is None:
        return pl.pallas_call(
            body, name=name, grid=grid, in_specs=in_specs, out_specs=o_spec, out_shape=out_shape,
            scratch_shapes=scratch, compiler_params=_cparams("parallel", "parallel", "arbitrary"),
        )(*args)
    scratch += [pltpu.SemaphoreType.DMA((comm.n_sems,)), pltpu.SemaphoreType.DMA((comm.n_sems,))]
    return pl.pallas_call(
        body, name=name, grid=grid, in_specs=in_specs + [_ANY] * n_cin,
        out_specs=[o_spec] + [_ANY] * n_cout, out_shape=[out_shape] + list(comm.out_shapes),
        input_output_aliases={n_main + s: 1 + d for s, d in comm.aliases.items()},
        scratch_shapes=scratch, compiler_params=_cparams("arbitrary", "arbitrary", "arbitrary"),
    )(*args, *comm.operands)


def _transpose(x, *, name):
    M, N = x.shape
    tc = _pick(N, (512, 384, 256, 128))

    def body(x_ref, o_ref):
        o_ref[...] = x_ref[...].astype(F32).T.astype(o_ref.dtype)

    return pl.pallas_call(
        body, name=name, grid=(N // tc,),
        in_specs=[pl.BlockSpec((M, tc), lambda j: (0, j))],
        out_specs=pl.BlockSpec((tc, M), lambda j: (j, 0)),
        out_shape=jax.ShapeDtypeStruct((N, M), x.dtype),
        compiler_params=_cparams("parallel"),
    )(x)


def _rms_fwd(h, g, *, name):
    L, D = h.shape
    tr = _pick(L, _TR)

    def body(h_ref, g_ref, o_ref):
        x = h_ref[...]
        r = lax.rsqrt(jnp.mean(x * x, axis=-1, keepdims=True) + EPS)
        o_ref[...] = (x * r * g_ref[...]).astype(o_ref.dtype)

    return pl.pallas_call(
        body, name=name, grid=(L // tr,),
        in_specs=[pl.BlockSpec((tr, D), lambda i: (i, 0)), pl.BlockSpec((1, D), lambda i: (0, 0))],
        out_specs=pl.BlockSpec((tr, D), lambda i: (i, 0)),
        out_shape=jax.ShapeDtypeStruct((L, D), BF16),
        compiler_params=_cparams("parallel"),
    )(h, g)


def _rms_bwd(dy, h, g, dh, *, name):
    L, D = h.shape
    tr = _pick(L, _TR)

    def body(dy_ref, h_ref, g_ref, dh_ref, o_ref, dg_ref):
        i = pl.program_id(0)
        x = h_ref[...]
        dyv = dy_ref[...]
        r = lax.rsqrt(jnp.mean(x * x, axis=-1, keepdims=True) + EPS)
        xh = x * r
        dxh = dyv * g_ref[...]
        dx = r * (dxh - xh * jnp.mean(dxh * xh, axis=-1, keepdims=True))
        o_ref[...] = dh_ref[...] + dx
        part = jnp.sum(dyv * xh, axis=0, keepdims=True)

        @pl.when(i == 0)
        def _():
            dg_ref[...] = part

        @pl.when(i > 0)
        def _():
            dg_ref[...] += part

    row = pl.BlockSpec((tr, D), lambda i: (i, 0))
    vec = pl.BlockSpec((1, D), lambda i: (0, 0))
    return pl.pallas_call(
        body, name=name, grid=(L // tr,),
        in_specs=[row, row, vec, row], out_specs=[row, vec],
        out_shape=[jax.ShapeDtypeStruct((L, D), F32), jax.ShapeDtypeStruct((1, D), F32)],
        compiler_params=_cparams("arbitrary"),
    )(dy, h, g, dh)


def _swiglu_fwd(gate, up, *, name):
    L, F = gate.shape
    tr, tc = _pick(L, _TR), _pick(F, _TN)

    def body(g_ref, u_ref, o_ref):
        g = g_ref[...].astype(F32)
        o_ref[...] = (g * jax.nn.sigmoid(g) * u_ref[...].astype(F32)).astype(o_ref.dtype)

    blk = pl.BlockSpec((tr, tc), lambda i, j: (i, j))
    return pl.pallas_call(
        body, name=name, grid=(L // tr, F // tc), in_specs=[blk, blk], out_specs=blk,
        out_shape=jax.ShapeDtypeStruct((L, F), BF16),
        compiler_params=_cparams("parallel", "parallel"),
    )(gate, up)


def _swiglu_bwd(dact, gate, up, *, name):
    L, F = gate.shape
    tr, tc = _pick(L, _TR), _pick(F, _TN)

    def body(d_ref, g_ref, u_ref, dg_ref, du_ref):
        d = d_ref[...].astype(F32)
        g = g_ref[...].astype(F32)
        u = u_ref[...].astype(F32)
        sg = jax.nn.sigmoid(g)
        du_ref[...] = (d * g * sg).astype(du_ref.dtype)
        dg_ref[...] = (d * u * sg * (1.0 + g * (1.0 - sg))).astype(dg_ref.dtype)

    blk = pl.BlockSpec((tr, tc), lambda i, j: (i, j))
    return pl.pallas_call(
        body, name=name, grid=(L // tr, F // tc), in_specs=[blk, blk, blk], out_specs=[blk, blk],
        out_shape=[jax.ShapeDtypeStruct((L, F), BF16)] * 2,
        compiler_params=_cparams("parallel", "parallel"),
    )(dact, gate, up)


def _cast_bf16(x, *, name):
    L, D = x.shape
    tr = _pick(L, _TR)

    def body(x_ref, o_ref):
        o_ref[...] = x_ref[...].astype(o_ref.dtype)

    blk = pl.BlockSpec((tr, D), lambda i: (i, 0))
    return pl.pallas_call(
        body, name=name, grid=(L // tr,), in_specs=[blk], out_specs=blk,
        out_shape=jax.ShapeDtypeStruct((L, D), BF16), compiler_params=_cparams("parallel"),
    )(x)


def _loss_grad(h, target, *, name):
    L, D = h.shape
    S = target.shape[0]
    nb = L // BLOCK

    def body(h_ref, t_ref, loss_ref, dh_ref):
        i = pl.program_id(0)

        @pl.when(i == 0)
        def _():
            loss_ref[...] = jnp.zeros_like(loss_ref)
            dh_ref[...] = jnp.zeros_like(dh_ref)

        @pl.when(i > 0)
        def _():
            err = h_ref[...] - t_ref[...]
            dh_ref[...] = err * (1.0 / D)
            loss_ref[...] += jnp.full(loss_ref.shape, (0.5 / D) * jnp.sum(err * err), F32)

    return pl.pallas_call(
        body, name=name, grid=(nb,),
        in_specs=[pl.BlockSpec((BLOCK, D), lambda i: (i, 0)),
                  pl.BlockSpec((BLOCK, D), lambda i: (jnp.maximum(i - 1, 0), 0))],
        out_specs=[pl.BlockSpec((1, LANES), lambda i: (0, 0)), pl.BlockSpec((BLOCK, D), lambda i: (i, 0))],
        out_shape=[jax.ShapeDtypeStruct((1, LANES), F32), jax.ShapeDtypeStruct((L, D), F32)],
        compiler_params=_cparams("arbitrary"),
    )(h, target)


def _ffn_fwd(h, g, w, tag, queue=None):
    hn = _rms_fwd(h, g, name=f"{tag}_rms")
    gate = _mm(queue, hn, w(0), name=f"{tag}_gate", b_chunked=True, out_dtype=BF16)
    up = _mm(queue, hn, w(1), name=f"{tag}_up", b_chunked=True, out_dtype=BF16)
    act = _swiglu_fwd(gate, up, name=f"{tag}_act")
    h_out = _mm(queue, act, w(2), name=f"{tag}_down", residual=h, scale=0.5)
    return h_out, (h, hn, gate, up, act)


def _ffn_bwd(dh, saved, g, wg, wu, wd, tag, queue=None):
    h, hn, gate, up, act = saved
    dout = _cast_bf16(dh, name=f"{tag}_dout")
    dact = _mm(queue, dout, wd, name=f"{tag}_dact", nt=True, out_dtype=BF16, scale=0.5)
    actT = _transpose(act, name=f"{tag}_actT")
    dwd = _mm(queue, actT, dout, name=f"{tag}_dwd", out_dtype=BF16, scale=0.5)
    dgate, dup = _swiglu_bwd(dact, gate, up, name=f"{tag}_dswiglu")
    hnT = _transpose(hn, name=f"{tag}_hnT")
    dwg = _mm(queue, hnT, dgate, name=f"{tag}_dwg", out_chunked=True, out_dtype=BF16)
    dwu = _mm(queue, hnT, dup, name=f"{tag}_dwu", out_chunked=True, out_dtype=BF16)
    dhn = _mm(queue, dgate, wg, name=f"{tag}_dhn_g", nt=True, b_chunked=True)
    dhn = _mm(queue, dup, wu, name=f"{tag}_dhn_u", nt=True, b_chunked=True, residual=dhn)
    dh_in, dg = _rms_bwd(dhn, h, g, dh, name=f"{tag}_drms")
    return dh_in, dg, dwg, dwu, dwd


class _MixDims:
    def __init__(self, d_model):
        self.wf = d_model // 2
        self.ws = d_model // 2
        self.pf = self.wf // LANES
        self.ps = self.ws // LANES
        self.hf = self.wf // HEAD_DIM
        self.hq = self.ws // HEAD_DIM
        self.nkv = max(1, self.hq // SWA_GROUP)
        self.g = self.hq // self.nkv
        self.bq_f, self.bk_f, self.bv_f = 0, self.pf, 2 * self.pf
        self.bq_s = 3 * self.pf
        self.bk_s = self.bq_s + self.ps
        self.bv_s = self.bk_s + self.nkv
        self.bz = self.bv_s + self.nkv
        self.nu = (self.bz + 1) * LANES
        self.nup = -(-self.nu // 512) * 512
        self.in_width = 3 * self.wf + self.hf + self.ws + 2 * self.nkv * HEAD_DIM
        assert self.hf <= 2 * (LANES // 8)

    def gate_lane(self, h):
        return 8 * (h // 2) + h % 2

    def column_map(self):
        wf, ws, hd = self.wf, self.ws, HEAD_DIM
        src = np.full((self.nup,), -1, np.int64)
        src[0:3 * wf] = np.arange(3 * wf)
        o_sq = 3 * wf + self.hf
        src[self.bq_s * LANES:self.bq_s * LANES + ws] = o_sq + np.arange(ws)
        o_sk = o_sq + ws
        o_sv = o_sk + self.nkv * hd
        for kv in range(self.nkv):
            for rep in range(2):
                c0 = (self.bk_s + kv) * LANES + rep * hd
                src[c0:c0 + hd] = o_sk + kv * hd + np.arange(hd)
                c0 = (self.bv_s + kv) * LANES + rep * hd
                src[c0:c0 + hd] = o_sv + kv * hd + np.arange(hd)
        for h in range(self.hf):
            src[self.bz * LANES + self.gate_lane(h)] = 3 * wf + h
        return src

    def grad_column_map(self):
        src = self.column_map()
        dst = np.zeros((self.in_width,), np.int64)
        for col in range(self.nup - 1, -1, -1):
            if src[col] >= 0:
                dst[src[col]] = col
        return dst


def _block_diag_mean():
    m = np.zeros((LANES, LANES), np.float32)
    m[:HEAD_DIM, :HEAD_DIM] = 1.0 / HEAD_DIM
    m[HEAD_DIM:, HEAD_DIM:] = 1.0 / HEAD_DIM
    return jnp.asarray(m)


def _fold_halves():
    m = np.eye(LANES, dtype=np.float32)
    m[np.arange(LANES), (np.arange(LANES) + HEAD_DIM) % LANES] = 1.0
    return jnp.asarray(m)


def _gate_expand(md):
    e = np.zeros((LANES, md.wf), np.float32)
    for h in range(md.hf):
        e[md.gate_lane(h), h * HEAD_DIM:(h + 1) * HEAD_DIM] = 1.0
    return jnp.asarray(e)


def _qblocks(L, qb):
    blocks = [(0, BLOCK)]
    r = BLOCK
    while r < L:
        blocks.append((r, qb))
        r += qb
    assert r == L
    return blocks


def _f32dot(a, b):
    return jnp.dot(a, b, precision=HIGHEST, preferred_element_type=F32)


_DIMS_NT = (((1,), (1,)), ((), ()))


def _dot_nt(a, b):
    return lax.dot_general(a, b, _DIMS_NT, preferred_element_type=F32)


def _dot_tn(a, b):
    return jnp.dot(a.T.astype(BF16), b, preferred_element_type=F32)


def _log_sigmoid(z):
    return jnp.minimum(z, 0.0) - jnp.log(1.0 + jnp.exp(-jnp.abs(z)))


def _gate_fwd(u, b, md, *, name):
    L = u.shape[0]
    nb = L // BLOCK
    expand = _gate_expand(md)

    def body(z_ref, b_ref, e_ref, cexp_ref, ct_ref, c_s):
        ri = lax.broadcasted_iota(jnp.int32, (BLOCK, BLOCK), 0)
        ci = lax.broadcasted_iota(jnp.int32, (BLOCK, BLOCK), 1)
        tri = (ri >= ci).astype(F32)
        carry = jnp.zeros((1, LANES), F32)
        for bi in range(nb):
            rows = pl.ds(bi * BLOCK, BLOCK)
            logf = _log_sigmoid(z_ref[rows, :] + b_ref[...])
            blk = _f32dot(tri, logf) + carry
            c_s[rows, :] = blk
            carry = blk[BLOCK - 1:BLOCK, :]
        c = c_s[...]
        ct_ref[...] = c.T
        cexp_ref[...] = _f32dot(c, e_ref[...])

    return pl.pallas_call(
        body, name=name, grid=(1,),
        in_specs=[pl.BlockSpec((L, LANES), lambda i: (0, md.bz)), pl.BlockSpec((1, LANES), lambda i: (0, 0)),
                  pl.BlockSpec((LANES, md.wf), lambda i: (0, 0))],
        out_specs=[pl.BlockSpec((L, md.wf), lambda i: (0, 0)), pl.BlockSpec((LANES, L), lambda i: (0, 0))],
        out_shape=[jax.ShapeDtypeStruct((L, md.wf), F32), jax.ShapeDtypeStruct((LANES, L), F32)],
        scratch_shapes=[pltpu.VMEM((L, LANES), F32)],
        compiler_params=_cparams("arbitrary"),
    )(u, b, expand)


def _gate_bwd(u, b, dck_t, md, *, name):
    L = u.shape[0]
    nb = L // BLOCK

    def body(z_ref, b_ref, dck_ref, dz_ref, db_ref, dc_s):
        ri = lax.broadcasted_iota(jnp.int32, (BLOCK, BLOCK), 0)
        ci = lax.broadcasted_iota(jnp.int32, (BLOCK, BLOCK), 1)
        triu = (ri <= ci).astype(F32)
        dc_s[...] = -dck_ref[...].T
        carry = jnp.zeros((1, LANES), F32)
        db = jnp.zeros((1, LANES), F32)
        for bi in range(nb - 1, -1, -1):
            rows = pl.ds(bi * BLOCK, BLOCK)
            blk = _f32dot(triu, dc_s[rows, :]) + carry
            carry = blk[0:1, :]
            z = z_ref[rows, :] + b_ref[...]
            dz = blk * jax.nn.sigmoid(-z)
            if bi == 0:
                dz = jnp.where(lax.broadcasted_iota(jnp.int32, (BLOCK, LANES), 0) >= PAD, dz, 0.0)
            dz_ref[rows, :] = dz
            db = db + jnp.sum(dz, axis=0, keepdims=True)
        db_ref[...] = db

    return pl.pallas_call(
        body, name=name, grid=(1,),
        in_specs=[pl.BlockSpec((L, LANES), lambda i: (0, md.bz)), pl.BlockSpec((1, LANES), lambda i: (0, 0)),
                  pl.BlockSpec((LANES, L), lambda i: (0, 0))],
        out_specs=[pl.BlockSpec((L, LANES), lambda i: (0, 0)), pl.BlockSpec((1, LANES), lambda i: (0, 0))],
        out_shape=[jax.ShapeDtypeStruct((L, LANES), F32), jax.ShapeDtypeStruct((1, LANES), F32)],
        scratch_shapes=[pltpu.VMEM((L, LANES), F32)],
        compiler_params=_cparams("arbitrary"),
    )(u, b, dck_t)


def _head_norm(x, g, bd):
    r = lax.rsqrt(_f32dot(x * x, bd) + EPS)
    xh = x * r
    return xh * g, xh, r


def _head_norm_bwd(dy, xh, r, g, bd):
    dxh = dy * g
    dx = r * (dxh - xh * _f32dot(dxh * xh, bd))
    return dx, jnp.sum(dy * xh, axis=0, keepdims=True)


def _lane_half():
    return lax.broadcasted_iota(jnp.int32, (1, LANES), 1) < HEAD_DIM


def _store_head_pair(x, half, a_s, b_s):
    a_s[...] = jnp.where(half, x, 0.0).astype(BF16)
    b_s[...] = jnp.where(half, 0.0, x).astype(BF16)


def _fox_scores(qm, kn_s, cexp_ref, ct_ref, r0, nr, klen, hh):
    s = _dot_nt(qm, kn_s[0:klen, :]) * (HEAD_DIM ** -0.5)
    s = s + cexp_ref[r0:r0 + nr, HEAD_DIM * hh:HEAD_DIM * hh + 1] - ct_ref[hh:hh + 1, 0:klen]
    qp = r0 + lax.broadcasted_iota(jnp.int32, (nr, klen), 0)
    kp = lax.broadcasted_iota(jnp.int32, (nr, klen), 1)
    return jnp.where((kp <= qp) & (kp >= PAD), s, NEG_INF)


def _fox_fwd(u, cexp, ct3, qg, kg, md, *, name):
    L = u.shape[0]
    blocks = _qblocks(L, QBLOCK)
    bd = _block_diag_mean()

    def body(q_ref, k_ref, v_ref, cexp_ref, ct_ref, qg_ref, kg_ref, bd_ref, o_ref, lse_ref, qa_s, qb_s, kn_s, v_s):
        half = _lane_half()
        bdv = bd_ref[...]
        _store_head_pair(_head_norm(q_ref[...], qg_ref[...], bdv)[0], half, qa_s, qb_s)
        kn_s[...] = _head_norm(k_ref[...], kg_ref[...], bdv)[0].astype(BF16)
        v_s[...] = v_ref[...].astype(BF16)
        for r0, nr in blocks:
            klen = r0 + nr
            o_blk = lse_blk = None
            for hh, q_s in enumerate((qa_s, qb_s)):
                s = _fox_scores(q_s[r0:r0 + nr, :], kn_s, cexp_ref, ct_ref, r0, nr, klen, hh)
                m = jnp.max(s, axis=-1, keepdims=True)
                p = jnp.exp(s - m)
                l = jnp.sum(p, axis=-1, keepdims=True)
                oh = jnp.dot(p.astype(BF16), v_s[0:klen, :], preferred_element_type=F32) * (1.0 / l)
                lh = jnp.broadcast_to(jnp.where(m > 0.5 * NEG_INF, m + jnp.log(l), 0.0), (nr, LANES))
                o_blk = oh if hh == 0 else jnp.where(half, o_blk, oh)
                lse_blk = lh if hh == 0 else jnp.where(half, lse_blk, lh)
            o_ref[r0:r0 + nr, :] = o_blk
            lse_ref[r0:r0 + nr, :] = lse_blk

    col = lambda base: pl.BlockSpec((L, LANES), lambda j: (0, base + j))
    vec = pl.BlockSpec((1, LANES), lambda j: (0, 0))
    return pl.pallas_call(
        body, name=name, grid=(md.pf,),
        in_specs=[col(md.bq_f), col(md.bk_f), col(md.bv_f), col(0),
                  pl.BlockSpec((None, 8, L), lambda j: (j, 0, 0)), vec, vec,
                  pl.BlockSpec((LANES, LANES), lambda j: (0, 0))],
        out_specs=[col(0), col(0)],
        out_shape=[jax.ShapeDtypeStruct((L, md.wf), F32)] * 2,
        scratch_shapes=[pltpu.VMEM((L, LANES), BF16)] * 4,
        compiler_params=_cparams("parallel"),
    )(u, u, u, cexp, ct3, qg, kg, bd)


def _softmax_bwd(p, dp):
    pdp = p * dp
    return pdp - p * jnp.sum(pdp, axis=-1, keepdims=True)


def _fox_bwd(u, cexp, ct3, qg, kg, do, lse, md, *, name):
    L = u.shape[0]
    blocks = _qblocks(L, QBLOCK // 2)
    bd = _block_diag_mean()
    fold = _fold_halves()
    npairs = md.pf

    def body(q_ref, k_ref, v_ref, cexp_ref, ct_ref, qg_ref, kg_ref, bd_ref, fold_ref, do_ref, lse_ref,
             dq_ref, dk_ref, dv_ref, dck_ref, dqg_ref, dkg_ref,
             qa_s, qb_s, kn_s, v_s, doa_s, dob_s, dqn_s, dkn_s, dvv_s):
        j = pl.program_id(0)
        half = _lane_half()
        bdv = bd_ref[...]
        qy, qh, rq = _head_norm(q_ref[...], qg_ref[...], bdv)
        ky, kh, rk = _head_norm(k_ref[...], kg_ref[...], bdv)
        _store_head_pair(qy, half, qa_s, qb_s)
        _store_head_pair(do_ref[...], half, doa_s, dob_s)
        kn_s[...] = ky.astype(BF16)
        v_s[...] = v_ref[...].astype(BF16)
        dkn_s[...] = jnp.zeros_like(dkn_s)
        dvv_s[...] = jnp.zeros_like(dvv_s)
        dck_ref[...] = jnp.zeros_like(dck_ref)
        for r0, nr in blocks:
            klen = r0 + nr
            dq_blk = None
            for hh, (q_s, do_s) in enumerate(((qa_s, doa_s), (qb_s, dob_s))):
                c0 = HEAD_DIM * hh
                qm = q_s[r0:r0 + nr, :]
                dom = do_s[r0:r0 + nr, :]
                s = _fox_scores(qm, kn_s, cexp_ref, ct_ref, r0, nr, klen, hh)
                p = jnp.exp(s - lse_ref[r0:r0 + nr, c0:c0 + 1])
                ds = _softmax_bwd(p, _dot_nt(dom, v_s[0:klen, :]))
                dck_ref[hh:hh + 1, 0:klen] += jnp.sum(ds, axis=0, keepdims=True)
                ds = ds * (HEAD_DIM ** -0.5)
                dq_h = jnp.dot(ds.astype(BF16), kn_s[0:klen, :], preferred_element_type=F32)
                dkn_s[0:klen, :] += _dot_tn(ds, qm)
                dvv_s[0:klen, :] += _dot_tn(p, dom)
                dq_blk = dq_h if hh == 0 else jnp.where(half, dq_blk, dq_h)
            dqn_s[r0:r0 + nr, :] = dq_blk
        dq, dqg = _head_norm_bwd(dqn_s[...], qh, rq, qg_ref[...], bdv)
        dk, dkg = _head_norm_bwd(dkn_s[...], kh, rk, kg_ref[...], bdv)
        dq_ref[...] = dq
        dk_ref[...] = dk
        dv_ref[...] = dvv_s[...]

        @pl.when(j == 0)
        def _():
            dqg_ref[...] = jnp.zeros_like(dqg_ref)
            dkg_ref[...] = jnp.zeros_like(dkg_ref)

        dqg_ref[...] += dqg
        dkg_ref[...] += dkg

        @pl.when(j == npairs - 1)
        def _():
            dqg_ref[...] = _f32dot(jnp.broadcast_to(dqg_ref[...], (8, LANES)), fold_ref[...])[0:1, :]
            dkg_ref[...] = _f32dot(jnp.broadcast_to(dkg_ref[...], (8, LANES)), fold_ref[...])[0:1, :]

    col = lambda base: pl.BlockSpec((L, LANES), lambda j: (0, base + j))
    vec = pl.BlockSpec((1, LANES), lambda j: (0, 0))
    sq = pl.BlockSpec((LANES, LANES), lambda j: (0, 0))
    ct_spec = pl.BlockSpec((None, 8, L), lambda j: (j, 0, 0))
    big = jax.ShapeDtypeStruct((L, md.wf), F32)
    small = jax.ShapeDtypeStruct((1, LANES), F32)
    return pl.pallas_call(
        body, name=name, grid=(md.pf,),
        in_specs=[col(md.bq_f), col(md.bk_f), col(md.bv_f), col(0), ct_spec, vec, vec, sq, sq, col(0), col(0)],
        out_specs=[col(0), col(0), col(0), ct_spec, vec, vec],
        out_shape=[big, big, big, jax.ShapeDtypeStruct((md.pf, 8, L), F32), small, small],
        scratch_shapes=[pltpu.VMEM((L, LANES), BF16)] * 6 + [pltpu.VMEM((L, LANES), F32)] * 3,
        compiler_params=_cparams("arbitrary"),
    )(u, u, u, cexp, ct3, qg, kg, bd, fold, do, lse)


def _swa_scores(qm, kn_s, slope, k0, r0, nr, klen):
    s = _dot_nt(qm, kn_s[k0:k0 + klen, :]) * (HEAD_DIM ** -0.5)
    qp = r0 + lax.broadcasted_iota(jnp.int32, (nr, klen), 0)
    kp = k0 + lax.broadcasted_iota(jnp.int32, (nr, klen), 1)
    dist = qp - kp
    s = s - slope * dist.astype(F32)
    return jnp.where((dist >= 0) & (dist < WINDOW) & (kp >= PAD), s, NEG_INF)


def _swa_fwd(u, sinkp, slopep, qg, kg, md, *, name):
    L = u.shape[0]
    blocks = _qblocks(L, QBLOCK)
    bd = _block_diag_mean()

    def body(q_ref, k_ref, v_ref, sink_ref, slope_ref, qg_ref, kg_ref, bd_ref, o_ref, lse_ref, qa_s, qb_s, kn_s, v_s):
        half = _lane_half()
        bdv = bd_ref[...]
        _store_head_pair(_head_norm(q_ref[...], qg_ref[...], bdv)[0], half, qa_s, qb_s)
        kn_s[...] = _head_norm(k_ref[...], kg_ref[...], bdv)[0].astype(BF16)
        v_s[...] = v_ref[...].astype(BF16)
        for r0, nr in blocks:
            k0 = max(r0 - BLOCK, 0)
            klen = r0 + nr - k0
            o_blk = lse_blk = None
            for hh, q_s in enumerate((qa_s, qb_s)):
                c0 = HEAD_DIM * hh
                s = _swa_scores(q_s[r0:r0 + nr, :], kn_s, slope_ref[0:1, c0:c0 + 1], k0, r0, nr, klen)
                sink = sink_ref[0:1, c0:c0 + 1]
                m = jnp.maximum(jnp.max(s, axis=-1, keepdims=True), sink)
                p = jnp.exp(s - m)
                den = jnp.sum(p, axis=-1, keepdims=True) + jnp.exp(sink - m)
                oh = jnp.dot(p.astype(BF16), v_s[k0:k0 + klen, :], preferred_element_type=F32) * (1.0 / den)
                lh = jnp.broadcast_to(m + jnp.log(den), (nr, LANES))
                o_blk = oh if hh == 0 else jnp.where(half, o_blk, oh)
                lse_blk = lh if hh == 0 else jnp.where(half, lse_blk, lh)
            o_ref[r0:r0 + nr, :] = o_blk
            lse_ref[r0:r0 + nr, :] = lse_blk

    g2 = md.g // 2
    qcol = pl.BlockSpec((L, LANES), lambda j: (0, md.bq_s + j))
    kcol = pl.BlockSpec((L, LANES), lambda j: (0, md.bk_s + j // g2))
    vcol = pl.BlockSpec((L, LANES), lambda j: (0, md.bv_s + j // g2))
    ocol = pl.BlockSpec((L, LANES), lambda j: (0, j))
    pvec = pl.BlockSpec((1, LANES), lambda j: (0, j))
    vec = pl.BlockSpec((1, LANES), lambda j: (0, 0))
    return pl.pallas_call(
        body, name=name, grid=(md.ps,),
        in_specs=[qcol, kcol, vcol, pvec, pvec, vec, vec, pl.BlockSpec((LANES, LANES), lambda j: (0, 0))],
        out_specs=[ocol, ocol],
        out_shape=[jax.ShapeDtypeStruct((L, md.ws), F32)] * 2,
        scratch_shapes=[pltpu.VMEM((L, LANES), BF16)] * 4,
        compiler_params=_cparams("parallel"),
    )(u, u, u, sinkp, slopep, qg, kg, bd)


def _swa_bwd(u, sinkp, slopep, qg, kg, do, lse, md, *, name):
    L = u.shape[0]
    blocks = _qblocks(L, QBLOCK // 2)
    bd = _block_diag_mean()
    fold = _fold_halves()
    g2 = md.g // 2
    npairs = md.ps

    def body(q_ref, k_ref, v_ref, sink_ref, slope_ref, qg_ref, kg_ref, bd_ref, fold_ref, do_ref, lse_ref,
             dq_ref, dk_ref, dv_ref, dsink_ref, dqg_ref, dkg_ref,
             qa_s, qb_s, kn_s, v_s, doa_s, dob_s, dqn_s):
        j = pl.program_id(0)
        half = _lane_half()
        bdv = bd_ref[...]
        qy, qh, rq = _head_norm(q_ref[...], qg_ref[...], bdv)
        ky, kh, rk = _head_norm(k_ref[...], kg_ref[...], bdv)
        _store_head_pair(qy, half, qa_s, qb_s)
        _store_head_pair(do_ref[...], half, doa_s, dob_s)
        kn_s[...] = ky.astype(BF16)
        v_s[...] = v_ref[...].astype(BF16)

        @pl.when(j % g2 == 0)
        def _():
            dk_ref[...] = jnp.zeros_like(dk_ref)
            dv_ref[...] = jnp.zeros_like(dv_ref)

        @pl.when(j == 0)
        def _():
            dqg_ref[...] = jnp.zeros_like(dqg_ref)
            dkg_ref[...] = jnp.zeros_like(dkg_ref)

        dsink = [jnp.zeros((1, 1), F32), jnp.zeros((1, 1), F32)]
        for r0, nr in blocks:
            k0 = max(r0 - BLOCK, 0)
            klen = r0 + nr - k0
            dq_blk = None
            for hh, (q_s, do_s) in enumerate(((qa_s, doa_s), (qb_s, dob_s))):
                c0 = HEAD_DIM * hh
                qm = q_s[r0:r0 + nr, :]
                dom = do_s[r0:r0 + nr, :]
                s = _swa_scores(qm, kn_s, slope_ref[0:1, c0:c0 + 1], k0, r0, nr, klen)
                lse_h = lse_ref[r0:r0 + nr, c0:c0 + 1]
                p = jnp.exp(s - lse_h)
                pdp = p * _dot_nt(dom, v_s[k0:k0 + klen, :])
                delta = jnp.sum(pdp, axis=-1, keepdims=True)
                p_sink = jnp.exp(sink_ref[0:1, c0:c0 + 1] - lse_h)
                dsink[hh] = dsink[hh] - jnp.sum(p_sink * delta, axis=0, keepdims=True)
                ds = (pdp - p * delta) * (HEAD_DIM ** -0.5)
                dq_h = jnp.dot(ds.astype(BF16), kn_s[k0:k0 + klen, :], preferred_element_type=F32)
                dk_ref[k0:k0 + klen, :] += _dot_tn(ds, qm)
                dv_ref[k0:k0 + klen, :] += _dot_tn(p, dom)
                dq_blk = dq_h if hh == 0 else jnp.where(half, dq_blk, dq_h)
            dqn_s[r0:r0 + nr, :] = dq_blk
        dq, dqg = _head_norm_bwd(dqn_s[...], qh, rq, qg_ref[...], bdv)
        dq_ref[...] = dq
        dqg_ref[...] += dqg
        dsink_ref[...] = jnp.where(half, jnp.broadcast_to(dsink[0], (1, LANES)), jnp.broadcast_to(dsink[1], (1, LANES)))

        @pl.when(j % g2 == g2 - 1)
        def _():
            dkn = _f32dot(dk_ref[...], fold_ref[...])
            dk, dkg = _head_norm_bwd(dkn, kh, rk, kg_ref[...], bdv)
            dk_ref[...] = jnp.where(half, dk, 0.0)
            dv_ref[...] = jnp.where(half, _f32dot(dv_ref[...], fold_ref[...]), 0.0)
            dkg_ref[...] += dkg

        @pl.when(j == npairs - 1)
        def _():
            dqg_ref[...] = _f32dot(jnp.broadcast_to(dqg_ref[...], (8, LANES)), fold_ref[...])[0:1, :]

    qcol = pl.BlockSpec((L, LANES), lambda j: (0, md.bq_s + j))
    kcol = pl.BlockSpec((L, LANES), lambda j: (0, md.bk_s + j // g2))
    vcol = pl.BlockSpec((L, LANES), lambda j: (0, md.bv_s + j // g2))
    ocol = pl.BlockSpec((L, LANES), lambda j: (0, j))
    kvout = pl.BlockSpec((L, LANES), lambda j: (0, j // g2))
    pvec = pl.BlockSpec((1, LANES), lambda j: (0, j))
    vec = pl.BlockSpec((1, LANES), lambda j: (0, 0))
    sq = pl.BlockSpec((LANES, LANES), lambda j: (0, 0))
    kvshape = jax.ShapeDtypeStruct((L, LANES * md.nkv), F32)
    small = jax.ShapeDtypeStruct((1, LANES), F32)
    return pl.pallas_call(
        body, name=name, grid=(md.ps,),
        in_specs=[qcol, kcol, vcol, pvec, pvec, vec, vec, sq, sq, ocol, ocol],
        out_specs=[ocol, kvout, kvout, pvec, vec, vec],
        out_shape=[jax.ShapeDtypeStruct((L, md.ws), F32), kvshape, kvshape,
                   jax.ShapeDtypeStruct((1, md.ws), F32), small, small],
        scratch_shapes=[pltpu.VMEM((L, LANES), BF16)] * 6 + [pltpu.VMEM((L, LANES), F32)],
        compiler_params=_cparams("arbitrary"),
    )(u, u, u, sinkp, slopep, qg, kg, bd, fold, do, lse)


def _outnorm_fwd(of, os_, gf, gs, *, name):
    L, wf = of.shape
    ws = os_.shape[1]
    tr = _pick(L, _TR)

    def body(of_ref, os_ref, gf_ref, gs_ref, o_ref):
        for src, g_ref, c0, w in ((of_ref, gf_ref, 0, wf), (os_ref, gs_ref, wf, ws)):
            x = src[...]
            r = lax.rsqrt(jnp.mean(x * x, axis=-1, keepdims=True) + EPS)
            o_ref[:, c0:c0 + w] = (x * r * g_ref[...]).astype(o_ref.dtype)

    return pl.pallas_call(
        body, name=name, grid=(L // tr,),
        in_specs=[pl.BlockSpec((tr, wf), lambda i: (i, 0)), pl.BlockSpec((tr, ws), lambda i: (i, 0)),
                  pl.BlockSpec((1, wf), lambda i: (0, 0)), pl.BlockSpec((1, ws), lambda i: (0, 0))],
        out_specs=pl.BlockSpec((tr, wf + ws), lambda i: (i, 0)),
        out_shape=jax.ShapeDtypeStruct((L, wf + ws), BF16),
        compiler_params=_cparams("parallel"),
    )(of, os_, gf, gs)


def _outnorm_bwd(don, of, os_, gf, gs, *, name):
    L, wf = of.shape
    ws = os_.shape[1]
    tr = _pick(L, _TR)

    def body(d_ref, of_ref, os_ref, gf_ref, gs_ref, dof_ref, dos_ref, dgf_ref, dgs_ref):
        i = pl.program_id(0)
        for src, g_ref, c0, w, dx_ref, dg_ref in ((of_ref, gf_ref, 0, wf, dof_ref, dgf_ref),
                                                  (os_ref, gs_ref, wf, ws, dos_ref, dgs_ref)):
            x = src[...]
            dy = d_ref[:, c0:c0 + w]
            r = lax.rsqrt(jnp.mean(x * x, axis=-1, keepdims=True) + EPS)
            xh = x * r
            dxh = dy * g_ref[...]
            dx_ref[...] = r * (dxh - xh * jnp.mean(dxh * xh, axis=-1, keepdims=True))
            part = jnp.sum(dy * xh, axis=0, keepdims=True)

            @pl.when(i == 0)
            def _():
                dg_ref[...] = part

            @pl.when(i > 0)
            def _():
                dg_ref[...] += part

    rf = pl.BlockSpec((tr, wf), lambda i: (i, 0))
    rs = pl.BlockSpec((tr, ws), lambda i: (i, 0))
    vf = pl.BlockSpec((1, wf), lambda i: (0, 0))
    vs = pl.BlockSpec((1, ws), lambda i: (0, 0))
    return pl.pallas_call(
        body, name=name, grid=(L // tr,),
        in_specs=[pl.BlockSpec((tr, wf + ws), lambda i: (i, 0)), rf, rs, vf, vs],
        out_specs=[rf, rs, vf, vs],
        out_shape=[jax.ShapeDtypeStruct((L, wf), F32), jax.ShapeDtypeStruct((L, ws), F32),
                   jax.ShapeDtypeStruct((1, wf), F32), jax.ShapeDtypeStruct((1, ws), F32)],
        compiler_params=_cparams("arbitrary"),
    )(don, of, os_, gf, gs)


def _win_to_mine(w, md):
    d = w.shape[0]
    wf, ws, hd = md.wf, md.ws, HEAD_DIM
    o_z = 3 * wf
    o_sq = o_z + md.hf
    o_sk = o_sq + ws
    o_sv = o_sk + md.nkv * hd
    parts = [w[:, :3 * wf], w[:, o_sq:o_sq + ws]]
    for base in (o_sk, o_sv):
        for kv in range(md.nkv):
            blk = w[:, base + kv * hd:base + (kv + 1) * hd]
            parts += [blk, blk]
    z = w[:, o_z:o_z + md.hf].reshape(d, md.hf // 2, 2)
    z = jnp.pad(z, ((0, 0), (0, 0), (0, 6))).reshape(d, 4 * md.hf)
    parts.append(jnp.pad(z, ((0, 0), (0, LANES - 4 * md.hf + md.nup - md.nu))))
    return jnp.concatenate(parts, axis=1)


def _win_grad_to_ref(dw, md):
    d = dw.shape[0]
    wf, ws, hd = md.wf, md.ws, HEAD_DIM
    z = dw[:, md.bz * LANES:md.bz * LANES + 4 * md.hf].reshape(d, md.hf // 2, 8)[:, :, :2].reshape(d, md.hf)
    parts = [dw[:, :3 * wf], z, dw[:, md.bq_s * LANES:md.bq_s * LANES + ws]]
    for base in (md.bk_s, md.bv_s):
        for kv in range(md.nkv):
            c0 = (base + kv) * LANES
            parts.append(dw[:, c0:c0 + hd])
    return jnp.concatenate(parts, axis=1)


def _mix_small(p, md):
    tile2 = lambda v: jnp.tile(v.reshape(1, HEAD_DIM), (1, 2))
    b = p["b_forget"].reshape(md.hf // 2, 2)
    b = jnp.pad(b, ((0, 0), (0, 6))).reshape(1, 4 * md.hf)
    slopes = np.asarray(2.0 ** (-8.0 * np.arange(1, md.hq + 1) / md.hq), np.float32)
    return dict(
        g_mix=p["mix_norm"].reshape(1, -1),
        b_gate=jnp.pad(b, ((0, 0), (0, LANES - 4 * md.hf))),
        fqg=tile2(p["fox_q_norm"]), fkg=tile2(p["fox_k_norm"]),
        sqg=tile2(p["swa_q_norm"]), skg=tile2(p["swa_k_norm"]),
        sinkp=jnp.repeat(p["swa_sinks"], HEAD_DIM).reshape(1, md.ws),
        slopep=jnp.asarray(np.repeat(slopes, HEAD_DIM).reshape(1, md.ws)),
        gfo=p["fox_out_norm"].reshape(1, md.wf), gso=p["swa_out_norm"].reshape(1, md.ws),
    )


def _mix_fwd(h, sp, w, md, queue=None):
    L = h.shape[0]
    hn = _rms_fwd(h, sp["g_mix"], name="mix_rms")
    u = _mm(queue, hn, w(0), name="mix_u")
    cexp, ct = _gate_fwd(u, sp["b_gate"], md, name="gate_fwd")
    ct3 = ct[:8 * md.pf].reshape(md.pf, 8, L)
    of, lsef = _fox_fwd(u, cexp, ct3, sp["fqg"], sp["fkg"], md, name="fox_fwd")
    os_, lses = _swa_fwd(u, sp["sinkp"], sp["slopep"], sp["sqg"], sp["skg"], md, name="swa_fwd")
    on = _outnorm_fwd(of, os_, sp["gfo"], sp["gso"], name="outnorm_fwd")
    h_out = _mm(queue, on, w(1), name="mix_out", residual=h)
    return h_out, (h, hn, u, cexp, ct3, of, lsef, os_, lses, on)


def _mix_bwd(dh, saved, sp, w_in, w_out, md, queue=None):
    h, hn, u, cexp, ct3, of, lsef, os_, lses, on = saved
    L = h.shape[0]
    dhb = _cast_bf16(dh, name="mix_dhb")
    don = _mm(queue, dhb, w_out, name="mix_don", nt=True)
    dw_out = _mm(queue, _transpose(on, name="mix_onT"), dhb, name="mix_dwout", out_dtype=BF16)
    dof, dos, dgfo, dgso = _outnorm_bwd(don, of, os_, sp["gfo"], sp["gso"], name="outnorm_bwd")
    duq, duk, duv, dck, dfqg, dfkg = _fox_bwd(u, cexp, ct3, sp["fqg"], sp["fkg"], dof, lsef, md, name="fox_bwd")
    dsq, dsk, dsv, dsinkp, dsqg, dskg = _swa_bwd(u, sp["sinkp"], sp["slopep"], sp["sqg"], sp["skg"], dos, lses, md, name="swa_bwd")
    dck_t = jnp.pad(dck.reshape(8 * md.pf, L), ((0, LANES - 8 * md.pf), (0, 0)))
    dz, db = _gate_bwd(u, sp["b_gate"], dck_t, md, name="gate_bwd")
    du = jnp.concatenate([duq, duk, duv, dsq, dsk, dsv, dz, jnp.zeros((L, md.nup - md.nu), F32)], axis=1).astype(BF16)
    dhn = _mm(queue, du, w_in, name="mix_dhn", nt=True)
    dw_in = _mm(queue, _transpose(hn, name="mix_hnT"), du, name="mix_dwin", out_dtype=BF16)
    dh_in, dg_mix = _rms_bwd(dhn, h, sp["g_mix"], dh, name="mix_drms")
    small = dict(
        mix_norm=dg_mix.reshape(-1),
        b_forget=db[0, :4 * md.hf].reshape(md.hf // 2, 8)[:, :2].reshape(md.hf),
        fox_q_norm=dfqg[0, :HEAD_DIM], fox_k_norm=dfkg[0, :HEAD_DIM],
        swa_q_norm=dsqg[0, :HEAD_DIM], swa_k_norm=dskg[0, :HEAD_DIM],
        swa_sinks=dsinkp[0, ::HEAD_DIM],
        fox_out_norm=dgfo.reshape(-1), swa_out_norm=dgso.reshape(-1),
    )
    return dh_in, dw_in, dw_out, small


_ANY = pl.BlockSpec(memory_space=pl.ANY)
_HALF_ROWS = (512, 352, 256, 192, 128, 64, 32, 16)


def _mesh_pos():
    return lax.axis_index("x"), lax.axis_index("y"), lax.axis_index("c")


def _other_chips(x, y):
    return [(1 - x, y), (x, 1 - y), (1 - x, 1 - y)]


def _rows_half(ref, which):
    rh = ref.shape[-2] // 2
    if len(ref.shape) == 2:
        return ref.at[pl.ds(which * rh, rh), :]
    return ref.at[:, pl.ds(which * rh, rh), :]


def _remote(src, dst, send_sems, recv_sems, idx, dev):
    return pltpu.make_async_remote_copy(src_ref=src, dst_ref=dst, send_sem=send_sems.at[idx], recv_sem=recv_sems.at[idx],
                                        device_id=dev, device_id_type=MESH)


def _cast_into_chunk(w, l_idx, k_idx, *, name):
    _, rows, cols = w.shape
    tr = _pick(rows, (512, 352, 256, 128, 64, 32, 16))

    def body(l_ref, k_ref, w_ref, o_ref):
        o_ref[...] = w_ref[...].astype(o_ref.dtype)

    return pl.pallas_call(
        body, name=name,
        grid_spec=pltpu.PrefetchScalarGridSpec(
            num_scalar_prefetch=2, grid=(rows // tr,),
            in_specs=[pl.BlockSpec((None, tr, cols), lambda i, l, k: (l[0], i, 0))],
            out_specs=pl.BlockSpec((None, tr, cols), lambda i, l, k: (k[0], i, 0))),
        out_shape=jax.ShapeDtypeStruct((NCH, rows, cols), BF16),
        compiler_params=_cparams("parallel"),
    )(l_idx, k_idx, w)


def _allgather_chips(arrs, *, name):
    n = len(arrs)

    def body(*refs):
        outs = refs[n:2 * n]
        send_sems, recv_sems = refs[2 * n:]
        x, y, c = _mesh_pos()
        k = 2 * x + y
        sibling = (x, y, 1 - c)
        chips = _other_chips(x, y)
        sent = []
        for i in range(n):
            mine = _rows_half(outs[i].at[k], c)
            for j, (cx, cy) in enumerate(chips):
                cp = _remote(mine, mine, send_sems, recv_sems, 6 * i + j, (cx, cy, c))
                cp.start()
                sent.append(cp)
        for i in range(n):
            for j, (cx, cy) in enumerate(chips):
                blk = _rows_half(outs[i].at[2 * cx + cy], c)
                _remote(blk, blk, send_sems, recv_sems, 6 * i + j, (cx, cy, c)).wait_recv()
                fwd = _remote(blk, blk, send_sems, recv_sems, 6 * i + 3 + j, sibling)
                fwd.start()
                sent.append(fwd)
        for i in range(n):
            for j, (cx, cy) in enumerate(chips):
                blk = _rows_half(outs[i].at[2 * cx + cy], 1 - c)
                _remote(blk, blk, send_sems, recv_sems, 6 * i + 3 + j, sibling).wait_recv()
        for cp in sent:
            cp.wait_send()

    return pl.pallas_call(
        body, name=name, in_specs=[_ANY] * n, out_specs=[_ANY] * n,
        out_shape=[jax.ShapeDtypeStruct(a.shape, a.dtype) for a in arrs],
        input_output_aliases={i: i for i in range(n)},
        scratch_shapes=[pltpu.SemaphoreType.DMA((6 * n,)), pltpu.SemaphoreType.DMA((6 * n,))],
    )(*arrs)


def _exchange_sibling_halves(arrs, *, name):
    n = len(arrs)

    def body(*refs):
        srcs, lands = refs[:n], refs[n:2 * n]
        send_sems, recv_sems = refs[2 * n:]
        x, y, c = _mesh_pos()
        sibling = (x, y, 1 - c)
        cps = [_remote(_rows_half(srcs[i], 1 - c), lands[i], send_sems, recv_sems, i, sibling) for i in range(n)]
        for cp in cps:
            cp.start()
        for cp in cps:
            cp.wait_recv()
        for cp in cps:
            cp.wait_send()

    return pl.pallas_call(
        body, name=name, in_specs=[_ANY] * n, out_specs=[_ANY] * n,
        out_shape=[jax.ShapeDtypeStruct((NCH, a.shape[1] // 2, a.shape[2]), a.dtype) for a in arrs],
        scratch_shapes=[pltpu.SemaphoreType.DMA((n,)), pltpu.SemaphoreType.DMA((n,))],
    )(*arrs)


class _ChipExchange:
    n_sems = NCH - 1

    def __init__(self, s, r0, r1, land=None):
        self.r0, self.r1 = r0, r1
        self.operands = [s] if land is None else [s, land]
        self.out_shapes = [jax.ShapeDtypeStruct((NCH - 1,) + s.shape[1:], s.dtype)]
        self.aliases = {} if land is None else {1: 0}

    def copies(self, cin, cout, send_sems, recv_sems):
        x, y, c = _mesh_pos()
        rows = pl.ds(self.r0, self.r1 - self.r0)
        cps = [_remote(cin[0].at[2 * cx + cy, rows, :], cout[0].at[j, rows, :], send_sems, recv_sems, j, (cx, cy, c))
               for j, (cx, cy) in enumerate(_other_chips(x, y))]
        return [(cp, cp) for cp in cps]


class _GatherStep:
    n_sems = 2 * (NCH - 1)

    def __init__(self, first, second):
        self.has_first = first is not None
        self.operands = [b for b in (first, second) if b is not None]
        self.out_shapes = [jax.ShapeDtypeStruct(b.shape, b.dtype) for b in self.operands]
        self.aliases = {i: i for i in range(len(self.operands))}

    def copies(self, cin, cout, send_sems, recv_sems):
        x, y, c = _mesh_pos()
        k = 2 * x + y
        sibling = (x, y, 1 - c)
        bufs = list(cout)
        out = []
        if self.has_first:
            buf = bufs.pop(0)
            mine = _rows_half(buf.at[k], c)
            for j, (cx, cy) in enumerate(_other_chips(x, y)):
                theirs = _rows_half(buf.at[2 * cx + cy], c)
                out.append((_remote(mine, mine, send_sems, recv_sems, j, (cx, cy, c)),
                            _remote(theirs, theirs, send_sems, recv_sems, j, (cx, cy, c))))
        if bufs:
            buf = bufs.pop(0)
            for j, (cx, cy) in enumerate(_other_chips(x, y)):
                landed = _rows_half(buf.at[2 * cx + cy], c)
                theirs = _rows_half(buf.at[2 * cx + cy], 1 - c)
                out.append((_remote(landed, landed, send_sems, recv_sems, NCH - 1 + j, sibling),
                            _remote(theirs, theirs, send_sems, recv_sems, NCH - 1 + j, sibling)))
        return out


class _GatherQueue:
    def __init__(self, bufs):
        self.bufs = list(bufs)
        self.bufs[0:2] = _allgather_chips(self.bufs[0:2], name="first_allgather")
        self.carrier = 0

    def take(self):
        j = self.carrier
        self.carrier += 1
        first = j + 2 if j + 2 < len(self.bufs) else None
        second = j + 1 if 2 <= j + 1 < len(self.bufs) else None
        self.cur = [i for i in (first, second) if i is not None]
        if not self.cur:
            return None
        return _GatherStep(None if first is None else self.bufs[first], None if second is None else self.bufs[second])

    def give(self, outs):
        for i, buf in zip(self.cur, outs):
            self.bufs[i] = buf


def _run_exchange(job, *, name):
    n_in, n_out = len(job.operands), len(job.out_shapes)

    def body(*refs):
        cps = job.copies(refs[:n_in], refs[n_in:n_in + n_out], refs[-2], refs[-1])
        for cp, _ in cps:
            cp.start()
        for _, arrival in cps:
            arrival.wait_recv()
        for cp, _ in cps:
            cp.wait_send()

    return pl.pallas_call(
        body, name=name, in_specs=[_ANY] * n_in, out_specs=[_ANY] * n_out, out_shape=list(job.out_shapes),
        input_output_aliases=dict(job.aliases),
        scratch_shapes=[pltpu.SemaphoreType.DMA((job.n_sems,)), pltpu.SemaphoreType.DMA((job.n_sems,))],
    )(*job.operands)


class _ExchangeQueue:
    def __init__(self, parts=2):
        self.parts = parts
        self.todo = []
        self.groups = {}

    def add(self, group, sums, done):
        self.groups[group] = dict(sums=sums, lands=[None] * len(sums), left=len(sums) * self.parts, done=done)
        self.todo += [(group, ai, p) for ai in range(len(sums)) for p in range(self.parts)]

    def take(self):
        if not self.todo:
            return None
        group, ai, p = self.cur = self.todo.pop(0)
        g = self.groups[group]
        step = g['sums'][ai].shape[1] // self.parts
        return _ChipExchange(g['sums'][ai], p * step, (p + 1) * step, g['lands'][ai])

    def give(self, outs):
        group, ai, _ = self.cur
        g = self.groups[group]
        g['lands'][ai] = outs[0]
        g['left'] -= 1

    def finish_ready(self):
        for group in [k for k, g in self.groups.items() if g['left'] == 0]:
            g = self.groups.pop(group)
            g['done'](g['sums'], g['lands'])

    def drain(self, name):
        while (job := self.take()) is not None:
            self.give(_run_exchange(job, name=name))
        self.finish_ready()


def _mm(queue, a, b, **kw):
    job = queue.take() if queue is not None else None
    if job is None:
        return _matmul(a, b, **kw)
    out, *landed = _matmul(a, b, comm=job, **kw)
    queue.give(landed)
    return out


def _share_with_sibling(arrs, *, name):
    n = len(arrs)

    def body(*refs):
        outs = refs[n:2 * n]
        send_sems, recv_sems = refs[2 * n:]
        x, y, c = _mesh_pos()
        sibling = (x, y, 1 - c)
        cps = []
        for i in range(n):
            mine = _rows_half(outs[i], c)
            cps.append(_remote(mine, mine, send_sems, recv_sems, i, sibling))
        for cp in cps:
            cp.start()
        for i in range(n):
            theirs = _rows_half(outs[i], 1 - c)
            _remote(theirs, theirs, send_sems, recv_sems, i, sibling).wait_recv()
        for cp in cps:
            cp.wait_send()

    return pl.pallas_call(
        body, name=name, in_specs=[_ANY] * n, out_specs=[_ANY] * n,
        out_shape=[jax.ShapeDtypeStruct(a.shape, a.dtype) for a in arrs],
        input_output_aliases={i: i for i in range(n)},
        scratch_shapes=[pltpu.SemaphoreType.DMA((n,)), pltpu.SemaphoreType.DMA((n,))],
    )(*arrs)


def _allgather_devices(v, *, name):
    m = v.shape[0]

    def body(v_ref, out_ref, send_sems, recv_sems, local_sem):
        x, y, c = _mesh_pos()
        mine = 4 * x + 2 * y + c
        own = pltpu.make_async_copy(v_ref, out_ref.at[mine], local_sem)
        own.start()
        cps = []
        for r in range(1, 8):
            px, py, pc = (x + (r >> 2)) % 2, (y + ((r >> 1) & 1)) % 2, (c + (r & 1)) % 2
            cps.append((_remote(v_ref, out_ref.at[mine], send_sems, recv_sems, r - 1, (px, py, pc)), 4 * px + 2 * py + pc))
        for cp, _ in cps:
            cp.start()
        for r, (cp, theirs) in enumerate(cps):
            blk = out_ref.at[theirs]
            _remote(blk, blk, send_sems, recv_sems, r, (x, y, c)).wait_recv()
        for cp, _ in cps:
            cp.wait_send()
        own.wait()

    return pl.pallas_call(
        body, name=name, in_specs=[_ANY], out_specs=_ANY,
        out_shape=jax.ShapeDtypeStruct((8, m, LANES), v.dtype),
        scratch_shapes=[pltpu.SemaphoreType.DMA((7,)), pltpu.SemaphoreType.DMA((7,)), pltpu.SemaphoreType.DMA],
    )(v)


def _add_own_half(g, land, c_idx, *, name):
    nch, rh, cols = land.shape
    tr = _pick(rh, _HALF_ROWS)
    nt = rh // tr

    def body(c_ref, g_ref, l_ref, o_ref):
        o_ref[...] = (g_ref[...].astype(F32) + l_ref[...].astype(F32)).astype(o_ref.dtype)

    return pl.pallas_call(
        body, name=name,
        grid_spec=pltpu.PrefetchScalarGridSpec(
            num_scalar_prefetch=1, grid=(nch, nt),
            in_specs=[pl.BlockSpec((None, tr, cols), lambda k, i, c: (k, c[0] * nt + i, 0)),
                      pl.BlockSpec((None, tr, cols), lambda k, i, c: (k, i, 0))],
            out_specs=pl.BlockSpec((None, tr, cols), lambda k, i, c: (k, i, 0))),
        out_shape=jax.ShapeDtypeStruct(land.shape, BF16),
        compiler_params=_cparams("parallel", "parallel"),
    )(c_idx, g, land)


def _add_chunks(s, land, k_idx, c_idx, *, name):
    _, rh, cols = s.shape
    tr = _pick(rh, _HALF_ROWS)
    nt = rh // tr

    def body(k_ref, c_ref, s_ref, l_ref, o_ref):
        t = s_ref[...].astype(F32)
        for j in range(NCH - 1):
            t = t + l_ref[j].astype(F32)
        o_ref[...] = t

    return pl.pallas_call(
        body, name=name,
        grid_spec=pltpu.PrefetchScalarGridSpec(
            num_scalar_prefetch=2, grid=(nt,),
            in_specs=[pl.BlockSpec((None, tr, cols), lambda i, k, c: (k[0], i, 0)),
                      pl.BlockSpec((NCH - 1, tr, cols), lambda i, k, c: (0, i, 0))],
            out_specs=pl.BlockSpec((tr, cols), lambda i, k, c: (c[0] * nt + i, 0))),
        out_shape=jax.ShapeDtypeStruct((2 * rh, cols), F32),
        compiler_params=_cparams("parallel"),
    )(k_idx, c_idx, s, land)


def _sum_devices(v, *, name):
    _, m, _ = v.shape

    def body(v_ref, o_ref):
        t = v_ref[0]
        for d in range(1, 8):
            t = t + v_ref[d]
        o_ref[...] = t

    return pl.pallas_call(
        body, name=name, grid=(1,),
        in_specs=[pl.BlockSpec((8, m, LANES), lambda i: (0, 0, 0))],
        out_specs=pl.BlockSpec((m, LANES), lambda i: (0, 0)),
        out_shape=jax.ShapeDtypeStruct((m, LANES), F32),
        compiler_params=_cparams("arbitrary"),
    )(v)


def _adamw_math(w, g, m, v):
    m = ADAM_B1 * m + (1.0 - ADAM_B1) * g
    v = ADAM_B2 * v + (1.0 - ADAM_B2) * (g * g)
    m_hat = m / (1.0 - ADAM_B1 ** ADAM_STEP)
    v_hat = v / (1.0 - ADAM_B2 ** ADAM_STEP)
    delta = -ADAM_LR * (m_hat / (jnp.sqrt(v_hat) + ADAM_EPS) + ADAM_WD * w)
    return delta, m, v


def _adamw_layer(w, m, v, g, layer, prev, *, name):
    depth, rows, cols = w.shape
    tr = _pick(rows, (256, 128, 64, 32, 16, 8))
    lay = pl.BlockSpec((None, tr, cols), lambda i, l: (l[0], i, 0))
    n_prev = 0 if prev is None else 4

    def body(l_ref, w_ref, m_ref, v_ref, g_ref, *rest):
        go_ref, d_ref, mo_ref, vo_ref = rest[n_prev:]
        g = g_ref[...]
        delta, m_new, v_new = _adamw_math(w_ref[...], g, m_ref[...], v_ref[...])
        go_ref[...] = g
        d_ref[...] = delta
        mo_ref[...] = m_new
        vo_ref[...] = v_new

    stack = jax.ShapeDtypeStruct(w.shape, F32)
    return pl.pallas_call(
        body, name=name,
        grid_spec=pltpu.PrefetchScalarGridSpec(
            num_scalar_prefetch=1, grid=(rows // tr,),
            in_specs=[lay, lay, lay, pl.BlockSpec((tr, cols), lambda i, l: (i, 0))] + [_ANY] * n_prev,
            out_specs=[lay] * 4),
        out_shape=[stack] * 4,
        input_output_aliases={} if prev is None else {5 + q: q for q in range(4)},
        compiler_params=_cparams("parallel"),
    )(layer, w, m, v, g, *(() if prev is None else prev))


def _adamw_flat(w, g, m, v, *, name):
    def body(w_ref, g_ref, m_ref, v_ref, d_ref, mo_ref, vo_ref):
        d_ref[...], mo_ref[...], vo_ref[...] = _adamw_math(w_ref[...], g_ref[...], m_ref[...], v_ref[...])

    blk = pl.BlockSpec(w.shape, lambda i: (0, 0))
    return pl.pallas_call(
        body, name=name, grid=(1,), in_specs=[blk] * 4, out_specs=[blk] * 3,
        out_shape=[jax.ShapeDtypeStruct(w.shape, F32)] * 3, compiler_params=_cparams("arbitrary"),
    )(w, g, m, v)


def _reduce_scatter_begin(parts, c_idx, tag):
    lands = _exchange_sibling_halves(parts, name=f"{tag}_rs_sibling")
    return [_add_own_half(p, l, c_idx, name=f"{tag}_rs_add2") for p, l in zip(parts, lands)]


def _reduce_scatter_end(sums, lands, c_idx, k_idx, tag):
    tots = [_add_chunks(s, l, k_idx, c_idx, name=f"{tag}_rs_add4") for s, l in zip(sums, lands)]
    return _share_with_sibling(tots, name=f"{tag}_rs_share")


_WEIGHTS = ('meta_tokens', 'ffn1_norm', 'ffn1_w_gate', 'ffn1_w_up', 'ffn1_w_down', 'mix_norm', 'w_in', 'b_forget',
            'fox_q_norm', 'fox_k_norm', 'swa_q_norm', 'swa_k_norm', 'swa_sinks', 'fox_out_norm', 'swa_out_norm', 'w_out',
            'ffn2_norm', 'ffn2_w_gate', 'ffn2_w_up', 'ffn2_w_down')
_BIG = ('ffn1_w_gate', 'ffn1_w_up', 'ffn1_w_down', 'w_in', 'w_out', 'ffn2_w_gate', 'ffn2_w_up', 'ffn2_w_down')
_SMALL = tuple(n for n in _WEIGHTS if n not in _BIG and n != 'meta_tokens')
_MIX_SMALL = ('mix_norm', 'b_forget', 'fox_q_norm', 'fox_k_norm', 'swa_q_norm', 'swa_k_norm', 'swa_sinks',
              'fox_out_norm', 'swa_out_norm')


def _pack_rows(vectors):
    flat = jnp.concatenate([v.reshape(-1) for v in vectors])
    n = flat.shape[0]
    m = -(-n // (8 * LANES)) * 8
    return jnp.pad(flat, (0, m * LANES - n)).reshape(m, LANES)


def _unpack_rows(packed, shapes):
    flat = packed.reshape(-1)
    out, o = [], 0
    for s in shapes:
        n = int(np.prod(s))
        out.append(flat[o:o + n].reshape(s))
        o += n
    return out


def kernel(x, meta_tokens, ffn1_norm, ffn1_w_gate, ffn1_w_up, ffn1_w_down, mix_norm, w_in, b_forget, fox_q_norm, fox_k_norm, swa_q_norm, swa_k_norm, swa_sinks, fox_out_norm, swa_out_norm, w_out, ffn2_norm, ffn2_w_gate, ffn2_w_up, ffn2_w_down, loss_target, m_meta_tokens, m_ffn1_norm, m_ffn1_w_gate, m_ffn1_w_up, m_ffn1_w_down, m_mix_norm, m_w_in, m_b_forget, m_fox_q_norm, m_fox_k_norm, m_swa_q_norm, m_swa_k_norm, m_swa_sinks, m_fox_out_norm, m_swa_out_norm, m_w_out, m_ffn2_norm, m_ffn2_w_gate, m_ffn2_w_up, m_ffn2_w_down, v_meta_tokens, v_ffn1_norm, v_ffn1_w_gate, v_ffn1_w_up, v_ffn1_w_down, v_mix_norm, v_w_in, v_b_forget, v_fox_q_norm, v_fox_k_norm, v_swa_q_norm, v_swa_k_norm, v_swa_sinks, v_fox_out_norm, v_swa_out_norm, v_w_out, v_ffn2_norm, v_ffn2_w_gate, v_ffn2_w_up, v_ffn2_w_down):
    W = dict(meta_tokens=meta_tokens, ffn1_norm=ffn1_norm, ffn1_w_gate=ffn1_w_gate, ffn1_w_up=ffn1_w_up, ffn1_w_down=ffn1_w_down, mix_norm=mix_norm, w_in=w_in, b_forget=b_forget, fox_q_norm=fox_q_norm, fox_k_norm=fox_k_norm, swa_q_norm=swa_q_norm, swa_k_norm=swa_k_norm, swa_sinks=swa_sinks, fox_out_norm=fox_out_norm, swa_out_norm=swa_out_norm, w_out=w_out, ffn2_norm=ffn2_norm, ffn2_w_gate=ffn2_w_gate, ffn2_w_up=ffn2_w_up, ffn2_w_down=ffn2_w_down)
    Mo = dict(meta_tokens=m_meta_tokens, ffn1_norm=m_ffn1_norm, ffn1_w_gate=m_ffn1_w_gate, ffn1_w_up=m_ffn1_w_up, ffn1_w_down=m_ffn1_w_down, mix_norm=m_mix_norm, w_in=m_w_in, b_forget=m_b_forget, fox_q_norm=m_fox_q_norm, fox_k_norm=m_fox_k_norm, swa_q_norm=m_swa_q_norm, swa_k_norm=m_swa_k_norm, swa_sinks=m_swa_sinks, fox_out_norm=m_fox_out_norm, swa_out_norm=m_swa_out_norm, w_out=m_w_out, ffn2_norm=m_ffn2_norm, ffn2_w_gate=m_ffn2_w_gate, ffn2_w_up=m_ffn2_w_up, ffn2_w_down=m_ffn2_w_down)
    Vo = dict(meta_tokens=v_meta_tokens, ffn1_norm=v_ffn1_norm, ffn1_w_gate=v_ffn1_w_gate, ffn1_w_up=v_ffn1_w_up, ffn1_w_down=v_ffn1_w_down, mix_norm=v_mix_norm, w_in=v_w_in, b_forget=v_b_forget, fox_q_norm=v_fox_q_norm, fox_k_norm=v_fox_k_norm, swa_q_norm=v_swa_q_norm, swa_k_norm=v_swa_k_norm, swa_sinks=v_swa_sinks, fox_out_norm=v_fox_out_norm, swa_out_norm=v_swa_out_norm, w_out=v_w_out, ffn2_norm=v_ffn2_norm, ffn2_w_gate=v_ffn2_w_gate, ffn2_w_up=v_ffn2_w_up, ffn2_w_down=v_ffn2_w_down)

    _, S, D = x.shape
    L = S + BLOCK
    depth = ffn1_norm.shape[0]
    md = _MixDims(D)
    mx, my, mc = _mesh_pos()
    k_idx = (2 * mx + my).astype(jnp.int32).reshape(1)
    c_idx = mc.astype(jnp.int32).reshape(1)
    dcols = D // NCH

    meta_all = _allgather_devices(meta_tokens.reshape(-1, LANES), name="meta_allgather")
    meta_full = jnp.transpose(meta_all[0::2].reshape(NCH, N_META, dcols), (1, 0, 2)).reshape(N_META, D)

    order = ('ffn1_w_gate', 'ffn1_w_up', 'ffn1_w_down', 'w_in', 'w_out', 'ffn2_w_gate', 'ffn2_w_up', 'ffn2_w_down')
    chunks = []
    for l in range(depth):
        l_idx = jnp.full((1,), l, jnp.int32)
        chunks += [_cast_into_chunk(W[name], l_idx, k_idx, name="cast_chunk") for name in order]
    gather = _GatherQueue(chunks)
    wts = [{} for _ in range(depth)]

    def weight(l, i):
        if i not in wts[l]:
            buf = gather.bufs[len(order) * l + i]
            if order[i] == 'w_in':
                buf = _win_to_mine(jnp.transpose(buf, (1, 0, 2)).reshape(D, NCH * buf.shape[2]), md)
            elif order[i] in ('w_out', 'ffn1_w_down', 'ffn2_w_down'):
                buf = buf.reshape(-1, D)
            wts[l][i] = buf
        return wts[l][i]

    h = jnp.concatenate([jnp.zeros((PAD, D), F32), meta_full, x[0]], axis=0)
    saved = []
    for l in range(depth):
        sp = _mix_small({n: W[n][l] for n in _MIX_SMALL}, md)
        h, s1 = _ffn_fwd(h, ffn1_norm[l].reshape(1, D), lambda i: weight(l, i), "ffn", gather)
        h, s2 = _mix_fwd(h, sp, lambda i: weight(l, 3 + i), md, gather)
        h, s3 = _ffn_fwd(h, ffn2_norm[l].reshape(1, D), lambda i: weight(l, 5 + i), "ffn", gather)
        saved.append((s1, s2, s3, sp))
    wts = [dict(g1=w[0], u1=w[1], d1=w[2], wi=w[3], wo=w[4], g2=w[5], u2=w[6], d2=w[7]) for w in wts]

    loss_part, dh = _loss_grad(h, loss_target[0], name="loss_grad")

    small_grads = {n: [None] * depth for n in _SMALL}
    stacks = {n: None for n in _BIG}

    def update(name, l, grad):
        stacks[name] = _adamw_layer(W[name], Mo[name], Vo[name], grad, jnp.full((1,), l, jnp.int32), stacks[name],
                                    name="adamw_layer")

    queue = _ExchangeQueue()

    def reduce_later(names, l, parts, tag):
        def done(sums, lands):
            for n, grad in zip(names, _reduce_scatter_end(sums, lands, c_idx, k_idx, tag)):
                update(n, l, grad)
        queue.add((tag, names[0], l), _reduce_scatter_begin(parts, c_idx, tag), done)

    for l in range(depth - 1, -1, -1):
        wl = wts[l]
        s1, s2, s3, sp = saved[l]
        dh, dg, dwg, dwu, dwd = _ffn_bwd(dh, s3, ffn2_norm[l].reshape(1, D), wl['g2'], wl['u2'], wl['d2'], "ffn", queue)
        queue.finish_ready()
        small_grads['ffn2_norm'][l] = dg.reshape(-1)
        reduce_later(('ffn2_w_gate', 'ffn2_w_up', 'ffn2_w_down'), l, [dwg, dwu, dwd.reshape(NCH, -1, D)], "ffn")

        dh, dwi, dwo, sm = _mix_bwd(dh, s2, sp, wl['wi'], wl['wo'], md, queue)
        queue.finish_ready()
        for n in _MIX_SMALL:
            small_grads[n][l] = sm[n]
        dwi = _win_grad_to_ref(dwi, md)
        dwi = jnp.transpose(dwi.reshape(D, NCH, -1), (1, 0, 2))
        reduce_later(('w_in', 'w_out'), l, [dwi, dwo.reshape(NCH, -1, D)], "mix")

        dh, dg, dwg, dwu, dwd = _ffn_bwd(dh, s1, ffn1_norm[l].reshape(1, D), wl['g1'], wl['u1'], wl['d1'], "ffn", queue)
        queue.finish_ready()
        small_grads['ffn1_norm'][l] = dg.reshape(-1)
        reduce_later(('ffn1_w_gate', 'ffn1_w_up', 'ffn1_w_down'), l, [dwg, dwu, dwd.reshape(NCH, -1, D)], "ffn")
    queue.drain("rs_chips")

    grad_x = dh[BLOCK:][None]

    small_shapes = [W[n].shape for n in _SMALL]
    parts = [jnp.stack(small_grads[n]) for n in _SMALL] + [dh[PAD:BLOCK], loss_part[0, :1]]
    packed = _pack_rows(parts)
    total = _sum_devices(_allgather_devices(packed, name="small_allgather"), name="small_sum")
    *g_small, g_meta, loss = _unpack_rows(total, small_shapes + [(N_META, D), (1,)])
    g_meta = lax.dynamic_slice(g_meta, (0, k_idx[0] * dcols), (N_META, dcols))

    sw = _pack_rows([W[n] for n in _SMALL])
    sd, smm, svv = _adamw_flat(sw, _pack_rows(g_small), _pack_rows([Mo[n] for n in _SMALL]),
                               _pack_rows([Vo[n] for n in _SMALL]), name="adamw_small")
    d_small, m_small, v_small = (_unpack_rows(t, small_shapes) for t in (sd, smm, svv))
    d_meta, m_meta, v_meta = _adamw_flat(meta_tokens, g_meta, m_meta_tokens, v_meta_tokens, name="adamw_meta")

    grads, deltas, new_m, new_v = {}, {}, {}, {}
    for n in _BIG:
        grads[n], deltas[n], new_m[n], new_v[n] = stacks[n]
    for i, n in enumerate(_SMALL):
        grads[n], deltas[n], new_m[n], new_v[n] = g_small[i], d_small[i], m_small[i], v_small[i]
    grads['meta_tokens'], deltas['meta_tokens'], new_m['meta_tokens'], new_v['meta_tokens'] = g_meta, d_meta, m_meta, v_meta
    return (loss.reshape(()), grad_x, *[grads[n] for n in _WEIGHTS], *[deltas[n] for n in _WEIGHTS],
            *[new_m[n] for n in _WEIGHTS], *[new_v[n] for n in _WEIGHTS])
```

```python
import numpy as np
import jax
import jax.numpy as jnp
from jax import lax
from jax.experimental import pallas as pl
from jax.experimental.pallas import tpu as pltpu

F32 = jnp.float32
BF16 = jnp.bfloat16

HEAD_DIM = 64
N_META = 16
BLOCK = 128
WINDOW = 128
PAD = BLOCK - N_META
EPS = 1e-6
NEG_INF = -1e30
SWA_GROUP = 8
NCH = 4
LANES = 128
QBLOCK = 512

ADAM_LR = 0.001
ADAM_B1 = 0.9
ADAM_B2 = 0.999
ADAM_EPS = 1e-08
ADAM_WD = 0.01
ADAM_STEP = 10

V7X_VMEM_BYTES = 64 * 1024 * 1024
VMEM_LIMIT = V7X_VMEM_BYTES - 8 * 1024 * 1024
MESH = pl.DeviceIdType.MESH
HIGHEST = lax.Precision.HIGHEST

_TM = (1088, 1024, 704, 512, 384, 256, 128)
_TN = (1408, 1024, 768, 512, 384, 256, 128)
_TK = (2176, 1408, 1024, 512, 384, 256, 128)
_TR = (544, 512, 384, 272, 256, 128)


def _pick(n, cands):
    for c in cands:
        if n % c == 0:
            return c
    return n


def _cparams(*sem):
    return pltpu.CompilerParams(dimension_semantics=sem if sem else None, vmem_limit_bytes=VMEM_LIMIT)


def _matmul(a, b, *, name, nt=False, b_chunked=False, out_chunked=False, out_dtype=F32,
            residual=None, scale=1.0, extras=(), epilogue=None, out_dtypes=None, comm=None):
    M, K = a.shape
    if not nt:
        N = b.shape[-1] * (NCH if b_chunked else 1)
        assert b.shape[-2] == K
        k_unit = K
    else:
        N = b.shape[-2]
        k_unit = b.shape[-1]
        assert k_unit * (NCH if b_chunked else 1) == K
    n_unit = N // NCH if (out_chunked or (b_chunked and not nt)) else N
    tm, tn, tk = _pick(M, _TM), _pick(n_unit, _TN), _pick(k_unit, _TK)
    if epilogue is None:
        extras = () if residual is None else (residual,)
        out_dtypes = (out_dtype,)

        def epilogue(acc, *res):
            r = acc * scale if scale != 1.0 else acc
            return (r + res[0] if res else r,)
    n_out = len(out_dtypes)
    n_temps = 4 if n_out > 1 else 0
    assert not (extras and out_chunked)

    def est(tm_):
        return (2 * tm_ * tk * 2 + 2 * tk * tn * 2 + tm_ * tn * 4
                + sum(2 * tm_ * tn * jnp.dtype(d).itemsize for d in out_dtypes)
                + sum(2 * tm_ * tn * e.dtype.itemsize for e in extras) + n_temps * tm_ * tn * 4)

    while est(tm) > VMEM_LIMIT * 3 // 4 and tm % 32 == 0:
        tm //= 2
    npc, kpc = n_unit // tn, k_unit // tk
    nk = K // tk
    grid = (M // tm, N // tn, nk)

    a_spec = pl.BlockSpec((tm, tk), lambda i, j, k: (i, k))
    if not nt:
        if b_chunked:
            b_spec = pl.BlockSpec((None, tk, tn), lambda i, j, k: (j // npc, k, j % npc))
        else:
            b_spec = pl.BlockSpec((tk, tn), lambda i, j, k: (k, j))
        dims = (((1,), (0,)), ((), ()))
    else:
        if b_chunked:
            b_spec = pl.BlockSpec((None, tn, tk), lambda i, j, k: (k // kpc, j, k % kpc))
        else:
            b_spec = pl.BlockSpec((tn, tk), lambda i, j, k: (j, k))
        dims = (((1,), (1,)), ((), ()))
    tile = pl.BlockSpec((tm, tn), lambda i, j, k: (i, j))
    if out_chunked:
        o_spec = pl.BlockSpec((None, tm, tn), lambda i, j, k: (j // npc, i, j % npc))
        out_shapes = [jax.ShapeDtypeStruct((NCH, M, n_unit), d) for d in out_dtypes]
    else:
        o_spec = tile
        out_shapes = [jax.ShapeDtypeStruct((M, N), d) for d in out_dtypes]
    in_specs = [a_spec, b_spec] + [tile] * len(extras)
    args = [a, b, *extras]

    n_main = len(args)
    n_cin = 0 if comm is None else len(comm.operands)
    n_cout = 0 if comm is None else len(comm.out_shapes)

    def body(*refs):
        a_ref, b_ref = refs[0], refs[1]
        e_refs = refs[2:n_main]
        o_refs = refs[n_main + n_cin:n_main + n_cin + n_out]
        acc_ref = refs[n_main + n_cin + n_out + n_cout]
        i, j, k = pl.program_id(0), pl.program_id(1), pl.program_id(2)
        if comm is not None:
            cin = refs[n_main:n_main + n_cin]
            cout = refs[n_main + n_cin + n_out:n_main + n_cin + n_out + n_cout]
            send_sems, recv_sems = refs[-2:]

            @pl.when((i == 0) & (j == 0) & (k == 0))
            def _():
                for cp, _ in comm.copies(cin, cout, send_sems, recv_sems):
                    cp.start()

        @pl.when(k == 0)
        def _():
            acc_ref[...] = jnp.zeros_like(acc_ref)

        acc_ref[...] += lax.dot_general(a_ref[...], b_ref[...], dims, preferred_element_type=F32)

        @pl.when(k == nk - 1)
        def _():
            for o_ref, val in zip(o_refs, epilogue(acc_ref[...], *[e[...] for e in e_refs])):
                o_ref[...] = val.astype(o_ref.dtype)

        if comm is not None:
            @pl.when((i == grid[0] - 1) & (j == grid[1] - 1) & (k == nk - 1))
            def _():
                cps = comm.copies(cin, cout, send_sems, recv_sems)
                for _, arrival in cps:
                    arrival.wait_recv()
                for cp, _ in cps:
                    cp.wait_send()

    scratch = [pltpu.VMEM((tm, tn), F32)]
    if comm is None:
        outs = pl.pallas_call(
            body, name=name, grid=grid, in_specs=in_specs, out_specs=[o_spec] * n_out, out_shape=out_shapes,
            scratch_shapes=scratch, compiler_params=_cparams("parallel", "parallel", "arbitrary"),
        )(*args)
    else:
        scratch += [pltpu.SemaphoreType.DMA((comm.n_sems,)), pltpu.SemaphoreType.DMA((comm.n_sems,))]
        outs = pl.pallas_call(
            body, name=name, grid=grid, in_specs=in_specs + [_ANY] * n_cin,
            out_specs=[o_spec] * n_out + [_ANY] * n_cout, out_shape=out_shapes + list(comm.out_shapes),
            input_output_aliases={n_main + s: n_out + d for s, d in comm.aliases.items()},
            scratch_shapes=scratch, compiler_params=_cparams("arbitrary", "arbitrary", "arbitrary"),
        )(*args, *comm.operands)
    return outs[0] if len(outs) == 1 else tuple(outs)


def _transpose(x, *, name):
    M, N = x.shape
    tc = _pick(N, (512, 384, 256, 128))

    def body(x_ref, o_ref):
        o_ref[...] = x_ref[...].astype(F32).T.astype(o_ref.dtype)

    return pl.pallas_call(
        body, name=name, grid=(N // tc,),
        in_specs=[pl.BlockSpec((M, tc), lambda j: (0, j))],
        out_specs=pl.BlockSpec((tc, M), lambda j: (j, 0)),
        out_shape=jax.ShapeDtypeStruct((N, M), x.dtype),
        compiler_params=_cparams("parallel"),
    )(x)


def _rms_fwd(h, g, *, name):
    L, D = h.shape
    tr = _pick(L, _TR)

    def body(h_ref, g_ref, o_ref):
        x = h_ref[...]
        r = lax.rsqrt(jnp.mean(x * x, axis=-1, keepdims=True) + EPS)
        o_ref[...] = (x * r * g_ref[...]).astype(o_ref.dtype)

    return pl.pallas_call(
        body, name=name, grid=(L // tr,),
        in_specs=[pl.BlockSpec((tr, D), lambda i: (i, 0)), pl.BlockSpec((1, D), lambda i: (0, 0))],
        out_specs=pl.BlockSpec((tr, D), lambda i: (i, 0)),
        out_shape=jax.ShapeDtypeStruct((L, D), BF16),
        compiler_params=_cparams("parallel"),
    )(h, g)


def _rms_bwd(dy, h, g, dh, *, name):
    L, D = h.shape
    tr = _pick(L, _TR)

    def body(dy_ref, h_ref, g_ref, dh_ref, o_ref, dg_ref):
        i = pl.program_id(0)
        x = h_ref[...]
        dyv = dy_ref[...]
        r = lax.rsqrt(jnp.mean(x * x, axis=-1, keepdims=True) + EPS)
        xh = x * r
        dxh = dyv * g_ref[...]
        dx = r * (dxh - xh * jnp.mean(dxh * xh, axis=-1, keepdims=True))
        o_ref[...] = dh_ref[...] + dx
        part = jnp.sum(dyv * xh, axis=0, keepdims=True)

        @pl.when(i == 0)
        def _():
            dg_ref[...] = part

        @pl.when(i > 0)
        def _():
            dg_ref[...] += part

    row = pl.BlockSpec((tr, D), lambda i: (i, 0))
    vec = pl.BlockSpec((1, D), lambda i: (0, 0))
    return pl.pallas_call(
        body, name=name, grid=(L // tr,),
        in_specs=[row, row, vec, row], out_specs=[row, vec],
        out_shape=[jax.ShapeDtypeStruct((L, D), F32), jax.ShapeDtypeStruct((1, D), F32)],
        compiler_params=_cparams("arbitrary"),
    )(dy, h, g, dh)


def _swiglu_epilogue(up, gate):
    g = gate.astype(F32)
    return up, g * jax.nn.sigmoid(g) * up


def _swiglu_bwd_epilogue(acc, gate, up):
    d = 0.5 * acc
    g = gate.astype(F32)
    sg = jax.nn.sigmoid(g)
    return d * up.astype(F32) * sg * (1.0 + g * (1.0 - sg)), d * g * sg


def _cast_bf16(x, *, name):
    L, D = x.shape
    tr = _pick(L, _TR)

    def body(x_ref, o_ref):
        o_ref[...] = x_ref[...].astype(o_ref.dtype)

    blk = pl.BlockSpec((tr, D), lambda i: (i, 0))
    return pl.pallas_call(
        body, name=name, grid=(L // tr,), in_specs=[blk], out_specs=blk,
        out_shape=jax.ShapeDtypeStruct((L, D), BF16), compiler_params=_cparams("parallel"),
    )(x)


def _loss_grad(h, target, *, name):
    L, D = h.shape
    S = target.shape[0]
    nb = L // BLOCK

    def body(h_ref, t_ref, loss_ref, dh_ref):
        i = pl.program_id(0)

        @pl.when(i == 0)
        def _():
            loss_ref[...] = jnp.zeros_like(loss_ref)
            dh_ref[...] = jnp.zeros_like(dh_ref)

        @pl.when(i > 0)
        def _():
            err = h_ref[...] - t_ref[...]
            dh_ref[...] = err * (1.0 / D)
            loss_ref[...] += jnp.full(loss_ref.shape, (0.5 / D) * jnp.sum(err * err), F32)

    return pl.pallas_call(
        body, name=name, grid=(nb,),
        in_specs=[pl.BlockSpec((BLOCK, D), lambda i: (i, 0)),
                  pl.BlockSpec((BLOCK, D), lambda i: (jnp.maximum(i - 1, 0), 0))],
        out_specs=[pl.BlockSpec((1, LANES), lambda i: (0, 0)), pl.BlockSpec((BLOCK, D), lambda i: (i, 0))],
        out_shape=[jax.ShapeDtypeStruct((1, LANES), F32), jax.ShapeDtypeStruct((L, D), F32)],
        compiler_params=_cparams("arbitrary"),
    )(h, target)


def _ffn_fwd(h, g, w, tag, queue=None):
    hn = _rms_fwd(h, g, name=f"{tag}_rms")
    gate = _mm(queue, hn, w(0), name=f"{tag}_gate", b_chunked=True, out_dtype=BF16)
    up, act = _mm(queue, hn, w(1), name=f"{tag}_up", b_chunked=True, extras=(gate,), epilogue=_swiglu_epilogue,
                  out_dtypes=(BF16, BF16))
    h_out = _mm(queue, act, w(2), name=f"{tag}_down", residual=h, scale=0.5)
    return h_out, (h, hn, gate, up, act)


def _ffn_bwd(dh, saved, g, wg, wu, wd, tag, queue=None):
    h, hn, gate, up, act = saved
    dout = _cast_bf16(dh, name=f"{tag}_dout")
    dgate, dup = _mm(queue, dout, wd, name=f"{tag}_dact", nt=True, extras=(gate, up), epilogue=_swiglu_bwd_epilogue,
                     out_dtypes=(BF16, BF16))
    actT = _transpose(act, name=f"{tag}_actT")
    dwd = _mm(queue, actT, dout, name=f"{tag}_dwd", out_dtype=BF16, scale=0.5)
    hnT = _transpose(hn, name=f"{tag}_hnT")
    dwg = _mm(queue, hnT, dgate, name=f"{tag}_dwg", out_chunked=True, out_dtype=BF16)
    dwu = _mm(queue, hnT, dup, name=f"{tag}_dwu", out_chunked=True, out_dtype=BF16)
    dhn = _mm(queue, dgate, wg, name=f"{tag}_dhn_g", nt=True, b_chunked=True)
    dhn = _mm(queue, dup, wu, name=f"{tag}_dhn_u", nt=True, b_chunked=True, residual=dhn)
    dh_in, dg = _rms_bwd(dhn, h, g, dh, name=f"{tag}_drms")
    return dh_in, dg, dwg, dwu, dwd


class _MixDims:
    def __init__(self, d_model):
        self.wf = d_model // 2
        self.ws = d_model // 2
        self.pf = self.wf // LANES
        self.ps = self.ws // LANES
        self.hf = self.wf // HEAD_DIM
        self.hq = self.ws // HEAD_DIM
        self.nkv = max(1, self.hq // SWA_GROUP)
        self.g = self.hq // self.nkv
        self.bq_f, self.bk_f, self.bv_f = 0, self.pf, 2 * self.pf
        self.bq_s = 3 * self.pf
        self.bk_s = self.bq_s + self.ps
        self.bv_s = self.bk_s + self.nkv
        self.bz = self.bv_s + self.nkv
        self.nu = (self.bz + 1) * LANES
        self.nup = -(-self.nu // 512) * 512
        self.in_width = 3 * self.wf + self.hf + self.ws + 2 * self.nkv * HEAD_DIM
        assert self.hf <= 2 * (LANES // 8)

    def gate_lane(self, h):
        return 8 * (h // 2) + h % 2

    def column_map(self):
        wf, ws, hd = self.wf, self.ws, HEAD_DIM
        src = np.full((self.nup,), -1, np.int64)
        src[0:3 * wf] = np.arange(3 * wf)
        o_sq = 3 * wf + self.hf
        src[self.bq_s * LANES:self.bq_s * LANES + ws] = o_sq + np.arange(ws)
        o_sk = o_sq + ws
        o_sv = o_sk + self.nkv * hd
        for kv in range(self.nkv):
            for rep in range(2):
                c0 = (self.bk_s + kv) * LANES + rep * hd
                src[c0:c0 + hd] = o_sk + kv * hd + np.arange(hd)
                c0 = (self.bv_s + kv) * LANES + rep * hd
                src[c0:c0 + hd] = o_sv + kv * hd + np.arange(hd)
        for h in range(self.hf):
            src[self.bz * LANES + self.gate_lane(h)] = 3 * wf + h
        return src

    def grad_column_map(self):
        src = self.column_map()
        dst = np.zeros((self.in_width,), np.int64)
        for col in range(self.nup - 1, -1, -1):
            if src[col] >= 0:
                dst[src[col]] = col
        return dst


def _block_diag_mean():
    m = np.zeros((LANES, LANES), np.float32)
    m[:HEAD_DIM, :HEAD_DIM] = 1.0 / HEAD_DIM
    m[HEAD_DIM:, HEAD_DIM:] = 1.0 / HEAD_DIM
    return jnp.asarray(m)


def _fold_halves():
    m = np.eye(LANES, dtype=np.float32)
    m[np.arange(LANES), (np.arange(LANES) + HEAD_DIM) % LANES] = 1.0
    return jnp.asarray(m)


def _gate_expand(md):
    e = np.zeros((LANES, md.wf), np.float32)
    for h in range(md.hf):
        e[md.gate_lane(h), h * HEAD_DIM:(h + 1) * HEAD_DIM] = 1.0
    return jnp.asarray(e)


def _qblocks(L, qb):
    blocks = [(0, BLOCK)]
    r = BLOCK
    while r < L:
        blocks.append((r, qb))
        r += qb
    assert r == L
    return blocks


def _f32dot(a, b):
    return jnp.dot(a, b, precision=HIGHEST, preferred_element_type=F32)


_DIMS_NT = (((1,), (1,)), ((), ()))


def _dot_nt(a, b):
    return lax.dot_general(a, b, _DIMS_NT, preferred_element_type=F32)


def _dot_tn(a, b):
    return jnp.dot(a.T.astype(BF16), b, preferred_element_type=F32)


def _log_sigmoid(z):
    return jnp.minimum(z, 0.0) - jnp.log(1.0 + jnp.exp(-jnp.abs(z)))


def _gate_fwd(u, b, md, *, name):
    L = u.shape[0]
    nb = L // BLOCK
    expand = _gate_expand(md)

    def body(z_ref, b_ref, e_ref, cexp_ref, ct_ref, c_s):
        ri = lax.broadcasted_iota(jnp.int32, (BLOCK, BLOCK), 0)
        ci = lax.broadcasted_iota(jnp.int32, (BLOCK, BLOCK), 1)
        tri = (ri >= ci).astype(F32)
        carry = jnp.zeros((1, LANES), F32)
        for bi in range(nb):
            rows = pl.ds(bi * BLOCK, BLOCK)
            logf = _log_sigmoid(z_ref[rows, :] + b_ref[...])
            blk = _f32dot(tri, logf) + carry
            c_s[rows, :] = blk
            carry = blk[BLOCK - 1:BLOCK, :]
        c = c_s[...]
        ct_ref[...] = c.T
        cexp_ref[...] = _f32dot(c, e_ref[...])

    return pl.pallas_call(
        body, name=name, grid=(1,),
        in_specs=[pl.BlockSpec((L, LANES), lambda i: (0, md.bz)), pl.BlockSpec((1, LANES), lambda i: (0, 0)),
                  pl.BlockSpec((LANES, md.wf), lambda i: (0, 0))],
        out_specs=[pl.BlockSpec((L, md.wf), lambda i: (0, 0)), pl.BlockSpec((LANES, L), lambda i: (0, 0))],
        out_shape=[jax.ShapeDtypeStruct((L, md.wf), F32), jax.ShapeDtypeStruct((LANES, L), F32)],
        scratch_shapes=[pltpu.VMEM((L, LANES), F32)],
        compiler_params=_cparams("arbitrary"),
    )(u, b, expand)


def _gate_bwd(u, b, dck_t, md, *, name):
    L = u.shape[0]
    nb = L // BLOCK

    def body(z_ref, b_ref, dck_ref, dz_ref, db_ref, dc_s):
        ri = lax.broadcasted_iota(jnp.int32, (BLOCK, BLOCK), 0)
        ci = lax.broadcasted_iota(jnp.int32, (BLOCK, BLOCK), 1)
        triu = (ri <= ci).astype(F32)
        dc_s[...] = -dck_ref[...].T
        carry = jnp.zeros((1, LANES), F32)
        db = jnp.zeros((1, LANES), F32)
        for bi in range(nb - 1, -1, -1):
            rows = pl.ds(bi * BLOCK, BLOCK)
            blk = _f32dot(triu, dc_s[rows, :]) + carry
            carry = blk[0:1, :]
            z = z_ref[rows, :] + b_ref[...]
            dz = blk * jax.nn.sigmoid(-z)
            if bi == 0:
                dz = jnp.where(lax.broadcasted_iota(jnp.int32, (BLOCK, LANES), 0) >= PAD, dz, 0.0)
            dz_ref[rows, :] = dz
            db = db + jnp.sum(dz, axis=0, keepdims=True)
        db_ref[...] = db

    return pl.pallas_call(
        body, name=name, grid=(1,),
        in_specs=[pl.BlockSpec((L, LANES), lambda i: (0, md.bz)), pl.BlockSpec((1, LANES), lambda i: (0, 0)),
                  pl.BlockSpec((LANES, L), lambda i: (0, 0))],
        out_specs=[pl.BlockSpec((L, LANES), lambda i: (0, 0)), pl.BlockSpec((1, LANES), lambda i: (0, 0))],
        out_shape=[jax.ShapeDtypeStruct((L, LANES), F32), jax.ShapeDtypeStruct((1, LANES), F32)],
        scratch_shapes=[pltpu.VMEM((L, LANES), F32)],
        compiler_params=_cparams("arbitrary"),
    )(u, b, dck_t)


def _head_norm(x, g, bd):
    r = lax.rsqrt(_f32dot(x * x, bd) + EPS)
    xh = x * r
    return xh * g, xh, r


def _head_norm_bwd(dy, xh, r, g, bd):
    dxh = dy * g
    dx = r * (dxh - xh * _f32dot(dxh * xh, bd))
    return dx, jnp.sum(dy * xh, axis=0, keepdims=True)


def _lane_half():
    return lax.broadcasted_iota(jnp.int32, (1, LANES), 1) < HEAD_DIM


def _store_head_pair(x, half, a_s, b_s):
    a_s[...] = jnp.where(half, x, 0.0).astype(BF16)
    b_s[...] = jnp.where(half, 0.0, x).astype(BF16)


def _fox_scores(qm, kn_s, cexp_ref, ct_ref, r0, nr, klen, hh):
    s = _dot_nt(qm, kn_s[0:klen, :]) * (HEAD_DIM ** -0.5)
    s = s + cexp_ref[r0:r0 + nr, HEAD_DIM * hh:HEAD_DIM * hh + 1] - ct_ref[hh:hh + 1, 0:klen]
    qp = r0 + lax.broadcasted_iota(jnp.int32, (nr, klen), 0)
    kp = lax.broadcasted_iota(jnp.int32, (nr, klen), 1)
    return jnp.where((kp <= qp) & (kp >= PAD), s, NEG_INF)


def _fox_fwd(u, cexp, ct3, qg, kg, md, *, name):
    L = u.shape[0]
    blocks = _qblocks(L, QBLOCK)
    bd = _block_diag_mean()

    def body(q_ref, k_ref, v_ref, cexp_ref, ct_ref, qg_ref, kg_ref, bd_ref, o_ref, lse_ref, qa_s, qb_s, kn_s, v_s):
        half = _lane_half()
        bdv = bd_ref[...]
        _store_head_pair(_head_norm(q_ref[...], qg_ref[...], bdv)[0], half, qa_s, qb_s)
        kn_s[...] = _head_norm(k_ref[...], kg_ref[...], bdv)[0].astype(BF16)
        v_s[...] = v_ref[...].astype(BF16)
        for r0, nr in blocks:
            klen = r0 + nr
            o_blk = lse_blk = None
            for hh, q_s in enumerate((qa_s, qb_s)):
                s = _fox_scores(q_s[r0:r0 + nr, :], kn_s, cexp_ref, ct_ref, r0, nr, klen, hh)
                m = jnp.max(s, axis=-1, keepdims=True)
                p = jnp.exp(s - m)
                l = jnp.sum(p, axis=-1, keepdims=True)
                oh = jnp.dot(p.astype(BF16), v_s[0:klen, :], preferred_element_type=F32) * (1.0 / l)
                lh = jnp.broadcast_to(jnp.where(m > 0.5 * NEG_INF, m + jnp.log(l), 0.0), (nr, LANES))
                o_blk = oh if hh == 0 else jnp.where(half, o_blk, oh)
                lse_blk = lh if hh == 0 else jnp.where(half, lse_blk, lh)
            o_ref[r0:r0 + nr, :] = o_blk
            lse_ref[r0:r0 + nr, :] = lse_blk

    col = lambda base: pl.BlockSpec((L, LANES), lambda j: (0, base + j))
    vec = pl.BlockSpec((1, LANES), lambda j: (0, 0))
    return pl.pallas_call(
        body, name=name, grid=(md.pf,),
        in_specs=[col(md.bq_f), col(md.bk_f), col(md.bv_f), col(0),
                  pl.BlockSpec((None, 8, L), lambda j: (j, 0, 0)), vec, vec,
                  pl.BlockSpec((LANES, LANES), lambda j: (0, 0))],
        out_specs=[col(0), col(0)],
        out_shape=[jax.ShapeDtypeStruct((L, md.wf), F32)] * 2,
        scratch_shapes=[pltpu.VMEM((L, LANES), BF16)] * 4,
        compiler_params=_cparams("parallel"),
    )(u, u, u, cexp, ct3, qg, kg, bd)


def _softmax_bwd(p, dp):
    pdp = p * dp
    return pdp - p * jnp.sum(pdp, axis=-1, keepdims=True)


def _fox_bwd(u, cexp, ct3, qg, kg, do, lse, md, *, name):
    L = u.shape[0]
    blocks = _qblocks(L, QBLOCK // 2)
    bd = _block_diag_mean()
    fold = _fold_halves()
    npairs = md.pf

    def body(q_ref, k_ref, v_ref, cexp_ref, ct_ref, qg_ref, kg_ref, bd_ref, fold_ref, do_ref, lse_ref,
             dq_ref, dk_ref, dv_ref, dck_ref, dqg_ref, dkg_ref,
             qa_s, qb_s, kn_s, v_s, doa_s, dob_s, dqn_s, dkn_s, dvv_s):
        j = pl.program_id(0)
        half = _lane_half()
        bdv = bd_ref[...]
        qy, qh, rq = _head_norm(q_ref[...], qg_ref[...], bdv)
        ky, kh, rk = _head_norm(k_ref[...], kg_ref[...], bdv)
        _store_head_pair(qy, half, qa_s, qb_s)
        _store_head_pair(do_ref[...], half, doa_s, dob_s)
        kn_s[...] = ky.astype(BF16)
        v_s[...] = v_ref[...].astype(BF16)
        dkn_s[...] = jnp.zeros_like(dkn_s)
        dvv_s[...] = jnp.zeros_like(dvv_s)
        dck_ref[...] = jnp.zeros_like(dck_ref)
        for r0, nr in blocks:
            klen = r0 + nr
            dq_blk = None
            for hh, (q_s, do_s) in enumerate(((qa_s, doa_s), (qb_s, dob_s))):
                c0 = HEAD_DIM * hh
                qm = q_s[r0:r0 + nr, :]
                dom = do_s[r0:r0 + nr, :]
                s = _fox_scores(qm, kn_s, cexp_ref, ct_ref, r0, nr, klen, hh)
                p = jnp.exp(s - lse_ref[r0:r0 + nr, c0:c0 + 1])
                ds = _softmax_bwd(p, _dot_nt(dom, v_s[0:klen, :]))
                dck_ref[hh:hh + 1, 0:klen] += jnp.sum(ds, axis=0, keepdims=True)
                ds = ds * (HEAD_DIM ** -0.5)
                dq_h = jnp.dot(ds.astype(BF16), kn_s[0:klen, :], preferred_element_type=F32)
                dkn_s[0:klen, :] += _dot_tn(ds, qm)
                dvv_s[0:klen, :] += _dot_tn(p, dom)
                dq_blk = dq_h if hh == 0 else jnp.where(half, dq_blk, dq_h)
            dqn_s[r0:r0 + nr, :] = dq_blk
        dq, dqg = _head_norm_bwd(dqn_s[...], qh, rq, qg_ref[...], bdv)
        dk, dkg = _head_norm_bwd(dkn_s[...], kh, rk, kg_ref[...], bdv)
        dq_ref[...] = dq
        dk_ref[...] = dk
        dv_ref[...] = dvv_s[...]

        @pl.when(j == 0)
        def _():
            dqg_ref[...] = jnp.zeros_like(dqg_ref)
            dkg_ref[...] = jnp.zeros_like(dkg_ref)

        dqg_ref[...] += dqg
        dkg_ref[...] += dkg

        @pl.when(j == npairs - 1)
        def _():
            dqg_ref[...] = _f32dot(jnp.broadcast_to(dqg_ref[...], (8, LANES)), fold_ref[...])[0:1, :]
            dkg_ref[...] = _f32dot(jnp.broadcast_to(dkg_ref[...], (8, LANES)), fold_ref[...])[0:1, :]

    col = lambda base: pl.BlockSpec((L, LANES), lambda j: (0, base + j))
    vec = pl.BlockSpec((1, LANES), lambda j: (0, 0))
    sq = pl.BlockSpec((LANES, LANES), lambda j: (0, 0))
    ct_spec = pl.BlockSpec((None, 8, L), lambda j: (j, 0, 0))
    big = jax.ShapeDtypeStruct((L, md.wf), F32)
    small = jax.ShapeDtypeStruct((1, LANES), F32)
    return pl.pallas_call(
        body, name=name, grid=(md.pf,),
        in_specs=[col(md.bq_f), col(md.bk_f), col(md.bv_f), col(0), ct_spec, vec, vec, sq, sq, col(0), col(0)],
        out_specs=[col(0), col(0), col(0), ct_spec, vec, vec],
        out_shape=[big, big, big, jax.ShapeDtypeStruct((md.pf, 8, L), F32), small, small],
        scratch_shapes=[pltpu.VMEM((L, LANES), BF16)] * 6 + [pltpu.VMEM((L, LANES), F32)] * 3,
        compiler_params=_cparams("arbitrary"),
    )(u, u, u, cexp, ct3, qg, kg, bd, fold, do, lse)


def _swa_scores(qm, kn_s, slope, k0, r0, nr, klen):
    s = _dot_nt(qm, kn_s[k0:k0 + klen, :]) * (HEAD_DIM ** -0.5)
    qp = r0 + lax.broadcasted_iota(jnp.int32, (nr, klen), 0)
    kp = k0 + lax.broadcasted_iota(jnp.int32, (nr, klen), 1)
    dist = qp - kp
    s = s - slope * dist.astype(F32)
    return jnp.where((dist >= 0) & (dist < WINDOW) & (kp >= PAD), s, NEG_INF)


def _swa_fwd(u, sinkp, slopep, qg, kg, md, *, name):
    L = u.shape[0]
    blocks = _qblocks(L, QBLOCK)
    bd = _block_diag_mean()

    def body(q_ref, k_ref, v_ref, sink_ref, slope_ref, qg_ref, kg_ref, bd_ref, o_ref, lse_ref, qa_s, qb_s, kn_s, v_s):
        half = _lane_half()
        bdv = bd_ref[...]
        _store_head_pair(_head_norm(q_ref[...], qg_ref[...], bdv)[0], half, qa_s, qb_s)
        kn_s[...] = _head_norm(k_ref[...], kg_ref[...], bdv)[0].astype(BF16)
        v_s[...] = v_ref[...].astype(BF16)
        for r0, nr in blocks:
            k0 = max(r0 - BLOCK, 0)
            klen = r0 + nr - k0
            o_blk = lse_blk = None
            for hh, q_s in enumerate((qa_s, qb_s)):
                c0 = HEAD_DIM * hh
                s = _swa_scores(q_s[r0:r0 + nr, :], kn_s, slope_ref[0:1, c0:c0 + 1], k0, r0, nr, klen)
                sink = sink_ref[0:1, c0:c0 + 1]
                m = jnp.maximum(jnp.max(s, axis=-1, keepdims=True), sink)
                p = jnp.exp(s - m)
                den = jnp.sum(p, axis=-1, keepdims=True) + jnp.exp(sink - m)
                oh = jnp.dot(p.astype(BF16), v_s[k0:k0 + klen, :], preferred_element_type=F32) * (1.0 / den)
                lh = jnp.broadcast_to(m + jnp.log(den), (nr, LANES))
                o_blk = oh if hh == 0 else jnp.where(half, o_blk, oh)
                lse_blk = lh if hh == 0 else jnp.where(half, lse_blk, lh)
            o_ref[r0:r0 + nr, :] = o_blk
            lse_ref[r0:r0 + nr, :] = lse_blk

    g2 = md.g // 2
    qcol = pl.BlockSpec((L, LANES), lambda j: (0, md.bq_s + j))
    kcol = pl.BlockSpec((L, LANES), lambda j: (0, md.bk_s + j // g2))
    vcol = pl.BlockSpec((L, LANES), lambda j: (0, md.bv_s + j // g2))
    ocol = pl.BlockSpec((L, LANES), lambda j: (0, j))
    pvec = pl.BlockSpec((1, LANES), lambda j: (0, j))
    vec = pl.BlockSpec((1, LANES), lambda j: (0, 0))
    return pl.pallas_call(
        body, name=name, grid=(md.ps,),
        in_specs=[qcol, kcol, vcol, pvec, pvec, vec, vec, pl.BlockSpec((LANES, LANES), lambda j: (0, 0))],
        out_specs=[ocol, ocol],
        out_shape=[jax.ShapeDtypeStruct((L, md.ws), F32)] * 2,
        scratch_shapes=[pltpu.VMEM((L, LANES), BF16)] * 4,
        compiler_params=_cparams("parallel"),
    )(u, u, u, sinkp, slopep, qg, kg, bd)


def _swa_bwd(u, sinkp, slopep, qg, kg, do, lse, md, *, name):
    L = u.shape[0]
    blocks = _qblocks(L, QBLOCK // 2)
    bd = _block_diag_mean()
    fold = _fold_halves()
    g2 = md.g // 2
    npairs = md.ps

    def body(q_ref, k_ref, v_ref, sink_ref, slope_ref, qg_ref, kg_ref, bd_ref, fold_ref, do_ref, lse_ref,
             dq_ref, dk_ref, dv_ref, dsink_ref, dqg_ref, dkg_ref,
             qa_s, qb_s, kn_s, v_s, doa_s, dob_s, dqn_s):
        j = pl.program_id(0)
        half = _lane_half()
        bdv = bd_ref[...]
        qy, qh, rq = _head_norm(q_ref[...], qg_ref[...], bdv)
        ky, kh, rk = _head_norm(k_ref[...], kg_ref[...], bdv)
        _store_head_pair(qy, half, qa_s, qb_s)
        _store_head_pair(do_ref[...], half, doa_s, dob_s)
        kn_s[...] = ky.astype(BF16)
        v_s[...] = v_ref[...].astype(BF16)

        @pl.when(j % g2 == 0)
        def _():
            dk_ref[...] = jnp.zeros_like(dk_ref)
            dv_ref[...] = jnp.zeros_like(dv_ref)

        @pl.when(j == 0)
        def _():
            dqg_ref[...] = jnp.zeros_like(dqg_ref)
            dkg_ref[...] = jnp.zeros_like(dkg_ref)

        dsink = [jnp.zeros((1, 1), F32), jnp.zeros((1, 1), F32)]
        for r0, nr in blocks:
            k0 = max(r0 - BLOCK, 0)
            klen = r0 + nr - k0
            dq_blk = None
            for hh, (q_s, do_s) in enumerate(((qa_s, doa_s), (qb_s, dob_s))):
                c0 = HEAD_DIM * hh
                qm = q_s[r0:r0 + nr, :]
                dom = do_s[r0:r0 + nr, :]
                s = _swa_scores(qm, kn_s, slope_ref[0:1, c0:c0 + 1], k0, r0, nr, klen)
                lse_h = lse_ref[r0:r0 + nr, c0:c0 + 1]
                p = jnp.exp(s - lse_h)
                pdp = p * _dot_nt(dom, v_s[k0:k0 + klen, :])
                delta = jnp.sum(pdp, axis=-1, keepdims=True)
                p_sink = jnp.exp(sink_ref[0:1, c0:c0 + 1] - lse_h)
                dsink[hh] = dsink[hh] - jnp.sum(p_sink * delta, axis=0, keepdims=True)
                ds = (pdp - p * delta) * (HEAD_DIM ** -0.5)
                dq_h = jnp.dot(ds.astype(BF16), kn_s[k0:k0 + klen, :], preferred_element_type=F32)
                dk_ref[k0:k0 + klen, :] += _dot_tn(ds, qm)
                dv_ref[k0:k0 + klen, :] += _dot_tn(p, dom)
                dq_blk = dq_h if hh == 0 else jnp.where(half, dq_blk, dq_h)
            dqn_s[r0:r0 + nr, :] = dq_blk
        dq, dqg = _head_norm_bwd(dqn_s[...], qh, rq, qg_ref[...], bdv)
        dq_ref[...] = dq
        dqg_ref[...] += dqg
        dsink_ref[...] = jnp.where(half, jnp.broadcast_to(dsink[0], (1, LANES)), jnp.broadcast_to(dsink[1], (1, LANES)))

        @pl.when(j % g2 == g2 - 1)
        def _():
            dkn = _f32dot(dk_ref[...], fold_ref[...])
            dk, dkg = _head_norm_bwd(dkn, kh, rk, kg_ref[...], bdv)
            dk_ref[...] = jnp.where(half, dk, 0.0)
            dv_ref[...] = jnp.where(half, _f32dot(dv_ref[...], fold_ref[...]), 0.0)
            dkg_ref[...] += dkg

        @pl.when(j == npairs - 1)
        def _():
            dqg_ref[...] = _f32dot(jnp.broadcast_to(dqg_ref[...], (8, LANES)), fold_ref[...])[0:1, :]

    qcol = pl.BlockSpec((L, LANES), lambda j: (0, md.bq_s + j))
    kcol = pl.BlockSpec((L, LANES), lambda j: (0, md.bk_s + j // g2))
    vcol = pl.BlockSpec((L, LANES), lambda j: (0, md.bv_s + j // g2))
    ocol = pl.BlockSpec((L, LANES), lambda j: (0, j))
    kvout = pl.BlockSpec((L, LANES), lambda j: (0, j // g2))
    pvec = pl.BlockSpec((1, LANES), lambda j: (0, j))
    vec = pl.BlockSpec((1, LANES), lambda j: (0, 0))
    sq = pl.BlockSpec((LANES, LANES), lambda j: (0, 0))
    kvshape = jax.ShapeDtypeStruct((L, LANES * md.nkv), F32)
    small = jax.ShapeDtypeStruct((1, LANES), F32)
    return pl.pallas_call(
        body, name=name, grid=(md.ps,),
        in_specs=[qcol, kcol, vcol, pvec, pvec, vec, vec, sq, sq, ocol, ocol],
        out_specs=[ocol, kvout, kvout, pvec, vec, vec],
        out_shape=[jax.ShapeDtypeStruct((L, md.ws), F32), kvshape, kvshape,
                   jax.ShapeDtypeStruct((1, md.ws), F32), small, small],
        scratch_shapes=[pltpu.VMEM((L, LANES), BF16)] * 6 + [pltpu.VMEM((L, LANES), F32)],
        compiler_params=_cparams("arbitrary"),
    )(u, u, u, sinkp, slopep, qg, kg, bd, fold, do, lse)


def _outnorm_fwd(of, os_, gf, gs, *, name):
    L, wf = of.shape
    ws = os_.shape[1]
    tr = _pick(L, _TR)

    def body(of_ref, os_ref, gf_ref, gs_ref, o_ref):
        for src, g_ref, c0, w in ((of_ref, gf_ref, 0, wf), (os_ref, gs_ref, wf, ws)):
            x = src[...]
            r = lax.rsqrt(jnp.mean(x * x, axis=-1, keepdims=True) + EPS)
            o_ref[:, c0:c0 + w] = (x * r * g_ref[...]).astype(o_ref.dtype)

    return pl.pallas_call(
        body, name=name, grid=(L // tr,),
        in_specs=[pl.BlockSpec((tr, wf), lambda i: (i, 0)), pl.BlockSpec((tr, ws), lambda i: (i, 0)),
                  pl.BlockSpec((1, wf), lambda i: (0, 0)), pl.BlockSpec((1, ws), lambda i: (0, 0))],
        out_specs=pl.BlockSpec((tr, wf + ws), lambda i: (i, 0)),
        out_shape=jax.ShapeDtypeStruct((L, wf + ws), BF16),
        compiler_params=_cparams("parallel"),
    )(of, os_, gf, gs)


def _outnorm_bwd(don, of, os_, gf, gs, *, name):
    L, wf = of.shape
    ws = os_.shape[1]
    tr = _pick(L, _TR)

    def body(d_ref, of_ref, os_ref, gf_ref, gs_ref, dof_ref, dos_ref, dgf_ref, dgs_ref):
        i = pl.program_id(0)
        for src, g_ref, c0, w, dx_ref, dg_ref in ((of_ref, gf_ref, 0, wf, dof_ref, dgf_ref),
                                                  (os_ref, gs_ref, wf, ws, dos_ref, dgs_ref)):
            x = src[...]
            dy = d_ref[:, c0:c0 + w]
            r = lax.rsqrt(jnp.mean(x * x, axis=-1, keepdims=True) + EPS)
            xh = x * r
            dxh = dy * g_ref[...]
            dx_ref[...] = r * (dxh - xh * jnp.mean(dxh * xh, axis=-1, keepdims=True))
            part = jnp.sum(dy * xh, axis=0, keepdims=True)

            @pl.when(i == 0)
            def _():
                dg_ref[...] = part

            @pl.when(i > 0)
            def _():
                dg_ref[...] += part

    rf = pl.BlockSpec((tr, wf), lambda i: (i, 0))
    rs = pl.BlockSpec((tr, ws), lambda i: (i, 0))
    vf = pl.BlockSpec((1, wf), lambda i: (0, 0))
    vs = pl.BlockSpec((1, ws), lambda i: (0, 0))
    return pl.pallas_call(
        body, name=name, grid=(L // tr,),
        in_specs=[pl.BlockSpec((tr, wf + ws), lambda i: (i, 0)), rf, rs, vf, vs],
        out_specs=[rf, rs, vf, vs],
        out_shape=[jax.ShapeDtypeStruct((L, wf), F32), jax.ShapeDtypeStruct((L, ws), F32),
                   jax.ShapeDtypeStruct((1, wf), F32), jax.ShapeDtypeStruct((1, ws), F32)],
        compiler_params=_cparams("arbitrary"),
    )(don, of, os_, gf, gs)


def _win_to_mine(w, md):
    d = w.shape[0]
    wf, ws, hd = md.wf, md.ws, HEAD_DIM
    o_z = 3 * wf
    o_sq = o_z + md.hf
    o_sk = o_sq + ws
    o_sv = o_sk + md.nkv * hd
    parts = [w[:, :3 * wf], w[:, o_sq:o_sq + ws]]
    for base in (o_sk, o_sv):
        for kv in range(md.nkv):
            blk = w[:, base + kv * hd:base + (kv + 1) * hd]
            parts += [blk, blk]
    z = w[:, o_z:o_z + md.hf].reshape(d, md.hf // 2, 2)
    z = jnp.pad(z, ((0, 0), (0, 0), (0, 6))).reshape(d, 4 * md.hf)
    parts.append(jnp.pad(z, ((0, 0), (0, LANES - 4 * md.hf + md.nup - md.nu))))
    return jnp.concatenate(parts, axis=1)


def _win_grad_to_ref(dw, md):
    d = dw.shape[0]
    wf, ws, hd = md.wf, md.ws, HEAD_DIM
    z = dw[:, md.bz * LANES:md.bz * LANES + 4 * md.hf].reshape(d, md.hf // 2, 8)[:, :, :2].reshape(d, md.hf)
    parts = [dw[:, :3 * wf], z, dw[:, md.bq_s * LANES:md.bq_s * LANES + ws]]
    for base in (md.bk_s, md.bv_s):
        for kv in range(md.nkv):
            c0 = (base + kv) * LANES
            parts.append(dw[:, c0:c0 + hd])
    return jnp.concatenate(parts, axis=1)


def _mix_small(p, md):
    tile2 = lambda v: jnp.tile(v.reshape(1, HEAD_DIM), (1, 2))
    b = p["b_forget"].reshape(md.hf // 2, 2)
    b = jnp.pad(b, ((0, 0), (0, 6))).reshape(1, 4 * md.hf)
    slopes = np.asarray(2.0 ** (-8.0 * np.arange(1, md.hq + 1) / md.hq), np.float32)
    return dict(
        g_mix=p["mix_norm"].reshape(1, -1),
        b_gate=jnp.pad(b, ((0, 0), (0, LANES - 4 * md.hf))),
        fqg=tile2(p["fox_q_norm"]), fkg=tile2(p["fox_k_norm"]),
        sqg=tile2(p["swa_q_norm"]), skg=tile2(p["swa_k_norm"]),
        sinkp=jnp.repeat(p["swa_sinks"], HEAD_DIM).reshape(1, md.ws),
        slopep=jnp.asarray(np.repeat(slopes, HEAD_DIM).reshape(1, md.ws)),
        gfo=p["fox_out_norm"].reshape(1, md.wf), gso=p["swa_out_norm"].reshape(1, md.ws),
    )


def _mix_fwd(h, sp, w, md, queue=None):
    L = h.shape[0]
    hn = _rms_fwd(h, sp["g_mix"], name="mix_rms")
    u = _mm(queue, hn, w(0), name="mix_u")
    cexp, ct = _gate_fwd(u, sp["b_gate"], md, name="gate_fwd")
    ct3 = ct[:8 * md.pf].reshape(md.pf, 8, L)
    of, lsef = _fox_fwd(u, cexp, ct3, sp["fqg"], sp["fkg"], md, name="fox_fwd")
    os_, lses = _swa_fwd(u, sp["sinkp"], sp["slopep"], sp["sqg"], sp["skg"], md, name="swa_fwd")
    on = _outnorm_fwd(of, os_, sp["gfo"], sp["gso"], name="outnorm_fwd")
    h_out = _mm(queue, on, w(1), name="mix_out", residual=h)
    return h_out, (h, hn, u, cexp, ct3, of, lsef, os_, lses, on)


def _mix_bwd(dh, saved, sp, w_in, w_out, md, queue=None):
    h, hn, u, cexp, ct3, of, lsef, os_, lses, on = saved
    L = h.shape[0]
    dhb = _cast_bf16(dh, name="mix_dhb")
    don = _mm(queue, dhb, w_out, name="mix_don", nt=True)
    dw_out = _mm(queue, _transpose(on, name="mix_onT"), dhb, name="mix_dwout", out_dtype=BF16)
    dof, dos, dgfo, dgso = _outnorm_bwd(don, of, os_, sp["gfo"], sp["gso"], name="outnorm_bwd")
    duq, duk, duv, dck, dfqg, dfkg = _fox_bwd(u, cexp, ct3, sp["fqg"], sp["fkg"], dof, lsef, md, name="fox_bwd")
    dsq, dsk, dsv, dsinkp, dsqg, dskg = _swa_bwd(u, sp["sinkp"], sp["slopep"], sp["sqg"], sp["skg"], dos, lses, md, name="swa_bwd")
    dck_t = jnp.pad(dck.reshape(8 * md.pf, L), ((0, LANES - 8 * md.pf), (0, 0)))
    dz, db = _gate_bwd(u, sp["b_gate"], dck_t, md, name="gate_bwd")
    du = jnp.concatenate([duq, duk, duv, dsq, dsk, dsv, dz, jnp.zeros((L, md.nup - md.nu), F32)], axis=1).astype(BF16)
    dhn = _mm(queue, du, w_in, name="mix_dhn", nt=True)
    dw_in = _mm(queue, _transpose(hn, name="mix_hnT"), du, name="mix_dwin", out_dtype=BF16)
    dh_in, dg_mix = _rms_bwd(dhn, h, sp["g_mix"], dh, name="mix_drms")
    small = dict(
        mix_norm=dg_mix.reshape(-1),
        b_forget=db[0, :4 * md.hf].reshape(md.hf // 2, 8)[:, :2].reshape(md.hf),
        fox_q_norm=dfqg[0, :HEAD_DIM], fox_k_norm=dfkg[0, :HEAD_DIM],
        swa_q_norm=dsqg[0, :HEAD_DIM], swa_k_norm=dskg[0, :HEAD_DIM],
        swa_sinks=dsinkp[0, ::HEAD_DIM],
        fox_out_norm=dgfo.reshape(-1), swa_out_norm=dgso.reshape(-1),
    )
    return dh_in, dw_in, dw_out, small


_ANY = pl.BlockSpec(memory_space=pl.ANY)
_HALF_ROWS = (512, 352, 256, 192, 128, 64, 32, 16)


def _mesh_pos():
    return lax.axis_index("x"), lax.axis_index("y"), lax.axis_index("c")


def _other_chips(x, y):
    return [(1 - x, y), (x, 1 - y), (1 - x, 1 - y)]


def _rows_half(ref, which):
    rh = ref.shape[-2] // 2
    if len(ref.shape) == 2:
        return ref.at[pl.ds(which * rh, rh), :]
    return ref.at[:, pl.ds(which * rh, rh), :]


def _remote(src, dst, send_sems, recv_sems, idx, dev):
    return pltpu.make_async_remote_copy(src_ref=src, dst_ref=dst, send_sem=send_sems.at[idx], recv_sem=recv_sems.at[idx],
                                        device_id=dev, device_id_type=MESH)


def _cast_into_chunk(w, l_idx, k_idx, *, name):
    _, rows, cols = w.shape
    tr = _pick(rows, (512, 352, 256, 128, 64, 32, 16))

    def body(l_ref, k_ref, w_ref, o_ref):
        o_ref[...] = w_ref[...].astype(o_ref.dtype)

    return pl.pallas_call(
        body, name=name,
        grid_spec=pltpu.PrefetchScalarGridSpec(
            num_scalar_prefetch=2, grid=(rows // tr,),
            in_specs=[pl.BlockSpec((None, tr, cols), lambda i, l, k: (l[0], i, 0))],
            out_specs=pl.BlockSpec((None, tr, cols), lambda i, l, k: (k[0], i, 0))),
        out_shape=jax.ShapeDtypeStruct((NCH, rows, cols), BF16),
        compiler_params=_cparams("parallel"),
    )(l_idx, k_idx, w)


def _allgather_chips(arrs, *, name):
    n = len(arrs)

    def body(*refs):
        outs = refs[n:2 * n]
        send_sems, recv_sems = refs[2 * n:]
        x, y, c = _mesh_pos()
        k = 2 * x + y
        sibling = (x, y, 1 - c)
        chips = _other_chips(x, y)
        sent = []
        for i in range(n):
            mine = _rows_half(outs[i].at[k], c)
            for j, (cx, cy) in enumerate(chips):
                cp = _remote(mine, mine, send_sems, recv_sems, 6 * i + j, (cx, cy, c))
                cp.start()
                sent.append(cp)
        for i in range(n):
            for j, (cx, cy) in enumerate(chips):
                blk = _rows_half(outs[i].at[2 * cx + cy], c)
                _remote(blk, blk, send_sems, recv_sems, 6 * i + j, (cx, cy, c)).wait_recv()
                fwd = _remote(blk, blk, send_sems, recv_sems, 6 * i + 3 + j, sibling)
                fwd.start()
                sent.append(fwd)
        for i in range(n):
            for j, (cx, cy) in enumerate(chips):
                blk = _rows_half(outs[i].at[2 * cx + cy], 1 - c)
                _remote(blk, blk, send_sems, recv_sems, 6 * i + 3 + j, sibling).wait_recv()
        for cp in sent:
            cp.wait_send()

    return pl.pallas_call(
        body, name=name, in_specs=[_ANY] * n, out_specs=[_ANY] * n,
        out_shape=[jax.ShapeDtypeStruct(a.shape, a.dtype) for a in arrs],
        input_output_aliases={i: i for i in range(n)},
        scratch_shapes=[pltpu.SemaphoreType.DMA((6 * n,)), pltpu.SemaphoreType.DMA((6 * n,))],
    )(*arrs)


def _exchange_sibling_halves(arrs, *, name):
    n = len(arrs)

    def body(*refs):
        srcs, lands = refs[:n], refs[n:2 * n]
        send_sems, recv_sems = refs[2 * n:]
        x, y, c = _mesh_pos()
        sibling = (x, y, 1 - c)
        cps = [_remote(_rows_half(srcs[i], 1 - c), lands[i], send_sems, recv_sems, i, sibling) for i in range(n)]
        for cp in cps:
            cp.start()
        for cp in cps:
            cp.wait_recv()
        for cp in cps:
            cp.wait_send()

    return pl.pallas_call(
        body, name=name, in_specs=[_ANY] * n, out_specs=[_ANY] * n,
        out_shape=[jax.ShapeDtypeStruct((NCH, a.shape[1] // 2, a.shape[2]), a.dtype) for a in arrs],
        scratch_shapes=[pltpu.SemaphoreType.DMA((n,)), pltpu.SemaphoreType.DMA((n,))],
    )(*arrs)


class _ChipExchange:
    n_sems = NCH - 1

    def __init__(self, s, r0, r1, land=None):
        self.r0, self.r1 = r0, r1
        self.operands = [s] if land is None else [s, land]
        self.out_shapes = [jax.ShapeDtypeStruct((NCH - 1,) + s.shape[1:], s.dtype)]
        self.aliases = {} if land is None else {1: 0}

    def copies(self, cin, cout, send_sems, recv_sems):
        x, y, c = _mesh_pos()
        rows = pl.ds(self.r0, self.r1 - self.r0)
        cps = [_remote(cin[0].at[2 * cx + cy, rows, :], cout[0].at[j, rows, :], send_sems, recv_sems, j, (cx, cy, c))
               for j, (cx, cy) in enumerate(_other_chips(x, y))]
        return [(cp, cp) for cp in cps]


class _GatherStep:
    n_sems = 2 * (NCH - 1)

    def __init__(self, first, second):
        self.has_first = first is not None
        self.operands = [b for b in (first, second) if b is not None]
        self.out_shapes = [jax.ShapeDtypeStruct(b.shape, b.dtype) for b in self.operands]
        self.aliases = {i: i for i in range(len(self.operands))}

    def copies(self, cin, cout, send_sems, recv_sems):
        x, y, c = _mesh_pos()
        k = 2 * x + y
        sibling = (x, y, 1 - c)
        bufs = list(cout)
        out = []
        if self.has_first:
            buf = bufs.pop(0)
            mine = _rows_half(buf.at[k], c)
            for j, (cx, cy) in enumerate(_other_chips(x, y)):
                theirs = _rows_half(buf.at[2 * cx + cy], c)
                out.append((_remote(mine, mine, send_sems, recv_sems, j, (cx, cy, c)),
                            _remote(theirs, theirs, send_sems, recv_sems, j, (cx, cy, c))))
        if bufs:
            buf = bufs.pop(0)
            for j, (cx, cy) in enumerate(_other_chips(x, y)):
                landed = _rows_half(buf.at[2 * cx + cy], c)
                theirs = _rows_half(buf.at[2 * cx + cy], 1 - c)
                out.append((_remote(landed, landed, send_sems, recv_sems, NCH - 1 + j, sibling),
                            _remote(theirs, theirs, send_sems, recv_sems, NCH - 1 + j, sibling)))
        return out


class _GatherQueue:
    def __init__(self, bufs):
        self.bufs = list(bufs)
        self.bufs[0:2] = _allgather_chips(self.bufs[0:2], name="first_allgather")
        self.carrier = 0

    def take(self):
        j = self.carrier
        self.carrier += 1
        first = j + 2 if j + 2 < len(self.bufs) else None
        second = j + 1 if 2 <= j + 1 < len(self.bufs) else None
        self.cur = [i for i in (first, second) if i is not None]
        if not self.cur:
            return None
        return _GatherStep(None if first is None else self.bufs[first], None if second is None else self.bufs[second])

    def give(self, outs):
        for i, buf in zip(self.cur, outs):
            self.bufs[i] = buf


def _run_exchange(job, *, name):
    n_in, n_out = len(job.operands), len(job.out_shapes)

    def body(*refs):
        cps = job.copies(refs[:n_in], refs[n_in:n_in + n_out], refs[-2], refs[-1])
        for cp, _ in cps:
            cp.start()
        for _, arrival in cps:
            arrival.wait_recv()
        for cp, _ in cps:
            cp.wait_send()

    return pl.pallas_call(
        body, name=name, in_specs=[_ANY] * n_in, out_specs=[_ANY] * n_out, out_shape=list(job.out_shapes),
        input_output_aliases=dict(job.aliases),
        scratch_shapes=[pltpu.SemaphoreType.DMA((job.n_sems,)), pltpu.SemaphoreType.DMA((job.n_sems,))],
    )(*job.operands)


class _ExchangeQueue:
    def __init__(self, parts=2):
        self.parts = parts
        self.todo = []
        self.groups = {}

    def add(self, group, sums, done):
        self.groups[group] = dict(sums=sums, lands=[None] * len(sums), left=len(sums) * self.parts, done=done)
        self.todo += [(group, ai, p) for ai in range(len(sums)) for p in range(self.parts)]

    def take(self):
        if not self.todo:
            return None
        group, ai, p = self.cur = self.todo.pop(0)
        g = self.groups[group]
        step = g['sums'][ai].shape[1] // self.parts
        return _ChipExchange(g['sums'][ai], p * step, (p + 1) * step, g['lands'][ai])

    def give(self, outs):
        group, ai, _ = self.cur
        g = self.groups[group]
        g['lands'][ai] = outs[0]
        g['left'] -= 1

    def finish_ready(self):
        for group in [k for k, g in self.groups.items() if g['left'] == 0]:
            g = self.groups.pop(group)
            g['done'](g['sums'], g['lands'])

    def drain(self, name):
        while (job := self.take()) is not None:
            self.give(_run_exchange(job, name=name))
        self.finish_ready()


def _mm(queue, a, b, **kw):
    job = queue.take() if queue is not None else None
    if job is None:
        return _matmul(a, b, **kw)
    n_out = len(kw.get("out_dtypes") or (0,))
    outs = _matmul(a, b, comm=job, **kw)
    queue.give(outs[n_out:])
    return outs[0] if n_out == 1 else outs[:n_out]


def _share_with_sibling(arrs, *, name):
    n = len(arrs)

    def body(*refs):
        outs = refs[n:2 * n]
        send_sems, recv_sems = refs[2 * n:]
        x, y, c = _mesh_pos()
        sibling = (x, y, 1 - c)
        cps = []
        for i in range(n):
            mine = _rows_half(outs[i], c)
            cps.append(_remote(mine, mine, send_sems, recv_sems, i, sibling))
        for cp in cps:
            cp.start()
        for i in range(n):
            theirs = _rows_half(outs[i], 1 - c)
            _remote(theirs, theirs, send_sems, recv_sems, i, sibling).wait_recv()
        for cp in cps:
            cp.wait_send()

    return pl.pallas_call(
        body, name=name, in_specs=[_ANY] * n, out_specs=[_ANY] * n,
        out_shape=[jax.ShapeDtypeStruct(a.shape, a.dtype) for a in arrs],
        input_output_aliases={i: i for i in range(n)},
        scratch_shapes=[pltpu.SemaphoreType.DMA((n,)), pltpu.SemaphoreType.DMA((n,))],
    )(*arrs)


def _allgather_devices(v, *, name):
    m = v.shape[0]

    def body(v_ref, out_ref, send_sems, recv_sems, local_sem):
        x, y, c = _mesh_pos()
        mine = 4 * x + 2 * y + c
        own = pltpu.make_async_copy(v_ref, out_ref.at[mine], local_sem)
        own.start()
        cps = []
        for r in range(1, 8):
            px, py, pc = (x + (r >> 2)) % 2, (y + ((r >> 1) & 1)) % 2, (c + (r & 1)) % 2
            cps.append((_remote(v_ref, out_ref.at[mine], send_sems, recv_sems, r - 1, (px, py, pc)), 4 * px + 2 * py + pc))
        for cp, _ in cps:
            cp.start()
        for r, (cp, theirs) in enumerate(cps):
            blk = out_ref.at[theirs]
            _remote(blk, blk, send_sems, recv_sems, r, (x, y, c)).wait_recv()
        for cp, _ in cps:
            cp.wait_send()
        own.wait()

    return pl.pallas_call(
        body, name=name, in_specs=[_ANY], out_specs=_ANY,
        out_shape=jax.ShapeDtypeStruct((8, m, LANES), v.dtype),
        scratch_shapes=[pltpu.SemaphoreType.DMA((7,)), pltpu.SemaphoreType.DMA((7,)), pltpu.SemaphoreType.DMA],
    )(v)


def _add_own_half(g, land, c_idx, *, name):
    nch, rh, cols = land.shape
    tr = _pick(rh, _HALF_ROWS)
    nt = rh // tr

    def body(c_ref, g_ref, l_ref, o_ref):
        o_ref[...] = (g_ref[...].astype(F32) + l_ref[...].astype(F32)).astype(o_ref.dtype)

    return pl.pallas_call(
        body, name=name,
        grid_spec=pltpu.PrefetchScalarGridSpec(
            num_scalar_prefetch=1, grid=(nch, nt),
            in_specs=[pl.BlockSpec((None, tr, cols), lambda k, i, c: (k, c[0] * nt + i, 0)),
                      pl.BlockSpec((None, tr, cols), lambda k, i, c: (k, i, 0))],
            out_specs=pl.BlockSpec((None, tr, cols), lambda k, i, c: (k, i, 0))),
        out_shape=jax.ShapeDtypeStruct(land.shape, BF16),
        compiler_params=_cparams("parallel", "parallel"),
    )(c_idx, g, land)


def _add_chunks(s, land, k_idx, c_idx, *, name):
    _, rh, cols = s.shape
    tr = _pick(rh, _HALF_ROWS)
    nt = rh // tr

    def body(k_ref, c_ref, s_ref, l_ref, o_ref):
        t = s_ref[...].astype(F32)
        for j in range(NCH - 1):
            t = t + l_ref[j].astype(F32)
        o_ref[...] = t

    return pl.pallas_call(
        body, name=name,
        grid_spec=pltpu.PrefetchScalarGridSpec(
            num_scalar_prefetch=2, grid=(nt,),
            in_specs=[pl.BlockSpec((None, tr, cols), lambda i, k, c: (k[0], i, 0)),
                      pl.BlockSpec((NCH - 1, tr, cols), lambda i, k, c: (0, i, 0))],
            out_specs=pl.BlockSpec((tr, cols), lambda i, k, c: (c[0] * nt + i, 0))),
        out_shape=jax.ShapeDtypeStruct((2 * rh, cols), F32),
        compiler_params=_cparams("parallel"),
    )(k_idx, c_idx, s, land)


def _sum_devices(v, *, name):
    _, m, _ = v.shape

    def body(v_ref, o_ref):
        t = v_ref[0]
        for d in range(1, 8):
            t = t + v_ref[d]
        o_ref[...] = t

    return pl.pallas_call(
        body, name=name, grid=(1,),
        in_specs=[pl.BlockSpec((8, m, LANES), lambda i: (0, 0, 0))],
        out_specs=pl.BlockSpec((m, LANES), lambda i: (0, 0)),
        out_shape=jax.ShapeDtypeStruct((m, LANES), F32),
        compiler_params=_cparams("arbitrary"),
    )(v)


def _adamw_math(w, g, m, v):
    m = ADAM_B1 * m + (1.0 - ADAM_B1) * g
    v = ADAM_B2 * v + (1.0 - ADAM_B2) * (g * g)
    m_hat = m / (1.0 - ADAM_B1 ** ADAM_STEP)
    v_hat = v / (1.0 - ADAM_B2 ** ADAM_STEP)
    delta = -ADAM_LR * (m_hat / (jnp.sqrt(v_hat) + ADAM_EPS) + ADAM_WD * w)
    return delta, m, v


def _adamw_layer(w, m, v, g, layer, prev, *, name):
    depth, rows, cols = w.shape
    tr = _pick(rows, (256, 128, 64, 32, 16, 8))
    lay = pl.BlockSpec((None, tr, cols), lambda i, l: (l[0], i, 0))
    n_prev = 0 if prev is None else 4

    def body(l_ref, w_ref, m_ref, v_ref, g_ref, *rest):
        go_ref, d_ref, mo_ref, vo_ref = rest[n_prev:]
        g = g_ref[...]
        delta, m_new, v_new = _adamw_math(w_ref[...], g, m_ref[...], v_ref[...])
        go_ref[...] = g
        d_ref[...] = delta
        mo_ref[...] = m_new
        vo_ref[...] = v_new

    stack = jax.ShapeDtypeStruct(w.shape, F32)
    return pl.pallas_call(
        body, name=name,
        grid_spec=pltpu.PrefetchScalarGridSpec(
            num_scalar_prefetch=1, grid=(rows // tr,),
            in_specs=[lay, lay, lay, pl.BlockSpec((tr, cols), lambda i, l: (i, 0))] + [_ANY] * n_prev,
            out_specs=[lay] * 4),
        out_shape=[stack] * 4,
        input_output_aliases={} if prev is None else {5 + q: q for q in range(4)},
        compiler_params=_cparams("parallel"),
    )(layer, w, m, v, g, *(() if prev is None else prev))


def _adamw_flat(w, g, m, v, *, name):
    def body(w_ref, g_ref, m_ref, v_ref, d_ref, mo_ref, vo_ref):
        d_ref[...], mo_ref[...], vo_ref[...] = _adamw_math(w_ref[...], g_ref[...], m_ref[...], v_ref[...])

    blk = pl.BlockSpec(w.shape, lambda i: (0, 0))
    return pl.pallas_call(
        body, name=name, grid=(1,), in_specs=[blk] * 4, out_specs=[blk] * 3,
        out_shape=[jax.ShapeDtypeStruct(w.shape, F32)] * 3, compiler_params=_cparams("arbitrary"),
    )(w, g, m, v)


def _reduce_scatter_begin(parts, c_idx, tag):
    lands = _exchange_sibling_halves(parts, name=f"{tag}_rs_sibling")
    return [_add_own_half(p, l, c_idx, name=f"{tag}_rs_add2") for p, l in zip(parts, lands)]


def _reduce_scatter_end(sums, lands, c_idx, k_idx, tag):
    tots = [_add_chunks(s, l, k_idx, c_idx, name=f"{tag}_rs_add4") for s, l in zip(sums, lands)]
    return _share_with_sibling(tots, name=f"{tag}_rs_share")


_WEIGHTS = ('meta_tokens', 'ffn1_norm', 'ffn1_w_gate', 'ffn1_w_up', 'ffn1_w_down', 'mix_norm', 'w_in', 'b_forget',
            'fox_q_norm', 'fox_k_norm', 'swa_q_norm', 'swa_k_norm', 'swa_sinks', 'fox_out_norm', 'swa_out_norm', 'w_out',
            'ffn2_norm', 'ffn2_w_gate', 'ffn2_w_up', 'ffn2_w_down')
_BIG = ('ffn1_w_gate', 'ffn1_w_up', 'ffn1_w_down', 'w_in', 'w_out', 'ffn2_w_gate', 'ffn2_w_up', 'ffn2_w_down')
_SMALL = tuple(n for n in _WEIGHTS if n not in _BIG and n != 'meta_tokens')
_MIX_SMALL = ('mix_norm', 'b_forget', 'fox_q_norm', 'fox_k_norm', 'swa_q_norm', 'swa_k_norm', 'swa_sinks',
              'fox_out_norm', 'swa_out_norm')


def _pack_rows(vectors):
    flat = jnp.concatenate([v.reshape(-1) for v in vectors])
    n = flat.shape[0]
    m = -(-n // (8 * LANES)) * 8
    return jnp.pad(flat, (0, m * LANES - n)).reshape(m, LANES)


def _unpack_rows(packed, shapes):
    flat = packed.reshape(-1)
    out, o = [], 0
    for s in shapes:
        n = int(np.prod(s))
        out.append(flat[o:o + n].reshape(s))
        o += n
    return out


def kernel(x, meta_tokens, ffn1_norm, ffn1_w_gate, ffn1_w_up, ffn1_w_down, mix_norm, w_in, b_forget, fox_q_norm, fox_k_norm, swa_q_norm, swa_k_norm, swa_sinks, fox_out_norm, swa_out_norm, w_out, ffn2_norm, ffn2_w_gate, ffn2_w_up, ffn2_w_down, loss_target, m_meta_tokens, m_ffn1_norm, m_ffn1_w_gate, m_ffn1_w_up, m_ffn1_w_down, m_mix_norm, m_w_in, m_b_forget, m_fox_q_norm, m_fox_k_norm, m_swa_q_norm, m_swa_k_norm, m_swa_sinks, m_fox_out_norm, m_swa_out_norm, m_w_out, m_ffn2_norm, m_ffn2_w_gate, m_ffn2_w_up, m_ffn2_w_down, v_meta_tokens, v_ffn1_norm, v_ffn1_w_gate, v_ffn1_w_up, v_ffn1_w_down, v_mix_norm, v_w_in, v_b_forget, v_fox_q_norm, v_fox_k_norm, v_swa_q_norm, v_swa_k_norm, v_swa_sinks, v_fox_out_norm, v_swa_out_norm, v_w_out, v_ffn2_norm, v_ffn2_w_gate, v_ffn2_w_up, v_ffn2_w_down):
    W = dict(meta_tokens=meta_tokens, ffn1_norm=ffn1_norm, ffn1_w_gate=ffn1_w_gate, ffn1_w_up=ffn1_w_up, ffn1_w_down=ffn1_w_down, mix_norm=mix_norm, w_in=w_in, b_forget=b_forget, fox_q_norm=fox_q_norm, fox_k_norm=fox_k_norm, swa_q_norm=swa_q_norm, swa_k_norm=swa_k_norm, swa_sinks=swa_sinks, fox_out_norm=fox_out_norm, swa_out_norm=swa_out_norm, w_out=w_out, ffn2_norm=ffn2_norm, ffn2_w_gate=ffn2_w_gate, ffn2_w_up=ffn2_w_up, ffn2_w_down=ffn2_w_down)
    Mo = dict(meta_tokens=m_meta_tokens, ffn1_norm=m_ffn1_norm, ffn1_w_gate=m_ffn1_w_gate, ffn1_w_up=m_ffn1_w_up, ffn1_w_down=m_ffn1_w_down, mix_norm=m_mix_norm, w_in=m_w_in, b_forget=m_b_forget, fox_q_norm=m_fox_q_norm, fox_k_norm=m_fox_k_norm, swa_q_norm=m_swa_q_norm, swa_k_norm=m_swa_k_norm, swa_sinks=m_swa_sinks, fox_out_norm=m_fox_out_norm, swa_out_norm=m_swa_out_norm, w_out=m_w_out, ffn2_norm=m_ffn2_norm, ffn2_w_gate=m_ffn2_w_gate, ffn2_w_up=m_ffn2_w_up, ffn2_w_down=m_ffn2_w_down)
    Vo = dict(meta_tokens=v_meta_tokens, ffn1_norm=v_ffn1_norm, ffn1_w_gate=v_ffn1_w_gate, ffn1_w_up=v_ffn1_w_up, ffn1_w_down=v_ffn1_w_down, mix_norm=v_mix_norm, w_in=v_w_in, b_forget=v_b_forget, fox_q_norm=v_fox_q_norm, fox_k_norm=v_fox_k_norm, swa_q_norm=v_swa_q_norm, swa_k_norm=v_swa_k_norm, swa_sinks=v_swa_sinks, fox_out_norm=v_fox_out_norm, swa_out_norm=v_swa_out_norm, w_out=v_w_out, ffn2_norm=v_ffn2_norm, ffn2_w_gate=v_ffn2_w_gate, ffn2_w_up=v_ffn2_w_up, ffn2_w_down=v_ffn2_w_down)

    _, S, D = x.shape
    L = S + BLOCK
    depth = ffn1_norm.shape[0]
    md = _MixDims(D)
    mx, my, mc = _mesh_pos()
    k_idx = (2 * mx + my).astype(jnp.int32).reshape(1)
    c_idx = mc.astype(jnp.int32).reshape(1)
    dcols = D // NCH

    meta_all = _allgather_devices(meta_tokens.reshape(-1, LANES), name="meta_allgather")
    meta_full = jnp.transpose(meta_all[0::2].reshape(NCH, N_META, dcols), (1, 0, 2)).reshape(N_META, D)

    order = ('ffn1_w_gate', 'ffn1_w_up', 'ffn1_w_down', 'w_in', 'w_out', 'ffn2_w_gate', 'ffn2_w_up', 'ffn2_w_down')
    chunks = []
    for l in range(depth):
        l_idx = jnp.full((1,), l, jnp.int32)
        chunks += [_cast_into_chunk(W[name], l_idx, k_idx, name="cast_chunk") for name in order]
    gather = _GatherQueue(chunks)
    wts = [{} for _ in range(depth)]

    def weight(l, i):
        if i not in wts[l]:
            buf = gather.bufs[len(order) * l + i]
            if order[i] == 'w_in':
                buf = _win_to_mine(jnp.transpose(buf, (1, 0, 2)).reshape(D, NCH * buf.shape[2]), md)
            elif order[i] in ('w_out', 'ffn1_w_down', 'ffn2_w_down'):
                buf = buf.reshape(-1, D)
            wts[l][i] = buf
        return wts[l][i]

    h = jnp.concatenate([jnp.zeros((PAD, D), F32), meta_full, x[0]], axis=0)
    saved = []
    for l in range(depth):
        sp = _mix_small({n: W[n][l] for n in _MIX_SMALL}, md)
        h, s1 = _ffn_fwd(h, ffn1_norm[l].reshape(1, D), lambda i: weight(l, i), "ffn", gather)
        h, s2 = _mix_fwd(h, sp, lambda i: weight(l, 3 + i), md, gather)
        h, s3 = _ffn_fwd(h, ffn2_norm[l].reshape(1, D), lambda i: weight(l, 5 + i), "ffn", gather)
        saved.append((s1, s2, s3, sp))
    wts = [dict(g1=w[0], u1=w[1], d1=w[2], wi=w[3], wo=w[4], g2=w[5], u2=w[6], d2=w[7]) for w in wts]

    loss_part, dh = _loss_grad(h, loss_target[0], name="loss_grad")

    small_grads = {n: [None] * depth for n in _SMALL}
    stacks = {n: None for n in _BIG}

    def update(name, l, grad):
        stacks[name] = _adamw_layer(W[name], Mo[name], Vo[name], grad, jnp.full((1,), l, jnp.int32), stacks[name],
                                    name="adamw_layer")

    queue = _ExchangeQueue()

    def reduce_later(names, l, parts, tag):
        def done(sums, lands):
            for n, grad in zip(names, _reduce_scatter_end(sums, lands, c_idx, k_idx, tag)):
                update(n, l, grad)
        queue.add((tag, names[0], l), _reduce_scatter_begin(parts, c_idx, tag), done)

    for l in range(depth - 1, -1, -1):
        wl = wts[l]
        s1, s2, s3, sp = saved[l]
        dh, dg, dwg, dwu, dwd = _ffn_bwd(dh, s3, ffn2_norm[l].reshape(1, D), wl['g2'], wl['u2'], wl['d2'], "ffn", queue)
        queue.finish_ready()
        small_grads['ffn2_norm'][l] = dg.reshape(-1)
        reduce_later(('ffn2_w_gate', 'ffn2_w_up', 'ffn2_w_down'), l, [dwg, dwu, dwd.reshape(NCH, -1, D)], "ffn")

        dh, dwi, dwo, sm = _mix_bwd(dh, s2, sp, wl['wi'], wl['wo'], md, queue)
        queue.finish_ready()
        for n in _MIX_SMALL:
            small_grads[n][l] = sm[n]
        dwi = _win_grad_to_ref(dwi, md)
        dwi = jnp.transpose(dwi.reshape(D, NCH, -1), (1, 0, 2))
        reduce_later(('w_in', 'w_out'), l, [dwi, dwo.reshape(NCH, -1, D)], "mix")

        dh, dg, dwg, dwu, dwd = _ffn_bwd(dh, s1, ffn1_norm[l].reshape(1, D), wl['g1'], wl['u1'], wl['d1'], "ffn", queue)
        queue.finish_ready()
        small_grads['ffn1_norm'][l] = dg.reshape(-1)
        reduce_later(('ffn1_w_gate', 'ffn1_w_up', 'ffn1_w_down'), l, [dwg, dwu, dwd.reshape(NCH, -1, D)], "ffn")
    queue.drain("rs_chips")

    grad_x = dh[BLOCK:][None]

    small_shapes = [W[n].shape for n in _SMALL]
    parts = [jnp.stack(small_grads[n]) for n in _SMALL] + [dh[PAD:BLOCK], loss_part[0, :1]]
    packed = _pack_rows(parts)
    total = _sum_devices(_allgather_devices(packed, name="small_allgather"), name="small_sum")
    *g_small, g_meta, loss = _unpack_rows(total, small_shapes + [(N_META, D), (1,)])
    g_meta = lax.dynamic_slice(g_meta, (0, k_idx[0] * dcols), (N_META, dcols))

    sw = _pack_rows([W[n] for n in _SMALL])
    sd, smm, svv = _adamw_flat(sw, _pack_rows(g_small), _pack_rows([Mo[n] for n in _SMALL]),
                               _pack_rows([Vo[n] for n in _SMALL]), name="adamw_small")
    d_small, m_small, v_small = (_unpack_rows(t, small_shapes) for t in (sd, smm, svv))
    d_meta, m_meta, v_meta = _adamw_flat(meta_tokens, g_meta, m_meta_tokens, v_meta_tokens, name="adamw_meta")

    grads, deltas, new_m, new_v = {}, {}, {}, {}
    for n in _BIG:
        grads[n], deltas[n], new_m[n], new_v[n] = stacks[n]
    for i, n in enumerate(_SMALL):
        grads[n], deltas[n], new_m[n], new_v[n] = g_small[i], d_small[i], m_small[i], v_small[i]
    grads['meta_tokens'], deltas['meta_tokens'], new_m['meta_tokens'], new_v['meta_tokens'] = g_meta, d_meta, m_meta, v_meta
    return (loss.reshape(()), grad_x, *[grads[n] for n in _WEIGHTS], *[deltas[n] for n in _WEIGHTS],
            *[new_m[n] for n in _WEIGHTS], *[new_v[n] for n in _WEIGHTS])
```

```python
import numpy as np
import jax
import jax.numpy as jnp
from jax import lax
from jax.experimental import pallas as pl
from jax.experimental.pallas import tpu as pltpu

F32 = jnp.float32
BF16 = jnp.bfloat16

HEAD_DIM = 64
N_META = 16
BLOCK = 128
WINDOW = 128
PAD = BLOCK - N_META
EPS = 1e-6
NEG_INF = -1e30
SWA_GROUP = 8
NCH = 4
LANES = 128
QBLOCK = 512

ADAM_LR = 0.001
ADAM_B1 = 0.9
ADAM_B2 = 0.999
ADAM_EPS = 1e-08
ADAM_WD = 0.01
ADAM_STEP = 10

V7X_VMEM_BYTES = 64 * 1024 * 1024
VMEM_LIMIT = V7X_VMEM_BYTES - 8 * 1024 * 1024
MESH = pl.DeviceIdType.MESH
HIGHEST = lax.Precision.HIGHEST

_TM = (1088, 1024, 704, 512, 384, 256, 128)
_TN = (1408, 1024, 768, 512, 384, 256, 128)
_TK = (2176, 1408, 1024, 512, 384, 256, 128)
_TR = (544, 512, 384, 272, 256, 128)


def _pick(n, cands):
    for c in cands:
        if n % c == 0:
            return c
    return n


def _cparams(*sem):
    return pltpu.CompilerParams(dimension_semantics=sem if sem else None, vmem_limit_bytes=VMEM_LIMIT)


def _matmul(a, b, *, name, nt=False, b_chunked=False, out_chunked=False, out_dtype=F32,
            residual=None, scale=1.0, extras=(), epilogue=None, out_dtypes=None, comm=None):
    M, K = a.shape
    if not nt:
        N = b.shape[-1] * (NCH if b_chunked else 1)
        assert b.shape[-2] == K
        k_unit = K
    else:
        N = b.shape[-2]
        k_unit = b.shape[-1]
        assert k_unit * (NCH if b_chunked else 1) == K
    n_unit = N // NCH if (out_chunked or (b_chunked and not nt)) else N
    tm, tn, tk = _pick(M, _TM), _pick(n_unit, _TN), _pick(k_unit, _TK)
    if epilogue is None:
        extras = () if residual is None else (residual,)
        out_dtypes = (out_dtype,)

        def epilogue(acc, *res):
            r = acc * scale if scale != 1.0 else acc
            return (r + res[0] if res else r,)
    n_out = len(out_dtypes)
    n_temps = 4 if n_out > 1 else 0
    assert not (extras and out_chunked)

    def est(tm_):
        return (2 * tm_ * tk * 2 + 2 * tk * tn * 2 + tm_ * tn * 4
                + sum(2 * tm_ * tn * jnp.dtype(d).itemsize for d in out_dtypes)
                + sum(2 * tm_ * tn * e.dtype.itemsize for e in extras) + n_temps * tm_ * tn * 4)

    while est(tm) > VMEM_LIMIT * 3 // 4 and tm % 32 == 0:
        tm //= 2
    npc, kpc = n_unit // tn, k_unit // tk
    nk = K // tk
    grid = (M // tm, N // tn, nk)

    a_spec = pl.BlockSpec((tm, tk), lambda i, j, k: (i, k))
    if not nt:
        if b_chunked:
            b_spec = pl.BlockSpec((None, tk, tn), lambda i, j, k: (j // npc, k, j % npc))
        else:
            b_spec = pl.BlockSpec((tk, tn), lambda i, j, k: (k, j))
        dims = (((1,), (0,)), ((), ()))
    else:
        if b_chunked:
            b_spec = pl.BlockSpec((None, tn, tk), lambda i, j, k: (k // kpc, j, k % kpc))
        else:
            b_spec = pl.BlockSpec((tn, tk), lambda i, j, k: (j, k))
        dims = (((1,), (1,)), ((), ()))
    tile = pl.BlockSpec((tm, tn), lambda i, j, k: (i, j))
    if out_chunked:
        o_spec = pl.BlockSpec((None, tm, tn), lambda i, j, k: (j // npc, i, j % npc))
        out_shapes = [jax.ShapeDtypeStruct((NCH, M, n_unit), d) for d in out_dtypes]
    else:
        o_spec = tile
        out_shapes = [jax.ShapeDtypeStruct((M, N), d) for d in out_dtypes]
    in_specs = [a_spec, b_spec] + [tile] * len(extras)
    args = [a, b, *extras]

    n_main = len(args)
    n_cin = 0 if comm is None else len(comm.operands)
    n_cout = 0 if comm is None else len(comm.out_shapes)

    def body(*refs):
        a_ref, b_ref = refs[0], refs[1]
        e_refs = refs[2:n_main]
        o_refs = refs[n_main + n_cin:n_main + n_cin + n_out]
        acc_ref = refs[n_main + n_cin + n_out + n_cout]
        i, j, k = pl.program_id(0), pl.program_id(1), pl.program_id(2)
        if comm is not None:
            cin = refs[n_main:n_main + n_cin]
            cout = refs[n_main + n_cin + n_out:n_main + n_cin + n_out + n_cout]
            send_sems, recv_sems = refs[-2:]

            @pl.when((i == 0) & (j == 0) & (k == 0))
            def _():
                for cp, _ in comm.copies(cin, cout, send_sems, recv_sems):
                    cp.start()

        @pl.when(k == 0)
        def _():
            acc_ref[...] = jnp.zeros_like(acc_ref)

        acc_ref[...] += lax.dot_general(a_ref[...], b_ref[...], dims, preferred_element_type=F32)

        @pl.when(k == nk - 1)
        def _():
            for o_ref, val in zip(o_refs, epilogue(acc_ref[...], *[e[...] for e in e_refs])):
                o_ref[...] = val.astype(o_ref.dtype)

        if comm is not None:
            @pl.when((i == grid[0] - 1) & (j == grid[1] - 1) & (k == nk - 1))
            def _():
                cps = comm.copies(cin, cout, send_sems, recv_sems)
                for _, arrival in cps:
                    arrival.wait_recv()
                for cp, _ in cps:
                    cp.wait_send()

    scratch = [pltpu.VMEM((tm, tn), F32)]
    if comm is None:
        outs = pl.pallas_call(
            body, name=name, grid=grid, in_specs=in_specs, out_specs=[o_spec] * n_out, out_shape=out_shapes,
            scratch_shapes=scratch, compiler_params=_cparams("parallel", "parallel", "arbitrary"),
        )(*args)
    else:
        scratch += [pltpu.SemaphoreType.DMA((comm.n_sems,)), pltpu.SemaphoreType.DMA((comm.n_sems,))]
        outs = pl.pallas_call(
            body, name=name, grid=grid, in_specs=in_specs + [_ANY] * n_cin,
            out_specs=[o_spec] * n_out + [_ANY] * n_cout, out_shape=out_shapes + list(comm.out_shapes),
            input_output_aliases={n_main + s: n_out + d for s, d in comm.aliases.items()},
            scratch_shapes=scratch, compiler_params=_cparams("arbitrary", "arbitrary", "arbitrary"),
        )(*args, *comm.operands)
    return outs[0] if len(outs) == 1 else tuple(outs)


def _call_carrying(body, comm, args, *, name, grid, in_specs, out_specs, out_shape, scratch_shapes):
    if comm is None:
        return pl.pallas_call(body, name=name, grid=grid, in_specs=in_specs, out_specs=out_specs, out_shape=out_shape,
                              scratch_shapes=scratch_shapes, compiler_params=_cparams("arbitrary"))(*args)
    n_in, n_out, n_scr = len(in_specs), len(out_specs), len(scratch_shapes)
    n_cin, n_cout = len(comm.operands), len(comm.out_shapes)

    def carrying(*refs):
        ins, cin = refs[:n_in], refs[n_in:n_in + n_cin]
        outs = refs[n_in + n_cin:n_in + n_cin + n_out]
        cout = refs[n_in + n_cin + n_out:n_in + n_cin + n_out + n_cout]
        scr = refs[n_in + n_cin + n_out + n_cout:n_in + n_cin + n_out + n_cout + n_scr]
        send_sems, recv_sems = refs[-2:]
        step = pl.program_id(0)

        @pl.when(step == 0)
        def _():
            for cp, _ in comm.copies(cin, cout, send_sems, recv_sems):
                cp.start()

        body(*ins, *outs, *scr)

        @pl.when(step == grid[0] - 1)
        def _():
            cps = comm.copies(cin, cout, send_sems, recv_sems)
            for _, arrival in cps:
                arrival.wait_recv()
            for cp, _ in cps:
                cp.wait_send()

    return pl.pallas_call(
        carrying, name=name, grid=grid, in_specs=list(in_specs) + [_ANY] * n_cin,
        out_specs=list(out_specs) + [_ANY] * n_cout, out_shape=list(out_shape) + list(comm.out_shapes),
        input_output_aliases={n_in + s: n_out + d for s, d in comm.aliases.items()},
        scratch_shapes=list(scratch_shapes) + [pltpu.SemaphoreType.DMA((comm.n_sems,)), pltpu.SemaphoreType.DMA((comm.n_sems,))],
        compiler_params=_cparams("arbitrary"),
    )(*args, *comm.operands)


def _carried(queue, fn, n_out, *args, **kw):
    job = queue.take() if queue is not None else None
    outs = fn(*args, comm=job, **kw)
    if job is not None:
        queue.give(outs[n_out:])
    return outs[:n_out]


def _transpose(x, *, name):
    M, N = x.shape
    tc = _pick(N, (512, 384, 256, 128))

    def body(x_ref, o_ref):
        o_ref[...] = x_ref[...].astype(F32).T.astype(o_ref.dtype)

    return pl.pallas_call(
        body, name=name, grid=(N // tc,),
        in_specs=[pl.BlockSpec((M, tc), lambda j: (0, j))],
        out_specs=pl.BlockSpec((tc, M), lambda j: (j, 0)),
        out_shape=jax.ShapeDtypeStruct((N, M), x.dtype),
        compiler_params=_cparams("parallel"),
    )(x)


def _rms_fwd(h, g, *, name):
    L, D = h.shape
    tr = _pick(L, _TR)

    def body(h_ref, g_ref, o_ref):
        x = h_ref[...]
        r = lax.rsqrt(jnp.mean(x * x, axis=-1, keepdims=True) + EPS)
        o_ref[...] = (x * r * g_ref[...]).astype(o_ref.dtype)

    return pl.pallas_call(
        body, name=name, grid=(L // tr,),
        in_specs=[pl.BlockSpec((tr, D), lambda i: (i, 0)), pl.BlockSpec((1, D), lambda i: (0, 0))],
        out_specs=pl.BlockSpec((tr, D), lambda i: (i, 0)),
        out_shape=jax.ShapeDtypeStruct((L, D), BF16),
        compiler_params=_cparams("parallel"),
    )(h, g)


def _rms_bwd(dy, h, g, dh, *, name):
    L, D = h.shape
    tr = _pick(L, _TR)

    def body(dy_ref, h_ref, g_ref, dh_ref, o_ref, dg_ref):
        i = pl.program_id(0)
        x = h_ref[...]
        dyv = dy_ref[...]
        r = lax.rsqrt(jnp.mean(x * x, axis=-1, keepdims=True) + EPS)
        xh = x * r
        dxh = dyv * g_ref[...]
        dx = r * (dxh - xh * jnp.mean(dxh * xh, axis=-1, keepdims=True))
        o_ref[...] = dh_ref[...] + dx
        part = jnp.sum(dyv * xh, axis=0, keepdims=True)

        @pl.when(i == 0)
        def _():
            dg_ref[...] = part

        @pl.when(i > 0)
        def _():
            dg_ref[...] += part

    row = pl.BlockSpec((tr, D), lambda i: (i, 0))
    vec = pl.BlockSpec((1, D), lambda i: (0, 0))
    return pl.pallas_call(
        body, name=name, grid=(L // tr,),
        in_specs=[row, row, vec, row], out_specs=[row, vec],
        out_shape=[jax.ShapeDtypeStruct((L, D), F32), jax.ShapeDtypeStruct((1, D), F32)],
        compiler_params=_cparams("arbitrary"),
    )(dy, h, g, dh)


def _swiglu_epilogue(up, gate):
    g = gate.astype(F32)
    return up, g * jax.nn.sigmoid(g) * up


def _swiglu_bwd_epilogue(acc, gate, up):
    d = 0.5 * acc
    g = gate.astype(F32)
    sg = jax.nn.sigmoid(g)
    return d * up.astype(F32) * sg * (1.0 + g * (1.0 - sg)), d * g * sg


def _cast_bf16(x, *, name):
    L, D = x.shape
    tr = _pick(L, _TR)

    def body(x_ref, o_ref):
        o_ref[...] = x_ref[...].astype(o_ref.dtype)

    blk = pl.BlockSpec((tr, D), lambda i: (i, 0))
    return pl.pallas_call(
        body, name=name, grid=(L // tr,), in_specs=[blk], out_specs=blk,
        out_shape=jax.ShapeDtypeStruct((L, D), BF16), compiler_params=_cparams("parallel"),
    )(x)


def _loss_grad(h, target, *, name):
    L, D = h.shape
    S = target.shape[0]
    nb = L // BLOCK

    def body(h_ref, t_ref, loss_ref, dh_ref):
        i = pl.program_id(0)

        @pl.when(i == 0)
        def _():
            loss_ref[...] = jnp.zeros_like(loss_ref)
            dh_ref[...] = jnp.zeros_like(dh_ref)

        @pl.when(i > 0)
        def _():
            err = h_ref[...] - t_ref[...]
            dh_ref[...] = err * (1.0 / D)
            loss_ref[...] += jnp.full(loss_ref.shape, (0.5 / D) * jnp.sum(err * err), F32)

    return pl.pallas_call(
        body, name=name, grid=(nb,),
        in_specs=[pl.BlockSpec((BLOCK, D), lambda i: (i, 0)),
                  pl.BlockSpec((BLOCK, D), lambda i: (jnp.maximum(i - 1, 0), 0))],
        out_specs=[pl.BlockSpec((1, LANES), lambda i: (0, 0)), pl.BlockSpec((BLOCK, D), lambda i: (i, 0))],
        out_shape=[jax.ShapeDtypeStruct((1, LANES), F32), jax.ShapeDtypeStruct((L, D), F32)],
        compiler_params=_cparams("arbitrary"),
    )(h, target)


def _ffn_fwd(h, g, w, tag, queue=None):
    hn = _rms_fwd(h, g, name=f"{tag}_rms")
    gate = _mm(queue, hn, w(0), name=f"{tag}_gate", b_chunked=True, out_dtype=BF16)
    up, act = _mm(queue, hn, w(1), name=f"{tag}_up", b_chunked=True, extras=(gate,), epilogue=_swiglu_epilogue,
                  out_dtypes=(BF16, BF16))
    h_out = _mm(queue, act, w(2), name=f"{tag}_down", residual=h, scale=0.5)
    return h_out, (h, hn, gate, up, act)


def _ffn_bwd(dh, saved, g, wg, wu, wd, tag, queue=None):
    h, hn, gate, up, act = saved
    dout = _cast_bf16(dh, name=f"{tag}_dout")
    dgate, dup = _mm(queue, dout, wd, name=f"{tag}_dact", nt=True, extras=(gate, up), epilogue=_swiglu_bwd_epilogue,
                     out_dtypes=(BF16, BF16))
    actT = _transpose(act, name=f"{tag}_actT")
    dwd = _mm(queue, actT, dout, name=f"{tag}_dwd", out_dtype=BF16, scale=0.5)
    hnT = _transpose(hn, name=f"{tag}_hnT")
    dwg = _mm(queue, hnT, dgate, name=f"{tag}_dwg", out_chunked=True, out_dtype=BF16)
    dwu = _mm(queue, hnT, dup, name=f"{tag}_dwu", out_chunked=True, out_dtype=BF16)
    dhn = _mm(queue, dgate, wg, name=f"{tag}_dhn_g", nt=True, b_chunked=True)
    dhn = _mm(queue, dup, wu, name=f"{tag}_dhn_u", nt=True, b_chunked=True, residual=dhn)
    dh_in, dg = _rms_bwd(dhn, h, g, dh, name=f"{tag}_drms")
    return dh_in, dg, dwg, dwu, dwd


class _MixDims:
    def __init__(self, d_model):
        self.wf = d_model // 2
        self.ws = d_model // 2
        self.pf = self.wf // LANES
        self.ps = self.ws // LANES
        self.hf = self.wf // HEAD_DIM
        self.hq = self.ws // HEAD_DIM
        self.nkv = max(1, self.hq // SWA_GROUP)
        self.g = self.hq // self.nkv
        self.bq_f, self.bk_f, self.bv_f = 0, self.pf, 2 * self.pf
        self.bq_s = 3 * self.pf
        self.bk_s = self.bq_s + self.ps
        self.bv_s = self.bk_s + self.nkv
        self.bz = self.bv_s + self.nkv
        self.nu = (self.bz + 1) * LANES
        self.nup = -(-self.nu // 512) * 512
        self.in_width = 3 * self.wf + self.hf + self.ws + 2 * self.nkv * HEAD_DIM
        assert self.hf <= 2 * (LANES // 8)

    def gate_lane(self, h):
        return 8 * (h // 2) + h % 2

    def column_map(self):
        wf, ws, hd = self.wf, self.ws, HEAD_DIM
        src = np.full((self.nup,), -1, np.int64)
        src[0:3 * wf] = np.arange(3 * wf)
        o_sq = 3 * wf + self.hf
        src[self.bq_s * LANES:self.bq_s * LANES + ws] = o_sq + np.arange(ws)
        o_sk = o_sq + ws
        o_sv = o_sk + self.nkv * hd
        for kv in range(self.nkv):
            for rep in range(2):
                c0 = (self.bk_s + kv) * LANES + rep * hd
                src[c0:c0 + hd] = o_sk + kv * hd + np.arange(hd)
                c0 = (self.bv_s + kv) * LANES + rep * hd
                src[c0:c0 + hd] = o_sv + kv * hd + np.arange(hd)
        for h in range(self.hf):
            src[self.bz * LANES + self.gate_lane(h)] = 3 * wf + h
        return src

    def grad_column_map(self):
        src = self.column_map()
        dst = np.zeros((self.in_width,), np.int64)
        for col in range(self.nup - 1, -1, -1):
            if src[col] >= 0:
                dst[src[col]] = col
        return dst


def _block_diag_mean():
    m = np.zeros((LANES, LANES), np.float32)
    m[:HEAD_DIM, :HEAD_DIM] = 1.0 / HEAD_DIM
    m[HEAD_DIM:, HEAD_DIM:] = 1.0 / HEAD_DIM
    return jnp.asarray(m)


def _fold_halves():
    m = np.eye(LANES, dtype=np.float32)
    m[np.arange(LANES), (np.arange(LANES) + HEAD_DIM) % LANES] = 1.0
    return jnp.asarray(m)


def _gate_expand(md):
    e = np.zeros((LANES, md.wf), np.float32)
    for h in range(md.hf):
        e[md.gate_lane(h), h * HEAD_DIM:(h + 1) * HEAD_DIM] = 1.0
    return jnp.asarray(e)


def _qblocks(L, qb):
    blocks = [(0, BLOCK)]
    r = BLOCK
    while r < L:
        blocks.append((r, qb))
        r += qb
    assert r == L
    return blocks


def _f32dot(a, b):
    return jnp.dot(a, b, precision=HIGHEST, preferred_element_type=F32)


_DIMS_NT = (((1,), (1,)), ((), ()))


def _dot_nt(a, b):
    return lax.dot_general(a, b, _DIMS_NT, preferred_element_type=F32)


def _dot_tn(a, b):
    return jnp.dot(a.T.astype(BF16), b, preferred_element_type=F32)


def _log_sigmoid(z):
    return jnp.minimum(z, 0.0) - jnp.log(1.0 + jnp.exp(-jnp.abs(z)))


def _gate_fwd(u, b, md, *, name):
    L = u.shape[0]
    nb = L // BLOCK
    expand = _gate_expand(md)

    def body(z_ref, b_ref, e_ref, cexp_ref, ct_ref, c_s):
        ri = lax.broadcasted_iota(jnp.int32, (BLOCK, BLOCK), 0)
        ci = lax.broadcasted_iota(jnp.int32, (BLOCK, BLOCK), 1)
        tri = (ri >= ci).astype(F32)
        carry = jnp.zeros((1, LANES), F32)
        for bi in range(nb):
            rows = pl.ds(bi * BLOCK, BLOCK)
            logf = _log_sigmoid(z_ref[rows, :] + b_ref[...])
            blk = _f32dot(tri, logf) + carry
            c_s[rows, :] = blk
            carry = blk[BLOCK - 1:BLOCK, :]
        c = c_s[...]
        ct_ref[...] = c.T
        cexp_ref[...] = _f32dot(c, e_ref[...])

    return pl.pallas_call(
        body, name=name, grid=(1,),
        in_specs=[pl.BlockSpec((L, LANES), lambda i: (0, md.bz)), pl.BlockSpec((1, LANES), lambda i: (0, 0)),
                  pl.BlockSpec((LANES, md.wf), lambda i: (0, 0))],
        out_specs=[pl.BlockSpec((L, md.wf), lambda i: (0, 0)), pl.BlockSpec((LANES, L), lambda i: (0, 0))],
        out_shape=[jax.ShapeDtypeStruct((L, md.wf), F32), jax.ShapeDtypeStruct((LANES, L), F32)],
        scratch_shapes=[pltpu.VMEM((L, LANES), F32)],
        compiler_params=_cparams("arbitrary"),
    )(u, b, expand)


def _gate_bwd(u, b, dck_t, md, *, name):
    L = u.shape[0]
    nb = L // BLOCK

    def body(z_ref, b_ref, dck_ref, dz_ref, db_ref, dc_s):
        ri = lax.broadcasted_iota(jnp.int32, (BLOCK, BLOCK), 0)
        ci = lax.broadcasted_iota(jnp.int32, (BLOCK, BLOCK), 1)
        triu = (ri <= ci).astype(F32)
        dc_s[...] = -dck_ref[...].T
        carry = jnp.zeros((1, LANES), F32)
        db = jnp.zeros((1, LANES), F32)
        for bi in range(nb - 1, -1, -1):
            rows = pl.ds(bi * BLOCK, BLOCK)
            blk = _f32dot(triu, dc_s[rows, :]) + carry
            carry = blk[0:1, :]
            z = z_ref[rows, :] + b_ref[...]
            dz = blk * jax.nn.sigmoid(-z)
            if bi == 0:
                dz = jnp.where(lax.broadcasted_iota(jnp.int32, (BLOCK, LANES), 0) >= PAD, dz, 0.0)
            dz_ref[rows, :] = dz
            db = db + jnp.sum(dz, axis=0, keepdims=True)
        db_ref[...] = db

    return pl.pallas_call(
        body, name=name, grid=(1,),
        in_specs=[pl.BlockSpec((L, LANES), lambda i: (0, md.bz)), pl.BlockSpec((1, LANES), lambda i: (0, 0)),
                  pl.BlockSpec((LANES, L), lambda i: (0, 0))],
        out_specs=[pl.BlockSpec((L, LANES), lambda i: (0, 0)), pl.BlockSpec((1, LANES), lambda i: (0, 0))],
        out_shape=[jax.ShapeDtypeStruct((L, LANES), F32), jax.ShapeDtypeStruct((1, LANES), F32)],
        scratch_shapes=[pltpu.VMEM((L, LANES), F32)],
        compiler_params=_cparams("arbitrary"),
    )(u, b, dck_t)


def _head_norm(x, g, bd):
    r = lax.rsqrt(_f32dot(x * x, bd) + EPS)
    xh = x * r
    return xh * g, xh, r


def _head_norm_bwd(dy, xh, r, g, bd):
    dxh = dy * g
    dx = r * (dxh - xh * _f32dot(dxh * xh, bd))
    return dx, jnp.sum(dy * xh, axis=0, keepdims=True)


def _lane_half():
    return lax.broadcasted_iota(jnp.int32, (1, LANES), 1) < HEAD_DIM


def _store_head_pair(x, half, a_s, b_s):
    a_s[...] = jnp.where(half, x, 0.0).astype(BF16)
    b_s[...] = jnp.where(half, 0.0, x).astype(BF16)


def _fox_scores(qm, kn_s, cexp_ref, ct_ref, r0, nr, klen, hh):
    s = _dot_nt(qm, kn_s[0:klen, :]) * (HEAD_DIM ** -0.5)
    s = s + cexp_ref[r0:r0 + nr, HEAD_DIM * hh:HEAD_DIM * hh + 1] - ct_ref[hh:hh + 1, 0:klen]
    qp = r0 + lax.broadcasted_iota(jnp.int32, (nr, klen), 0)
    kp = lax.broadcasted_iota(jnp.int32, (nr, klen), 1)
    return jnp.where((kp <= qp) & (kp >= PAD), s, NEG_INF)


def _fox_fwd(u, cexp, ct3, qg, kg, md, *, name, comm=None):
    L = u.shape[0]
    blocks = _qblocks(L, QBLOCK)
    bd = _block_diag_mean()

    def body(q_ref, k_ref, v_ref, cexp_ref, ct_ref, qg_ref, kg_ref, bd_ref, o_ref, lse_ref, qa_s, qb_s, kn_s, v_s):
        half = _lane_half()
        bdv = bd_ref[...]
        _store_head_pair(_head_norm(q_ref[...], qg_ref[...], bdv)[0], half, qa_s, qb_s)
        kn_s[...] = _head_norm(k_ref[...], kg_ref[...], bdv)[0].astype(BF16)
        v_s[...] = v_ref[...].astype(BF16)
        for r0, nr in blocks:
            klen = r0 + nr
            o_blk = lse_blk = None
            for hh, q_s in enumerate((qa_s, qb_s)):
                s = _fox_scores(q_s[r0:r0 + nr, :], kn_s, cexp_ref, ct_ref, r0, nr, klen, hh)
                m = jnp.max(s, axis=-1, keepdims=True)
                p = jnp.exp(s - m)
                l = jnp.sum(p, axis=-1, keepdims=True)
                oh = jnp.dot(p.astype(BF16), v_s[0:klen, :], preferred_element_type=F32) * (1.0 / l)
                lh = jnp.broadcast_to(jnp.where(m > 0.5 * NEG_INF, m + jnp.log(l), 0.0), (nr, LANES))
                o_blk = oh if hh == 0 else jnp.where(half, o_blk, oh)
                lse_blk = lh if hh == 0 else jnp.where(half, lse_blk, lh)
            o_ref[r0:r0 + nr, :] = o_blk
            lse_ref[r0:r0 + nr, :] = lse_blk

    col = lambda base: pl.BlockSpec((L, LANES), lambda j: (0, base + j))
    vec = pl.BlockSpec((1, LANES), lambda j: (0, 0))
    return _call_carrying(
        body, comm, (u, u, u, cexp, ct3, qg, kg, bd), name=name, grid=(md.pf,),
        in_specs=[col(md.bq_f), col(md.bk_f), col(md.bv_f), col(0),
                  pl.BlockSpec((None, 8, L), lambda j: (j, 0, 0)), vec, vec,
                  pl.BlockSpec((LANES, LANES), lambda j: (0, 0))],
        out_specs=[col(0), col(0)],
        out_shape=[jax.ShapeDtypeStruct((L, md.wf), F32)] * 2,
        scratch_shapes=[pltpu.VMEM((L, LANES), BF16)] * 4)


def _softmax_bwd(p, dp):
    pdp = p * dp
    return pdp - p * jnp.sum(pdp, axis=-1, keepdims=True)


def _fox_bwd(u, cexp, ct3, qg, kg, do, lse, md, *, name, comm=None):
    L = u.shape[0]
    blocks = _qblocks(L, QBLOCK // 2)
    bd = _block_diag_mean()
    fold = _fold_halves()
    npairs = md.pf

    def body(q_ref, k_ref, v_ref, cexp_ref, ct_ref, qg_ref, kg_ref, bd_ref, fold_ref, do_ref, lse_ref,
             dq_ref, dk_ref, dv_ref, dck_ref, dqg_ref, dkg_ref,
             qa_s, qb_s, kn_s, v_s, doa_s, dob_s, dqn_s, dkn_s, dvv_s):
        j = pl.program_id(0)
        half = _lane_half()
        bdv = bd_ref[...]
        qy, qh, rq = _head_norm(q_ref[...], qg_ref[...], bdv)
        ky, kh, rk = _head_norm(k_ref[...], kg_ref[...], bdv)
        _store_head_pair(qy, half, qa_s, qb_s)
        _store_head_pair(do_ref[...], half, doa_s, dob_s)
        kn_s[...] = ky.astype(BF16)
        v_s[...] = v_ref[...].astype(BF16)
        dkn_s[...] = jnp.zeros_like(dkn_s)
        dvv_s[...] = jnp.zeros_like(dvv_s)
        dck_ref[...] = jnp.zeros_like(dck_ref)
        for r0, nr in blocks:
            klen = r0 + nr
            dq_blk = None
            for hh, (q_s, do_s) in enumerate(((qa_s, doa_s), (qb_s, dob_s))):
                c0 = HEAD_DIM * hh
                qm = q_s[r0:r0 + nr, :]
                dom = do_s[r0:r0 + nr, :]
                s = _fox_scores(qm, kn_s, cexp_ref, ct_ref, r0, nr, klen, hh)
                p = jnp.exp(s - lse_ref[r0:r0 + nr, c0:c0 + 1])
                ds = _softmax_bwd(p, _dot_nt(dom, v_s[0:klen, :]))
                dck_ref[hh:hh + 1, 0:klen] += jnp.sum(ds, axis=0, keepdims=True)
                ds = ds * (HEAD_DIM ** -0.5)
                dq_h = jnp.dot(ds.astype(BF16), kn_s[0:klen, :], preferred_element_type=F32)
                dkn_s[0:klen, :] += _dot_tn(ds, qm)
                dvv_s[0:klen, :] += _dot_tn(p, dom)
                dq_blk = dq_h if hh == 0 else jnp.where(half, dq_blk, dq_h)
            dqn_s[r0:r0 + nr, :] = dq_blk
        dq, dqg = _head_norm_bwd(dqn_s[...], qh, rq, qg_ref[...], bdv)
        dk, dkg = _head_norm_bwd(dkn_s[...], kh, rk, kg_ref[...], bdv)
        dq_ref[...] = dq
        dk_ref[...] = dk
        dv_ref[...] = dvv_s[...]

        @pl.when(j == 0)
        def _():
            dqg_ref[...] = jnp.zeros_like(dqg_ref)
            dkg_ref[...] = jnp.zeros_like(dkg_ref)

        dqg_ref[...] += dqg
        dkg_ref[...] += dkg

        @pl.when(j == npairs - 1)
        def _():
            dqg_ref[...] = _f32dot(jnp.broadcast_to(dqg_ref[...], (8, LANES)), fold_ref[...])[0:1, :]
            dkg_ref[...] = _f32dot(jnp.broadcast_to(dkg_ref[...], (8, LANES)), fold_ref[...])[0:1, :]

    col = lambda base: pl.BlockSpec((L, LANES), lambda j: (0, base + j))
    vec = pl.BlockSpec((1, LANES), lambda j: (0, 0))
    sq = pl.BlockSpec((LANES, LANES), lambda j: (0, 0))
    ct_spec = pl.BlockSpec((None, 8, L), lambda j: (j, 0, 0))
    big = jax.ShapeDtypeStruct((L, md.wf), F32)
    small = jax.ShapeDtypeStruct((1, LANES), F32)
    return _call_carrying(
        body, comm, (u, u, u, cexp, ct3, qg, kg, bd, fold, do, lse), name=name, grid=(md.pf,),
        in_specs=[col(md.bq_f), col(md.bk_f), col(md.bv_f), col(0), ct_spec, vec, vec, sq, sq, col(0), col(0)],
        out_specs=[col(0), col(0), col(0), ct_spec, vec, vec],
        out_shape=[big, big, big, jax.ShapeDtypeStruct((md.pf, 8, L), F32), small, small],
        scratch_shapes=[pltpu.VMEM((L, LANES), BF16)] * 6 + [pltpu.VMEM((L, LANES), F32)] * 3)


def _swa_scores(qm, kn_s, slope, k0, r0, nr, klen):
    s = _dot_nt(qm, kn_s[k0:k0 + klen, :]) * (HEAD_DIM ** -0.5)
    qp = r0 + lax.broadcasted_iota(jnp.int32, (nr, klen), 0)
    kp = k0 + lax.broadcasted_iota(jnp.int32, (nr, klen), 1)
    dist = qp - kp
    s = s - slope * dist.astype(F32)
    return jnp.where((dist >= 0) & (dist < WINDOW) & (kp >= PAD), s, NEG_INF)


def _swa_fwd(u, sinkp, slopep, qg, kg, md, *, name, comm=None):
    L = u.shape[0]
    blocks = _qblocks(L, QBLOCK)
    bd = _block_diag_mean()

    def body(q_ref, k_ref, v_ref, sink_ref, slope_ref, qg_ref, kg_ref, bd_ref, o_ref, lse_ref, qa_s, qb_s, kn_s, v_s):
        half = _lane_half()
        bdv = bd_ref[...]
        _store_head_pair(_head_norm(q_ref[...], qg_ref[...], bdv)[0], half, qa_s, qb_s)
        kn_s[...] = _head_norm(k_ref[...], kg_ref[...], bdv)[0].astype(BF16)
        v_s[...] = v_ref[...].astype(BF16)
        for r0, nr in blocks:
            k0 = max(r0 - BLOCK, 0)
            klen = r0 + nr - k0
            o_blk = lse_blk = None
            for hh, q_s in enumerate((qa_s, qb_s)):
                c0 = HEAD_DIM * hh
                s = _swa_scores(q_s[r0:r0 + nr, :], kn_s, slope_ref[0:1, c0:c0 + 1], k0, r0, nr, klen)
                sink = sink_ref[0:1, c0:c0 + 1]
                m = jnp.maximum(jnp.max(s, axis=-1, keepdims=True), sink)
                p = jnp.exp(s - m)
                den = jnp.sum(p, axis=-1, keepdims=True) + jnp.exp(sink - m)
                oh = jnp.dot(p.astype(BF16), v_s[k0:k0 + klen, :], preferred_element_type=F32) * (1.0 / den)
                lh = jnp.broadcast_to(m + jnp.log(den), (nr, LANES))
                o_blk = oh if hh == 0 else jnp.where(half, o_blk, oh)
                lse_blk = lh if hh == 0 else jnp.where(half, lse_blk, lh)
            o_ref[r0:r0 + nr, :] = o_blk
            lse_ref[r0:r0 + nr, :] = lse_blk

    g2 = md.g // 2
    qcol = pl.BlockSpec((L, LANES), lambda j: (0, md.bq_s + j))
    kcol = pl.BlockSpec((L, LANES), lambda j: (0, md.bk_s + j // g2))
    vcol = pl.BlockSpec((L, LANES), lambda j: (0, md.bv_s + j // g2))
    ocol = pl.BlockSpec((L, LANES), lambda j: (0, j))
    pvec = pl.BlockSpec((1, LANES), lambda j: (0, j))
    vec = pl.BlockSpec((1, LANES), lambda j: (0, 0))
    return _call_carrying(
        body, comm, (u, u, u, sinkp, slopep, qg, kg, bd), name=name, grid=(md.ps,),
        in_specs=[qcol, kcol, vcol, pvec, pvec, vec, vec, pl.BlockSpec((LANES, LANES), lambda j: (0, 0))],
        out_specs=[ocol, ocol],
        out_shape=[jax.ShapeDtypeStruct((L, md.ws), F32)] * 2,
        scratch_shapes=[pltpu.VMEM((L, LANES), BF16)] * 4)


def _swa_bwd(u, sinkp, slopep, qg, kg, do, lse, md, *, name, comm=None):
    L = u.shape[0]
    blocks = _qblocks(L, QBLOCK // 2)
    bd = _block_diag_mean()
    fold = _fold_halves()
    g2 = md.g // 2
    npairs = md.ps

    def body(q_ref, k_ref, v_ref, sink_ref, slope_ref, qg_ref, kg_ref, bd_ref, fold_ref, do_ref, lse_ref,
             dq_ref, dk_ref, dv_ref, dsink_ref, dqg_ref, dkg_ref,
             qa_s, qb_s, kn_s, v_s, doa_s, dob_s, dqn_s):
        j = pl.program_id(0)
        half = _lane_half()
        bdv = bd_ref[...]
        qy, qh, rq = _head_norm(q_ref[...], qg_ref[...], bdv)
        ky, kh, rk = _head_norm(k_ref[...], kg_ref[...], bdv)
        _store_head_pair(qy, half, qa_s, qb_s)
        _store_head_pair(do_ref[...], half, doa_s, dob_s)
        kn_s[...] = ky.astype(BF16)
        v_s[...] = v_ref[...].astype(BF16)

        @pl.when(j % g2 == 0)
        def _():
            dk_ref[...] = jnp.zeros_like(dk_ref)
            dv_ref[...] = jnp.zeros_like(dv_ref)

        @pl.when(j == 0)
        def _():
            dqg_ref[...] = jnp.zeros_like(dqg_ref)
            dkg_ref[...] = jnp.zeros_like(dkg_ref)

        dsink = [jnp.zeros((1, 1), F32), jnp.zeros((1, 1), F32)]
        for r0, nr in blocks:
            k0 = max(r0 - BLOCK, 0)
            klen = r0 + nr - k0
            dq_blk = None
            for hh, (q_s, do_s) in enumerate(((qa_s, doa_s), (qb_s, dob_s))):
                c0 = HEAD_DIM * hh
                qm = q_s[r0:r0 + nr, :]
                dom = do_s[r0:r0 + nr, :]
                s = _swa_scores(qm, kn_s, slope_ref[0:1, c0:c0 + 1], k0, r0, nr, klen)
                lse_h = lse_ref[r0:r0 + nr, c0:c0 + 1]
                p = jnp.exp(s - lse_h)
                pdp = p * _dot_nt(dom, v_s[k0:k0 + klen, :])
                delta = jnp.sum(pdp, axis=-1, keepdims=True)
                p_sink = jnp.exp(sink_ref[0:1, c0:c0 + 1] - lse_h)
                dsink[hh] = dsink[hh] - jnp.sum(p_sink * delta, axis=0, keepdims=True)
                ds = (pdp - p * delta) * (HEAD_DIM ** -0.5)
                dq_h = jnp.dot(ds.astype(BF16), kn_s[k0:k0 + klen, :], preferred_element_type=F32)
                dk_ref[k0:k0 + klen, :] += _dot_tn(ds, qm)
                dv_ref[k0:k0 + klen, :] += _dot_tn(p, dom)
                dq_blk = dq_h if hh == 0 else jnp.where(half, dq_blk, dq_h)
            dqn_s[r0:r0 + nr, :] = dq_blk
        dq, dqg = _head_norm_bwd(dqn_s[...], qh, rq, qg_ref[...], bdv)
        dq_ref[...] = dq
        dqg_ref[...] += dqg
        dsink_ref[...] = jnp.where(half, jnp.broadcast_to(dsink[0], (1, LANES)), jnp.broadcast_to(dsink[1], (1, LANES)))

        @pl.when(j % g2 == g2 - 1)
        def _():
            dkn = _f32dot(dk_ref[...], fold_ref[...])
            dk, dkg = _head_norm_bwd(dkn, kh, rk, kg_ref[...], bdv)
            dk_ref[...] = jnp.where(half, dk, 0.0)
            dv_ref[...] = jnp.where(half, _f32dot(dv_ref[...], fold_ref[...]), 0.0)
            dkg_ref[...] += dkg

        @pl.when(j == npairs - 1)
        def _():
            dqg_ref[...] = _f32dot(jnp.broadcast_to(dqg_ref[...], (8, LANES)), fold_ref[...])[0:1, :]

    qcol = pl.BlockSpec((L, LANES), lambda j: (0, md.bq_s + j))
    kcol = pl.BlockSpec((L, LANES), lambda j: (0, md.bk_s + j // g2))
    vcol = pl.BlockSpec((L, LANES), lambda j: (0, md.bv_s + j // g2))
    ocol = pl.BlockSpec((L, LANES), lambda j: (0, j))
    kvout = pl.BlockSpec((L, LANES), lambda j: (0, j // g2))
    pvec = pl.BlockSpec((1, LANES), lambda j: (0, j))
    vec = pl.BlockSpec((1, LANES), lambda j: (0, 0))
    sq = pl.BlockSpec((LANES, LANES), lambda j: (0, 0))
    kvshape = jax.ShapeDtypeStruct((L, LANES * md.nkv), F32)
    small = jax.ShapeDtypeStruct((1, LANES), F32)
    return _call_carrying(
        body, comm, (u, u, u, sinkp, slopep, qg, kg, bd, fold, do, lse), name=name, grid=(md.ps,),
        in_specs=[qcol, kcol, vcol, pvec, pvec, vec, vec, sq, sq, ocol, ocol],
        out_specs=[ocol, kvout, kvout, pvec, vec, vec],
        out_shape=[jax.ShapeDtypeStruct((L, md.ws), F32), kvshape, kvshape,
                   jax.ShapeDtypeStruct((1, md.ws), F32), small, small],
        scratch_shapes=[pltpu.VMEM((L, LANES), BF16)] * 6 + [pltpu.VMEM((L, LANES), F32)])


def _outnorm_fwd(of, os_, gf, gs, *, name):
    L, wf = of.shape
    ws = os_.shape[1]
    tr = _pick(L, _TR)

    def body(of_ref, os_ref, gf_ref, gs_ref, o_ref):
        for src, g_ref, c0, w in ((of_ref, gf_ref, 0, wf), (os_ref, gs_ref, wf, ws)):
            x = src[...]
            r = lax.rsqrt(jnp.mean(x * x, axis=-1, keepdims=True) + EPS)
            o_ref[:, c0:c0 + w] = (x * r * g_ref[...]).astype(o_ref.dtype)

    return pl.pallas_call(
        body, name=name, grid=(L // tr,),
        in_specs=[pl.BlockSpec((tr, wf), lambda i: (i, 0)), pl.BlockSpec((tr, ws), lambda i: (i, 0)),
                  pl.BlockSpec((1, wf), lambda i: (0, 0)), pl.BlockSpec((1, ws), lambda i: (0, 0))],
        out_specs=pl.BlockSpec((tr, wf + ws), lambda i: (i, 0)),
        out_shape=jax.ShapeDtypeStruct((L, wf + ws), BF16),
        compiler_params=_cparams("parallel"),
    )(of, os_, gf, gs)


def _outnorm_bwd(don, of, os_, gf, gs, *, name):
    L, wf = of.shape
    ws = os_.shape[1]
    tr = _pick(L, _TR)

    def body(d_ref, of_ref, os_ref, gf_ref, gs_ref, dof_ref, dos_ref, dgf_ref, dgs_ref):
        i = pl.program_id(0)
        for src, g_ref, c0, w, dx_ref, dg_ref in ((of_ref, gf_ref, 0, wf, dof_ref, dgf_ref),
                                                  (os_ref, gs_ref, wf, ws, dos_ref, dgs_ref)):
            x = src[...]
            dy = d_ref[:, c0:c0 + w]
            r = lax.rsqrt(jnp.mean(x * x, axis=-1, keepdims=True) + EPS)
            xh = x * r
            dxh = dy * g_ref[...]
            dx_ref[...] = r * (dxh - xh * jnp.mean(dxh * xh, axis=-1, keepdims=True))
            part = jnp.sum(dy * xh, axis=0, keepdims=True)

            @pl.when(i == 0)
            def _():
                dg_ref[...] = part

            @pl.when(i > 0)
            def _():
                dg_ref[...] += part

    rf = pl.BlockSpec((tr, wf), lambda i: (i, 0))
    rs = pl.BlockSpec((tr, ws), lambda i: (i, 0))
    vf = pl.BlockSpec((1, wf), lambda i: (0, 0))
    vs = pl.BlockSpec((1, ws), lambda i: (0, 0))
    return pl.pallas_call(
        body, name=name, grid=(L // tr,),
        in_specs=[pl.BlockSpec((tr, wf + ws), lambda i: (i, 0)), rf, rs, vf, vs],
        out_specs=[rf, rs, vf, vs],
        out_shape=[jax.ShapeDtypeStruct((L, wf), F32), jax.ShapeDtypeStruct((L, ws), F32),
                   jax.ShapeDtypeStruct((1, wf), F32), jax.ShapeDtypeStruct((1, ws), F32)],
        compiler_params=_cparams("arbitrary"),
    )(don, of, os_, gf, gs)


def _win_to_mine(w, md):
    d = w.shape[0]
    wf, ws, hd = md.wf, md.ws, HEAD_DIM
    o_z = 3 * wf
    o_sq = o_z + md.hf
    o_sk = o_sq + ws
    o_sv = o_sk + md.nkv * hd
    parts = [w[:, :3 * wf], w[:, o_sq:o_sq + ws]]
    for base in (o_sk, o_sv):
        for kv in range(md.nkv):
            blk = w[:, base + kv * hd:base + (kv + 1) * hd]
            parts += [blk, blk]
    z = w[:, o_z:o_z + md.hf].reshape(d, md.hf // 2, 2)
    z = jnp.pad(z, ((0, 0), (0, 0), (0, 6))).reshape(d, 4 * md.hf)
    parts.append(jnp.pad(z, ((0, 0), (0, LANES - 4 * md.hf + md.nup - md.nu))))
    return jnp.concatenate(parts, axis=1)


def _win_grad_to_ref(dw, md):
    d = dw.shape[0]
    wf, ws, hd = md.wf, md.ws, HEAD_DIM
    z = dw[:, md.bz * LANES:md.bz * LANES + 4 * md.hf].reshape(d, md.hf // 2, 8)[:, :, :2].reshape(d, md.hf)
    parts = [dw[:, :3 * wf], z, dw[:, md.bq_s * LANES:md.bq_s * LANES + ws]]
    for base in (md.bk_s, md.bv_s):
        for kv in range(md.nkv):
            c0 = (base + kv) * LANES
            parts.append(dw[:, c0:c0 + hd])
    return jnp.concatenate(parts, axis=1)


def _mix_small(p, md):
    tile2 = lambda v: jnp.tile(v.reshape(1, HEAD_DIM), (1, 2))
    b = p["b_forget"].reshape(md.hf // 2, 2)
    b = jnp.pad(b, ((0, 0), (0, 6))).reshape(1, 4 * md.hf)
    slopes = np.asarray(2.0 ** (-8.0 * np.arange(1, md.hq + 1) / md.hq), np.float32)
    return dict(
        g_mix=p["mix_norm"].reshape(1, -1),
        b_gate=jnp.pad(b, ((0, 0), (0, LANES - 4 * md.hf))),
        fqg=tile2(p["fox_q_norm"]), fkg=tile2(p["fox_k_norm"]),
        sqg=tile2(p["swa_q_norm"]), skg=tile2(p["swa_k_norm"]),
        sinkp=jnp.repeat(p["swa_sinks"], HEAD_DIM).reshape(1, md.ws),
        slopep=jnp.asarray(np.repeat(slopes, HEAD_DIM).reshape(1, md.ws)),
        gfo=p["fox_out_norm"].reshape(1, md.wf), gso=p["swa_out_norm"].reshape(1, md.ws),
    )


def _mix_fwd(h, sp, w, md, queue=None):
    L = h.shape[0]
    hn = _rms_fwd(h, sp["g_mix"], name="mix_rms")
    u = _mm(queue, hn, w(0), name="mix_u")
    cexp, ct = _gate_fwd(u, sp["b_gate"], md, name="gate_fwd")
    ct3 = ct[:8 * md.pf].reshape(md.pf, 8, L)
    of, lsef = _carried(queue, _fox_fwd, 2, u, cexp, ct3, sp["fqg"], sp["fkg"], md, name="fox_fwd")
    os_, lses = _carried(queue, _swa_fwd, 2, u, sp["sinkp"], sp["slopep"], sp["sqg"], sp["skg"], md, name="swa_fwd")
    on = _outnorm_fwd(of, os_, sp["gfo"], sp["gso"], name="outnorm_fwd")
    h_out = _mm(queue, on, w(1), name="mix_out", residual=h)
    return h_out, (h, hn, u, cexp, ct3, of, lsef, os_, lses, on)


def _mix_bwd(dh, saved, sp, w_in, w_out, md, queue=None):
    h, hn, u, cexp, ct3, of, lsef, os_, lses, on = saved
    L = h.shape[0]
    dhb = _cast_bf16(dh, name="mix_dhb")
    don = _matmul(dhb, w_out, name="mix_don", nt=True)
    dw_out = _matmul(_transpose(on, name="mix_onT"), dhb, name="mix_dwout", out_dtype=BF16)
    dof, dos, dgfo, dgso = _outnorm_bwd(don, of, os_, sp["gfo"], sp["gso"], name="outnorm_bwd")
    duq, duk, duv, dck, dfqg, dfkg = _carried(queue, _fox_bwd, 6, u, cexp, ct3, sp["fqg"], sp["fkg"], dof, lsef, md,
                                              name="fox_bwd")
    dsq, dsk, dsv, dsinkp, dsqg, dskg = _carried(queue, _swa_bwd, 6, u, sp["sinkp"], sp["slopep"], sp["sqg"], sp["skg"], dos,
                                                 lses, md, name="swa_bwd")
    dck_t = jnp.pad(dck.reshape(8 * md.pf, L), ((0, LANES - 8 * md.pf), (0, 0)))
    dz, db = _gate_bwd(u, sp["b_gate"], dck_t, md, name="gate_bwd")
    du = jnp.concatenate([duq, duk, duv, dsq, dsk, dsv, dz, jnp.zeros((L, md.nup - md.nu), F32)], axis=1).astype(BF16)
    dhn = _mm(queue, du, w_in, name="mix_dhn", nt=True)
    dw_in = _mm(queue, _transpose(hn, name="mix_hnT"), du, name="mix_dwin", out_dtype=BF16)
    dh_in, dg_mix = _rms_bwd(dhn, h, sp["g_mix"], dh, name="mix_drms")
    small = dict(
        mix_norm=dg_mix.reshape(-1),
        b_forget=db[0, :4 * md.hf].reshape(md.hf // 2, 8)[:, :2].reshape(md.hf),
        fox_q_norm=dfqg[0, :HEAD_DIM], fox_k_norm=dfkg[0, :HEAD_DIM],
        swa_q_norm=dsqg[0, :HEAD_DIM], swa_k_norm=dskg[0, :HEAD_DIM],
        swa_sinks=dsinkp[0, ::HEAD_DIM],
        fox_out_norm=dgfo.reshape(-1), swa_out_norm=dgso.reshape(-1),
    )
    return dh_in, dw_in, dw_out, small


_ANY = pl.BlockSpec(memory_space=pl.ANY)
_HALF_ROWS = (512, 352, 256, 192, 128, 64, 32, 16)


def _mesh_pos():
    return lax.axis_index("x"), lax.axis_index("y"), lax.axis_index("c")


def _other_chips(x, y):
    return [(1 - x, y), (x, 1 - y), (1 - x, 1 - y)]


def _rows_half(ref, which):
    rh = ref.shape[-2] // 2
    if len(ref.shape) == 2:
        return ref.at[pl.ds(which * rh, rh), :]
    return ref.at[:, pl.ds(which * rh, rh), :]


def _remote(src, dst, send_sems, recv_sems, idx, dev):
    return pltpu.make_async_remote_copy(src_ref=src, dst_ref=dst, send_sem=send_sems.at[idx], recv_sem=recv_sems.at[idx],
                                        device_id=dev, device_id_type=MESH)


def _cast_into_chunk(w, l_idx, k_idx, *, name):
    _, rows, cols = w.shape
    tr = _pick(rows, (512, 352, 256, 128, 64, 32, 16))

    def body(l_ref, k_ref, w_ref, o_ref):
        o_ref[...] = w_ref[...].astype(o_ref.dtype)

    return pl.pallas_call(
        body, name=name,
        grid_spec=pltpu.PrefetchScalarGridSpec(
            num_scalar_prefetch=2, grid=(rows // tr,),
            in_specs=[pl.BlockSpec((None, tr, cols), lambda i, l, k: (l[0], i, 0))],
            out_specs=pl.BlockSpec((None, tr, cols), lambda i, l, k: (k[0], i, 0))),
        out_shape=jax.ShapeDtypeStruct((NCH, rows, cols), BF16),
        compiler_params=_cparams("parallel"),
    )(l_idx, k_idx, w)


def _allgather_chips(arrs, *, name):
    n = len(arrs)

    def body(*refs):
        outs = refs[n:2 * n]
        send_sems, recv_sems = refs[2 * n:]
        x, y, c = _mesh_pos()
        k = 2 * x + y
        sibling = (x, y, 1 - c)
        chips = _other_chips(x, y)
        sent = []
        for i in range(n):
            mine = _rows_half(outs[i].at[k], c)
            for j, (cx, cy) in enumerate(chips):
                cp = _remote(mine, mine, send_sems, recv_sems, 6 * i + j, (cx, cy, c))
                cp.start()
                sent.append(cp)
        for i in range(n):
            for j, (cx, cy) in enumerate(chips):
                blk = _rows_half(outs[i].at[2 * cx + cy], c)
                _remote(blk, blk, send_sems, recv_sems, 6 * i + j, (cx, cy, c)).wait_recv()
                fwd = _remote(blk, blk, send_sems, recv_sems, 6 * i + 3 + j, sibling)
                fwd.start()
                sent.append(fwd)
        for i in range(n):
            for j, (cx, cy) in enumerate(chips):
                blk = _rows_half(outs[i].at[2 * cx + cy], 1 - c)
                _remote(blk, blk, send_sems, recv_sems, 6 * i + 3 + j, sibling).wait_recv()
        for cp in sent:
            cp.wait_send()

    return pl.pallas_call(
        body, name=name, in_specs=[_ANY] * n, out_specs=[_ANY] * n,
        out_shape=[jax.ShapeDtypeStruct(a.shape, a.dtype) for a in arrs],
        input_output_aliases={i: i for i in range(n)},
        scratch_shapes=[pltpu.SemaphoreType.DMA((6 * n,)), pltpu.SemaphoreType.DMA((6 * n,))],
    )(*arrs)


def _exchange_sibling_halves(arrs, *, name):
    n = len(arrs)

    def body(*refs):
        srcs, lands = refs[:n], refs[n:2 * n]
        send_sems, recv_sems = refs[2 * n:]
        x, y, c = _mesh_pos()
        sibling = (x, y, 1 - c)
        cps = [_remote(_rows_half(srcs[i], 1 - c), lands[i], send_sems, recv_sems, i, sibling) for i in range(n)]
        for cp in cps:
            cp.start()
        for cp in cps:
            cp.wait_recv()
        for cp in cps:
            cp.wait_send()

    return pl.pallas_call(
        body, name=name, in_specs=[_ANY] * n, out_specs=[_ANY] * n,
        out_shape=[jax.ShapeDtypeStruct((NCH, a.shape[1] // 2, a.shape[2]), a.dtype) for a in arrs],
        scratch_shapes=[pltpu.SemaphoreType.DMA((n,)), pltpu.SemaphoreType.DMA((n,))],
    )(*arrs)


class _ChipExchange:
    n_sems = NCH - 1

    def __init__(self, s, r0, r1, land=None):
        self.r0, self.r1 = r0, r1
        self.operands = [s] if land is None else [s, land]
        self.out_shapes = [jax.ShapeDtypeStruct((NCH - 1,) + s.shape[1:], s.dtype)]
        self.aliases = {} if land is None else {1: 0}

    def copies(self, cin, cout, send_sems, recv_sems):
        x, y, c = _mesh_pos()
        rows = pl.ds(self.r0, self.r1 - self.r0)
        cps = [_remote(cin[0].at[2 * cx + cy, rows, :], cout[0].at[j, rows, :], send_sems, recv_sems, j, (cx, cy, c))
               for j, (cx, cy) in enumerate(_other_chips(x, y))]
        return [(cp, cp) for cp in cps]


class _GatherStep:
    n_sems = 2 * (NCH - 1)

    def __init__(self, first, second):
        self.has_first = first is not None
        self.operands = [b for b in (first, second) if b is not None]
        self.out_shapes = [jax.ShapeDtypeStruct(b.shape, b.dtype) for b in self.operands]
        self.aliases = {i: i for i in range(len(self.operands))}

    def copies(self, cin, cout, send_sems, recv_sems):
        x, y, c = _mesh_pos()
        k = 2 * x + y
        sibling = (x, y, 1 - c)
        bufs = list(cout)
        out = []
        if self.has_first:
            buf = bufs.pop(0)
            mine = _rows_half(buf.at[k], c)
            for j, (cx, cy) in enumerate(_other_chips(x, y)):
                theirs = _rows_half(buf.at[2 * cx + cy], c)
                out.append((_remote(mine, mine, send_sems, recv_sems, j, (cx, cy, c)),
                            _remote(theirs, theirs, send_sems, recv_sems, j, (cx, cy, c))))
        if bufs:
            buf = bufs.pop(0)
            for j, (cx, cy) in enumerate(_other_chips(x, y)):
                landed = _rows_half(buf.at[2 * cx + cy], c)
                theirs = _rows_half(buf.at[2 * cx + cy], 1 - c)
                out.append((_remote(landed, landed, send_sems, recv_sems, NCH - 1 + j, sibling),
                            _remote(theirs, theirs, send_sems, recv_sems, NCH - 1 + j, sibling)))
        return out


class _GatherQueue:
    def __init__(self, bufs):
        self.bufs = list(bufs)
        self.bufs[0:2] = _allgather_chips(self.bufs[0:2], name="first_allgather")
        self.carrier = 0

    def take(self):
        j = self.carrier
        self.carrier += 1
        first = j + 2 if j + 2 < len(self.bufs) else None
        second = j + 1 if 2 <= j + 1 < len(self.bufs) else None
        self.cur = [i for i in (first, second) if i is not None]
        if not self.cur:
            return None
        return _GatherStep(None if first is None else self.bufs[first], None if second is None else self.bufs[second])

    def give(self, outs):
        for i, buf in zip(self.cur, outs):
            self.bufs[i] = buf


def _run_exchange(job, *, name):
    n_in, n_out = len(job.operands), len(job.out_shapes)

    def body(*refs):
        cps = job.copies(refs[:n_in], refs[n_in:n_in + n_out], refs[-2], refs[-1])
        for cp, _ in cps:
            cp.start()
        for _, arrival in cps:
            arrival.wait_recv()
        for cp, _ in cps:
            cp.wait_send()

    return pl.pallas_call(
        body, name=name, in_specs=[_ANY] * n_in, out_specs=[_ANY] * n_out, out_shape=list(job.out_shapes),
        input_output_aliases=dict(job.aliases),
        scratch_shapes=[pltpu.SemaphoreType.DMA((job.n_sems,)), pltpu.SemaphoreType.DMA((job.n_sems,))],
    )(*job.operands)


class _ExchangeQueue:
    def __init__(self, parts=2):
        self.parts = parts
        self.todo = []
        self.groups = {}

    def add(self, group, sums, done):
        self.groups[group] = dict(sums=sums, lands=[None] * len(sums), left=len(sums) * self.parts, done=done)
        self.todo += [(group, ai, p) for ai in range(len(sums)) for p in range(self.parts)]

    def take(self):
        if not self.todo:
            return None
        group, ai, p = self.cur = self.todo.pop(0)
        g = self.groups[group]
        step = g['sums'][ai].shape[1] // self.parts
        return _ChipExchange(g['sums'][ai], p * step, (p + 1) * step, g['lands'][ai])

    def give(self, outs):
        group, ai, _ = self.cur
        g = self.groups[group]
        g['lands'][ai] = outs[0]
        g['left'] -= 1

    def finish_ready(self):
        for group in [k for k, g in self.groups.items() if g['left'] == 0]:
            g = self.groups.pop(group)
            g['done'](g['sums'], g['lands'])

    def drain(self, name):
        while (job := self.take()) is not None:
            self.give(_run_exchange(job, name=name))
        self.finish_ready()


def _mm(queue, a, b, **kw):
    job = queue.take() if queue is not None else None
    if job is None:
        return _matmul(a, b, **kw)
    n_out = len(kw.get("out_dtypes") or (0,))
    outs = _matmul(a, b, comm=job, **kw)
    queue.give(outs[n_out:])
    return outs[0] if n_out == 1 else outs[:n_out]


def _share_with_sibling(arrs, *, name):
    n = len(arrs)

    def body(*refs):
        outs = refs[n:2 * n]
        send_sems, recv_sems = refs[2 * n:]
        x, y, c = _mesh_pos()
        sibling = (x, y, 1 - c)
        cps = []
        for i in range(n):
            mine = _rows_half(outs[i], c)
            cps.append(_remote(mine, mine, send_sems, recv_sems, i, sibling))
        for cp in cps:
            cp.start()
        for i in range(n):
            theirs = _rows_half(outs[i], 1 - c)
            _remote(theirs, theirs, send_sems, recv_sems, i, sibling).wait_recv()
        for cp in cps:
            cp.wait_send()

    return pl.pallas_call(
        body, name=name, in_specs=[_ANY] * n, out_specs=[_ANY] * n,
        out_shape=[jax.ShapeDtypeStruct(a.shape, a.dtype) for a in arrs],
        input_output_aliases={i: i for i in range(n)},
        scratch_shapes=[pltpu.SemaphoreType.DMA((n,)), pltpu.SemaphoreType.DMA((n,))],
    )(*arrs)


def _allgather_devices(v, *, name):
    m = v.shape[0]

    def body(v_ref, out_ref, send_sems, recv_sems, local_sem):
        x, y, c = _mesh_pos()
        mine = 4 * x + 2 * y + c
        own = pltpu.make_async_copy(v_ref, out_ref.at[mine], local_sem)
        own.start()
        cps = []
        for r in range(1, 8):
            px, py, pc = (x + (r >> 2)) % 2, (y + ((r >> 1) & 1)) % 2, (c + (r & 1)) % 2
            cps.append((_remote(v_ref, out_ref.at[mine], send_sems, recv_sems, r - 1, (px, py, pc)), 4 * px + 2 * py + pc))
        for cp, _ in cps:
            cp.start()
        for r, (cp, theirs) in enumerate(cps):
            blk = out_ref.at[theirs]
            _remote(blk, blk, send_sems, recv_sems, r, (x, y, c)).wait_recv()
        for cp, _ in cps:
            cp.wait_send()
        own.wait()

    return pl.pallas_call(
        body, name=name, in_specs=[_ANY], out_specs=_ANY,
        out_shape=jax.ShapeDtypeStruct((8, m, LANES), v.dtype),
        scratch_shapes=[pltpu.SemaphoreType.DMA((7,)), pltpu.SemaphoreType.DMA((7,)), pltpu.SemaphoreType.DMA],
    )(v)


def _add_own_half(g, land, c_idx, *, name):
    nch, rh, cols = land.shape
    tr = _pick(rh, _HALF_ROWS)
    nt = rh // tr

    def body(c_ref, g_ref, l_ref, o_ref):
        o_ref[...] = (g_ref[...].astype(F32) + l_ref[...].astype(F32)).astype(o_ref.dtype)

    return pl.pallas_call(
        body, name=name,
        grid_spec=pltpu.PrefetchScalarGridSpec(
            num_scalar_prefetch=1, grid=(nch, nt),
            in_specs=[pl.BlockSpec((None, tr, cols), lambda k, i, c: (k, c[0] * nt + i, 0)),
                      pl.BlockSpec((None, tr, cols), lambda k, i, c: (k, i, 0))],
            out_specs=pl.BlockSpec((None, tr, cols), lambda k, i, c: (k, i, 0))),
        out_shape=jax.ShapeDtypeStruct(land.shape, BF16),
        compiler_params=_cparams("parallel", "parallel"),
    )(c_idx, g, land)


def _add_chunks(s, land, k_idx, c_idx, *, name):
    _, rh, cols = s.shape
    tr = _pick(rh, _HALF_ROWS)
    nt = rh // tr

    def body(k_ref, c_ref, s_ref, l_ref, o_ref):
        t = s_ref[...].astype(F32)
        for j in range(NCH - 1):
            t = t + l_ref[j].astype(F32)
        o_ref[...] = t

    return pl.pallas_call(
        body, name=name,
        grid_spec=pltpu.PrefetchScalarGridSpec(
            num_scalar_prefetch=2, grid=(nt,),
            in_specs=[pl.BlockSpec((None, tr, cols), lambda i, k, c: (k[0], i, 0)),
                      pl.BlockSpec((NCH - 1, tr, cols), lambda i, k, c: (0, i, 0))],
            out_specs=pl.BlockSpec((tr, cols), lambda i, k, c: (c[0] * nt + i, 0))),
        out_shape=jax.ShapeDtypeStruct((2 * rh, cols), F32),
        compiler_params=_cparams("parallel"),
    )(k_idx, c_idx, s, land)


def _sum_devices(v, *, name):
    _, m, _ = v.shape

    def body(v_ref, o_ref):
        t = v_ref[0]
        for d in range(1, 8):
            t = t + v_ref[d]
        o_ref[...] = t

    return pl.pallas_call(
        body, name=name, grid=(1,),
        in_specs=[pl.BlockSpec((8, m, LANES), lambda i: (0, 0, 0))],
        out_specs=pl.BlockSpec((m, LANES), lambda i: (0, 0)),
        out_shape=jax.ShapeDtypeStruct((m, LANES), F32),
        compiler_params=_cparams("arbitrary"),
    )(v)


def _adamw_math(w, g, m, v):
    m = ADAM_B1 * m + (1.0 - ADAM_B1) * g
    v = ADAM_B2 * v + (1.0 - ADAM_B2) * (g * g)
    m_hat = m / (1.0 - ADAM_B1 ** ADAM_STEP)
    v_hat = v / (1.0 - ADAM_B2 ** ADAM_STEP)
    delta = -ADAM_LR * (m_hat / (jnp.sqrt(v_hat) + ADAM_EPS) + ADAM_WD * w)
    return delta, m, v


def _adamw_layer(w, m, v, g, layer, prev, *, name):
    depth, rows, cols = w.shape
    tr = _pick(rows, (256, 128, 64, 32, 16, 8))
    lay = pl.BlockSpec((None, tr, cols), lambda i, l: (l[0], i, 0))
    n_prev = 0 if prev is None else 4

    def body(l_ref, w_ref, m_ref, v_ref, g_ref, *rest):
        go_ref, d_ref, mo_ref, vo_ref = rest[n_prev:]
        g = g_ref[...]
        delta, m_new, v_new = _adamw_math(w_ref[...], g, m_ref[...], v_ref[...])
        go_ref[...] = g
        d_ref[...] = delta
        mo_ref[...] = m_new
        vo_ref[...] = v_new

    stack = jax.ShapeDtypeStruct(w.shape, F32)
    return pl.pallas_call(
        body, name=name,
        grid_spec=pltpu.PrefetchScalarGridSpec(
            num_scalar_prefetch=1, grid=(rows // tr,),
            in_specs=[lay, lay, lay, pl.BlockSpec((tr, cols), lambda i, l: (i, 0))] + [_ANY] * n_prev,
            out_specs=[lay] * 4),
        out_shape=[stack] * 4,
        input_output_aliases={} if prev is None else {5 + q: q for q in range(4)},
        compiler_params=_cparams("parallel"),
    )(layer, w, m, v, g, *(() if prev is None else prev))


def _adamw_flat(w, g, m, v, *, name):
    def body(w_ref, g_ref, m_ref, v_ref, d_ref, mo_ref, vo_ref):
        d_ref[...], mo_ref[...], vo_ref[...] = _adamw_math(w_ref[...], g_ref[...], m_ref[...], v_ref[...])

    blk = pl.BlockSpec(w.shape, lambda i: (0, 0))
    return pl.pallas_call(
        body, name=name, grid=(1,), in_specs=[blk] * 4, out_specs=[blk] * 3,
        out_shape=[jax.ShapeDtypeStruct(w.shape, F32)] * 3, compiler_params=_cparams("arbitrary"),
    )(w, g, m, v)


def _reduce_scatter_begin(parts, c_idx, tag):
    lands = _exchange_sibling_halves(parts, name=f"{tag}_rs_sibling")
    return [_add_own_half(p, l, c_idx, name=f"{tag}_rs_add2") for p, l in zip(parts, lands)]


def _reduce_scatter_end(sums, lands, c_idx, k_idx, tag):
    tots = [_add_chunks(s, l, k_idx, c_idx, name=f"{tag}_rs_add4") for s, l in zip(sums, lands)]
    return _share_with_sibling(tots, name=f"{tag}_rs_share")


_WEIGHTS = ('meta_tokens', 'ffn1_norm', 'ffn1_w_gate', 'ffn1_w_up', 'ffn1_w_down', 'mix_norm', 'w_in', 'b_forget',
            'fox_q_norm', 'fox_k_norm', 'swa_q_norm', 'swa_k_norm', 'swa_sinks', 'fox_out_norm', 'swa_out_norm', 'w_out',
            'ffn2_norm', 'ffn2_w_gate', 'ffn2_w_up', 'ffn2_w_down')
_BIG = ('ffn1_w_gate', 'ffn1_w_up', 'ffn1_w_down', 'w_in', 'w_out', 'ffn2_w_gate', 'ffn2_w_up', 'ffn2_w_down')
_SMALL = tuple(n for n in _WEIGHTS if n not in _BIG and n != 'meta_tokens')
_MIX_SMALL = ('mix_norm', 'b_forget', 'fox_q_norm', 'fox_k_norm', 'swa_q_norm', 'swa_k_norm', 'swa_sinks',
              'fox_out_norm', 'swa_out_norm')


def _pack_rows(vectors):
    flat = jnp.concatenate([v.reshape(-1) for v in vectors])
    n = flat.shape[0]
    m = -(-n // (8 * LANES)) * 8
    return jnp.pad(flat, (0, m * LANES - n)).reshape(m, LANES)


def _unpack_rows(packed, shapes):
    flat = packed.reshape(-1)
    out, o = [], 0
    for s in shapes:
        n = int(np.prod(s))
        out.append(flat[o:o + n].reshape(s))
        o += n
    return out


def kernel(x, meta_tokens, ffn1_norm, ffn1_w_gate, ffn1_w_up, ffn1_w_down, mix_norm, w_in, b_forget, fox_q_norm, fox_k_norm, swa_q_norm, swa_k_norm, swa_sinks, fox_out_norm, swa_out_norm, w_out, ffn2_norm, ffn2_w_gate, ffn2_w_up, ffn2_w_down, loss_target, m_meta_tokens, m_ffn1_norm, m_ffn1_w_gate, m_ffn1_w_up, m_ffn1_w_down, m_mix_norm, m_w_in, m_b_forget, m_fox_q_norm, m_fox_k_norm, m_swa_q_norm, m_swa_k_norm, m_swa_sinks, m_fox_out_norm, m_swa_out_norm, m_w_out, m_ffn2_norm, m_ffn2_w_gate, m_ffn2_w_up, m_ffn2_w_down, v_meta_tokens, v_ffn1_norm, v_ffn1_w_gate, v_ffn1_w_up, v_ffn1_w_down, v_mix_norm, v_w_in, v_b_forget, v_fox_q_norm, v_fox_k_norm, v_swa_q_norm, v_swa_k_norm, v_swa_sinks, v_fox_out_norm, v_swa_out_norm, v_w_out, v_ffn2_norm, v_ffn2_w_gate, v_ffn2_w_up, v_ffn2_w_down):
    W = dict(meta_tokens=meta_tokens, ffn1_norm=ffn1_norm, ffn1_w_gate=ffn1_w_gate, ffn1_w_up=ffn1_w_up, ffn1_w_down=ffn1_w_down, mix_norm=mix_norm, w_in=w_in, b_forget=b_forget, fox_q_norm=fox_q_norm, fox_k_norm=fox_k_norm, swa_q_norm=swa_q_norm, swa_k_norm=swa_k_norm, swa_sinks=swa_sinks, fox_out_norm=fox_out_norm, swa_out_norm=swa_out_norm, w_out=w_out, ffn2_norm=ffn2_norm, ffn2_w_gate=ffn2_w_gate, ffn2_w_up=ffn2_w_up, ffn2_w_down=ffn2_w_down)
    Mo = dict(meta_tokens=m_meta_tokens, ffn1_norm=m_ffn1_norm, ffn1_w_gate=m_ffn1_w_gate, ffn1_w_up=m_ffn1_w_up, ffn1_w_down=m_ffn1_w_down, mix_norm=m_mix_norm, w_in=m_w_in, b_forget=m_b_forget, fox_q_norm=m_fox_q_norm, fox_k_norm=m_fox_k_norm, swa_q_norm=m_swa_q_norm, swa_k_norm=m_swa_k_norm, swa_sinks=m_swa_sinks, fox_out_norm=m_fox_out_norm, swa_out_norm=m_swa_out_norm, w_out=m_w_out, ffn2_norm=m_ffn2_norm, ffn2_w_gate=m_ffn2_w_gate, ffn2_w_up=m_ffn2_w_up, ffn2_w_down=m_ffn2_w_down)
    Vo = dict(meta_tokens=v_meta_tokens, ffn1_norm=v_ffn1_norm, ffn1_w_gate=v_ffn1_w_gate, ffn1_w_up=v_ffn1_w_up, ffn1_w_down=v_ffn1_w_down, mix_norm=v_mix_norm, w_in=v_w_in, b_forget=v_b_forget, fox_q_norm=v_fox_q_norm, fox_k_norm=v_fox_k_norm, swa_q_norm=v_swa_q_norm, swa_k_norm=v_swa_k_norm, swa_sinks=v_swa_sinks, fox_out_norm=v_fox_out_norm, swa_out_norm=v_swa_out_norm, w_out=v_w_out, ffn2_norm=v_ffn2_norm, ffn2_w_gate=v_ffn2_w_gate, ffn2_w_up=v_ffn2_w_up, ffn2_w_down=v_ffn2_w_down)

    _, S, D = x.shape
    L = S + BLOCK
    depth = ffn1_norm.shape[0]
    md = _MixDims(D)
    mx, my, mc = _mesh_pos()
    k_idx = (2 * mx + my).astype(jnp.int32).reshape(1)
    c_idx = mc.astype(jnp.int32).reshape(1)
    dcols = D // NCH

    meta_all = _allgather_devices(meta_tokens.reshape(-1, LANES), name="meta_allgather")
    meta_full = jnp.transpose(meta_all[0::2].reshape(NCH, N_META, dcols), (1, 0, 2)).reshape(N_META, D)

    order = ('ffn1_w_gate', 'ffn1_w_up', 'ffn1_w_down', 'w_in', 'w_out', 'ffn2_w_gate', 'ffn2_w_up', 'ffn2_w_down')
    chunks = []
    for l in range(depth):
        l_idx = jnp.full((1,), l, jnp.int32)
        chunks += [_cast_into_chunk(W[name], l_idx, k_idx, name="cast_chunk") for name in order]
    gather = _GatherQueue(chunks)
    wts = [{} for _ in range(depth)]

    def weight(l, i):
        if i not in wts[l]:
            buf = gather.bufs[len(order) * l + i]
            if order[i] == 'w_in':
                buf = _win_to_mine(jnp.transpose(buf, (1, 0, 2)).reshape(D, NCH * buf.shape[2]), md)
            elif order[i] in ('w_out', 'ffn1_w_down', 'ffn2_w_down'):
                buf = buf.reshape(-1, D)
            wts[l][i] = buf
        return wts[l][i]

    h = jnp.concatenate([jnp.zeros((PAD, D), F32), meta_full, x[0]], axis=0)
    saved = []
    for l in range(depth):
        sp = _mix_small({n: W[n][l] for n in _MIX_SMALL}, md)
        h, s1 = _ffn_fwd(h, ffn1_norm[l].reshape(1, D), lambda i: weight(l, i), "ffn", gather)
        h, s2 = _mix_fwd(h, sp, lambda i: weight(l, 3 + i), md, gather)
        h, s3 = _ffn_fwd(h, ffn2_norm[l].reshape(1, D), lambda i: weight(l, 5 + i), "ffn", gather)
        saved.append((s1, s2, s3, sp))
    wts = [dict(g1=w[0], u1=w[1], d1=w[2], wi=w[3], wo=w[4], g2=w[5], u2=w[6], d2=w[7]) for w in wts]

    loss_part, dh = _loss_grad(h, loss_target[0], name="loss_grad")

    small_grads = {n: [None] * depth for n in _SMALL}
    stacks = {n: None for n in _BIG}

    def update(name, l, grad):
        stacks[name] = _adamw_layer(W[name], Mo[name], Vo[name], grad, jnp.full((1,), l, jnp.int32), stacks[name],
                                    name="adamw_layer")

    queue = _ExchangeQueue()

    def reduce_later(names, l, parts, tag):
        def done(sums, lands):
            for n, grad in zip(names, _reduce_scatter_end(sums, lands, c_idx, k_idx, tag)):
                update(n, l, grad)
        queue.add((tag, names[0], l), _reduce_scatter_begin(parts, c_idx, tag), done)

    for l in range(depth - 1, -1, -1):
        wl = wts[l]
        s1, s2, s3, sp = saved[l]
        dh, dg, dwg, dwu, dwd = _ffn_bwd(dh, s3, ffn2_norm[l].reshape(1, D), wl['g2'], wl['u2'], wl['d2'], "ffn", queue)
        queue.finish_ready()
        small_grads['ffn2_norm'][l] = dg.reshape(-1)
        reduce_later(('ffn2_w_gate', 'ffn2_w_up', 'ffn2_w_down'), l, [dwg, dwu, dwd.reshape(NCH, -1, D)], "ffn")

        dh, dwi, dwo, sm = _mix_bwd(dh, s2, sp, wl['wi'], wl['wo'], md, queue)
        queue.finish_ready()
        for n in _MIX_SMALL:
            small_grads[n][l] = sm[n]
        dwi = _win_grad_to_ref(dwi, md)
        dwi = jnp.transpose(dwi.reshape(D, NCH, -1), (1, 0, 2))
        reduce_later(('w_in', 'w_out'), l, [dwi, dwo.reshape(NCH, -1, D)], "mix")

        dh, dg, dwg, dwu, dwd = _ffn_bwd(dh, s1, ffn1_norm[l].reshape(1, D), wl['g1'], wl['u1'], wl['d1'], "ffn", queue)
        queue.finish_ready()
        small_grads['ffn1_norm'][l] = dg.reshape(-1)
        reduce_later(('ffn1_w_gate', 'ffn1_w_up', 'ffn1_w_down'), l, [dwg, dwu, dwd.reshape(NCH, -1, D)], "ffn")
    queue.drain("rs_chips")

    grad_x = dh[BLOCK:][None]

    small_shapes = [W[n].shape for n in _SMALL]
    parts = [jnp.stack(small_grads[n]) for n in _SMALL] + [dh[PAD:BLOCK], loss_part[0, :1]]
    packed = _pack_rows(parts)
    total = _sum_devices(_allgather_devices(packed, name="small_allgather"), name="small_sum")
    *g_small, g_meta, loss = _unpack_rows(total, small_shapes + [(N_META, D), (1,)])
    g_meta = lax.dynamic_slice(g_meta, (0, k_idx[0] * dcols), (N_META, dcols))

    sw = _pack_rows([W[n] for n in _SMALL])
    sd, smm, svv = _adamw_flat(sw, _pack_rows(g_small), _pack_rows([Mo[n] for n in _SMALL]),
                               _pack_rows([Vo[n] for n in _SMALL]), name="adamw_small")
    d_small, m_small, v_small = (_unpack_rows(t, small_shapes) for t in (sd, smm, svv))
    d_meta, m_meta, v_meta = _adamw_flat(meta_tokens, g_meta, m_meta_tokens, v_meta_tokens, name="adamw_meta")

    grads, deltas, new_m, new_v = {}, {}, {}, {}
    for n in _BIG:
        grads[n], deltas[n], new_m[n], new_v[n] = stacks[n]
    for i, n in enumerate(_SMALL):
        grads[n], deltas[n], new_m[n], new_v[n] = g_small[i], d_small[i], m_small[i], v_small[i]
    grads['meta_tokens'], deltas['meta_tokens'], new_m['meta_tokens'], new_v['meta_tokens'] = g_meta, d_meta, m_meta, v_meta
    return (loss.reshape(()), grad_x, *[grads[n] for n in _WEIGHTS], *[deltas[n] for n in _WEIGHTS],
            *[new_m[n] for n in _WEIGHTS], *[new_v[n] for n in _WEIGHTS])
```

```python
import numpy as np
import jax
import jax.numpy as jnp
from jax import lax
from jax.experimental import pallas as pl
from jax.experimental.pallas import tpu as pltpu

F32 = jnp.float32
BF16 = jnp.bfloat16

HEAD_DIM = 64
N_META = 16
BLOCK = 128
WINDOW = 128
PAD = BLOCK - N_META
EPS = 1e-6
NEG_INF = -1e30
SWA_GROUP = 8
NCH = 4
LANES = 128
QBLOCK = 512

ADAM_LR = 0.001
ADAM_B1 = 0.9
ADAM_B2 = 0.999
ADAM_EPS = 1e-08
ADAM_WD = 0.01
ADAM_STEP = 10

V7X_VMEM_BYTES = 64 * 1024 * 1024
VMEM_LIMIT = V7X_VMEM_BYTES - 8 * 1024 * 1024
MESH = pl.DeviceIdType.MESH
HIGHEST = lax.Precision.HIGHEST

_TM = (1088, 1024, 704, 512, 384, 256, 128)
_TN = (1408, 1024, 768, 512, 384, 256, 128)
_TK = (2176, 1408, 1024, 512, 384, 256, 128)
_TR = (544, 512, 384, 272, 256, 128)


def _pick(n, cands):
    for c in cands:
        if n % c == 0:
            return c
    return n


def _cparams(*sem):
    return pltpu.CompilerParams(dimension_semantics=sem if sem else None, vmem_limit_bytes=VMEM_LIMIT)


def _matmul(a, b, *, name, nt=False, b_chunked=False, out_chunked=False, out_dtype=F32,
            residual=None, scale=1.0, extras=(), epilogue=None, out_dtypes=None, comm=None):
    M, K = a.shape
    if not nt:
        N = b.shape[-1] * (NCH if b_chunked else 1)
        assert b.shape[-2] == K
        k_unit = K
    else:
        N = b.shape[-2]
        k_unit = b.shape[-1]
        assert k_unit * (NCH if b_chunked else 1) == K
    n_unit = N // NCH if (out_chunked or (b_chunked and not nt)) else N
    tm, tn, tk = _pick(M, _TM), _pick(n_unit, _TN), _pick(k_unit, _TK)
    if epilogue is None:
        extras = () if residual is None else (residual,)
        out_dtypes = (out_dtype,)

        def epilogue(acc, *res):
            r = acc * scale if scale != 1.0 else acc
            return (r + res[0] if res else r,)
    n_out = len(out_dtypes)
    n_temps = 4 if n_out > 1 else 0
    assert not (extras and out_chunked)

    def est(tm_):
        return (2 * tm_ * tk * 2 + 2 * tk * tn * 2 + tm_ * tn * 4
                + sum(2 * tm_ * tn * jnp.dtype(d).itemsize for d in out_dtypes)
                + sum(2 * tm_ * tn * e.dtype.itemsize for e in extras) + n_temps * tm_ * tn * 4)

    while est(tm) > VMEM_LIMIT * 3 // 4 and tm % 32 == 0:
        tm //= 2
    npc, kpc = n_unit // tn, k_unit // tk
    nk = K // tk
    grid = (M // tm, N // tn, nk)

    a_spec = pl.BlockSpec((tm, tk), lambda i, j, k: (i, k))
    if not nt:
        if b_chunked:
            b_spec = pl.BlockSpec((None, tk, tn), lambda i, j, k: (j // npc, k, j % npc))
        else:
            b_spec = pl.BlockSpec((tk, tn), lambda i, j, k: (k, j))
        dims = (((1,), (0,)), ((), ()))
    else:
        if b_chunked:
            b_spec = pl.BlockSpec((None, tn, tk), lambda i, j, k: (k // kpc, j, k % kpc))
        else:
            b_spec = pl.BlockSpec((tn, tk), lambda i, j, k: (j, k))
        dims = (((1,), (1,)), ((), ()))
    tile = pl.BlockSpec((tm, tn), lambda i, j, k: (i, j))
    if out_chunked:
        o_spec = pl.BlockSpec((None, tm, tn), lambda i, j, k: (j // npc, i, j % npc))
        out_shapes = [jax.ShapeDtypeStruct((NCH, M, n_unit), d) for d in out_dtypes]
    else:
        o_spec = tile
        out_shapes = [jax.ShapeDtypeStruct((M, N), d) for d in out_dtypes]
    in_specs = [a_spec, b_spec] + [tile] * len(extras)
    args = [a, b, *extras]

    n_main = len(args)
    n_cin = 0 if comm is None else len(comm.operands)
    n_cout = 0 if comm is None else len(comm.out_shapes)

    def body(*refs):
        a_ref, b_ref = refs[0], refs[1]
        e_refs = refs[2:n_main]
        o_refs = refs[n_main + n_cin:n_main + n_cin + n_out]
        acc_ref = refs[n_main + n_cin + n_out + n_cout]
        i, j, k = pl.program_id(0), pl.program_id(1), pl.program_id(2)
        if comm is not None:
            cin = refs[n_main:n_main + n_cin]
            cout = refs[n_main + n_cin + n_out:n_main + n_cin + n_out + n_cout]
            send_sems, recv_sems = refs[-2:]

            @pl.when((i == 0) & (j == 0) & (k == 0))
            def _():
                for cp, _ in comm.copies(cin, cout, send_sems, recv_sems):
                    cp.start()

        @pl.when(k == 0)
        def _():
            acc_ref[...] = jnp.zeros_like(acc_ref)

        acc_ref[...] += lax.dot_general(a_ref[...], b_ref[...], dims, preferred_element_type=F32)

        @pl.when(k == nk - 1)
        def _():
            for o_ref, val in zip(o_refs, epilogue(acc_ref[...], *[e[...] for e in e_refs])):
                o_ref[...] = val.astype(o_ref.dtype)

        if comm is not None:
            @pl.when((i == grid[0] - 1) & (j == grid[1] - 1) & (k == nk - 1))
            def _():
                cps = comm.copies(cin, cout, send_sems, recv_sems)
                for _, arrival in cps:
                    arrival.wait_recv()
                for cp, _ in cps:
                    cp.wait_send()

    scratch = [pltpu.VMEM((tm, tn), F32)]
    if comm is None:
        outs = pl.pallas_call(
            body, name=name, grid=grid, in_specs=in_specs, out_specs=[o_spec] * n_out, out_shape=out_shapes,
            scratch_shapes=scratch, compiler_params=_cparams("parallel", "parallel", "arbitrary"),
        )(*args)
    else:
        scratch += [pltpu.SemaphoreType.DMA((comm.n_sems,)), pltpu.SemaphoreType.DMA((comm.n_sems,))]
        outs = pl.pallas_call(
            body, name=name, grid=grid, in_specs=in_specs + [_ANY] * n_cin,
            out_specs=[o_spec] * n_out + [_ANY] * n_cout, out_shape=out_shapes + list(comm.out_shapes),
            input_output_aliases={n_main + s: n_out + d for s, d in comm.aliases.items()},
            scratch_shapes=scratch, compiler_params=_cparams("arbitrary", "arbitrary", "arbitrary"),
        )(*args, *comm.operands)
    return outs[0] if len(outs) == 1 else tuple(outs)


def _call_carrying(body, comm, args, *, name, grid, in_specs, out_specs, out_shape, scratch_shapes):
    if comm is None:
        return pl.pallas_call(body, name=name, grid=grid, in_specs=in_specs, out_specs=out_specs, out_shape=out_shape,
                              scratch_shapes=scratch_shapes, compiler_params=_cparams("arbitrary"))(*args)
    n_in, n_out, n_scr = len(in_specs), len(out_specs), len(scratch_shapes)
    n_cin, n_cout = len(comm.operands), len(comm.out_shapes)

    def carrying(*refs):
        ins, cin = refs[:n_in], refs[n_in:n_in + n_cin]
        outs = refs[n_in + n_cin:n_in + n_cin + n_out]
        cout = refs[n_in + n_cin + n_out:n_in + n_cin + n_out + n_cout]
        scr = refs[n_in + n_cin + n_out + n_cout:n_in + n_cin + n_out + n_cout + n_scr]
        send_sems, recv_sems = refs[-2:]
        step = pl.program_id(0)

        @pl.when(step == 0)
        def _():
            for cp, _ in comm.copies(cin, cout, send_sems, recv_sems):
                cp.start()

        body(*ins, *outs, *scr)

        @pl.when(step == grid[0] - 1)
        def _():
            cps = comm.copies(cin, cout, send_sems, recv_sems)
            for _, arrival in cps:
                arrival.wait_recv()
            for cp, _ in cps:
                cp.wait_send()

    return pl.pallas_call(
        carrying, name=name, grid=grid, in_specs=list(in_specs) + [_ANY] * n_cin,
        out_specs=list(out_specs) + [_ANY] * n_cout, out_shape=list(out_shape) + list(comm.out_shapes),
        input_output_aliases={n_in + s: n_out + d for s, d in comm.aliases.items()},
        scratch_shapes=list(scratch_shapes) + [pltpu.SemaphoreType.DMA((comm.n_sems,)), pltpu.SemaphoreType.DMA((comm.n_sems,))],
        compiler_params=_cparams("arbitrary"),
    )(*args, *comm.operands)


def _carried(queue, fn, n_out, *args, **kw):
    job = queue.take() if queue is not None else None
    outs = fn(*args, comm=job, **kw)
    if job is not None:
        queue.give(outs[n_out:])
    return outs[:n_out]


def _transpose(x, *, name):
    M, N = x.shape
    tc = _pick(N, (512, 384, 256, 128))

    def body(x_ref, o_ref):
        o_ref[...] = x_ref[...].astype(F32).T.astype(o_ref.dtype)

    return pl.pallas_call(
        body, name=name, grid=(N // tc,),
        in_specs=[pl.BlockSpec((M, tc), lambda j: (0, j))],
        out_specs=pl.BlockSpec((tc, M), lambda j: (j, 0)),
        out_shape=jax.ShapeDtypeStruct((N, M), x.dtype),
        compiler_params=_cparams("parallel"),
    )(x)


def _rms_fwd(h, g, *, name):
    L, D = h.shape
    tr = _pick(L, _TR)

    def body(h_ref, g_ref, o_ref):
        x = h_ref[...]
        r = lax.rsqrt(jnp.mean(x * x, axis=-1, keepdims=True) + EPS)
        o_ref[...] = (x * r * g_ref[...]).astype(o_ref.dtype)

    return pl.pallas_call(
        body, name=name, grid=(L // tr,),
        in_specs=[pl.BlockSpec((tr, D), lambda i: (i, 0)), pl.BlockSpec((1, D), lambda i: (0, 0))],
        out_specs=pl.BlockSpec((tr, D), lambda i: (i, 0)),
        out_shape=jax.ShapeDtypeStruct((L, D), BF16),
        compiler_params=_cparams("parallel"),
    )(h, g)


def _rms_bwd(dy, h, g, dh, *, name):
    L, D = h.shape
    tr = _pick(L, _TR)

    def body(dy_ref, h_ref, g_ref, dh_ref, o_ref, dg_ref):
        i = pl.program_id(0)
        x = h_ref[...]
        dyv = dy_ref[...]
        r = lax.rsqrt(jnp.mean(x * x, axis=-1, keepdims=True) + EPS)
        xh = x * r
        dxh = dyv * g_ref[...]
        dx = r * (dxh - xh * jnp.mean(dxh * xh, axis=-1, keepdims=True))
        o_ref[...] = dh_ref[...] + dx
        part = jnp.sum(dyv * xh, axis=0, keepdims=True)

        @pl.when(i == 0)
        def _():
            dg_ref[...] = part

        @pl.when(i > 0)
        def _():
            dg_ref[...] += part

    row = pl.BlockSpec((tr, D), lambda i: (i, 0))
    vec = pl.BlockSpec((1, D), lambda i: (0, 0))
    return pl.pallas_call(
        body, name=name, grid=(L // tr,),
        in_specs=[row, row, vec, row], out_specs=[row, vec],
        out_shape=[jax.ShapeDtypeStruct((L, D), F32), jax.ShapeDtypeStruct((1, D), F32)],
        compiler_params=_cparams("arbitrary"),
    )(dy, h, g, dh)


def _swiglu_epilogue(up, gate):
    g = gate.astype(F32)
    return up, g * jax.nn.sigmoid(g) * up


def _swiglu_bwd_epilogue(acc, gate, up):
    d = 0.5 * acc
    g = gate.astype(F32)
    sg = jax.nn.sigmoid(g)
    return d * up.astype(F32) * sg * (1.0 + g * (1.0 - sg)), d * g * sg


def _cast_bf16(x, *, name):
    L, D = x.shape
    tr = _pick(L, _TR)

    def body(x_ref, o_ref):
        o_ref[...] = x_ref[...].astype(o_ref.dtype)

    blk = pl.BlockSpec((tr, D), lambda i: (i, 0))
    return pl.pallas_call(
        body, name=name, grid=(L // tr,), in_specs=[blk], out_specs=blk,
        out_shape=jax.ShapeDtypeStruct((L, D), BF16), compiler_params=_cparams("parallel"),
    )(x)


def _loss_grad(h, target, *, name):
    L, D = h.shape
    S = target.shape[0]
    nb = L // BLOCK

    def body(h_ref, t_ref, loss_ref, dh_ref):
        i = pl.program_id(0)

        @pl.when(i == 0)
        def _():
            loss_ref[...] = jnp.zeros_like(loss_ref)
            dh_ref[...] = jnp.zeros_like(dh_ref)

        @pl.when(i > 0)
        def _():
            err = h_ref[...] - t_ref[...]
            dh_ref[...] = err * (1.0 / D)
            loss_ref[...] += jnp.full(loss_ref.shape, (0.5 / D) * jnp.sum(err * err), F32)

    return pl.pallas_call(
        body, name=name, grid=(nb,),
        in_specs=[pl.BlockSpec((BLOCK, D), lambda i: (i, 0)),
                  pl.BlockSpec((BLOCK, D), lambda i: (jnp.maximum(i - 1, 0), 0))],
        out_specs=[pl.BlockSpec((1, LANES), lambda i: (0, 0)), pl.BlockSpec((BLOCK, D), lambda i: (i, 0))],
        out_shape=[jax.ShapeDtypeStruct((1, LANES), F32), jax.ShapeDtypeStruct((L, D), F32)],
        compiler_params=_cparams("arbitrary"),
    )(h, target)


def _ffn_fwd(h, g, w, tag, queue=None):
    hn = _rms_fwd(h, g, name=f"{tag}_rms")
    gate = _mm(queue, hn, w(0), name=f"{tag}_gate", b_chunked=True, out_dtype=BF16)
    up, act = _mm(queue, hn, w(1), name=f"{tag}_up", b_chunked=True, extras=(gate,), epilogue=_swiglu_epilogue,
                  out_dtypes=(BF16, BF16))
    h_out = _mm(queue, act, w(2), name=f"{tag}_down", residual=h, scale=0.5)
    return h_out, (h, hn, gate, up, act)


def _ffn_bwd(dh, saved, g, wg, wu, wd, tag, queue=None):
    h, hn, gate, up, act = saved
    dout = _cast_bf16(dh, name=f"{tag}_dout")
    dgate, dup = _mm(queue, dout, wd, name=f"{tag}_dact", nt=True, extras=(gate, up), epilogue=_swiglu_bwd_epilogue,
                     out_dtypes=(BF16, BF16))
    actT = _transpose(act, name=f"{tag}_actT")
    dwd = _mm(queue, actT, dout, name=f"{tag}_dwd", out_dtype=BF16, scale=0.5)
    hnT = _transpose(hn, name=f"{tag}_hnT")
    dwg = _mm(queue, hnT, dgate, name=f"{tag}_dwg", out_chunked=True, out_dtype=BF16)
    dwu = _mm(queue, hnT, dup, name=f"{tag}_dwu", out_chunked=True, out_dtype=BF16)
    dhn = _mm(queue, dgate, wg, name=f"{tag}_dhn_g", nt=True, b_chunked=True)
    dhn = _mm(queue, dup, wu, name=f"{tag}_dhn_u", nt=True, b_chunked=True, residual=dhn)
    dh_in, dg = _rms_bwd(dhn, h, g, dh, name=f"{tag}_drms")
    return dh_in, dg, dwg, dwu, dwd


class _MixDims:
    def __init__(self, d_model):
        self.wf = d_model // 2
        self.ws = d_model // 2
        self.pf = self.wf // LANES
        self.ps = self.ws // LANES
        self.hf = self.wf // HEAD_DIM
        self.hq = self.ws // HEAD_DIM
        self.nkv = max(1, self.hq // SWA_GROUP)
        self.g = self.hq // self.nkv
        self.bq_f, self.bk_f, self.bv_f = 0, self.pf, 2 * self.pf
        self.bq_s = 3 * self.pf
        self.bk_s = self.bq_s + self.ps
        self.bv_s = self.bk_s + self.nkv
        self.bz = self.bv_s + self.nkv
        self.nu = (self.bz + 1) * LANES
        self.nup = -(-self.nu // 512) * 512
        self.in_width = 3 * self.wf + self.hf + self.ws + 2 * self.nkv * HEAD_DIM
        assert self.hf <= 2 * (LANES // 8)

    def gate_lane(self, h):
        return 8 * (h // 2) + h % 2

    def column_map(self):
        wf, ws, hd = self.wf, self.ws, HEAD_DIM
        src = np.full((self.nup,), -1, np.int64)
        src[0:3 * wf] = np.arange(3 * wf)
        o_sq = 3 * wf + self.hf
        src[self.bq_s * LANES:self.bq_s * LANES + ws] = o_sq + np.arange(ws)
        o_sk = o_sq + ws
        o_sv = o_sk + self.nkv * hd
        for kv in range(self.nkv):
            for rep in range(2):
                c0 = (self.bk_s + kv) * LANES + rep * hd
                src[c0:c0 + hd] = o_sk + kv * hd + np.arange(hd)
                c0 = (self.bv_s + kv) * LANES + rep * hd
                src[c0:c0 + hd] = o_sv + kv * hd + np.arange(hd)
        for h in range(self.hf):
            src[self.bz * LANES + self.gate_lane(h)] = 3 * wf + h
        return src

    def grad_column_map(self):
        src = self.column_map()
        dst = np.zeros((self.in_width,), np.int64)
        for col in range(self.nup - 1, -1, -1):
            if src[col] >= 0:
                dst[src[col]] = col
        return dst


def _block_diag_mean():
    m = np.zeros((LANES, LANES), np.float32)
    m[:HEAD_DIM, :HEAD_DIM] = 1.0 / HEAD_DIM
    m[HEAD_DIM:, HEAD_DIM:] = 1.0 / HEAD_DIM
    return jnp.asarray(m)


def _fold_halves():
    m = np.eye(LANES, dtype=np.float32)
    m[np.arange(LANES), (np.arange(LANES) + HEAD_DIM) % LANES] = 1.0
    return jnp.asarray(m)


def _gate_expand(md):
    e = np.zeros((LANES, md.wf), np.float32)
    for h in range(md.hf):
        e[md.gate_lane(h), h * HEAD_DIM:(h + 1) * HEAD_DIM] = 1.0
    return jnp.asarray(e)


def _qblocks(L, qb):
    blocks = [(0, BLOCK)]
    r = BLOCK
    while r < L:
        blocks.append((r, qb))
        r += qb
    assert r == L
    return blocks


def _f32dot(a, b):
    return jnp.dot(a, b, precision=HIGHEST, preferred_element_type=F32)


_DIMS_NT = (((1,), (1,)), ((), ()))


def _dot_nt(a, b):
    return lax.dot_general(a, b, _DIMS_NT, preferred_element_type=F32)


def _dot_tn(a, b):
    return jnp.dot(a.T.astype(BF16), b, preferred_element_type=F32)


def _log_sigmoid(z):
    return jnp.minimum(z, 0.0) - jnp.log(1.0 + jnp.exp(-jnp.abs(z)))


def _gate_fwd(u, b, md, *, name):
    L = u.shape[0]
    nb = L // BLOCK
    expand = _gate_expand(md)

    def body(z_ref, b_ref, e_ref, cexp_ref, ct_ref, c_s):
        ri = lax.broadcasted_iota(jnp.int32, (BLOCK, BLOCK), 0)
        ci = lax.broadcasted_iota(jnp.int32, (BLOCK, BLOCK), 1)
        tri = (ri >= ci).astype(F32)
        carry = jnp.zeros((1, LANES), F32)
        for bi in range(nb):
            rows = pl.ds(bi * BLOCK, BLOCK)
            logf = _log_sigmoid(z_ref[rows, :] + b_ref[...])
            blk = _f32dot(tri, logf) + carry
            c_s[rows, :] = blk
            carry = blk[BLOCK - 1:BLOCK, :]
        c = c_s[...]
        ct_ref[...] = c.T
        cexp_ref[...] = _f32dot(c, e_ref[...])

    return pl.pallas_call(
        body, name=name, grid=(1,),
        in_specs=[pl.BlockSpec((L, LANES), lambda i: (0, md.bz)), pl.BlockSpec((1, LANES), lambda i: (0, 0)),
                  pl.BlockSpec((LANES, md.wf), lambda i: (0, 0))],
        out_specs=[pl.BlockSpec((L, md.wf), lambda i: (0, 0)), pl.BlockSpec((LANES, L), lambda i: (0, 0))],
        out_shape=[jax.ShapeDtypeStruct((L, md.wf), F32), jax.ShapeDtypeStruct((LANES, L), F32)],
        scratch_shapes=[pltpu.VMEM((L, LANES), F32)],
        compiler_params=_cparams("arbitrary"),
    )(u, b, expand)


def _gate_bwd(u, b, dck_t, md, *, name):
    L = u.shape[0]
    nb = L // BLOCK

    def body(z_ref, b_ref, dck_ref, dz_ref, db_ref, dc_s):
        ri = lax.broadcasted_iota(jnp.int32, (BLOCK, BLOCK), 0)
        ci = lax.broadcasted_iota(jnp.int32, (BLOCK, BLOCK), 1)
        triu = (ri <= ci).astype(F32)
        dc_s[...] = -dck_ref[...].T
        carry = jnp.zeros((1, LANES), F32)
        db = jnp.zeros((1, LANES), F32)
        for bi in range(nb - 1, -1, -1):
            rows = pl.ds(bi * BLOCK, BLOCK)
            blk = _f32dot(triu, dc_s[rows, :]) + carry
            carry = blk[0:1, :]
            z = z_ref[rows, :] + b_ref[...]
            dz = blk * jax.nn.sigmoid(-z)
            if bi == 0:
                dz = jnp.where(lax.broadcasted_iota(jnp.int32, (BLOCK, LANES), 0) >= PAD, dz, 0.0)
            dz_ref[rows, :] = dz
            db = db + jnp.sum(dz, axis=0, keepdims=True)
        db_ref[...] = db

    return pl.pallas_call(
        body, name=name, grid=(1,),
        in_specs=[pl.BlockSpec((L, LANES), lambda i: (0, md.bz)), pl.BlockSpec((1, LANES), lambda i: (0, 0)),
                  pl.BlockSpec((LANES, L), lambda i: (0, 0))],
        out_specs=[pl.BlockSpec((L, LANES), lambda i: (0, 0)), pl.BlockSpec((1, LANES), lambda i: (0, 0))],
        out_shape=[jax.ShapeDtypeStruct((L, LANES), F32), jax.ShapeDtypeStruct((1, LANES), F32)],
        scratch_shapes=[pltpu.VMEM((L, LANES), F32)],
        compiler_params=_cparams("arbitrary"),
    )(u, b, dck_t)


def _head_norm(x, g, bd):
    r = lax.rsqrt(_f32dot(x * x, bd) + EPS)
    xh = x * r
    return xh * g, xh, r


def _head_norm_bwd(dy, xh, r, g, bd):
    dxh = dy * g
    dx = r * (dxh - xh * _f32dot(dxh * xh, bd))
    return dx, jnp.sum(dy * xh, axis=0, keepdims=True)


def _lane_half():
    return lax.broadcasted_iota(jnp.int32, (1, LANES), 1) < HEAD_DIM


def _store_head_pair(x, half, a_s, b_s):
    a_s[...] = jnp.where(half, x, 0.0).astype(BF16)
    b_s[...] = jnp.where(half, 0.0, x).astype(BF16)


def _fox_scores(qm, kn_s, cexp_ref, ct_ref, r0, nr, klen, hh):
    s = _dot_nt(qm, kn_s[0:klen, :]) * (HEAD_DIM ** -0.5)
    s = s + cexp_ref[r0:r0 + nr, HEAD_DIM * hh:HEAD_DIM * hh + 1] - ct_ref[hh:hh + 1, 0:klen]
    qp = r0 + lax.broadcasted_iota(jnp.int32, (nr, klen), 0)
    kp = lax.broadcasted_iota(jnp.int32, (nr, klen), 1)
    return jnp.where((kp <= qp) & (kp >= PAD), s, NEG_INF)


def _fox_fwd(u, cexp, ct3, qg, kg, md, *, name, comm=None):
    L = u.shape[0]
    blocks = _qblocks(L, QBLOCK)
    bd = _block_diag_mean()

    def body(q_ref, k_ref, v_ref, cexp_ref, ct_ref, qg_ref, kg_ref, bd_ref, o_ref, lse_ref, qa_s, qb_s, kn_s, v_s):
        half = _lane_half()
        bdv = bd_ref[...]
        _store_head_pair(_head_norm(q_ref[...], qg_ref[...], bdv)[0], half, qa_s, qb_s)
        kn_s[...] = _head_norm(k_ref[...], kg_ref[...], bdv)[0].astype(BF16)
        v_s[...] = v_ref[...].astype(BF16)
        for r0, nr in blocks:
            klen = r0 + nr
            o_blk = lse_blk = None
            for hh, q_s in enumerate((qa_s, qb_s)):
                s = _fox_scores(q_s[r0:r0 + nr, :], kn_s, cexp_ref, ct_ref, r0, nr, klen, hh)
                m = jnp.max(s, axis=-1, keepdims=True)
                p = jnp.exp(s - m)
                l = jnp.sum(p, axis=-1, keepdims=True)
                oh = jnp.dot(p.astype(BF16), v_s[0:klen, :], preferred_element_type=F32) * (1.0 / l)
                lh = jnp.broadcast_to(jnp.where(m > 0.5 * NEG_INF, m + jnp.log(l), 0.0), (nr, LANES))
                o_blk = oh if hh == 0 else jnp.where(half, o_blk, oh)
                lse_blk = lh if hh == 0 else jnp.where(half, lse_blk, lh)
            o_ref[r0:r0 + nr, :] = o_blk
            lse_ref[r0:r0 + nr, :] = lse_blk

    col = lambda base: pl.BlockSpec((L, LANES), lambda j: (0, base + j))
    vec = pl.BlockSpec((1, LANES), lambda j: (0, 0))
    return _call_carrying(
        body, comm, (u, u, u, cexp, ct3, qg, kg, bd), name=name, grid=(md.pf,),
        in_specs=[col(md.bq_f), col(md.bk_f), col(md.bv_f), col(0),
                  pl.BlockSpec((None, 8, L), lambda j: (j, 0, 0)), vec, vec,
                  pl.BlockSpec((LANES, LANES), lambda j: (0, 0))],
        out_specs=[col(0), col(0)],
        out_shape=[jax.ShapeDtypeStruct((L, md.wf), F32)] * 2,
        scratch_shapes=[pltpu.VMEM((L, LANES), BF16)] * 4)


def _softmax_bwd(p, dp):
    pdp = p * dp
    return pdp - p * jnp.sum(pdp, axis=-1, keepdims=True)


def _fox_bwd(u, cexp, ct3, qg, kg, do, lse, md, *, name, comm=None):
    L = u.shape[0]
    blocks = _qblocks(L, QBLOCK // 2)
    bd = _block_diag_mean()
    fold = _fold_halves()
    npairs = md.pf

    def body(q_ref, k_ref, v_ref, cexp_ref, ct_ref, qg_ref, kg_ref, bd_ref, fold_ref, do_ref, lse_ref,
             dq_ref, dk_ref, dv_ref, dck_ref, dqg_ref, dkg_ref,
             qa_s, qb_s, kn_s, v_s, doa_s, dob_s, dqn_s, dkn_s, dvv_s):
        j = pl.program_id(0)
        half = _lane_half()
        bdv = bd_ref[...]
        qy, qh, rq = _head_norm(q_ref[...], qg_ref[...], bdv)
        ky, kh, rk = _head_norm(k_ref[...], kg_ref[...], bdv)
        _store_head_pair(qy, half, qa_s, qb_s)
        _store_head_pair(do_ref[...], half, doa_s, dob_s)
        kn_s[...] = ky.astype(BF16)
        v_s[...] = v_ref[...].astype(BF16)
        dkn_s[...] = jnp.zeros_like(dkn_s)
        dvv_s[...] = jnp.zeros_like(dvv_s)
        dck_ref[...] = jnp.zeros_like(dck_ref)
        for r0, nr in blocks:
            klen = r0 + nr
            dq_blk = None
            for hh, (q_s, do_s) in enumerate(((qa_s, doa_s), (qb_s, dob_s))):
                c0 = HEAD_DIM * hh
                qm = q_s[r0:r0 + nr, :]
                dom = do_s[r0:r0 + nr, :]
                s = _fox_scores(qm, kn_s, cexp_ref, ct_ref, r0, nr, klen, hh)
                p = jnp.exp(s - lse_ref[r0:r0 + nr, c0:c0 + 1])
                ds = _softmax_bwd(p, _dot_nt(dom, v_s[0:klen, :]))
                dck_ref[hh:hh + 1, 0:klen] += jnp.sum(ds, axis=0, keepdims=True)
                ds = ds * (HEAD_DIM ** -0.5)
                dq_h = jnp.dot(ds.astype(BF16), kn_s[0:klen, :], preferred_element_type=F32)
                dkn_s[0:klen, :] += _dot_tn(ds, qm)
                dvv_s[0:klen, :] += _dot_tn(p, dom)
                dq_blk = dq_h if hh == 0 else jnp.where(half, dq_blk, dq_h)
            dqn_s[r0:r0 + nr, :] = dq_blk
        dq, dqg = _head_norm_bwd(dqn_s[...], qh, rq, qg_ref[...], bdv)
        dk, dkg = _head_norm_bwd(dkn_s[...], kh, rk, kg_ref[...], bdv)
        dq_ref[...] = dq
        dk_ref[...] = dk
        dv_ref[...] = dvv_s[...]

        @pl.when(j == 0)
        def _():
            dqg_ref[...] = jnp.zeros_like(dqg_ref)
            dkg_ref[...] = jnp.zeros_like(dkg_ref)

        dqg_ref[...] += dqg
        dkg_ref[...] += dkg

        @pl.when(j == npairs - 1)
        def _():
            dqg_ref[...] = _f32dot(jnp.broadcast_to(dqg_ref[...], (8, LANES)), fold_ref[...])[0:1, :]
            dkg_ref[...] = _f32dot(jnp.broadcast_to(dkg_ref[...], (8, LANES)), fold_ref[...])[0:1, :]

    col = lambda base: pl.BlockSpec((L, LANES), lambda j: (0, base + j))
    vec = pl.BlockSpec((1, LANES), lambda j: (0, 0))
    sq = pl.BlockSpec((LANES, LANES), lambda j: (0, 0))
    ct_spec = pl.BlockSpec((None, 8, L), lambda j: (j, 0, 0))
    big = jax.ShapeDtypeStruct((L, md.wf), F32)
    small = jax.ShapeDtypeStruct((1, LANES), F32)
    return _call_carrying(
        body, comm, (u, u, u, cexp, ct3, qg, kg, bd, fold, do, lse), name=name, grid=(md.pf,),
        in_specs=[col(md.bq_f), col(md.bk_f), col(md.bv_f), col(0), ct_spec, vec, vec, sq, sq, col(0), col(0)],
        out_specs=[col(0), col(0), col(0), ct_spec, vec, vec],
        out_shape=[big, big, big, jax.ShapeDtypeStruct((md.pf, 8, L), F32), small, small],
        scratch_shapes=[pltpu.VMEM((L, LANES), BF16)] * 6 + [pltpu.VMEM((L, LANES), F32)] * 3)


def _swa_scores(qm, kn_s, slope, k0, r0, nr, klen):
    s = _dot_nt(qm, kn_s[k0:k0 + klen, :]) * (HEAD_DIM ** -0.5)
    qp = r0 + lax.broadcasted_iota(jnp.int32, (nr, klen), 0)
    kp = k0 + lax.broadcasted_iota(jnp.int32, (nr, klen), 1)
    dist = qp - kp
    s = s - slope * dist.astype(F32)
    return jnp.where((dist >= 0) & (dist < WINDOW) & (kp >= PAD), s, NEG_INF)


def _swa_fwd(u, sinkp, slopep, qg, kg, md, *, name, comm=None):
    L = u.shape[0]
    blocks = _qblocks(L, QBLOCK)
    bd = _block_diag_mean()

    def body(q_ref, k_ref, v_ref, sink_ref, slope_ref, qg_ref, kg_ref, bd_ref, o_ref, lse_ref, qa_s, qb_s, kn_s, v_s):
        half = _lane_half()
        bdv = bd_ref[...]
        _store_head_pair(_head_norm(q_ref[...], qg_ref[...], bdv)[0], half, qa_s, qb_s)
        kn_s[...] = _head_norm(k_ref[...], kg_ref[...], bdv)[0].astype(BF16)
        v_s[...] = v_ref[...].astype(BF16)
        for r0, nr in blocks:
            k0 = max(r0 - BLOCK, 0)
            klen = r0 + nr - k0
            o_blk = lse_blk = None
            for hh, q_s in enumerate((qa_s, qb_s)):
                c0 = HEAD_DIM * hh
                s = _swa_scores(q_s[r0:r0 + nr, :], kn_s, slope_ref[0:1, c0:c0 + 1], k0, r0, nr, klen)
                sink = sink_ref[0:1, c0:c0 + 1]
                m = jnp.maximum(jnp.max(s, axis=-1, keepdims=True), sink)
                p = jnp.exp(s - m)
                den = jnp.sum(p, axis=-1, keepdims=True) + jnp.exp(sink - m)
                oh = jnp.dot(p.astype(BF16), v_s[k0:k0 + klen, :], preferred_element_type=F32) * (1.0 / den)
                lh = jnp.broadcast_to(m + jnp.log(den), (nr, LANES))
                o_blk = oh if hh == 0 else jnp.where(half, o_blk, oh)
                lse_blk = lh if hh == 0 else jnp.where(half, lse_blk, lh)
            o_ref[r0:r0 + nr, :] = o_blk
            lse_ref[r0:r0 + nr, :] = lse_blk

    g2 = md.g // 2
    qcol = pl.BlockSpec((L, LANES), lambda j: (0, md.bq_s + j))
    kcol = pl.BlockSpec((L, LANES), lambda j: (0, md.bk_s + j // g2))
    vcol = pl.BlockSpec((L, LANES), lambda j: (0, md.bv_s + j // g2))
    ocol = pl.BlockSpec((L, LANES), lambda j: (0, j))
    pvec = pl.BlockSpec((1, LANES), lambda j: (0, j))
    vec = pl.BlockSpec((1, LANES), lambda j: (0, 0))
    return _call_carrying(
        body, comm, (u, u, u, sinkp, slopep, qg, kg, bd), name=name, grid=(md.ps,),
        in_specs=[qcol, kcol, vcol, pvec, pvec, vec, vec, pl.BlockSpec((LANES, LANES), lambda j: (0, 0))],
        out_specs=[ocol, ocol],
        out_shape=[jax.ShapeDtypeStruct((L, md.ws), F32)] * 2,
        scratch_shapes=[pltpu.VMEM((L, LANES), BF16)] * 4)


def _swa_bwd(u, sinkp, slopep, qg, kg, do, lse, md, *, name, comm=None):
    L = u.shape[0]
    blocks = _qblocks(L, QBLOCK // 2)
    bd = _block_diag_mean()
    fold = _fold_halves()
    g2 = md.g // 2
    npairs = md.ps

    def body(q_ref, k_ref, v_ref, sink_ref, slope_ref, qg_ref, kg_ref, bd_ref, fold_ref, do_ref, lse_ref,
             dq_ref, dk_ref, dv_ref, dsink_ref, dqg_ref, dkg_ref,
             qa_s, qb_s, kn_s, v_s, doa_s, dob_s, dqn_s):
        j = pl.program_id(0)
        half = _lane_half()
        bdv = bd_ref[...]
        qy, qh, rq = _head_norm(q_ref[...], qg_ref[...], bdv)
        ky, kh, rk = _head_norm(k_ref[...], kg_ref[...], bdv)
        _store_head_pair(qy, half, qa_s, qb_s)
        _store_head_pair(do_ref[...], half, doa_s, dob_s)
        kn_s[...] = ky.astype(BF16)
        v_s[...] = v_ref[...].astype(BF16)

        @pl.when(j % g2 == 0)
        def _():
            dk_ref[...] = jnp.zeros_like(dk_ref)
            dv_ref[...] = jnp.zeros_like(dv_ref)

        @pl.when(j == 0)
        def _():
            dqg_ref[...] = jnp.zeros_like(dqg_ref)
            dkg_ref[...] = jnp.zeros_like(dkg_ref)

        dsink = [jnp.zeros((1, 1), F32), jnp.zeros((1, 1), F32)]
        for r0, nr in blocks:
            k0 = max(r0 - BLOCK, 0)
            klen = r0 + nr - k0
            dq_blk = None
            for hh, (q_s, do_s) in enumerate(((qa_s, doa_s), (qb_s, dob_s))):
                c0 = HEAD_DIM * hh
                qm = q_s[r0:r0 + nr, :]
                dom = do_s[r0:r0 + nr, :]
                s = _swa_scores(qm, kn_s, slope_ref[0:1, c0:c0 + 1], k0, r0, nr, klen)
                lse_h = lse_ref[r0:r0 + nr, c0:c0 + 1]
                p = jnp.exp(s - lse_h)
                pdp = p * _dot_nt(dom, v_s[k0:k0 + klen, :])
                delta = jnp.sum(pdp, axis=-1, keepdims=True)
                p_sink = jnp.exp(sink_ref[0:1, c0:c0 + 1] - lse_h)
                dsink[hh] = dsink[hh] - jnp.sum(p_sink * delta, axis=0, keepdims=True)
                ds = (pdp - p * delta) * (HEAD_DIM ** -0.5)
                dq_h = jnp.dot(ds.astype(BF16), kn_s[k0:k0 + klen, :], preferred_element_type=F32)
                dk_ref[k0:k0 + klen, :] += _dot_tn(ds, qm)
                dv_ref[k0:k0 + klen, :] += _dot_tn(p, dom)
                dq_blk = dq_h if hh == 0 else jnp.where(half, dq_blk, dq_h)
            dqn_s[r0:r0 + nr, :] = dq_blk
        dq, dqg = _head_norm_bwd(dqn_s[...], qh, rq, qg_ref[...], bdv)
        dq_ref[...] = dq
        dqg_ref[...] += dqg
        dsink_ref[...] = jnp.where(half, jnp.broadcast_to(dsink[0], (1, LANES)), jnp.broadcast_to(dsink[1], (1, LANES)))

        @pl.when(j % g2 == g2 - 1)
        def _():
            dkn = _f32dot(dk_ref[...], fold_ref[...])
            dk, dkg = _head_norm_bwd(dkn, kh, rk, kg_ref[...], bdv)
            dk_ref[...] = jnp.where(half, dk, 0.0)
            dv_ref[...] = jnp.where(half, _f32dot(dv_ref[...], fold_ref[...]), 0.0)
            dkg_ref[...] += dkg

        @pl.when(j == npairs - 1)
        def _():
            dqg_ref[...] = _f32dot(jnp.broadcast_to(dqg_ref[...], (8, LANES)), fold_ref[...])[0:1, :]

    qcol = pl.BlockSpec((L, LANES), lambda j: (0, md.bq_s + j))
    kcol = pl.BlockSpec((L, LANES), lambda j: (0, md.bk_s + j // g2))
    vcol = pl.BlockSpec((L, LANES), lambda j: (0, md.bv_s + j // g2))
    ocol = pl.BlockSpec((L, LANES), lambda j: (0, j))
    kvout = pl.BlockSpec((L, LANES), lambda j: (0, j // g2))
    pvec = pl.BlockSpec((1, LANES), lambda j: (0, j))
    vec = pl.BlockSpec((1, LANES), lambda j: (0, 0))
    sq = pl.BlockSpec((LANES, LANES), lambda j: (0, 0))
    kvshape = jax.ShapeDtypeStruct((L, LANES * md.nkv), F32)
    small = jax.ShapeDtypeStruct((1, LANES), F32)
    return _call_carrying(
        body, comm, (u, u, u, sinkp, slopep, qg, kg, bd, fold, do, lse), name=name, grid=(md.ps,),
        in_specs=[qcol, kcol, vcol, pvec, pvec, vec, vec, sq, sq, ocol, ocol],
        out_specs=[ocol, kvout, kvout, pvec, vec, vec],
        out_shape=[jax.ShapeDtypeStruct((L, md.ws), F32), kvshape, kvshape,
                   jax.ShapeDtypeStruct((1, md.ws), F32), small, small],
        scratch_shapes=[pltpu.VMEM((L, LANES), BF16)] * 6 + [pltpu.VMEM((L, LANES), F32)])


def _outnorm_fwd(of, os_, gf, gs, *, name):
    L, wf = of.shape
    ws = os_.shape[1]
    tr = _pick(L, _TR)

    def body(of_ref, os_ref, gf_ref, gs_ref, o_ref):
        for src, g_ref, c0, w in ((of_ref, gf_ref, 0, wf), (os_ref, gs_ref, wf, ws)):
            x = src[...]
            r = lax.rsqrt(jnp.mean(x * x, axis=-1, keepdims=True) + EPS)
            o_ref[:, c0:c0 + w] = (x * r * g_ref[...]).astype(o_ref.dtype)

    return pl.pallas_call(
        body, name=name, grid=(L // tr,),
        in_specs=[pl.BlockSpec((tr, wf), lambda i: (i, 0)), pl.BlockSpec((tr, ws), lambda i: (i, 0)),
                  pl.BlockSpec((1, wf), lambda i: (0, 0)), pl.BlockSpec((1, ws), lambda i: (0, 0))],
        out_specs=pl.BlockSpec((tr, wf + ws), lambda i: (i, 0)),
        out_shape=jax.ShapeDtypeStruct((L, wf + ws), BF16),
        compiler_params=_cparams("parallel"),
    )(of, os_, gf, gs)


def _outnorm_bwd(don, of, os_, gf, gs, *, name):
    L, wf = of.shape
    ws = os_.shape[1]
    tr = _pick(L, _TR)

    def body(d_ref, of_ref, os_ref, gf_ref, gs_ref, dof_ref, dos_ref, dgf_ref, dgs_ref):
        i = pl.program_id(0)
        for src, g_ref, c0, w, dx_ref, dg_ref in ((of_ref, gf_ref, 0, wf, dof_ref, dgf_ref),
                                                  (os_ref, gs_ref, wf, ws, dos_ref, dgs_ref)):
            x = src[...]
            dy = d_ref[:, c0:c0 + w]
            r = lax.rsqrt(jnp.mean(x * x, axis=-1, keepdims=True) + EPS)
            xh = x * r
            dxh = dy * g_ref[...]
            dx_ref[...] = r * (dxh - xh * jnp.mean(dxh * xh, axis=-1, keepdims=True))
            part = jnp.sum(dy * xh, axis=0, keepdims=True)

            @pl.when(i == 0)
            def _():
                dg_ref[...] = part

            @pl.when(i > 0)
            def _():
                dg_ref[...] += part

    rf = pl.BlockSpec((tr, wf), lambda i: (i, 0))
    rs = pl.BlockSpec((tr, ws), lambda i: (i, 0))
    vf = pl.BlockSpec((1, wf), lambda i: (0, 0))
    vs = pl.BlockSpec((1, ws), lambda i: (0, 0))
    return pl.pallas_call(
        body, name=name, grid=(L // tr,),
        in_specs=[pl.BlockSpec((tr, wf + ws), lambda i: (i, 0)), rf, rs, vf, vs],
        out_specs=[rf, rs, vf, vs],
        out_shape=[jax.ShapeDtypeStruct((L, wf), F32), jax.ShapeDtypeStruct((L, ws), F32),
                   jax.ShapeDtypeStruct((1, wf), F32), jax.ShapeDtypeStruct((1, ws), F32)],
        compiler_params=_cparams("arbitrary"),
    )(don, of, os_, gf, gs)


def _win_to_mine(w, md):
    d = w.shape[0]
    wf, ws, hd = md.wf, md.ws, HEAD_DIM
    o_z = 3 * wf
    o_sq = o_z + md.hf
    o_sk = o_sq + ws
    o_sv = o_sk + md.nkv * hd
    parts = [w[:, :3 * wf], w[:, o_sq:o_sq + ws]]
    for base in (o_sk, o_sv):
        for kv in range(md.nkv):
            blk = w[:, base + kv * hd:base + (kv + 1) * hd]
            parts += [blk, blk]
    z = w[:, o_z:o_z + md.hf].reshape(d, md.hf // 2, 2)
    z = jnp.pad(z, ((0, 0), (0, 0), (0, 6))).reshape(d, 4 * md.hf)
    parts.append(jnp.pad(z, ((0, 0), (0, LANES - 4 * md.hf + md.nup - md.nu))))
    return jnp.concatenate(parts, axis=1)


def _win_grad_to_ref(dw, md):
    d = dw.shape[0]
    wf, ws, hd = md.wf, md.ws, HEAD_DIM
    z = dw[:, md.bz * LANES:md.bz * LANES + 4 * md.hf].reshape(d, md.hf // 2, 8)[:, :, :2].reshape(d, md.hf)
    parts = [dw[:, :3 * wf], z, dw[:, md.bq_s * LANES:md.bq_s * LANES + ws]]
    for base in (md.bk_s, md.bv_s):
        for kv in range(md.nkv):
            c0 = (base + kv) * LANES
            parts.append(dw[:, c0:c0 + hd])
    return jnp.concatenate(parts, axis=1)


def _mix_small(p, md):
    tile2 = lambda v: jnp.tile(v.reshape(1, HEAD_DIM), (1, 2))
    b = p["b_forget"].reshape(md.hf // 2, 2)
    b = jnp.pad(b, ((0, 0), (0, 6))).reshape(1, 4 * md.hf)
    slopes = np.asarray(2.0 ** (-8.0 * np.arange(1, md.hq + 1) / md.hq), np.float32)
    return dict(
        g_mix=p["mix_norm"].reshape(1, -1),
        b_gate=jnp.pad(b, ((0, 0), (0, LANES - 4 * md.hf))),
        fqg=tile2(p["fox_q_norm"]), fkg=tile2(p["fox_k_norm"]),
        sqg=tile2(p["swa_q_norm"]), skg=tile2(p["swa_k_norm"]),
        sinkp=jnp.repeat(p["swa_sinks"], HEAD_DIM).reshape(1, md.ws),
        slopep=jnp.asarray(np.repeat(slopes, HEAD_DIM).reshape(1, md.ws)),
        gfo=p["fox_out_norm"].reshape(1, md.wf), gso=p["swa_out_norm"].reshape(1, md.ws),
    )


def _mix_fwd(h, sp, w, md, queue=None):
    L = h.shape[0]
    hn = _rms_fwd(h, sp["g_mix"], name="mix_rms")
    u = _mm(queue, hn, w(0), name="mix_u")
    cexp, ct = _gate_fwd(u, sp["b_gate"], md, name="gate_fwd")
    ct3 = ct[:8 * md.pf].reshape(md.pf, 8, L)
    of, lsef = _carried(queue, _fox_fwd, 2, u, cexp, ct3, sp["fqg"], sp["fkg"], md, name="fox_fwd")
    os_, lses = _carried(queue, _swa_fwd, 2, u, sp["sinkp"], sp["slopep"], sp["sqg"], sp["skg"], md, name="swa_fwd")
    on = _outnorm_fwd(of, os_, sp["gfo"], sp["gso"], name="outnorm_fwd")
    h_out = _mm(queue, on, w(1), name="mix_out", residual=h)
    return h_out, (h, hn, u, cexp, ct3, of, lsef, os_, lses, on)


def _mix_bwd(dh, saved, sp, w_in, w_out, md, queue=None):
    h, hn, u, cexp, ct3, of, lsef, os_, lses, on = saved
    L = h.shape[0]
    dhb = _cast_bf16(dh, name="mix_dhb")
    don = _matmul(dhb, w_out, name="mix_don", nt=True)
    dw_out = _matmul(_transpose(on, name="mix_onT"), dhb, name="mix_dwout", out_dtype=BF16)
    dof, dos, dgfo, dgso = _outnorm_bwd(don, of, os_, sp["gfo"], sp["gso"], name="outnorm_bwd")
    duq, duk, duv, dck, dfqg, dfkg = _carried(queue, _fox_bwd, 6, u, cexp, ct3, sp["fqg"], sp["fkg"], dof, lsef, md,
                                              name="fox_bwd")
    dsq, dsk, dsv, dsinkp, dsqg, dskg = _carried(queue, _swa_bwd, 6, u, sp["sinkp"], sp["slopep"], sp["sqg"], sp["skg"], dos,
                                                 lses, md, name="swa_bwd")
    dck_t = jnp.pad(dck.reshape(8 * md.pf, L), ((0, LANES - 8 * md.pf), (0, 0)))
    dz, db = _gate_bwd(u, sp["b_gate"], dck_t, md, name="gate_bwd")
    du = jnp.concatenate([duq, duk, duv, dsq, dsk, dsv, dz, jnp.zeros((L, md.nup - md.nu), F32)], axis=1).astype(BF16)
    dhn = _mm(queue, du, w_in, name="mix_dhn", nt=True)
    dw_in = _mm(queue, _transpose(hn, name="mix_hnT"), du, name="mix_dwin", out_dtype=BF16)
    dh_in, dg_mix = _rms_bwd(dhn, h, sp["g_mix"], dh, name="mix_drms")
    small = dict(
        mix_norm=dg_mix.reshape(-1),
        b_forget=db[0, :4 * md.hf].reshape(md.hf // 2, 8)[:, :2].reshape(md.hf),
        fox_q_norm=dfqg[0, :HEAD_DIM], fox_k_norm=dfkg[0, :HEAD_DIM],
        swa_q_norm=dsqg[0, :HEAD_DIM], swa_k_norm=dskg[0, :HEAD_DIM],
        swa_sinks=dsinkp[0, ::HEAD_DIM],
        fox_out_norm=dgfo.reshape(-1), swa_out_norm=dgso.reshape(-1),
    )
    return dh_in, dw_in, dw_out, small


_ANY = pl.BlockSpec(memory_space=pl.ANY)
_HALF_ROWS = (512, 352, 256, 192, 128, 64, 32, 16)


def _mesh_pos():
    return lax.axis_index("x"), lax.axis_index("y"), lax.axis_index("c")


def _other_chips(x, y):
    return [(1 - x, y), (x, 1 - y), (1 - x, 1 - y)]


def _rows_half(ref, which):
    rh = ref.shape[-2] // 2
    if len(ref.shape) == 2:
        return ref.at[pl.ds(which * rh, rh), :]
    return ref.at[:, pl.ds(which * rh, rh), :]


def _remote(src, dst, send_sems, recv_sems, idx, dev):
    return pltpu.make_async_remote_copy(src_ref=src, dst_ref=dst, send_sem=send_sems.at[idx], recv_sem=recv_sems.at[idx],
                                        device_id=dev, device_id_type=MESH)


def _cast_into_chunk(w, l_idx, k_idx, *, name):
    _, rows, cols = w.shape
    tr = _pick(rows, (512, 352, 256, 128, 64, 32, 16))

    def body(l_ref, k_ref, w_ref, o_ref):
        o_ref[...] = w_ref[...].astype(o_ref.dtype)

    return pl.pallas_call(
        body, name=name,
        grid_spec=pltpu.PrefetchScalarGridSpec(
            num_scalar_prefetch=2, grid=(rows // tr,),
            in_specs=[pl.BlockSpec((None, tr, cols), lambda i, l, k: (l[0], i, 0))],
            out_specs=pl.BlockSpec((None, tr, cols), lambda i, l, k: (k[0], i, 0))),
        out_shape=jax.ShapeDtypeStruct((NCH, rows, cols), BF16),
        compiler_params=_cparams("parallel"),
    )(l_idx, k_idx, w)


def _exchange_sibling_halves(arrs, *, name):
    n = len(arrs)

    def body(*refs):
        srcs, lands = refs[:n], refs[n:2 * n]
        send_sems, recv_sems = refs[2 * n:]
        x, y, c = _mesh_pos()
        sibling = (x, y, 1 - c)
        cps = [_remote(_rows_half(srcs[i], 1 - c), lands[i], send_sems, recv_sems, i, sibling) for i in range(n)]
        for cp in cps:
            cp.start()
        for cp in cps:
            cp.wait_recv()
        for cp in cps:
            cp.wait_send()

    return pl.pallas_call(
        body, name=name, in_specs=[_ANY] * n, out_specs=[_ANY] * n,
        out_shape=[jax.ShapeDtypeStruct((NCH, a.shape[1] // 2, a.shape[2]), a.dtype) for a in arrs],
        scratch_shapes=[pltpu.SemaphoreType.DMA((n,)), pltpu.SemaphoreType.DMA((n,))],
    )(*arrs)


class _ChipExchange:
    n_sems = NCH - 1

    def __init__(self, s, r0, r1, land=None):
        self.r0, self.r1 = r0, r1
        self.operands = [s] if land is None else [s, land]
        self.out_shapes = [jax.ShapeDtypeStruct((NCH - 1,) + s.shape[1:], s.dtype)]
        self.aliases = {} if land is None else {1: 0}

    def copies(self, cin, cout, send_sems, recv_sems):
        x, y, c = _mesh_pos()
        rows = pl.ds(self.r0, self.r1 - self.r0)
        cps = [_remote(cin[0].at[2 * cx + cy, rows, :], cout[0].at[j, rows, :], send_sems, recv_sems, j, (cx, cy, c))
               for j, (cx, cy) in enumerate(_other_chips(x, y))]
        return [(cp, cp) for cp in cps]


def _rows_quarter(ref, c, q):
    rq = ref.shape[0] // 4
    return ref.at[pl.ds((2 * c + q) * rq, rq), :]


class _GatherStep:
    n_sems = 7

    def __init__(self, stage1, stage2, stage3):
        given = [(st, b) for st, b in ((1, stage1), (2, stage2), (3, stage3)) if b is not None]
        self.stages = [st for st, _ in given]
        self.operands = [b for _, b in given]
        self.out_shapes = [jax.ShapeDtypeStruct(b.shape, b.dtype) for b in self.operands]
        self.aliases = {i: i for i in range(len(self.operands))}

    def copies(self, cin, cout, send_sems, recv_sems):
        x, y, c = _mesh_pos()
        k, kx, ky, kd = 2 * x + y, 2 * (1 - x) + y, 2 * x + (1 - y), 2 * (1 - x) + (1 - y)
        xn, yn, sibling = (1 - x, y, c), (x, 1 - y, c), (x, y, 1 - c)

        def pair(idx, dev, src, arrival):
            return (_remote(src, src, send_sems, recv_sems, idx, dev), _remote(arrival, arrival, send_sems, recv_sems, idx, dev))

        out = []
        for stage, buf in zip(self.stages, cout):
            if stage == 1:
                mine = _rows_half(buf.at[k], c)
                out.append(pair(0, xn, mine, _rows_half(buf.at[kx], c)))
                out.append(pair(1, yn, mine, _rows_half(buf.at[ky], c)))
            elif stage == 2:
                out.append(pair(2, yn, _rows_quarter(buf.at[kx], c, 0), _rows_quarter(buf.at[kd], c, 0)))
                out.append(pair(3, xn, _rows_quarter(buf.at[ky], c, 1), _rows_quarter(buf.at[kd], c, 1)))
                out.append(pair(4, sibling, _rows_half(buf.at[kx], c), _rows_half(buf.at[kx], 1 - c)))
                out.append(pair(5, sibling, _rows_half(buf.at[ky], c), _rows_half(buf.at[ky], 1 - c)))
            else:
                out.append(pair(6, sibling, _rows_half(buf.at[kd], c), _rows_half(buf.at[kd], 1 - c)))
        return out


class _GatherQueue:
    def __init__(self, bufs):
        self.bufs = list(bufs)
        self.step = -3
        for _ in range(3):
            self.give(_run_exchange(self.take(), name="first_allgather"))

    def take(self):
        s = self.step
        self.step += 1
        at = [i if 0 <= i < len(self.bufs) else None for i in (s + 3, s + 2, s + 1)]
        self.cur = [i for i in at if i is not None]
        if not self.cur:
            return None
        return _GatherStep(*[None if i is None else self.bufs[i] for i in at])

    def give(self, outs):
        for i, buf in zip(self.cur, outs):
            self.bufs[i] = buf


def _run_exchange(job, *, name):
    n_in, n_out = len(job.operands), len(job.out_shapes)

    def body(*refs):
        cps = job.copies(refs[:n_in], refs[n_in:n_in + n_out], refs[-2], refs[-1])
        for cp, _ in cps:
            cp.start()
        for _, arrival in cps:
            arrival.wait_recv()
        for cp, _ in cps:
            cp.wait_send()

    return pl.pallas_call(
        body, name=name, in_specs=[_ANY] * n_in, out_specs=[_ANY] * n_out, out_shape=list(job.out_shapes),
        input_output_aliases=dict(job.aliases),
        scratch_shapes=[pltpu.SemaphoreType.DMA((job.n_sems,)), pltpu.SemaphoreType.DMA((job.n_sems,))],
    )(*job.operands)


class _ExchangeQueue:
    def __init__(self, parts=2):
        self.parts = parts
        self.todo = []
        self.groups = {}

    def add(self, group, sums, done):
        self.groups[group] = dict(sums=sums, lands=[None] * len(sums), left=len(sums) * self.parts, done=done)
        self.todo += [(group, ai, p) for ai in range(len(sums)) for p in range(self.parts)]

    def take(self):
        if not self.todo:
            return None
        group, ai, p = self.cur = self.todo.pop(0)
        g = self.groups[group]
        step = g['sums'][ai].shape[1] // self.parts
        return _ChipExchange(g['sums'][ai], p * step, (p + 1) * step, g['lands'][ai])

    def give(self, outs):
        group, ai, _ = self.cur
        g = self.groups[group]
        g['lands'][ai] = outs[0]
        g['left'] -= 1

    def finish_ready(self):
        for group in [k for k, g in self.groups.items() if g['left'] == 0]:
            g = self.groups.pop(group)
            g['done'](g['sums'], g['lands'])

    def drain(self, name):
        while (job := self.take()) is not None:
            self.give(_run_exchange(job, name=name))
        self.finish_ready()


def _mm(queue, a, b, **kw):
    job = queue.take() if queue is not None else None
    if job is None:
        return _matmul(a, b, **kw)
    n_out = len(kw.get("out_dtypes") or (0,))
    outs = _matmul(a, b, comm=job, **kw)
    queue.give(outs[n_out:])
    return outs[0] if n_out == 1 else outs[:n_out]


def _share_with_sibling(arrs, *, name):
    n = len(arrs)

    def body(*refs):
        outs = refs[n:2 * n]
        send_sems, recv_sems = refs[2 * n:]
        x, y, c = _mesh_pos()
        sibling = (x, y, 1 - c)
        cps = []
        for i in range(n):
            mine = _rows_half(outs[i], c)
            cps.append(_remote(mine, mine, send_sems, recv_sems, i, sibling))
        for cp in cps:
            cp.start()
        for i in range(n):
            theirs = _rows_half(outs[i], 1 - c)
            _remote(theirs, theirs, send_sems, recv_sems, i, sibling).wait_recv()
        for cp in cps:
            cp.wait_send()

    return pl.pallas_call(
        body, name=name, in_specs=[_ANY] * n, out_specs=[_ANY] * n,
        out_shape=[jax.ShapeDtypeStruct(a.shape, a.dtype) for a in arrs],
        input_output_aliases={i: i for i in range(n)},
        scratch_shapes=[pltpu.SemaphoreType.DMA((n,)), pltpu.SemaphoreType.DMA((n,))],
    )(*arrs)


def _allgather_devices(v, *, name):
    m = v.shape[0]

    def body(v_ref, out_ref, send_sems, recv_sems, local_sem):
        x, y, c = _mesh_pos()
        mine = 4 * x + 2 * y + c
        own = pltpu.make_async_copy(v_ref, out_ref.at[mine], local_sem)
        own.start()
        cps = []
        for r in range(1, 8):
            px, py, pc = (x + (r >> 2)) % 2, (y + ((r >> 1) & 1)) % 2, (c + (r & 1)) % 2
            cps.append((_remote(v_ref, out_ref.at[mine], send_sems, recv_sems, r - 1, (px, py, pc)), 4 * px + 2 * py + pc))
        for cp, _ in cps:
            cp.start()
        for r, (cp, theirs) in enumerate(cps):
            blk = out_ref.at[theirs]
            _remote(blk, blk, send_sems, recv_sems, r, (x, y, c)).wait_recv()
        for cp, _ in cps:
            cp.wait_send()
        own.wait()

    return pl.pallas_call(
        body, name=name, in_specs=[_ANY], out_specs=_ANY,
        out_shape=jax.ShapeDtypeStruct((8, m, LANES), v.dtype),
        scratch_shapes=[pltpu.SemaphoreType.DMA((7,)), pltpu.SemaphoreType.DMA((7,)), pltpu.SemaphoreType.DMA],
    )(v)


def _add_own_half(g, land, c_idx, *, name):
    nch, rh, cols = land.shape
    tr = _pick(rh, _HALF_ROWS)
    nt = rh // tr

    def body(c_ref, g_ref, l_ref, o_ref):
        o_ref[...] = (g_ref[...].astype(F32) + l_ref[...].astype(F32)).astype(o_ref.dtype)

    return pl.pallas_call(
        body, name=name,
        grid_spec=pltpu.PrefetchScalarGridSpec(
            num_scalar_prefetch=1, grid=(nch, nt),
            in_specs=[pl.BlockSpec((None, tr, cols), lambda k, i, c: (k, c[0] * nt + i, 0)),
                      pl.BlockSpec((None, tr, cols), lambda k, i, c: (k, i, 0))],
            out_specs=pl.BlockSpec((None, tr, cols), lambda k, i, c: (k, i, 0))),
        out_shape=jax.ShapeDtypeStruct(land.shape, BF16),
        compiler_params=_cparams("parallel", "parallel"),
    )(c_idx, g, land)


def _add_chunks(s, land, k_idx, c_idx, *, name):
    _, rh, cols = s.shape
    tr = _pick(rh, _HALF_ROWS)
    nt = rh // tr

    def body(k_ref, c_ref, s_ref, l_ref, o_ref):
        t = s_ref[...].astype(F32)
        for j in range(NCH - 1):
            t = t + l_ref[j].astype(F32)
        o_ref[...] = t

    return pl.pallas_call(
        body, name=name,
        grid_spec=pltpu.PrefetchScalarGridSpec(
            num_scalar_prefetch=2, grid=(nt,),
            in_specs=[pl.BlockSpec((None, tr, cols), lambda i, k, c: (k[0], i, 0)),
                      pl.BlockSpec((NCH - 1, tr, cols), lambda i, k, c: (0, i, 0))],
            out_specs=pl.BlockSpec((tr, cols), lambda i, k, c: (c[0] * nt + i, 0))),
        out_shape=jax.ShapeDtypeStruct((2 * rh, cols), F32),
        compiler_params=_cparams("parallel"),
    )(k_idx, c_idx, s, land)


def _sum_devices(v, *, name):
    _, m, _ = v.shape

    def body(v_ref, o_ref):
        t = v_ref[0]
        for d in range(1, 8):
            t = t + v_ref[d]
        o_ref[...] = t

    return pl.pallas_call(
        body, name=name, grid=(1,),
        in_specs=[pl.BlockSpec((8, m, LANES), lambda i: (0, 0, 0))],
        out_specs=pl.BlockSpec((m, LANES), lambda i: (0, 0)),
        out_shape=jax.ShapeDtypeStruct((m, LANES), F32),
        compiler_params=_cparams("arbitrary"),
    )(v)


def _adamw_math(w, g, m, v):
    m = ADAM_B1 * m + (1.0 - ADAM_B1) * g
    v = ADAM_B2 * v + (1.0 - ADAM_B2) * (g * g)
    m_hat = m / (1.0 - ADAM_B1 ** ADAM_STEP)
    v_hat = v / (1.0 - ADAM_B2 ** ADAM_STEP)
    delta = -ADAM_LR * (m_hat / (jnp.sqrt(v_hat) + ADAM_EPS) + ADAM_WD * w)
    return delta, m, v


def _adamw_layer(w, m, v, g, layer, prev, *, name):
    depth, rows, cols = w.shape
    tr = _pick(rows, (256, 128, 64, 32, 16, 8))
    lay = pl.BlockSpec((None, tr, cols), lambda i, l: (l[0], i, 0))
    n_prev = 0 if prev is None else 4

    def body(l_ref, w_ref, m_ref, v_ref, g_ref, *rest):
        go_ref, d_ref, mo_ref, vo_ref = rest[n_prev:]
        g = g_ref[...]
        delta, m_new, v_new = _adamw_math(w_ref[...], g, m_ref[...], v_ref[...])
        go_ref[...] = g
        d_ref[...] = delta
        mo_ref[...] = m_new
        vo_ref[...] = v_new

    stack = jax.ShapeDtypeStruct(w.shape, F32)
    return pl.pallas_call(
        body, name=name,
        grid_spec=pltpu.PrefetchScalarGridSpec(
            num_scalar_prefetch=1, grid=(rows // tr,),
            in_specs=[lay, lay, lay, pl.BlockSpec((tr, cols), lambda i, l: (i, 0))] + [_ANY] * n_prev,
            out_specs=[lay] * 4),
        out_shape=[stack] * 4,
        input_output_aliases={} if prev is None else {5 + q: q for q in range(4)},
        compiler_params=_cparams("parallel"),
    )(layer, w, m, v, g, *(() if prev is None else prev))


def _adamw_flat(w, g, m, v, *, name):
    def body(w_ref, g_ref, m_ref, v_ref, d_ref, mo_ref, vo_ref):
        d_ref[...], mo_ref[...], vo_ref[...] = _adamw_math(w_ref[...], g_ref[...], m_ref[...], v_ref[...])

    blk = pl.BlockSpec(w.shape, lambda i: (0, 0))
    return pl.pallas_call(
        body, name=name, grid=(1,), in_specs=[blk] * 4, out_specs=[blk] * 3,
        out_shape=[jax.ShapeDtypeStruct(w.shape, F32)] * 3, compiler_params=_cparams("arbitrary"),
    )(w, g, m, v)


def _reduce_scatter_begin(parts, c_idx, tag):
    lands = _exchange_sibling_halves(parts, name=f"{tag}_rs_sibling")
    return [_add_own_half(p, l, c_idx, name=f"{tag}_rs_add2") for p, l in zip(parts, lands)]


def _reduce_scatter_end(sums, lands, c_idx, k_idx, tag):
    tots = [_add_chunks(s, l, k_idx, c_idx, name=f"{tag}_rs_add4") for s, l in zip(sums, lands)]
    return _share_with_sibling(tots, name=f"{tag}_rs_share")


_WEIGHTS = ('meta_tokens', 'ffn1_norm', 'ffn1_w_gate', 'ffn1_w_up', 'ffn1_w_down', 'mix_norm', 'w_in', 'b_forget',
            'fox_q_norm', 'fox_k_norm', 'swa_q_norm', 'swa_k_norm', 'swa_sinks', 'fox_out_norm', 'swa_out_norm', 'w_out',
            'ffn2_norm', 'ffn2_w_gate', 'ffn2_w_up', 'ffn2_w_down')
_BIG = ('ffn1_w_gate', 'ffn1_w_up', 'ffn1_w_down', 'w_in', 'w_out', 'ffn2_w_gate', 'ffn2_w_up', 'ffn2_w_down')
_SMALL = tuple(n for n in _WEIGHTS if n not in _BIG and n != 'meta_tokens')
_MIX_SMALL = ('mix_norm', 'b_forget', 'fox_q_norm', 'fox_k_norm', 'swa_q_norm', 'swa_k_norm', 'swa_sinks',
              'fox_out_norm', 'swa_out_norm')


def _pack_rows(vectors):
    flat = jnp.concatenate([v.reshape(-1) for v in vectors])
    n = flat.shape[0]
    m = -(-n // (8 * LANES)) * 8
    return jnp.pad(flat, (0, m * LANES - n)).reshape(m, LANES)


def _unpack_rows(packed, shapes):
    flat = packed.reshape(-1)
    out, o = [], 0
    for s in shapes:
        n = int(np.prod(s))
        out.append(flat[o:o + n].reshape(s))
        o += n
    return out


def kernel(x, meta_tokens, ffn1_norm, ffn1_w_gate, ffn1_w_up, ffn1_w_down, mix_norm, w_in, b_forget, fox_q_norm, fox_k_norm, swa_q_norm, swa_k_norm, swa_sinks, fox_out_norm, swa_out_norm, w_out, ffn2_norm, ffn2_w_gate, ffn2_w_up, ffn2_w_down, loss_target, m_meta_tokens, m_ffn1_norm, m_ffn1_w_gate, m_ffn1_w_up, m_ffn1_w_down, m_mix_norm, m_w_in, m_b_forget, m_fox_q_norm, m_fox_k_norm, m_swa_q_norm, m_swa_k_norm, m_swa_sinks, m_fox_out_norm, m_swa_out_norm, m_w_out, m_ffn2_norm, m_ffn2_w_gate, m_ffn2_w_up, m_ffn2_w_down, v_meta_tokens, v_ffn1_norm, v_ffn1_w_gate, v_ffn1_w_up, v_ffn1_w_down, v_mix_norm, v_w_in, v_b_forget, v_fox_q_norm, v_fox_k_norm, v_swa_q_norm, v_swa_k_norm, v_swa_sinks, v_fox_out_norm, v_swa_out_norm, v_w_out, v_ffn2_norm, v_ffn2_w_gate, v_ffn2_w_up, v_ffn2_w_down):
    W = dict(meta_tokens=meta_tokens, ffn1_norm=ffn1_norm, ffn1_w_gate=ffn1_w_gate, ffn1_w_up=ffn1_w_up, ffn1_w_down=ffn1_w_down, mix_norm=mix_norm, w_in=w_in, b_forget=b_forget, fox_q_norm=fox_q_norm, fox_k_norm=fox_k_norm, swa_q_norm=swa_q_norm, swa_k_norm=swa_k_norm, swa_sinks=swa_sinks, fox_out_norm=fox_out_norm, swa_out_norm=swa_out_norm, w_out=w_out, ffn2_norm=ffn2_norm, ffn2_w_gate=ffn2_w_gate, ffn2_w_up=ffn2_w_up, ffn2_w_down=ffn2_w_down)
    Mo = dict(meta_tokens=m_meta_tokens, ffn1_norm=m_ffn1_norm, ffn1_w_gate=m_ffn1_w_gate, ffn1_w_up=m_ffn1_w_up, ffn1_w_down=m_ffn1_w_down, mix_norm=m_mix_norm, w_in=m_w_in, b_forget=m_b_forget, fox_q_norm=m_fox_q_norm, fox_k_norm=m_fox_k_norm, swa_q_norm=m_swa_q_norm, swa_k_norm=m_swa_k_norm, swa_sinks=m_swa_sinks, fox_out_norm=m_fox_out_norm, swa_out_norm=m_swa_out_norm, w_out=m_w_out, ffn2_norm=m_ffn2_norm, ffn2_w_gate=m_ffn2_w_gate, ffn2_w_up=m_ffn2_w_up, ffn2_w_down=m_ffn2_w_down)
    Vo = dict(meta_tokens=v_meta_tokens, ffn1_norm=v_ffn1_norm, ffn1_w_gate=v_ffn1_w_gate, ffn1_w_up=v_ffn1_w_up, ffn1_w_down=v_ffn1_w_down, mix_norm=v_mix_norm, w_in=v_w_in, b_forget=v_b_forget, fox_q_norm=v_fox_q_norm, fox_k_norm=v_fox_k_norm, swa_q_norm=v_swa_q_norm, swa_k_norm=v_swa_k_norm, swa_sinks=v_swa_sinks, fox_out_norm=v_fox_out_norm, swa_out_norm=v_swa_out_norm, w_out=v_w_out, ffn2_norm=v_ffn2_norm, ffn2_w_gate=v_ffn2_w_gate, ffn2_w_up=v_ffn2_w_up, ffn2_w_down=v_ffn2_w_down)

    _, S, D = x.shape
    L = S + BLOCK
    depth = ffn1_norm.shape[0]
    md = _MixDims(D)
    mx, my, mc = _mesh_pos()
    k_idx = (2 * mx + my).astype(jnp.int32).reshape(1)
    c_idx = mc.astype(jnp.int32).reshape(1)
    dcols = D // NCH

    meta_all = _allgather_devices(meta_tokens.reshape(-1, LANES), name="meta_allgather")
    meta_full = jnp.transpose(meta_all[0::2].reshape(NCH, N_META, dcols), (1, 0, 2)).reshape(N_META, D)

    order = ('ffn1_w_gate', 'ffn1_w_up', 'ffn1_w_down', 'w_in', 'w_out', 'ffn2_w_gate', 'ffn2_w_up', 'ffn2_w_down')
    chunks = []
    for l in range(depth):
        l_idx = jnp.full((1,), l, jnp.int32)
        chunks += [_cast_into_chunk(W[name], l_idx, k_idx, name="cast_chunk") for name in order]
    gather = _GatherQueue(chunks)
    wts = [{} for _ in range(depth)]

    def weight(l, i):
        if i not in wts[l]:
            buf = gather.bufs[len(order) * l + i]
            if order[i] == 'w_in':
                buf = _win_to_mine(jnp.transpose(buf, (1, 0, 2)).reshape(D, NCH * buf.shape[2]), md)
            elif order[i] in ('w_out', 'ffn1_w_down', 'ffn2_w_down'):
                buf = buf.reshape(-1, D)
            wts[l][i] = buf
        return wts[l][i]

    h = jnp.concatenate([jnp.zeros((PAD, D), F32), meta_full, x[0]], axis=0)
    saved = []
    for l in range(depth):
        sp = _mix_small({n: W[n][l] for n in _MIX_SMALL}, md)
        h, s1 = _ffn_fwd(h, ffn1_norm[l].reshape(1, D), lambda i: weight(l, i), "ffn", gather)
        h, s2 = _mix_fwd(h, sp, lambda i: weight(l, 3 + i), md, gather)
        h, s3 = _ffn_fwd(h, ffn2_norm[l].reshape(1, D), lambda i: weight(l, 5 + i), "ffn", gather)
        saved.append((s1, s2, s3, sp))
    wts = [dict(g1=w[0], u1=w[1], d1=w[2], wi=w[3], wo=w[4], g2=w[5], u2=w[6], d2=w[7]) for w in wts]

    loss_part, dh = _loss_grad(h, loss_target[0], name="loss_grad")

    small_grads = {n: [None] * depth for n in _SMALL}
    stacks = {n: None for n in _BIG}

    def update(name, l, grad):
        stacks[name] = _adamw_layer(W[name], Mo[name], Vo[name], grad, jnp.full((1,), l, jnp.int32), stacks[name],
                                    name="adamw_layer")

    queue = _ExchangeQueue()

    def reduce_later(names, l, parts, tag):
        def done(sums, lands):
            for n, grad in zip(names, _reduce_scatter_end(sums, lands, c_idx, k_idx, tag)):
                update(n, l, grad)
        queue.add((tag, names[0], l), _reduce_scatter_begin(parts, c_idx, tag), done)

    for l in range(depth - 1, -1, -1):
        wl = wts[l]
        s1, s2, s3, sp = saved[l]
        dh, dg, dwg, dwu, dwd = _ffn_bwd(dh, s3, ffn2_norm[l].reshape(1, D), wl['g2'], wl['u2'], wl['d2'], "ffn", queue)
        queue.finish_ready()
        small_grads['ffn2_norm'][l] = dg.reshape(-1)
        reduce_later(('ffn2_w_gate', 'ffn2_w_up', 'ffn2_w_down'), l, [dwg, dwu, dwd.reshape(NCH, -1, D)], "ffn")

        dh, dwi, dwo, sm = _mix_bwd(dh, s2, sp, wl['wi'], wl['wo'], md, queue)
        queue.finish_ready()
        for n in _MIX_SMALL:
            small_grads[n][l] = sm[n]
        dwi = _win_grad_to_ref(dwi, md)
        dwi = jnp.transpose(dwi.reshape(D, NCH, -1), (1, 0, 2))
        reduce_later(('w_in', 'w_out'), l, [dwi, dwo.reshape(NCH, -1, D)], "mix")

        dh, dg, dwg, dwu, dwd = _ffn_bwd(dh, s1, ffn1_norm[l].reshape(1, D), wl['g1'], wl['u1'], wl['d1'], "ffn", queue)
        queue.finish_ready()
        small_grads['ffn1_norm'][l] = dg.reshape(-1)
        reduce_later(('ffn1_w_gate', 'ffn1_w_up', 'ffn1_w_down'), l, [dwg, dwu, dwd.reshape(NCH, -1, D)], "ffn")
    queue.drain("rs_chips")

    grad_x = dh[BLOCK:][None]

    small_shapes = [W[n].shape for n in _SMALL]
    parts = [jnp.stack(small_grads[n]) for n in _SMALL] + [dh[PAD:BLOCK], loss_part[0, :1]]
    packed = _pack_rows(parts)
    total = _sum_devices(_allgather_devices(packed, name="small_allgather"), name="small_sum")
    *g_small, g_meta, loss = _unpack_rows(total, small_shapes + [(N_META, D), (1,)])
    g_meta = lax.dynamic_slice(g_meta, (0, k_idx[0] * dcols), (N_META, dcols))

    sw = _pack_rows([W[n] for n in _SMALL])
    sd, smm, svv = _adamw_flat(sw, _pack_rows(g_small), _pack_rows([Mo[n] for n in _SMALL]),
                               _pack_rows([Vo[n] for n in _SMALL]), name="adamw_small")
    d_small, m_small, v_small = (_unpack_rows(t, small_shapes) for t in (sd, smm, svv))
    d_meta, m_meta, v_meta = _adamw_flat(meta_tokens, g_meta, m_meta_tokens, v_meta_tokens, name="adamw_meta")

    grads, deltas, new_m, new_v = {}, {}, {}, {}
    for n in _BIG:
        grads[n], deltas[n], new_m[n], new_v[n] = stacks[n]
    for i, n in enumerate(_SMALL):
        grads[n], deltas[n], new_m[n], new_v[n] = g_small[i], d_small[i], m_small[i], v_small[i]
    grads['meta_tokens'], deltas['meta_tokens'], new_m['meta_tokens'], new_v['meta_tokens'] = g_meta, d_meta, m_meta, v_meta
    return (loss.reshape(()), grad_x, *[grads[n] for n in _WEIGHTS], *[deltas[n] for n in _WEIGHTS],
            *[new_m[n] for n in _WEIGHTS], *[new_v[n] for n in _WEIGHTS])
```

```python
import numpy as np
import jax
import jax.numpy as jnp
from jax import lax
from jax.experimental import pallas as pl
from jax.experimental.pallas import tpu as pltpu

F32 = jnp.float32
BF16 = jnp.bfloat16

HEAD_DIM = 64
N_META = 16
BLOCK = 128
WINDOW = 128
PAD = BLOCK - N_META
EPS = 1e-6
NEG_INF = -1e30
SWA_GROUP = 8
NCH = 4
LANES = 128
QBLOCK = 512

ADAM_LR = 0.001
ADAM_B1 = 0.9
ADAM_B2 = 0.999
ADAM_EPS = 1e-08
ADAM_WD = 0.01
ADAM_STEP = 10

V7X_VMEM_BYTES = 64 * 1024 * 1024
VMEM_LIMIT = V7X_VMEM_BYTES - 8 * 1024 * 1024
MESH = pl.DeviceIdType.MESH
HIGHEST = lax.Precision.HIGHEST

_TM = (1088, 1024, 704, 512, 384, 256, 128)
_TN = (1408, 1024, 768, 512, 384, 256, 128)
_TK = (2176, 1408, 1024, 512, 384, 256, 128)
_TR = (544, 512, 384, 272, 256, 128)


def _pick(n, cands):
    for c in cands:
        if n % c == 0:
            return c
    return n


def _cparams(*sem):
    return pltpu.CompilerParams(dimension_semantics=sem if sem else None, vmem_limit_bytes=VMEM_LIMIT)


def _matmul(a, b, *, name, nt=False, b_chunked=False, out_chunked=False, out_dtype=F32,
            residual=None, scale=1.0, extras=(), epilogue=None, out_dtypes=None, comm=None):
    M, K = a.shape
    if not nt:
        N = b.shape[-1] * (NCH if b_chunked else 1)
        assert b.shape[-2] == K
        k_unit = K
    else:
        N = b.shape[-2]
        k_unit = b.shape[-1]
        assert k_unit * (NCH if b_chunked else 1) == K
    n_unit = N // NCH if (out_chunked or (b_chunked and not nt)) else N
    tm, tn, tk = _pick(M, _TM), _pick(n_unit, _TN), _pick(k_unit, _TK)
    if epilogue is None:
        extras = () if residual is None else (residual,)
        out_dtypes = (out_dtype,)

        def epilogue(acc, *res):
            r = acc * scale if scale != 1.0 else acc
            return (r + res[0] if res else r,)
    n_out = len(out_dtypes)
    n_temps = 4 if n_out > 1 else 0
    assert not (extras and out_chunked)

    def est(tm_):
        return (2 * tm_ * tk * 2 + 2 * tk * tn * 2 + tm_ * tn * 4
                + sum(2 * tm_ * tn * jnp.dtype(d).itemsize for d in out_dtypes)
                + sum(2 * tm_ * tn * e.dtype.itemsize for e in extras) + n_temps * tm_ * tn * 4)

    while est(tm) > VMEM_LIMIT * 3 // 4 and tm % 32 == 0:
        tm //= 2
    npc, kpc = n_unit // tn, k_unit // tk
    nk = K // tk
    grid = (M // tm, N // tn, nk)

    a_spec = pl.BlockSpec((tm, tk), lambda i, j, k: (i, k))
    if not nt:
        if b_chunked:
            b_spec = pl.BlockSpec((None, tk, tn), lambda i, j, k: (j // npc, k, j % npc))
        else:
            b_spec = pl.BlockSpec((tk, tn), lambda i, j, k: (k, j))
        dims = (((1,), (0,)), ((), ()))
    else:
        if b_chunked:
            b_spec = pl.BlockSpec((None, tn, tk), lambda i, j, k: (k // kpc, j, k % kpc))
        else:
            b_spec = pl.BlockSpec((tn, tk), lambda i, j, k: (j, k))
        dims = (((1,), (1,)), ((), ()))
    tile = pl.BlockSpec((tm, tn), lambda i, j, k: (i, j))
    if out_chunked:
        o_spec = pl.BlockSpec((None, tm, tn), lambda i, j, k: (j // npc, i, j % npc))
        out_shapes = [jax.ShapeDtypeStruct((NCH, M, n_unit), d) for d in out_dtypes]
    else:
        o_spec = tile
        out_shapes = [jax.ShapeDtypeStruct((M, N), d) for d in out_dtypes]
    in_specs = [a_spec, b_spec] + [tile] * len(extras)
    args = [a, b, *extras]

    n_main = len(args)
    n_cin = 0 if comm is None else len(comm.operands)
    n_cout = 0 if comm is None else len(comm.out_shapes)

    def body(*refs):
        a_ref, b_ref = refs[0], refs[1]
        e_refs = refs[2:n_main]
        o_refs = refs[n_main + n_cin:n_main + n_cin + n_out]
        acc_ref = refs[n_main + n_cin + n_out + n_cout]
        i, j, k = pl.program_id(0), pl.program_id(1), pl.program_id(2)
        if comm is not None:
            cin = refs[n_main:n_main + n_cin]
            cout = refs[n_main + n_cin + n_out:n_main + n_cin + n_out + n_cout]
            send_sems, recv_sems = refs[-2:]

            @pl.when((i == 0) & (j == 0) & (k == 0))
            def _():
                for cp, _ in comm.copies(cin, cout, send_sems, recv_sems):
                    cp.start()

        @pl.when(k == 0)
        def _():
            acc_ref[...] = jnp.zeros_like(acc_ref)

        acc_ref[...] += lax.dot_general(a_ref[...], b_ref[...], dims, preferred_element_type=F32)

        @pl.when(k == nk - 1)
        def _():
            for o_ref, val in zip(o_refs, epilogue(acc_ref[...], *[e[...] for e in e_refs])):
                o_ref[...] = val.astype(o_ref.dtype)

        if comm is not None:
            @pl.when((i == grid[0] - 1) & (j == grid[1] - 1) & (k == nk - 1))
            def _():
                cps = comm.copies(cin, cout, send_sems, recv_sems)
                for _, arrival in cps:
                    arrival.wait_recv()
                for cp, _ in cps:
                    cp.wait_send()

    scratch = [pltpu.VMEM((tm, tn), F32)]
    if comm is None:
        outs = pl.pallas_call(
            body, name=name, grid=grid, in_specs=in_specs, out_specs=[o_spec] * n_out, out_shape=out_shapes,
            scratch_shapes=scratch, compiler_params=_cparams("parallel", "parallel", "arbitrary"),
        )(*args)
    else:
        scratch += [pltpu.SemaphoreType.DMA((comm.n_sems,)), pltpu.SemaphoreType.DMA((comm.n_sems,))]
        outs = pl.pallas_call(
            body, name=name, grid=grid, in_specs=in_specs + [_ANY] * n_cin,
            out_specs=[o_spec] * n_out + [_ANY] * n_cout, out_shape=out_shapes + list(comm.out_shapes),
            input_output_aliases={n_main + s: n_out + d for s, d in comm.aliases.items()},
            scratch_shapes=scratch, compiler_params=_cparams("arbitrary", "arbitrary", "arbitrary"),
        )(*args, *comm.operands)
    return outs[0] if len(outs) == 1 else tuple(outs)


def _call_carrying(body, comm, args, *, name, grid, in_specs, out_specs, out_shape, scratch_shapes):
    if comm is None:
        return pl.pallas_call(body, name=name, grid=grid, in_specs=in_specs, out_specs=out_specs, out_shape=out_shape,
                              scratch_shapes=scratch_shapes, compiler_params=_cparams("arbitrary"))(*args)
    n_in, n_out, n_scr = len(in_specs), len(out_specs), len(scratch_shapes)
    n_cin, n_cout = len(comm.operands), len(comm.out_shapes)

    def carrying(*refs):
        ins, cin = refs[:n_in], refs[n_in:n_in + n_cin]
        outs = refs[n_in + n_cin:n_in + n_cin + n_out]
        cout = refs[n_in + n_cin + n_out:n_in + n_cin + n_out + n_cout]
        scr = refs[n_in + n_cin + n_out + n_cout:n_in + n_cin + n_out + n_cout + n_scr]
        send_sems, recv_sems = refs[-2:]
        step = pl.program_id(0)

        @pl.when(step == 0)
        def _():
            for cp, _ in comm.copies(cin, cout, send_sems, recv_sems):
                cp.start()

        body(*ins, *outs, *scr)

        @pl.when(step == grid[0] - 1)
        def _():
            cps = comm.copies(cin, cout, send_sems, recv_sems)
            for _, arrival in cps:
                arrival.wait_recv()
            for cp, _ in cps:
                cp.wait_send()

    return pl.pallas_call(
        carrying, name=name, grid=grid, in_specs=list(in_specs) + [_ANY] * n_cin,
        out_specs=list(out_specs) + [_ANY] * n_cout, out_shape=list(out_shape) + list(comm.out_shapes),
        input_output_aliases={n_in + s: n_out + d for s, d in comm.aliases.items()},
        scratch_shapes=list(scratch_shapes) + [pltpu.SemaphoreType.DMA((comm.n_sems,)), pltpu.SemaphoreType.DMA((comm.n_sems,))],
        compiler_params=_cparams("arbitrary"),
    )(*args, *comm.operands)


def _carried(queue, fn, n_out, *args, **kw):
    job = queue.take() if queue is not None else None
    outs = fn(*args, comm=job, **kw)
    if job is not None:
        queue.give(outs[n_out:])
    return outs[:n_out]


def _transpose(x, *, name):
    M, N = x.shape
    tc = _pick(N, (512, 384, 256, 128))

    def body(x_ref, o_ref):
        o_ref[...] = x_ref[...].astype(F32).T.astype(o_ref.dtype)

    return pl.pallas_call(
        body, name=name, grid=(N // tc,),
        in_specs=[pl.BlockSpec((M, tc), lambda j: (0, j))],
        out_specs=pl.BlockSpec((tc, M), lambda j: (j, 0)),
        out_shape=jax.ShapeDtypeStruct((N, M), x.dtype),
        compiler_params=_cparams("parallel"),
    )(x)


def _rms_fwd(h, g, *, name):
    L, D = h.shape
    tr = _pick(L, _TR)

    def body(h_ref, g_ref, o_ref):
        x = h_ref[...]
        r = lax.rsqrt(jnp.mean(x * x, axis=-1, keepdims=True) + EPS)
        o_ref[...] = (x * r * g_ref[...]).astype(o_ref.dtype)

    return pl.pallas_call(
        body, name=name, grid=(L // tr,),
        in_specs=[pl.BlockSpec((tr, D), lambda i: (i, 0)), pl.BlockSpec((1, D), lambda i: (0, 0))],
        out_specs=pl.BlockSpec((tr, D), lambda i: (i, 0)),
        out_shape=jax.ShapeDtypeStruct((L, D), BF16),
        compiler_params=_cparams("parallel"),
    )(h, g)


def _rms_bwd(dy, h, g, dh, *, name):
    L, D = h.shape
    tr = _pick(L, _TR)

    def body(dy_ref, h_ref, g_ref, dh_ref, o_ref, dg_ref):
        i = pl.program_id(0)
        x = h_ref[...]
        dyv = dy_ref[...]
        r = lax.rsqrt(jnp.mean(x * x, axis=-1, keepdims=True) + EPS)
        xh = x * r
        dxh = dyv * g_ref[...]
        dx = r * (dxh - xh * jnp.mean(dxh * xh, axis=-1, keepdims=True))
        o_ref[...] = dh_ref[...] + dx
        part = jnp.sum(dyv * xh, axis=0, keepdims=True)

        @pl.when(i == 0)
        def _():
            dg_ref[...] = part

        @pl.when(i > 0)
        def _():
            dg_ref[...] += part

    row = pl.BlockSpec((tr, D), lambda i: (i, 0))
    vec = pl.BlockSpec((1, D), lambda i: (0, 0))
    return pl.pallas_call(
        body, name=name, grid=(L // tr,),
        in_specs=[row, row, vec, row], out_specs=[row, vec],
        out_shape=[jax.ShapeDtypeStruct((L, D), F32), jax.ShapeDtypeStruct((1, D), F32)],
        compiler_params=_cparams("arbitrary"),
    )(dy, h, g, dh)


def _swiglu_epilogue(up, gate):
    g = gate.astype(F32)
    return up, g * jax.nn.sigmoid(g) * up


def _swiglu_bwd_epilogue(acc, gate, up):
    d = 0.5 * acc
    g = gate.astype(F32)
    sg = jax.nn.sigmoid(g)
    return d * up.astype(F32) * sg * (1.0 + g * (1.0 - sg)), d * g * sg


def _cast_bf16(x, *, name):
    L, D = x.shape
    tr = _pick(L, _TR)

    def body(x_ref, o_ref):
        o_ref[...] = x_ref[...].astype(o_ref.dtype)

    blk = pl.BlockSpec((tr, D), lambda i: (i, 0))
    return pl.pallas_call(
        body, name=name, grid=(L // tr,), in_specs=[blk], out_specs=blk,
        out_shape=jax.ShapeDtypeStruct((L, D), BF16), compiler_params=_cparams("parallel"),
    )(x)


def _loss_grad(h, target, *, name):
    L, D = h.shape
    S = target.shape[0]
    nb = L // BLOCK

    def body(h_ref, t_ref, loss_ref, dh_ref):
        i = pl.program_id(0)

        @pl.when(i == 0)
        def _():
            loss_ref[...] = jnp.zeros_like(loss_ref)
            dh_ref[...] = jnp.zeros_like(dh_ref)

        @pl.when(i > 0)
        def _():
            err = h_ref[...] - t_ref[...]
            dh_ref[...] = err * (1.0 / D)
            loss_ref[...] += jnp.full(loss_ref.shape, (0.5 / D) * jnp.sum(err * err), F32)

    return pl.pallas_call(
        body, name=name, grid=(nb,),
        in_specs=[pl.BlockSpec((BLOCK, D), lambda i: (i, 0)),
                  pl.BlockSpec((BLOCK, D), lambda i: (jnp.maximum(i - 1, 0), 0))],
        out_specs=[pl.BlockSpec((1, LANES), lambda i: (0, 0)), pl.BlockSpec((BLOCK, D), lambda i: (i, 0))],
        out_shape=[jax.ShapeDtypeStruct((1, LANES), F32), jax.ShapeDtypeStruct((L, D), F32)],
        compiler_params=_cparams("arbitrary"),
    )(h, target)


def _ffn_fwd(h, g, w, tag, queue=None):
    hn = _rms_fwd(h, g, name=f"{tag}_rms")
    gate = _mm(queue, hn, w(0), name=f"{tag}_gate", b_chunked=True, out_dtype=BF16)
    up, act = _mm(queue, hn, w(1), name=f"{tag}_up", b_chunked=True, extras=(gate,), epilogue=_swiglu_epilogue,
                  out_dtypes=(BF16, BF16))
    h_out = _mm(queue, act, w(2), name=f"{tag}_down", residual=h, scale=0.5)
    return h_out, (h, hn, gate, up, act)


def _ffn_bwd(dh, saved, g, wg, wu, wd, tag, queue=None):
    h, hn, gate, up, act = saved
    dout = _cast_bf16(dh, name=f"{tag}_dout")
    dgate, dup = _mm(queue, dout, wd, name=f"{tag}_dact", nt=True, extras=(gate, up), epilogue=_swiglu_bwd_epilogue,
                     out_dtypes=(BF16, BF16))
    actT = _transpose(act, name=f"{tag}_actT")
    dwd = _mm(queue, actT, dout, name=f"{tag}_dwd", out_dtype=BF16, scale=0.5)
    hnT = _transpose(hn, name=f"{tag}_hnT")
    dwg = _mm(queue, hnT, dgate, name=f"{tag}_dwg", out_chunked=True, out_dtype=BF16)
    dwu = _mm(queue, hnT, dup, name=f"{tag}_dwu", out_chunked=True, out_dtype=BF16)
    dhn = _mm(queue, dgate, wg, name=f"{tag}_dhn_g", nt=True, b_chunked=True)
    dhn = _mm(queue, dup, wu, name=f"{tag}_dhn_u", nt=True, b_chunked=True, residual=dhn)
    dh_in, dg = _rms_bwd(dhn, h, g, dh, name=f"{tag}_drms")
    return dh_in, dg, dwg, dwu, dwd


class _MixDims:
    def __init__(self, d_model):
        self.wf = d_model // 2
        self.ws = d_model // 2
        self.pf = self.wf // LANES
        self.ps = self.ws // LANES
        self.hf = self.wf // HEAD_DIM
        self.hq = self.ws // HEAD_DIM
        self.nkv = max(1, self.hq // SWA_GROUP)
        self.g = self.hq // self.nkv
        self.bq_f, self.bk_f, self.bv_f = 0, self.pf, 2 * self.pf
        self.bq_s = 3 * self.pf
        self.bk_s = self.bq_s + self.ps
        self.bv_s = self.bk_s + self.nkv
        self.bz = self.bv_s + self.nkv
        self.nu = (self.bz + 1) * LANES
        self.nup = -(-self.nu // 512) * 512
        self.in_width = 3 * self.wf + self.hf + self.ws + 2 * self.nkv * HEAD_DIM
        assert self.hf <= 2 * (LANES // 8)

    def gate_lane(self, h):
        return 8 * (h // 2) + h % 2

    def column_map(self):
        wf, ws, hd = self.wf, self.ws, HEAD_DIM
        src = np.full((self.nup,), -1, np.int64)
        src[0:3 * wf] = np.arange(3 * wf)
        o_sq = 3 * wf + self.hf
        src[self.bq_s * LANES:self.bq_s * LANES + ws] = o_sq + np.arange(ws)
        o_sk = o_sq + ws
        o_sv = o_sk + self.nkv * hd
        for kv in range(self.nkv):
            for rep in range(2):
                c0 = (self.bk_s + kv) * LANES + rep * hd
                src[c0:c0 + hd] = o_sk + kv * hd + np.arange(hd)
                c0 = (self.bv_s + kv) * LANES + rep * hd
                src[c0:c0 + hd] = o_sv + kv * hd + np.arange(hd)
        for h in range(self.hf):
            src[self.bz * LANES + self.gate_lane(h)] = 3 * wf + h
        return src

    def grad_column_map(self):
        src = self.column_map()
        dst = np.zeros((self.in_width,), np.int64)
        for col in range(self.nup - 1, -1, -1):
            if src[col] >= 0:
                dst[src[col]] = col
        return dst


def _block_diag_mean():
    m = np.zeros((LANES, LANES), np.float32)
    m[:HEAD_DIM, :HEAD_DIM] = 1.0 / HEAD_DIM
    m[HEAD_DIM:, HEAD_DIM:] = 1.0 / HEAD_DIM
    return jnp.asarray(m)


def _fold_halves():
    m = np.eye(LANES, dtype=np.float32)
    m[np.arange(LANES), (np.arange(LANES) + HEAD_DIM) % LANES] = 1.0
    return jnp.asarray(m)


def _gate_expand(md):
    e = np.zeros((LANES, md.wf), np.float32)
    for h in range(md.hf):
        e[md.gate_lane(h), h * HEAD_DIM:(h + 1) * HEAD_DIM] = 1.0
    return jnp.asarray(e)


def _qblocks(L, qb):
    blocks = [(0, BLOCK)]
    r = BLOCK
    while r < L:
        blocks.append((r, qb))
        r += qb
    assert r == L
    return blocks


def _f32dot(a, b):
    return jnp.dot(a, b, precision=HIGHEST, preferred_element_type=F32)


_DIMS_NT = (((1,), (1,)), ((), ()))


def _dot_nt(a, b):
    return lax.dot_general(a, b, _DIMS_NT, preferred_element_type=F32)


def _dot_tn(a, b):
    return jnp.dot(a.T.astype(BF16), b, preferred_element_type=F32)


def _log_sigmoid(z):
    return jnp.minimum(z, 0.0) - jnp.log(1.0 + jnp.exp(-jnp.abs(z)))


def _gate_fwd(u, b, md, *, name):
    L = u.shape[0]
    nb = L // BLOCK
    expand = _gate_expand(md)

    def body(z_ref, b_ref, e_ref, cexp_ref, ct_ref, c_s):
        ri = lax.broadcasted_iota(jnp.int32, (BLOCK, BLOCK), 0)
        ci = lax.broadcasted_iota(jnp.int32, (BLOCK, BLOCK), 1)
        tri = (ri >= ci).astype(F32)
        carry = jnp.zeros((1, LANES), F32)
        for bi in range(nb):
            rows = pl.ds(bi * BLOCK, BLOCK)
            logf = _log_sigmoid(z_ref[rows, :] + b_ref[...])
            blk = _f32dot(tri, logf) + carry
            c_s[rows, :] = blk
            carry = blk[BLOCK - 1:BLOCK, :]
        c = c_s[...]
        ct_ref[...] = c.T
        cexp_ref[...] = _f32dot(c, e_ref[...])

    return pl.pallas_call(
        body, name=name, grid=(1,),
        in_specs=[pl.BlockSpec((L, LANES), lambda i: (0, md.bz)), pl.BlockSpec((1, LANES), lambda i: (0, 0)),
                  pl.BlockSpec((LANES, md.wf), lambda i: (0, 0))],
        out_specs=[pl.BlockSpec((L, md.wf), lambda i: (0, 0)), pl.BlockSpec((LANES, L), lambda i: (0, 0))],
        out_shape=[jax.ShapeDtypeStruct((L, md.wf), F32), jax.ShapeDtypeStruct((LANES, L), F32)],
        scratch_shapes=[pltpu.VMEM((L, LANES), F32)],
        compiler_params=_cparams("arbitrary"),
    )(u, b, expand)


def _gate_bwd(u, b, dck_t, md, *, name):
    L = u.shape[0]
    nb = L // BLOCK

    def body(z_ref, b_ref, dck_ref, dz_ref, db_ref, dc_s):
        ri = lax.broadcasted_iota(jnp.int32, (BLOCK, BLOCK), 0)
        ci = lax.broadcasted_iota(jnp.int32, (BLOCK, BLOCK), 1)
        triu = (ri <= ci).astype(F32)
        dc_s[...] = -dck_ref[...].T
        carry = jnp.zeros((1, LANES), F32)
        db = jnp.zeros((1, LANES), F32)
        for bi in range(nb - 1, -1, -1):
            rows = pl.ds(bi * BLOCK, BLOCK)
            blk = _f32dot(triu, dc_s[rows, :]) + carry
            carry = blk[0:1, :]
            z = z_ref[rows, :] + b_ref[...]
            dz = blk * jax.nn.sigmoid(-z)
            if bi == 0:
                dz = jnp.where(lax.broadcasted_iota(jnp.int32, (BLOCK, LANES), 0) >= PAD, dz, 0.0)
            dz_ref[rows, :] = dz
            db = db + jnp.sum(dz, axis=0, keepdims=True)
        db_ref[...] = db

    return pl.pallas_call(
        body, name=name, grid=(1,),
        in_specs=[pl.BlockSpec((L, LANES), lambda i: (0, md.bz)), pl.BlockSpec((1, LANES), lambda i: (0, 0)),
                  pl.BlockSpec((LANES, L), lambda i: (0, 0))],
        out_specs=[pl.BlockSpec((L, LANES), lambda i: (0, 0)), pl.BlockSpec((1, LANES), lambda i: (0, 0))],
        out_shape=[jax.ShapeDtypeStruct((L, LANES), F32), jax.ShapeDtypeStruct((1, LANES), F32)],
        scratch_shapes=[pltpu.VMEM((L, LANES), F32)],
        compiler_params=_cparams("arbitrary"),
    )(u, b, dck_t)


def _head_norm(x, g, bd):
    r = lax.rsqrt(_f32dot(x * x, bd) + EPS)
    xh = x * r
    return xh * g, xh, r


def _head_norm_bwd(dy, xh, r, g, bd):
    dxh = dy * g
    dx = r * (dxh - xh * _f32dot(dxh * xh, bd))
    return dx, jnp.sum(dy * xh, axis=0, keepdims=True)


def _lane_half():
    return lax.broadcasted_iota(jnp.int32, (1, LANES), 1) < HEAD_DIM


def _store_head_pair(x, half, a_s, b_s):
    a_s[...] = jnp.where(half, x, 0.0).astype(BF16)
    b_s[...] = jnp.where(half, 0.0, x).astype(BF16)


def _fox_scores(qm, kn_s, cexp_ref, ct_ref, r0, nr, klen, hh):
    s = _dot_nt(qm, kn_s[0:klen, :]) * (HEAD_DIM ** -0.5)
    s = s + cexp_ref[r0:r0 + nr, HEAD_DIM * hh:HEAD_DIM * hh + 1] - ct_ref[hh:hh + 1, 0:klen]
    qp = r0 + lax.broadcasted_iota(jnp.int32, (nr, klen), 0)
    kp = lax.broadcasted_iota(jnp.int32, (nr, klen), 1)
    return jnp.where((kp <= qp) & (kp >= PAD), s, NEG_INF)


def _fox_fwd(u, cexp, ct3, qg, kg, md, *, name, comm=None):
    L = u.shape[0]
    blocks = _qblocks(L, QBLOCK)
    bd = _block_diag_mean()

    def body(q_ref, k_ref, v_ref, cexp_ref, ct_ref, qg_ref, kg_ref, bd_ref, o_ref, lse_ref, qa_s, qb_s, kn_s, v_s):
        half = _lane_half()
        bdv = bd_ref[...]
        _store_head_pair(_head_norm(q_ref[...], qg_ref[...], bdv)[0], half, qa_s, qb_s)
        kn_s[...] = _head_norm(k_ref[...], kg_ref[...], bdv)[0].astype(BF16)
        v_s[...] = v_ref[...].astype(BF16)
        for r0, nr in blocks:
            klen = r0 + nr
            o_blk = lse_blk = None
            for hh, q_s in enumerate((qa_s, qb_s)):
                s = _fox_scores(q_s[r0:r0 + nr, :], kn_s, cexp_ref, ct_ref, r0, nr, klen, hh)
                m = jnp.max(s, axis=-1, keepdims=True)
                p = jnp.exp(s - m)
                l = jnp.sum(p, axis=-1, keepdims=True)
                oh = jnp.dot(p.astype(BF16), v_s[0:klen, :], preferred_element_type=F32) * (1.0 / l)
                lh = jnp.broadcast_to(jnp.where(m > 0.5 * NEG_INF, m + jnp.log(l), 0.0), (nr, LANES))
                o_blk = oh if hh == 0 else jnp.where(half, o_blk, oh)
                lse_blk = lh if hh == 0 else jnp.where(half, lse_blk, lh)
            o_ref[r0:r0 + nr, :] = o_blk
            lse_ref[r0:r0 + nr, :] = lse_blk

    col = lambda base: pl.BlockSpec((L, LANES), lambda j: (0, base + j))
    vec = pl.BlockSpec((1, LANES), lambda j: (0, 0))
    return _call_carrying(
        body, comm, (u, u, u, cexp, ct3, qg, kg, bd), name=name, grid=(md.pf,),
        in_specs=[col(md.bq_f), col(md.bk_f), col(md.bv_f), col(0),
                  pl.BlockSpec((None, 8, L), lambda j: (j, 0, 0)), vec, vec,
                  pl.BlockSpec((LANES, LANES), lambda j: (0, 0))],
        out_specs=[col(0), col(0)],
        out_shape=[jax.ShapeDtypeStruct((L, md.wf), F32)] * 2,
        scratch_shapes=[pltpu.VMEM((L, LANES), BF16)] * 4)


def _softmax_bwd(p, dp):
    pdp = p * dp
    return pdp - p * jnp.sum(pdp, axis=-1, keepdims=True)


def _fox_bwd(u, cexp, ct3, qg, kg, do, lse, md, *, name, comm=None):
    L = u.shape[0]
    blocks = _qblocks(L, QBLOCK // 2)
    bd = _block_diag_mean()
    fold = _fold_halves()
    npairs = md.pf

    def body(q_ref, k_ref, v_ref, cexp_ref, ct_ref, qg_ref, kg_ref, bd_ref, fold_ref, do_ref, lse_ref,
             dq_ref, dk_ref, dv_ref, dck_ref, dqg_ref, dkg_ref,
             qa_s, qb_s, kn_s, v_s, doa_s, dob_s, dqn_s, dkn_s, dvv_s):
        j = pl.program_id(0)
        half = _lane_half()
        bdv = bd_ref[...]
        qy, qh, rq = _head_norm(q_ref[...], qg_ref[...], bdv)
        ky, kh, rk = _head_norm(k_ref[...], kg_ref[...], bdv)
        _store_head_pair(qy, half, qa_s, qb_s)
        _store_head_pair(do_ref[...], half, doa_s, dob_s)
        kn_s[...] = ky.astype(BF16)
        v_s[...] = v_ref[...].astype(BF16)
        dkn_s[...] = jnp.zeros_like(dkn_s)
        dvv_s[...] = jnp.zeros_like(dvv_s)
        dck_ref[...] = jnp.zeros_like(dck_ref)
        for r0, nr in blocks:
            klen = r0 + nr
            dq_blk = None
            for hh, (q_s, do_s) in enumerate(((qa_s, doa_s), (qb_s, dob_s))):
                c0 = HEAD_DIM * hh
                qm = q_s[r0:r0 + nr, :]
                dom = do_s[r0:r0 + nr, :]
                s = _fox_scores(qm, kn_s, cexp_ref, ct_ref, r0, nr, klen, hh)
                p = jnp.exp(s - lse_ref[r0:r0 + nr, c0:c0 + 1])
                ds = _softmax_bwd(p, _dot_nt(dom, v_s[0:klen, :]))
                dck_ref[hh:hh + 1, 0:klen] += jnp.sum(ds, axis=0, keepdims=True)
                ds = ds * (HEAD_DIM ** -0.5)
                dq_h = jnp.dot(ds.astype(BF16), kn_s[0:klen, :], preferred_element_type=F32)
                dkn_s[0:klen, :] += _dot_tn(ds, qm)
                dvv_s[0:klen, :] += _dot_tn(p, dom)
                dq_blk = dq_h if hh == 0 else jnp.where(half, dq_blk, dq_h)
            dqn_s[r0:r0 + nr, :] = dq_blk
        dq, dqg = _head_norm_bwd(dqn_s[...], qh, rq, qg_ref[...], bdv)
        dk, dkg = _head_norm_bwd(dkn_s[...], kh, rk, kg_ref[...], bdv)
        dq_ref[...] = dq
        dk_ref[...] = dk
        dv_ref[...] = dvv_s[...]

        @pl.when(j == 0)
        def _():
            dqg_ref[...] = jnp.zeros_like(dqg_ref)
            dkg_ref[...] = jnp.zeros_like(dkg_ref)

        dqg_ref[...] += dqg
        dkg_ref[...] += dkg

        @pl.when(j == npairs - 1)
        def _():
            dqg_ref[...] = _f32dot(jnp.broadcast_to(dqg_ref[...], (8, LANES)), fold_ref[...])[0:1, :]
            dkg_ref[...] = _f32dot(jnp.broadcast_to(dkg_ref[...], (8, LANES)), fold_ref[...])[0:1, :]

    col = lambda base: pl.BlockSpec((L, LANES), lambda j: (0, base + j))
    vec = pl.BlockSpec((1, LANES), lambda j: (0, 0))
    sq = pl.BlockSpec((LANES, LANES), lambda j: (0, 0))
    ct_spec = pl.BlockSpec((None, 8, L), lambda j: (j, 0, 0))
    big = jax.ShapeDtypeStruct((L, md.wf), F32)
    small = jax.ShapeDtypeStruct((1, LANES), F32)
    return _call_carrying(
        body, comm, (u, u, u, cexp, ct3, qg, kg, bd, fold, do, lse), name=name, grid=(md.pf,),
        in_specs=[col(md.bq_f), col(md.bk_f), col(md.bv_f), col(0), ct_spec, vec, vec, sq, sq, col(0), col(0)],
        out_specs=[col(0), col(0), col(0), ct_spec, vec, vec],
        out_shape=[big, big, big, jax.ShapeDtypeStruct((md.pf, 8, L), F32), small, small],
        scratch_shapes=[pltpu.VMEM((L, LANES), BF16)] * 6 + [pltpu.VMEM((L, LANES), F32)] * 3)


def _swa_scores(qm, kn_s, slope, k0, r0, nr, klen):
    s = _dot_nt(qm, kn_s[k0:k0 + klen, :]) * (HEAD_DIM ** -0.5)
    qp = r0 + lax.broadcasted_iota(jnp.int32, (nr, klen), 0)
    kp = k0 + lax.broadcasted_iota(jnp.int32, (nr, klen), 1)
    dist = qp - kp
    s = s - slope * dist.astype(F32)
    return jnp.where((dist >= 0) & (dist < WINDOW) & (kp >= PAD), s, NEG_INF)


def _swa_fwd(u, sinkp, slopep, qg, kg, md, *, name, comm=None):
    L = u.shape[0]
    blocks = _qblocks(L, QBLOCK)
    bd = _block_diag_mean()

    def body(q_ref, k_ref, v_ref, sink_ref, slope_ref, qg_ref, kg_ref, bd_ref, o_ref, lse_ref, qa_s, qb_s, kn_s, v_s):
        half = _lane_half()
        bdv = bd_ref[...]
        _store_head_pair(_head_norm(q_ref[...], qg_ref[...], bdv)[0], half, qa_s, qb_s)
        kn_s[...] = _head_norm(k_ref[...], kg_ref[...], bdv)[0].astype(BF16)
        v_s[...] = v_ref[...].astype(BF16)
        for r0, nr in blocks:
            k0 = max(r0 - BLOCK, 0)
            klen = r0 + nr - k0
            o_blk = lse_blk = None
            for hh, q_s in enumerate((qa_s, qb_s)):
                c0 = HEAD_DIM * hh
                s = _swa_scores(q_s[r0:r0 + nr, :], kn_s, slope_ref[0:1, c0:c0 + 1], k0, r0, nr, klen)
                sink = sink_ref[0:1, c0:c0 + 1]
                m = jnp.maximum(jnp.max(s, axis=-1, keepdims=True), sink)
                p = jnp.exp(s - m)
                den = jnp.sum(p, axis=-1, keepdims=True) + jnp.exp(sink - m)
                oh = jnp.dot(p.astype(BF16), v_s[k0:k0 + klen, :], preferred_element_type=F32) * (1.0 / den)
                lh = jnp.broadcast_to(m + jnp.log(den), (nr, LANES))
                o_blk = oh if hh == 0 else jnp.where(half, o_blk, oh)
                lse_blk = lh if hh == 0 else jnp.where(half, lse_blk, lh)
            o_ref[r0:r0 + nr, :] = o_blk
            lse_ref[r0:r0 + nr, :] = lse_blk

    g2 = md.g // 2
    qcol = pl.BlockSpec((L, LANES), lambda j: (0, md.bq_s + j))
    kcol = pl.BlockSpec((L, LANES), lambda j: (0, md.bk_s + j // g2))
    vcol = pl.BlockSpec((L, LANES), lambda j: (0, md.bv_s + j // g2))
    ocol = pl.BlockSpec((L, LANES), lambda j: (0, j))
    pvec = pl.BlockSpec((1, LANES), lambda j: (0, j))
    vec = pl.BlockSpec((1, LANES), lambda j: (0, 0))
    return _call_carrying(
        body, comm, (u, u, u, sinkp, slopep, qg, kg, bd), name=name, grid=(md.ps,),
        in_specs=[qcol, kcol, vcol, pvec, pvec, vec, vec, pl.BlockSpec((LANES, LANES), lambda j: (0, 0))],
        out_specs=[ocol, ocol],
        out_shape=[jax.ShapeDtypeStruct((L, md.ws), F32)] * 2,
        scratch_shapes=[pltpu.VMEM((L, LANES), BF16)] * 4)


def _swa_bwd(u, sinkp, slopep, qg, kg, do, lse, md, *, name, comm=None):
    L = u.shape[0]
    blocks = _qblocks(L, QBLOCK // 2)
    bd = _block_diag_mean()
    fold = _fold_halves()
    g2 = md.g // 2
    npairs = md.ps

    def body(q_ref, k_ref, v_ref, sink_ref, slope_ref, qg_ref, kg_ref, bd_ref, fold_ref, do_ref, lse_ref,
             dq_ref, dk_ref, dv_ref, dsink_ref, dqg_ref, dkg_ref,
             qa_s, qb_s, kn_s, v_s, doa_s, dob_s, dqn_s):
        j = pl.program_id(0)
        half = _lane_half()
        bdv = bd_ref[...]
        qy, qh, rq = _head_norm(q_ref[...], qg_ref[...], bdv)
        ky, kh, rk = _head_norm(k_ref[...], kg_ref[...], bdv)
        _store_head_pair(qy, half, qa_s, qb_s)
        _store_head_pair(do_ref[...], half, doa_s, dob_s)
        kn_s[...] = ky.astype(BF16)
        v_s[...] = v_ref[...].astype(BF16)

        @pl.when(j % g2 == 0)
        def _():
            dk_ref[...] = jnp.zeros_like(dk_ref)
            dv_ref[...] = jnp.zeros_like(dv_ref)

        @pl.when(j == 0)
        def _():
            dqg_ref[...] = jnp.zeros_like(dqg_ref)
            dkg_ref[...] = jnp.zeros_like(dkg_ref)

        dsink = [jnp.zeros((1, 1), F32), jnp.zeros((1, 1), F32)]
        for r0, nr in blocks:
            k0 = max(r0 - BLOCK, 0)
            klen = r0 + nr - k0
            dq_blk = None
            for hh, (q_s, do_s) in enumerate(((qa_s, doa_s), (qb_s, dob_s))):
                c0 = HEAD_DIM * hh
                qm = q_s[r0:r0 + nr, :]
                dom = do_s[r0:r0 + nr, :]
                s = _swa_scores(qm, kn_s, slope_ref[0:1, c0:c0 + 1], k0, r0, nr, klen)
                lse_h = lse_ref[r0:r0 + nr, c0:c0 + 1]
                p = jnp.exp(s - lse_h)
                pdp = p * _dot_nt(dom, v_s[k0:k0 + klen, :])
                delta = jnp.sum(pdp, axis=-1, keepdims=True)
                p_sink = jnp.exp(sink_ref[0:1, c0:c0 + 1] - lse_h)
                dsink[hh] = dsink[hh] - jnp.sum(p_sink * delta, axis=0, keepdims=True)
                ds = (pdp - p * delta) * (HEAD_DIM ** -0.5)
                dq_h = jnp.dot(ds.astype(BF16), kn_s[k0:k0 + klen, :], preferred_element_type=F32)
                dk_ref[k0:k0 + klen, :] += _dot_tn(ds, qm)
                dv_ref[k0:k0 + klen, :] += _dot_tn(p, dom)
                dq_blk = dq_h if hh == 0 else jnp.where(half, dq_blk, dq_h)
            dqn_s[r0:r0 + nr, :] = dq_blk
        dq, dqg = _head_norm_bwd(dqn_s[...], qh, rq, qg_ref[...], bdv)
        dq_ref[...] = dq
        dqg_ref[...] += dqg
        dsink_ref[...] = jnp.where(half, jnp.broadcast_to(dsink[0], (1, LANES)), jnp.broadcast_to(dsink[1], (1, LANES)))

        @pl.when(j % g2 == g2 - 1)
        def _():
            dkn = _f32dot(dk_ref[...], fold_ref[...])
            dk, dkg = _head_norm_bwd(dkn, kh, rk, kg_ref[...], bdv)
            dk_ref[...] = jnp.where(half, dk, 0.0)
            dv_ref[...] = jnp.where(half, _f32dot(dv_ref[...], fold_ref[...]), 0.0)
            dkg_ref[...] += dkg

        @pl.when(j == npairs - 1)
        def _():
            dqg_ref[...] = _f32dot(jnp.broadcast_to(dqg_ref[...], (8, LANES)), fold_ref[...])[0:1, :]

    qcol = pl.BlockSpec((L, LANES), lambda j: (0, md.bq_s + j))
    kcol = pl.BlockSpec((L, LANES), lambda j: (0, md.bk_s + j // g2))
    vcol = pl.BlockSpec((L, LANES), lambda j: (0, md.bv_s + j // g2))
    ocol = pl.BlockSpec((L, LANES), lambda j: (0, j))
    kvout = pl.BlockSpec((L, LANES), lambda j: (0, j // g2))
    pvec = pl.BlockSpec((1, LANES), lambda j: (0, j))
    vec = pl.BlockSpec((1, LANES), lambda j: (0, 0))
    sq = pl.BlockSpec((LANES, LANES), lambda j: (0, 0))
    kvshape = jax.ShapeDtypeStruct((L, LANES * md.nkv), F32)
    small = jax.ShapeDtypeStruct((1, LANES), F32)
    return _call_carrying(
        body, comm, (u, u, u, sinkp, slopep, qg, kg, bd, fold, do, lse), name=name, grid=(md.ps,),
        in_specs=[qcol, kcol, vcol, pvec, pvec, vec, vec, sq, sq, ocol, ocol],
        out_specs=[ocol, kvout, kvout, pvec, vec, vec],
        out_shape=[jax.ShapeDtypeStruct((L, md.ws), F32), kvshape, kvshape,
                   jax.ShapeDtypeStruct((1, md.ws), F32), small, small],
        scratch_shapes=[pltpu.VMEM((L, LANES), BF16)] * 6 + [pltpu.VMEM((L, LANES), F32)])


def _outnorm_fwd(of, os_, gf, gs, *, name):
    L, wf = of.shape
    ws = os_.shape[1]
    tr = _pick(L, _TR)

    def body(of_ref, os_ref, gf_ref, gs_ref, o_ref):
        for src, g_ref, c0, w in ((of_ref, gf_ref, 0, wf), (os_ref, gs_ref, wf, ws)):
            x = src[...]
            r = lax.rsqrt(jnp.mean(x * x, axis=-1, keepdims=True) + EPS)
            o_ref[:, c0:c0 + w] = (x * r * g_ref[...]).astype(o_ref.dtype)

    return pl.pallas_call(
        body, name=name, grid=(L // tr,),
        in_specs=[pl.BlockSpec((tr, wf), lambda i: (i, 0)), pl.BlockSpec((tr, ws), lambda i: (i, 0)),
                  pl.BlockSpec((1, wf), lambda i: (0, 0)), pl.BlockSpec((1, ws), lambda i: (0, 0))],
        out_specs=pl.BlockSpec((tr, wf + ws), lambda i: (i, 0)),
        out_shape=jax.ShapeDtypeStruct((L, wf + ws), BF16),
        compiler_params=_cparams("parallel"),
    )(of, os_, gf, gs)


def _outnorm_bwd(don, of, os_, gf, gs, *, name):
    L, wf = of.shape
    ws = os_.shape[1]
    tr = _pick(L, _TR)

    def body(d_ref, of_ref, os_ref, gf_ref, gs_ref, dof_ref, dos_ref, dgf_ref, dgs_ref):
        i = pl.program_id(0)
        for src, g_ref, c0, w, dx_ref, dg_ref in ((of_ref, gf_ref, 0, wf, dof_ref, dgf_ref),
                                                  (os_ref, gs_ref, wf, ws, dos_ref, dgs_ref)):
            x = src[...]
            dy = d_ref[:, c0:c0 + w]
            r = lax.rsqrt(jnp.mean(x * x, axis=-1, keepdims=True) + EPS)
            xh = x * r
            dxh = dy * g_ref[...]
            dx_ref[...] = r * (dxh - xh * jnp.mean(dxh * xh, axis=-1, keepdims=True))
            part = jnp.sum(dy * xh, axis=0, keepdims=True)

            @pl.when(i == 0)
            def _():
                dg_ref[...] = part

            @pl.when(i > 0)
            def _():
                dg_ref[...] += part

    rf = pl.BlockSpec((tr, wf), lambda i: (i, 0))
    rs = pl.BlockSpec((tr, ws), lambda i: (i, 0))
    vf = pl.BlockSpec((1, wf), lambda i: (0, 0))
    vs = pl.BlockSpec((1, ws), lambda i: (0, 0))
    return pl.pallas_call(
        body, name=name, grid=(L // tr,),
        in_specs=[pl.BlockSpec((tr, wf + ws), lambda i: (i, 0)), rf, rs, vf, vs],
        out_specs=[rf, rs, vf, vs],
        out_shape=[jax.ShapeDtypeStruct((L, wf), F32), jax.ShapeDtypeStruct((L, ws), F32),
                   jax.ShapeDtypeStruct((1, wf), F32), jax.ShapeDtypeStruct((1, ws), F32)],
        compiler_params=_cparams("arbitrary"),
    )(don, of, os_, gf, gs)


def _win_to_mine(w, md):
    d = w.shape[0]
    wf, ws, hd = md.wf, md.ws, HEAD_DIM
    o_z = 3 * wf
    o_sq = o_z + md.hf
    o_sk = o_sq + ws
    o_sv = o_sk + md.nkv * hd
    parts = [w[:, :3 * wf], w[:, o_sq:o_sq + ws]]
    for base in (o_sk, o_sv):
        for kv in range(md.nkv):
            blk = w[:, base + kv * hd:base + (kv + 1) * hd]
            parts += [blk, blk]
    z = w[:, o_z:o_z + md.hf].reshape(d, md.hf // 2, 2)
    z = jnp.pad(z, ((0, 0), (0, 0), (0, 6))).reshape(d, 4 * md.hf)
    parts.append(jnp.pad(z, ((0, 0), (0, LANES - 4 * md.hf + md.nup - md.nu))))
    return jnp.concatenate(parts, axis=1)


def _win_grad_to_ref(dw, md):
    d = dw.shape[0]
    wf, ws, hd = md.wf, md.ws, HEAD_DIM
    z = dw[:, md.bz * LANES:md.bz * LANES + 4 * md.hf].reshape(d, md.hf // 2, 8)[:, :, :2].reshape(d, md.hf)
    parts = [dw[:, :3 * wf], z, dw[:, md.bq_s * LANES:md.bq_s * LANES + ws]]
    for base in (md.bk_s, md.bv_s):
        for kv in range(md.nkv):
            c0 = (base + kv) * LANES
            parts.append(dw[:, c0:c0 + hd])
    return jnp.concatenate(parts, axis=1)


def _mix_small(p, md):
    tile2 = lambda v: jnp.tile(v.reshape(1, HEAD_DIM), (1, 2))
    b = p["b_forget"].reshape(md.hf // 2, 2)
    b = jnp.pad(b, ((0, 0), (0, 6))).reshape(1, 4 * md.hf)
    slopes = np.asarray(2.0 ** (-8.0 * np.arange(1, md.hq + 1) / md.hq), np.float32)
    return dict(
        g_mix=p["mix_norm"].reshape(1, -1),
        b_gate=jnp.pad(b, ((0, 0), (0, LANES - 4 * md.hf))),
        fqg=tile2(p["fox_q_norm"]), fkg=tile2(p["fox_k_norm"]),
        sqg=tile2(p["swa_q_norm"]), skg=tile2(p["swa_k_norm"]),
        sinkp=jnp.repeat(p["swa_sinks"], HEAD_DIM).reshape(1, md.ws),
        slopep=jnp.asarray(np.repeat(slopes, HEAD_DIM).reshape(1, md.ws)),
        gfo=p["fox_out_norm"].reshape(1, md.wf), gso=p["swa_out_norm"].reshape(1, md.ws),
    )


def _mix_fwd(h, sp, w, md, queue=None):
    L = h.shape[0]
    hn = _rms_fwd(h, sp["g_mix"], name="mix_rms")
    u = _mm(queue, hn, w(0), name="mix_u")
    cexp, ct = _gate_fwd(u, sp["b_gate"], md, name="gate_fwd")
    ct3 = ct[:8 * md.pf].reshape(md.pf, 8, L)
    of, lsef = _carried(queue, _fox_fwd, 2, u, cexp, ct3, sp["fqg"], sp["fkg"], md, name="fox_fwd")
    os_, lses = _carried(queue, _swa_fwd, 2, u, sp["sinkp"], sp["slopep"], sp["sqg"], sp["skg"], md, name="swa_fwd")
    on = _outnorm_fwd(of, os_, sp["gfo"], sp["gso"], name="outnorm_fwd")
    h_out = _mm(queue, on, w(1), name="mix_out", residual=h)
    return h_out, (h, hn, u, cexp, ct3, of, lsef, os_, lses, on)


def _mix_bwd(dh, saved, sp, w_in, w_out, md, queue=None):
    h, hn, u, cexp, ct3, of, lsef, os_, lses, on = saved
    L = h.shape[0]
    dhb = _cast_bf16(dh, name="mix_dhb")
    don = _matmul(dhb, w_out, name="mix_don", nt=True)
    dw_out = _matmul(_transpose(on, name="mix_onT"), dhb, name="mix_dwout", out_dtype=BF16)
    dof, dos, dgfo, dgso = _outnorm_bwd(don, of, os_, sp["gfo"], sp["gso"], name="outnorm_bwd")
    duq, duk, duv, dck, dfqg, dfkg = _carried(queue, _fox_bwd, 6, u, cexp, ct3, sp["fqg"], sp["fkg"], dof, lsef, md,
                                              name="fox_bwd")
    dsq, dsk, dsv, dsinkp, dsqg, dskg = _carried(queue, _swa_bwd, 6, u, sp["sinkp"], sp["slopep"], sp["sqg"], sp["skg"], dos,
                                                 lses, md, name="swa_bwd")
    dck_t = jnp.pad(dck.reshape(8 * md.pf, L), ((0, LANES - 8 * md.pf), (0, 0)))
    dz, db = _gate_bwd(u, sp["b_gate"], dck_t, md, name="gate_bwd")
    du = jnp.concatenate([duq, duk, duv, dsq, dsk, dsv, dz, jnp.zeros((L, md.nup - md.nu), F32)], axis=1).astype(BF16)
    dhn = _mm(queue, du, w_in, name="mix_dhn", nt=True)
    dw_in = _mm(queue, _transpose(hn, name="mix_hnT"), du, name="mix_dwin", out_dtype=BF16)
    dh_in, dg_mix = _rms_bwd(dhn, h, sp["g_mix"], dh, name="mix_drms")
    small = dict(
        mix_norm=dg_mix.reshape(-1),
        b_forget=db[0, :4 * md.hf].reshape(md.hf // 2, 8)[:, :2].reshape(md.hf),
        fox_q_norm=dfqg[0, :HEAD_DIM], fox_k_norm=dfkg[0, :HEAD_DIM],
        swa_q_norm=dsqg[0, :HEAD_DIM], swa_k_norm=dskg[0, :HEAD_DIM],
        swa_sinks=dsinkp[0, ::HEAD_DIM],
        fox_out_norm=dgfo.reshape(-1), swa_out_norm=dgso.reshape(-1),
    )
    return dh_in, dw_in, dw_out, small


_ANY = pl.BlockSpec(memory_space=pl.ANY)
_HALF_ROWS = (512, 352, 256, 192, 128, 64, 32, 16)


def _mesh_pos():
    return lax.axis_index("x"), lax.axis_index("y"), lax.axis_index("c")


def _other_chips(x, y):
    return [(1 - x, y), (x, 1 - y), (1 - x, 1 - y)]


def _rows_half(ref, which):
    rh = ref.shape[-2] // 2
    if len(ref.shape) == 2:
        return ref.at[pl.ds(which * rh, rh), :]
    return ref.at[:, pl.ds(which * rh, rh), :]


def _remote(src, dst, send_sems, recv_sems, idx, dev):
    return pltpu.make_async_remote_copy(src_ref=src, dst_ref=dst, send_sem=send_sems.at[idx], recv_sem=recv_sems.at[idx],
                                        device_id=dev, device_id_type=MESH)


def _cast_into_chunk(w, l_idx, k_idx, *, name):
    _, rows, cols = w.shape
    tr = _pick(rows, (512, 352, 256, 128, 64, 32, 16))

    def body(l_ref, k_ref, w_ref, o_ref):
        o_ref[...] = w_ref[...].astype(o_ref.dtype)

    return pl.pallas_call(
        body, name=name,
        grid_spec=pltpu.PrefetchScalarGridSpec(
            num_scalar_prefetch=2, grid=(rows // tr,),
            in_specs=[pl.BlockSpec((None, tr, cols), lambda i, l, k: (l[0], i, 0))],
            out_specs=pl.BlockSpec((None, tr, cols), lambda i, l, k: (k[0], i, 0))),
        out_shape=jax.ShapeDtypeStruct((NCH, rows, cols), BF16),
        compiler_params=_cparams("parallel"),
    )(l_idx, k_idx, w)


class _SiblingSwap:
    n_sems = 1

    def __init__(self, g):
        self.operands = [g]
        self.out_shapes = [jax.ShapeDtypeStruct((NCH, g.shape[1] // 2, g.shape[2]), g.dtype)]
        self.aliases = {}

    def copies(self, cin, cout, send_sems, recv_sems, base=0):
        x, y, c = _mesh_pos()
        cp = _remote(_rows_half(cin[0], 1 - c), cout[0], send_sems, recv_sems, base, (x, y, 1 - c))
        return [(cp, cp)]


class _ChipExchange:
    n_sems = NCH - 1

    def __init__(self, s, r0, r1, land=None):
        self.r0, self.r1 = r0, r1
        self.operands = [s] if land is None else [s, land]
        self.out_shapes = [jax.ShapeDtypeStruct((NCH - 1,) + s.shape[1:], s.dtype)]
        self.aliases = {} if land is None else {1: 0}

    def copies(self, cin, cout, send_sems, recv_sems, base=0):
        x, y, c = _mesh_pos()
        rows = pl.ds(self.r0, self.r1 - self.r0)
        cps = [_remote(cin[0].at[2 * cx + cy, rows, :], cout[0].at[j, rows, :], send_sems, recv_sems, base + j, (cx, cy, c))
               for j, (cx, cy) in enumerate(_other_chips(x, y))]
        return [(cp, cp) for cp in cps]


class _SiblingShare:
    n_sems = 1

    def __init__(self, tot):
        self.operands = [tot]
        self.out_shapes = [jax.ShapeDtypeStruct(tot.shape, tot.dtype)]
        self.aliases = {0: 0}

    def copies(self, cin, cout, send_sems, recv_sems, base=0):
        x, y, c = _mesh_pos()
        mine, theirs = _rows_half(cout[0], c), _rows_half(cout[0], 1 - c)
        return [(_remote(mine, mine, send_sems, recv_sems, base, (x, y, 1 - c)),
                 _remote(theirs, theirs, send_sems, recv_sems, base, (x, y, 1 - c)))]


class _Both:
    def __init__(self, a, b):
        self.a, self.b = a, b
        self.operands = a.operands + b.operands
        self.out_shapes = a.out_shapes + b.out_shapes
        self.aliases = dict(a.aliases)
        self.aliases.update({len(a.operands) + s: len(a.out_shapes) + d for s, d in b.aliases.items()})
        self.n_sems = a.n_sems + b.n_sems

    def copies(self, cin, cout, send_sems, recv_sems, base=0):
        na, nao = len(self.a.operands), len(self.a.out_shapes)
        return (self.a.copies(cin[:na], cout[:nao], send_sems, recv_sems, base)
                + self.b.copies(cin[na:], cout[nao:], send_sems, recv_sems, base + self.a.n_sems))


def _rows_quarter(ref, c, q):
    rq = ref.shape[0] // 4
    return ref.at[pl.ds((2 * c + q) * rq, rq), :]


class _GatherStep:
    n_sems = 7

    def __init__(self, stage1, stage2, stage3):
        given = [(st, b) for st, b in ((1, stage1), (2, stage2), (3, stage3)) if b is not None]
        self.stages = [st for st, _ in given]
        self.operands = [b for _, b in given]
        self.out_shapes = [jax.ShapeDtypeStruct(b.shape, b.dtype) for b in self.operands]
        self.aliases = {i: i for i in range(len(self.operands))}

    def copies(self, cin, cout, send_sems, recv_sems, base=0):
        x, y, c = _mesh_pos()
        k, kx, ky, kd = 2 * x + y, 2 * (1 - x) + y, 2 * x + (1 - y), 2 * (1 - x) + (1 - y)
        xn, yn, sibling = (1 - x, y, c), (x, 1 - y, c), (x, y, 1 - c)

        def pair(idx, dev, src, arrival):
            return (_remote(src, src, send_sems, recv_sems, base + idx, dev),
                    _remote(arrival, arrival, send_sems, recv_sems, base + idx, dev))

        out = []
        for stage, buf in zip(self.stages, cout):
            if stage == 1:
                mine = _rows_half(buf.at[k], c)
                out.append(pair(0, xn, mine, _rows_half(buf.at[kx], c)))
                out.append(pair(1, yn, mine, _rows_half(buf.at[ky], c)))
            elif stage == 2:
                out.append(pair(2, yn, _rows_quarter(buf.at[kx], c, 0), _rows_quarter(buf.at[kd], c, 0)))
                out.append(pair(3, xn, _rows_quarter(buf.at[ky], c, 1), _rows_quarter(buf.at[kd], c, 1)))
                out.append(pair(4, sibling, _rows_half(buf.at[kx], c), _rows_half(buf.at[kx], 1 - c)))
                out.append(pair(5, sibling, _rows_half(buf.at[ky], c), _rows_half(buf.at[ky], 1 - c)))
            else:
                out.append(pair(6, sibling, _rows_half(buf.at[kd], c), _rows_half(buf.at[kd], 1 - c)))
        return out


class _GatherQueue:
    def __init__(self, bufs):
        self.bufs = list(bufs)
        self.step = -3
        for _ in range(3):
            self.give(_run_exchange(self.take(), name="first_allgather"))

    def take(self):
        s = self.step
        self.step += 1
        at = [i if 0 <= i < len(self.bufs) else None for i in (s + 3, s + 2, s + 1)]
        self.cur = [i for i in at if i is not None]
        if not self.cur:
            return None
        return _GatherStep(*[None if i is None else self.bufs[i] for i in at])

    def give(self, outs):
        for i, buf in zip(self.cur, outs):
            self.bufs[i] = buf


def _run_exchange(job, *, name):
    n_in, n_out = len(job.operands), len(job.out_shapes)

    def body(*refs):
        cps = job.copies(refs[:n_in], refs[n_in:n_in + n_out], refs[-2], refs[-1])
        for cp, _ in cps:
            cp.start()
        for _, arrival in cps:
            arrival.wait_recv()
        for cp, _ in cps:
            cp.wait_send()

    return pl.pallas_call(
        body, name=name, in_specs=[_ANY] * n_in, out_specs=[_ANY] * n_out, out_shape=list(job.out_shapes),
        input_output_aliases=dict(job.aliases),
        scratch_shapes=[pltpu.SemaphoreType.DMA((job.n_sems,)), pltpu.SemaphoreType.DMA((job.n_sems,))],
    )(*job.operands)


class _ExchangeQueue:
    def __init__(self):
        self.lines = ([], [])
        self.cur = []

    def put(self, line, make_job, done):
        self.lines[line].append((make_job, done))

    def take(self):
        entries = [ln.pop(0) for ln in self.lines if ln]
        if not entries:
            return None
        jobs = [make() for make, _ in entries]
        self.cur = [(job, done) for job, (_, done) in zip(jobs, entries)]
        return jobs[0] if len(jobs) == 1 else _Both(*jobs)

    def give(self, outs):
        outs = list(outs)
        for job, done in self.cur:
            n = len(job.out_shapes)
            done(outs[:n])
            outs = outs[n:]

    def drain(self, name):
        while (job := self.take()) is not None:
            self.give(_run_exchange(job, name=name))


ICI_LINE, D2D_LINE = 0, 1


def _reduce_scatter_later(queue, parts, c_idx, k_idx, tag, done, n_parts=2):
    n = len(parts)
    st = dict(lands=[None] * n, sums=None, chip_lands=[None] * n, tots=[None] * n, left=n)

    def shared(i, outs):
        st['tots'][i] = outs[0]
        st['left'] -= 1
        if st['left'] == 0:
            done(st['tots'])

    def exchanged(i, outs):
        st['chip_lands'][i] = outs[0]
        st['left'] -= 1
        if st['left'] == 0:
            st['left'] = n
            for j in range(n):
                tot = _add_chunks(st['sums'][j], st['chip_lands'][j], k_idx, c_idx, name=f"{tag}_rs_add4")
                queue.put(D2D_LINE, lambda tot=tot: _SiblingShare(tot), lambda outs, j=j: shared(j, outs))

    def swapped(i, outs):
        st['lands'][i] = outs[0]
        st['left'] -= 1
        if st['left'] == 0:
            st['sums'] = [_add_own_half(p, l, c_idx, name=f"{tag}_rs_add2") for p, l in zip(parts, st['lands'])]
            st['left'] = n * n_parts
            for j in range(n):
                step = st['sums'][j].shape[1] // n_parts
                for p in range(n_parts):
                    queue.put(ICI_LINE,
                              lambda j=j, p=p, step=step: _ChipExchange(st['sums'][j], p * step, (p + 1) * step, st['chip_lands'][j]),
                              lambda outs, j=j: exchanged(j, outs))

    for i in range(n):
        queue.put(D2D_LINE, lambda i=i: _SiblingSwap(parts[i]), lambda outs, i=i: swapped(i, outs))


def _mm(queue, a, b, **kw):
    job = queue.take() if queue is not None else None
    if job is None:
        return _matmul(a, b, **kw)
    n_out = len(kw.get("out_dtypes") or (0,))
    outs = _matmul(a, b, comm=job, **kw)
    queue.give(outs[n_out:])
    return outs[0] if n_out == 1 else outs[:n_out]


def _allgather_devices(v, *, name):
    m = v.shape[0]

    def body(v_ref, out_ref, send_sems, recv_sems, local_sem):
        x, y, c = _mesh_pos()
        mine = 4 * x + 2 * y + c
        own = pltpu.make_async_copy(v_ref, out_ref.at[mine], local_sem)
        own.start()
        cps = []
        for r in range(1, 8):
            px, py, pc = (x + (r >> 2)) % 2, (y + ((r >> 1) & 1)) % 2, (c + (r & 1)) % 2
            cps.append((_remote(v_ref, out_ref.at[mine], send_sems, recv_sems, r - 1, (px, py, pc)), 4 * px + 2 * py + pc))
        for cp, _ in cps:
            cp.start()
        for r, (cp, theirs) in enumerate(cps):
            blk = out_ref.at[theirs]
            _remote(blk, blk, send_sems, recv_sems, r, (x, y, c)).wait_recv()
        for cp, _ in cps:
            cp.wait_send()
        own.wait()

    return pl.pallas_call(
        body, name=name, in_specs=[_ANY], out_specs=_ANY,
        out_shape=jax.ShapeDtypeStruct((8, m, LANES), v.dtype),
        scratch_shapes=[pltpu.SemaphoreType.DMA((7,)), pltpu.SemaphoreType.DMA((7,)), pltpu.SemaphoreType.DMA],
    )(v)


def _add_own_half(g, land, c_idx, *, name):
    nch, rh, cols = land.shape
    tr = _pick(rh, _HALF_ROWS)
    nt = rh // tr

    def body(c_ref, g_ref, l_ref, o_ref):
        o_ref[...] = (g_ref[...].astype(F32) + l_ref[...].astype(F32)).astype(o_ref.dtype)

    return pl.pallas_call(
        body, name=name,
        grid_spec=pltpu.PrefetchScalarGridSpec(
            num_scalar_prefetch=1, grid=(nch, nt),
            in_specs=[pl.BlockSpec((None, tr, cols), lambda k, i, c: (k, c[0] * nt + i, 0)),
                      pl.BlockSpec((None, tr, cols), lambda k, i, c: (k, i, 0))],
            out_specs=pl.BlockSpec((None, tr, cols), lambda k, i, c: (k, i, 0))),
        out_shape=jax.ShapeDtypeStruct(land.shape, BF16),
        compiler_params=_cparams("parallel", "parallel"),
    )(c_idx, g, land)


def _add_chunks(s, land, k_idx, c_idx, *, name):
    _, rh, cols = s.shape
    tr = _pick(rh, _HALF_ROWS)
    nt = rh // tr

    def body(k_ref, c_ref, s_ref, l_ref, o_ref):
        t = s_ref[...].astype(F32)
        for j in range(NCH - 1):
            t = t + l_ref[j].astype(F32)
        o_ref[...] = t

    return pl.pallas_call(
        body, name=name,
        grid_spec=pltpu.PrefetchScalarGridSpec(
            num_scalar_prefetch=2, grid=(nt,),
            in_specs=[pl.BlockSpec((None, tr, cols), lambda i, k, c: (k[0], i, 0)),
                      pl.BlockSpec((NCH - 1, tr, cols), lambda i, k, c: (0, i, 0))],
            out_specs=pl.BlockSpec((tr, cols), lambda i, k, c: (c[0] * nt + i, 0))),
        out_shape=jax.ShapeDtypeStruct((2 * rh, cols), F32),
        compiler_params=_cparams("parallel"),
    )(k_idx, c_idx, s, land)


def _sum_devices(v, *, name):
    _, m, _ = v.shape

    def body(v_ref, o_ref):
        t = v_ref[0]
        for d in range(1, 8):
            t = t + v_ref[d]
        o_ref[...] = t

    return pl.pallas_call(
        body, name=name, grid=(1,),
        in_specs=[pl.BlockSpec((8, m, LANES), lambda i: (0, 0, 0))],
        out_specs=pl.BlockSpec((m, LANES), lambda i: (0, 0)),
        out_shape=jax.ShapeDtypeStruct((m, LANES), F32),
        compiler_params=_cparams("arbitrary"),
    )(v)


def _adamw_math(w, g, m, v):
    m = ADAM_B1 * m + (1.0 - ADAM_B1) * g
    v = ADAM_B2 * v + (1.0 - ADAM_B2) * (g * g)
    m_hat = m / (1.0 - ADAM_B1 ** ADAM_STEP)
    v_hat = v / (1.0 - ADAM_B2 ** ADAM_STEP)
    delta = -ADAM_LR * (m_hat / (jnp.sqrt(v_hat) + ADAM_EPS) + ADAM_WD * w)
    return delta, m, v


def _adamw_layer(w, m, v, g, layer, prev, *, name):
    depth, rows, cols = w.shape
    tr = _pick(rows, (256, 128, 64, 32, 16, 8))
    lay = pl.BlockSpec((None, tr, cols), lambda i, l: (l[0], i, 0))
    n_prev = 0 if prev is None else 4

    def body(l_ref, w_ref, m_ref, v_ref, g_ref, *rest):
        go_ref, d_ref, mo_ref, vo_ref = rest[n_prev:]
        g = g_ref[...]
        delta, m_new, v_new = _adamw_math(w_ref[...], g, m_ref[...], v_ref[...])
        go_ref[...] = g
        d_ref[...] = delta
        mo_ref[...] = m_new
        vo_ref[...] = v_new

    stack = jax.ShapeDtypeStruct(w.shape, F32)
    return pl.pallas_call(
        body, name=name,
        grid_spec=pltpu.PrefetchScalarGridSpec(
            num_scalar_prefetch=1, grid=(rows // tr,),
            in_specs=[lay, lay, lay, pl.BlockSpec((tr, cols), lambda i, l: (i, 0))] + [_ANY] * n_prev,
            out_specs=[lay] * 4),
        out_shape=[stack] * 4,
        input_output_aliases={} if prev is None else {5 + q: q for q in range(4)},
        compiler_params=_cparams("parallel"),
    )(layer, w, m, v, g, *(() if prev is None else prev))


def _adamw_flat(w, g, m, v, *, name):
    def body(w_ref, g_ref, m_ref, v_ref, d_ref, mo_ref, vo_ref):
        d_ref[...], mo_ref[...], vo_ref[...] = _adamw_math(w_ref[...], g_ref[...], m_ref[...], v_ref[...])

    blk = pl.BlockSpec(w.shape, lambda i: (0, 0))
    return pl.pallas_call(
        body, name=name, grid=(1,), in_specs=[blk] * 4, out_specs=[blk] * 3,
        out_shape=[jax.ShapeDtypeStruct(w.shape, F32)] * 3, compiler_params=_cparams("arbitrary"),
    )(w, g, m, v)


_WEIGHTS = ('meta_tokens', 'ffn1_norm', 'ffn1_w_gate', 'ffn1_w_up', 'ffn1_w_down', 'mix_norm', 'w_in', 'b_forget',
            'fox_q_norm', 'fox_k_norm', 'swa_q_norm', 'swa_k_norm', 'swa_sinks', 'fox_out_norm', 'swa_out_norm', 'w_out',
            'ffn2_norm', 'ffn2_w_gate', 'ffn2_w_up', 'ffn2_w_down')
_BIG = ('ffn1_w_gate', 'ffn1_w_up', 'ffn1_w_down', 'w_in', 'w_out', 'ffn2_w_gate', 'ffn2_w_up', 'ffn2_w_down')
_SMALL = tuple(n for n in _WEIGHTS if n not in _BIG and n != 'meta_tokens')
_MIX_SMALL = ('mix_norm', 'b_forget', 'fox_q_norm', 'fox_k_norm', 'swa_q_norm', 'swa_k_norm', 'swa_sinks',
              'fox_out_norm', 'swa_out_norm')


def _pack_rows(vectors):
    flat = jnp.concatenate([v.reshape(-1) for v in vectors])
    n = flat.shape[0]
    m = -(-n // (8 * LANES)) * 8
    return jnp.pad(flat, (0, m * LANES - n)).reshape(m, LANES)


def _unpack_rows(packed, shapes):
    flat = packed.reshape(-1)
    out, o = [], 0
    for s in shapes:
        n = int(np.prod(s))
        out.append(flat[o:o + n].reshape(s))
        o += n
    return out


def kernel(x, meta_tokens, ffn1_norm, ffn1_w_gate, ffn1_w_up, ffn1_w_down, mix_norm, w_in, b_forget, fox_q_norm, fox_k_norm, swa_q_norm, swa_k_norm, swa_sinks, fox_out_norm, swa_out_norm, w_out, ffn2_norm, ffn2_w_gate, ffn2_w_up, ffn2_w_down, loss_target, m_meta_tokens, m_ffn1_norm, m_ffn1_w_gate, m_ffn1_w_up, m_ffn1_w_down, m_mix_norm, m_w_in, m_b_forget, m_fox_q_norm, m_fox_k_norm, m_swa_q_norm, m_swa_k_norm, m_swa_sinks, m_fox_out_norm, m_swa_out_norm, m_w_out, m_ffn2_norm, m_ffn2_w_gate, m_ffn2_w_up, m_ffn2_w_down, v_meta_tokens, v_ffn1_norm, v_ffn1_w_gate, v_ffn1_w_up, v_ffn1_w_down, v_mix_norm, v_w_in, v_b_forget, v_fox_q_norm, v_fox_k_norm, v_swa_q_norm, v_swa_k_norm, v_swa_sinks, v_fox_out_norm, v_swa_out_norm, v_w_out, v_ffn2_norm, v_ffn2_w_gate, v_ffn2_w_up, v_ffn2_w_down):
    W = dict(meta_tokens=meta_tokens, ffn1_norm=ffn1_norm, ffn1_w_gate=ffn1_w_gate, ffn1_w_up=ffn1_w_up, ffn1_w_down=ffn1_w_down, mix_norm=mix_norm, w_in=w_in, b_forget=b_forget, fox_q_norm=fox_q_norm, fox_k_norm=fox_k_norm, swa_q_norm=swa_q_norm, swa_k_norm=swa_k_norm, swa_sinks=swa_sinks, fox_out_norm=fox_out_norm, swa_out_norm=swa_out_norm, w_out=w_out, ffn2_norm=ffn2_norm, ffn2_w_gate=ffn2_w_gate, ffn2_w_up=ffn2_w_up, ffn2_w_down=ffn2_w_down)
    Mo = dict(meta_tokens=m_meta_tokens, ffn1_norm=m_ffn1_norm, ffn1_w_gate=m_ffn1_w_gate, ffn1_w_up=m_ffn1_w_up, ffn1_w_down=m_ffn1_w_down, mix_norm=m_mix_norm, w_in=m_w_in, b_forget=m_b_forget, fox_q_norm=m_fox_q_norm, fox_k_norm=m_fox_k_norm, swa_q_norm=m_swa_q_norm, swa_k_norm=m_swa_k_norm, swa_sinks=m_swa_sinks, fox_out_norm=m_fox_out_norm, swa_out_norm=m_swa_out_norm, w_out=m_w_out, ffn2_norm=m_ffn2_norm, ffn2_w_gate=m_ffn2_w_gate, ffn2_w_up=m_ffn2_w_up, ffn2_w_down=m_ffn2_w_down)
    Vo = dict(meta_tokens=v_meta_tokens, ffn1_norm=v_ffn1_norm, ffn1_w_gate=v_ffn1_w_gate, ffn1_w_up=v_ffn1_w_up, ffn1_w_down=v_ffn1_w_down, mix_norm=v_mix_norm, w_in=v_w_in, b_forget=v_b_forget, fox_q_norm=v_fox_q_norm, fox_k_norm=v_fox_k_norm, swa_q_norm=v_swa_q_norm, swa_k_norm=v_swa_k_norm, swa_sinks=v_swa_sinks, fox_out_norm=v_fox_out_norm, swa_out_norm=v_swa_out_norm, w_out=v_w_out, ffn2_norm=v_ffn2_norm, ffn2_w_gate=v_ffn2_w_gate, ffn2_w_up=v_ffn2_w_up, ffn2_w_down=v_ffn2_w_down)

    _, S, D = x.shape
    L = S + BLOCK
    depth = ffn1_norm.shape[0]
    md = _MixDims(D)
    mx, my, mc = _mesh_pos()
    k_idx = (2 * mx + my).astype(jnp.int32).reshape(1)
    c_idx = mc.astype(jnp.int32).reshape(1)
    dcols = D // NCH

    meta_all = _allgather_devices(meta_tokens.reshape(-1, LANES), name="meta_allgather")
    meta_full = jnp.transpose(meta_all[0::2].reshape(NCH, N_META, dcols), (1, 0, 2)).reshape(N_META, D)

    order = ('ffn1_w_gate', 'ffn1_w_up', 'ffn1_w_down', 'w_in', 'w_out', 'ffn2_w_gate', 'ffn2_w_up', 'ffn2_w_down')
    chunks = []
    for l in range(depth):
        l_idx = jnp.full((1,), l, jnp.int32)
        chunks += [_cast_into_chunk(W[name], l_idx, k_idx, name="cast_chunk") for name in order]
    gather = _GatherQueue(chunks)
    wts = [{} for _ in range(depth)]

    def weight(l, i):
        if i not in wts[l]:
            buf = gather.bufs[len(order) * l + i]
            if order[i] == 'w_in':
                buf = _win_to_mine(jnp.transpose(buf, (1, 0, 2)).reshape(D, NCH * buf.shape[2]), md)
            elif order[i] in ('w_out', 'ffn1_w_down', 'ffn2_w_down'):
                buf = buf.reshape(-1, D)
            wts[l][i] = buf
        return wts[l][i]

    h = jnp.concatenate([jnp.zeros((PAD, D), F32), meta_full, x[0]], axis=0)
    saved = []
    for l in range(depth):
        sp = _mix_small({n: W[n][l] for n in _MIX_SMALL}, md)
        h, s1 = _ffn_fwd(h, ffn1_norm[l].reshape(1, D), lambda i: weight(l, i), "ffn", gather)
        h, s2 = _mix_fwd(h, sp, lambda i: weight(l, 3 + i), md, gather)
        h, s3 = _ffn_fwd(h, ffn2_norm[l].reshape(1, D), lambda i: weight(l, 5 + i), "ffn", gather)
        saved.append((s1, s2, s3, sp))
    wts = [dict(g1=w[0], u1=w[1], d1=w[2], wi=w[3], wo=w[4], g2=w[5], u2=w[6], d2=w[7]) for w in wts]

    loss_part, dh = _loss_grad(h, loss_target[0], name="loss_grad")

    small_grads = {n: [None] * depth for n in _SMALL}
    stacks = {n: None for n in _BIG}

    def update(name, l, grad):
        stacks[name] = _adamw_layer(W[name], Mo[name], Vo[name], grad, jnp.full((1,), l, jnp.int32), stacks[name],
                                    name="adamw_layer")

    queue = _ExchangeQueue()

    def reduce_later(names, l, parts, tag):
        def done(grads):
            for n, grad in zip(names, grads):
                update(n, l, grad)
        _reduce_scatter_later(queue, parts, c_idx, k_idx, tag, done)

    for l in range(depth - 1, -1, -1):
        wl = wts[l]
        s1, s2, s3, sp = saved[l]
        dh, dg, dwg, dwu, dwd = _ffn_bwd(dh, s3, ffn2_norm[l].reshape(1, D), wl['g2'], wl['u2'], wl['d2'], "ffn", queue)
        small_grads['ffn2_norm'][l] = dg.reshape(-1)
        reduce_later(('ffn2_w_gate', 'ffn2_w_up', 'ffn2_w_down'), l, [dwg, dwu, dwd.reshape(NCH, -1, D)], "ffn")

        dh, dwi, dwo, sm = _mix_bwd(dh, s2, sp, wl['wi'], wl['wo'], md, queue)
        for n in _MIX_SMALL:
            small_grads[n][l] = sm[n]
        dwi = _win_grad_to_ref(dwi, md)
        dwi = jnp.transpose(dwi.reshape(D, NCH, -1), (1, 0, 2))
        reduce_later(('w_in', 'w_out'), l, [dwi, dwo.reshape(NCH, -1, D)], "mix")

        dh, dg, dwg, dwu, dwd = _ffn_bwd(dh, s1, ffn1_norm[l].reshape(1, D), wl['g1'], wl['u1'], wl['d1'], "ffn", queue)
        small_grads['ffn1_norm'][l] = dg.reshape(-1)
        reduce_later(('ffn1_w_gate', 'ffn1_w_up', 'ffn1_w_down'), l, [dwg, dwu, dwd.reshape(NCH, -1, D)], "ffn")
    queue.drain("rs_chips")

    grad_x = dh[BLOCK:][None]

    small_shapes = [W[n].shape for n in _SMALL]
    parts = [jnp.stack(small_grads[n]) for n in _SMALL] + [dh[PAD:BLOCK], loss_part[0, :1]]
    packed = _pack_rows(parts)
    total = _sum_devices(_allgather_devices(packed, name="small_allgather"), name="small_sum")
    *g_small, g_meta, loss = _unpack_rows(total, small_shapes + [(N_META, D), (1,)])
    g_meta = lax.dynamic_slice(g_meta, (0, k_idx[0] * dcols), (N_META, dcols))

    sw = _pack_rows([W[n] for n in _SMALL])
    sd, smm, svv = _adamw_flat(sw, _pack_rows(g_small), _pack_rows([Mo[n] for n in _SMALL]),
                               _pack_rows([Vo[n] for n in _SMALL]), name="adamw_small")
    d_small, m_small, v_small = (_unpack_rows(t, small_shapes) for t in (sd, smm, svv))
    d_meta, m_meta, v_meta = _adamw_flat(meta_tokens, g_meta, m_meta_tokens, v_meta_tokens, name="adamw_meta")

    grads, deltas, new_m, new_v = {}, {}, {}, {}
    for n in _BIG:
        grads[n], deltas[n], new_m[n], new_v[n] = stacks[n]
    for i, n in enumerate(_SMALL):
        grads[n], deltas[n], new_m[n], new_v[n] = g_small[i], d_small[i], m_small[i], v_small[i]
    grads['meta_tokens'], deltas['meta_tokens'], new_m['meta_tokens'], new_v['meta_tokens'] = g_meta, d_meta, m_meta, v_meta
    return (loss.reshape(()), grad_x, *[grads[n] for n in _WEIGHTS], *[deltas[n] for n in _WEIGHTS],
            *[new_m[n] for n in _WEIGHTS], *[new_v[n] for n in _WEIGHTS])
```

```python
import numpy as np
import jax
import jax.numpy as jnp
from jax import lax
from jax.experimental import pallas as pl
from jax.experimental.pallas import tpu as pltpu

F32 = jnp.float32
BF16 = jnp.bfloat16

HEAD_DIM = 64
N_META = 16
BLOCK = 128
WINDOW = 128
PAD = BLOCK - N_META
EPS = 1e-6
NEG_INF = -1e30
SWA_GROUP = 8
NCH = 4
LANES = 128
QBLOCK = 512

ADAM_LR = 0.001
ADAM_B1 = 0.9
ADAM_B2 = 0.999
ADAM_EPS = 1e-08
ADAM_WD = 0.01
ADAM_STEP = 10

V7X_VMEM_BYTES = 64 * 1024 * 1024
VMEM_LIMIT = V7X_VMEM_BYTES - 8 * 1024 * 1024
MESH = pl.DeviceIdType.MESH
HIGHEST = lax.Precision.HIGHEST

_TM = (1088, 1024, 704, 512, 384, 256, 128)
_TN = (1408, 1024, 768, 512, 384, 256, 128)
_TK = (2176, 1408, 1024, 512, 384, 256, 128)
_TR = (544, 512, 384, 272, 256, 128)


def _pick(n, cands):
    for c in cands:
        if n % c == 0:
            return c
    return n


def _cparams(*sem):
    return pltpu.CompilerParams(dimension_semantics=sem if sem else None, vmem_limit_bytes=VMEM_LIMIT)


def _matmul(a, b, *, name, nt=False, b_chunked=False, out_chunked=False, out_dtype=F32,
            residual=None, scale=1.0, extras=(), epilogue=None, out_dtypes=None, comm=None):
    M, K = a.shape
    if not nt:
        N = b.shape[-1] * (NCH if b_chunked else 1)
        assert b.shape[-2] == K
        k_unit = K
    else:
        N = b.shape[-2]
        k_unit = b.shape[-1]
        assert k_unit * (NCH if b_chunked else 1) == K
    n_unit = N // NCH if (out_chunked or (b_chunked and not nt)) else N
    tm, tn, tk = _pick(M, _TM), _pick(n_unit, _TN), _pick(k_unit, _TK)
    if epilogue is None:
        extras = () if residual is None else (residual,)
        out_dtypes = (out_dtype,)

        def epilogue(acc, *res):
            r = acc * scale if scale != 1.0 else acc
            return (r + res[0] if res else r,)
    n_out = len(out_dtypes)
    n_temps = 4 if n_out > 1 else 0
    assert not (extras and out_chunked)

    def est(tm_):
        return (2 * tm_ * tk * 2 + 2 * tk * tn * 2 + tm_ * tn * 4
                + sum(2 * tm_ * tn * jnp.dtype(d).itemsize for d in out_dtypes)
                + sum(2 * tm_ * tn * e.dtype.itemsize for e in extras) + n_temps * tm_ * tn * 4)

    while est(tm) > VMEM_LIMIT * 3 // 4 and tm % 32 == 0:
        tm //= 2
    npc, kpc = n_unit // tn, k_unit // tk
    nk = K // tk
    grid = (M // tm, N // tn, nk)

    a_spec = pl.BlockSpec((tm, tk), lambda i, j, k: (i, k))
    if not nt:
        if b_chunked:
            b_spec = pl.BlockSpec((None, tk, tn), lambda i, j, k: (j // npc, k, j % npc))
        else:
            b_spec = pl.BlockSpec((tk, tn), lambda i, j, k: (k, j))
        dims = (((1,), (0,)), ((), ()))
    else:
        if b_chunked:
            b_spec = pl.BlockSpec((None, tn, tk), lambda i, j, k: (k // kpc, j, k % kpc))
        else:
            b_spec = pl.BlockSpec((tn, tk), lambda i, j, k: (j, k))
        dims = (((1,), (1,)), ((), ()))
    tile = pl.BlockSpec((tm, tn), lambda i, j, k: (i, j))
    if out_chunked:
        o_spec = pl.BlockSpec((None, tm, tn), lambda i, j, k: (j // npc, i, j % npc))
        out_shapes = [jax.ShapeDtypeStruct((NCH, M, n_unit), d) for d in out_dtypes]
    else:
        o_spec = tile
        out_shapes = [jax.ShapeDtypeStruct((M, N), d) for d in out_dtypes]
    in_specs = [a_spec, b_spec] + [tile] * len(extras)
    args = [a, b, *extras]

    n_main = len(args)
    n_cin = 0 if comm is None else len(comm.operands)
    n_cout = 0 if comm is None else len(comm.out_shapes)

    def body(*refs):
        a_ref, b_ref = refs[0], refs[1]
        e_refs = refs[2:n_main]
        o_refs = refs[n_main + n_cin:n_main + n_cin + n_out]
        acc_ref = refs[n_main + n_cin + n_out + n_cout]
        i, j, k = pl.program_id(0), pl.program_id(1), pl.program_id(2)
        if comm is not None:
            cin = refs[n_main:n_main + n_cin]
            cout = refs[n_main + n_cin + n_out:n_main + n_cin + n_out + n_cout]
            send_sems, recv_sems = refs[-2:]

            @pl.when((i == 0) & (j == 0) & (k == 0))
            def _():
                for cp, _ in comm.copies(cin, cout, send_sems, recv_sems):
                    cp.start()

        @pl.when(k == 0)
        def _():
            acc_ref[...] = jnp.zeros_like(acc_ref)

        acc_ref[...] += lax.dot_general(a_ref[...], b_ref[...], dims, preferred_element_type=F32)

        @pl.when(k == nk - 1)
        def _():
            for o_ref, val in zip(o_refs, epilogue(acc_ref[...], *[e[...] for e in e_refs])):
                o_ref[...] = val.astype(o_ref.dtype)

        if comm is not None:
            @pl.when((i == grid[0] - 1) & (j == grid[1] - 1) & (k == nk - 1))
            def _():
                cps = comm.copies(cin, cout, send_sems, recv_sems)
                for _, arrival in cps:
                    arrival.wait_recv()
                for cp, _ in cps:
                    cp.wait_send()

    scratch = [pltpu.VMEM((tm, tn), F32)]
    if comm is None:
        outs = pl.pallas_call(
            body, name=name, grid=grid, in_specs=in_specs, out_specs=[o_spec] * n_out, out_shape=out_shapes,
            scratch_shapes=scratch, compiler_params=_cparams("parallel", "parallel", "arbitrary"),
        )(*args)
    else:
        scratch += [pltpu.SemaphoreType.DMA((comm.n_sems,)), pltpu.SemaphoreType.DMA((comm.n_sems,))]
        outs = pl.pallas_call(
            body, name=name, grid=grid, in_specs=in_specs + [_ANY] * n_cin,
            out_specs=[o_spec] * n_out + [_ANY] * n_cout, out_shape=out_shapes + list(comm.out_shapes),
            input_output_aliases={n_main + s: n_out + d for s, d in comm.aliases.items()},
            scratch_shapes=scratch, compiler_params=_cparams("arbitrary", "arbitrary", "arbitrary"),
        )(*args, *comm.operands)
    return outs[0] if len(outs) == 1 else tuple(outs)


def _call_carrying(body, comm, args, *, name, grid, in_specs, out_specs, out_shape, scratch_shapes):
    if comm is None:
        return pl.pallas_call(body, name=name, grid=grid, in_specs=in_specs, out_specs=out_specs, out_shape=out_shape,
                              scratch_shapes=scratch_shapes, compiler_params=_cparams("arbitrary"))(*args)
    n_in, n_out, n_scr = len(in_specs), len(out_specs), len(scratch_shapes)
    n_cin, n_cout = len(comm.operands), len(comm.out_shapes)

    def carrying(*refs):
        ins, cin = refs[:n_in], refs[n_in:n_in + n_cin]
        outs = refs[n_in + n_cin:n_in + n_cin + n_out]
        cout = refs[n_in + n_cin + n_out:n_in + n_cin + n_out + n_cout]
        scr = refs[n_in + n_cin + n_out + n_cout:n_in + n_cin + n_out + n_cout + n_scr]
        send_sems, recv_sems = refs[-2:]
        step = pl.program_id(0)

        @pl.when(step == 0)
        def _():
            for cp, _ in comm.copies(cin, cout, send_sems, recv_sems):
                cp.start()

        body(*ins, *outs, *scr)

        @pl.when(step == grid[0] - 1)
        def _():
            cps = comm.copies(cin, cout, send_sems, recv_sems)
            for _, arrival in cps:
                arrival.wait_recv()
            for cp, _ in cps:
                cp.wait_send()

    return pl.pallas_call(
        carrying, name=name, grid=grid, in_specs=list(in_specs) + [_ANY] * n_cin,
        out_specs=list(out_specs) + [_ANY] * n_cout, out_shape=list(out_shape) + list(comm.out_shapes),
        input_output_aliases={n_in + s: n_out + d for s, d in comm.aliases.items()},
        scratch_shapes=list(scratch_shapes) + [pltpu.SemaphoreType.DMA((comm.n_sems,)), pltpu.SemaphoreType.DMA((comm.n_sems,))],
        compiler_params=_cparams("arbitrary"),
    )(*args, *comm.operands)


def _carried(queue, fn, n_out, *args, **kw):
    job = queue.take() if queue is not None else None
    outs = fn(*args, comm=job, **kw)
    if job is not None:
        queue.give(outs[n_out:])
    return outs[:n_out]


def _transpose(x, *, name):
    M, N = x.shape
    tc = _pick(N, (512, 384, 256, 128))

    def body(x_ref, o_ref):
        o_ref[...] = x_ref[...].astype(F32).T.astype(o_ref.dtype)

    return pl.pallas_call(
        body, name=name, grid=(N // tc,),
        in_specs=[pl.BlockSpec((M, tc), lambda j: (0, j))],
        out_specs=pl.BlockSpec((tc, M), lambda j: (j, 0)),
        out_shape=jax.ShapeDtypeStruct((N, M), x.dtype),
        compiler_params=_cparams("parallel"),
    )(x)


def _rms_fwd(h, g, *, name):
    L, D = h.shape
    tr = _pick(L, _TR)

    def body(h_ref, g_ref, o_ref):
        x = h_ref[...]
        r = lax.rsqrt(jnp.mean(x * x, axis=-1, keepdims=True) + EPS)
        o_ref[...] = (x * r * g_ref[...]).astype(o_ref.dtype)

    return pl.pallas_call(
        body, name=name, grid=(L // tr,),
        in_specs=[pl.BlockSpec((tr, D), lambda i: (i, 0)), pl.BlockSpec((1, D), lambda i: (0, 0))],
        out_specs=pl.BlockSpec((tr, D), lambda i: (i, 0)),
        out_shape=jax.ShapeDtypeStruct((L, D), BF16),
        compiler_params=_cparams("parallel"),
    )(h, g)


def _rms_bwd(dy, h, g, dh, *, name):
    L, D = h.shape
    tr = _pick(L, _TR)

    def body(dy_ref, h_ref, g_ref, dh_ref, o_ref, dg_ref):
        i = pl.program_id(0)
        x = h_ref[...]
        dyv = dy_ref[...]
        r = lax.rsqrt(jnp.mean(x * x, axis=-1, keepdims=True) + EPS)
        xh = x * r
        dxh = dyv * g_ref[...]
        dx = r * (dxh - xh * jnp.mean(dxh * xh, axis=-1, keepdims=True))
        o_ref[...] = dh_ref[...] + dx
        part = jnp.sum(dyv * xh, axis=0, keepdims=True)

        @pl.when(i == 0)
        def _():
            dg_ref[...] = part

        @pl.when(i > 0)
        def _():
            dg_ref[...] += part

    row = pl.BlockSpec((tr, D), lambda i: (i, 0))
    vec = pl.BlockSpec((1, D), lambda i: (0, 0))
    return pl.pallas_call(
        body, name=name, grid=(L // tr,),
        in_specs=[row, row, vec, row], out_specs=[row, vec],
        out_shape=[jax.ShapeDtypeStruct((L, D), F32), jax.ShapeDtypeStruct((1, D), F32)],
        compiler_params=_cparams("arbitrary"),
    )(dy, h, g, dh)


def _swiglu_epilogue(up, gate):
    g = gate.astype(F32)
    return up, g * jax.nn.sigmoid(g) * up


def _swiglu_bwd_epilogue(acc, gate, up):
    d = 0.5 * acc
    g = gate.astype(F32)
    sg = jax.nn.sigmoid(g)
    return d * up.astype(F32) * sg * (1.0 + g * (1.0 - sg)), d * g * sg


def _cast_bf16(x, *, name):
    L, D = x.shape
    tr = _pick(L, _TR)

    def body(x_ref, o_ref):
        o_ref[...] = x_ref[...].astype(o_ref.dtype)

    blk = pl.BlockSpec((tr, D), lambda i: (i, 0))
    return pl.pallas_call(
        body, name=name, grid=(L // tr,), in_specs=[blk], out_specs=blk,
        out_shape=jax.ShapeDtypeStruct((L, D), BF16), compiler_params=_cparams("parallel"),
    )(x)


def _loss_grad(h, target, *, name):
    L, D = h.shape
    S = target.shape[0]
    nb = L // BLOCK

    def body(h_ref, t_ref, loss_ref, dh_ref):
        i = pl.program_id(0)

        @pl.when(i == 0)
        def _():
            loss_ref[...] = jnp.zeros_like(loss_ref)
            dh_ref[...] = jnp.zeros_like(dh_ref)

        @pl.when(i > 0)
        def _():
            err = h_ref[...] - t_ref[...]
            dh_ref[...] = err * (1.0 / D)
            loss_ref[...] += jnp.full(loss_ref.shape, (0.5 / D) * jnp.sum(err * err), F32)

    return pl.pallas_call(
        body, name=name, grid=(nb,),
        in_specs=[pl.BlockSpec((BLOCK, D), lambda i: (i, 0)),
                  pl.BlockSpec((BLOCK, D), lambda i: (jnp.maximum(i - 1, 0), 0))],
        out_specs=[pl.BlockSpec((1, LANES), lambda i: (0, 0)), pl.BlockSpec((BLOCK, D), lambda i: (i, 0))],
        out_shape=[jax.ShapeDtypeStruct((1, LANES), F32), jax.ShapeDtypeStruct((L, D), F32)],
        compiler_params=_cparams("arbitrary"),
    )(h, target)


def _ffn_fwd(h, g, w, tag, queue=None):
    hn = _rms_fwd(h, g, name=f"{tag}_rms")
    gate = _mm(queue, hn, w(0), name=f"{tag}_gate", b_chunked=True, out_dtype=BF16)
    up, act = _mm(queue, hn, w(1), name=f"{tag}_up", b_chunked=True, extras=(gate,), epilogue=_swiglu_epilogue,
                  out_dtypes=(BF16, BF16))
    h_out = _mm(queue, act, w(2), name=f"{tag}_down", residual=h, scale=0.5)
    return h_out, (h, hn, gate, up, act)


def _ffn_bwd(dh, saved, g, wg, wu, wd, tag, queue=None, on_dw=None):
    h, hn, gate, up, act = saved
    dout = _cast_bf16(dh, name=f"{tag}_dout")
    dgate, dup = _mm(queue, dout, wd, name=f"{tag}_dact", nt=True, extras=(gate, up), epilogue=_swiglu_bwd_epilogue,
                     out_dtypes=(BF16, BF16))
    actT = _transpose(act, name=f"{tag}_actT")
    on_dw(2, _mm(queue, actT, dout, name=f"{tag}_dwd", out_dtype=BF16, scale=0.5))
    hnT = _transpose(hn, name=f"{tag}_hnT")
    on_dw(0, _mm(queue, hnT, dgate, name=f"{tag}_dwg", out_chunked=True, out_dtype=BF16))
    on_dw(1, _mm(queue, hnT, dup, name=f"{tag}_dwu", out_chunked=True, out_dtype=BF16))
    dhn = _mm(queue, dgate, wg, name=f"{tag}_dhn_g", nt=True, b_chunked=True)
    dhn = _mm(queue, dup, wu, name=f"{tag}_dhn_u", nt=True, b_chunked=True, residual=dhn)
    return _rms_bwd(dhn, h, g, dh, name=f"{tag}_drms")


class _MixDims:
    def __init__(self, d_model):
        self.wf = d_model // 2
        self.ws = d_model // 2
        self.pf = self.wf // LANES
        self.ps = self.ws // LANES
        self.hf = self.wf // HEAD_DIM
        self.hq = self.ws // HEAD_DIM
        self.nkv = max(1, self.hq // SWA_GROUP)
        self.g = self.hq // self.nkv
        self.bq_f, self.bk_f, self.bv_f = 0, self.pf, 2 * self.pf
        self.bq_s = 3 * self.pf
        self.bk_s = self.bq_s + self.ps
        self.bv_s = self.bk_s + self.nkv
        self.bz = self.bv_s + self.nkv
        self.nu = (self.bz + 1) * LANES
        self.nup = -(-self.nu // 512) * 512
        self.in_width = 3 * self.wf + self.hf + self.ws + 2 * self.nkv * HEAD_DIM
        assert self.hf <= 2 * (LANES // 8)

    def gate_lane(self, h):
        return 8 * (h // 2) + h % 2

    def column_map(self):
        wf, ws, hd = self.wf, self.ws, HEAD_DIM
        src = np.full((self.nup,), -1, np.int64)
        src[0:3 * wf] = np.arange(3 * wf)
        o_sq = 3 * wf + self.hf
        src[self.bq_s * LANES:self.bq_s * LANES + ws] = o_sq + np.arange(ws)
        o_sk = o_sq + ws
        o_sv = o_sk + self.nkv * hd
        for kv in range(self.nkv):
            for rep in range(2):
                c0 = (self.bk_s + kv) * LANES + rep * hd
                src[c0:c0 + hd] = o_sk + kv * hd + np.arange(hd)
                c0 = (self.bv_s + kv) * LANES + rep * hd
                src[c0:c0 + hd] = o_sv + kv * hd + np.arange(hd)
        for h in range(self.hf):
            src[self.bz * LANES + self.gate_lane(h)] = 3 * wf + h
        return src

    def grad_column_map(self):
        src = self.column_map()
        dst = np.zeros((self.in_width,), np.int64)
        for col in range(self.nup - 1, -1, -1):
            if src[col] >= 0:
                dst[src[col]] = col
        return dst


def _block_diag_mean():
    m = np.zeros((LANES, LANES), np.float32)
    m[:HEAD_DIM, :HEAD_DIM] = 1.0 / HEAD_DIM
    m[HEAD_DIM:, HEAD_DIM:] = 1.0 / HEAD_DIM
    return jnp.asarray(m)


def _fold_halves():
    m = np.eye(LANES, dtype=np.float32)
    m[np.arange(LANES), (np.arange(LANES) + HEAD_DIM) % LANES] = 1.0
    return jnp.asarray(m)


def _gate_expand(md):
    e = np.zeros((LANES, md.wf), np.float32)
    for h in range(md.hf):
        e[md.gate_lane(h), h * HEAD_DIM:(h + 1) * HEAD_DIM] = 1.0
    return jnp.asarray(e)


def _qblocks(L, qb):
    blocks = [(0, BLOCK)]
    r = BLOCK
    while r < L:
        blocks.append((r, qb))
        r += qb
    assert r == L
    return blocks


def _f32dot(a, b):
    return jnp.dot(a, b, precision=HIGHEST, preferred_element_type=F32)


_DIMS_NT = (((1,), (1,)), ((), ()))


def _dot_nt(a, b):
    return lax.dot_general(a, b, _DIMS_NT, preferred_element_type=F32)


def _dot_tn(a, b):
    return jnp.dot(a.T.astype(BF16), b, preferred_element_type=F32)


def _log_sigmoid(z):
    return jnp.minimum(z, 0.0) - jnp.log(1.0 + jnp.exp(-jnp.abs(z)))


def _gate_fwd(u, b, md, *, name):
    L = u.shape[0]
    nb = L // BLOCK
    expand = _gate_expand(md)

    def body(z_ref, b_ref, e_ref, cexp_ref, ct_ref, c_s):
        ri = lax.broadcasted_iota(jnp.int32, (BLOCK, BLOCK), 0)
        ci = lax.broadcasted_iota(jnp.int32, (BLOCK, BLOCK), 1)
        tri = (ri >= ci).astype(F32)
        carry = jnp.zeros((1, LANES), F32)
        for bi in range(nb):
            rows = pl.ds(bi * BLOCK, BLOCK)
            logf = _log_sigmoid(z_ref[rows, :] + b_ref[...])
            blk = _f32dot(tri, logf) + carry
            c_s[rows, :] = blk
            carry = blk[BLOCK - 1:BLOCK, :]
        c = c_s[...]
        ct_ref[...] = c.T
        cexp_ref[...] = _f32dot(c, e_ref[...])

    return pl.pallas_call(
        body, name=name, grid=(1,),
        in_specs=[pl.BlockSpec((L, LANES), lambda i: (0, md.bz)), pl.BlockSpec((1, LANES), lambda i: (0, 0)),
                  pl.BlockSpec((LANES, md.wf), lambda i: (0, 0))],
        out_specs=[pl.BlockSpec((L, md.wf), lambda i: (0, 0)), pl.BlockSpec((LANES, L), lambda i: (0, 0))],
        out_shape=[jax.ShapeDtypeStruct((L, md.wf), F32), jax.ShapeDtypeStruct((LANES, L), F32)],
        scratch_shapes=[pltpu.VMEM((L, LANES), F32)],
        compiler_params=_cparams("arbitrary"),
    )(u, b, expand)


def _gate_bwd(u, b, dck_t, md, *, name):
    L = u.shape[0]
    nb = L // BLOCK

    def body(z_ref, b_ref, dck_ref, dz_ref, db_ref, dc_s):
        ri = lax.broadcasted_iota(jnp.int32, (BLOCK, BLOCK), 0)
        ci = lax.broadcasted_iota(jnp.int32, (BLOCK, BLOCK), 1)
        triu = (ri <= ci).astype(F32)
        dc_s[...] = -dck_ref[...].T
        carry = jnp.zeros((1, LANES), F32)
        db = jnp.zeros((1, LANES), F32)
        for bi in range(nb - 1, -1, -1):
            rows = pl.ds(bi * BLOCK, BLOCK)
            blk = _f32dot(triu, dc_s[rows, :]) + carry
            carry = blk[0:1, :]
            z = z_ref[rows, :] + b_ref[...]
            dz = blk * jax.nn.sigmoid(-z)
            if bi == 0:
                dz = jnp.where(lax.broadcasted_iota(jnp.int32, (BLOCK, LANES), 0) >= PAD, dz, 0.0)
            dz_ref[rows, :] = dz
            db = db + jnp.sum(dz, axis=0, keepdims=True)
        db_ref[...] = db

    return pl.pallas_call(
        body, name=name, grid=(1,),
        in_specs=[pl.BlockSpec((L, LANES), lambda i: (0, md.bz)), pl.BlockSpec((1, LANES), lambda i: (0, 0)),
                  pl.BlockSpec((LANES, L), lambda i: (0, 0))],
        out_specs=[pl.BlockSpec((L, LANES), lambda i: (0, 0)), pl.BlockSpec((1, LANES), lambda i: (0, 0))],
        out_shape=[jax.ShapeDtypeStruct((L, LANES), F32), jax.ShapeDtypeStruct((1, LANES), F32)],
        scratch_shapes=[pltpu.VMEM((L, LANES), F32)],
        compiler_params=_cparams("arbitrary"),
    )(u, b, dck_t)


def _head_norm(x, g, bd):
    r = lax.rsqrt(_f32dot(x * x, bd) + EPS)
    xh = x * r
    return xh * g, xh, r


def _head_norm_bwd(dy, xh, r, g, bd):
    dxh = dy * g
    dx = r * (dxh - xh * _f32dot(dxh * xh, bd))
    return dx, jnp.sum(dy * xh, axis=0, keepdims=True)


def _lane_half():
    return lax.broadcasted_iota(jnp.int32, (1, LANES), 1) < HEAD_DIM


def _store_head_pair(x, half, a_s, b_s):
    a_s[...] = jnp.where(half, x, 0.0).astype(BF16)
    b_s[...] = jnp.where(half, 0.0, x).astype(BF16)


def _fox_scores(qm, kn_s, cexp_ref, ct_ref, r0, nr, klen, hh):
    s = _dot_nt(qm, kn_s[0:klen, :]) * (HEAD_DIM ** -0.5)
    s = s + cexp_ref[r0:r0 + nr, HEAD_DIM * hh:HEAD_DIM * hh + 1] - ct_ref[hh:hh + 1, 0:klen]
    qp = r0 + lax.broadcasted_iota(jnp.int32, (nr, klen), 0)
    kp = lax.broadcasted_iota(jnp.int32, (nr, klen), 1)
    return jnp.where((kp <= qp) & (kp >= PAD), s, NEG_INF)


def _fox_fwd(u, cexp, ct3, qg, kg, md, *, name, comm=None):
    L = u.shape[0]
    blocks = _qblocks(L, QBLOCK)
    bd = _block_diag_mean()

    def body(q_ref, k_ref, v_ref, cexp_ref, ct_ref, qg_ref, kg_ref, bd_ref, o_ref, lse_ref, qa_s, qb_s, kn_s, v_s):
        half = _lane_half()
        bdv = bd_ref[...]
        _store_head_pair(_head_norm(q_ref[...], qg_ref[...], bdv)[0], half, qa_s, qb_s)
        kn_s[...] = _head_norm(k_ref[...], kg_ref[...], bdv)[0].astype(BF16)
        v_s[...] = v_ref[...].astype(BF16)
        for r0, nr in blocks:
            klen = r0 + nr
            o_blk = lse_blk = None
            for hh, q_s in enumerate((qa_s, qb_s)):
                s = _fox_scores(q_s[r0:r0 + nr, :], kn_s, cexp_ref, ct_ref, r0, nr, klen, hh)
                m = jnp.max(s, axis=-1, keepdims=True)
                p = jnp.exp(s - m)
                l = jnp.sum(p, axis=-1, keepdims=True)
                oh = jnp.dot(p.astype(BF16), v_s[0:klen, :], preferred_element_type=F32) * (1.0 / l)
                lh = jnp.broadcast_to(jnp.where(m > 0.5 * NEG_INF, m + jnp.log(l), 0.0), (nr, LANES))
                o_blk = oh if hh == 0 else jnp.where(half, o_blk, oh)
                lse_blk = lh if hh == 0 else jnp.where(half, lse_blk, lh)
            o_ref[r0:r0 + nr, :] = o_blk
            lse_ref[r0:r0 + nr, :] = lse_blk

    col = lambda base: pl.BlockSpec((L, LANES), lambda j: (0, base + j))
    vec = pl.BlockSpec((1, LANES), lambda j: (0, 0))
    return _call_carrying(
        body, comm, (u, u, u, cexp, ct3, qg, kg, bd), name=name, grid=(md.pf,),
        in_specs=[col(md.bq_f), col(md.bk_f), col(md.bv_f), col(0),
                  pl.BlockSpec((None, 8, L), lambda j: (j, 0, 0)), vec, vec,
                  pl.BlockSpec((LANES, LANES), lambda j: (0, 0))],
        out_specs=[col(0), col(0)],
        out_shape=[jax.ShapeDtypeStruct((L, md.wf), F32)] * 2,
        scratch_shapes=[pltpu.VMEM((L, LANES), BF16)] * 4)


def _softmax_bwd(p, dp):
    pdp = p * dp
    return pdp - p * jnp.sum(pdp, axis=-1, keepdims=True)


def _fox_bwd(u, cexp, ct3, qg, kg, do, lse, md, *, name, comm=None):
    L = u.shape[0]
    blocks = _qblocks(L, QBLOCK // 2)
    bd = _block_diag_mean()
    fold = _fold_halves()
    npairs = md.pf

    def body(q_ref, k_ref, v_ref, cexp_ref, ct_ref, qg_ref, kg_ref, bd_ref, fold_ref, do_ref, lse_ref,
             dq_ref, dk_ref, dv_ref, dck_ref, dqg_ref, dkg_ref,
             qa_s, qb_s, kn_s, v_s, doa_s, dob_s, dqn_s, dkn_s, dvv_s):
        j = pl.program_id(0)
        half = _lane_half()
        bdv = bd_ref[...]
        qy, qh, rq = _head_norm(q_ref[...], qg_ref[...], bdv)
        ky, kh, rk = _head_norm(k_ref[...], kg_ref[...], bdv)
        _store_head_pair(qy, half, qa_s, qb_s)
        _store_head_pair(do_ref[...], half, doa_s, dob_s)
        kn_s[...] = ky.astype(BF16)
        v_s[...] = v_ref[...].astype(BF16)
        dkn_s[...] = jnp.zeros_like(dkn_s)
        dvv_s[...] = jnp.zeros_like(dvv_s)
        dck_ref[...] = jnp.zeros_like(dck_ref)
        for r0, nr in blocks:
            klen = r0 + nr
            dq_blk = None
            for hh, (q_s, do_s) in enumerate(((qa_s, doa_s), (qb_s, dob_s))):
                c0 = HEAD_DIM * hh
                qm = q_s[r0:r0 + nr, :]
                dom = do_s[r0:r0 + nr, :]
                s = _fox_scores(qm, kn_s, cexp_ref, ct_ref, r0, nr, klen, hh)
                p = jnp.exp(s - lse_ref[r0:r0 + nr, c0:c0 + 1])
                ds = _softmax_bwd(p, _dot_nt(dom, v_s[0:klen, :]))
                dck_ref[hh:hh + 1, 0:klen] += jnp.sum(ds, axis=0, keepdims=True)
                ds = ds * (HEAD_DIM ** -0.5)
                dq_h = jnp.dot(ds.astype(BF16), kn_s[0:klen, :], preferred_element_type=F32)
                dkn_s[0:klen, :] += _dot_tn(ds, qm)
                dvv_s[0:klen, :] += _dot_tn(p, dom)
                dq_blk = dq_h if hh == 0 else jnp.where(half, dq_blk, dq_h)
            dqn_s[r0:r0 + nr, :] = dq_blk
        dq, dqg = _head_norm_bwd(dqn_s[...], qh, rq, qg_ref[...], bdv)
        dk, dkg = _head_norm_bwd(dkn_s[...], kh, rk, kg_ref[...], bdv)
        dq_ref[...] = dq
        dk_ref[...] = dk
        dv_ref[...] = dvv_s[...]

        @pl.when(j == 0)
        def _():
            dqg_ref[...] = jnp.zeros_like(dqg_ref)
            dkg_ref[...] = jnp.zeros_like(dkg_ref)

        dqg_ref[...] += dqg
        dkg_ref[...] += dkg

        @pl.when(j == npairs - 1)
        def _():
            dqg_ref[...] = _f32dot(jnp.broadcast_to(dqg_ref[...], (8, LANES)), fold_ref[...])[0:1, :]
            dkg_ref[...] = _f32dot(jnp.broadcast_to(dkg_ref[...], (8, LANES)), fold_ref[...])[0:1, :]

    col = lambda base: pl.BlockSpec((L, LANES), lambda j: (0, base + j))
    vec = pl.BlockSpec((1, LANES), lambda j: (0, 0))
    sq = pl.BlockSpec((LANES, LANES), lambda j: (0, 0))
    ct_spec = pl.BlockSpec((None, 8, L), lambda j: (j, 0, 0))
    big = jax.ShapeDtypeStruct((L, md.wf), F32)
    small = jax.ShapeDtypeStruct((1, LANES), F32)
    return _call_carrying(
        body, comm, (u, u, u, cexp, ct3, qg, kg, bd, fold, do, lse), name=name, grid=(md.pf,),
        in_specs=[col(md.bq_f), col(md.bk_f), col(md.bv_f), col(0), ct_spec, vec, vec, sq, sq, col(0), col(0)],
        out_specs=[col(0), col(0), col(0), ct_spec, vec, vec],
        out_shape=[big, big, big, jax.ShapeDtypeStruct((md.pf, 8, L), F32), small, small],
        scratch_shapes=[pltpu.VMEM((L, LANES), BF16)] * 6 + [pltpu.VMEM((L, LANES), F32)] * 3)


def _swa_scores(qm, kn_s, slope, k0, r0, nr, klen):
    s = _dot_nt(qm, kn_s[k0:k0 + klen, :]) * (HEAD_DIM ** -0.5)
    qp = r0 + lax.broadcasted_iota(jnp.int32, (nr, klen), 0)
    kp = k0 + lax.broadcasted_iota(jnp.int32, (nr, klen), 1)
    dist = qp - kp
    s = s - slope * dist.astype(F32)
    return jnp.where((dist >= 0) & (dist < WINDOW) & (kp >= PAD), s, NEG_INF)


def _swa_fwd(u, sinkp, slopep, qg, kg, md, *, name, comm=None):
    L = u.shape[0]
    blocks = _qblocks(L, QBLOCK)
    bd = _block_diag_mean()

    def body(q_ref, k_ref, v_ref, sink_ref, slope_ref, qg_ref, kg_ref, bd_ref, o_ref, lse_ref, qa_s, qb_s, kn_s, v_s):
        half = _lane_half()
        bdv = bd_ref[...]
        _store_head_pair(_head_norm(q_ref[...], qg_ref[...], bdv)[0], half, qa_s, qb_s)
        kn_s[...] = _head_norm(k_ref[...], kg_ref[...], bdv)[0].astype(BF16)
        v_s[...] = v_ref[...].astype(BF16)
        for r0, nr in blocks:
            k0 = max(r0 - BLOCK, 0)
            klen = r0 + nr - k0
            o_blk = lse_blk = None
            for hh, q_s in enumerate((qa_s, qb_s)):
                c0 = HEAD_DIM * hh
                s = _swa_scores(q_s[r0:r0 + nr, :], kn_s, slope_ref[0:1, c0:c0 + 1], k0, r0, nr, klen)
                sink = sink_ref[0:1, c0:c0 + 1]
                m = jnp.maximum(jnp.max(s, axis=-1, keepdims=True), sink)
                p = jnp.exp(s - m)
                den = jnp.sum(p, axis=-1, keepdims=True) + jnp.exp(sink - m)
                oh = jnp.dot(p.astype(BF16), v_s[k0:k0 + klen, :], preferred_element_type=F32) * (1.0 / den)
                lh = jnp.broadcast_to(m + jnp.log(den), (nr, LANES))
                o_blk = oh if hh == 0 else jnp.where(half, o_blk, oh)
                lse_blk = lh if hh == 0 else jnp.where(half, lse_blk, lh)
            o_ref[r0:r0 + nr, :] = o_blk
            lse_ref[r0:r0 + nr, :] = lse_blk

    g2 = md.g // 2
    qcol = pl.BlockSpec((L, LANES), lambda j: (0, md.bq_s + j))
    kcol = pl.BlockSpec((L, LANES), lambda j: (0, md.bk_s + j // g2))
    vcol = pl.BlockSpec((L, LANES), lambda j: (0, md.bv_s + j // g2))
    ocol = pl.BlockSpec((L, LANES), lambda j: (0, j))
    pvec = pl.BlockSpec((1, LANES), lambda j: (0, j))
    vec = pl.BlockSpec((1, LANES), lambda j: (0, 0))
    return _call_carrying(
        body, comm, (u, u, u, sinkp, slopep, qg, kg, bd), name=name, grid=(md.ps,),
        in_specs=[qcol, kcol, vcol, pvec, pvec, vec, vec, pl.BlockSpec((LANES, LANES), lambda j: (0, 0))],
        out_specs=[ocol, ocol],
        out_shape=[jax.ShapeDtypeStruct((L, md.ws), F32)] * 2,
        scratch_shapes=[pltpu.VMEM((L, LANES), BF16)] * 4)


def _swa_bwd(u, sinkp, slopep, qg, kg, do, lse, md, *, name, comm=None):
    L = u.shape[0]
    blocks = _qblocks(L, QBLOCK // 2)
    bd = _block_diag_mean()
    fold = _fold_halves()
    g2 = md.g // 2
    npairs = md.ps

    def body(q_ref, k_ref, v_ref, sink_ref, slope_ref, qg_ref, kg_ref, bd_ref, fold_ref, do_ref, lse_ref,
             dq_ref, dk_ref, dv_ref, dsink_ref, dqg_ref, dkg_ref,
             qa_s, qb_s, kn_s, v_s, doa_s, dob_s, dqn_s):
        j = pl.program_id(0)
        half = _lane_half()
        bdv = bd_ref[...]
        qy, qh, rq = _head_norm(q_ref[...], qg_ref[...], bdv)
        ky, kh, rk = _head_norm(k_ref[...], kg_ref[...], bdv)
        _store_head_pair(qy, half, qa_s, qb_s)
        _store_head_pair(do_ref[...], half, doa_s, dob_s)
        kn_s[...] = ky.astype(BF16)
        v_s[...] = v_ref[...].astype(BF16)

        @pl.when(j % g2 == 0)
        def _():
            dk_ref[...] = jnp.zeros_like(dk_ref)
            dv_ref[...] = jnp.zeros_like(dv_ref)

        @pl.when(j == 0)
        def _():
            dqg_ref[...] = jnp.zeros_like(dqg_ref)
            dkg_ref[...] = jnp.zeros_like(dkg_ref)

        dsink = [jnp.zeros((1, 1), F32), jnp.zeros((1, 1), F32)]
        for r0, nr in blocks:
            k0 = max(r0 - BLOCK, 0)
            klen = r0 + nr - k0
            dq_blk = None
            for hh, (q_s, do_s) in enumerate(((qa_s, doa_s), (qb_s, dob_s))):
                c0 = HEAD_DIM * hh
                qm = q_s[r0:r0 + nr, :]
                dom = do_s[r0:r0 + nr, :]
                s = _swa_scores(qm, kn_s, slope_ref[0:1, c0:c0 + 1], k0, r0, nr, klen)
                lse_h = lse_ref[r0:r0 + nr, c0:c0 + 1]
                p = jnp.exp(s - lse_h)
                pdp = p * _dot_nt(dom, v_s[k0:k0 + klen, :])
                delta = jnp.sum(pdp, axis=-1, keepdims=True)
                p_sink = jnp.exp(sink_ref[0:1, c0:c0 + 1] - lse_h)
                dsink[hh] = dsink[hh] - jnp.sum(p_sink * delta, axis=0, keepdims=True)
                ds = (pdp - p * delta) * (HEAD_DIM ** -0.5)
                dq_h = jnp.dot(ds.astype(BF16), kn_s[k0:k0 + klen, :], preferred_element_type=F32)
                dk_ref[k0:k0 + klen, :] += _dot_tn(ds, qm)
                dv_ref[k0:k0 + klen, :] += _dot_tn(p, dom)
                dq_blk = dq_h if hh == 0 else jnp.where(half, dq_blk, dq_h)
            dqn_s[r0:r0 + nr, :] = dq_blk
        dq, dqg = _head_norm_bwd(dqn_s[...], qh, rq, qg_ref[...], bdv)
        dq_ref[...] = dq
        dqg_ref[...] += dqg
        dsink_ref[...] = jnp.where(half, jnp.broadcast_to(dsink[0], (1, LANES)), jnp.broadcast_to(dsink[1], (1, LANES)))

        @pl.when(j % g2 == g2 - 1)
        def _():
            dkn = _f32dot(dk_ref[...], fold_ref[...])
            dk, dkg = _head_norm_bwd(dkn, kh, rk, kg_ref[...], bdv)
            dk_ref[...] = jnp.where(half, dk, 0.0)
            dv_ref[...] = jnp.where(half, _f32dot(dv_ref[...], fold_ref[...]), 0.0)
            dkg_ref[...] += dkg

        @pl.when(j == npairs - 1)
        def _():
            dqg_ref[...] = _f32dot(jnp.broadcast_to(dqg_ref[...], (8, LANES)), fold_ref[...])[0:1, :]

    qcol = pl.BlockSpec((L, LANES), lambda j: (0, md.bq_s + j))
    kcol = pl.BlockSpec((L, LANES), lambda j: (0, md.bk_s + j // g2))
    vcol = pl.BlockSpec((L, LANES), lambda j: (0, md.bv_s + j // g2))
    ocol = pl.BlockSpec((L, LANES), lambda j: (0, j))
    kvout = pl.BlockSpec((L, LANES), lambda j: (0, j // g2))
    pvec = pl.BlockSpec((1, LANES), lambda j: (0, j))
    vec = pl.BlockSpec((1, LANES), lambda j: (0, 0))
    sq = pl.BlockSpec((LANES, LANES), lambda j: (0, 0))
    kvshape = jax.ShapeDtypeStruct((L, LANES * md.nkv), F32)
    small = jax.ShapeDtypeStruct((1, LANES), F32)
    return _call_carrying(
        body, comm, (u, u, u, sinkp, slopep, qg, kg, bd, fold, do, lse), name=name, grid=(md.ps,),
        in_specs=[qcol, kcol, vcol, pvec, pvec, vec, vec, sq, sq, ocol, ocol],
        out_specs=[ocol, kvout, kvout, pvec, vec, vec],
        out_shape=[jax.ShapeDtypeStruct((L, md.ws), F32), kvshape, kvshape,
                   jax.ShapeDtypeStruct((1, md.ws), F32), small, small],
        scratch_shapes=[pltpu.VMEM((L, LANES), BF16)] * 6 + [pltpu.VMEM((L, LANES), F32)])


def _outnorm_fwd(of, os_, gf, gs, *, name):
    L, wf = of.shape
    ws = os_.shape[1]
    tr = _pick(L, _TR)

    def body(of_ref, os_ref, gf_ref, gs_ref, o_ref):
        for src, g_ref, c0, w in ((of_ref, gf_ref, 0, wf), (os_ref, gs_ref, wf, ws)):
            x = src[...]
            r = lax.rsqrt(jnp.mean(x * x, axis=-1, keepdims=True) + EPS)
            o_ref[:, c0:c0 + w] = (x * r * g_ref[...]).astype(o_ref.dtype)

    return pl.pallas_call(
        body, name=name, grid=(L // tr,),
        in_specs=[pl.BlockSpec((tr, wf), lambda i: (i, 0)), pl.BlockSpec((tr, ws), lambda i: (i, 0)),
                  pl.BlockSpec((1, wf), lambda i: (0, 0)), pl.BlockSpec((1, ws), lambda i: (0, 0))],
        out_specs=pl.BlockSpec((tr, wf + ws), lambda i: (i, 0)),
        out_shape=jax.ShapeDtypeStruct((L, wf + ws), BF16),
        compiler_params=_cparams("parallel"),
    )(of, os_, gf, gs)


def _outnorm_bwd(don, of, os_, gf, gs, *, name):
    L, wf = of.shape
    ws = os_.shape[1]
    tr = _pick(L, _TR)

    def body(d_ref, of_ref, os_ref, gf_ref, gs_ref, dof_ref, dos_ref, dgf_ref, dgs_ref):
        i = pl.program_id(0)
        for src, g_ref, c0, w, dx_ref, dg_ref in ((of_ref, gf_ref, 0, wf, dof_ref, dgf_ref),
                                                  (os_ref, gs_ref, wf, ws, dos_ref, dgs_ref)):
            x = src[...]
            dy = d_ref[:, c0:c0 + w]
            r = lax.rsqrt(jnp.mean(x * x, axis=-1, keepdims=True) + EPS)
            xh = x * r
            dxh = dy * g_ref[...]
            dx_ref[...] = r * (dxh - xh * jnp.mean(dxh * xh, axis=-1, keepdims=True))
            part = jnp.sum(dy * xh, axis=0, keepdims=True)

            @pl.when(i == 0)
            def _():
                dg_ref[...] = part

            @pl.when(i > 0)
            def _():
                dg_ref[...] += part

    rf = pl.BlockSpec((tr, wf), lambda i: (i, 0))
    rs = pl.BlockSpec((tr, ws), lambda i: (i, 0))
    vf = pl.BlockSpec((1, wf), lambda i: (0, 0))
    vs = pl.BlockSpec((1, ws), lambda i: (0, 0))
    return pl.pallas_call(
        body, name=name, grid=(L // tr,),
        in_specs=[pl.BlockSpec((tr, wf + ws), lambda i: (i, 0)), rf, rs, vf, vs],
        out_specs=[rf, rs, vf, vs],
        out_shape=[jax.ShapeDtypeStruct((L, wf), F32), jax.ShapeDtypeStruct((L, ws), F32),
                   jax.ShapeDtypeStruct((1, wf), F32), jax.ShapeDtypeStruct((1, ws), F32)],
        compiler_params=_cparams("arbitrary"),
    )(don, of, os_, gf, gs)


def _win_to_mine(w, md):
    d = w.shape[0]
    wf, ws, hd = md.wf, md.ws, HEAD_DIM
    o_z = 3 * wf
    o_sq = o_z + md.hf
    o_sk = o_sq + ws
    o_sv = o_sk + md.nkv * hd
    parts = [w[:, :3 * wf], w[:, o_sq:o_sq + ws]]
    for base in (o_sk, o_sv):
        for kv in range(md.nkv):
            blk = w[:, base + kv * hd:base + (kv + 1) * hd]
            parts += [blk, blk]
    z = w[:, o_z:o_z + md.hf].reshape(d, md.hf // 2, 2)
    z = jnp.pad(z, ((0, 0), (0, 0), (0, 6))).reshape(d, 4 * md.hf)
    parts.append(jnp.pad(z, ((0, 0), (0, LANES - 4 * md.hf + md.nup - md.nu))))
    return jnp.concatenate(parts, axis=1)


def _win_grad_to_ref(dw, md):
    d = dw.shape[0]
    wf, ws, hd = md.wf, md.ws, HEAD_DIM
    z = dw[:, md.bz * LANES:md.bz * LANES + 4 * md.hf].reshape(d, md.hf // 2, 8)[:, :, :2].reshape(d, md.hf)
    parts = [dw[:, :3 * wf], z, dw[:, md.bq_s * LANES:md.bq_s * LANES + ws]]
    for base in (md.bk_s, md.bv_s):
        for kv in range(md.nkv):
            c0 = (base + kv) * LANES
            parts.append(dw[:, c0:c0 + hd])
    return jnp.concatenate(parts, axis=1)


def _mix_small(p, md):
    tile2 = lambda v: jnp.tile(v.reshape(1, HEAD_DIM), (1, 2))
    b = p["b_forget"].reshape(md.hf // 2, 2)
    b = jnp.pad(b, ((0, 0), (0, 6))).reshape(1, 4 * md.hf)
    slopes = np.asarray(2.0 ** (-8.0 * np.arange(1, md.hq + 1) / md.hq), np.float32)
    return dict(
        g_mix=p["mix_norm"].reshape(1, -1),
        b_gate=jnp.pad(b, ((0, 0), (0, LANES - 4 * md.hf))),
        fqg=tile2(p["fox_q_norm"]), fkg=tile2(p["fox_k_norm"]),
        sqg=tile2(p["swa_q_norm"]), skg=tile2(p["swa_k_norm"]),
        sinkp=jnp.repeat(p["swa_sinks"], HEAD_DIM).reshape(1, md.ws),
        slopep=jnp.asarray(np.repeat(slopes, HEAD_DIM).reshape(1, md.ws)),
        gfo=p["fox_out_norm"].reshape(1, md.wf), gso=p["swa_out_norm"].reshape(1, md.ws),
    )


def _mix_fwd(h, sp, w, md, queue=None):
    L = h.shape[0]
    hn = _rms_fwd(h, sp["g_mix"], name="mix_rms")
    u = _mm(queue, hn, w(0), name="mix_u")
    cexp, ct = _gate_fwd(u, sp["b_gate"], md, name="gate_fwd")
    ct3 = ct[:8 * md.pf].reshape(md.pf, 8, L)
    of, lsef = _carried(queue, _fox_fwd, 2, u, cexp, ct3, sp["fqg"], sp["fkg"], md, name="fox_fwd")
    os_, lses = _carried(queue, _swa_fwd, 2, u, sp["sinkp"], sp["slopep"], sp["sqg"], sp["skg"], md, name="swa_fwd")
    on = _outnorm_fwd(of, os_, sp["gfo"], sp["gso"], name="outnorm_fwd")
    h_out = _mm(queue, on, w(1), name="mix_out", residual=h)
    return h_out, (h, hn, u, cexp, ct3, of, lsef, os_, lses, on)


def _mix_bwd(dh, saved, sp, w_in, w_out, md, queue=None):
    h, hn, u, cexp, ct3, of, lsef, os_, lses, on = saved
    L = h.shape[0]
    dhb = _cast_bf16(dh, name="mix_dhb")
    don = _matmul(dhb, w_out, name="mix_don", nt=True)
    dw_out = _matmul(_transpose(on, name="mix_onT"), dhb, name="mix_dwout", out_dtype=BF16)
    dof, dos, dgfo, dgso = _outnorm_bwd(don, of, os_, sp["gfo"], sp["gso"], name="outnorm_bwd")
    duq, duk, duv, dck, dfqg, dfkg = _carried(queue, _fox_bwd, 6, u, cexp, ct3, sp["fqg"], sp["fkg"], dof, lsef, md,
                                              name="fox_bwd")
    dsq, dsk, dsv, dsinkp, dsqg, dskg = _carried(queue, _swa_bwd, 6, u, sp["sinkp"], sp["slopep"], sp["sqg"], sp["skg"], dos,
                                                 lses, md, name="swa_bwd")
    dck_t = jnp.pad(dck.reshape(8 * md.pf, L), ((0, LANES - 8 * md.pf), (0, 0)))
    dz, db = _gate_bwd(u, sp["b_gate"], dck_t, md, name="gate_bwd")
    du = jnp.concatenate([duq, duk, duv, dsq, dsk, dsv, dz, jnp.zeros((L, md.nup - md.nu), F32)], axis=1).astype(BF16)
    dhn = _mm(queue, du, w_in, name="mix_dhn", nt=True)
    dw_in = _mm(queue, _transpose(hn, name="mix_hnT"), du, name="mix_dwin", out_dtype=BF16)
    dh_in, dg_mix = _rms_bwd(dhn, h, sp["g_mix"], dh, name="mix_drms")
    small = dict(
        mix_norm=dg_mix.reshape(-1),
        b_forget=db[0, :4 * md.hf].reshape(md.hf // 2, 8)[:, :2].reshape(md.hf),
        fox_q_norm=dfqg[0, :HEAD_DIM], fox_k_norm=dfkg[0, :HEAD_DIM],
        swa_q_norm=dsqg[0, :HEAD_DIM], swa_k_norm=dskg[0, :HEAD_DIM],
        swa_sinks=dsinkp[0, ::HEAD_DIM],
        fox_out_norm=dgfo.reshape(-1), swa_out_norm=dgso.reshape(-1),
    )
    return dh_in, dw_in, dw_out, small


_ANY = pl.BlockSpec(memory_space=pl.ANY)
_HALF_ROWS = (512, 352, 256, 192, 128, 64, 32, 16)


def _mesh_pos():
    return lax.axis_index("x"), lax.axis_index("y"), lax.axis_index("c")


def _other_chips(x, y):
    return [(1 - x, y), (x, 1 - y), (1 - x, 1 - y)]


def _rows_half(ref, which):
    rh = ref.shape[-2] // 2
    if len(ref.shape) == 2:
        return ref.at[pl.ds(which * rh, rh), :]
    return ref.at[:, pl.ds(which * rh, rh), :]


def _remote(src, dst, send_sems, recv_sems, idx, dev):
    return pltpu.make_async_remote_copy(src_ref=src, dst_ref=dst, send_sem=send_sems.at[idx], recv_sem=recv_sems.at[idx],
                                        device_id=dev, device_id_type=MESH)


def _cast_into_chunk(w, l_idx, k_idx, *, name):
    _, rows, cols = w.shape
    tr = _pick(rows, (512, 352, 256, 128, 64, 32, 16))

    def body(l_ref, k_ref, w_ref, o_ref):
        o_ref[...] = w_ref[...].astype(o_ref.dtype)

    return pl.pallas_call(
        body, name=name,
        grid_spec=pltpu.PrefetchScalarGridSpec(
            num_scalar_prefetch=2, grid=(rows // tr,),
            in_specs=[pl.BlockSpec((None, tr, cols), lambda i, l, k: (l[0], i, 0))],
            out_specs=pl.BlockSpec((None, tr, cols), lambda i, l, k: (k[0], i, 0))),
        out_shape=jax.ShapeDtypeStruct((NCH, rows, cols), BF16),
        compiler_params=_cparams("parallel"),
    )(l_idx, k_idx, w)


class _SiblingSwap:
    n_sems = 1

    def __init__(self, g):
        self.operands = [g]
        self.out_shapes = [jax.ShapeDtypeStruct((NCH, g.shape[1] // 2, g.shape[2]), g.dtype)]
        self.aliases = {}

    def copies(self, cin, cout, send_sems, recv_sems, base=0):
        x, y, c = _mesh_pos()
        cp = _remote(_rows_half(cin[0], 1 - c), cout[0], send_sems, recv_sems, base, (x, y, 1 - c))
        return [(cp, cp)]


class _ChipExchange:
    n_sems = NCH - 1

    def __init__(self, s, r0, r1, land=None):
        self.r0, self.r1 = r0, r1
        self.operands = [s] if land is None else [s, land]
        self.out_shapes = [jax.ShapeDtypeStruct((NCH - 1,) + s.shape[1:], s.dtype)]
        self.aliases = {} if land is None else {1: 0}

    def copies(self, cin, cout, send_sems, recv_sems, base=0):
        x, y, c = _mesh_pos()
        rows = pl.ds(self.r0, self.r1 - self.r0)
        cps = [_remote(cin[0].at[2 * cx + cy, rows, :], cout[0].at[j, rows, :], send_sems, recv_sems, base + j, (cx, cy, c))
               for j, (cx, cy) in enumerate(_other_chips(x, y))]
        return [(cp, cp) for cp in cps]


class _SiblingShare:
    n_sems = 1

    def __init__(self, tot):
        self.operands = [tot]
        self.out_shapes = [jax.ShapeDtypeStruct(tot.shape, tot.dtype)]
        self.aliases = {0: 0}

    def copies(self, cin, cout, send_sems, recv_sems, base=0):
        x, y, c = _mesh_pos()
        mine, theirs = _rows_half(cout[0], c), _rows_half(cout[0], 1 - c)
        return [(_remote(mine, mine, send_sems, recv_sems, base, (x, y, 1 - c)),
                 _remote(theirs, theirs, send_sems, recv_sems, base, (x, y, 1 - c)))]


class _Both:
    def __init__(self, a, b):
        self.a, self.b = a, b
        self.operands = a.operands + b.operands
        self.out_shapes = a.out_shapes + b.out_shapes
        self.aliases = dict(a.aliases)
        self.aliases.update({len(a.operands) + s: len(a.out_shapes) + d for s, d in b.aliases.items()})
        self.n_sems = a.n_sems + b.n_sems

    def copies(self, cin, cout, send_sems, recv_sems, base=0):
        na, nao = len(self.a.operands), len(self.a.out_shapes)
        return (self.a.copies(cin[:na], cout[:nao], send_sems, recv_sems, base)
                + self.b.copies(cin[na:], cout[nao:], send_sems, recv_sems, base + self.a.n_sems))


def _rows_quarter(ref, c, q):
    rq = ref.shape[0] // 4
    return ref.at[pl.ds((2 * c + q) * rq, rq), :]


class _GatherStep:
    n_sems = 7

    def __init__(self, stage1, stage2, stage3):
        given = [(st, b) for st, b in ((1, stage1), (2, stage2), (3, stage3)) if b is not None]
        self.stages = [st for st, _ in given]
        self.operands = [b for _, b in given]
        self.out_shapes = [jax.ShapeDtypeStruct(b.shape, b.dtype) for b in self.operands]
        self.aliases = {i: i for i in range(len(self.operands))}

    def copies(self, cin, cout, send_sems, recv_sems, base=0):
        x, y, c = _mesh_pos()
        k, kx, ky, kd = 2 * x + y, 2 * (1 - x) + y, 2 * x + (1 - y), 2 * (1 - x) + (1 - y)
        xn, yn, sibling = (1 - x, y, c), (x, 1 - y, c), (x, y, 1 - c)

        def pair(idx, dev, src, arrival):
            return (_remote(src, src, send_sems, recv_sems, base + idx, dev),
                    _remote(arrival, arrival, send_sems, recv_sems, base + idx, dev))

        out = []
        for stage, buf in zip(self.stages, cout):
            if stage == 1:
                mine = _rows_half(buf.at[k], c)
                out.append(pair(0, xn, mine, _rows_half(buf.at[kx], c)))
                out.append(pair(1, yn, mine, _rows_half(buf.at[ky], c)))
            elif stage == 2:
                out.append(pair(2, yn, _rows_quarter(buf.at[kx], c, 0), _rows_quarter(buf.at[kd], c, 0)))
                out.append(pair(3, xn, _rows_quarter(buf.at[ky], c, 1), _rows_quarter(buf.at[kd], c, 1)))
                out.append(pair(4, sibling, _rows_half(buf.at[kx], c), _rows_half(buf.at[kx], 1 - c)))
                out.append(pair(5, sibling, _rows_half(buf.at[ky], c), _rows_half(buf.at[ky], 1 - c)))
            else:
                out.append(pair(6, sibling, _rows_half(buf.at[kd], c), _rows_half(buf.at[kd], 1 - c)))
        return out


class _GatherQueue:
    def __init__(self, bufs):
        self.bufs = list(bufs)
        self.step = -3
        for _ in range(3):
            self.give(_run_exchange(self.take(), name="first_allgather"))

    def take(self):
        s = self.step
        self.step += 1
        at = [i if 0 <= i < len(self.bufs) else None for i in (s + 3, s + 2, s + 1)]
        self.cur = [i for i in at if i is not None]
        if not self.cur:
            return None
        return _GatherStep(*[None if i is None else self.bufs[i] for i in at])

    def give(self, outs):
        for i, buf in zip(self.cur, outs):
            self.bufs[i] = buf


def _run_exchange(job, *, name):
    n_in, n_out = len(job.operands), len(job.out_shapes)

    def body(*refs):
        cps = job.copies(refs[:n_in], refs[n_in:n_in + n_out], refs[-2], refs[-1])
        for cp, _ in cps:
            cp.start()
        for _, arrival in cps:
            arrival.wait_recv()
        for cp, _ in cps:
            cp.wait_send()

    return pl.pallas_call(
        body, name=name, in_specs=[_ANY] * n_in, out_specs=[_ANY] * n_out, out_shape=list(job.out_shapes),
        input_output_aliases=dict(job.aliases),
        scratch_shapes=[pltpu.SemaphoreType.DMA((job.n_sems,)), pltpu.SemaphoreType.DMA((job.n_sems,))],
    )(*job.operands)


class _ExchangeQueue:
    def __init__(self):
        self.lines = ([], [])
        self.cur = []

    def put(self, line, make_job, done):
        self.lines[line].append((make_job, done))

    def take(self):
        entries = [ln.pop(0) for ln in self.lines if ln]
        if not entries:
            return None
        jobs = [make() for make, _ in entries]
        self.cur = [(job, done) for job, (_, done) in zip(jobs, entries)]
        return jobs[0] if len(jobs) == 1 else _Both(*jobs)

    def give(self, outs):
        outs = list(outs)
        for job, done in self.cur:
            n = len(job.out_shapes)
            done(outs[:n])
            outs = outs[n:]

    def drain(self, name):
        while (job := self.take()) is not None:
            self.give(_run_exchange(job, name=name))


ICI_LINE, D2D_LINE = 0, 1


def _reduce_scatter_later(queue, parts, c_idx, k_idx, tag, done, n_parts=2):
    n = len(parts)
    st = dict(lands=[None] * n, sums=None, chip_lands=[None] * n, tots=[None] * n, left=n)

    def shared(i, outs):
        st['tots'][i] = outs[0]
        st['left'] -= 1
        if st['left'] == 0:
            done(st['tots'])

    def exchanged(i, outs):
        st['chip_lands'][i] = outs[0]
        st['left'] -= 1
        if st['left'] == 0:
            st['left'] = n
            for j in range(n):
                tot = _add_chunks(st['sums'][j], st['chip_lands'][j], k_idx, c_idx, name=f"{tag}_rs_add4")
                queue.put(D2D_LINE, lambda tot=tot: _SiblingShare(tot), lambda outs, j=j: shared(j, outs))

    def swapped(i, outs):
        st['lands'][i] = outs[0]
        st['left'] -= 1
        if st['left'] == 0:
            st['sums'] = [_add_own_half(p, l, c_idx, name=f"{tag}_rs_add2") for p, l in zip(parts, st['lands'])]
            st['left'] = n * n_parts
            for j in range(n):
                step = st['sums'][j].shape[1] // n_parts
                for p in range(n_parts):
                    queue.put(ICI_LINE,
                              lambda j=j, p=p, step=step: _ChipExchange(st['sums'][j], p * step, (p + 1) * step, st['chip_lands'][j]),
                              lambda outs, j=j: exchanged(j, outs))

    for i in range(n):
        queue.put(D2D_LINE, lambda i=i: _SiblingSwap(parts[i]), lambda outs, i=i: swapped(i, outs))


def _mm(queue, a, b, **kw):
    job = queue.take() if queue is not None else None
    if job is None:
        return _matmul(a, b, **kw)
    n_out = len(kw.get("out_dtypes") or (0,))
    outs = _matmul(a, b, comm=job, **kw)
    queue.give(outs[n_out:])
    return outs[0] if n_out == 1 else outs[:n_out]


def _allgather_devices(v, *, name):
    m = v.shape[0]

    def body(v_ref, out_ref, send_sems, recv_sems, local_sem):
        x, y, c = _mesh_pos()
        mine = 4 * x + 2 * y + c
        own = pltpu.make_async_copy(v_ref, out_ref.at[mine], local_sem)
        own.start()
        cps = []
        for r in range(1, 8):
            px, py, pc = (x + (r >> 2)) % 2, (y + ((r >> 1) & 1)) % 2, (c + (r & 1)) % 2
            cps.append((_remote(v_ref, out_ref.at[mine], send_sems, recv_sems, r - 1, (px, py, pc)), 4 * px + 2 * py + pc))
        for cp, _ in cps:
            cp.start()
        for r, (cp, theirs) in enumerate(cps):
            blk = out_ref.at[theirs]
            _remote(blk, blk, send_sems, recv_sems, r, (x, y, c)).wait_recv()
        for cp, _ in cps:
            cp.wait_send()
        own.wait()

    return pl.pallas_call(
        body, name=name, in_specs=[_ANY], out_specs=_ANY,
        out_shape=jax.ShapeDtypeStruct((8, m, LANES), v.dtype),
        scratch_shapes=[pltpu.SemaphoreType.DMA((7,)), pltpu.SemaphoreType.DMA((7,)), pltpu.SemaphoreType.DMA],
    )(v)


def _add_own_half(g, land, c_idx, *, name):
    nch, rh, cols = land.shape
    tr = _pick(rh, _HALF_ROWS)
    nt = rh // tr

    def body(c_ref, g_ref, l_ref, o_ref):
        o_ref[...] = (g_ref[...].astype(F32) + l_ref[...].astype(F32)).astype(o_ref.dtype)

    return pl.pallas_call(
        body, name=name,
        grid_spec=pltpu.PrefetchScalarGridSpec(
            num_scalar_prefetch=1, grid=(nch, nt),
            in_specs=[pl.BlockSpec((None, tr, cols), lambda k, i, c: (k, c[0] * nt + i, 0)),
                      pl.BlockSpec((None, tr, cols), lambda k, i, c: (k, i, 0))],
            out_specs=pl.BlockSpec((None, tr, cols), lambda k, i, c: (k, i, 0))),
        out_shape=jax.ShapeDtypeStruct(land.shape, BF16),
        compiler_params=_cparams("parallel", "parallel"),
    )(c_idx, g, land)


def _add_chunks(s, land, k_idx, c_idx, *, name):
    _, rh, cols = s.shape
    tr = _pick(rh, _HALF_ROWS)
    nt = rh // tr

    def body(k_ref, c_ref, s_ref, l_ref, o_ref):
        t = s_ref[...].astype(F32)
        for j in range(NCH - 1):
            t = t + l_ref[j].astype(F32)
        o_ref[...] = t

    return pl.pallas_call(
        body, name=name,
        grid_spec=pltpu.PrefetchScalarGridSpec(
            num_scalar_prefetch=2, grid=(nt,),
            in_specs=[pl.BlockSpec((None, tr, cols), lambda i, k, c: (k[0], i, 0)),
                      pl.BlockSpec((NCH - 1, tr, cols), lambda i, k, c: (0, i, 0))],
            out_specs=pl.BlockSpec((tr, cols), lambda i, k, c: (c[0] * nt + i, 0))),
        out_shape=jax.ShapeDtypeStruct((2 * rh, cols), F32),
        compiler_params=_cparams("parallel"),
    )(k_idx, c_idx, s, land)


def _sum_devices(v, *, name):
    _, m, _ = v.shape

    def body(v_ref, o_ref):
        t = v_ref[0]
        for d in range(1, 8):
            t = t + v_ref[d]
        o_ref[...] = t

    return pl.pallas_call(
        body, name=name, grid=(1,),
        in_specs=[pl.BlockSpec((8, m, LANES), lambda i: (0, 0, 0))],
        out_specs=pl.BlockSpec((m, LANES), lambda i: (0, 0)),
        out_shape=jax.ShapeDtypeStruct((m, LANES), F32),
        compiler_params=_cparams("arbitrary"),
    )(v)


def _adamw_math(w, g, m, v):
    m = ADAM_B1 * m + (1.0 - ADAM_B1) * g
    v = ADAM_B2 * v + (1.0 - ADAM_B2) * (g * g)
    m_hat = m / (1.0 - ADAM_B1 ** ADAM_STEP)
    v_hat = v / (1.0 - ADAM_B2 ** ADAM_STEP)
    delta = -ADAM_LR * (m_hat / (jnp.sqrt(v_hat) + ADAM_EPS) + ADAM_WD * w)
    return delta, m, v


def _adamw_layer(w, m, v, g, layer, prev, *, name):
    depth, rows, cols = w.shape
    tr = _pick(rows, (256, 128, 64, 32, 16, 8))
    lay = pl.BlockSpec((None, tr, cols), lambda i, l: (l[0], i, 0))
    n_prev = 0 if prev is None else 4

    def body(l_ref, w_ref, m_ref, v_ref, g_ref, *rest):
        go_ref, d_ref, mo_ref, vo_ref = rest[n_prev:]
        g = g_ref[...]
        delta, m_new, v_new = _adamw_math(w_ref[...], g, m_ref[...], v_ref[...])
        go_ref[...] = g
        d_ref[...] = delta
        mo_ref[...] = m_new
        vo_ref[...] = v_new

    stack = jax.ShapeDtypeStruct(w.shape, F32)
    return pl.pallas_call(
        body, name=name,
        grid_spec=pltpu.PrefetchScalarGridSpec(
            num_scalar_prefetch=1, grid=(rows // tr,),
            in_specs=[lay, lay, lay, pl.BlockSpec((tr, cols), lambda i, l: (i, 0))] + [_ANY] * n_prev,
            out_specs=[lay] * 4),
        out_shape=[stack] * 4,
        input_output_aliases={} if prev is None else {5 + q: q for q in range(4)},
        compiler_params=_cparams("parallel"),
    )(layer, w, m, v, g, *(() if prev is None else prev))


def _adamw_flat(w, g, m, v, *, name):
    def body(w_ref, g_ref, m_ref, v_ref, d_ref, mo_ref, vo_ref):
        d_ref[...], mo_ref[...], vo_ref[...] = _adamw_math(w_ref[...], g_ref[...], m_ref[...], v_ref[...])

    blk = pl.BlockSpec(w.shape, lambda i: (0, 0))
    return pl.pallas_call(
        body, name=name, grid=(1,), in_specs=[blk] * 4, out_specs=[blk] * 3,
        out_shape=[jax.ShapeDtypeStruct(w.shape, F32)] * 3, compiler_params=_cparams("arbitrary"),
    )(w, g, m, v)


_WEIGHTS = ('meta_tokens', 'ffn1_norm', 'ffn1_w_gate', 'ffn1_w_up', 'ffn1_w_down', 'mix_norm', 'w_in', 'b_forget',
            'fox_q_norm', 'fox_k_norm', 'swa_q_norm', 'swa_k_norm', 'swa_sinks', 'fox_out_norm', 'swa_out_norm', 'w_out',
            'ffn2_norm', 'ffn2_w_gate', 'ffn2_w_up', 'ffn2_w_down')
_BIG = ('ffn1_w_gate', 'ffn1_w_up', 'ffn1_w_down', 'w_in', 'w_out', 'ffn2_w_gate', 'ffn2_w_up', 'ffn2_w_down')
_SMALL = tuple(n for n in _WEIGHTS if n not in _BIG and n != 'meta_tokens')
_MIX_SMALL = ('mix_norm', 'b_forget', 'fox_q_norm', 'fox_k_norm', 'swa_q_norm', 'swa_k_norm', 'swa_sinks',
              'fox_out_norm', 'swa_out_norm')


def _pack_rows(vectors):
    flat = jnp.concatenate([v.reshape(-1) for v in vectors])
    n = flat.shape[0]
    m = -(-n // (8 * LANES)) * 8
    return jnp.pad(flat, (0, m * LANES - n)).reshape(m, LANES)


def _unpack_rows(packed, shapes):
    flat = packed.reshape(-1)
    out, o = [], 0
    for s in shapes:
        n = int(np.prod(s))
        out.append(flat[o:o + n].reshape(s))
        o += n
    return out


def kernel(x, meta_tokens, ffn1_norm, ffn1_w_gate, ffn1_w_up, ffn1_w_down, mix_norm, w_in, b_forget, fox_q_norm, fox_k_norm, swa_q_norm, swa_k_norm, swa_sinks, fox_out_norm, swa_out_norm, w_out, ffn2_norm, ffn2_w_gate, ffn2_w_up, ffn2_w_down, loss_target, m_meta_tokens, m_ffn1_norm, m_ffn1_w_gate, m_ffn1_w_up, m_ffn1_w_down, m_mix_norm, m_w_in, m_b_forget, m_fox_q_norm, m_fox_k_norm, m_swa_q_norm, m_swa_k_norm, m_swa_sinks, m_fox_out_norm, m_swa_out_norm, m_w_out, m_ffn2_norm, m_ffn2_w_gate, m_ffn2_w_up, m_ffn2_w_down, v_meta_tokens, v_ffn1_norm, v_ffn1_w_gate, v_ffn1_w_up, v_ffn1_w_down, v_mix_norm, v_w_in, v_b_forget, v_fox_q_norm, v_fox_k_norm, v_swa_q_norm, v_swa_k_norm, v_swa_sinks, v_fox_out_norm, v_swa_out_norm, v_w_out, v_ffn2_norm, v_ffn2_w_gate, v_ffn2_w_up, v_ffn2_w_down):
    W = dict(meta_tokens=meta_tokens, ffn1_norm=ffn1_norm, ffn1_w_gate=ffn1_w_gate, ffn1_w_up=ffn1_w_up, ffn1_w_down=ffn1_w_down, mix_norm=mix_norm, w_in=w_in, b_forget=b_forget, fox_q_norm=fox_q_norm, fox_k_norm=fox_k_norm, swa_q_norm=swa_q_norm, swa_k_norm=swa_k_norm, swa_sinks=swa_sinks, fox_out_norm=fox_out_norm, swa_out_norm=swa_out_norm, w_out=w_out, ffn2_norm=ffn2_norm, ffn2_w_gate=ffn2_w_gate, ffn2_w_up=ffn2_w_up, ffn2_w_down=ffn2_w_down)
    Mo = dict(meta_tokens=m_meta_tokens, ffn1_norm=m_ffn1_norm, ffn1_w_gate=m_ffn1_w_gate, ffn1_w_up=m_ffn1_w_up, ffn1_w_down=m_ffn1_w_down, mix_norm=m_mix_norm, w_in=m_w_in, b_forget=m_b_forget, fox_q_norm=m_fox_q_norm, fox_k_norm=m_fox_k_norm, swa_q_norm=m_swa_q_norm, swa_k_norm=m_swa_k_norm, swa_sinks=m_swa_sinks, fox_out_norm=m_fox_out_norm, swa_out_norm=m_swa_out_norm, w_out=m_w_out, ffn2_norm=m_ffn2_norm, ffn2_w_gate=m_ffn2_w_gate, ffn2_w_up=m_ffn2_w_up, ffn2_w_down=m_ffn2_w_down)
    Vo = dict(meta_tokens=v_meta_tokens, ffn1_norm=v_ffn1_norm, ffn1_w_gate=v_ffn1_w_gate, ffn1_w_up=v_ffn1_w_up, ffn1_w_down=v_ffn1_w_down, mix_norm=v_mix_norm, w_in=v_w_in, b_forget=v_b_forget, fox_q_norm=v_fox_q_norm, fox_k_norm=v_fox_k_norm, swa_q_norm=v_swa_q_norm, swa_k_norm=v_swa_k_norm, swa_sinks=v_swa_sinks, fox_out_norm=v_fox_out_norm, swa_out_norm=v_swa_out_norm, w_out=v_w_out, ffn2_norm=v_ffn2_norm, ffn2_w_gate=v_ffn2_w_gate, ffn2_w_up=v_ffn2_w_up, ffn2_w_down=v_ffn2_w_down)

    _, S, D = x.shape
    L = S + BLOCK
    depth = ffn1_norm.shape[0]
    md = _MixDims(D)
    mx, my, mc = _mesh_pos()
    k_idx = (2 * mx + my).astype(jnp.int32).reshape(1)
    c_idx = mc.astype(jnp.int32).reshape(1)
    dcols = D // NCH

    meta_all = _allgather_devices(meta_tokens.reshape(-1, LANES), name="meta_allgather")
    meta_full = jnp.transpose(meta_all[0::2].reshape(NCH, N_META, dcols), (1, 0, 2)).reshape(N_META, D)

    order = ('ffn1_w_gate', 'ffn1_w_up', 'ffn1_w_down', 'w_in', 'w_out', 'ffn2_w_gate', 'ffn2_w_up', 'ffn2_w_down')
    chunks = []
    for l in range(depth):
        l_idx = jnp.full((1,), l, jnp.int32)
        chunks += [_cast_into_chunk(W[name], l_idx, k_idx, name="cast_chunk") for name in order]
    gather = _GatherQueue(chunks)
    wts = [{} for _ in range(depth)]

    def weight(l, i):
        if i not in wts[l]:
            buf = gather.bufs[len(order) * l + i]
            if order[i] == 'w_in':
                buf = _win_to_mine(jnp.transpose(buf, (1, 0, 2)).reshape(D, NCH * buf.shape[2]), md)
            elif order[i] in ('w_out', 'ffn1_w_down', 'ffn2_w_down'):
                buf = buf.reshape(-1, D)
            wts[l][i] = buf
        return wts[l][i]

    h = jnp.concatenate([jnp.zeros((PAD, D), F32), meta_full, x[0]], axis=0)
    saved = []
    for l in range(depth):
        sp = _mix_small({n: W[n][l] for n in _MIX_SMALL}, md)
        h, s1 = _ffn_fwd(h, ffn1_norm[l].reshape(1, D), lambda i: weight(l, i), "ffn", gather)
        h, s2 = _mix_fwd(h, sp, lambda i: weight(l, 3 + i), md, gather)
        h, s3 = _ffn_fwd(h, ffn2_norm[l].reshape(1, D), lambda i: weight(l, 5 + i), "ffn", gather)
        saved.append((s1, s2, s3, sp))
    wts = [dict(g1=w[0], u1=w[1], d1=w[2], wi=w[3], wo=w[4], g2=w[5], u2=w[6], d2=w[7]) for w in wts]

    loss_part, dh = _loss_grad(h, loss_target[0], name="loss_grad")

    small_grads = {n: [None] * depth for n in _SMALL}
    stacks = {n: None for n in _BIG}

    def update(name, l, grad):
        stacks[name] = _adamw_layer(W[name], Mo[name], Vo[name], grad, jnp.full((1,), l, jnp.int32), stacks[name],
                                    name="adamw_layer")

    queue = _ExchangeQueue()

    def reduce_later(names, l, parts, tag):
        def done(grads):
            for n, grad in zip(names, grads):
                update(n, l, grad)
        _reduce_scatter_later(queue, parts, c_idx, k_idx, tag, done)

    def ffn_dw(names, l):
        return lambda i, dw: reduce_later((names[i],), l, [dw.reshape(NCH, -1, D) if i == 2 else dw], "ffn")

    for l in range(depth - 1, -1, -1):
        wl = wts[l]
        s1, s2, s3, sp = saved[l]
        dh, dg = _ffn_bwd(dh, s3, ffn2_norm[l].reshape(1, D), wl['g2'], wl['u2'], wl['d2'], "ffn", queue,
                          ffn_dw(('ffn2_w_gate', 'ffn2_w_up', 'ffn2_w_down'), l))
        small_grads['ffn2_norm'][l] = dg.reshape(-1)

        dh, dwi, dwo, sm = _mix_bwd(dh, s2, sp, wl['wi'], wl['wo'], md, queue)
        for n in _MIX_SMALL:
            small_grads[n][l] = sm[n]
        dwi = _win_grad_to_ref(dwi, md)
        dwi = jnp.transpose(dwi.reshape(D, NCH, -1), (1, 0, 2))
        reduce_later(('w_in', 'w_out'), l, [dwi, dwo.reshape(NCH, -1, D)], "mix")

        dh, dg = _ffn_bwd(dh, s1, ffn1_norm[l].reshape(1, D), wl['g1'], wl['u1'], wl['d1'], "ffn", queue,
                          ffn_dw(('ffn1_w_gate', 'ffn1_w_up', 'ffn1_w_down'), l))
        small_grads['ffn1_norm'][l] = dg.reshape(-1)
    queue.drain("rs_chips")

    grad_x = dh[BLOCK:][None]

    small_shapes = [W[n].shape for n in _SMALL]
    parts = [jnp.stack(small_grads[n]) for n in _SMALL] + [dh[PAD:BLOCK], loss_part[0, :1]]
    packed = _pack_rows(parts)
    total = _sum_devices(_allgather_devices(packed, name="small_allgather"), name="small_sum")
    *g_small, g_meta, loss = _unpack_rows(total, small_shapes + [(N_META, D), (1,)])
    g_meta = lax.dynamic_slice(g_meta, (0, k_idx[0] * dcols), (N_META, dcols))

    sw = _pack_rows([W[n] for n in _SMALL])
    sd, smm, svv = _adamw_flat(sw, _pack_rows(g_small), _pack_rows([Mo[n] for n in _SMALL]),
                               _pack_rows([Vo[n] for n in _SMALL]), name="adamw_small")
    d_small, m_small, v_small = (_unpack_rows(t, small_shapes) for t in (sd, smm, svv))
    d_meta, m_meta, v_meta = _adamw_flat(meta_tokens, g_meta, m_meta_tokens, v_meta_tokens, name="adamw_meta")

    grads, deltas, new_m, new_v = {}, {}, {}, {}
    for n in _BIG:
        grads[n], deltas[n], new_m[n], new_v[n] = stacks[n]
    for i, n in enumerate(_SMALL):
        grads[n], deltas[n], new_m[n], new_v[n] = g_small[i], d_small[i], m_small[i], v_small[i]
    grads['meta_tokens'], deltas['meta_tokens'], new_m['meta_tokens'], new_v['meta_tokens'] = g_meta, d_meta, m_meta, v_meta
    return (loss.reshape(()), grad_x, *[grads[n] for n in _WEIGHTS], *[deltas[n] for n in _WEIGHTS],
            *[new_m[n] for n in _WEIGHTS], *[new_v[n] for n in _WEIGHTS])
```

```python
import numpy as np
import jax
import jax.numpy as jnp
from jax import lax
from jax.experimental import pallas as pl
from jax.experimental.pallas import tpu as pltpu

F32 = jnp.float32
BF16 = jnp.bfloat16

HEAD_DIM = 64
N_META = 16
BLOCK = 128
WINDOW = 128
PAD = BLOCK - N_META
EPS = 1e-6
NEG_INF = -1e30
SWA_GROUP = 8
NCH = 4
LANES = 128
QBLOCK = 512

ADAM_LR = 0.001
ADAM_B1 = 0.9
ADAM_B2 = 0.999
ADAM_EPS = 1e-08
ADAM_WD = 0.01
ADAM_STEP = 10

V7X_VMEM_BYTES = 64 * 1024 * 1024
VMEM_LIMIT = V7X_VMEM_BYTES - 8 * 1024 * 1024
MESH = pl.DeviceIdType.MESH
HIGHEST = lax.Precision.HIGHEST

_TM = (1088, 1024, 704, 512, 384, 256, 128)
_TN = (1408, 1024, 768, 512, 384, 256, 128)
_TK = (2176, 1408, 1024, 512, 384, 256, 128)
_TR = (544, 512, 384, 272, 256, 128)


def _pick(n, cands):
    for c in cands:
        if n % c == 0:
            return c
    return n


def _cparams(*sem):
    return pltpu.CompilerParams(dimension_semantics=sem if sem else None, vmem_limit_bytes=VMEM_LIMIT)


def _matmul(a, b, *, name, nt=False, b_chunked=False, out_chunked=False, out_dtype=F32,
            residual=None, scale=1.0, extras=(), epilogue=None, out_dtypes=None, comm=None):
    M, K = a.shape
    if not nt:
        N = b.shape[-1] * (NCH if b_chunked else 1)
        assert b.shape[-2] == K
        k_unit = K
    else:
        N = b.shape[-2]
        k_unit = b.shape[-1]
        assert k_unit * (NCH if b_chunked else 1) == K
    n_unit = N // NCH if (out_chunked or (b_chunked and not nt)) else N
    tm, tn, tk = _pick(M, _TM), _pick(n_unit, _TN), _pick(k_unit, _TK)
    if epilogue is None:
        extras = () if residual is None else (residual,)
        out_dtypes = (out_dtype,)

        def epilogue(acc, *res):
            r = acc * scale if scale != 1.0 else acc
            return (r + res[0] if res else r,)
    n_out = len(out_dtypes)
    n_temps = 4 if n_out > 1 else 0
    assert not (extras and out_chunked)

    def est(tm_):
        return (2 * tm_ * tk * 2 + 2 * tk * tn * 2 + tm_ * tn * 4
                + sum(2 * tm_ * tn * jnp.dtype(d).itemsize for d in out_dtypes)
                + sum(2 * tm_ * tn * e.dtype.itemsize for e in extras) + n_temps * tm_ * tn * 4)

    while est(tm) > VMEM_LIMIT * 3 // 4 and tm % 32 == 0:
        tm //= 2
    npc, kpc = n_unit // tn, k_unit // tk
    nk = K // tk
    grid = (M // tm, N // tn, nk)

    a_spec = pl.BlockSpec((tm, tk), lambda i, j, k: (i, k))
    if not nt:
        if b_chunked:
            b_spec = pl.BlockSpec((None, tk, tn), lambda i, j, k: (j // npc, k, j % npc))
        else:
            b_spec = pl.BlockSpec((tk, tn), lambda i, j, k: (k, j))
        dims = (((1,), (0,)), ((), ()))
    else:
        if b_chunked:
            b_spec = pl.BlockSpec((None, tn, tk), lambda i, j, k: (k // kpc, j, k % kpc))
        else:
            b_spec = pl.BlockSpec((tn, tk), lambda i, j, k: (j, k))
        dims = (((1,), (1,)), ((), ()))
    tile = pl.BlockSpec((tm, tn), lambda i, j, k: (i, j))
    if out_chunked:
        o_spec = pl.BlockSpec((None, tm, tn), lambda i, j, k: (j // npc, i, j % npc))
        out_shapes = [jax.ShapeDtypeStruct((NCH, M, n_unit), d) for d in out_dtypes]
    else:
        o_spec = tile
        out_shapes = [jax.ShapeDtypeStruct((M, N), d) for d in out_dtypes]
    in_specs = [a_spec, b_spec] + [tile] * len(extras)
    args = [a, b, *extras]

    n_main = len(args)
    n_cin = 0 if comm is None else len(comm.operands)
    n_cout = 0 if comm is None else len(comm.out_shapes)

    def body(*refs):
        a_ref, b_ref = refs[0], refs[1]
        e_refs = refs[2:n_main]
        o_refs = refs[n_main + n_cin:n_main + n_cin + n_out]
        acc_ref = refs[n_main + n_cin + n_out + n_cout]
        i, j, k = pl.program_id(0), pl.program_id(1), pl.program_id(2)
        if comm is not None:
            cin = refs[n_main:n_main + n_cin]
            cout = refs[n_main + n_cin + n_out:n_main + n_cin + n_out + n_cout]
            send_sems, recv_sems = refs[-2:]

            @pl.when((i == 0) & (j == 0) & (k == 0))
            def _():
                for cp, _ in comm.copies(cin, cout, send_sems, recv_sems):
                    cp.start()

        @pl.when(k == 0)
        def _():
            acc_ref[...] = jnp.zeros_like(acc_ref)

        acc_ref[...] += lax.dot_general(a_ref[...], b_ref[...], dims, preferred_element_type=F32)

        @pl.when(k == nk - 1)
        def _():
            for o_ref, val in zip(o_refs, epilogue(acc_ref[...], *[e[...] for e in e_refs])):
                o_ref[...] = val.astype(o_ref.dtype)

        if comm is not None:
            @pl.when((i == grid[0] - 1) & (j == grid[1] - 1) & (k == nk - 1))
            def _():
                cps = comm.copies(cin, cout, send_sems, recv_sems)
                for _, arrival in cps:
                    arrival.wait_recv()
                for cp, _ in cps:
                    cp.wait_send()

    scratch = [pltpu.VMEM((tm, tn), F32)]
    if comm is None:
        outs = pl.pallas_call(
            body, name=name, grid=grid, in_specs=in_specs, out_specs=[o_spec] * n_out, out_shape=out_shapes,
            scratch_shapes=scratch, compiler_params=_cparams("parallel", "parallel", "arbitrary"),
        )(*args)
    else:
        scratch += [pltpu.SemaphoreType.DMA((comm.n_sems,)), pltpu.SemaphoreType.DMA((comm.n_sems,))]
        outs = pl.pallas_call(
            body, name=name, grid=grid, in_specs=in_specs + [_ANY] * n_cin,
            out_specs=[o_spec] * n_out + [_ANY] * n_cout, out_shape=out_shapes + list(comm.out_shapes),
            input_output_aliases={n_main + s: n_out + d for s, d in comm.aliases.items()},
            scratch_shapes=scratch, compiler_params=_cparams("arbitrary", "arbitrary", "arbitrary"),
        )(*args, *comm.operands)
    return outs[0] if len(outs) == 1 else tuple(outs)


def _call_carrying(body, comm, args, *, name, grid, in_specs, out_specs, out_shape, scratch_shapes):
    if comm is None:
        return pl.pallas_call(body, name=name, grid=grid, in_specs=in_specs, out_specs=out_specs, out_shape=out_shape,
                              scratch_shapes=scratch_shapes, compiler_params=_cparams("arbitrary"))(*args)
    n_in, n_out, n_scr = len(in_specs), len(out_specs), len(scratch_shapes)
    n_cin, n_cout = len(comm.operands), len(comm.out_shapes)

    def carrying(*refs):
        ins, cin = refs[:n_in], refs[n_in:n_in + n_cin]
        outs = refs[n_in + n_cin:n_in + n_cin + n_out]
        cout = refs[n_in + n_cin + n_out:n_in + n_cin + n_out + n_cout]
        scr = refs[n_in + n_cin + n_out + n_cout:n_in + n_cin + n_out + n_cout + n_scr]
        send_sems, recv_sems = refs[-2:]
        step = pl.program_id(0)

        @pl.when(step == 0)
        def _():
            for cp, _ in comm.copies(cin, cout, send_sems, recv_sems):
                cp.start()

        body(*ins, *outs, *scr)

        @pl.when(step == grid[0] - 1)
        def _():
            cps = comm.copies(cin, cout, send_sems, recv_sems)
            for _, arrival in cps:
                arrival.wait_recv()
            for cp, _ in cps:
                cp.wait_send()

    return pl.pallas_call(
        carrying, name=name, grid=grid, in_specs=list(in_specs) + [_ANY] * n_cin,
        out_specs=list(out_specs) + [_ANY] * n_cout, out_shape=list(out_shape) + list(comm.out_shapes),
        input_output_aliases={n_in + s: n_out + d for s, d in comm.aliases.items()},
        scratch_shapes=list(scratch_shapes) + [pltpu.SemaphoreType.DMA((comm.n_sems,)), pltpu.SemaphoreType.DMA((comm.n_sems,))],
        compiler_params=_cparams("arbitrary"),
    )(*args, *comm.operands)


def _carried(queue, fn, n_out, *args, **kw):
    job = queue.take() if queue is not None else None
    outs = fn(*args, comm=job, **kw)
    if job is not None:
        queue.give(outs[n_out:])
    return outs[:n_out]


def _transpose(x, *, name):
    M, N = x.shape
    tc = _pick(N, (512, 384, 256, 128))

    def body(x_ref, o_ref):
        o_ref[...] = x_ref[...].astype(F32).T.astype(o_ref.dtype)

    return pl.pallas_call(
        body, name=name, grid=(N // tc,),
        in_specs=[pl.BlockSpec((M, tc), lambda j: (0, j))],
        out_specs=pl.BlockSpec((tc, M), lambda j: (j, 0)),
        out_shape=jax.ShapeDtypeStruct((N, M), x.dtype),
        compiler_params=_cparams("parallel"),
    )(x)


def _rms_fwd(h, g, *, name):
    L, D = h.shape
    tr = _pick(L, _TR)

    def body(h_ref, g_ref, o_ref):
        x = h_ref[...]
        r = lax.rsqrt(jnp.mean(x * x, axis=-1, keepdims=True) + EPS)
        o_ref[...] = (x * r * g_ref[...]).astype(o_ref.dtype)

    return pl.pallas_call(
        body, name=name, grid=(L // tr,),
        in_specs=[pl.BlockSpec((tr, D), lambda i: (i, 0)), pl.BlockSpec((1, D), lambda i: (0, 0))],
        out_specs=pl.BlockSpec((tr, D), lambda i: (i, 0)),
        out_shape=jax.ShapeDtypeStruct((L, D), BF16),
        compiler_params=_cparams("parallel"),
    )(h, g)


def _rms_bwd(dy, h, g, dh, *, name):
    L, D = h.shape
    tr = _pick(L, _TR)

    def body(dy_ref, h_ref, g_ref, dh_ref, o_ref, dg_ref):
        i = pl.program_id(0)
        x = h_ref[...]
        dyv = dy_ref[...]
        r = lax.rsqrt(jnp.mean(x * x, axis=-1, keepdims=True) + EPS)
        xh = x * r
        dxh = dyv * g_ref[...]
        dx = r * (dxh - xh * jnp.mean(dxh * xh, axis=-1, keepdims=True))
        o_ref[...] = dh_ref[...] + dx
        part = jnp.sum(dyv * xh, axis=0, keepdims=True)

        @pl.when(i == 0)
        def _():
            dg_ref[...] = part

        @pl.when(i > 0)
        def _():
            dg_ref[...] += part

    row = pl.BlockSpec((tr, D), lambda i: (i, 0))
    vec = pl.BlockSpec((1, D), lambda i: (0, 0))
    return pl.pallas_call(
        body, name=name, grid=(L // tr,),
        in_specs=[row, row, vec, row], out_specs=[row, vec],
        out_shape=[jax.ShapeDtypeStruct((L, D), F32), jax.ShapeDtypeStruct((1, D), F32)],
        compiler_params=_cparams("arbitrary"),
    )(dy, h, g, dh)


def _swiglu_epilogue(up, gate):
    g = gate.astype(F32)
    return up, g * jax.nn.sigmoid(g) * up


def _swiglu_bwd_epilogue(acc, gate, up):
    d = 0.5 * acc
    g = gate.astype(F32)
    sg = jax.nn.sigmoid(g)
    return d * up.astype(F32) * sg * (1.0 + g * (1.0 - sg)), d * g * sg


def _cast_bf16(x, *, name):
    L, D = x.shape
    tr = _pick(L, _TR)

    def body(x_ref, o_ref):
        o_ref[...] = x_ref[...].astype(o_ref.dtype)

    blk = pl.BlockSpec((tr, D), lambda i: (i, 0))
    return pl.pallas_call(
        body, name=name, grid=(L // tr,), in_specs=[blk], out_specs=blk,
        out_shape=jax.ShapeDtypeStruct((L, D), BF16), compiler_params=_cparams("parallel"),
    )(x)


def _loss_grad(h, target, *, name):
    L, D = h.shape
    S = target.shape[0]
    nb = L // BLOCK

    def body(h_ref, t_ref, loss_ref, dh_ref):
        i = pl.program_id(0)

        @pl.when(i == 0)
        def _():
            loss_ref[...] = jnp.zeros_like(loss_ref)
            dh_ref[...] = jnp.zeros_like(dh_ref)

        @pl.when(i > 0)
        def _():
            err = h_ref[...] - t_ref[...]
            dh_ref[...] = err * (1.0 / D)
            loss_ref[...] += jnp.full(loss_ref.shape, (0.5 / D) * jnp.sum(err * err), F32)

    return pl.pallas_call(
        body, name=name, grid=(nb,),
        in_specs=[pl.BlockSpec((BLOCK, D), lambda i: (i, 0)),
                  pl.BlockSpec((BLOCK, D), lambda i: (jnp.maximum(i - 1, 0), 0))],
        out_specs=[pl.BlockSpec((1, LANES), lambda i: (0, 0)), pl.BlockSpec((BLOCK, D), lambda i: (i, 0))],
        out_shape=[jax.ShapeDtypeStruct((1, LANES), F32), jax.ShapeDtypeStruct((L, D), F32)],
        compiler_params=_cparams("arbitrary"),
    )(h, target)


def _ffn_fwd(h, g, w, tag, queue=None):
    hn = _rms_fwd(h, g, name=f"{tag}_rms")
    gate = _mm(queue, hn, w(0), name=f"{tag}_gate", b_chunked=True, out_dtype=BF16)
    up, act = _mm(queue, hn, w(1), name=f"{tag}_up", b_chunked=True, extras=(gate,), epilogue=_swiglu_epilogue,
                  out_dtypes=(BF16, BF16))
    h_out = _mm(queue, act, w(2), name=f"{tag}_down", residual=h, scale=0.5)
    return h_out, (h, hn, gate, up, act)


def _ffn_bwd(dh, saved, g, wg, wu, wd, tag, queue=None, on_dw=None):
    h, hn, gate, up, act = saved
    dout = _cast_bf16(dh, name=f"{tag}_dout")
    dgate, dup = _mm(queue, dout, wd, name=f"{tag}_dact", nt=True, extras=(gate, up), epilogue=_swiglu_bwd_epilogue,
                     out_dtypes=(BF16, BF16))
    actT = _transpose(act, name=f"{tag}_actT")
    on_dw(2, _mm(queue, actT, dout, name=f"{tag}_dwd", out_dtype=BF16, scale=0.5))
    hnT = _transpose(hn, name=f"{tag}_hnT")
    on_dw(0, _mm(queue, hnT, dgate, name=f"{tag}_dwg", out_chunked=True, out_dtype=BF16))
    on_dw(1, _mm(queue, hnT, dup, name=f"{tag}_dwu", out_chunked=True, out_dtype=BF16))
    dhn = _mm(queue, dgate, wg, name=f"{tag}_dhn_g", nt=True, b_chunked=True)
    dhn = _mm(queue, dup, wu, name=f"{tag}_dhn_u", nt=True, b_chunked=True, residual=dhn)
    return _rms_bwd(dhn, h, g, dh, name=f"{tag}_drms")


class _MixDims:
    def __init__(self, d_model):
        self.wf = d_model // 2
        self.ws = d_model // 2
        self.pf = self.wf // LANES
        self.ps = self.ws // LANES
        self.hf = self.wf // HEAD_DIM
        self.hq = self.ws // HEAD_DIM
        self.nkv = max(1, self.hq // SWA_GROUP)
        self.g = self.hq // self.nkv
        self.bq_f, self.bk_f, self.bv_f = 0, self.pf, 2 * self.pf
        self.bq_s = 3 * self.pf
        self.bk_s = self.bq_s + self.ps
        self.bv_s = self.bk_s + self.nkv
        self.bz = self.bv_s + self.nkv
        self.nu = (self.bz + 1) * LANES
        self.nup = -(-self.nu // 512) * 512
        self.in_width = 3 * self.wf + self.hf + self.ws + 2 * self.nkv * HEAD_DIM
        assert self.hf <= 2 * (LANES // 8)

    def gate_lane(self, h):
        return 8 * (h // 2) + h % 2

    def column_map(self):
        wf, ws, hd = self.wf, self.ws, HEAD_DIM
        src = np.full((self.nup,), -1, np.int64)
        src[0:3 * wf] = np.arange(3 * wf)
        o_sq = 3 * wf + self.hf
        src[self.bq_s * LANES:self.bq_s * LANES + ws] = o_sq + np.arange(ws)
        o_sk = o_sq + ws
        o_sv = o_sk + self.nkv * hd
        for kv in range(self.nkv):
            for rep in range(2):
                c0 = (self.bk_s + kv) * LANES + rep * hd
                src[c0:c0 + hd] = o_sk + kv * hd + np.arange(hd)
                c0 = (self.bv_s + kv) * LANES + rep * hd
                src[c0:c0 + hd] = o_sv + kv * hd + np.arange(hd)
        for h in range(self.hf):
            src[self.bz * LANES + self.gate_lane(h)] = 3 * wf + h
        return src

    def grad_column_map(self):
        src = self.column_map()
        dst = np.zeros((self.in_width,), np.int64)
        for col in range(self.nup - 1, -1, -1):
            if src[col] >= 0:
                dst[src[col]] = col
        return dst


def _block_diag_mean():
    m = np.zeros((LANES, LANES), np.float32)
    m[:HEAD_DIM, :HEAD_DIM] = 1.0 / HEAD_DIM
    m[HEAD_DIM:, HEAD_DIM:] = 1.0 / HEAD_DIM
    return jnp.asarray(m)


def _fold_halves():
    m = np.eye(LANES, dtype=np.float32)
    m[np.arange(LANES), (np.arange(LANES) + HEAD_DIM) % LANES] = 1.0
    return jnp.asarray(m)


def _gate_expand(md):
    e = np.zeros((LANES, md.wf), np.float32)
    for h in range(md.hf):
        e[md.gate_lane(h), h * HEAD_DIM:(h + 1) * HEAD_DIM] = 1.0
    return jnp.asarray(e)


def _qblocks(L, qb):
    blocks = [(0, BLOCK)]
    r = BLOCK
    while r < L:
        blocks.append((r, qb))
        r += qb
    assert r == L
    return blocks


def _f32dot(a, b):
    return jnp.dot(a, b, precision=HIGHEST, preferred_element_type=F32)


_DIMS_NT = (((1,), (1,)), ((), ()))


def _dot_nt(a, b):
    return lax.dot_general(a, b, _DIMS_NT, preferred_element_type=F32)


def _dot_tn(a, b):
    return jnp.dot(a.T.astype(BF16), b, preferred_element_type=F32)


def _log_sigmoid(z):
    return jnp.minimum(z, 0.0) - jnp.log(1.0 + jnp.exp(-jnp.abs(z)))


def _gate_fwd(u, b, md, *, name):
    L = u.shape[0]
    nb = L // BLOCK
    expand = _gate_expand(md)

    def body(z_ref, b_ref, e_ref, cexp_ref, ct_ref, c_s):
        ri = lax.broadcasted_iota(jnp.int32, (BLOCK, BLOCK), 0)
        ci = lax.broadcasted_iota(jnp.int32, (BLOCK, BLOCK), 1)
        tri = (ri >= ci).astype(F32)
        carry = jnp.zeros((1, LANES), F32)
        for bi in range(nb):
            rows = pl.ds(bi * BLOCK, BLOCK)
            logf = _log_sigmoid(z_ref[rows, :] + b_ref[...])
            blk = _f32dot(tri, logf) + carry
            c_s[rows, :] = blk
            carry = blk[BLOCK - 1:BLOCK, :]
        c = c_s[...]
        ct_ref[...] = c.T
        cexp_ref[...] = _f32dot(c, e_ref[...])

    return pl.pallas_call(
        body, name=name, grid=(1,),
        in_specs=[pl.BlockSpec((L, LANES), lambda i: (0, md.bz)), pl.BlockSpec((1, LANES), lambda i: (0, 0)),
                  pl.BlockSpec((LANES, md.wf), lambda i: (0, 0))],
        out_specs=[pl.BlockSpec((L, md.wf), lambda i: (0, 0)), pl.BlockSpec((LANES, L), lambda i: (0, 0))],
        out_shape=[jax.ShapeDtypeStruct((L, md.wf), F32), jax.ShapeDtypeStruct((LANES, L), F32)],
        scratch_shapes=[pltpu.VMEM((L, LANES), F32)],
        compiler_params=_cparams("arbitrary"),
    )(u, b, expand)


def _gate_bwd(u, b, dck_t, md, *, name):
    L = u.shape[0]
    nb = L // BLOCK

    def body(z_ref, b_ref, dck_ref, dz_ref, db_ref, dc_s):
        ri = lax.broadcasted_iota(jnp.int32, (BLOCK, BLOCK), 0)
        ci = lax.broadcasted_iota(jnp.int32, (BLOCK, BLOCK), 1)
        triu = (ri <= ci).astype(F32)
        dc_s[...] = -dck_ref[...].T
        carry = jnp.zeros((1, LANES), F32)
        db = jnp.zeros((1, LANES), F32)
        for bi in range(nb - 1, -1, -1):
            rows = pl.ds(bi * BLOCK, BLOCK)
            blk = _f32dot(triu, dc_s[rows, :]) + carry
            carry = blk[0:1, :]
            z = z_ref[rows, :] + b_ref[...]
            dz = blk * jax.nn.sigmoid(-z)
            if bi == 0:
                dz = jnp.where(lax.broadcasted_iota(jnp.int32, (BLOCK, LANES), 0) >= PAD, dz, 0.0)
            dz_ref[rows, :] = dz
            db = db + jnp.sum(dz, axis=0, keepdims=True)
        db_ref[...] = db

    return pl.pallas_call(
        body, name=name, grid=(1,),
        in_specs=[pl.BlockSpec((L, LANES), lambda i: (0, md.bz)), pl.BlockSpec((1, LANES), lambda i: (0, 0)),
                  pl.BlockSpec((LANES, L), lambda i: (0, 0))],
        out_specs=[pl.BlockSpec((L, LANES), lambda i: (0, 0)), pl.BlockSpec((1, LANES), lambda i: (0, 0))],
        out_shape=[jax.ShapeDtypeStruct((L, LANES), F32), jax.ShapeDtypeStruct((1, LANES), F32)],
        scratch_shapes=[pltpu.VMEM((L, LANES), F32)],
        compiler_params=_cparams("arbitrary"),
    )(u, b, dck_t)


def _head_norm(x, g, bd):
    r = lax.rsqrt(_f32dot(x * x, bd) + EPS)
    xh = x * r
    return xh * g, xh, r


def _head_norm_bwd(dy, xh, r, g, bd):
    dxh = dy * g
    dx = r * (dxh - xh * _f32dot(dxh * xh, bd))
    return dx, jnp.sum(dy * xh, axis=0, keepdims=True)


def _lane_half():
    return lax.broadcasted_iota(jnp.int32, (1, LANES), 1) < HEAD_DIM


def _store_head_pair(x, half, a_s, b_s):
    a_s[...] = jnp.where(half, x, 0.0).astype(BF16)
    b_s[...] = jnp.where(half, 0.0, x).astype(BF16)


def _fox_scores(qm, kn_s, cexp_ref, ct_ref, r0, nr, klen, hh):
    s = _dot_nt(qm, kn_s[0:klen, :]) * (HEAD_DIM ** -0.5)
    s = s + cexp_ref[r0:r0 + nr, HEAD_DIM * hh:HEAD_DIM * hh + 1] - ct_ref[hh:hh + 1, 0:klen]
    qp = r0 + lax.broadcasted_iota(jnp.int32, (nr, klen), 0)
    kp = lax.broadcasted_iota(jnp.int32, (nr, klen), 1)
    return jnp.where((kp <= qp) & (kp >= PAD), s, NEG_INF)


def _fox_fwd(u, cexp, ct3, qg, kg, md, *, name, comm=None):
    L = u.shape[0]
    blocks = _qblocks(L, QBLOCK)
    bd = _block_diag_mean()

    def body(q_ref, k_ref, v_ref, cexp_ref, ct_ref, qg_ref, kg_ref, bd_ref, o_ref, lse_ref, qa_s, qb_s, kn_s, v_s):
        half = _lane_half()
        bdv = bd_ref[...]
        _store_head_pair(_head_norm(q_ref[...], qg_ref[...], bdv)[0], half, qa_s, qb_s)
        kn_s[...] = _head_norm(k_ref[...], kg_ref[...], bdv)[0].astype(BF16)
        v_s[...] = v_ref[...].astype(BF16)
        for r0, nr in blocks:
            klen = r0 + nr
            o_blk = lse_blk = None
            for hh, q_s in enumerate((qa_s, qb_s)):
                s = _fox_scores(q_s[r0:r0 + nr, :], kn_s, cexp_ref, ct_ref, r0, nr, klen, hh)
                m = jnp.max(s, axis=-1, keepdims=True)
                p = jnp.exp(s - m)
                l = jnp.sum(p, axis=-1, keepdims=True)
                oh = jnp.dot(p.astype(BF16), v_s[0:klen, :], preferred_element_type=F32) * (1.0 / l)
                lh = jnp.broadcast_to(jnp.where(m > 0.5 * NEG_INF, m + jnp.log(l), 0.0), (nr, LANES))
                o_blk = oh if hh == 0 else jnp.where(half, o_blk, oh)
                lse_blk = lh if hh == 0 else jnp.where(half, lse_blk, lh)
            o_ref[r0:r0 + nr, :] = o_blk
            lse_ref[r0:r0 + nr, :] = lse_blk

    col = lambda base: pl.BlockSpec((L, LANES), lambda j: (0, base + j))
    vec = pl.BlockSpec((1, LANES), lambda j: (0, 0))
    return _call_carrying(
        body, comm, (u, u, u, cexp, ct3, qg, kg, bd), name=name, grid=(md.pf,),
        in_specs=[col(md.bq_f), col(md.bk_f), col(md.bv_f), col(0),
                  pl.BlockSpec((None, 8, L), lambda j: (j, 0, 0)), vec, vec,
                  pl.BlockSpec((LANES, LANES), lambda j: (0, 0))],
        out_specs=[col(0), col(0)],
        out_shape=[jax.ShapeDtypeStruct((L, md.wf), F32)] * 2,
        scratch_shapes=[pltpu.VMEM((L, LANES), BF16)] * 4)


def _softmax_bwd(p, dp):
    pdp = p * dp
    return pdp - p * jnp.sum(pdp, axis=-1, keepdims=True)


def _fox_bwd(u, cexp, ct3, qg, kg, do, lse, md, *, name, comm=None):
    L = u.shape[0]
    blocks = _qblocks(L, QBLOCK // 2)
    bd = _block_diag_mean()
    fold = _fold_halves()
    npairs = md.pf

    def body(q_ref, k_ref, v_ref, cexp_ref, ct_ref, qg_ref, kg_ref, bd_ref, fold_ref, do_ref, lse_ref,
             dq_ref, dk_ref, dv_ref, dck_ref, dqg_ref, dkg_ref,
             qa_s, qb_s, kn_s, v_s, doa_s, dob_s, dqn_s, dkn_s, dvv_s):
        j = pl.program_id(0)
        half = _lane_half()
        bdv = bd_ref[...]
        qy, qh, rq = _head_norm(q_ref[...], qg_ref[...], bdv)
        ky, kh, rk = _head_norm(k_ref[...], kg_ref[...], bdv)
        _store_head_pair(qy, half, qa_s, qb_s)
        _store_head_pair(do_ref[...], half, doa_s, dob_s)
        kn_s[...] = ky.astype(BF16)
        v_s[...] = v_ref[...].astype(BF16)
        dkn_s[...] = jnp.zeros_like(dkn_s)
        dvv_s[...] = jnp.zeros_like(dvv_s)
        dck_ref[...] = jnp.zeros_like(dck_ref)
        for r0, nr in blocks:
            klen = r0 + nr
            dq_blk = None
            for hh, (q_s, do_s) in enumerate(((qa_s, doa_s), (qb_s, dob_s))):
                c0 = HEAD_DIM * hh
                qm = q_s[r0:r0 + nr, :]
                dom = do_s[r0:r0 + nr, :]
                s = _fox_scores(qm, kn_s, cexp_ref, ct_ref, r0, nr, klen, hh)
                p = jnp.exp(s - lse_ref[r0:r0 + nr, c0:c0 + 1])
                ds = _softmax_bwd(p, _dot_nt(dom, v_s[0:klen, :]))
                dck_ref[hh:hh + 1, 0:klen] += jnp.sum(ds, axis=0, keepdims=True)
                ds = ds * (HEAD_DIM ** -0.5)
                dq_h = jnp.dot(ds.astype(BF16), kn_s[0:klen, :], preferred_element_type=F32)
                dkn_s[0:klen, :] += _dot_tn(ds, qm)
                dvv_s[0:klen, :] += _dot_tn(p, dom)
                dq_blk = dq_h if hh == 0 else jnp.where(half, dq_blk, dq_h)
            dqn_s[r0:r0 + nr, :] = dq_blk
        dq, dqg = _head_norm_bwd(dqn_s[...], qh, rq, qg_ref[...], bdv)
        dk, dkg = _head_norm_bwd(dkn_s[...], kh, rk, kg_ref[...], bdv)
        dq_ref[...] = dq
        dk_ref[...] = dk
        dv_ref[...] = dvv_s[...]

        @pl.when(j == 0)
        def _():
            dqg_ref[...] = jnp.zeros_like(dqg_ref)
            dkg_ref[...] = jnp.zeros_like(dkg_ref)

        dqg_ref[...] += dqg
        dkg_ref[...] += dkg

        @pl.when(j == npairs - 1)
        def _():
            dqg_ref[...] = _f32dot(jnp.broadcast_to(dqg_ref[...], (8, LANES)), fold_ref[...])[0:1, :]
            dkg_ref[...] = _f32dot(jnp.broadcast_to(dkg_ref[...], (8, LANES)), fold_ref[...])[0:1, :]

    col = lambda base: pl.BlockSpec((L, LANES), lambda j: (0, base + j))
    vec = pl.BlockSpec((1, LANES), lambda j: (0, 0))
    sq = pl.BlockSpec((LANES, LANES), lambda j: (0, 0))
    ct_spec = pl.BlockSpec((None, 8, L), lambda j: (j, 0, 0))
    big = jax.ShapeDtypeStruct((L, md.wf), F32)
    small = jax.ShapeDtypeStruct((1, LANES), F32)
    return _call_carrying(
        body, comm, (u, u, u, cexp, ct3, qg, kg, bd, fold, do, lse), name=name, grid=(md.pf,),
        in_specs=[col(md.bq_f), col(md.bk_f), col(md.bv_f), col(0), ct_spec, vec, vec, sq, sq, col(0), col(0)],
        out_specs=[col(0), col(0), col(0), ct_spec, vec, vec],
        out_shape=[big, big, big, jax.ShapeDtypeStruct((md.pf, 8, L), F32), small, small],
        scratch_shapes=[pltpu.VMEM((L, LANES), BF16)] * 6 + [pltpu.VMEM((L, LANES), F32)] * 3)


def _swa_scores(qm, kn_s, slope, k0, r0, nr, klen):
    s = _dot_nt(qm, kn_s[k0:k0 + klen, :]) * (HEAD_DIM ** -0.5)
    qp = r0 + lax.broadcasted_iota(jnp.int32, (nr, klen), 0)
    kp = k0 + lax.broadcasted_iota(jnp.int32, (nr, klen), 1)
    dist = qp - kp
    s = s - slope * dist.astype(F32)
    return jnp.where((dist >= 0) & (dist < WINDOW) & (kp >= PAD), s, NEG_INF)


def _swa_fwd(u, sinkp, slopep, qg, kg, md, *, name, comm=None):
    L = u.shape[0]
    blocks = _qblocks(L, QBLOCK)
    bd = _block_diag_mean()

    def body(q_ref, k_ref, v_ref, sink_ref, slope_ref, qg_ref, kg_ref, bd_ref, o_ref, lse_ref, qa_s, qb_s, kn_s, v_s):
        half = _lane_half()
        bdv = bd_ref[...]
        _store_head_pair(_head_norm(q_ref[...], qg_ref[...], bdv)[0], half, qa_s, qb_s)
        kn_s[...] = _head_norm(k_ref[...], kg_ref[...], bdv)[0].astype(BF16)
        v_s[...] = v_ref[...].astype(BF16)
        for r0, nr in blocks:
            k0 = max(r0 - BLOCK, 0)
            klen = r0 + nr - k0
            o_blk = lse_blk = None
            for hh, q_s in enumerate((qa_s, qb_s)):
                c0 = HEAD_DIM * hh
                s = _swa_scores(q_s[r0:r0 + nr, :], kn_s, slope_ref[0:1, c0:c0 + 1], k0, r0, nr, klen)
                sink = sink_ref[0:1, c0:c0 + 1]
                m = jnp.maximum(jnp.max(s, axis=-1, keepdims=True), sink)
                p = jnp.exp(s - m)
                den = jnp.sum(p, axis=-1, keepdims=True) + jnp.exp(sink - m)
                oh = jnp.dot(p.astype(BF16), v_s[k0:k0 + klen, :], preferred_element_type=F32) * (1.0 / den)
                lh = jnp.broadcast_to(m + jnp.log(den), (nr, LANES))
                o_blk = oh if hh == 0 else jnp.where(half, o_blk, oh)
                lse_blk = lh if hh == 0 else jnp.where(half, lse_blk, lh)
            o_ref[r0:r0 + nr, :] = o_blk
            lse_ref[r0:r0 + nr, :] = lse_blk

    g2 = md.g // 2
    qcol = pl.BlockSpec((L, LANES), lambda j: (0, md.bq_s + j))
    kcol = pl.BlockSpec((L, LANES), lambda j: (0, md.bk_s + j // g2))
    vcol = pl.BlockSpec((L, LANES), lambda j: (0, md.bv_s + j // g2))
    ocol = pl.BlockSpec((L, LANES), lambda j: (0, j))
    pvec = pl.BlockSpec((1, LANES), lambda j: (0, j))
    vec = pl.BlockSpec((1, LANES), lambda j: (0, 0))
    return _call_carrying(
        body, comm, (u, u, u, sinkp, slopep, qg, kg, bd), name=name, grid=(md.ps,),
        in_specs=[qcol, kcol, vcol, pvec, pvec, vec, vec, pl.BlockSpec((LANES, LANES), lambda j: (0, 0))],
        out_specs=[ocol, ocol],
        out_shape=[jax.ShapeDtypeStruct((L, md.ws), F32)] * 2,
        scratch_shapes=[pltpu.VMEM((L, LANES), BF16)] * 4)


def _swa_bwd(u, sinkp, slopep, qg, kg, do, lse, md, *, name, comm=None):
    L = u.shape[0]
    blocks = _qblocks(L, QBLOCK // 2)
    bd = _block_diag_mean()
    fold = _fold_halves()
    g2 = md.g // 2
    npairs = md.ps

    def body(q_ref, k_ref, v_ref, sink_ref, slope_ref, qg_ref, kg_ref, bd_ref, fold_ref, do_ref, lse_ref,
             dq_ref, dk_ref, dv_ref, dsink_ref, dqg_ref, dkg_ref,
             qa_s, qb_s, kn_s, v_s, doa_s, dob_s, dqn_s):
        j = pl.program_id(0)
        half = _lane_half()
        bdv = bd_ref[...]
        qy, qh, rq = _head_norm(q_ref[...], qg_ref[...], bdv)
        ky, kh, rk = _head_norm(k_ref[...], kg_ref[...], bdv)
        _store_head_pair(qy, half, qa_s, qb_s)
        _store_head_pair(do_ref[...], half, doa_s, dob_s)
        kn_s[...] = ky.astype(BF16)
        v_s[...] = v_ref[...].astype(BF16)

        @pl.when(j % g2 == 0)
        def _():
            dk_ref[...] = jnp.zeros_like(dk_ref)
            dv_ref[...] = jnp.zeros_like(dv_ref)

        @pl.when(j == 0)
        def _():
            dqg_ref[...] = jnp.zeros_like(dqg_ref)
            dkg_ref[...] = jnp.zeros_like(dkg_ref)

        dsink = [jnp.zeros((1, 1), F32), jnp.zeros((1, 1), F32)]
        for r0, nr in blocks:
            k0 = max(r0 - BLOCK, 0)
            klen = r0 + nr - k0
            dq_blk = None
            for hh, (q_s, do_s) in enumerate(((qa_s, doa_s), (qb_s, dob_s))):
                c0 = HEAD_DIM * hh
                qm = q_s[r0:r0 + nr, :]
                dom = do_s[r0:r0 + nr, :]
                s = _swa_scores(qm, kn_s, slope_ref[0:1, c0:c0 + 1], k0, r0, nr, klen)
                lse_h = lse_ref[r0:r0 + nr, c0:c0 + 1]
                p = jnp.exp(s - lse_h)
                pdp = p * _dot_nt(dom, v_s[k0:k0 + klen, :])
                delta = jnp.sum(pdp, axis=-1, keepdims=True)
                p_sink = jnp.exp(sink_ref[0:1, c0:c0 + 1] - lse_h)
                dsink[hh] = dsink[hh] - jnp.sum(p_sink * delta, axis=0, keepdims=True)
                ds = (pdp - p * delta) * (HEAD_DIM ** -0.5)
                dq_h = jnp.dot(ds.astype(BF16), kn_s[k0:k0 + klen, :], preferred_element_type=F32)
                dk_ref[k0:k0 + klen, :] += _dot_tn(ds, qm)
                dv_ref[k0:k0 + klen, :] += _dot_tn(p, dom)
                dq_blk = dq_h if hh == 0 else jnp.where(half, dq_blk, dq_h)
            dqn_s[r0:r0 + nr, :] = dq_blk
        dq, dqg = _head_norm_bwd(dqn_s[...], qh, rq, qg_ref[...], bdv)
        dq_ref[...] = dq
        dqg_ref[...] += dqg
        dsink_ref[...] = jnp.where(half, jnp.broadcast_to(dsink[0], (1, LANES)), jnp.broadcast_to(dsink[1], (1, LANES)))

        @pl.when(j % g2 == g2 - 1)
        def _():
            dkn = _f32dot(dk_ref[...], fold_ref[...])
            dk, dkg = _head_norm_bwd(dkn, kh, rk, kg_ref[...], bdv)
            dk_ref[...] = jnp.where(half, dk, 0.0)
            dv_ref[...] = jnp.where(half, _f32dot(dv_ref[...], fold_ref[...]), 0.0)
            dkg_ref[...] += dkg

        @pl.when(j == npairs - 1)
        def _():
            dqg_ref[...] = _f32dot(jnp.broadcast_to(dqg_ref[...], (8, LANES)), fold_ref[...])[0:1, :]

    qcol = pl.BlockSpec((L, LANES), lambda j: (0, md.bq_s + j))
    kcol = pl.BlockSpec((L, LANES), lambda j: (0, md.bk_s + j // g2))
    vcol = pl.BlockSpec((L, LANES), lambda j: (0, md.bv_s + j // g2))
    ocol = pl.BlockSpec((L, LANES), lambda j: (0, j))
    kvout = pl.BlockSpec((L, LANES), lambda j: (0, j // g2))
    pvec = pl.BlockSpec((1, LANES), lambda j: (0, j))
    vec = pl.BlockSpec((1, LANES), lambda j: (0, 0))
    sq = pl.BlockSpec((LANES, LANES), lambda j: (0, 0))
    kvshape = jax.ShapeDtypeStruct((L, LANES * md.nkv), F32)
    small = jax.ShapeDtypeStruct((1, LANES), F32)
    return _call_carrying(
        body, comm, (u, u, u, sinkp, slopep, qg, kg, bd, fold, do, lse), name=name, grid=(md.ps,),
        in_specs=[qcol, kcol, vcol, pvec, pvec, vec, vec, sq, sq, ocol, ocol],
        out_specs=[ocol, kvout, kvout, pvec, vec, vec],
        out_shape=[jax.ShapeDtypeStruct((L, md.ws), F32), kvshape, kvshape,
                   jax.ShapeDtypeStruct((1, md.ws), F32), small, small],
        scratch_shapes=[pltpu.VMEM((L, LANES), BF16)] * 6 + [pltpu.VMEM((L, LANES), F32)])


def _outnorm_fwd(of, os_, gf, gs, *, name):
    L, wf = of.shape
    ws = os_.shape[1]
    tr = _pick(L, _TR)

    def body(of_ref, os_ref, gf_ref, gs_ref, o_ref):
        for src, g_ref, c0, w in ((of_ref, gf_ref, 0, wf), (os_ref, gs_ref, wf, ws)):
            x = src[...]
            r = lax.rsqrt(jnp.mean(x * x, axis=-1, keepdims=True) + EPS)
            o_ref[:, c0:c0 + w] = (x * r * g_ref[...]).astype(o_ref.dtype)

    return pl.pallas_call(
        body, name=name, grid=(L // tr,),
        in_specs=[pl.BlockSpec((tr, wf), lambda i: (i, 0)), pl.BlockSpec((tr, ws), lambda i: (i, 0)),
                  pl.BlockSpec((1, wf), lambda i: (0, 0)), pl.BlockSpec((1, ws), lambda i: (0, 0))],
        out_specs=pl.BlockSpec((tr, wf + ws), lambda i: (i, 0)),
        out_shape=jax.ShapeDtypeStruct((L, wf + ws), BF16),
        compiler_params=_cparams("parallel"),
    )(of, os_, gf, gs)


def _outnorm_bwd(don, of, os_, gf, gs, *, name):
    L, wf = of.shape
    ws = os_.shape[1]
    tr = _pick(L, _TR)

    def body(d_ref, of_ref, os_ref, gf_ref, gs_ref, dof_ref, dos_ref, dgf_ref, dgs_ref):
        i = pl.program_id(0)
        for src, g_ref, c0, w, dx_ref, dg_ref in ((of_ref, gf_ref, 0, wf, dof_ref, dgf_ref),
                                                  (os_ref, gs_ref, wf, ws, dos_ref, dgs_ref)):
            x = src[...]
            dy = d_ref[:, c0:c0 + w]
            r = lax.rsqrt(jnp.mean(x * x, axis=-1, keepdims=True) + EPS)
            xh = x * r
            dxh = dy * g_ref[...]
            dx_ref[...] = r * (dxh - xh * jnp.mean(dxh * xh, axis=-1, keepdims=True))
            part = jnp.sum(dy * xh, axis=0, keepdims=True)

            @pl.when(i == 0)
            def _():
                dg_ref[...] = part

            @pl.when(i > 0)
            def _():
                dg_ref[...] += part

    rf = pl.BlockSpec((tr, wf), lambda i: (i, 0))
    rs = pl.BlockSpec((tr, ws), lambda i: (i, 0))
    vf = pl.BlockSpec((1, wf), lambda i: (0, 0))
    vs = pl.BlockSpec((1, ws), lambda i: (0, 0))
    return pl.pallas_call(
        body, name=name, grid=(L // tr,),
        in_specs=[pl.BlockSpec((tr, wf + ws), lambda i: (i, 0)), rf, rs, vf, vs],
        out_specs=[rf, rs, vf, vs],
        out_shape=[jax.ShapeDtypeStruct((L, wf), F32), jax.ShapeDtypeStruct((L, ws), F32),
                   jax.ShapeDtypeStruct((1, wf), F32), jax.ShapeDtypeStruct((1, ws), F32)],
        compiler_params=_cparams("arbitrary"),
    )(don, of, os_, gf, gs)


def _win_to_mine(w, md):
    d = w.shape[0]
    wf, ws, hd = md.wf, md.ws, HEAD_DIM
    o_z = 3 * wf
    o_sq = o_z + md.hf
    o_sk = o_sq + ws
    o_sv = o_sk + md.nkv * hd
    parts = [w[:, :3 * wf], w[:, o_sq:o_sq + ws]]
    for base in (o_sk, o_sv):
        for kv in range(md.nkv):
            blk = w[:, base + kv * hd:base + (kv + 1) * hd]
            parts += [blk, blk]
    z = w[:, o_z:o_z + md.hf].reshape(d, md.hf // 2, 2)
    z = jnp.pad(z, ((0, 0), (0, 0), (0, 6))).reshape(d, 4 * md.hf)
    parts.append(jnp.pad(z, ((0, 0), (0, LANES - 4 * md.hf + md.nup - md.nu))))
    return jnp.concatenate(parts, axis=1)


def _win_grad_to_ref(dw, md):
    d = dw.shape[0]
    wf, ws, hd = md.wf, md.ws, HEAD_DIM
    z = dw[:, md.bz * LANES:md.bz * LANES + 4 * md.hf].reshape(d, md.hf // 2, 8)[:, :, :2].reshape(d, md.hf)
    parts = [dw[:, :3 * wf], z, dw[:, md.bq_s * LANES:md.bq_s * LANES + ws]]
    for base in (md.bk_s, md.bv_s):
        for kv in range(md.nkv):
            c0 = (base + kv) * LANES
            parts.append(dw[:, c0:c0 + hd])
    return jnp.concatenate(parts, axis=1)


def _mix_small(p, md):
    tile2 = lambda v: jnp.tile(v.reshape(1, HEAD_DIM), (1, 2))
    b = p["b_forget"].reshape(md.hf // 2, 2)
    b = jnp.pad(b, ((0, 0), (0, 6))).reshape(1, 4 * md.hf)
    slopes = np.asarray(2.0 ** (-8.0 * np.arange(1, md.hq + 1) / md.hq), np.float32)
    return dict(
        g_mix=p["mix_norm"].reshape(1, -1),
        b_gate=jnp.pad(b, ((0, 0), (0, LANES - 4 * md.hf))),
        fqg=tile2(p["fox_q_norm"]), fkg=tile2(p["fox_k_norm"]),
        sqg=tile2(p["swa_q_norm"]), skg=tile2(p["swa_k_norm"]),
        sinkp=jnp.repeat(p["swa_sinks"], HEAD_DIM).reshape(1, md.ws),
        slopep=jnp.asarray(np.repeat(slopes, HEAD_DIM).reshape(1, md.ws)),
        gfo=p["fox_out_norm"].reshape(1, md.wf), gso=p["swa_out_norm"].reshape(1, md.ws),
    )


def _mix_fwd(h, sp, w, md, queue=None):
    L = h.shape[0]
    hn = _rms_fwd(h, sp["g_mix"], name="mix_rms")
    u = _mm(queue, hn, w(0), name="mix_u")
    cexp, ct = _gate_fwd(u, sp["b_gate"], md, name="gate_fwd")
    ct3 = ct[:8 * md.pf].reshape(md.pf, 8, L)
    of, lsef = _carried(queue, _fox_fwd, 2, u, cexp, ct3, sp["fqg"], sp["fkg"], md, name="fox_fwd")
    os_, lses = _carried(queue, _swa_fwd, 2, u, sp["sinkp"], sp["slopep"], sp["sqg"], sp["skg"], md, name="swa_fwd")
    on = _outnorm_fwd(of, os_, sp["gfo"], sp["gso"], name="outnorm_fwd")
    h_out = _matmul(on, w(1), name="mix_out", residual=h)
    return h_out, (h, hn, u, cexp, ct3, of, lsef, os_, lses, on)


def _mix_bwd(dh, saved, sp, w_in, w_out, md, queue=None):
    h, hn, u, cexp, ct3, of, lsef, os_, lses, on = saved
    L = h.shape[0]
    dhb = _cast_bf16(dh, name="mix_dhb")
    don = _matmul(dhb, w_out, name="mix_don", nt=True)
    dw_out = _matmul(_transpose(on, name="mix_onT"), dhb, name="mix_dwout", out_dtype=BF16)
    dof, dos, dgfo, dgso = _outnorm_bwd(don, of, os_, sp["gfo"], sp["gso"], name="outnorm_bwd")
    duq, duk, duv, dck, dfqg, dfkg = _carried(queue, _fox_bwd, 6, u, cexp, ct3, sp["fqg"], sp["fkg"], dof, lsef, md,
                                              name="fox_bwd")
    dsq, dsk, dsv, dsinkp, dsqg, dskg = _carried(queue, _swa_bwd, 6, u, sp["sinkp"], sp["slopep"], sp["sqg"], sp["skg"], dos,
                                                 lses, md, name="swa_bwd")
    dck_t = jnp.pad(dck.reshape(8 * md.pf, L), ((0, LANES - 8 * md.pf), (0, 0)))
    dz, db = _gate_bwd(u, sp["b_gate"], dck_t, md, name="gate_bwd")
    du = jnp.concatenate([duq, duk, duv, dsq, dsk, dsv, dz, jnp.zeros((L, md.nup - md.nu), F32)], axis=1).astype(BF16)
    dhn = _mm(queue, du, w_in, name="mix_dhn", nt=True)
    dw_in = _mm(queue, _transpose(hn, name="mix_hnT"), du, name="mix_dwin", out_dtype=BF16)
    dh_in, dg_mix = _rms_bwd(dhn, h, sp["g_mix"], dh, name="mix_drms")
    small = dict(
        mix_norm=dg_mix.reshape(-1),
        b_forget=db[0, :4 * md.hf].reshape(md.hf // 2, 8)[:, :2].reshape(md.hf),
        fox_q_norm=dfqg[0, :HEAD_DIM], fox_k_norm=dfkg[0, :HEAD_DIM],
        swa_q_norm=dsqg[0, :HEAD_DIM], swa_k_norm=dskg[0, :HEAD_DIM],
        swa_sinks=dsinkp[0, ::HEAD_DIM],
        fox_out_norm=dgfo.reshape(-1), swa_out_norm=dgso.reshape(-1),
    )
    return dh_in, dw_in, dw_out, small


_ANY = pl.BlockSpec(memory_space=pl.ANY)
_HALF_ROWS = (512, 352, 256, 192, 128, 64, 32, 16)


def _mesh_pos():
    return lax.axis_index("x"), lax.axis_index("y"), lax.axis_index("c")


def _other_chips(x, y):
    return [(1 - x, y), (x, 1 - y), (1 - x, 1 - y)]


def _rows_half(ref, which):
    rh = ref.shape[-2] // 2
    if len(ref.shape) == 2:
        return ref.at[pl.ds(which * rh, rh), :]
    return ref.at[:, pl.ds(which * rh, rh), :]


def _remote(src, dst, send_sems, recv_sems, idx, dev):
    return pltpu.make_async_remote_copy(src_ref=src, dst_ref=dst, send_sem=send_sems.at[idx], recv_sem=recv_sems.at[idx],
                                        device_id=dev, device_id_type=MESH)


def _cast_into_chunk(w, l_idx, k_idx, *, name):
    _, rows, cols = w.shape
    tr = _pick(rows, (512, 352, 256, 128, 64, 32, 16))

    def body(l_ref, k_ref, w_ref, o_ref):
        o_ref[...] = w_ref[...].astype(o_ref.dtype)

    return pl.pallas_call(
        body, name=name,
        grid_spec=pltpu.PrefetchScalarGridSpec(
            num_scalar_prefetch=2, grid=(rows // tr,),
            in_specs=[pl.BlockSpec((None, tr, cols), lambda i, l, k: (l[0], i, 0))],
            out_specs=pl.BlockSpec((None, tr, cols), lambda i, l, k: (k[0], i, 0))),
        out_shape=jax.ShapeDtypeStruct((NCH, rows, cols), BF16),
        compiler_params=_cparams("parallel"),
    )(l_idx, k_idx, w)


class _SiblingSwap:
    n_sems = 1

    def __init__(self, g):
        self.operands = [g]
        self.out_shapes = [jax.ShapeDtypeStruct((NCH, g.shape[1] // 2, g.shape[2]), g.dtype)]
        self.aliases = {}

    def copies(self, cin, cout, send_sems, recv_sems, base=0):
        x, y, c = _mesh_pos()
        cp = _remote(_rows_half(cin[0], 1 - c), cout[0], send_sems, recv_sems, base, (x, y, 1 - c))
        return [(cp, cp)]


class _ChipExchange:
    n_sems = NCH - 1

    def __init__(self, s, r0, r1, land=None):
        self.r0, self.r1 = r0, r1
        self.operands = [s] if land is None else [s, land]
        self.out_shapes = [jax.ShapeDtypeStruct((NCH - 1,) + s.shape[1:], s.dtype)]
        self.aliases = {} if land is None else {1: 0}

    def copies(self, cin, cout, send_sems, recv_sems, base=0):
        x, y, c = _mesh_pos()
        rows = pl.ds(self.r0, self.r1 - self.r0)
        cps = [_remote(cin[0].at[2 * cx + cy, rows, :], cout[0].at[j, rows, :], send_sems, recv_sems, base + j, (cx, cy, c))
               for j, (cx, cy) in enumerate(_other_chips(x, y))]
        return [(cp, cp) for cp in cps]


class _SiblingShare:
    n_sems = 1

    def __init__(self, tot):
        self.operands = [tot]
        self.out_shapes = [jax.ShapeDtypeStruct(tot.shape, tot.dtype)]
        self.aliases = {0: 0}

    def copies(self, cin, cout, send_sems, recv_sems, base=0):
        x, y, c = _mesh_pos()
        mine, theirs = _rows_half(cout[0], c), _rows_half(cout[0], 1 - c)
        return [(_remote(mine, mine, send_sems, recv_sems, base, (x, y, 1 - c)),
                 _remote(theirs, theirs, send_sems, recv_sems, base, (x, y, 1 - c)))]


class _Both:
    def __init__(self, a, b):
        self.a, self.b = a, b
        self.operands = a.operands + b.operands
        self.out_shapes = a.out_shapes + b.out_shapes
        self.aliases = dict(a.aliases)
        self.aliases.update({len(a.operands) + s: len(a.out_shapes) + d for s, d in b.aliases.items()})
        self.n_sems = a.n_sems + b.n_sems

    def copies(self, cin, cout, send_sems, recv_sems, base=0):
        na, nao = len(self.a.operands), len(self.a.out_shapes)
        return (self.a.copies(cin[:na], cout[:nao], send_sems, recv_sems, base)
                + self.b.copies(cin[na:], cout[nao:], send_sems, recv_sems, base + self.a.n_sems))


def _rows_quarter(ref, c, q):
    rq = ref.shape[0] // 4
    return ref.at[pl.ds((2 * c + q) * rq, rq), :]


class _GatherStep:
    n_sems = 7

    def __init__(self, stage1, stage2, stage3):
        given = [(st, b) for st, b in ((1, stage1), (2, stage2), (3, stage3)) if b is not None]
        self.stages = [st for st, _ in given]
        self.operands = [b for _, b in given]
        self.out_shapes = [jax.ShapeDtypeStruct(b.shape, b.dtype) for b in self.operands]
        self.aliases = {i: i for i in range(len(self.operands))}

    def copies(self, cin, cout, send_sems, recv_sems, base=0):
        x, y, c = _mesh_pos()
        k, kx, ky, kd = 2 * x + y, 2 * (1 - x) + y, 2 * x + (1 - y), 2 * (1 - x) + (1 - y)
        xn, yn, sibling = (1 - x, y, c), (x, 1 - y, c), (x, y, 1 - c)

        def pair(idx, dev, src, arrival):
            return (_remote(src, src, send_sems, recv_sems, base + idx, dev),
                    _remote(arrival, arrival, send_sems, recv_sems, base + idx, dev))

        out = []
        for stage, buf in zip(self.stages, cout):
            if stage == 1:
                mine = _rows_half(buf.at[k], c)
                out.append(pair(0, xn, mine, _rows_half(buf.at[kx], c)))
                out.append(pair(1, yn, mine, _rows_half(buf.at[ky], c)))
            elif stage == 2:
                out.append(pair(2, yn, _rows_quarter(buf.at[kx], c, 0), _rows_quarter(buf.at[kd], c, 0)))
                out.append(pair(3, xn, _rows_quarter(buf.at[ky], c, 1), _rows_quarter(buf.at[kd], c, 1)))
                out.append(pair(4, sibling, _rows_half(buf.at[kx], c), _rows_half(buf.at[kx], 1 - c)))
                out.append(pair(5, sibling, _rows_half(buf.at[ky], c), _rows_half(buf.at[ky], 1 - c)))
            else:
                out.append(pair(6, sibling, _rows_half(buf.at[kd], c), _rows_half(buf.at[kd], 1 - c)))
        return out


class _GatherQueue:
    def __init__(self, bufs):
        self.bufs = list(bufs)
        self.step = -3
        for _ in range(3):
            self.give(_run_exchange(self.take(), name="first_allgather"))

    def take(self):
        s = self.step
        self.step += 1
        at = [i if 0 <= i < len(self.bufs) else None for i in (s + 3, s + 2, s + 1)]
        self.cur = [i for i in at if i is not None]
        if not self.cur:
            return None
        return _GatherStep(*[None if i is None else self.bufs[i] for i in at])

    def give(self, outs):
        for i, buf in zip(self.cur, outs):
            self.bufs[i] = buf


def _run_exchange(job, *, name):
    n_in, n_out = len(job.operands), len(job.out_shapes)

    def body(*refs):
        cps = job.copies(refs[:n_in], refs[n_in:n_in + n_out], refs[-2], refs[-1])
        for cp, _ in cps:
            cp.start()
        for _, arrival in cps:
            arrival.wait_recv()
        for cp, _ in cps:
            cp.wait_send()

    return pl.pallas_call(
        body, name=name, in_specs=[_ANY] * n_in, out_specs=[_ANY] * n_out, out_shape=list(job.out_shapes),
        input_output_aliases=dict(job.aliases),
        scratch_shapes=[pltpu.SemaphoreType.DMA((job.n_sems,)), pltpu.SemaphoreType.DMA((job.n_sems,))],
    )(*job.operands)


class _ExchangeQueue:
    def __init__(self):
        self.lines = ([], [])
        self.cur = []

    def put(self, line, make_job, done):
        self.lines[line].append((make_job, done))

    def take(self):
        entries = [ln.pop(0) for ln in self.lines if ln]
        if not entries:
            return None
        jobs = [make() for make, _ in entries]
        self.cur = [(job, done) for job, (_, done) in zip(jobs, entries)]
        return jobs[0] if len(jobs) == 1 else _Both(*jobs)

    def give(self, outs):
        outs = list(outs)
        for job, done in self.cur:
            n = len(job.out_shapes)
            done(outs[:n])
            outs = outs[n:]

    def drain(self, name):
        while (job := self.take()) is not None:
            self.give(_run_exchange(job, name=name))


ICI_LINE, D2D_LINE = 0, 1


def _reduce_scatter_later(queue, parts, c_idx, k_idx, tag, done, n_parts=2):
    n = len(parts)
    st = dict(lands=[None] * n, sums=None, chip_lands=[None] * n, tots=[None] * n, left=n)

    def shared(i, outs):
        st['tots'][i] = outs[0]
        st['left'] -= 1
        if st['left'] == 0:
            done(st['tots'])

    def exchanged(i, outs):
        st['chip_lands'][i] = outs[0]
        st['left'] -= 1
        if st['left'] == 0:
            st['left'] = n
            for j in range(n):
                tot = _add_chunks(st['sums'][j], st['chip_lands'][j], k_idx, c_idx, name=f"{tag}_rs_add4")
                queue.put(D2D_LINE, lambda tot=tot: _SiblingShare(tot), lambda outs, j=j: shared(j, outs))

    def swapped(i, outs):
        st['lands'][i] = outs[0]
        st['left'] -= 1
        if st['left'] == 0:
            st['sums'] = [_add_own_half(p, l, c_idx, name=f"{tag}_rs_add2") for p, l in zip(parts, st['lands'])]
            st['left'] = n * n_parts
            for j in range(n):
                step = st['sums'][j].shape[1] // n_parts
                for p in range(n_parts):
                    queue.put(ICI_LINE,
                              lambda j=j, p=p, step=step: _ChipExchange(st['sums'][j], p * step, (p + 1) * step, st['chip_lands'][j]),
                              lambda outs, j=j: exchanged(j, outs))

    for i in range(n):
        queue.put(D2D_LINE, lambda i=i: _SiblingSwap(parts[i]), lambda outs, i=i: swapped(i, outs))


def _mm(queue, a, b, **kw):
    job = queue.take() if queue is not None else None
    if job is None:
        return _matmul(a, b, **kw)
    n_out = len(kw.get("out_dtypes") or (0,))
    outs = _matmul(a, b, comm=job, **kw)
    queue.give(outs[n_out:])
    return outs[0] if n_out == 1 else outs[:n_out]


def _allgather_devices(v, *, name):
    m = v.shape[0]

    def body(v_ref, out_ref, send_sems, recv_sems, local_sem):
        x, y, c = _mesh_pos()
        mine = 4 * x + 2 * y + c
        own = pltpu.make_async_copy(v_ref, out_ref.at[mine], local_sem)
        own.start()
        cps = []
        for r in range(1, 8):
            px, py, pc = (x + (r >> 2)) % 2, (y + ((r >> 1) & 1)) % 2, (c + (r & 1)) % 2
            cps.append((_remote(v_ref, out_ref.at[mine], send_sems, recv_sems, r - 1, (px, py, pc)), 4 * px + 2 * py + pc))
        for cp, _ in cps:
            cp.start()
        for r, (cp, theirs) in enumerate(cps):
            blk = out_ref.at[theirs]
            _remote(blk, blk, send_sems, recv_sems, r, (x, y, c)).wait_recv()
        for cp, _ in cps:
            cp.wait_send()
        own.wait()

    return pl.pallas_call(
        body, name=name, in_specs=[_ANY], out_specs=_ANY,
        out_shape=jax.ShapeDtypeStruct((8, m, LANES), v.dtype),
        scratch_shapes=[pltpu.SemaphoreType.DMA((7,)), pltpu.SemaphoreType.DMA((7,)), pltpu.SemaphoreType.DMA],
    )(v)


def _add_own_half(g, land, c_idx, *, name):
    nch, rh, cols = land.shape
    tr = _pick(rh, _HALF_ROWS)
    nt = rh // tr

    def body(c_ref, g_ref, l_ref, o_ref):
        o_ref[...] = (g_ref[...].astype(F32) + l_ref[...].astype(F32)).astype(o_ref.dtype)

    return pl.pallas_call(
        body, name=name,
        grid_spec=pltpu.PrefetchScalarGridSpec(
            num_scalar_prefetch=1, grid=(nch, nt),
            in_specs=[pl.BlockSpec((None, tr, cols), lambda k, i, c: (k, c[0] * nt + i, 0)),
                      pl.BlockSpec((None, tr, cols), lambda k, i, c: (k, i, 0))],
            out_specs=pl.BlockSpec((None, tr, cols), lambda k, i, c: (k, i, 0))),
        out_shape=jax.ShapeDtypeStruct(land.shape, BF16),
        compiler_params=_cparams("parallel", "parallel"),
    )(c_idx, g, land)


def _add_chunks(s, land, k_idx, c_idx, *, name):
    _, rh, cols = s.shape
    tr = _pick(rh, _HALF_ROWS)
    nt = rh // tr

    def body(k_ref, c_ref, s_ref, l_ref, o_ref):
        t = s_ref[...].astype(F32)
        for j in range(NCH - 1):
            t = t + l_ref[j].astype(F32)
        o_ref[...] = t

    return pl.pallas_call(
        body, name=name,
        grid_spec=pltpu.PrefetchScalarGridSpec(
            num_scalar_prefetch=2, grid=(nt,),
            in_specs=[pl.BlockSpec((None, tr, cols), lambda i, k, c: (k[0], i, 0)),
                      pl.BlockSpec((NCH - 1, tr, cols), lambda i, k, c: (0, i, 0))],
            out_specs=pl.BlockSpec((tr, cols), lambda i, k, c: (c[0] * nt + i, 0))),
        out_shape=jax.ShapeDtypeStruct((2 * rh, cols), F32),
        compiler_params=_cparams("parallel"),
    )(k_idx, c_idx, s, land)


def _sum_devices(v, *, name):
    _, m, _ = v.shape

    def body(v_ref, o_ref):
        t = v_ref[0]
        for d in range(1, 8):
            t = t + v_ref[d]
        o_ref[...] = t

    return pl.pallas_call(
        body, name=name, grid=(1,),
        in_specs=[pl.BlockSpec((8, m, LANES), lambda i: (0, 0, 0))],
        out_specs=pl.BlockSpec((m, LANES), lambda i: (0, 0)),
        out_shape=jax.ShapeDtypeStruct((m, LANES), F32),
        compiler_params=_cparams("arbitrary"),
    )(v)


def _adamw_math(w, g, m, v):
    m = ADAM_B1 * m + (1.0 - ADAM_B1) * g
    v = ADAM_B2 * v + (1.0 - ADAM_B2) * (g * g)
    m_hat = m / (1.0 - ADAM_B1 ** ADAM_STEP)
    v_hat = v / (1.0 - ADAM_B2 ** ADAM_STEP)
    delta = -ADAM_LR * (m_hat / (jnp.sqrt(v_hat) + ADAM_EPS) + ADAM_WD * w)
    return delta, m, v


def _adamw_layer(w, m, v, g, layer, prev, *, name):
    depth, rows, cols = w.shape
    tr = _pick(rows, (256, 128, 64, 32, 16, 8))
    lay = pl.BlockSpec((None, tr, cols), lambda i, l: (l[0], i, 0))
    n_prev = 0 if prev is None else 4

    def body(l_ref, w_ref, m_ref, v_ref, g_ref, *rest):
        go_ref, d_ref, mo_ref, vo_ref = rest[n_prev:]
        g = g_ref[...]
        delta, m_new, v_new = _adamw_math(w_ref[...], g, m_ref[...], v_ref[...])
        go_ref[...] = g
        d_ref[...] = delta
        mo_ref[...] = m_new
        vo_ref[...] = v_new

    stack = jax.ShapeDtypeStruct(w.shape, F32)
    return pl.pallas_call(
        body, name=name,
        grid_spec=pltpu.PrefetchScalarGridSpec(
            num_scalar_prefetch=1, grid=(rows // tr,),
            in_specs=[lay, lay, lay, pl.BlockSpec((tr, cols), lambda i, l: (i, 0))] + [_ANY] * n_prev,
            out_specs=[lay] * 4),
        out_shape=[stack] * 4,
        input_output_aliases={} if prev is None else {5 + q: q for q in range(4)},
        compiler_params=_cparams("parallel"),
    )(layer, w, m, v, g, *(() if prev is None else prev))


def _adamw_flat(w, g, m, v, *, name):
    def body(w_ref, g_ref, m_ref, v_ref, d_ref, mo_ref, vo_ref):
        d_ref[...], mo_ref[...], vo_ref[...] = _adamw_math(w_ref[...], g_ref[...], m_ref[...], v_ref[...])

    blk = pl.BlockSpec(w.shape, lambda i: (0, 0))
    return pl.pallas_call(
        body, name=name, grid=(1,), in_specs=[blk] * 4, out_specs=[blk] * 3,
        out_shape=[jax.ShapeDtypeStruct(w.shape, F32)] * 3, compiler_params=_cparams("arbitrary"),
    )(w, g, m, v)


_WEIGHTS = ('meta_tokens', 'ffn1_norm', 'ffn1_w_gate', 'ffn1_w_up', 'ffn1_w_down', 'mix_norm', 'w_in', 'b_forget',
            'fox_q_norm', 'fox_k_norm', 'swa_q_norm', 'swa_k_norm', 'swa_sinks', 'fox_out_norm', 'swa_out_norm', 'w_out',
            'ffn2_norm', 'ffn2_w_gate', 'ffn2_w_up', 'ffn2_w_down')
_BIG = ('ffn1_w_gate', 'ffn1_w_up', 'ffn1_w_down', 'w_in', 'w_out', 'ffn2_w_gate', 'ffn2_w_up', 'ffn2_w_down')
_SMALL = tuple(n for n in _WEIGHTS if n not in _BIG and n != 'meta_tokens')
_MIX_SMALL = ('mix_norm', 'b_forget', 'fox_q_norm', 'fox_k_norm', 'swa_q_norm', 'swa_k_norm', 'swa_sinks',
              'fox_out_norm', 'swa_out_norm')


def _pack_rows(vectors):
    flat = jnp.concatenate([v.reshape(-1) for v in vectors])
    n = flat.shape[0]
    m = -(-n // (8 * LANES)) * 8
    return jnp.pad(flat, (0, m * LANES - n)).reshape(m, LANES)


def _unpack_rows(packed, shapes):
    flat = packed.reshape(-1)
    out, o = [], 0
    for s in shapes:
        n = int(np.prod(s))
        out.append(flat[o:o + n].reshape(s))
        o += n
    return out


def kernel(x, meta_tokens, ffn1_norm, ffn1_w_gate, ffn1_w_up, ffn1_w_down, mix_norm, w_in, b_forget, fox_q_norm, fox_k_norm, swa_q_norm, swa_k_norm, swa_sinks, fox_out_norm, swa_out_norm, w_out, ffn2_norm, ffn2_w_gate, ffn2_w_up, ffn2_w_down, loss_target, m_meta_tokens, m_ffn1_norm, m_ffn1_w_gate, m_ffn1_w_up, m_ffn1_w_down, m_mix_norm, m_w_in, m_b_forget, m_fox_q_norm, m_fox_k_norm, m_swa_q_norm, m_swa_k_norm, m_swa_sinks, m_fox_out_norm, m_swa_out_norm, m_w_out, m_ffn2_norm, m_ffn2_w_gate, m_ffn2_w_up, m_ffn2_w_down, v_meta_tokens, v_ffn1_norm, v_ffn1_w_gate, v_ffn1_w_up, v_ffn1_w_down, v_mix_norm, v_w_in, v_b_forget, v_fox_q_norm, v_fox_k_norm, v_swa_q_norm, v_swa_k_norm, v_swa_sinks, v_fox_out_norm, v_swa_out_norm, v_w_out, v_ffn2_norm, v_ffn2_w_gate, v_ffn2_w_up, v_ffn2_w_down):
    W = dict(meta_tokens=meta_tokens, ffn1_norm=ffn1_norm, ffn1_w_gate=ffn1_w_gate, ffn1_w_up=ffn1_w_up, ffn1_w_down=ffn1_w_down, mix_norm=mix_norm, w_in=w_in, b_forget=b_forget, fox_q_norm=fox_q_norm, fox_k_norm=fox_k_norm, swa_q_norm=swa_q_norm, swa_k_norm=swa_k_norm, swa_sinks=swa_sinks, fox_out_norm=fox_out_norm, swa_out_norm=swa_out_norm, w_out=w_out, ffn2_norm=ffn2_norm, ffn2_w_gate=ffn2_w_gate, ffn2_w_up=ffn2_w_up, ffn2_w_down=ffn2_w_down)
    Mo = dict(meta_tokens=m_meta_tokens, ffn1_norm=m_ffn1_norm, ffn1_w_gate=m_ffn1_w_gate, ffn1_w_up=m_ffn1_w_up, ffn1_w_down=m_ffn1_w_down, mix_norm=m_mix_norm, w_in=m_w_in, b_forget=m_b_forget, fox_q_norm=m_fox_q_norm, fox_k_norm=m_fox_k_norm, swa_q_norm=m_swa_q_norm, swa_k_norm=m_swa_k_norm, swa_sinks=m_swa_sinks, fox_out_norm=m_fox_out_norm, swa_out_norm=m_swa_out_norm, w_out=m_w_out, ffn2_norm=m_ffn2_norm, ffn2_w_gate=m_ffn2_w_gate, ffn2_w_up=m_ffn2_w_up, ffn2_w_down=m_ffn2_w_down)
    Vo = dict(meta_tokens=v_meta_tokens, ffn1_norm=v_ffn1_norm, ffn1_w_gate=v_ffn1_w_gate, ffn1_w_up=v_ffn1_w_up, ffn1_w_down=v_ffn1_w_down, mix_norm=v_mix_norm, w_in=v_w_in, b_forget=v_b_forget, fox_q_norm=v_fox_q_norm, fox_k_norm=v_fox_k_norm, swa_q_norm=v_swa_q_norm, swa_k_norm=v_swa_k_norm, swa_sinks=v_swa_sinks, fox_out_norm=v_fox_out_norm, swa_out_norm=v_swa_out_norm, w_out=v_w_out, ffn2_norm=v_ffn2_norm, ffn2_w_gate=v_ffn2_w_gate, ffn2_w_up=v_ffn2_w_up, ffn2_w_down=v_ffn2_w_down)

    _, S, D = x.shape
    L = S + BLOCK
    depth = ffn1_norm.shape[0]
    md = _MixDims(D)
    mx, my, mc = _mesh_pos()
    k_idx = (2 * mx + my).astype(jnp.int32).reshape(1)
    c_idx = mc.astype(jnp.int32).reshape(1)
    dcols = D // NCH

    meta_all = _allgather_devices(meta_tokens.reshape(-1, LANES), name="meta_allgather")
    meta_full = jnp.transpose(meta_all[0::2].reshape(NCH, N_META, dcols), (1, 0, 2)).reshape(N_META, D)

    order = ('ffn1_w_gate', 'ffn1_w_up', 'ffn1_w_down', 'w_in', 'w_out', 'ffn2_w_gate', 'ffn2_w_up', 'ffn2_w_down')
    chunks = []
    for l in range(depth):
        l_idx = jnp.full((1,), l, jnp.int32)
        chunks += [_cast_into_chunk(W[name], l_idx, k_idx, name="cast_chunk") for name in order]
    gather = _GatherQueue(chunks)
    wts = [{} for _ in range(depth)]

    def weight(l, i):
        if i not in wts[l]:
            buf = gather.bufs[len(order) * l + i]
            if order[i] == 'w_in':
                buf = _win_to_mine(jnp.transpose(buf, (1, 0, 2)).reshape(D, NCH * buf.shape[2]), md)
            elif order[i] in ('w_out', 'ffn1_w_down', 'ffn2_w_down'):
                buf = buf.reshape(-1, D)
            wts[l][i] = buf
        return wts[l][i]

    h = jnp.concatenate([jnp.zeros((PAD, D), F32), meta_full, x[0]], axis=0)
    saved = []
    for l in range(depth):
        sp = _mix_small({n: W[n][l] for n in _MIX_SMALL}, md)
        h, s1 = _ffn_fwd(h, ffn1_norm[l].reshape(1, D), lambda i: weight(l, i), "ffn", gather)
        h, s2 = _mix_fwd(h, sp, lambda i: weight(l, 3 + i), md, gather)
        h, s3 = _ffn_fwd(h, ffn2_norm[l].reshape(1, D), lambda i: weight(l, 5 + i), "ffn", gather)
        saved.append((s1, s2, s3, sp))
    wts = [dict(g1=w[0], u1=w[1], d1=w[2], wi=w[3], wo=w[4], g2=w[5], u2=w[6], d2=w[7]) for w in wts]

    loss_part, dh = _loss_grad(h, loss_target[0], name="loss_grad")

    small_grads = {n: [None] * depth for n in _SMALL}
    stacks = {n: None for n in _BIG}

    def update(name, l, grad):
        stacks[name] = _adamw_layer(W[name], Mo[name], Vo[name], grad, jnp.full((1,), l, jnp.int32), stacks[name],
                                    name="adamw_layer")

    queue = _ExchangeQueue()

    def reduce_later(names, l, parts, tag):
        def done(grads):
            for n, grad in zip(names, grads):
                update(n, l, grad)
        _reduce_scatter_later(queue, parts, c_idx, k_idx, tag, done)

    def ffn_dw(names, l):
        return lambda i, dw: reduce_later((names[i],), l, [dw.reshape(NCH, -1, D) if i == 2 else dw], "ffn")

    for l in range(depth - 1, -1, -1):
        wl = wts[l]
        s1, s2, s3, sp = saved[l]
        dh, dg = _ffn_bwd(dh, s3, ffn2_norm[l].reshape(1, D), wl['g2'], wl['u2'], wl['d2'], "ffn", queue,
                          ffn_dw(('ffn2_w_gate', 'ffn2_w_up', 'ffn2_w_down'), l))
        small_grads['ffn2_norm'][l] = dg.reshape(-1)

        dh, dwi, dwo, sm = _mix_bwd(dh, s2, sp, wl['wi'], wl['wo'], md, queue)
        for n in _MIX_SMALL:
            small_grads[n][l] = sm[n]
        dwi = _win_grad_to_ref(dwi, md)
        dwi = jnp.transpose(dwi.reshape(D, NCH, -1), (1, 0, 2))
        reduce_later(('w_in', 'w_out'), l, [dwi, dwo.reshape(NCH, -1, D)], "mix")

        dh, dg = _ffn_bwd(dh, s1, ffn1_norm[l].reshape(1, D), wl['g1'], wl['u1'], wl['d1'], "ffn", queue,
                          ffn_dw(('ffn1_w_gate', 'ffn1_w_up', 'ffn1_w_down'), l))
        small_grads['ffn1_norm'][l] = dg.reshape(-1)
    queue.drain("rs_chips")

    grad_x = dh[BLOCK:][None]

    small_shapes = [W[n].shape for n in _SMALL]
    parts = [jnp.stack(small_grads[n]) for n in _SMALL] + [dh[PAD:BLOCK], loss_part[0, :1]]
    packed = _pack_rows(parts)
    total = _sum_devices(_allgather_devices(packed, name="small_allgather"), name="small_sum")
    *g_small, g_meta, loss = _unpack_rows(total, small_shapes + [(N_META, D), (1,)])
    g_meta = lax.dynamic_slice(g_meta, (0, k_idx[0] * dcols), (N_META, dcols))

    sw = _pack_rows([W[n] for n in _SMALL])
    sd, smm, svv = _adamw_flat(sw, _pack_rows(g_small), _pack_rows([Mo[n] for n in _SMALL]),
                               _pack_rows([Vo[n] for n in _SMALL]), name="adamw_small")
    d_small, m_small, v_small = (_unpack_rows(t, small_shapes) for t in (sd, smm, svv))
    d_meta, m_meta, v_meta = _adamw_flat(meta_tokens, g_meta, m_meta_tokens, v_meta_tokens, name="adamw_meta")

    grads, deltas, new_m, new_v = {}, {}, {}, {}
    for n in _BIG:
        grads[n], deltas[n], new_m[n], new_v[n] = stacks[n]
    for i, n in enumerate(_SMALL):
        grads[n], deltas[n], new_m[n], new_v[n] = g_small[i], d_small[i], m_small[i], v_small[i]
    grads['meta_tokens'], deltas['meta_tokens'], new_m['meta_tokens'], new_v['meta_tokens'] = g_meta, d_meta, m_meta, v_meta
    return (loss.reshape(()), grad_x, *[grads[n] for n in _WEIGHTS], *[deltas[n] for n in _WEIGHTS],
            *[new_m[n] for n in _WEIGHTS], *[new_v[n] for n in _WEIGHTS])
```

```python
import numpy as np
import jax
import jax.numpy as jnp
from jax import lax
from jax.experimental import pallas as pl
from jax.experimental.pallas import tpu as pltpu

F32 = jnp.float32
BF16 = jnp.bfloat16

HEAD_DIM = 64
N_META = 16
BLOCK = 128
WINDOW = 128
PAD = BLOCK - N_META
EPS = 1e-6
NEG_INF = -1e30
SWA_GROUP = 8
NCH = 4
LANES = 128
QBLOCK = 512

ADAM_LR = 0.001
ADAM_B1 = 0.9
ADAM_B2 = 0.999
ADAM_EPS = 1e-08
ADAM_WD = 0.01
ADAM_STEP = 10

V7X_VMEM_BYTES = 64 * 1024 * 1024
VMEM_LIMIT = V7X_VMEM_BYTES - 8 * 1024 * 1024
MESH = pl.DeviceIdType.MESH
HIGHEST = lax.Precision.HIGHEST

_TM = (1088, 1024, 704, 512, 384, 256, 128)
_TN = (1408, 1024, 768, 512, 384, 256, 128)
_TK = (2176, 1408, 1024, 512, 384, 256, 128)
_TR = (544, 512, 384, 272, 256, 128)


def _pick(n, cands):
    for c in cands:
        if n % c == 0:
            return c
    return n


def _cparams(*sem):
    return pltpu.CompilerParams(dimension_semantics=sem if sem else None, vmem_limit_bytes=VMEM_LIMIT)


def _matmul(a, b, *, name, nt=False, b_chunked=False, out_chunked=False, out_dtype=F32,
            residual=None, scale=1.0, extras=(), epilogue=None, out_dtypes=None, comm=None):
    M, K = a.shape
    if not nt:
        N = b.shape[-1] * (NCH if b_chunked else 1)
        assert b.shape[-2] == K
        k_unit = K
    else:
        N = b.shape[-2]
        k_unit = b.shape[-1]
        assert k_unit * (NCH if b_chunked else 1) == K
    n_unit = N // NCH if (out_chunked or (b_chunked and not nt)) else N
    tm, tn, tk = _pick(M, _TM), _pick(n_unit, _TN), _pick(k_unit, _TK)
    if epilogue is None:
        extras = () if residual is None else (residual,)
        out_dtypes = (out_dtype,)

        def epilogue(acc, *res):
            r = acc * scale if scale != 1.0 else acc
            return (r + res[0] if res else r,)
    n_out = len(out_dtypes)
    n_temps = 4 if n_out > 1 else 0
    assert not (extras and out_chunked)

    def est(tm_):
        return (2 * tm_ * tk * 2 + 2 * tk * tn * 2 + tm_ * tn * 4
                + sum(2 * tm_ * tn * jnp.dtype(d).itemsize for d in out_dtypes)
                + sum(2 * tm_ * tn * e.dtype.itemsize for e in extras) + n_temps * tm_ * tn * 4)

    while est(tm) > VMEM_LIMIT * 3 // 4 and tm % 32 == 0:
        tm //= 2
    npc, kpc = n_unit // tn, k_unit // tk
    nk = K // tk
    grid = (M // tm, N // tn, nk)

    a_spec = pl.BlockSpec((tm, tk), lambda i, j, k: (i, k))
    if not nt:
        if b_chunked:
            b_spec = pl.BlockSpec((None, tk, tn), lambda i, j, k: (j // npc, k, j % npc))
        else:
            b_spec = pl.BlockSpec((tk, tn), lambda i, j, k: (k, j))
        dims = (((1,), (0,)), ((), ()))
    else:
        if b_chunked:
            b_spec = pl.BlockSpec((None, tn, tk), lambda i, j, k: (k // kpc, j, k % kpc))
        else:
            b_spec = pl.BlockSpec((tn, tk), lambda i, j, k: (j, k))
        dims = (((1,), (1,)), ((), ()))
    tile = pl.BlockSpec((tm, tn), lambda i, j, k: (i, j))
    if out_chunked:
        o_spec = pl.BlockSpec((None, tm, tn), lambda i, j, k: (j // npc, i, j % npc))
        out_shapes = [jax.ShapeDtypeStruct((NCH, M, n_unit), d) for d in out_dtypes]
    else:
        o_spec = tile
        out_shapes = [jax.ShapeDtypeStruct((M, N), d) for d in out_dtypes]
    in_specs = [a_spec, b_spec] + [tile] * len(extras)
    args = [a, b, *extras]

    n_main = len(args)
    n_cin = 0 if comm is None else len(comm.operands)
    n_cout = 0 if comm is None else len(comm.out_shapes)

    def body(*refs):
        a_ref, b_ref = refs[0], refs[1]
        e_refs = refs[2:n_main]
        o_refs = refs[n_main + n_cin:n_main + n_cin + n_out]
        acc_ref = refs[n_main + n_cin + n_out + n_cout]
        i, j, k = pl.program_id(0), pl.program_id(1), pl.program_id(2)
        if comm is not None:
            cin = refs[n_main:n_main + n_cin]
            cout = refs[n_main + n_cin + n_out:n_main + n_cin + n_out + n_cout]
            send_sems, recv_sems = refs[-2:]

            @pl.when((i == 0) & (j == 0) & (k == 0))
            def _():
                for cp, _ in comm.copies(cin, cout, send_sems, recv_sems):
                    cp.start()

        @pl.when(k == 0)
        def _():
            acc_ref[...] = jnp.zeros_like(acc_ref)

        acc_ref[...] += lax.dot_general(a_ref[...], b_ref[...], dims, preferred_element_type=F32)

        @pl.when(k == nk - 1)
        def _():
            for o_ref, val in zip(o_refs, epilogue(acc_ref[...], *[e[...] for e in e_refs])):
                o_ref[...] = val.astype(o_ref.dtype)

        if comm is not None:
            @pl.when((i == grid[0] - 1) & (j == grid[1] - 1) & (k == nk - 1))
            def _():
                cps = comm.copies(cin, cout, send_sems, recv_sems)
                for _, arrival in cps:
                    arrival.wait_recv()
                for cp, _ in cps:
                    cp.wait_send()

    scratch = [pltpu.VMEM((tm, tn), F32)]
    if comm is None:
        outs = pl.pallas_call(
            body, name=name, grid=grid, in_specs=in_specs, out_specs=[o_spec] * n_out, out_shape=out_shapes,
            scratch_shapes=scratch, compiler_params=_cparams("parallel", "parallel", "arbitrary"),
        )(*args)
    else:
        scratch += [pltpu.SemaphoreType.DMA((comm.n_sems,)), pltpu.SemaphoreType.DMA((comm.n_sems,))]
        outs = pl.pallas_call(
            body, name=name, grid=grid, in_specs=in_specs + [_ANY] * n_cin,
            out_specs=[o_spec] * n_out + [_ANY] * n_cout, out_shape=out_shapes + list(comm.out_shapes),
            input_output_aliases={n_main + s: n_out + d for s, d in comm.aliases.items()},
            scratch_shapes=scratch, compiler_params=_cparams("arbitrary", "arbitrary", "arbitrary"),
        )(*args, *comm.operands)
    return outs[0] if len(outs) == 1 else tuple(outs)


def _call_carrying(body, comm, args, *, name, grid, in_specs, out_specs, out_shape, scratch_shapes):
    if comm is None:
        return pl.pallas_call(body, name=name, grid=grid, in_specs=in_specs, out_specs=out_specs, out_shape=out_shape,
                              scratch_shapes=scratch_shapes, compiler_params=_cparams("arbitrary"))(*args)
    n_in, n_out, n_scr = len(in_specs), len(out_specs), len(scratch_shapes)
    n_cin, n_cout = len(comm.operands), len(comm.out_shapes)

    def carrying(*refs):
        ins, cin = refs[:n_in], refs[n_in:n_in + n_cin]
        outs = refs[n_in + n_cin:n_in + n_cin + n_out]
        cout = refs[n_in + n_cin + n_out:n_in + n_cin + n_out + n_cout]
        scr = refs[n_in + n_cin + n_out + n_cout:n_in + n_cin + n_out + n_cout + n_scr]
        send_sems, recv_sems = refs[-2:]
        step = pl.program_id(0)

        @pl.when(step == 0)
        def _():
            for cp, _ in comm.copies(cin, cout, send_sems, recv_sems):
                cp.start()

        body(*ins, *outs, *scr)

        @pl.when(step == grid[0] - 1)
        def _():
            cps = comm.copies(cin, cout, send_sems, recv_sems)
            for _, arrival in cps:
                arrival.wait_recv()
            for cp, _ in cps:
                cp.wait_send()

    return pl.pallas_call(
        carrying, name=name, grid=grid, in_specs=list(in_specs) + [_ANY] * n_cin,
        out_specs=list(out_specs) + [_ANY] * n_cout, out_shape=list(out_shape) + list(comm.out_shapes),
        input_output_aliases={n_in + s: n_out + d for s, d in comm.aliases.items()},
        scratch_shapes=list(scratch_shapes) + [pltpu.SemaphoreType.DMA((comm.n_sems,)), pltpu.SemaphoreType.DMA((comm.n_sems,))],
        compiler_params=_cparams("arbitrary"),
    )(*args, *comm.operands)


def _carried(queue, fn, n_out, *args, **kw):
    job = queue.take() if queue is not None else None
    outs = fn(*args, comm=job, **kw)
    if job is not None:
        queue.give(outs[n_out:])
    return outs[:n_out]


def _transpose(x, *, name):
    M, N = x.shape
    tc = _pick(N, (512, 384, 256, 128))

    def body(x_ref, o_ref):
        o_ref[...] = x_ref[...].astype(F32).T.astype(o_ref.dtype)

    return pl.pallas_call(
        body, name=name, grid=(N // tc,),
        in_specs=[pl.BlockSpec((M, tc), lambda j: (0, j))],
        out_specs=pl.BlockSpec((tc, M), lambda j: (j, 0)),
        out_shape=jax.ShapeDtypeStruct((N, M), x.dtype),
        compiler_params=_cparams("parallel"),
    )(x)


def _rms_fwd(h, g, *, name):
    L, D = h.shape
    tr = _pick(L, _TR)

    def body(h_ref, g_ref, o_ref):
        x = h_ref[...]
        r = lax.rsqrt(jnp.mean(x * x, axis=-1, keepdims=True) + EPS)
        o_ref[...] = (x * r * g_ref[...]).astype(o_ref.dtype)

    return pl.pallas_call(
        body, name=name, grid=(L // tr,),
        in_specs=[pl.BlockSpec((tr, D), lambda i: (i, 0)), pl.BlockSpec((1, D), lambda i: (0, 0))],
        out_specs=pl.BlockSpec((tr, D), lambda i: (i, 0)),
        out_shape=jax.ShapeDtypeStruct((L, D), BF16),
        compiler_params=_cparams("parallel"),
    )(h, g)


def _rms_bwd(dy, h, g, dh, *, name):
    L, D = h.shape
    tr = _pick(L, _TR)

    def body(dy_ref, h_ref, g_ref, dh_ref, o_ref, ob_ref, dg_ref):
        i = pl.program_id(0)
        x = h_ref[...]
        dyv = dy_ref[...]
        r = lax.rsqrt(jnp.mean(x * x, axis=-1, keepdims=True) + EPS)
        xh = x * r
        dxh = dyv * g_ref[...]
        dx = r * (dxh - xh * jnp.mean(dxh * xh, axis=-1, keepdims=True))
        total = dh_ref[...] + dx
        o_ref[...] = total
        ob_ref[...] = total.astype(ob_ref.dtype)
        part = jnp.sum(dyv * xh, axis=0, keepdims=True)

        @pl.when(i == 0)
        def _():
            dg_ref[...] = part

        @pl.when(i > 0)
        def _():
            dg_ref[...] += part

    row = pl.BlockSpec((tr, D), lambda i: (i, 0))
    vec = pl.BlockSpec((1, D), lambda i: (0, 0))
    return pl.pallas_call(
        body, name=name, grid=(L // tr,),
        in_specs=[row, row, vec, row], out_specs=[row, row, vec],
        out_shape=[jax.ShapeDtypeStruct((L, D), F32), jax.ShapeDtypeStruct((L, D), BF16), jax.ShapeDtypeStruct((1, D), F32)],
        compiler_params=_cparams("arbitrary"),
    )(dy, h, g, dh)


def _swiglu_epilogue(up, gate):
    g = gate.astype(F32)
    return up, g * jax.nn.sigmoid(g) * up


def _swiglu_bwd_epilogue(acc, gate, up):
    d = 0.5 * acc
    g = gate.astype(F32)
    sg = jax.nn.sigmoid(g)
    return d * up.astype(F32) * sg * (1.0 + g * (1.0 - sg)), d * g * sg


def _loss_grad(h, target, *, name):
    L, D = h.shape
    S = target.shape[0]
    nb = L // BLOCK

    def body(h_ref, t_ref, loss_ref, dh_ref, dhb_ref):
        i = pl.program_id(0)

        @pl.when(i == 0)
        def _():
            loss_ref[...] = jnp.zeros_like(loss_ref)
            dh_ref[...] = jnp.zeros_like(dh_ref)
            dhb_ref[...] = jnp.zeros_like(dhb_ref)

        @pl.when(i > 0)
        def _():
            err = h_ref[...] - t_ref[...]
            d = err * (1.0 / D)
            dh_ref[...] = d
            dhb_ref[...] = d.astype(dhb_ref.dtype)
            loss_ref[...] += jnp.full(loss_ref.shape, (0.5 / D) * jnp.sum(err * err), F32)

    row = pl.BlockSpec((BLOCK, D), lambda i: (i, 0))
    return pl.pallas_call(
        body, name=name, grid=(nb,),
        in_specs=[row, pl.BlockSpec((BLOCK, D), lambda i: (jnp.maximum(i - 1, 0), 0))],
        out_specs=[pl.BlockSpec((1, LANES), lambda i: (0, 0)), row, row],
        out_shape=[jax.ShapeDtypeStruct((1, LANES), F32), jax.ShapeDtypeStruct((L, D), F32),
                   jax.ShapeDtypeStruct((L, D), BF16)],
        compiler_params=_cparams("arbitrary"),
    )(h, target)


def _ffn_fwd(h, g, w, tag, queue=None):
    hn = _rms_fwd(h, g, name=f"{tag}_rms")
    gate = _mm(queue, hn, w(0), name=f"{tag}_gate", b_chunked=True, out_dtype=BF16)
    up, act = _mm(queue, hn, w(1), name=f"{tag}_up", b_chunked=True, extras=(gate,), epilogue=_swiglu_epilogue,
                  out_dtypes=(BF16, BF16))
    h_out = _mm(queue, act, w(2), name=f"{tag}_down", residual=h, scale=0.5)
    return h_out, (h, hn, gate, up, act)


def _ffn_bwd(dh, dout, saved, g, wg, wu, wd, tag, queue=None, on_dw=None):
    h, hn, gate, up, act = saved
    dgate, dup = _mm(queue, dout, wd, name=f"{tag}_dact", nt=True, extras=(gate, up), epilogue=_swiglu_bwd_epilogue,
                     out_dtypes=(BF16, BF16))
    actT = _transpose(act, name=f"{tag}_actT")
    on_dw(2, _mm(queue, actT, dout, name=f"{tag}_dwd", out_dtype=BF16, scale=0.5))
    hnT = _transpose(hn, name=f"{tag}_hnT")
    on_dw(0, _mm(queue, hnT, dgate, name=f"{tag}_dwg", out_chunked=True, out_dtype=BF16))
    on_dw(1, _mm(queue, hnT, dup, name=f"{tag}_dwu", out_chunked=True, out_dtype=BF16))
    dhn = _mm(queue, dgate, wg, name=f"{tag}_dhn_g", nt=True, b_chunked=True)
    dhn = _mm(queue, dup, wu, name=f"{tag}_dhn_u", nt=True, b_chunked=True, residual=dhn)
    return _rms_bwd(dhn, h, g, dh, name=f"{tag}_drms")


class _MixDims:
    def __init__(self, d_model):
        self.wf = d_model // 2
        self.ws = d_model // 2
        self.pf = self.wf // LANES
        self.ps = self.ws // LANES
        self.hf = self.wf // HEAD_DIM
        self.hq = self.ws // HEAD_DIM
        self.nkv = max(1, self.hq // SWA_GROUP)
        self.g = self.hq // self.nkv
        self.bq_f, self.bk_f, self.bv_f = 0, self.pf, 2 * self.pf
        self.bq_s = 3 * self.pf
        self.bk_s = self.bq_s + self.ps
        self.bv_s = self.bk_s + self.nkv
        self.bz = self.bv_s + self.nkv
        self.nu = (self.bz + 1) * LANES
        self.nup = -(-self.nu // 512) * 512
        self.in_width = 3 * self.wf + self.hf + self.ws + 2 * self.nkv * HEAD_DIM
        assert self.hf <= 2 * (LANES // 8)

    def gate_lane(self, h):
        return 8 * (h // 2) + h % 2

    def column_map(self):
        wf, ws, hd = self.wf, self.ws, HEAD_DIM
        src = np.full((self.nup,), -1, np.int64)
        src[0:3 * wf] = np.arange(3 * wf)
        o_sq = 3 * wf + self.hf
        src[self.bq_s * LANES:self.bq_s * LANES + ws] = o_sq + np.arange(ws)
        o_sk = o_sq + ws
        o_sv = o_sk + self.nkv * hd
        for kv in range(self.nkv):
            for rep in range(2):
                c0 = (self.bk_s + kv) * LANES + rep * hd
                src[c0:c0 + hd] = o_sk + kv * hd + np.arange(hd)
                c0 = (self.bv_s + kv) * LANES + rep * hd
                src[c0:c0 + hd] = o_sv + kv * hd + np.arange(hd)
        for h in range(self.hf):
            src[self.bz * LANES + self.gate_lane(h)] = 3 * wf + h
        return src

    def grad_column_map(self):
        src = self.column_map()
        dst = np.zeros((self.in_width,), np.int64)
        for col in range(self.nup - 1, -1, -1):
            if src[col] >= 0:
                dst[src[col]] = col
        return dst


def _block_diag_mean():
    m = np.zeros((LANES, LANES), np.float32)
    m[:HEAD_DIM, :HEAD_DIM] = 1.0 / HEAD_DIM
    m[HEAD_DIM:, HEAD_DIM:] = 1.0 / HEAD_DIM
    return jnp.asarray(m)


def _fold_halves():
    m = np.eye(LANES, dtype=np.float32)
    m[np.arange(LANES), (np.arange(LANES) + HEAD_DIM) % LANES] = 1.0
    return jnp.asarray(m)


def _gate_expand(md):
    e = np.zeros((LANES, md.wf), np.float32)
    for h in range(md.hf):
        e[md.gate_lane(h), h * HEAD_DIM:(h + 1) * HEAD_DIM] = 1.0
    return jnp.asarray(e)


def _qblocks(L, qb):
    blocks = [(0, BLOCK)]
    r = BLOCK
    while r < L:
        blocks.append((r, qb))
        r += qb
    assert r == L
    return blocks


def _f32dot(a, b):
    return jnp.dot(a, b, precision=HIGHEST, preferred_element_type=F32)


_DIMS_NT = (((1,), (1,)), ((), ()))


def _dot_nt(a, b):
    return lax.dot_general(a, b, _DIMS_NT, preferred_element_type=F32)


def _dot_tn(a, b):
    return jnp.dot(a.T.astype(BF16), b, preferred_element_type=F32)


def _log_sigmoid(z):
    return jnp.minimum(z, 0.0) - jnp.log(1.0 + jnp.exp(-jnp.abs(z)))


def _gate_fwd(u, b, md, *, name):
    L = u.shape[0]
    nb = L // BLOCK
    expand = _gate_expand(md)

    def body(z_ref, b_ref, e_ref, cexp_ref, ct_ref, c_s):
        ri = lax.broadcasted_iota(jnp.int32, (BLOCK, BLOCK), 0)
        ci = lax.broadcasted_iota(jnp.int32, (BLOCK, BLOCK), 1)
        tri = (ri >= ci).astype(F32)
        carry = jnp.zeros((1, LANES), F32)
        for bi in range(nb):
            rows = pl.ds(bi * BLOCK, BLOCK)
            logf = _log_sigmoid(z_ref[rows, :] + b_ref[...])
            blk = _f32dot(tri, logf) + carry
            c_s[rows, :] = blk
            carry = blk[BLOCK - 1:BLOCK, :]
        c = c_s[...]
        ct_ref[...] = c.T
        cexp_ref[...] = _f32dot(c, e_ref[...])

    return pl.pallas_call(
        body, name=name, grid=(1,),
        in_specs=[pl.BlockSpec((L, LANES), lambda i: (0, md.bz)), pl.BlockSpec((1, LANES), lambda i: (0, 0)),
                  pl.BlockSpec((LANES, md.wf), lambda i: (0, 0))],
        out_specs=[pl.BlockSpec((L, md.wf), lambda i: (0, 0)), pl.BlockSpec((LANES, L), lambda i: (0, 0))],
        out_shape=[jax.ShapeDtypeStruct((L, md.wf), F32), jax.ShapeDtypeStruct((LANES, L), F32)],
        scratch_shapes=[pltpu.VMEM((L, LANES), F32)],
        compiler_params=_cparams("arbitrary"),
    )(u, b, expand)


def _gate_bwd(u, b, dck_t, md, *, name):
    L = u.shape[0]
    nb = L // BLOCK

    def body(z_ref, b_ref, dck_ref, dz_ref, db_ref, dc_s):
        ri = lax.broadcasted_iota(jnp.int32, (BLOCK, BLOCK), 0)
        ci = lax.broadcasted_iota(jnp.int32, (BLOCK, BLOCK), 1)
        triu = (ri <= ci).astype(F32)
        dc_s[...] = -dck_ref[...].T
        carry = jnp.zeros((1, LANES), F32)
        db = jnp.zeros((1, LANES), F32)
        for bi in range(nb - 1, -1, -1):
            rows = pl.ds(bi * BLOCK, BLOCK)
            blk = _f32dot(triu, dc_s[rows, :]) + carry
            carry = blk[0:1, :]
            z = z_ref[rows, :] + b_ref[...]
            dz = blk * jax.nn.sigmoid(-z)
            if bi == 0:
                dz = jnp.where(lax.broadcasted_iota(jnp.int32, (BLOCK, LANES), 0) >= PAD, dz, 0.0)
            dz_ref[rows, :] = dz
            db = db + jnp.sum(dz, axis=0, keepdims=True)
        db_ref[...] = db

    return pl.pallas_call(
        body, name=name, grid=(1,),
        in_specs=[pl.BlockSpec((L, LANES), lambda i: (0, md.bz)), pl.BlockSpec((1, LANES), lambda i: (0, 0)),
                  pl.BlockSpec((LANES, L), lambda i: (0, 0))],
        out_specs=[pl.BlockSpec((L, LANES), lambda i: (0, 0)), pl.BlockSpec((1, LANES), lambda i: (0, 0))],
        out_shape=[jax.ShapeDtypeStruct((L, LANES), F32), jax.ShapeDtypeStruct((1, LANES), F32)],
        scratch_shapes=[pltpu.VMEM((L, LANES), F32)],
        compiler_params=_cparams("arbitrary"),
    )(u, b, dck_t)


def _head_norm(x, g, bd):
    r = lax.rsqrt(_f32dot(x * x, bd) + EPS)
    xh = x * r
    return xh * g, xh, r


def _head_norm_bwd(dy, xh, r, g, bd):
    dxh = dy * g
    dx = r * (dxh - xh * _f32dot(dxh * xh, bd))
    return dx, jnp.sum(dy * xh, axis=0, keepdims=True)


def _lane_half():
    return lax.broadcasted_iota(jnp.int32, (1, LANES), 1) < HEAD_DIM


def _store_head_pair(x, half, a_s, b_s):
    a_s[...] = jnp.where(half, x, 0.0).astype(BF16)
    b_s[...] = jnp.where(half, 0.0, x).astype(BF16)


def _fox_scores(qm, kn_s, cexp_ref, ct_ref, r0, nr, klen, hh):
    s = _dot_nt(qm, kn_s[0:klen, :]) * (HEAD_DIM ** -0.5)
    s = s + cexp_ref[r0:r0 + nr, HEAD_DIM * hh:HEAD_DIM * hh + 1] - ct_ref[hh:hh + 1, 0:klen]
    qp = r0 + lax.broadcasted_iota(jnp.int32, (nr, klen), 0)
    kp = lax.broadcasted_iota(jnp.int32, (nr, klen), 1)
    return jnp.where((kp <= qp) & (kp >= PAD), s, NEG_INF)


def _fox_fwd(u, cexp, ct3, qg, kg, md, *, name, comm=None):
    L = u.shape[0]
    blocks = _qblocks(L, QBLOCK)
    bd = _block_diag_mean()

    def body(q_ref, k_ref, v_ref, cexp_ref, ct_ref, qg_ref, kg_ref, bd_ref, o_ref, lse_ref, qa_s, qb_s, kn_s, v_s):
        half = _lane_half()
        bdv = bd_ref[...]
        _store_head_pair(_head_norm(q_ref[...], qg_ref[...], bdv)[0], half, qa_s, qb_s)
        kn_s[...] = _head_norm(k_ref[...], kg_ref[...], bdv)[0].astype(BF16)
        v_s[...] = v_ref[...].astype(BF16)
        for r0, nr in blocks:
            klen = r0 + nr
            o_blk = lse_blk = None
            for hh, q_s in enumerate((qa_s, qb_s)):
                s = _fox_scores(q_s[r0:r0 + nr, :], kn_s, cexp_ref, ct_ref, r0, nr, klen, hh)
                m = jnp.max(s, axis=-1, keepdims=True)
                p = jnp.exp(s - m)
                l = jnp.sum(p, axis=-1, keepdims=True)
                oh = jnp.dot(p.astype(BF16), v_s[0:klen, :], preferred_element_type=F32) * (1.0 / l)
                lh = jnp.broadcast_to(jnp.where(m > 0.5 * NEG_INF, m + jnp.log(l), 0.0), (nr, LANES))
                o_blk = oh if hh == 0 else jnp.where(half, o_blk, oh)
                lse_blk = lh if hh == 0 else jnp.where(half, lse_blk, lh)
            o_ref[r0:r0 + nr, :] = o_blk
            lse_ref[r0:r0 + nr, :] = lse_blk

    col = lambda base: pl.BlockSpec((L, LANES), lambda j: (0, base + j))
    vec = pl.BlockSpec((1, LANES), lambda j: (0, 0))
    return _call_carrying(
        body, comm, (u, u, u, cexp, ct3, qg, kg, bd), name=name, grid=(md.pf,),
        in_specs=[col(md.bq_f), col(md.bk_f), col(md.bv_f), col(0),
                  pl.BlockSpec((None, 8, L), lambda j: (j, 0, 0)), vec, vec,
                  pl.BlockSpec((LANES, LANES), lambda j: (0, 0))],
        out_specs=[col(0), col(0)],
        out_shape=[jax.ShapeDtypeStruct((L, md.wf), F32)] * 2,
        scratch_shapes=[pltpu.VMEM((L, LANES), BF16)] * 4)


def _softmax_bwd(p, dp):
    pdp = p * dp
    return pdp - p * jnp.sum(pdp, axis=-1, keepdims=True)


def _fox_bwd(u, cexp, ct3, qg, kg, do, lse, md, *, name, comm=None):
    L = u.shape[0]
    blocks = _qblocks(L, QBLOCK // 2)
    bd = _block_diag_mean()
    fold = _fold_halves()
    npairs = md.pf

    def body(q_ref, k_ref, v_ref, cexp_ref, ct_ref, qg_ref, kg_ref, bd_ref, fold_ref, do_ref, lse_ref,
             dq_ref, dk_ref, dv_ref, dck_ref, dqg_ref, dkg_ref,
             qa_s, qb_s, kn_s, v_s, doa_s, dob_s, dqn_s, dkn_s, dvv_s):
        j = pl.program_id(0)
        half = _lane_half()
        bdv = bd_ref[...]
        qy, qh, rq = _head_norm(q_ref[...], qg_ref[...], bdv)
        ky, kh, rk = _head_norm(k_ref[...], kg_ref[...], bdv)
        _store_head_pair(qy, half, qa_s, qb_s)
        _store_head_pair(do_ref[...], half, doa_s, dob_s)
        kn_s[...] = ky.astype(BF16)
        v_s[...] = v_ref[...].astype(BF16)
        dkn_s[...] = jnp.zeros_like(dkn_s)
        dvv_s[...] = jnp.zeros_like(dvv_s)
        dck_ref[...] = jnp.zeros_like(dck_ref)
        for r0, nr in blocks:
            klen = r0 + nr
            dq_blk = None
            for hh, (q_s, do_s) in enumerate(((qa_s, doa_s), (qb_s, dob_s))):
                c0 = HEAD_DIM * hh
                qm = q_s[r0:r0 + nr, :]
                dom = do_s[r0:r0 + nr, :]
                s = _fox_scores(qm, kn_s, cexp_ref, ct_ref, r0, nr, klen, hh)
                p = jnp.exp(s - lse_ref[r0:r0 + nr, c0:c0 + 1])
                ds = _softmax_bwd(p, _dot_nt(dom, v_s[0:klen, :]))
                dck_ref[hh:hh + 1, 0:klen] += jnp.sum(ds, axis=0, keepdims=True)
                ds = ds * (HEAD_DIM ** -0.5)
                dq_h = jnp.dot(ds.astype(BF16), kn_s[0:klen, :], preferred_element_type=F32)
                dkn_s[0:klen, :] += _dot_tn(ds, qm)
                dvv_s[0:klen, :] += _dot_tn(p, dom)
                dq_blk = dq_h if hh == 0 else jnp.where(half, dq_blk, dq_h)
            dqn_s[r0:r0 + nr, :] = dq_blk
        dq, dqg = _head_norm_bwd(dqn_s[...], qh, rq, qg_ref[...], bdv)
        dk, dkg = _head_norm_bwd(dkn_s[...], kh, rk, kg_ref[...], bdv)
        dq_ref[...] = dq
        dk_ref[...] = dk
        dv_ref[...] = dvv_s[...]

        @pl.when(j == 0)
        def _():
            dqg_ref[...] = jnp.zeros_like(dqg_ref)
            dkg_ref[...] = jnp.zeros_like(dkg_ref)

        dqg_ref[...] += dqg
        dkg_ref[...] += dkg

        @pl.when(j == npairs - 1)
        def _():
            dqg_ref[...] = _f32dot(jnp.broadcast_to(dqg_ref[...], (8, LANES)), fold_ref[...])[0:1, :]
            dkg_ref[...] = _f32dot(jnp.broadcast_to(dkg_ref[...], (8, LANES)), fold_ref[...])[0:1, :]

    col = lambda base: pl.BlockSpec((L, LANES), lambda j: (0, base + j))
    vec = pl.BlockSpec((1, LANES), lambda j: (0, 0))
    sq = pl.BlockSpec((LANES, LANES), lambda j: (0, 0))
    ct_spec = pl.BlockSpec((None, 8, L), lambda j: (j, 0, 0))
    big = jax.ShapeDtypeStruct((L, md.wf), F32)
    small = jax.ShapeDtypeStruct((1, LANES), F32)
    return _call_carrying(
        body, comm, (u, u, u, cexp, ct3, qg, kg, bd, fold, do, lse), name=name, grid=(md.pf,),
        in_specs=[col(md.bq_f), col(md.bk_f), col(md.bv_f), col(0), ct_spec, vec, vec, sq, sq, col(0), col(0)],
        out_specs=[col(0), col(0), col(0), ct_spec, vec, vec],
        out_shape=[big, big, big, jax.ShapeDtypeStruct((md.pf, 8, L), F32), small, small],
        scratch_shapes=[pltpu.VMEM((L, LANES), BF16)] * 6 + [pltpu.VMEM((L, LANES), F32)] * 3)


def _swa_scores(qm, kn_s, slope, k0, r0, nr, klen):
    s = _dot_nt(qm, kn_s[k0:k0 + klen, :]) * (HEAD_DIM ** -0.5)
    qp = r0 + lax.broadcasted_iota(jnp.int32, (nr, klen), 0)
    kp = k0 + lax.broadcasted_iota(jnp.int32, (nr, klen), 1)
    dist = qp - kp
    s = s - slope * dist.astype(F32)
    return jnp.where((dist >= 0) & (dist < WINDOW) & (kp >= PAD), s, NEG_INF)


def _swa_fwd(u, sinkp, slopep, qg, kg, md, *, name, comm=None):
    L = u.shape[0]
    blocks = _qblocks(L, QBLOCK)
    bd = _block_diag_mean()

    def body(q_ref, k_ref, v_ref, sink_ref, slope_ref, qg_ref, kg_ref, bd_ref, o_ref, lse_ref, qa_s, qb_s, kn_s, v_s):
        half = _lane_half()
        bdv = bd_ref[...]
        _store_head_pair(_head_norm(q_ref[...], qg_ref[...], bdv)[0], half, qa_s, qb_s)
        kn_s[...] = _head_norm(k_ref[...], kg_ref[...], bdv)[0].astype(BF16)
        v_s[...] = v_ref[...].astype(BF16)
        for r0, nr in blocks:
            k0 = max(r0 - BLOCK, 0)
            klen = r0 + nr - k0
            o_blk = lse_blk = None
            for hh, q_s in enumerate((qa_s, qb_s)):
                c0 = HEAD_DIM * hh
                s = _swa_scores(q_s[r0:r0 + nr, :], kn_s, slope_ref[0:1, c0:c0 + 1], k0, r0, nr, klen)
                sink = sink_ref[0:1, c0:c0 + 1]
                m = jnp.maximum(jnp.max(s, axis=-1, keepdims=True), sink)
                p = jnp.exp(s - m)
                den = jnp.sum(p, axis=-1, keepdims=True) + jnp.exp(sink - m)
                oh = jnp.dot(p.astype(BF16), v_s[k0:k0 + klen, :], preferred_element_type=F32) * (1.0 / den)
                lh = jnp.broadcast_to(m + jnp.log(den), (nr, LANES))
                o_blk = oh if hh == 0 else jnp.where(half, o_blk, oh)
                lse_blk = lh if hh == 0 else jnp.where(half, lse_blk, lh)
            o_ref[r0:r0 + nr, :] = o_blk
            lse_ref[r0:r0 + nr, :] = lse_blk

    g2 = md.g // 2
    qcol = pl.BlockSpec((L, LANES), lambda j: (0, md.bq_s + j))
    kcol = pl.BlockSpec((L, LANES), lambda j: (0, md.bk_s + j // g2))
    vcol = pl.BlockSpec((L, LANES), lambda j: (0, md.bv_s + j // g2))
    ocol = pl.BlockSpec((L, LANES), lambda j: (0, j))
    pvec = pl.BlockSpec((1, LANES), lambda j: (0, j))
    vec = pl.BlockSpec((1, LANES), lambda j: (0, 0))
    return _call_carrying(
        body, comm, (u, u, u, sinkp, slopep, qg, kg, bd), name=name, grid=(md.ps,),
        in_specs=[qcol, kcol, vcol, pvec, pvec, vec, vec, pl.BlockSpec((LANES, LANES), lambda j: (0, 0))],
        out_specs=[ocol, ocol],
        out_shape=[jax.ShapeDtypeStruct((L, md.ws), F32)] * 2,
        scratch_shapes=[pltpu.VMEM((L, LANES), BF16)] * 4)


def _swa_bwd(u, sinkp, slopep, qg, kg, do, lse, md, *, name, comm=None):
    L = u.shape[0]
    blocks = _qblocks(L, QBLOCK // 2)
    bd = _block_diag_mean()
    fold = _fold_halves()
    g2 = md.g // 2
    npairs = md.ps

    def body(q_ref, k_ref, v_ref, sink_ref, slope_ref, qg_ref, kg_ref, bd_ref, fold_ref, do_ref, lse_ref,
             dq_ref, dk_ref, dv_ref, dsink_ref, dqg_ref, dkg_ref,
             qa_s, qb_s, kn_s, v_s, doa_s, dob_s, dqn_s):
        j = pl.program_id(0)
        half = _lane_half()
        bdv = bd_ref[...]
        qy, qh, rq = _head_norm(q_ref[...], qg_ref[...], bdv)
        ky, kh, rk = _head_norm(k_ref[...], kg_ref[...], bdv)
        _store_head_pair(qy, half, qa_s, qb_s)
        _store_head_pair(do_ref[...], half, doa_s, dob_s)
        kn_s[...] = ky.astype(BF16)
        v_s[...] = v_ref[...].astype(BF16)

        @pl.when(j % g2 == 0)
        def _():
            dk_ref[...] = jnp.zeros_like(dk_ref)
            dv_ref[...] = jnp.zeros_like(dv_ref)

        @pl.when(j == 0)
        def _():
            dqg_ref[...] = jnp.zeros_like(dqg_ref)
            dkg_ref[...] = jnp.zeros_like(dkg_ref)

        dsink = [jnp.zeros((1, 1), F32), jnp.zeros((1, 1), F32)]
        for r0, nr in blocks:
            k0 = max(r0 - BLOCK, 0)
            klen = r0 + nr - k0
            dq_blk = None
            for hh, (q_s, do_s) in enumerate(((qa_s, doa_s), (qb_s, dob_s))):
                c0 = HEAD_DIM * hh
                qm = q_s[r0:r0 + nr, :]
                dom = do_s[r0:r0 + nr, :]
                s = _swa_scores(qm, kn_s, slope_ref[0:1, c0:c0 + 1], k0, r0, nr, klen)
                lse_h = lse_ref[r0:r0 + nr, c0:c0 + 1]
                p = jnp.exp(s - lse_h)
                pdp = p * _dot_nt(dom, v_s[k0:k0 + klen, :])
                delta = jnp.sum(pdp, axis=-1, keepdims=True)
                p_sink = jnp.exp(sink_ref[0:1, c0:c0 + 1] - lse_h)
                dsink[hh] = dsink[hh] - jnp.sum(p_sink * delta, axis=0, keepdims=True)
                ds = (pdp - p * delta) * (HEAD_DIM ** -0.5)
                dq_h = jnp.dot(ds.astype(BF16), kn_s[k0:k0 + klen, :], preferred_element_type=F32)
                dk_ref[k0:k0 + klen, :] += _dot_tn(ds, qm)
                dv_ref[k0:k0 + klen, :] += _dot_tn(p, dom)
                dq_blk = dq_h if hh == 0 else jnp.where(half, dq_blk, dq_h)
            dqn_s[r0:r0 + nr, :] = dq_blk
        dq, dqg = _head_norm_bwd(dqn_s[...], qh, rq, qg_ref[...], bdv)
        dq_ref[...] = dq
        dqg_ref[...] += dqg
        dsink_ref[...] = jnp.where(half, jnp.broadcast_to(dsink[0], (1, LANES)), jnp.broadcast_to(dsink[1], (1, LANES)))

        @pl.when(j % g2 == g2 - 1)
        def _():
            dkn = _f32dot(dk_ref[...], fold_ref[...])
            dk, dkg = _head_norm_bwd(dkn, kh, rk, kg_ref[...], bdv)
            dk_ref[...] = jnp.where(half, dk, 0.0)
            dv_ref[...] = jnp.where(half, _f32dot(dv_ref[...], fold_ref[...]), 0.0)
            dkg_ref[...] += dkg

        @pl.when(j == npairs - 1)
        def _():
            dqg_ref[...] = _f32dot(jnp.broadcast_to(dqg_ref[...], (8, LANES)), fold_ref[...])[0:1, :]

    qcol = pl.BlockSpec((L, LANES), lambda j: (0, md.bq_s + j))
    kcol = pl.BlockSpec((L, LANES), lambda j: (0, md.bk_s + j // g2))
    vcol = pl.BlockSpec((L, LANES), lambda j: (0, md.bv_s + j // g2))
    ocol = pl.BlockSpec((L, LANES), lambda j: (0, j))
    kvout = pl.BlockSpec((L, LANES), lambda j: (0, j // g2))
    pvec = pl.BlockSpec((1, LANES), lambda j: (0, j))
    vec = pl.BlockSpec((1, LANES), lambda j: (0, 0))
    sq = pl.BlockSpec((LANES, LANES), lambda j: (0, 0))
    kvshape = jax.ShapeDtypeStruct((L, LANES * md.nkv), F32)
    small = jax.ShapeDtypeStruct((1, LANES), F32)
    return _call_carrying(
        body, comm, (u, u, u, sinkp, slopep, qg, kg, bd, fold, do, lse), name=name, grid=(md.ps,),
        in_specs=[qcol, kcol, vcol, pvec, pvec, vec, vec, sq, sq, ocol, ocol],
        out_specs=[ocol, kvout, kvout, pvec, vec, vec],
        out_shape=[jax.ShapeDtypeStruct((L, md.ws), F32), kvshape, kvshape,
                   jax.ShapeDtypeStruct((1, md.ws), F32), small, small],
        scratch_shapes=[pltpu.VMEM((L, LANES), BF16)] * 6 + [pltpu.VMEM((L, LANES), F32)])


def _outnorm_fwd(of, os_, gf, gs, *, name):
    L, wf = of.shape
    ws = os_.shape[1]
    tr = _pick(L, _TR)

    def body(of_ref, os_ref, gf_ref, gs_ref, o_ref):
        for src, g_ref, c0, w in ((of_ref, gf_ref, 0, wf), (os_ref, gs_ref, wf, ws)):
            x = src[...]
            r = lax.rsqrt(jnp.mean(x * x, axis=-1, keepdims=True) + EPS)
            o_ref[:, c0:c0 + w] = (x * r * g_ref[...]).astype(o_ref.dtype)

    return pl.pallas_call(
        body, name=name, grid=(L // tr,),
        in_specs=[pl.BlockSpec((tr, wf), lambda i: (i, 0)), pl.BlockSpec((tr, ws), lambda i: (i, 0)),
                  pl.BlockSpec((1, wf), lambda i: (0, 0)), pl.BlockSpec((1, ws), lambda i: (0, 0))],
        out_specs=pl.BlockSpec((tr, wf + ws), lambda i: (i, 0)),
        out_shape=jax.ShapeDtypeStruct((L, wf + ws), BF16),
        compiler_params=_cparams("parallel"),
    )(of, os_, gf, gs)


def _outnorm_bwd(don, of, os_, gf, gs, *, name):
    L, wf = of.shape
    ws = os_.shape[1]
    tr = _pick(L, _TR)

    def body(d_ref, of_ref, os_ref, gf_ref, gs_ref, dof_ref, dos_ref, dgf_ref, dgs_ref):
        i = pl.program_id(0)
        for src, g_ref, c0, w, dx_ref, dg_ref in ((of_ref, gf_ref, 0, wf, dof_ref, dgf_ref),
                                                  (os_ref, gs_ref, wf, ws, dos_ref, dgs_ref)):
            x = src[...]
            dy = d_ref[:, c0:c0 + w]
            r = lax.rsqrt(jnp.mean(x * x, axis=-1, keepdims=True) + EPS)
            xh = x * r
            dxh = dy * g_ref[...]
            dx_ref[...] = r * (dxh - xh * jnp.mean(dxh * xh, axis=-1, keepdims=True))
            part = jnp.sum(dy * xh, axis=0, keepdims=True)

            @pl.when(i == 0)
            def _():
                dg_ref[...] = part

            @pl.when(i > 0)
            def _():
                dg_ref[...] += part

    rf = pl.BlockSpec((tr, wf), lambda i: (i, 0))
    rs = pl.BlockSpec((tr, ws), lambda i: (i, 0))
    vf = pl.BlockSpec((1, wf), lambda i: (0, 0))
    vs = pl.BlockSpec((1, ws), lambda i: (0, 0))
    return pl.pallas_call(
        body, name=name, grid=(L // tr,),
        in_specs=[pl.BlockSpec((tr, wf + ws), lambda i: (i, 0)), rf, rs, vf, vs],
        out_specs=[rf, rs, vf, vs],
        out_shape=[jax.ShapeDtypeStruct((L, wf), F32), jax.ShapeDtypeStruct((L, ws), F32),
                   jax.ShapeDtypeStruct((1, wf), F32), jax.ShapeDtypeStruct((1, ws), F32)],
        compiler_params=_cparams("arbitrary"),
    )(don, of, os_, gf, gs)


def _win_to_mine(w, md):
    d = w.shape[0]
    wf, ws, hd = md.wf, md.ws, HEAD_DIM
    o_z = 3 * wf
    o_sq = o_z + md.hf
    o_sk = o_sq + ws
    o_sv = o_sk + md.nkv * hd
    parts = [w[:, :3 * wf], w[:, o_sq:o_sq + ws]]
    for base in (o_sk, o_sv):
        for kv in range(md.nkv):
            blk = w[:, base + kv * hd:base + (kv + 1) * hd]
            parts += [blk, blk]
    z = w[:, o_z:o_z + md.hf].reshape(d, md.hf // 2, 2)
    z = jnp.pad(z, ((0, 0), (0, 0), (0, 6))).reshape(d, 4 * md.hf)
    parts.append(jnp.pad(z, ((0, 0), (0, LANES - 4 * md.hf + md.nup - md.nu))))
    return jnp.concatenate(parts, axis=1)


def _win_grad_to_ref(dw, md):
    d = dw.shape[0]
    wf, ws, hd = md.wf, md.ws, HEAD_DIM
    z = dw[:, md.bz * LANES:md.bz * LANES + 4 * md.hf].reshape(d, md.hf // 2, 8)[:, :, :2].reshape(d, md.hf)
    parts = [dw[:, :3 * wf], z, dw[:, md.bq_s * LANES:md.bq_s * LANES + ws]]
    for base in (md.bk_s, md.bv_s):
        for kv in range(md.nkv):
            c0 = (base + kv) * LANES
            parts.append(dw[:, c0:c0 + hd])
    return jnp.concatenate(parts, axis=1)


def _mix_small(p, md):
    tile2 = lambda v: jnp.tile(v.reshape(1, HEAD_DIM), (1, 2))
    b = p["b_forget"].reshape(md.hf // 2, 2)
    b = jnp.pad(b, ((0, 0), (0, 6))).reshape(1, 4 * md.hf)
    slopes = np.asarray(2.0 ** (-8.0 * np.arange(1, md.hq + 1) / md.hq), np.float32)
    return dict(
        g_mix=p["mix_norm"].reshape(1, -1),
        b_gate=jnp.pad(b, ((0, 0), (0, LANES - 4 * md.hf))),
        fqg=tile2(p["fox_q_norm"]), fkg=tile2(p["fox_k_norm"]),
        sqg=tile2(p["swa_q_norm"]), skg=tile2(p["swa_k_norm"]),
        sinkp=jnp.repeat(p["swa_sinks"], HEAD_DIM).reshape(1, md.ws),
        slopep=jnp.asarray(np.repeat(slopes, HEAD_DIM).reshape(1, md.ws)),
        gfo=p["fox_out_norm"].reshape(1, md.wf), gso=p["swa_out_norm"].reshape(1, md.ws),
    )


def _mix_fwd(h, sp, w, md, queue=None):
    L = h.shape[0]
    hn = _rms_fwd(h, sp["g_mix"], name="mix_rms")
    u = _mm(queue, hn, w(0), name="mix_u")
    cexp, ct = _gate_fwd(u, sp["b_gate"], md, name="gate_fwd")
    ct3 = ct[:8 * md.pf].reshape(md.pf, 8, L)
    of, lsef = _carried(queue, _fox_fwd, 2, u, cexp, ct3, sp["fqg"], sp["fkg"], md, name="fox_fwd")
    os_, lses = _carried(queue, _swa_fwd, 2, u, sp["sinkp"], sp["slopep"], sp["sqg"], sp["skg"], md, name="swa_fwd")
    on = _outnorm_fwd(of, os_, sp["gfo"], sp["gso"], name="outnorm_fwd")
    h_out = _matmul(on, w(1), name="mix_out", residual=h)
    return h_out, (h, hn, u, cexp, ct3, of, lsef, os_, lses, on)


def _mix_bwd(dh, dhb, saved, sp, w_in, w_out, md, queue=None):
    h, hn, u, cexp, ct3, of, lsef, os_, lses, on = saved
    L = h.shape[0]
    don = _matmul(dhb, w_out, name="mix_don", nt=True)
    dw_out = _matmul(_transpose(on, name="mix_onT"), dhb, name="mix_dwout", out_dtype=BF16)
    dof, dos, dgfo, dgso = _outnorm_bwd(don, of, os_, sp["gfo"], sp["gso"], name="outnorm_bwd")
    duq, duk, duv, dck, dfqg, dfkg = _carried(queue, _fox_bwd, 6, u, cexp, ct3, sp["fqg"], sp["fkg"], dof, lsef, md,
                                              name="fox_bwd")
    dsq, dsk, dsv, dsinkp, dsqg, dskg = _carried(queue, _swa_bwd, 6, u, sp["sinkp"], sp["slopep"], sp["sqg"], sp["skg"], dos,
                                                 lses, md, name="swa_bwd")
    dck_t = jnp.pad(dck.reshape(8 * md.pf, L), ((0, LANES - 8 * md.pf), (0, 0)))
    dz, db = _gate_bwd(u, sp["b_gate"], dck_t, md, name="gate_bwd")
    du = jnp.concatenate([duq, duk, duv, dsq, dsk, dsv, dz, jnp.zeros((L, md.nup - md.nu), F32)], axis=1).astype(BF16)
    dhn = _mm(queue, du, w_in, name="mix_dhn", nt=True)
    dw_in = _mm(queue, _transpose(hn, name="mix_hnT"), du, name="mix_dwin", out_dtype=BF16)
    dh_in, dhb_in, dg_mix = _rms_bwd(dhn, h, sp["g_mix"], dh, name="mix_drms")
    small = dict(
        mix_norm=dg_mix.reshape(-1),
        b_forget=db[0, :4 * md.hf].reshape(md.hf // 2, 8)[:, :2].reshape(md.hf),
        fox_q_norm=dfqg[0, :HEAD_DIM], fox_k_norm=dfkg[0, :HEAD_DIM],
        swa_q_norm=dsqg[0, :HEAD_DIM], swa_k_norm=dskg[0, :HEAD_DIM],
        swa_sinks=dsinkp[0, ::HEAD_DIM],
        fox_out_norm=dgfo.reshape(-1), swa_out_norm=dgso.reshape(-1),
    )
    return dh_in, dhb_in, dw_in, dw_out, small


_ANY = pl.BlockSpec(memory_space=pl.ANY)
_HALF_ROWS = (512, 352, 256, 192, 128, 64, 32, 16)


def _mesh_pos():
    return lax.axis_index("x"), lax.axis_index("y"), lax.axis_index("c")


def _other_chips(x, y):
    return [(1 - x, y), (x, 1 - y), (1 - x, 1 - y)]


def _rows_half(ref, which):
    rh = ref.shape[-2] // 2
    if len(ref.shape) == 2:
        return ref.at[pl.ds(which * rh, rh), :]
    return ref.at[:, pl.ds(which * rh, rh), :]


def _remote(src, dst, send_sems, recv_sems, idx, dev):
    return pltpu.make_async_remote_copy(src_ref=src, dst_ref=dst, send_sem=send_sems.at[idx], recv_sem=recv_sems.at[idx],
                                        device_id=dev, device_id_type=MESH)


def _cast_into_chunk(w, l_idx, k_idx, *, name):
    _, rows, cols = w.shape
    tr = _pick(rows, (512, 352, 256, 128, 64, 32, 16))

    def body(l_ref, k_ref, w_ref, o_ref):
        o_ref[...] = w_ref[...].astype(o_ref.dtype)

    return pl.pallas_call(
        body, name=name,
        grid_spec=pltpu.PrefetchScalarGridSpec(
            num_scalar_prefetch=2, grid=(rows // tr,),
            in_specs=[pl.BlockSpec((None, tr, cols), lambda i, l, k: (l[0], i, 0))],
            out_specs=pl.BlockSpec((None, tr, cols), lambda i, l, k: (k[0], i, 0))),
        out_shape=jax.ShapeDtypeStruct((NCH, rows, cols), BF16),
        compiler_params=_cparams("parallel"),
    )(l_idx, k_idx, w)


class _SiblingSwap:
    n_sems = 1

    def __init__(self, g):
        self.operands = [g]
        self.out_shapes = [jax.ShapeDtypeStruct((NCH, g.shape[1] // 2, g.shape[2]), g.dtype)]
        self.aliases = {}

    def copies(self, cin, cout, send_sems, recv_sems, base=0):
        x, y, c = _mesh_pos()
        cp = _remote(_rows_half(cin[0], 1 - c), cout[0], send_sems, recv_sems, base, (x, y, 1 - c))
        return [(cp, cp)]


class _ChipExchange:
    n_sems = NCH - 1

    def __init__(self, s, r0, r1, land=None):
        self.r0, self.r1 = r0, r1
        self.operands = [s] if land is None else [s, land]
        self.out_shapes = [jax.ShapeDtypeStruct((NCH - 1,) + s.shape[1:], s.dtype)]
        self.aliases = {} if land is None else {1: 0}

    def copies(self, cin, cout, send_sems, recv_sems, base=0):
        x, y, c = _mesh_pos()
        rows = pl.ds(self.r0, self.r1 - self.r0)
        cps = [_remote(cin[0].at[2 * cx + cy, rows, :], cout[0].at[j, rows, :], send_sems, recv_sems, base + j, (cx, cy, c))
               for j, (cx, cy) in enumerate(_other_chips(x, y))]
        return [(cp, cp) for cp in cps]


class _SiblingShare:
    n_sems = 1

    def __init__(self, tot):
        self.operands = [tot]
        self.out_shapes = [jax.ShapeDtypeStruct(tot.shape, tot.dtype)]
        self.aliases = {0: 0}

    def copies(self, cin, cout, send_sems, recv_sems, base=0):
        x, y, c = _mesh_pos()
        mine, theirs = _rows_half(cout[0], c), _rows_half(cout[0], 1 - c)
        return [(_remote(mine, mine, send_sems, recv_sems, base, (x, y, 1 - c)),
                 _remote(theirs, theirs, send_sems, recv_sems, base, (x, y, 1 - c)))]


class _Both:
    def __init__(self, a, b):
        self.a, self.b = a, b
        self.operands = a.operands + b.operands
        self.out_shapes = a.out_shapes + b.out_shapes
        self.aliases = dict(a.aliases)
        self.aliases.update({len(a.operands) + s: len(a.out_shapes) + d for s, d in b.aliases.items()})
        self.n_sems = a.n_sems + b.n_sems

    def copies(self, cin, cout, send_sems, recv_sems, base=0):
        na, nao = len(self.a.operands), len(self.a.out_shapes)
        return (self.a.copies(cin[:na], cout[:nao], send_sems, recv_sems, base)
                + self.b.copies(cin[na:], cout[nao:], send_sems, recv_sems, base + self.a.n_sems))


def _rows_quarter(ref, c, q):
    rq = ref.shape[0] // 4
    return ref.at[pl.ds((2 * c + q) * rq, rq), :]


class _GatherStep:
    n_sems = 7

    def __init__(self, stage1, stage2, stage3):
        given = [(st, b) for st, b in ((1, stage1), (2, stage2), (3, stage3)) if b is not None]
        self.stages = [st for st, _ in given]
        self.operands = [b for _, b in given]
        self.out_shapes = [jax.ShapeDtypeStruct(b.shape, b.dtype) for b in self.operands]
        self.aliases = {i: i for i in range(len(self.operands))}

    def copies(self, cin, cout, send_sems, recv_sems, base=0):
        x, y, c = _mesh_pos()
        k, kx, ky, kd = 2 * x + y, 2 * (1 - x) + y, 2 * x + (1 - y), 2 * (1 - x) + (1 - y)
        xn, yn, sibling = (1 - x, y, c), (x, 1 - y, c), (x, y, 1 - c)

        def pair(idx, dev, src, arrival):
            return (_remote(src, src, send_sems, recv_sems, base + idx, dev),
                    _remote(arrival, arrival, send_sems, recv_sems, base + idx, dev))

        out = []
        for stage, buf in zip(self.stages, cout):
            if stage == 1:
                mine = _rows_half(buf.at[k], c)
                out.append(pair(0, xn, mine, _rows_half(buf.at[kx], c)))
                out.append(pair(1, yn, mine, _rows_half(buf.at[ky], c)))
            elif stage == 2:
                out.append(pair(2, yn, _rows_quarter(buf.at[kx], c, 0), _rows_quarter(buf.at[kd], c, 0)))
                out.append(pair(3, xn, _rows_quarter(buf.at[ky], c, 1), _rows_quarter(buf.at[kd], c, 1)))
                out.append(pair(4, sibling, _rows_half(buf.at[kx], c), _rows_half(buf.at[kx], 1 - c)))
                out.append(pair(5, sibling, _rows_half(buf.at[ky], c), _rows_half(buf.at[ky], 1 - c)))
            else:
                out.append(pair(6, sibling, _rows_half(buf.at[kd], c), _rows_half(buf.at[kd], 1 - c)))
        return out


class _GatherQueue:
    def __init__(self, bufs):
        self.bufs = list(bufs)
        self.step = -3
        for _ in range(3):
            self.give(_run_exchange(self.take(), name="first_allgather"))

    def take(self):
        s = self.step
        self.step += 1
        at = [i if 0 <= i < len(self.bufs) else None for i in (s + 3, s + 2, s + 1)]
        self.cur = [i for i in at if i is not None]
        if not self.cur:
            return None
        return _GatherStep(*[None if i is None else self.bufs[i] for i in at])

    def give(self, outs):
        for i, buf in zip(self.cur, outs):
            self.bufs[i] = buf


def _run_exchange(job, *, name):
    n_in, n_out = len(job.operands), len(job.out_shapes)

    def body(*refs):
        cps = job.copies(refs[:n_in], refs[n_in:n_in + n_out], refs[-2], refs[-1])
        for cp, _ in cps:
            cp.start()
        for _, arrival in cps:
            arrival.wait_recv()
        for cp, _ in cps:
            cp.wait_send()

    return pl.pallas_call(
        body, name=name, in_specs=[_ANY] * n_in, out_specs=[_ANY] * n_out, out_shape=list(job.out_shapes),
        input_output_aliases=dict(job.aliases),
        scratch_shapes=[pltpu.SemaphoreType.DMA((job.n_sems,)), pltpu.SemaphoreType.DMA((job.n_sems,))],
    )(*job.operands)


class _ExchangeQueue:
    def __init__(self):
        self.lines = ([], [])
        self.cur = []

    def put(self, line, make_job, done):
        self.lines[line].append((make_job, done))

    def take(self):
        entries = [ln.pop(0) for ln in self.lines if ln]
        if not entries:
            return None
        jobs = [make() for make, _ in entries]
        self.cur = [(job, done) for job, (_, done) in zip(jobs, entries)]
        return jobs[0] if len(jobs) == 1 else _Both(*jobs)

    def give(self, outs):
        outs = list(outs)
        for job, done in self.cur:
            n = len(job.out_shapes)
            done(outs[:n])
            outs = outs[n:]

    def drain(self, name):
        while (job := self.take()) is not None:
            self.give(_run_exchange(job, name=name))


ICI_LINE, D2D_LINE = 0, 1


def _reduce_scatter_later(queue, parts, c_idx, k_idx, tag, done, n_parts=2):
    n = len(parts)
    st = dict(lands=[None] * n, sums=None, chip_lands=[None] * n, tots=[None] * n, left=n)

    def shared(i, outs):
        st['tots'][i] = outs[0]
        st['left'] -= 1
        if st['left'] == 0:
            done(st['tots'])

    def exchanged(i, outs):
        st['chip_lands'][i] = outs[0]
        st['left'] -= 1
        if st['left'] == 0:
            st['left'] = n
            for j in range(n):
                tot = _add_chunks(st['sums'][j], st['chip_lands'][j], k_idx, c_idx, name=f"{tag}_rs_add4")
                queue.put(D2D_LINE, lambda tot=tot: _SiblingShare(tot), lambda outs, j=j: shared(j, outs))

    def swapped(i, outs):
        st['lands'][i] = outs[0]
        st['left'] -= 1
        if st['left'] == 0:
            st['sums'] = [_add_own_half(p, l, c_idx, name=f"{tag}_rs_add2") for p, l in zip(parts, st['lands'])]
            st['left'] = n * n_parts
            for j in range(n):
                step = st['sums'][j].shape[1] // n_parts
                for p in range(n_parts):
                    queue.put(ICI_LINE,
                              lambda j=j, p=p, step=step: _ChipExchange(st['sums'][j], p * step, (p + 1) * step, st['chip_lands'][j]),
                              lambda outs, j=j: exchanged(j, outs))

    for i in range(n):
        queue.put(D2D_LINE, lambda i=i: _SiblingSwap(parts[i]), lambda outs, i=i: swapped(i, outs))


def _mm(queue, a, b, **kw):
    job = queue.take() if queue is not None else None
    if job is None:
        return _matmul(a, b, **kw)
    n_out = len(kw.get("out_dtypes") or (0,))
    outs = _matmul(a, b, comm=job, **kw)
    queue.give(outs[n_out:])
    return outs[0] if n_out == 1 else outs[:n_out]


def _allgather_devices(v, *, name):
    m = v.shape[0]

    def body(v_ref, out_ref, send_sems, recv_sems, local_sem):
        x, y, c = _mesh_pos()
        mine = 4 * x + 2 * y + c
        own = pltpu.make_async_copy(v_ref, out_ref.at[mine], local_sem)
        own.start()
        cps = []
        for r in range(1, 8):
            px, py, pc = (x + (r >> 2)) % 2, (y + ((r >> 1) & 1)) % 2, (c + (r & 1)) % 2
            cps.append((_remote(v_ref, out_ref.at[mine], send_sems, recv_sems, r - 1, (px, py, pc)), 4 * px + 2 * py + pc))
        for cp, _ in cps:
            cp.start()
        for r, (cp, theirs) in enumerate(cps):
            blk = out_ref.at[theirs]
            _remote(blk, blk, send_sems, recv_sems, r, (x, y, c)).wait_recv()
        for cp, _ in cps:
            cp.wait_send()
        own.wait()

    return pl.pallas_call(
        body, name=name, in_specs=[_ANY], out_specs=_ANY,
        out_shape=jax.ShapeDtypeStruct((8, m, LANES), v.dtype),
        scratch_shapes=[pltpu.SemaphoreType.DMA((7,)), pltpu.SemaphoreType.DMA((7,)), pltpu.SemaphoreType.DMA],
    )(v)


def _add_own_half(g, land, c_idx, *, name):
    nch, rh, cols = land.shape
    tr = _pick(rh, _HALF_ROWS)
    nt = rh // tr

    def body(c_ref, g_ref, l_ref, o_ref):
        o_ref[...] = (g_ref[...].astype(F32) + l_ref[...].astype(F32)).astype(o_ref.dtype)

    return pl.pallas_call(
        body, name=name,
        grid_spec=pltpu.PrefetchScalarGridSpec(
            num_scalar_prefetch=1, grid=(nch, nt),
            in_specs=[pl.BlockSpec((None, tr, cols), lambda k, i, c: (k, c[0] * nt + i, 0)),
                      pl.BlockSpec((None, tr, cols), lambda k, i, c: (k, i, 0))],
            out_specs=pl.BlockSpec((None, tr, cols), lambda k, i, c: (k, i, 0))),
        out_shape=jax.ShapeDtypeStruct(land.shape, BF16),
        compiler_params=_cparams("parallel", "parallel"),
    )(c_idx, g, land)


def _add_chunks(s, land, k_idx, c_idx, *, name):
    _, rh, cols = s.shape
    tr = _pick(rh, _HALF_ROWS)
    nt = rh // tr

    def body(k_ref, c_ref, s_ref, l_ref, o_ref):
        t = s_ref[...].astype(F32)
        for j in range(NCH - 1):
            t = t + l_ref[j].astype(F32)
        o_ref[...] = t

    return pl.pallas_call(
        body, name=name,
        grid_spec=pltpu.PrefetchScalarGridSpec(
            num_scalar_prefetch=2, grid=(nt,),
            in_specs=[pl.BlockSpec((None, tr, cols), lambda i, k, c: (k[0], i, 0)),
                      pl.BlockSpec((NCH - 1, tr, cols), lambda i, k, c: (0, i, 0))],
            out_specs=pl.BlockSpec((tr, cols), lambda i, k, c: (c[0] * nt + i, 0))),
        out_shape=jax.ShapeDtypeStruct((2 * rh, cols), F32),
        compiler_params=_cparams("parallel"),
    )(k_idx, c_idx, s, land)


def _sum_devices(v, *, name):
    _, m, _ = v.shape

    def body(v_ref, o_ref):
        t = v_ref[0]
        for d in range(1, 8):
            t = t + v_ref[d]
        o_ref[...] = t

    return pl.pallas_call(
        body, name=name, grid=(1,),
        in_specs=[pl.BlockSpec((8, m, LANES), lambda i: (0, 0, 0))],
        out_specs=pl.BlockSpec((m, LANES), lambda i: (0, 0)),
        out_shape=jax.ShapeDtypeStruct((m, LANES), F32),
        compiler_params=_cparams("arbitrary"),
    )(v)


def _adamw_math(w, g, m, v):
    m = ADAM_B1 * m + (1.0 - ADAM_B1) * g
    v = ADAM_B2 * v + (1.0 - ADAM_B2) * (g * g)
    m_hat = m / (1.0 - ADAM_B1 ** ADAM_STEP)
    v_hat = v / (1.0 - ADAM_B2 ** ADAM_STEP)
    delta = -ADAM_LR * (m_hat / (jnp.sqrt(v_hat) + ADAM_EPS) + ADAM_WD * w)
    return delta, m, v


def _adamw_layer(w, m, v, g, layer, prev, *, name):
    depth, rows, cols = w.shape
    tr = _pick(rows, (256, 128, 64, 32, 16, 8))
    lay = pl.BlockSpec((None, tr, cols), lambda i, l: (l[0], i, 0))
    n_prev = 0 if prev is None else 4

    def body(l_ref, w_ref, m_ref, v_ref, g_ref, *rest):
        go_ref, d_ref, mo_ref, vo_ref = rest[n_prev:]
        g = g_ref[...]
        delta, m_new, v_new = _adamw_math(w_ref[...], g, m_ref[...], v_ref[...])
        go_ref[...] = g
        d_ref[...] = delta
        mo_ref[...] = m_new
        vo_ref[...] = v_new

    stack = jax.ShapeDtypeStruct(w.shape, F32)
    return pl.pallas_call(
        body, name=name,
        grid_spec=pltpu.PrefetchScalarGridSpec(
            num_scalar_prefetch=1, grid=(rows // tr,),
            in_specs=[lay, lay, lay, pl.BlockSpec((tr, cols), lambda i, l: (i, 0))] + [_ANY] * n_prev,
            out_specs=[lay] * 4),
        out_shape=[stack] * 4,
        input_output_aliases={} if prev is None else {5 + q: q for q in range(4)},
        compiler_params=_cparams("parallel"),
    )(layer, w, m, v, g, *(() if prev is None else prev))


def _adamw_flat(w, g, m, v, *, name):
    def body(w_ref, g_ref, m_ref, v_ref, d_ref, mo_ref, vo_ref):
        d_ref[...], mo_ref[...], vo_ref[...] = _adamw_math(w_ref[...], g_ref[...], m_ref[...], v_ref[...])

    blk = pl.BlockSpec(w.shape, lambda i: (0, 0))
    return pl.pallas_call(
        body, name=name, grid=(1,), in_specs=[blk] * 4, out_specs=[blk] * 3,
        out_shape=[jax.ShapeDtypeStruct(w.shape, F32)] * 3, compiler_params=_cparams("arbitrary"),
    )(w, g, m, v)


_WEIGHTS = ('meta_tokens', 'ffn1_norm', 'ffn1_w_gate', 'ffn1_w_up', 'ffn1_w_down', 'mix_norm', 'w_in', 'b_forget',
            'fox_q_norm', 'fox_k_norm', 'swa_q_norm', 'swa_k_norm', 'swa_sinks', 'fox_out_norm', 'swa_out_norm', 'w_out',
            'ffn2_norm', 'ffn2_w_gate', 'ffn2_w_up', 'ffn2_w_down')
_BIG = ('ffn1_w_gate', 'ffn1_w_up', 'ffn1_w_down', 'w_in', 'w_out', 'ffn2_w_gate', 'ffn2_w_up', 'ffn2_w_down')
_SMALL = tuple(n for n in _WEIGHTS if n not in _BIG and n != 'meta_tokens')
_MIX_SMALL = ('mix_norm', 'b_forget', 'fox_q_norm', 'fox_k_norm', 'swa_q_norm', 'swa_k_norm', 'swa_sinks',
              'fox_out_norm', 'swa_out_norm')


def _pack_rows(vectors):
    flat = jnp.concatenate([v.reshape(-1) for v in vectors])
    n = flat.shape[0]
    m = -(-n // (8 * LANES)) * 8
    return jnp.pad(flat, (0, m * LANES - n)).reshape(m, LANES)


def _unpack_rows(packed, shapes):
    flat = packed.reshape(-1)
    out, o = [], 0
    for s in shapes:
        n = int(np.prod(s))
        out.append(flat[o:o + n].reshape(s))
        o += n
    return out


def kernel(x, meta_tokens, ffn1_norm, ffn1_w_gate, ffn1_w_up, ffn1_w_down, mix_norm, w_in, b_forget, fox_q_norm, fox_k_norm, swa_q_norm, swa_k_norm, swa_sinks, fox_out_norm, swa_out_norm, w_out, ffn2_norm, ffn2_w_gate, ffn2_w_up, ffn2_w_down, loss_target, m_meta_tokens, m_ffn1_norm, m_ffn1_w_gate, m_ffn1_w_up, m_ffn1_w_down, m_mix_norm, m_w_in, m_b_forget, m_fox_q_norm, m_fox_k_norm, m_swa_q_norm, m_swa_k_norm, m_swa_sinks, m_fox_out_norm, m_swa_out_norm, m_w_out, m_ffn2_norm, m_ffn2_w_gate, m_ffn2_w_up, m_ffn2_w_down, v_meta_tokens, v_ffn1_norm, v_ffn1_w_gate, v_ffn1_w_up, v_ffn1_w_down, v_mix_norm, v_w_in, v_b_forget, v_fox_q_norm, v_fox_k_norm, v_swa_q_norm, v_swa_k_norm, v_swa_sinks, v_fox_out_norm, v_swa_out_norm, v_w_out, v_ffn2_norm, v_ffn2_w_gate, v_ffn2_w_up, v_ffn2_w_down):
    W = dict(meta_tokens=meta_tokens, ffn1_norm=ffn1_norm, ffn1_w_gate=ffn1_w_gate, ffn1_w_up=ffn1_w_up, ffn1_w_down=ffn1_w_down, mix_norm=mix_norm, w_in=w_in, b_forget=b_forget, fox_q_norm=fox_q_norm, fox_k_norm=fox_k_norm, swa_q_norm=swa_q_norm, swa_k_norm=swa_k_norm, swa_sinks=swa_sinks, fox_out_norm=fox_out_norm, swa_out_norm=swa_out_norm, w_out=w_out, ffn2_norm=ffn2_norm, ffn2_w_gate=ffn2_w_gate, ffn2_w_up=ffn2_w_up, ffn2_w_down=ffn2_w_down)
    Mo = dict(meta_tokens=m_meta_tokens, ffn1_norm=m_ffn1_norm, ffn1_w_gate=m_ffn1_w_gate, ffn1_w_up=m_ffn1_w_up, ffn1_w_down=m_ffn1_w_down, mix_norm=m_mix_norm, w_in=m_w_in, b_forget=m_b_forget, fox_q_norm=m_fox_q_norm, fox_k_norm=m_fox_k_norm, swa_q_norm=m_swa_q_norm, swa_k_norm=m_swa_k_norm, swa_sinks=m_swa_sinks, fox_out_norm=m_fox_out_norm, swa_out_norm=m_swa_out_norm, w_out=m_w_out, ffn2_norm=m_ffn2_norm, ffn2_w_gate=m_ffn2_w_gate, ffn2_w_up=m_ffn2_w_up, ffn2_w_down=m_ffn2_w_down)
    Vo = dict(meta_tokens=v_meta_tokens, ffn1_norm=v_ffn1_norm, ffn1_w_gate=v_ffn1_w_gate, ffn1_w_up=v_ffn1_w_up, ffn1_w_down=v_ffn1_w_down, mix_norm=v_mix_norm, w_in=v_w_in, b_forget=v_b_forget, fox_q_norm=v_fox_q_norm, fox_k_norm=v_fox_k_norm, swa_q_norm=v_swa_q_norm, swa_k_norm=v_swa_k_norm, swa_sinks=v_swa_sinks, fox_out_norm=v_fox_out_norm, swa_out_norm=v_swa_out_norm, w_out=v_w_out, ffn2_norm=v_ffn2_norm, ffn2_w_gate=v_ffn2_w_gate, ffn2_w_up=v_ffn2_w_up, ffn2_w_down=v_ffn2_w_down)

    _, S, D = x.shape
    L = S + BLOCK
    depth = ffn1_norm.shape[0]
    md = _MixDims(D)
    mx, my, mc = _mesh_pos()
    k_idx = (2 * mx + my).astype(jnp.int32).reshape(1)
    c_idx = mc.astype(jnp.int32).reshape(1)
    dcols = D // NCH

    meta_all = _allgather_devices(meta_tokens.reshape(-1, LANES), name="meta_allgather")
    meta_full = jnp.transpose(meta_all[0::2].reshape(NCH, N_META, dcols), (1, 0, 2)).reshape(N_META, D)

    order = ('ffn1_w_gate', 'ffn1_w_up', 'ffn1_w_down', 'w_in', 'w_out', 'ffn2_w_gate', 'ffn2_w_up', 'ffn2_w_down')
    chunks = []
    for l in range(depth):
        l_idx = jnp.full((1,), l, jnp.int32)
        chunks += [_cast_into_chunk(W[name], l_idx, k_idx, name="cast_chunk") for name in order]
    gather = _GatherQueue(chunks)
    wts = [{} for _ in range(depth)]

    def weight(l, i):
        if i not in wts[l]:
            buf = gather.bufs[len(order) * l + i]
            if order[i] == 'w_in':
                buf = _win_to_mine(jnp.transpose(buf, (1, 0, 2)).reshape(D, NCH * buf.shape[2]), md)
            elif order[i] in ('w_out', 'ffn1_w_down', 'ffn2_w_down'):
                buf = buf.reshape(-1, D)
            wts[l][i] = buf
        return wts[l][i]

    h = jnp.concatenate([jnp.zeros((PAD, D), F32), meta_full, x[0]], axis=0)
    saved = []
    for l in range(depth):
        sp = _mix_small({n: W[n][l] for n in _MIX_SMALL}, md)
        h, s1 = _ffn_fwd(h, ffn1_norm[l].reshape(1, D), lambda i: weight(l, i), "ffn", gather)
        h, s2 = _mix_fwd(h, sp, lambda i: weight(l, 3 + i), md, gather)
        h, s3 = _ffn_fwd(h, ffn2_norm[l].reshape(1, D), lambda i: weight(l, 5 + i), "ffn", gather)
        saved.append((s1, s2, s3, sp))
    wts = [dict(g1=w[0], u1=w[1], d1=w[2], wi=w[3], wo=w[4], g2=w[5], u2=w[6], d2=w[7]) for w in wts]

    loss_part, dh, dhb = _loss_grad(h, loss_target[0], name="loss_grad")

    small_grads = {n: [None] * depth for n in _SMALL}
    stacks = {n: None for n in _BIG}

    def update(name, l, grad):
        stacks[name] = _adamw_layer(W[name], Mo[name], Vo[name], grad, jnp.full((1,), l, jnp.int32), stacks[name],
                                    name="adamw_layer")

    queue = _ExchangeQueue()

    def reduce_later(names, l, parts, tag):
        def done(grads):
            for n, grad in zip(names, grads):
                update(n, l, grad)
        _reduce_scatter_later(queue, parts, c_idx, k_idx, tag, done)

    def ffn_dw(names, l):
        return lambda i, dw: reduce_later((names[i],), l, [dw.reshape(NCH, -1, D) if i == 2 else dw], "ffn")

    for l in range(depth - 1, -1, -1):
        wl = wts[l]
        s1, s2, s3, sp = saved[l]
        dh, dhb, dg = _ffn_bwd(dh, dhb, s3, ffn2_norm[l].reshape(1, D), wl['g2'], wl['u2'], wl['d2'], "ffn", queue,
                               ffn_dw(('ffn2_w_gate', 'ffn2_w_up', 'ffn2_w_down'), l))
        small_grads['ffn2_norm'][l] = dg.reshape(-1)

        dh, dhb, dwi, dwo, sm = _mix_bwd(dh, dhb, s2, sp, wl['wi'], wl['wo'], md, queue)
        for n in _MIX_SMALL:
            small_grads[n][l] = sm[n]
        dwi = _win_grad_to_ref(dwi, md)
        dwi = jnp.transpose(dwi.reshape(D, NCH, -1), (1, 0, 2))
        reduce_later(('w_in', 'w_out'), l, [dwi, dwo.reshape(NCH, -1, D)], "mix")

        dh, dhb, dg = _ffn_bwd(dh, dhb, s1, ffn1_norm[l].reshape(1, D), wl['g1'], wl['u1'], wl['d1'], "ffn", queue,
                               ffn_dw(('ffn1_w_gate', 'ffn1_w_up', 'ffn1_w_down'), l))
        small_grads['ffn1_norm'][l] = dg.reshape(-1)
    queue.drain("rs_chips")

    grad_x = dh[BLOCK:][None]

    small_shapes = [W[n].shape for n in _SMALL]
    parts = [jnp.stack(small_grads[n]) for n in _SMALL] + [dh[PAD:BLOCK], loss_part[0, :1]]
    packed = _pack_rows(parts)
    total = _sum_devices(_allgather_devices(packed, name="small_allgather"), name="small_sum")
    *g_small, g_meta, loss = _unpack_rows(total, small_shapes + [(N_META, D), (1,)])
    g_meta = lax.dynamic_slice(g_meta, (0, k_idx[0] * dcols), (N_META, dcols))

    sw = _pack_rows([W[n] for n in _SMALL])
    sd, smm, svv = _adamw_flat(sw, _pack_rows(g_small), _pack_rows([Mo[n] for n in _SMALL]),
                               _pack_rows([Vo[n] for n in _SMALL]), name="adamw_small")
    d_small, m_small, v_small = (_unpack_rows(t, small_shapes) for t in (sd, smm, svv))
    d_meta, m_meta, v_meta = _adamw_flat(meta_tokens, g_meta, m_meta_tokens, v_meta_tokens, name="adamw_meta")

    grads, deltas, new_m, new_v = {}, {}, {}, {}
    for n in _BIG:
        grads[n], deltas[n], new_m[n], new_v[n] = stacks[n]
    for i, n in enumerate(_SMALL):
        grads[n], deltas[n], new_m[n], new_v[n] = g_small[i], d_small[i], m_small[i], v_small[i]
    grads['meta_tokens'], deltas['meta_tokens'], new_m['meta_tokens'], new_v['meta_tokens'] = g_meta, d_meta, m_meta, v_meta
    return (loss.reshape(()), grad_x, *[grads[n] for n in _WEIGHTS], *[deltas[n] for n in _WEIGHTS],
            *[new_m[n] for n in _WEIGHTS], *[new_v[n] for n in _WEIGHTS])
```

```python
import numpy as np
import jax
import jax.numpy as jnp
from jax import lax
from jax.experimental import pallas as pl
from jax.experimental.pallas import tpu as pltpu

F32 = jnp.float32
BF16 = jnp.bfloat16

HEAD_DIM = 64
N_META = 16
BLOCK = 128
WINDOW = 128
PAD = BLOCK - N_META
EPS = 1e-6
NEG_INF = -1e30
SWA_GROUP = 8
NCH = 4
LANES = 128
QBLOCK = 512

ADAM_LR = 0.001
ADAM_B1 = 0.9
ADAM_B2 = 0.999
ADAM_EPS = 1e-08
ADAM_WD = 0.01
ADAM_STEP = 10

V7X_VMEM_BYTES = 64 * 1024 * 1024
VMEM_LIMIT = V7X_VMEM_BYTES - 8 * 1024 * 1024
MESH = pl.DeviceIdType.MESH
HIGHEST = lax.Precision.HIGHEST

_TM = (1088, 1024, 704, 512, 384, 256, 128)
_TN = (1408, 1024, 768, 512, 384, 256, 128)
_TK = (2176, 1408, 1024, 512, 384, 256, 128)
_TR = (544, 512, 384, 272, 256, 128)


def _pick(n, cands):
    for c in cands:
        if n % c == 0:
            return c
    return n


def _cparams(*sem):
    return pltpu.CompilerParams(dimension_semantics=sem if sem else None, vmem_limit_bytes=VMEM_LIMIT)


def _matmul(a, b, *, name, nt=False, b_chunked=False, out_chunked=False, out_dtype=F32,
            residual=None, scale=1.0, extras=(), epilogue=None, out_dtypes=None, comm=None, ta=False):
    K, M = a.shape if ta else a.shape[::-1]
    if not nt:
        N = b.shape[-1] * (NCH if b_chunked else 1)
        assert b.shape[-2] == K
        k_unit = K
    else:
        N = b.shape[-2]
        k_unit = b.shape[-1]
        assert k_unit * (NCH if b_chunked else 1) == K
    n_unit = N // NCH if (out_chunked or (b_chunked and not nt)) else N
    tm, tn, tk = _pick(M, _TN if ta else _TM), _pick(n_unit, _TN), _pick(k_unit, _TK)
    if epilogue is None:
        extras = () if residual is None else (residual,)
        out_dtypes = (out_dtype,)

        def epilogue(acc, *res):
            r = acc * scale if scale != 1.0 else acc
            return (r + res[0] if res else r,)
    n_out = len(out_dtypes)
    n_temps = 4 if n_out > 1 else 0
    assert not (extras and out_chunked)

    def est(tm_):
        return (2 * tm_ * tk * 2 + 2 * tk * tn * 2 + tm_ * tn * 4
                + sum(2 * tm_ * tn * jnp.dtype(d).itemsize for d in out_dtypes)
                + sum(2 * tm_ * tn * e.dtype.itemsize for e in extras) + n_temps * tm_ * tn * 4)

    while est(tm) > VMEM_LIMIT * 3 // 4 and tm % 32 == 0:
        tm //= 2
    npc, kpc = n_unit // tn, k_unit // tk
    nk = K // tk
    grid = (M // tm, N // tn, nk)

    if ta:
        assert not nt
        a_spec = pl.BlockSpec((tk, tm), lambda i, j, k: (k, i))
    else:
        a_spec = pl.BlockSpec((tm, tk), lambda i, j, k: (i, k))
    if not nt:
        if b_chunked:
            b_spec = pl.BlockSpec((None, tk, tn), lambda i, j, k: (j // npc, k, j % npc))
        else:
            b_spec = pl.BlockSpec((tk, tn), lambda i, j, k: (k, j))
        dims = (((0 if ta else 1,), (0,)), ((), ()))
    else:
        if b_chunked:
            b_spec = pl.BlockSpec((None, tn, tk), lambda i, j, k: (k // kpc, j, k % kpc))
        else:
            b_spec = pl.BlockSpec((tn, tk), lambda i, j, k: (j, k))
        dims = (((1,), (1,)), ((), ()))
    tile = pl.BlockSpec((tm, tn), lambda i, j, k: (i, j))
    if out_chunked:
        o_spec = pl.BlockSpec((None, tm, tn), lambda i, j, k: (j // npc, i, j % npc))
        out_shapes = [jax.ShapeDtypeStruct((NCH, M, n_unit), d) for d in out_dtypes]
    else:
        o_spec = tile
        out_shapes = [jax.ShapeDtypeStruct((M, N), d) for d in out_dtypes]
    in_specs = [a_spec, b_spec] + [tile] * len(extras)
    args = [a, b, *extras]

    n_main = len(args)
    n_cin = 0 if comm is None else len(comm.operands)
    n_cout = 0 if comm is None else len(comm.out_shapes)

    def body(*refs):
        a_ref, b_ref = refs[0], refs[1]
        e_refs = refs[2:n_main]
        o_refs = refs[n_main + n_cin:n_main + n_cin + n_out]
        acc_ref = refs[n_main + n_cin + n_out + n_cout]
        i, j, k = pl.program_id(0), pl.program_id(1), pl.program_id(2)
        if comm is not None:
            cin = refs[n_main:n_main + n_cin]
            cout = refs[n_main + n_cin + n_out:n_main + n_cin + n_out + n_cout]
            send_sems, recv_sems = refs[-2:]

            @pl.when((i == 0) & (j == 0) & (k == 0))
            def _():
                for cp, _ in comm.copies(cin, cout, send_sems, recv_sems):
                    cp.start()

        @pl.when(k == 0)
        def _():
            acc_ref[...] = jnp.zeros_like(acc_ref)

        acc_ref[...] += lax.dot_general(a_ref[...], b_ref[...], dims, preferred_element_type=F32)

        @pl.when(k == nk - 1)
        def _():
            for o_ref, val in zip(o_refs, epilogue(acc_ref[...], *[e[...] for e in e_refs])):
                o_ref[...] = val.astype(o_ref.dtype)

        if comm is not None:
            @pl.when((i == grid[0] - 1) & (j == grid[1] - 1) & (k == nk - 1))
            def _():
                cps = comm.copies(cin, cout, send_sems, recv_sems)
                for _, arrival in cps:
                    arrival.wait_recv()
                for cp, _ in cps:
                    cp.wait_send()

    scratch = [pltpu.VMEM((tm, tn), F32)]
    if comm is None:
        outs = pl.pallas_call(
            body, name=name, grid=grid, in_specs=in_specs, out_specs=[o_spec] * n_out, out_shape=out_shapes,
            scratch_shapes=scratch, compiler_params=_cparams("parallel", "parallel", "arbitrary"),
        )(*args)
    else:
        scratch += [pltpu.SemaphoreType.DMA((comm.n_sems,)), pltpu.SemaphoreType.DMA((comm.n_sems,))]
        outs = pl.pallas_call(
            body, name=name, grid=grid, in_specs=in_specs + [_ANY] * n_cin,
            out_specs=[o_spec] * n_out + [_ANY] * n_cout, out_shape=out_shapes + list(comm.out_shapes),
            input_output_aliases={n_main + s: n_out + d for s, d in comm.aliases.items()},
            scratch_shapes=scratch, compiler_params=_cparams("arbitrary", "arbitrary", "arbitrary"),
        )(*args, *comm.operands)
    return outs[0] if len(outs) == 1 else tuple(outs)


def _call_carrying(body, comm, args, *, name, grid, in_specs, out_specs, out_shape, scratch_shapes):
    if comm is None:
        return pl.pallas_call(body, name=name, grid=grid, in_specs=in_specs, out_specs=out_specs, out_shape=out_shape,
                              scratch_shapes=scratch_shapes, compiler_params=_cparams("arbitrary"))(*args)
    n_in, n_out, n_scr = len(in_specs), len(out_specs), len(scratch_shapes)
    n_cin, n_cout = len(comm.operands), len(comm.out_shapes)

    def carrying(*refs):
        ins, cin = refs[:n_in], refs[n_in:n_in + n_cin]
        outs = refs[n_in + n_cin:n_in + n_cin + n_out]
        cout = refs[n_in + n_cin + n_out:n_in + n_cin + n_out + n_cout]
        scr = refs[n_in + n_cin + n_out + n_cout:n_in + n_cin + n_out + n_cout + n_scr]
        send_sems, recv_sems = refs[-2:]
        step = pl.program_id(0)

        @pl.when(step == 0)
        def _():
            for cp, _ in comm.copies(cin, cout, send_sems, recv_sems):
                cp.start()

        body(*ins, *outs, *scr)

        @pl.when(step == grid[0] - 1)
        def _():
            cps = comm.copies(cin, cout, send_sems, recv_sems)
            for _, arrival in cps:
                arrival.wait_recv()
            for cp, _ in cps:
                cp.wait_send()

    return pl.pallas_call(
        carrying, name=name, grid=grid, in_specs=list(in_specs) + [_ANY] * n_cin,
        out_specs=list(out_specs) + [_ANY] * n_cout, out_shape=list(out_shape) + list(comm.out_shapes),
        input_output_aliases={n_in + s: n_out + d for s, d in comm.aliases.items()},
        scratch_shapes=list(scratch_shapes) + [pltpu.SemaphoreType.DMA((comm.n_sems,)), pltpu.SemaphoreType.DMA((comm.n_sems,))],
        compiler_params=_cparams("arbitrary"),
    )(*args, *comm.operands)


def _carried(queue, fn, n_out, *args, **kw):
    job = queue.take() if queue is not None else None
    outs = fn(*args, comm=job, **kw)
    if job is not None:
        queue.give(outs[n_out:])
    return outs[:n_out]


def _transpose(x, *, name):
    M, N = x.shape
    tc = _pick(N, (512, 384, 256, 128))

    def body(x_ref, o_ref):
        o_ref[...] = x_ref[...].astype(F32).T.astype(o_ref.dtype)

    return pl.pallas_call(
        body, name=name, grid=(N // tc,),
        in_specs=[pl.BlockSpec((M, tc), lambda j: (0, j))],
        out_specs=pl.BlockSpec((tc, M), lambda j: (j, 0)),
        out_shape=jax.ShapeDtypeStruct((N, M), x.dtype),
        compiler_params=_cparams("parallel"),
    )(x)


def _rms_fwd(h, g, *, name):
    L, D = h.shape
    tr = _pick(L, _TR)

    def body(h_ref, g_ref, o_ref):
        x = h_ref[...]
        r = lax.rsqrt(jnp.mean(x * x, axis=-1, keepdims=True) + EPS)
        o_ref[...] = (x * r * g_ref[...]).astype(o_ref.dtype)

    return pl.pallas_call(
        body, name=name, grid=(L // tr,),
        in_specs=[pl.BlockSpec((tr, D), lambda i: (i, 0)), pl.BlockSpec((1, D), lambda i: (0, 0))],
        out_specs=pl.BlockSpec((tr, D), lambda i: (i, 0)),
        out_shape=jax.ShapeDtypeStruct((L, D), BF16),
        compiler_params=_cparams("parallel"),
    )(h, g)


def _rms_bwd(dy, h, g, dh, *, name):
    L, D = h.shape
    tr = _pick(L, _TR)

    def body(dy_ref, h_ref, g_ref, dh_ref, o_ref, ob_ref, dg_ref):
        i = pl.program_id(0)
        x = h_ref[...]
        dyv = dy_ref[...]
        r = lax.rsqrt(jnp.mean(x * x, axis=-1, keepdims=True) + EPS)
        xh = x * r
        dxh = dyv * g_ref[...]
        dx = r * (dxh - xh * jnp.mean(dxh * xh, axis=-1, keepdims=True))
        total = dh_ref[...] + dx
        o_ref[...] = total
        ob_ref[...] = total.astype(ob_ref.dtype)
        part = jnp.sum(dyv * xh, axis=0, keepdims=True)

        @pl.when(i == 0)
        def _():
            dg_ref[...] = part

        @pl.when(i > 0)
        def _():
            dg_ref[...] += part

    row = pl.BlockSpec((tr, D), lambda i: (i, 0))
    vec = pl.BlockSpec((1, D), lambda i: (0, 0))
    return pl.pallas_call(
        body, name=name, grid=(L // tr,),
        in_specs=[row, row, vec, row], out_specs=[row, row, vec],
        out_shape=[jax.ShapeDtypeStruct((L, D), F32), jax.ShapeDtypeStruct((L, D), BF16), jax.ShapeDtypeStruct((1, D), F32)],
        compiler_params=_cparams("arbitrary"),
    )(dy, h, g, dh)


def _swiglu_epilogue(up, gate):
    g = gate.astype(F32)
    return up, g * jax.nn.sigmoid(g) * up


def _swiglu_bwd_epilogue(acc, gate, up):
    d = 0.5 * acc
    g = gate.astype(F32)
    sg = jax.nn.sigmoid(g)
    return d * up.astype(F32) * sg * (1.0 + g * (1.0 - sg)), d * g * sg


def _loss_grad(h, target, *, name):
    L, D = h.shape
    S = target.shape[0]
    nb = L // BLOCK

    def body(h_ref, t_ref, loss_ref, dh_ref, dhb_ref):
        i = pl.program_id(0)

        @pl.when(i == 0)
        def _():
            loss_ref[...] = jnp.zeros_like(loss_ref)
            dh_ref[...] = jnp.zeros_like(dh_ref)
            dhb_ref[...] = jnp.zeros_like(dhb_ref)

        @pl.when(i > 0)
        def _():
            err = h_ref[...] - t_ref[...]
            d = err * (1.0 / D)
            dh_ref[...] = d
            dhb_ref[...] = d.astype(dhb_ref.dtype)
            loss_ref[...] += jnp.full(loss_ref.shape, (0.5 / D) * jnp.sum(err * err), F32)

    row = pl.BlockSpec((BLOCK, D), lambda i: (i, 0))
    return pl.pallas_call(
        body, name=name, grid=(nb,),
        in_specs=[row, pl.BlockSpec((BLOCK, D), lambda i: (jnp.maximum(i - 1, 0), 0))],
        out_specs=[pl.BlockSpec((1, LANES), lambda i: (0, 0)), row, row],
        out_shape=[jax.ShapeDtypeStruct((1, LANES), F32), jax.ShapeDtypeStruct((L, D), F32),
                   jax.ShapeDtypeStruct((L, D), BF16)],
        compiler_params=_cparams("arbitrary"),
    )(h, target)


def _ffn_fwd(h, g, w, tag, queue=None):
    hn = _rms_fwd(h, g, name=f"{tag}_rms")
    gate = _mm(queue, hn, w(0), name=f"{tag}_gate", b_chunked=True, out_dtype=BF16)
    up, act = _mm(queue, hn, w(1), name=f"{tag}_up", b_chunked=True, extras=(gate,), epilogue=_swiglu_epilogue,
                  out_dtypes=(BF16, BF16))
    h_out = _mm(queue, act, w(2), name=f"{tag}_down", residual=h, scale=0.5)
    return h_out, (h, hn, gate, up, act)


def _ffn_bwd(dh, dout, saved, g, wg, wu, wd, tag, queue=None, on_dw=None):
    h, hn, gate, up, act = saved
    dgate, dup = _mm(queue, dout, wd, name=f"{tag}_dact", nt=True, extras=(gate, up), epilogue=_swiglu_bwd_epilogue,
                     out_dtypes=(BF16, BF16))
    on_dw(2, _mm(queue, act, dout, name=f"{tag}_dwd", ta=True, out_dtype=BF16, scale=0.5))
    on_dw(0, _mm(queue, hn, dgate, name=f"{tag}_dwg", ta=True, out_chunked=True, out_dtype=BF16))
    on_dw(1, _mm(queue, hn, dup, name=f"{tag}_dwu", ta=True, out_chunked=True, out_dtype=BF16))
    dhn = _mm(queue, dgate, wg, name=f"{tag}_dhn_g", nt=True, b_chunked=True)
    dhn = _mm(queue, dup, wu, name=f"{tag}_dhn_u", nt=True, b_chunked=True, residual=dhn)
    return _rms_bwd(dhn, h, g, dh, name=f"{tag}_drms")


class _MixDims:
    def __init__(self, d_model):
        self.wf = d_model // 2
        self.ws = d_model // 2
        self.pf = self.wf // LANES
        self.ps = self.ws // LANES
        self.hf = self.wf // HEAD_DIM
        self.hq = self.ws // HEAD_DIM
        self.nkv = max(1, self.hq // SWA_GROUP)
        self.g = self.hq // self.nkv
        self.bq_f, self.bk_f, self.bv_f = 0, self.pf, 2 * self.pf
        self.bq_s = 3 * self.pf
        self.bk_s = self.bq_s + self.ps
        self.bv_s = self.bk_s + self.nkv
        self.bz = self.bv_s + self.nkv
        self.nu = (self.bz + 1) * LANES
        self.nup = -(-self.nu // 512) * 512
        self.in_width = 3 * self.wf + self.hf + self.ws + 2 * self.nkv * HEAD_DIM
        assert self.hf <= 2 * (LANES // 8)

    def gate_lane(self, h):
        return 8 * (h // 2) + h % 2

    def column_map(self):
        wf, ws, hd = self.wf, self.ws, HEAD_DIM
        src = np.full((self.nup,), -1, np.int64)
        src[0:3 * wf] = np.arange(3 * wf)
        o_sq = 3 * wf + self.hf
        src[self.bq_s * LANES:self.bq_s * LANES + ws] = o_sq + np.arange(ws)
        o_sk = o_sq + ws
        o_sv = o_sk + self.nkv * hd
        for kv in range(self.nkv):
            for rep in range(2):
                c0 = (self.bk_s + kv) * LANES + rep * hd
                src[c0:c0 + hd] = o_sk + kv * hd + np.arange(hd)
                c0 = (self.bv_s + kv) * LANES + rep * hd
                src[c0:c0 + hd] = o_sv + kv * hd + np.arange(hd)
        for h in range(self.hf):
            src[self.bz * LANES + self.gate_lane(h)] = 3 * wf + h
        return src

    def grad_column_map(self):
        src = self.column_map()
        dst = np.zeros((self.in_width,), np.int64)
        for col in range(self.nup - 1, -1, -1):
            if src[col] >= 0:
                dst[src[col]] = col
        return dst


def _block_diag_mean():
    m = np.zeros((LANES, LANES), np.float32)
    m[:HEAD_DIM, :HEAD_DIM] = 1.0 / HEAD_DIM
    m[HEAD_DIM:, HEAD_DIM:] = 1.0 / HEAD_DIM
    return jnp.asarray(m)


def _fold_halves():
    m = np.eye(LANES, dtype=np.float32)
    m[np.arange(LANES), (np.arange(LANES) + HEAD_DIM) % LANES] = 1.0
    return jnp.asarray(m)


def _gate_expand(md):
    e = np.zeros((LANES, md.wf), np.float32)
    for h in range(md.hf):
        e[md.gate_lane(h), h * HEAD_DIM:(h + 1) * HEAD_DIM] = 1.0
    return jnp.asarray(e)


def _qblocks(L, qb):
    blocks = [(0, BLOCK)]
    r = BLOCK
    while r < L:
        blocks.append((r, qb))
        r += qb
    assert r == L
    return blocks


def _f32dot(a, b):
    return jnp.dot(a, b, precision=HIGHEST, preferred_element_type=F32)


_DIMS_NT = (((1,), (1,)), ((), ()))


def _dot_nt(a, b):
    return lax.dot_general(a, b, _DIMS_NT, preferred_element_type=F32)


def _dot_tn(a, b):
    return jnp.dot(a.T.astype(BF16), b, preferred_element_type=F32)


def _log_sigmoid(z):
    return jnp.minimum(z, 0.0) - jnp.log(1.0 + jnp.exp(-jnp.abs(z)))


def _gate_fwd(u, b, md, *, name):
    L = u.shape[0]
    nb = L // BLOCK
    expand = _gate_expand(md)

    def body(z_ref, b_ref, e_ref, cexp_ref, ct_ref, c_s):
        ri = lax.broadcasted_iota(jnp.int32, (BLOCK, BLOCK), 0)
        ci = lax.broadcasted_iota(jnp.int32, (BLOCK, BLOCK), 1)
        tri = (ri >= ci).astype(F32)
        carry = jnp.zeros((1, LANES), F32)
        for bi in range(nb):
            rows = pl.ds(bi * BLOCK, BLOCK)
            logf = _log_sigmoid(z_ref[rows, :] + b_ref[...])
            blk = _f32dot(tri, logf) + carry
            c_s[rows, :] = blk
            carry = blk[BLOCK - 1:BLOCK, :]
        c = c_s[...]
        ct_ref[...] = c.T
        cexp_ref[...] = _f32dot(c, e_ref[...])

    return pl.pallas_call(
        body, name=name, grid=(1,),
        in_specs=[pl.BlockSpec((L, LANES), lambda i: (0, md.bz)), pl.BlockSpec((1, LANES), lambda i: (0, 0)),
                  pl.BlockSpec((LANES, md.wf), lambda i: (0, 0))],
        out_specs=[pl.BlockSpec((L, md.wf), lambda i: (0, 0)), pl.BlockSpec((LANES, L), lambda i: (0, 0))],
        out_shape=[jax.ShapeDtypeStruct((L, md.wf), F32), jax.ShapeDtypeStruct((LANES, L), F32)],
        scratch_shapes=[pltpu.VMEM((L, LANES), F32)],
        compiler_params=_cparams("arbitrary"),
    )(u, b, expand)


def _gate_bwd(u, b, dck_t, md, *, name):
    L = u.shape[0]
    nb = L // BLOCK

    def body(z_ref, b_ref, dck_ref, dz_ref, db_ref, dc_s):
        ri = lax.broadcasted_iota(jnp.int32, (BLOCK, BLOCK), 0)
        ci = lax.broadcasted_iota(jnp.int32, (BLOCK, BLOCK), 1)
        triu = (ri <= ci).astype(F32)
        dc_s[...] = -dck_ref[...].T
        carry = jnp.zeros((1, LANES), F32)
        db = jnp.zeros((1, LANES), F32)
        for bi in range(nb - 1, -1, -1):
            rows = pl.ds(bi * BLOCK, BLOCK)
            blk = _f32dot(triu, dc_s[rows, :]) + carry
            carry = blk[0:1, :]
            z = z_ref[rows, :] + b_ref[...]
            dz = blk * jax.nn.sigmoid(-z)
            if bi == 0:
                dz = jnp.where(lax.broadcasted_iota(jnp.int32, (BLOCK, LANES), 0) >= PAD, dz, 0.0)
            dz_ref[rows, :] = dz
            db = db + jnp.sum(dz, axis=0, keepdims=True)
        db_ref[...] = db

    return pl.pallas_call(
        body, name=name, grid=(1,),
        in_specs=[pl.BlockSpec((L, LANES), lambda i: (0, md.bz)), pl.BlockSpec((1, LANES), lambda i: (0, 0)),
                  pl.BlockSpec((LANES, L), lambda i: (0, 0))],
        out_specs=[pl.BlockSpec((L, LANES), lambda i: (0, 0)), pl.BlockSpec((1, LANES), lambda i: (0, 0))],
        out_shape=[jax.ShapeDtypeStruct((L, LANES), F32), jax.ShapeDtypeStruct((1, LANES), F32)],
        scratch_shapes=[pltpu.VMEM((L, LANES), F32)],
        compiler_params=_cparams("arbitrary"),
    )(u, b, dck_t)


def _head_norm(x, g, bd):
    r = lax.rsqrt(_f32dot(x * x, bd) + EPS)
    xh = x * r
    return xh * g, xh, r


def _head_norm_bwd(dy, xh, r, g, bd):
    dxh = dy * g
    dx = r * (dxh - xh * _f32dot(dxh * xh, bd))
    return dx, jnp.sum(dy * xh, axis=0, keepdims=True)


def _lane_half():
    return lax.broadcasted_iota(jnp.int32, (1, LANES), 1) < HEAD_DIM


def _store_head_pair(x, half, a_s, b_s):
    a_s[...] = jnp.where(half, x, 0.0).astype(BF16)
    b_s[...] = jnp.where(half, 0.0, x).astype(BF16)


def _fox_scores(qm, kn_s, cexp_ref, ct_ref, r0, nr, klen, hh):
    s = _dot_nt(qm, kn_s[0:klen, :]) * (HEAD_DIM ** -0.5)
    s = s + cexp_ref[r0:r0 + nr, HEAD_DIM * hh:HEAD_DIM * hh + 1] - ct_ref[hh:hh + 1, 0:klen]
    qp = r0 + lax.broadcasted_iota(jnp.int32, (nr, klen), 0)
    kp = lax.broadcasted_iota(jnp.int32, (nr, klen), 1)
    return jnp.where((kp <= qp) & (kp >= PAD), s, NEG_INF)


def _fox_fwd(u, cexp, ct3, qg, kg, md, *, name, comm=None):
    L = u.shape[0]
    blocks = _qblocks(L, QBLOCK)
    bd = _block_diag_mean()

    def body(q_ref, k_ref, v_ref, cexp_ref, ct_ref, qg_ref, kg_ref, bd_ref, o_ref, lse_ref, qa_s, qb_s, kn_s, v_s):
        half = _lane_half()
        bdv = bd_ref[...]
        _store_head_pair(_head_norm(q_ref[...], qg_ref[...], bdv)[0], half, qa_s, qb_s)
        kn_s[...] = _head_norm(k_ref[...], kg_ref[...], bdv)[0].astype(BF16)
        v_s[...] = v_ref[...].astype(BF16)
        for r0, nr in blocks:
            klen = r0 + nr
            o_blk = lse_blk = None
            for hh, q_s in enumerate((qa_s, qb_s)):
                s = _fox_scores(q_s[r0:r0 + nr, :], kn_s, cexp_ref, ct_ref, r0, nr, klen, hh)
                m = jnp.max(s, axis=-1, keepdims=True)
                p = jnp.exp(s - m)
                l = jnp.sum(p, axis=-1, keepdims=True)
                oh = jnp.dot(p.astype(BF16), v_s[0:klen, :], preferred_element_type=F32) * (1.0 / l)
                lh = jnp.broadcast_to(jnp.where(m > 0.5 * NEG_INF, m + jnp.log(l), 0.0), (nr, LANES))
                o_blk = oh if hh == 0 else jnp.where(half, o_blk, oh)
                lse_blk = lh if hh == 0 else jnp.where(half, lse_blk, lh)
            o_ref[r0:r0 + nr, :] = o_blk
            lse_ref[r0:r0 + nr, :] = lse_blk

    col = lambda base: pl.BlockSpec((L, LANES), lambda j: (0, base + j))
    vec = pl.BlockSpec((1, LANES), lambda j: (0, 0))
    return _call_carrying(
        body, comm, (u, u, u, cexp, ct3, qg, kg, bd), name=name, grid=(md.pf,),
        in_specs=[col(md.bq_f), col(md.bk_f), col(md.bv_f), col(0),
                  pl.BlockSpec((None, 8, L), lambda j: (j, 0, 0)), vec, vec,
                  pl.BlockSpec((LANES, LANES), lambda j: (0, 0))],
        out_specs=[col(0), col(0)],
        out_shape=[jax.ShapeDtypeStruct((L, md.wf), F32)] * 2,
        scratch_shapes=[pltpu.VMEM((L, LANES), BF16)] * 4)


def _softmax_bwd(p, dp):
    pdp = p * dp
    return pdp - p * jnp.sum(pdp, axis=-1, keepdims=True)


def _fox_bwd(u, cexp, ct3, qg, kg, do, lse, md, *, name, comm=None):
    L = u.shape[0]
    blocks = _qblocks(L, QBLOCK // 2)
    bd = _block_diag_mean()
    fold = _fold_halves()
    npairs = md.pf

    def body(q_ref, k_ref, v_ref, cexp_ref, ct_ref, qg_ref, kg_ref, bd_ref, fold_ref, do_ref, lse_ref,
             dq_ref, dk_ref, dv_ref, dck_ref, dqg_ref, dkg_ref,
             qa_s, qb_s, kn_s, v_s, doa_s, dob_s, dqn_s, dkn_s, dvv_s):
        j = pl.program_id(0)
        half = _lane_half()
        bdv = bd_ref[...]
        qy, qh, rq = _head_norm(q_ref[...], qg_ref[...], bdv)
        ky, kh, rk = _head_norm(k_ref[...], kg_ref[...], bdv)
        _store_head_pair(qy, half, qa_s, qb_s)
        _store_head_pair(do_ref[...], half, doa_s, dob_s)
        kn_s[...] = ky.astype(BF16)
        v_s[...] = v_ref[...].astype(BF16)
        dkn_s[...] = jnp.zeros_like(dkn_s)
        dvv_s[...] = jnp.zeros_like(dvv_s)
        dck_ref[...] = jnp.zeros_like(dck_ref)
        for r0, nr in blocks:
            klen = r0 + nr
            dq_blk = None
            for hh, (q_s, do_s) in enumerate(((qa_s, doa_s), (qb_s, dob_s))):
                c0 = HEAD_DIM * hh
                qm = q_s[r0:r0 + nr, :]
                dom = do_s[r0:r0 + nr, :]
                s = _fox_scores(qm, kn_s, cexp_ref, ct_ref, r0, nr, klen, hh)
                p = jnp.exp(s - lse_ref[r0:r0 + nr, c0:c0 + 1])
                ds = _softmax_bwd(p, _dot_nt(dom, v_s[0:klen, :]))
                dck_ref[hh:hh + 1, 0:klen] += jnp.sum(ds, axis=0, keepdims=True)
                ds = ds * (HEAD_DIM ** -0.5)
                dq_h = jnp.dot(ds.astype(BF16), kn_s[0:klen, :], preferred_element_type=F32)
                dkn_s[0:klen, :] += _dot_tn(ds, qm)
                dvv_s[0:klen, :] += _dot_tn(p, dom)
                dq_blk = dq_h if hh == 0 else jnp.where(half, dq_blk, dq_h)
            dqn_s[r0:r0 + nr, :] = dq_blk
        dq, dqg = _head_norm_bwd(dqn_s[...], qh, rq, qg_ref[...], bdv)
        dk, dkg = _head_norm_bwd(dkn_s[...], kh, rk, kg_ref[...], bdv)
        dq_ref[...] = dq
        dk_ref[...] = dk
        dv_ref[...] = dvv_s[...]

        @pl.when(j == 0)
        def _():
            dqg_ref[...] = jnp.zeros_like(dqg_ref)
            dkg_ref[...] = jnp.zeros_like(dkg_ref)

        dqg_ref[...] += dqg
        dkg_ref[...] += dkg

        @pl.when(j == npairs - 1)
        def _():
            dqg_ref[...] = _f32dot(jnp.broadcast_to(dqg_ref[...], (8, LANES)), fold_ref[...])[0:1, :]
            dkg_ref[...] = _f32dot(jnp.broadcast_to(dkg_ref[...], (8, LANES)), fold_ref[...])[0:1, :]

    col = lambda base: pl.BlockSpec((L, LANES), lambda j: (0, base + j))
    vec = pl.BlockSpec((1, LANES), lambda j: (0, 0))
    sq = pl.BlockSpec((LANES, LANES), lambda j: (0, 0))
    ct_spec = pl.BlockSpec((None, 8, L), lambda j: (j, 0, 0))
    big = jax.ShapeDtypeStruct((L, md.wf), F32)
    small = jax.ShapeDtypeStruct((1, LANES), F32)
    return _call_carrying(
        body, comm, (u, u, u, cexp, ct3, qg, kg, bd, fold, do, lse), name=name, grid=(md.pf,),
        in_specs=[col(md.bq_f), col(md.bk_f), col(md.bv_f), col(0), ct_spec, vec, vec, sq, sq, col(0), col(0)],
        out_specs=[col(0), col(0), col(0), ct_spec, vec, vec],
        out_shape=[big, big, big, jax.ShapeDtypeStruct((md.pf, 8, L), F32), small, small],
        scratch_shapes=[pltpu.VMEM((L, LANES), BF16)] * 6 + [pltpu.VMEM((L, LANES), F32)] * 3)


def _swa_scores(qm, kn_s, slope, k0, r0, nr, klen):
    s = _dot_nt(qm, kn_s[k0:k0 + klen, :]) * (HEAD_DIM ** -0.5)
    qp = r0 + lax.broadcasted_iota(jnp.int32, (nr, klen), 0)
    kp = k0 + lax.broadcasted_iota(jnp.int32, (nr, klen), 1)
    dist = qp - kp
    s = s - slope * dist.astype(F32)
    return jnp.where((dist >= 0) & (dist < WINDOW) & (kp >= PAD), s, NEG_INF)


def _swa_fwd(u, sinkp, slopep, qg, kg, md, *, name, comm=None):
    L = u.shape[0]
    blocks = _qblocks(L, QBLOCK)
    bd = _block_diag_mean()

    def body(q_ref, k_ref, v_ref, sink_ref, slope_ref, qg_ref, kg_ref, bd_ref, o_ref, lse_ref, qa_s, qb_s, kn_s, v_s):
        half = _lane_half()
        bdv = bd_ref[...]
        _store_head_pair(_head_norm(q_ref[...], qg_ref[...], bdv)[0], half, qa_s, qb_s)
        kn_s[...] = _head_norm(k_ref[...], kg_ref[...], bdv)[0].astype(BF16)
        v_s[...] = v_ref[...].astype(BF16)
        for r0, nr in blocks:
            k0 = max(r0 - BLOCK, 0)
            klen = r0 + nr - k0
            o_blk = lse_blk = None
            for hh, q_s in enumerate((qa_s, qb_s)):
                c0 = HEAD_DIM * hh
                s = _swa_scores(q_s[r0:r0 + nr, :], kn_s, slope_ref[0:1, c0:c0 + 1], k0, r0, nr, klen)
                sink = sink_ref[0:1, c0:c0 + 1]
                m = jnp.maximum(jnp.max(s, axis=-1, keepdims=True), sink)
                p = jnp.exp(s - m)
                den = jnp.sum(p, axis=-1, keepdims=True) + jnp.exp(sink - m)
                oh = jnp.dot(p.astype(BF16), v_s[k0:k0 + klen, :], preferred_element_type=F32) * (1.0 / den)
                lh = jnp.broadcast_to(m + jnp.log(den), (nr, LANES))
                o_blk = oh if hh == 0 else jnp.where(half, o_blk, oh)
                lse_blk = lh if hh == 0 else jnp.where(half, lse_blk, lh)
            o_ref[r0:r0 + nr, :] = o_blk
            lse_ref[r0:r0 + nr, :] = lse_blk

    g2 = md.g // 2
    qcol = pl.BlockSpec((L, LANES), lambda j: (0, md.bq_s + j))
    kcol = pl.BlockSpec((L, LANES), lambda j: (0, md.bk_s + j // g2))
    vcol = pl.BlockSpec((L, LANES), lambda j: (0, md.bv_s + j // g2))
    ocol = pl.BlockSpec((L, LANES), lambda j: (0, j))
    pvec = pl.BlockSpec((1, LANES), lambda j: (0, j))
    vec = pl.BlockSpec((1, LANES), lambda j: (0, 0))
    return _call_carrying(
        body, comm, (u, u, u, sinkp, slopep, qg, kg, bd), name=name, grid=(md.ps,),
        in_specs=[qcol, kcol, vcol, pvec, pvec, vec, vec, pl.BlockSpec((LANES, LANES), lambda j: (0, 0))],
        out_specs=[ocol, ocol],
        out_shape=[jax.ShapeDtypeStruct((L, md.ws), F32)] * 2,
        scratch_shapes=[pltpu.VMEM((L, LANES), BF16)] * 4)


def _swa_bwd(u, sinkp, slopep, qg, kg, do, lse, md, *, name, comm=None):
    L = u.shape[0]
    blocks = _qblocks(L, QBLOCK // 2)
    bd = _block_diag_mean()
    fold = _fold_halves()
    g2 = md.g // 2
    npairs = md.ps

    def body(q_ref, k_ref, v_ref, sink_ref, slope_ref, qg_ref, kg_ref, bd_ref, fold_ref, do_ref, lse_ref,
             dq_ref, dk_ref, dv_ref, dsink_ref, dqg_ref, dkg_ref,
             qa_s, qb_s, kn_s, v_s, doa_s, dob_s, dqn_s):
        j = pl.program_id(0)
        half = _lane_half()
        bdv = bd_ref[...]
        qy, qh, rq = _head_norm(q_ref[...], qg_ref[...], bdv)
        ky, kh, rk = _head_norm(k_ref[...], kg_ref[...], bdv)
        _store_head_pair(qy, half, qa_s, qb_s)
        _store_head_pair(do_ref[...], half, doa_s, dob_s)
        kn_s[...] = ky.astype(BF16)
        v_s[...] = v_ref[...].astype(BF16)

        @pl.when(j % g2 == 0)
        def _():
            dk_ref[...] = jnp.zeros_like(dk_ref)
            dv_ref[...] = jnp.zeros_like(dv_ref)

        @pl.when(j == 0)
        def _():
            dqg_ref[...] = jnp.zeros_like(dqg_ref)
            dkg_ref[...] = jnp.zeros_like(dkg_ref)

        dsink = [jnp.zeros((1, 1), F32), jnp.zeros((1, 1), F32)]
        for r0, nr in blocks:
            k0 = max(r0 - BLOCK, 0)
            klen = r0 + nr - k0
            dq_blk = None
            for hh, (q_s, do_s) in enumerate(((qa_s, doa_s), (qb_s, dob_s))):
                c0 = HEAD_DIM * hh
                qm = q_s[r0:r0 + nr, :]
                dom = do_s[r0:r0 + nr, :]
                s = _swa_scores(qm, kn_s, slope_ref[0:1, c0:c0 + 1], k0, r0, nr, klen)
                lse_h = lse_ref[r0:r0 + nr, c0:c0 + 1]
                p = jnp.exp(s - lse_h)
                pdp = p * _dot_nt(dom, v_s[k0:k0 + klen, :])
                delta = jnp.sum(pdp, axis=-1, keepdims=True)
                p_sink = jnp.exp(sink_ref[0:1, c0:c0 + 1] - lse_h)
                dsink[hh] = dsink[hh] - jnp.sum(p_sink * delta, axis=0, keepdims=True)
                ds = (pdp - p * delta) * (HEAD_DIM ** -0.5)
                dq_h = jnp.dot(ds.astype(BF16), kn_s[k0:k0 + klen, :], preferred_element_type=F32)
                dk_ref[k0:k0 + klen, :] += _dot_tn(ds, qm)
                dv_ref[k0:k0 + klen, :] += _dot_tn(p, dom)
                dq_blk = dq_h if hh == 0 else jnp.where(half, dq_blk, dq_h)
            dqn_s[r0:r0 + nr, :] = dq_blk
        dq, dqg = _head_norm_bwd(dqn_s[...], qh, rq, qg_ref[...], bdv)
        dq_ref[...] = dq
        dqg_ref[...] += dqg
        dsink_ref[...] = jnp.where(half, jnp.broadcast_to(dsink[0], (1, LANES)), jnp.broadcast_to(dsink[1], (1, LANES)))

        @pl.when(j % g2 == g2 - 1)
        def _():
            dkn = _f32dot(dk_ref[...], fold_ref[...])
            dk, dkg = _head_norm_bwd(dkn, kh, rk, kg_ref[...], bdv)
            dk_ref[...] = jnp.where(half, dk, 0.0)
            dv_ref[...] = jnp.where(half, _f32dot(dv_ref[...], fold_ref[...]), 0.0)
            dkg_ref[...] += dkg

        @pl.when(j == npairs - 1)
        def _():
            dqg_ref[...] = _f32dot(jnp.broadcast_to(dqg_ref[...], (8, LANES)), fold_ref[...])[0:1, :]

    qcol = pl.BlockSpec((L, LANES), lambda j: (0, md.bq_s + j))
    kcol = pl.BlockSpec((L, LANES), lambda j: (0, md.bk_s + j // g2))
    vcol = pl.BlockSpec((L, LANES), lambda j: (0, md.bv_s + j // g2))
    ocol = pl.BlockSpec((L, LANES), lambda j: (0, j))
    kvout = pl.BlockSpec((L, LANES), lambda j: (0, j // g2))
    pvec = pl.BlockSpec((1, LANES), lambda j: (0, j))
    vec = pl.BlockSpec((1, LANES), lambda j: (0, 0))
    sq = pl.BlockSpec((LANES, LANES), lambda j: (0, 0))
    kvshape = jax.ShapeDtypeStruct((L, LANES * md.nkv), F32)
    small = jax.ShapeDtypeStruct((1, LANES), F32)
    return _call_carrying(
        body, comm, (u, u, u, sinkp, slopep, qg, kg, bd, fold, do, lse), name=name, grid=(md.ps,),
        in_specs=[qcol, kcol, vcol, pvec, pvec, vec, vec, sq, sq, ocol, ocol],
        out_specs=[ocol, kvout, kvout, pvec, vec, vec],
        out_shape=[jax.ShapeDtypeStruct((L, md.ws), F32), kvshape, kvshape,
                   jax.ShapeDtypeStruct((1, md.ws), F32), small, small],
        scratch_shapes=[pltpu.VMEM((L, LANES), BF16)] * 6 + [pltpu.VMEM((L, LANES), F32)])


def _outnorm_fwd(of, os_, gf, gs, *, name):
    L, wf = of.shape
    ws = os_.shape[1]
    tr = _pick(L, _TR)

    def body(of_ref, os_ref, gf_ref, gs_ref, o_ref):
        for src, g_ref, c0, w in ((of_ref, gf_ref, 0, wf), (os_ref, gs_ref, wf, ws)):
            x = src[...]
            r = lax.rsqrt(jnp.mean(x * x, axis=-1, keepdims=True) + EPS)
            o_ref[:, c0:c0 + w] = (x * r * g_ref[...]).astype(o_ref.dtype)

    return pl.pallas_call(
        body, name=name, grid=(L // tr,),
        in_specs=[pl.BlockSpec((tr, wf), lambda i: (i, 0)), pl.BlockSpec((tr, ws), lambda i: (i, 0)),
                  pl.BlockSpec((1, wf), lambda i: (0, 0)), pl.BlockSpec((1, ws), lambda i: (0, 0))],
        out_specs=pl.BlockSpec((tr, wf + ws), lambda i: (i, 0)),
        out_shape=jax.ShapeDtypeStruct((L, wf + ws), BF16),
        compiler_params=_cparams("parallel"),
    )(of, os_, gf, gs)


def _outnorm_bwd(don, of, os_, gf, gs, *, name):
    L, wf = of.shape
    ws = os_.shape[1]
    tr = _pick(L, _TR)

    def body(d_ref, of_ref, os_ref, gf_ref, gs_ref, dof_ref, dos_ref, dgf_ref, dgs_ref):
        i = pl.program_id(0)
        for src, g_ref, c0, w, dx_ref, dg_ref in ((of_ref, gf_ref, 0, wf, dof_ref, dgf_ref),
                                                  (os_ref, gs_ref, wf, ws, dos_ref, dgs_ref)):
            x = src[...]
            dy = d_ref[:, c0:c0 + w]
            r = lax.rsqrt(jnp.mean(x * x, axis=-1, keepdims=True) + EPS)
            xh = x * r
            dxh = dy * g_ref[...]
            dx_ref[...] = r * (dxh - xh * jnp.mean(dxh * xh, axis=-1, keepdims=True))
            part = jnp.sum(dy * xh, axis=0, keepdims=True)

            @pl.when(i == 0)
            def _():
                dg_ref[...] = part

            @pl.when(i > 0)
            def _():
                dg_ref[...] += part

    rf = pl.BlockSpec((tr, wf), lambda i: (i, 0))
    rs = pl.BlockSpec((tr, ws), lambda i: (i, 0))
    vf = pl.BlockSpec((1, wf), lambda i: (0, 0))
    vs = pl.BlockSpec((1, ws), lambda i: (0, 0))
    return pl.pallas_call(
        body, name=name, grid=(L // tr,),
        in_specs=[pl.BlockSpec((tr, wf + ws), lambda i: (i, 0)), rf, rs, vf, vs],
        out_specs=[rf, rs, vf, vs],
        out_shape=[jax.ShapeDtypeStruct((L, wf), F32), jax.ShapeDtypeStruct((L, ws), F32),
                   jax.ShapeDtypeStruct((1, wf), F32), jax.ShapeDtypeStruct((1, ws), F32)],
        compiler_params=_cparams("arbitrary"),
    )(don, of, os_, gf, gs)


def _win_to_mine(w, md):
    d = w.shape[0]
    wf, ws, hd = md.wf, md.ws, HEAD_DIM
    o_z = 3 * wf
    o_sq = o_z + md.hf
    o_sk = o_sq + ws
    o_sv = o_sk + md.nkv * hd
    parts = [w[:, :3 * wf], w[:, o_sq:o_sq + ws]]
    for base in (o_sk, o_sv):
        for kv in range(md.nkv):
            blk = w[:, base + kv * hd:base + (kv + 1) * hd]
            parts += [blk, blk]
    z = w[:, o_z:o_z + md.hf].reshape(d, md.hf // 2, 2)
    z = jnp.pad(z, ((0, 0), (0, 0), (0, 6))).reshape(d, 4 * md.hf)
    parts.append(jnp.pad(z, ((0, 0), (0, LANES - 4 * md.hf + md.nup - md.nu))))
    return jnp.concatenate(parts, axis=1)


def _win_grad_to_ref(dw, md):
    d = dw.shape[0]
    wf, ws, hd = md.wf, md.ws, HEAD_DIM
    z = dw[:, md.bz * LANES:md.bz * LANES + 4 * md.hf].reshape(d, md.hf // 2, 8)[:, :, :2].reshape(d, md.hf)
    parts = [dw[:, :3 * wf], z, dw[:, md.bq_s * LANES:md.bq_s * LANES + ws]]
    for base in (md.bk_s, md.bv_s):
        for kv in range(md.nkv):
            c0 = (base + kv) * LANES
            parts.append(dw[:, c0:c0 + hd])
    return jnp.concatenate(parts, axis=1)


def _mix_small(p, md):
    tile2 = lambda v: jnp.tile(v.reshape(1, HEAD_DIM), (1, 2))
    b = p["b_forget"].reshape(md.hf // 2, 2)
    b = jnp.pad(b, ((0, 0), (0, 6))).reshape(1, 4 * md.hf)
    slopes = np.asarray(2.0 ** (-8.0 * np.arange(1, md.hq + 1) / md.hq), np.float32)
    return dict(
        g_mix=p["mix_norm"].reshape(1, -1),
        b_gate=jnp.pad(b, ((0, 0), (0, LANES - 4 * md.hf))),
        fqg=tile2(p["fox_q_norm"]), fkg=tile2(p["fox_k_norm"]),
        sqg=tile2(p["swa_q_norm"]), skg=tile2(p["swa_k_norm"]),
        sinkp=jnp.repeat(p["swa_sinks"], HEAD_DIM).reshape(1, md.ws),
        slopep=jnp.asarray(np.repeat(slopes, HEAD_DIM).reshape(1, md.ws)),
        gfo=p["fox_out_norm"].reshape(1, md.wf), gso=p["swa_out_norm"].reshape(1, md.ws),
    )


def _mix_fwd(h, sp, w, md, queue=None):
    L = h.shape[0]
    hn = _rms_fwd(h, sp["g_mix"], name="mix_rms")
    u = _mm(queue, hn, w(0), name="mix_u")
    cexp, ct = _gate_fwd(u, sp["b_gate"], md, name="gate_fwd")
    ct3 = ct[:8 * md.pf].reshape(md.pf, 8, L)
    of, lsef = _carried(queue, _fox_fwd, 2, u, cexp, ct3, sp["fqg"], sp["fkg"], md, name="fox_fwd")
    os_, lses = _carried(queue, _swa_fwd, 2, u, sp["sinkp"], sp["slopep"], sp["sqg"], sp["skg"], md, name="swa_fwd")
    on = _outnorm_fwd(of, os_, sp["gfo"], sp["gso"], name="outnorm_fwd")
    h_out = _matmul(on, w(1), name="mix_out", residual=h)
    return h_out, (h, hn, u, cexp, ct3, of, lsef, os_, lses, on)


def _mix_bwd(dh, dhb, saved, sp, w_in, w_out, md, queue=None):
    h, hn, u, cexp, ct3, of, lsef, os_, lses, on = saved
    L = h.shape[0]
    don = _matmul(dhb, w_out, name="mix_don", nt=True)
    dw_out = _matmul(on, dhb, name="mix_dwout", ta=True, out_dtype=BF16)
    dof, dos, dgfo, dgso = _outnorm_bwd(don, of, os_, sp["gfo"], sp["gso"], name="outnorm_bwd")
    duq, duk, duv, dck, dfqg, dfkg = _carried(queue, _fox_bwd, 6, u, cexp, ct3, sp["fqg"], sp["fkg"], dof, lsef, md,
                                              name="fox_bwd")
    dsq, dsk, dsv, dsinkp, dsqg, dskg = _carried(queue, _swa_bwd, 6, u, sp["sinkp"], sp["slopep"], sp["sqg"], sp["skg"], dos,
                                                 lses, md, name="swa_bwd")
    dck_t = jnp.pad(dck.reshape(8 * md.pf, L), ((0, LANES - 8 * md.pf), (0, 0)))
    dz, db = _gate_bwd(u, sp["b_gate"], dck_t, md, name="gate_bwd")
    du = jnp.concatenate([duq, duk, duv, dsq, dsk, dsv, dz, jnp.zeros((L, md.nup - md.nu), F32)], axis=1).astype(BF16)
    dhn = _mm(queue, du, w_in, name="mix_dhn", nt=True)
    dw_in = _mm(queue, hn, du, name="mix_dwin", ta=True, out_dtype=BF16)
    dh_in, dhb_in, dg_mix = _rms_bwd(dhn, h, sp["g_mix"], dh, name="mix_drms")
    small = dict(
        mix_norm=dg_mix.reshape(-1),
        b_forget=db[0, :4 * md.hf].reshape(md.hf // 2, 8)[:, :2].reshape(md.hf),
        fox_q_norm=dfqg[0, :HEAD_DIM], fox_k_norm=dfkg[0, :HEAD_DIM],
        swa_q_norm=dsqg[0, :HEAD_DIM], swa_k_norm=dskg[0, :HEAD_DIM],
        swa_sinks=dsinkp[0, ::HEAD_DIM],
        fox_out_norm=dgfo.reshape(-1), swa_out_norm=dgso.reshape(-1),
    )
    return dh_in, dhb_in, dw_in, dw_out, small


_ANY = pl.BlockSpec(memory_space=pl.ANY)
_HALF_ROWS = (512, 352, 256, 192, 128, 64, 32, 16)


def _mesh_pos():
    return lax.axis_index("x"), lax.axis_index("y"), lax.axis_index("c")


def _other_chips(x, y):
    return [(1 - x, y), (x, 1 - y), (1 - x, 1 - y)]


def _rows_half(ref, which):
    rh = ref.shape[-2] // 2
    if len(ref.shape) == 2:
        return ref.at[pl.ds(which * rh, rh), :]
    return ref.at[:, pl.ds(which * rh, rh), :]


def _remote(src, dst, send_sems, recv_sems, idx, dev):
    return pltpu.make_async_remote_copy(src_ref=src, dst_ref=dst, send_sem=send_sems.at[idx], recv_sem=recv_sems.at[idx],
                                        device_id=dev, device_id_type=MESH)


def _cast_into_chunk(w, l_idx, k_idx, *, name):
    _, rows, cols = w.shape
    tr = _pick(rows, (512, 352, 256, 128, 64, 32, 16))

    def body(l_ref, k_ref, w_ref, o_ref):
        o_ref[...] = w_ref[...].astype(o_ref.dtype)

    return pl.pallas_call(
        body, name=name,
        grid_spec=pltpu.PrefetchScalarGridSpec(
            num_scalar_prefetch=2, grid=(rows // tr,),
            in_specs=[pl.BlockSpec((None, tr, cols), lambda i, l, k: (l[0], i, 0))],
            out_specs=pl.BlockSpec((None, tr, cols), lambda i, l, k: (k[0], i, 0))),
        out_shape=jax.ShapeDtypeStruct((NCH, rows, cols), BF16),
        compiler_params=_cparams("parallel"),
    )(l_idx, k_idx, w)


class _SiblingSwap:
    n_sems = 1

    def __init__(self, g):
        self.operands = [g]
        self.out_shapes = [jax.ShapeDtypeStruct((NCH, g.shape[1] // 2, g.shape[2]), g.dtype)]
        self.aliases = {}

    def copies(self, cin, cout, send_sems, recv_sems, base=0):
        x, y, c = _mesh_pos()
        cp = _remote(_rows_half(cin[0], 1 - c), cout[0], send_sems, recv_sems, base, (x, y, 1 - c))
        return [(cp, cp)]


class _ChipExchange:
    n_sems = NCH - 1

    def __init__(self, s, r0, r1, land=None):
        self.r0, self.r1 = r0, r1
        self.operands = [s] if land is None else [s, land]
        self.out_shapes = [jax.ShapeDtypeStruct((NCH - 1,) + s.shape[1:], s.dtype)]
        self.aliases = {} if land is None else {1: 0}

    def copies(self, cin, cout, send_sems, recv_sems, base=0):
        x, y, c = _mesh_pos()
        rows = pl.ds(self.r0, self.r1 - self.r0)
        cps = [_remote(cin[0].at[2 * cx + cy, rows, :], cout[0].at[j, rows, :], send_sems, recv_sems, base + j, (cx, cy, c))
               for j, (cx, cy) in enumerate(_other_chips(x, y))]
        return [(cp, cp) for cp in cps]


class _SiblingShare:
    n_sems = 1

    def __init__(self, tot):
        self.operands = [tot]
        self.out_shapes = [jax.ShapeDtypeStruct(tot.shape, tot.dtype)]
        self.aliases = {0: 0}

    def copies(self, cin, cout, send_sems, recv_sems, base=0):
        x, y, c = _mesh_pos()
        mine, theirs = _rows_half(cout[0], c), _rows_half(cout[0], 1 - c)
        return [(_remote(mine, mine, send_sems, recv_sems, base, (x, y, 1 - c)),
                 _remote(theirs, theirs, send_sems, recv_sems, base, (x, y, 1 - c)))]


class _Both:
    def __init__(self, a, b):
        self.a, self.b = a, b
        self.operands = a.operands + b.operands
        self.out_shapes = a.out_shapes + b.out_shapes
        self.aliases = dict(a.aliases)
        self.aliases.update({len(a.operands) + s: len(a.out_shapes) + d for s, d in b.aliases.items()})
        self.n_sems = a.n_sems + b.n_sems

    def copies(self, cin, cout, send_sems, recv_sems, base=0):
        na, nao = len(self.a.operands), len(self.a.out_shapes)
        return (self.a.copies(cin[:na], cout[:nao], send_sems, recv_sems, base)
                + self.b.copies(cin[na:], cout[nao:], send_sems, recv_sems, base + self.a.n_sems))


def _rows_quarter(ref, c, q):
    rq = ref.shape[0] // 4
    return ref.at[pl.ds((2 * c + q) * rq, rq), :]


class _GatherStep:
    n_sems = 7

    def __init__(self, stage1, stage2, stage3):
        given = [(st, b) for st, b in ((1, stage1), (2, stage2), (3, stage3)) if b is not None]
        self.stages = [st for st, _ in given]
        self.operands = [b for _, b in given]
        self.out_shapes = [jax.ShapeDtypeStruct(b.shape, b.dtype) for b in self.operands]
        self.aliases = {i: i for i in range(len(self.operands))}

    def copies(self, cin, cout, send_sems, recv_sems, base=0):
        x, y, c = _mesh_pos()
        k, kx, ky, kd = 2 * x + y, 2 * (1 - x) + y, 2 * x + (1 - y), 2 * (1 - x) + (1 - y)
        xn, yn, sibling = (1 - x, y, c), (x, 1 - y, c), (x, y, 1 - c)

        def pair(idx, dev, src, arrival):
            return (_remote(src, src, send_sems, recv_sems, base + idx, dev),
                    _remote(arrival, arrival, send_sems, recv_sems, base + idx, dev))

        out = []
        for stage, buf in zip(self.stages, cout):
            if stage == 1:
                mine = _rows_half(buf.at[k], c)
                out.append(pair(0, xn, mine, _rows_half(buf.at[kx], c)))
                out.append(pair(1, yn, mine, _rows_half(buf.at[ky], c)))
            elif stage == 2:
                out.append(pair(2, yn, _rows_quarter(buf.at[kx], c, 0), _rows_quarter(buf.at[kd], c, 0)))
                out.append(pair(3, xn, _rows_quarter(buf.at[ky], c, 1), _rows_quarter(buf.at[kd], c, 1)))
                out.append(pair(4, sibling, _rows_half(buf.at[kx], c), _rows_half(buf.at[kx], 1 - c)))
                out.append(pair(5, sibling, _rows_half(buf.at[ky], c), _rows_half(buf.at[ky], 1 - c)))
            else:
                out.append(pair(6, sibling, _rows_half(buf.at[kd], c), _rows_half(buf.at[kd], 1 - c)))
        return out


class _GatherQueue:
    def __init__(self, bufs):
        self.bufs = list(bufs)
        self.step = -3
        for _ in range(3):
            self.give(_run_exchange(self.take(), name="first_allgather"))

    def take(self):
        s = self.step
        self.step += 1
        at = [i if 0 <= i < len(self.bufs) else None for i in (s + 3, s + 2, s + 1)]
        self.cur = [i for i in at if i is not None]
        if not self.cur:
            return None
        return _GatherStep(*[None if i is None else self.bufs[i] for i in at])

    def give(self, outs):
        for i, buf in zip(self.cur, outs):
            self.bufs[i] = buf


def _run_exchange(job, *, name):
    n_in, n_out = len(job.operands), len(job.out_shapes)

    def body(*refs):
        cps = job.copies(refs[:n_in], refs[n_in:n_in + n_out], refs[-2], refs[-1])
        for cp, _ in cps:
            cp.start()
        for _, arrival in cps:
            arrival.wait_recv()
        for cp, _ in cps:
            cp.wait_send()

    return pl.pallas_call(
        body, name=name, in_specs=[_ANY] * n_in, out_specs=[_ANY] * n_out, out_shape=list(job.out_shapes),
        input_output_aliases=dict(job.aliases),
        scratch_shapes=[pltpu.SemaphoreType.DMA((job.n_sems,)), pltpu.SemaphoreType.DMA((job.n_sems,))],
    )(*job.operands)


class _ExchangeQueue:
    def __init__(self):
        self.lines = ([], [])
        self.cur = []

    def put(self, line, make_job, done):
        self.lines[line].append((make_job, done))

    def take(self):
        entries = [ln.pop(0) for ln in self.lines if ln]
        if not entries:
            return None
        jobs = [make() for make, _ in entries]
        self.cur = [(job, done) for job, (_, done) in zip(jobs, entries)]
        return jobs[0] if len(jobs) == 1 else _Both(*jobs)

    def give(self, outs):
        outs = list(outs)
        for job, done in self.cur:
            n = len(job.out_shapes)
            done(outs[:n])
            outs = outs[n:]

    def drain(self, name):
        while (job := self.take()) is not None:
            self.give(_run_exchange(job, name=name))


ICI_LINE, D2D_LINE = 0, 1


def _reduce_scatter_later(queue, parts, c_idx, k_idx, tag, done, n_parts=2):
    n = len(parts)
    st = dict(lands=[None] * n, sums=None, chip_lands=[None] * n, tots=[None] * n, left=n)

    def shared(i, outs):
        st['tots'][i] = outs[0]
        st['left'] -= 1
        if st['left'] == 0:
            done(st['tots'])

    def exchanged(i, outs):
        st['chip_lands'][i] = outs[0]
        st['left'] -= 1
        if st['left'] == 0:
            st['left'] = n
            for j in range(n):
                tot = _add_chunks(st['sums'][j], st['chip_lands'][j], k_idx, c_idx, name=f"{tag}_rs_add4")
                queue.put(D2D_LINE, lambda tot=tot: _SiblingShare(tot), lambda outs, j=j: shared(j, outs))

    def swapped(i, outs):
        st['lands'][i] = outs[0]
        st['left'] -= 1
        if st['left'] == 0:
            st['sums'] = [_add_own_half(p, l, c_idx, name=f"{tag}_rs_add2") for p, l in zip(parts, st['lands'])]
            st['left'] = n * n_parts
            for j in range(n):
                step = st['sums'][j].shape[1] // n_parts
                for p in range(n_parts):
                    queue.put(ICI_LINE,
                              lambda j=j, p=p, step=step: _ChipExchange(st['sums'][j], p * step, (p + 1) * step, st['chip_lands'][j]),
                              lambda outs, j=j: exchanged(j, outs))

    for i in range(n):
        queue.put(D2D_LINE, lambda i=i: _SiblingSwap(parts[i]), lambda outs, i=i: swapped(i, outs))


def _mm(queue, a, b, **kw):
    job = queue.take() if queue is not None else None
    if job is None:
        return _matmul(a, b, **kw)
    n_out = len(kw.get("out_dtypes") or (0,))
    outs = _matmul(a, b, comm=job, **kw)
    queue.give(outs[n_out:])
    return outs[0] if n_out == 1 else outs[:n_out]


def _allgather_devices(v, *, name):
    m = v.shape[0]

    def body(v_ref, out_ref, send_sems, recv_sems, local_sem):
        x, y, c = _mesh_pos()
        mine = 4 * x + 2 * y + c
        own = pltpu.make_async_copy(v_ref, out_ref.at[mine], local_sem)
        own.start()
        cps = []
        for r in range(1, 8):
            px, py, pc = (x + (r >> 2)) % 2, (y + ((r >> 1) & 1)) % 2, (c + (r & 1)) % 2
            cps.append((_remote(v_ref, out_ref.at[mine], send_sems, recv_sems, r - 1, (px, py, pc)), 4 * px + 2 * py + pc))
        for cp, _ in cps:
            cp.start()
        for r, (cp, theirs) in enumerate(cps):
            blk = out_ref.at[theirs]
            _remote(blk, blk, send_sems, recv_sems, r, (x, y, c)).wait_recv()
        for cp, _ in cps:
            cp.wait_send()
        own.wait()

    return pl.pallas_call(
        body, name=name, in_specs=[_ANY], out_specs=_ANY,
        out_shape=jax.ShapeDtypeStruct((8, m, LANES), v.dtype),
        scratch_shapes=[pltpu.SemaphoreType.DMA((7,)), pltpu.SemaphoreType.DMA((7,)), pltpu.SemaphoreType.DMA],
    )(v)


def _add_own_half(g, land, c_idx, *, name):
    nch, rh, cols = land.shape
    tr = _pick(rh, _HALF_ROWS)
    nt = rh // tr

    def body(c_ref, g_ref, l_ref, o_ref):
        o_ref[...] = (g_ref[...].astype(F32) + l_ref[...].astype(F32)).astype(o_ref.dtype)

    return pl.pallas_call(
        body, name=name,
        grid_spec=pltpu.PrefetchScalarGridSpec(
            num_scalar_prefetch=1, grid=(nch, nt),
            in_specs=[pl.BlockSpec((None, tr, cols), lambda k, i, c: (k, c[0] * nt + i, 0)),
                      pl.BlockSpec((None, tr, cols), lambda k, i, c: (k, i, 0))],
            out_specs=pl.BlockSpec((None, tr, cols), lambda k, i, c: (k, i, 0))),
        out_shape=jax.ShapeDtypeStruct(land.shape, BF16),
        compiler_params=_cparams("parallel", "parallel"),
    )(c_idx, g, land)


def _add_chunks(s, land, k_idx, c_idx, *, name):
    _, rh, cols = s.shape
    tr = _pick(rh, _HALF_ROWS)
    nt = rh // tr

    def body(k_ref, c_ref, s_ref, l_ref, o_ref):
        t = s_ref[...].astype(F32)
        for j in range(NCH - 1):
            t = t + l_ref[j].astype(F32)
        o_ref[...] = t

    return pl.pallas_call(
        body, name=name,
        grid_spec=pltpu.PrefetchScalarGridSpec(
            num_scalar_prefetch=2, grid=(nt,),
            in_specs=[pl.BlockSpec((None, tr, cols), lambda i, k, c: (k[0], i, 0)),
                      pl.BlockSpec((NCH - 1, tr, cols), lambda i, k, c: (0, i, 0))],
            out_specs=pl.BlockSpec((tr, cols), lambda i, k, c: (c[0] * nt + i, 0))),
        out_shape=jax.ShapeDtypeStruct((2 * rh, cols), F32),
        compiler_params=_cparams("parallel"),
    )(k_idx, c_idx, s, land)


def _sum_devices(v, *, name):
    _, m, _ = v.shape

    def body(v_ref, o_ref):
        t = v_ref[0]
        for d in range(1, 8):
            t = t + v_ref[d]
        o_ref[...] = t

    return pl.pallas_call(
        body, name=name, grid=(1,),
        in_specs=[pl.BlockSpec((8, m, LANES), lambda i: (0, 0, 0))],
        out_specs=pl.BlockSpec((m, LANES), lambda i: (0, 0)),
        out_shape=jax.ShapeDtypeStruct((m, LANES), F32),
        compiler_params=_cparams("arbitrary"),
    )(v)


def _adamw_math(w, g, m, v):
    m = ADAM_B1 * m + (1.0 - ADAM_B1) * g
    v = ADAM_B2 * v + (1.0 - ADAM_B2) * (g * g)
    m_hat = m / (1.0 - ADAM_B1 ** ADAM_STEP)
    v_hat = v / (1.0 - ADAM_B2 ** ADAM_STEP)
    delta = -ADAM_LR * (m_hat / (jnp.sqrt(v_hat) + ADAM_EPS) + ADAM_WD * w)
    return delta, m, v


def _adamw_layer(w, m, v, g, layer, prev, *, name):
    depth, rows, cols = w.shape
    tr = _pick(rows, (256, 128, 64, 32, 16, 8))
    lay = pl.BlockSpec((None, tr, cols), lambda i, l: (l[0], i, 0))
    n_prev = 0 if prev is None else 4

    def body(l_ref, w_ref, m_ref, v_ref, g_ref, *rest):
        go_ref, d_ref, mo_ref, vo_ref = rest[n_prev:]
        g = g_ref[...]
        delta, m_new, v_new = _adamw_math(w_ref[...], g, m_ref[...], v_ref[...])
        go_ref[...] = g
        d_ref[...] = delta
        mo_ref[...] = m_new
        vo_ref[...] = v_new

    stack = jax.ShapeDtypeStruct(w.shape, F32)
    return pl.pallas_call(
        body, name=name,
        grid_spec=pltpu.PrefetchScalarGridSpec(
            num_scalar_prefetch=1, grid=(rows // tr,),
            in_specs=[lay, lay, lay, pl.BlockSpec((tr, cols), lambda i, l: (i, 0))] + [_ANY] * n_prev,
            out_specs=[lay] * 4),
        out_shape=[stack] * 4,
        input_output_aliases={} if prev is None else {5 + q: q for q in range(4)},
        compiler_params=_cparams("parallel"),
    )(layer, w, m, v, g, *(() if prev is None else prev))


def _adamw_flat(w, g, m, v, *, name):
    def body(w_ref, g_ref, m_ref, v_ref, d_ref, mo_ref, vo_ref):
        d_ref[...], mo_ref[...], vo_ref[...] = _adamw_math(w_ref[...], g_ref[...], m_ref[...], v_ref[...])

    blk = pl.BlockSpec(w.shape, lambda i: (0, 0))
    return pl.pallas_call(
        body, name=name, grid=(1,), in_specs=[blk] * 4, out_specs=[blk] * 3,
        out_shape=[jax.ShapeDtypeStruct(w.shape, F32)] * 3, compiler_params=_cparams("arbitrary"),
    )(w, g, m, v)


_WEIGHTS = ('meta_tokens', 'ffn1_norm', 'ffn1_w_gate', 'ffn1_w_up', 'ffn1_w_down', 'mix_norm', 'w_in', 'b_forget',
            'fox_q_norm', 'fox_k_norm', 'swa_q_norm', 'swa_k_norm', 'swa_sinks', 'fox_out_norm', 'swa_out_norm', 'w_out',
            'ffn2_norm', 'ffn2_w_gate', 'ffn2_w_up', 'ffn2_w_down')
_BIG = ('ffn1_w_gate', 'ffn1_w_up', 'ffn1_w_down', 'w_in', 'w_out', 'ffn2_w_gate', 'ffn2_w_up', 'ffn2_w_down')
_SMALL = tuple(n for n in _WEIGHTS if n not in _BIG and n != 'meta_tokens')
_MIX_SMALL = ('mix_norm', 'b_forget', 'fox_q_norm', 'fox_k_norm', 'swa_q_norm', 'swa_k_norm', 'swa_sinks',
              'fox_out_norm', 'swa_out_norm')


def _pack_rows(vectors):
    flat = jnp.concatenate([v.reshape(-1) for v in vectors])
    n = flat.shape[0]
    m = -(-n // (8 * LANES)) * 8
    return jnp.pad(flat, (0, m * LANES - n)).reshape(m, LANES)


def _unpack_rows(packed, shapes):
    flat = packed.reshape(-1)
    out, o = [], 0
    for s in shapes:
        n = int(np.prod(s))
        out.append(flat[o:o + n].reshape(s))
        o += n
    return out


def kernel(x, meta_tokens, ffn1_norm, ffn1_w_gate, ffn1_w_up, ffn1_w_down, mix_norm, w_in, b_forget, fox_q_norm, fox_k_norm, swa_q_norm, swa_k_norm, swa_sinks, fox_out_norm, swa_out_norm, w_out, ffn2_norm, ffn2_w_gate, ffn2_w_up, ffn2_w_down, loss_target, m_meta_tokens, m_ffn1_norm, m_ffn1_w_gate, m_ffn1_w_up, m_ffn1_w_down, m_mix_norm, m_w_in, m_b_forget, m_fox_q_norm, m_fox_k_norm, m_swa_q_norm, m_swa_k_norm, m_swa_sinks, m_fox_out_norm, m_swa_out_norm, m_w_out, m_ffn2_norm, m_ffn2_w_gate, m_ffn2_w_up, m_ffn2_w_down, v_meta_tokens, v_ffn1_norm, v_ffn1_w_gate, v_ffn1_w_up, v_ffn1_w_down, v_mix_norm, v_w_in, v_b_forget, v_fox_q_norm, v_fox_k_norm, v_swa_q_norm, v_swa_k_norm, v_swa_sinks, v_fox_out_norm, v_swa_out_norm, v_w_out, v_ffn2_norm, v_ffn2_w_gate, v_ffn2_w_up, v_ffn2_w_down):
    W = dict(meta_tokens=meta_tokens, ffn1_norm=ffn1_norm, ffn1_w_gate=ffn1_w_gate, ffn1_w_up=ffn1_w_up, ffn1_w_down=ffn1_w_down, mix_norm=mix_norm, w_in=w_in, b_forget=b_forget, fox_q_norm=fox_q_norm, fox_k_norm=fox_k_norm, swa_q_norm=swa_q_norm, swa_k_norm=swa_k_norm, swa_sinks=swa_sinks, fox_out_norm=fox_out_norm, swa_out_norm=swa_out_norm, w_out=w_out, ffn2_norm=ffn2_norm, ffn2_w_gate=ffn2_w_gate, ffn2_w_up=ffn2_w_up, ffn2_w_down=ffn2_w_down)
    Mo = dict(meta_tokens=m_meta_tokens, ffn1_norm=m_ffn1_norm, ffn1_w_gate=m_ffn1_w_gate, ffn1_w_up=m_ffn1_w_up, ffn1_w_down=m_ffn1_w_down, mix_norm=m_mix_norm, w_in=m_w_in, b_forget=m_b_forget, fox_q_norm=m_fox_q_norm, fox_k_norm=m_fox_k_norm, swa_q_norm=m_swa_q_norm, swa_k_norm=m_swa_k_norm, swa_sinks=m_swa_sinks, fox_out_norm=m_fox_out_norm, swa_out_norm=m_swa_out_norm, w_out=m_w_out, ffn2_norm=m_ffn2_norm, ffn2_w_gate=m_ffn2_w_gate, ffn2_w_up=m_ffn2_w_up, ffn2_w_down=m_ffn2_w_down)
    Vo = dict(meta_tokens=v_meta_tokens, ffn1_norm=v_ffn1_norm, ffn1_w_gate=v_ffn1_w_gate, ffn1_w_up=v_ffn1_w_up, ffn1_w_down=v_ffn1_w_down, mix_norm=v_mix_norm, w_in=v_w_in, b_forget=v_b_forget, fox_q_norm=v_fox_q_norm, fox_k_norm=v_fox_k_norm, swa_q_norm=v_swa_q_norm, swa_k_norm=v_swa_k_norm, swa_sinks=v_swa_sinks, fox_out_norm=v_fox_out_norm, swa_out_norm=v_swa_out_norm, w_out=v_w_out, ffn2_norm=v_ffn2_norm, ffn2_w_gate=v_ffn2_w_gate, ffn2_w_up=v_ffn2_w_up, ffn2_w_down=v_ffn2_w_down)

    _, S, D = x.shape
    L = S + BLOCK
    depth = ffn1_norm.shape[0]
    md = _MixDims(D)
    mx, my, mc = _mesh_pos()
    k_idx = (2 * mx + my).astype(jnp.int32).reshape(1)
    c_idx = mc.astype(jnp.int32).reshape(1)
    dcols = D // NCH

    meta_all = _allgather_devices(meta_tokens.reshape(-1, LANES), name="meta_allgather")
    meta_full = jnp.transpose(meta_all[0::2].reshape(NCH, N_META, dcols), (1, 0, 2)).reshape(N_META, D)

    order = ('ffn1_w_gate', 'ffn1_w_up', 'ffn1_w_down', 'w_in', 'w_out', 'ffn2_w_gate', 'ffn2_w_up', 'ffn2_w_down')
    chunks = []
    for l in range(depth):
        l_idx = jnp.full((1,), l, jnp.int32)
        chunks += [_cast_into_chunk(W[name], l_idx, k_idx, name="cast_chunk") for name in order]
    gather = _GatherQueue(chunks)
    wts = [{} for _ in range(depth)]

    def weight(l, i):
        if i not in wts[l]:
            buf = gather.bufs[len(order) * l + i]
            if order[i] == 'w_in':
                buf = _win_to_mine(jnp.transpose(buf, (1, 0, 2)).reshape(D, NCH * buf.shape[2]), md)
            elif order[i] in ('w_out', 'ffn1_w_down', 'ffn2_w_down'):
                buf = buf.reshape(-1, D)
            wts[l][i] = buf
        return wts[l][i]

    h = jnp.concatenate([jnp.zeros((PAD, D), F32), meta_full, x[0]], axis=0)
    saved = []
    for l in range(depth):
        sp = _mix_small({n: W[n][l] for n in _MIX_SMALL}, md)
        h, s1 = _ffn_fwd(h, ffn1_norm[l].reshape(1, D), lambda i: weight(l, i), "ffn", gather)
        h, s2 = _mix_fwd(h, sp, lambda i: weight(l, 3 + i), md, gather)
        h, s3 = _ffn_fwd(h, ffn2_norm[l].reshape(1, D), lambda i: weight(l, 5 + i), "ffn", gather)
        saved.append((s1, s2, s3, sp))
    wts = [dict(g1=w[0], u1=w[1], d1=w[2], wi=w[3], wo=w[4], g2=w[5], u2=w[6], d2=w[7]) for w in wts]

    loss_part, dh, dhb = _loss_grad(h, loss_target[0], name="loss_grad")

    small_grads = {n: [None] * depth for n in _SMALL}
    stacks = {n: None for n in _BIG}

    def update(name, l, grad):
        stacks[name] = _adamw_layer(W[name], Mo[name], Vo[name], grad, jnp.full((1,), l, jnp.int32), stacks[name],
                                    name="adamw_layer")

    queue = _ExchangeQueue()

    def reduce_later(names, l, parts, tag):
        def done(grads):
            for n, grad in zip(names, grads):
                update(n, l, grad)
        _reduce_scatter_later(queue, parts, c_idx, k_idx, tag, done)

    def ffn_dw(names, l):
        return lambda i, dw: reduce_later((names[i],), l, [dw.reshape(NCH, -1, D) if i == 2 else dw], "ffn")

    for l in range(depth - 1, -1, -1):
        wl = wts[l]
        s1, s2, s3, sp = saved[l]
        dh, dhb, dg = _ffn_bwd(dh, dhb, s3, ffn2_norm[l].reshape(1, D), wl['g2'], wl['u2'], wl['d2'], "ffn", queue,
                               ffn_dw(('ffn2_w_gate', 'ffn2_w_up', 'ffn2_w_down'), l))
        small_grads['ffn2_norm'][l] = dg.reshape(-1)

        dh, dhb, dwi, dwo, sm = _mix_bwd(dh, dhb, s2, sp, wl['wi'], wl['wo'], md, queue)
        for n in _MIX_SMALL:
            small_grads[n][l] = sm[n]
        dwi = _win_grad_to_ref(dwi, md)
        dwi = jnp.transpose(dwi.reshape(D, NCH, -1), (1, 0, 2))
        reduce_later(('w_in', 'w_out'), l, [dwi, dwo.reshape(NCH, -1, D)], "mix")

        dh, dhb, dg = _ffn_bwd(dh, dhb, s1, ffn1_norm[l].reshape(1, D), wl['g1'], wl['u1'], wl['d1'], "ffn", queue,
                               ffn_dw(('ffn1_w_gate', 'ffn1_w_up', 'ffn1_w_down'), l))
        small_grads['ffn1_norm'][l] = dg.reshape(-1)
    queue.drain("rs_chips")

    grad_x = dh[BLOCK:][None]

    small_shapes = [W[n].shape for n in _SMALL]
    parts = [jnp.stack(small_grads[n]) for n in _SMALL] + [dh[PAD:BLOCK], loss_part[0, :1]]
    packed = _pack_rows(parts)
    total = _sum_devices(_allgather_devices(packed, name="small_allgather"), name="small_sum")
    *g_small, g_meta, loss = _unpack_rows(total, small_shapes + [(N_META, D), (1,)])
    g_meta = lax.dynamic_slice(g_meta, (0, k_idx[0] * dcols), (N_META, dcols))

    sw = _pack_rows([W[n] for n in _SMALL])
    sd, smm, svv = _adamw_flat(sw, _pack_rows(g_small), _pack_rows([Mo[n] for n in _SMALL]),
                               _pack_rows([Vo[n] for n in _SMALL]), name="adamw_small")
    d_small, m_small, v_small = (_unpack_rows(t, small_shapes) for t in (sd, smm, svv))
    d_meta, m_meta, v_meta = _adamw_flat(meta_tokens, g_meta, m_meta_tokens, v_meta_tokens, name="adamw_meta")

    grads, deltas, new_m, new_v = {}, {}, {}, {}
    for n in _BIG:
        grads[n], deltas[n], new_m[n], new_v[n] = stacks[n]
    for i, n in enumerate(_SMALL):
        grads[n], deltas[n], new_m[n], new_v[n] = g_small[i], d_small[i], m_small[i], v_small[i]
    grads['meta_tokens'], deltas['meta_tokens'], new_m['meta_tokens'], new_v['meta_tokens'] = g_meta, d_meta, m_meta, v_meta
    return (loss.reshape(()), grad_x, *[grads[n] for n in _WEIGHTS], *[deltas[n] for n in _WEIGHTS],
            *[new_m[n] for n in _WEIGHTS], *[new_v[n] for n in _WEIGHTS])
```
